```python
import math
import jax, jax.numpy as jnp
from jax import lax
import numpy as np

D_MODEL = 1024
BATCH = 8
SEQ = 4096
DEPTH = 4

MLA_HEADS = 8
QK_NOPE_DIM = 64
QK_ROPE_DIM = 32
QK_HEAD_DIM = QK_NOPE_DIM + QK_ROPE_DIM
V_HEAD_DIM = 64
Q_LORA_RANK = 256
KV_LORA_RANK = 256
CONV_CHANNELS = D_MODEL // 2
SHORT_CONV_WIDTH = 3
MIX_IN_DIM = Q_LORA_RANK + KV_LORA_RANK + QK_ROPE_DIM + 3 * CONV_CHANNELS
MIX_OUT_DIM = MLA_HEADS * V_HEAD_DIM + CONV_CHANNELS
ROPE_THETA = 10000.0
Q_BLOCK = 128
SSM_WIDTH = D_MODEL
SSM_GROUP = 16
SSM_GROUPS = SSM_WIDTH // SSM_GROUP
SSM_STATE = 64
DT_MIN = 1e-3
DT_MAX = 1e-1
FFN_HIDDEN = 2816
FFN_CONV_WIDTH = 3
N_EVEN = (DEPTH + 1) // 2
N_ODD = DEPTH // 2
EPS = 1e-6

kernel_name = "hybrid_mla_shortconv_s5_convffn"


def rms_norm(x, g):
    x32 = x.astype(jnp.float32)
    y = x32 * lax.rsqrt(jnp.mean(x32 * x32, axis=-1, keepdims=True) + EPS)
    return (y * g.astype(jnp.float32)).astype(x.dtype)


def causal_depthwise_conv(x, w):
    k_width, channels = w.shape
    return lax.conv_general_dilated(
        x, w[:, None, :].astype(x.dtype), window_strides=(1,), padding=[(k_width - 1, 0)],
        dimension_numbers=("NWC", "WIO", "NWC"), feature_group_count=channels)


def rope_tables(seq):
    inv_freq = 1.0 / (ROPE_THETA ** (jnp.arange(0, QK_ROPE_DIM, 2, dtype=jnp.float32) / QK_ROPE_DIM))
    ang = jnp.arange(seq, dtype=jnp.float32)[:, None] * inv_freq[None, :]
    return jnp.cos(ang)[:, None, :], jnp.sin(ang)[:, None, :]


def apply_rope(x, cos, sin):
    x1, x2 = jnp.split(x.astype(jnp.float32), 2, axis=-1)
    return jnp.concatenate([x1 * cos - x2 * sin, x2 * cos + x1 * sin], axis=-1).astype(x.dtype)


def causal_block_attention(q, k, v):
    seq = q.shape[1]
    scale = q.shape[-1] ** -0.5
    outs = []
    for i in range(seq // Q_BLOCK):
        lo, hi = i * Q_BLOCK, (i + 1) * Q_BLOCK
        s = jnp.einsum("bqhd,bkhd->bhqk", q[:, lo:hi], k[:, :hi]).astype(jnp.float32) * scale
        causal = jnp.arange(hi)[None, :] <= jnp.arange(lo, hi)[:, None]
        s = jnp.where(causal, s, -jnp.inf)
        p = jax.nn.softmax(s, axis=-1).astype(v.dtype)
        outs.append(jnp.einsum("bhqk,bkhd->bqhd", p, v[:, :hi]))
    return jnp.concatenate(outs, axis=1)


def mla_shortconv_mixer(h, w_in, cq_norm, ckv_norm, w_uq, w_ukv, q_gain, k_gain, sconv_w, w_out, cos, sin):
    bsz, seq, _ = h.shape
    proj = h @ w_in
    splits = np.cumsum([Q_LORA_RANK, KV_LORA_RANK, QK_ROPE_DIM, CONV_CHANNELS, CONV_CHANNELS]).tolist()
    c_q, c_kv, k_rope, gate_b, gate_c, conv_in = jnp.split(proj, splits, axis=-1)
    q = (rms_norm(c_q, cq_norm) @ w_uq).reshape(bsz, seq, MLA_HEADS, QK_HEAD_DIM)
    kv = (rms_norm(c_kv, ckv_norm) @ w_ukv).reshape(bsz, seq, MLA_HEADS, QK_NOPE_DIM + V_HEAD_DIM)
    k_nope, v = kv[..., :QK_NOPE_DIM], kv[..., QK_NOPE_DIM:]
    k = jnp.concatenate(
        [k_nope, jnp.broadcast_to(k_rope[:, :, None, :], (bsz, seq, MLA_HEADS, QK_ROPE_DIM))], axis=-1)
    q = rms_norm(q, q_gain)
    k = rms_norm(k, k_gain)
    q = jnp.concatenate([q[..., :QK_NOPE_DIM], apply_rope(q[..., QK_NOPE_DIM:], cos, sin)], axis=-1)
    k = jnp.concatenate([k[..., :QK_NOPE_DIM], apply_rope(k[..., QK_NOPE_DIM:], cos, sin)], axis=-1)
    attn = causal_block_attention(q, k, v).reshape(bsz, seq, MLA_HEADS * V_HEAD_DIM)
    conv = gate_b * causal_depthwise_conv(gate_c * conv_in, sconv_w)
    return jnp.concatenate([attn, conv], axis=-1) @ w_out


def _ssm_combine(earlier, later):
    ar1, ai1, br1, bi1 = earlier
    ar2, ai2, br2, bi2 = later
    return (ar2 * ar1 - ai2 * ai1,
            ar2 * ai1 + ai2 * ar1,
            ar2 * br1 - ai2 * bi1 + br2,
            ar2 * bi1 + ai2 * br1 + bi2)


def s5_mixer(h, w_in, lambda_re, lambda_im, log_step, b_re, b_im, c_re, c_im, d_skip, w_glu):
    bsz, seq, _ = h.shape
    f32 = jnp.float32
    u = (h @ w_in).astype(f32)
    ug = u.reshape(bsz, seq, SSM_GROUPS, SSM_GROUP)
    lr, li = lambda_re.astype(f32), lambda_im.astype(f32)
    dt = jnp.exp(log_step.astype(f32))[:, None]
    mag = jnp.exp(lr * dt)
    ar, ai = mag * jnp.cos(li * dt), mag * jnp.sin(li * dt)
    nr, ni = ar - 1.0, ai
    den = lr * lr + li * li
    zr, zi = (nr * lr + ni * li) / den, (ni * lr - nr * li) / den
    br_, bi_ = b_re.astype(f32), b_im.astype(f32)
    bbar_r = zr[..., None] * br_ - zi[..., None] * bi_
    bbar_i = zr[..., None] * bi_ + zi[..., None] * br_
    bu_r = jnp.einsum("gpc,bsgc->bsgp", bbar_r, ug)
    bu_i = jnp.einsum("gpc,bsgc->bsgp", bbar_i, ug)
    a_r = jnp.broadcast_to(ar, (1, seq, SSM_GROUPS, SSM_STATE))
    a_i = jnp.broadcast_to(ai, (1, seq, SSM_GROUPS, SSM_STATE))
    _, _, st_r, st_i = lax.associative_scan(_ssm_combine, (a_r, a_i, bu_r, bu_i), axis=1)
    y = (jnp.einsum("gcp,bsgp->bsgc", c_re.astype(f32), st_r)
         - jnp.einsum("gcp,bsgp->bsgc", c_im.astype(f32), st_i)).reshape(bsz, seq, SSM_WIDTH)
    y = y + d_skip.astype(f32) * u
    g = jax.nn.gelu(y).astype(h.dtype)
    a, b = jnp.split(g @ w_glu, 2, axis=-1)
    return a * jax.nn.sigmoid(b)


def conv_ffn(h, w_up, conv_w, w_down):
    up = causal_depthwise_conv(h @ w_up, conv_w)
    gate, val = jnp.split(up, 2, axis=-1)
    return (jax.nn.silu(gate) * val) @ w_down


def _fwd_setup_inputs(seed: int = 0) -> dict:
    key = jax.random.key(seed)
    ks = jax.random.split(key, 32)
    f32 = jnp.float32

    def nrm(k, shape, scale):
        return jax.random.normal(k, shape, f32) * scale

    def gain(k, shape):
        return 1.0 + 0.02 * jax.random.normal(k, shape, f32)

    lam_im_base = jnp.pi * jnp.arange(SSM_STATE, dtype=f32)
    return {
        "x": nrm(ks[0], (BATCH, SEQ, D_MODEL), 1.0),
        "attn_norm": gain(ks[1], (N_EVEN, D_MODEL)),
        "mix_w_in": nrm(ks[2], (N_EVEN, D_MODEL, MIX_IN_DIM), D_MODEL ** -0.5),
        "cq_norm": gain(ks[3], (N_EVEN, Q_LORA_RANK)),
        "ckv_norm": gain(ks[4], (N_EVEN, KV_LORA_RANK)),
        "w_uq": nrm(ks[5], (N_EVEN, Q_LORA_RANK, MLA_HEADS * QK_HEAD_DIM), Q_LORA_RANK ** -0.5),
        "w_ukv": nrm(ks[6], (N_EVEN, KV_LORA_RANK, MLA_HEADS * (QK_NOPE_DIM + V_HEAD_DIM)), KV_LORA_RANK ** -0.5),
        "q_gain": gain(ks[7], (N_EVEN, QK_HEAD_DIM)),
        "k_gain": gain(ks[8], (N_EVEN, QK_HEAD_DIM)),
        "sconv_w": nrm(ks[9], (N_EVEN, SHORT_CONV_WIDTH, CONV_CHANNELS), SHORT_CONV_WIDTH ** -0.5),
        "mix_w_out": nrm(ks[10], (N_EVEN, MIX_OUT_DIM, D_MODEL), MIX_OUT_DIM ** -0.5),
        "ssm_norm": gain(ks[11], (N_ODD, D_MODEL)),
        "ssm_w_in": nrm(ks[12], (N_ODD, D_MODEL, SSM_WIDTH), D_MODEL ** -0.5),
        "lambda_re": -0.5 + 0.01 * jax.random.normal(ks[13], (N_ODD, SSM_GROUPS, SSM_STATE), f32),
        "lambda_im": lam_im_base + 0.01 * jax.random.normal(ks[14], (N_ODD, SSM_GROUPS, SSM_STATE), f32),
        "log_step": jax.random.uniform(ks[15], (N_ODD, SSM_GROUPS), f32,
                                       minval=math.log(DT_MIN), maxval=math.log(DT_MAX)),
        "b_re": nrm(ks[16], (N_ODD, SSM_GROUPS, SSM_STATE, SSM_GROUP), (2 * SSM_GROUP) ** -0.5),
        "b_im": nrm(ks[17], (N_ODD, SSM_GROUPS, SSM_STATE, SSM_GROUP), (2 * SSM_GROUP) ** -0.5),
        "c_re": nrm(ks[18], (N_ODD, SSM_GROUPS, SSM_GROUP, SSM_STATE), (2 * SSM_STATE) ** -0.5),
        "c_im": nrm(ks[19], (N_ODD, SSM_GROUPS, SSM_GROUP, SSM_STATE), (2 * SSM_STATE) ** -0.5),
        "d_skip": nrm(ks[20], (N_ODD, SSM_WIDTH), 1.0),
        "w_glu": nrm(ks[21], (N_ODD, SSM_WIDTH, 2 * D_MODEL), SSM_WIDTH ** -0.5),
        "ffn_norm": gain(ks[22], (DEPTH, D_MODEL)),
        "ffn_w_up": nrm(ks[23], (DEPTH, D_MODEL, 2 * FFN_HIDDEN), D_MODEL ** -0.5),
        "ffn_conv_w": nrm(ks[24], (DEPTH, FFN_CONV_WIDTH, 2 * FFN_HIDDEN), FFN_CONV_WIDTH ** -0.5),
        "ffn_w_down": nrm(ks[25], (DEPTH, FFN_HIDDEN, D_MODEL), FFN_HIDDEN ** -0.5),
    }


def _fwd_reference(x, attn_norm, mix_w_in, cq_norm, ckv_norm, w_uq, w_ukv, q_gain, k_gain, sconv_w, mix_w_out,
              ssm_norm, ssm_w_in, lambda_re, lambda_im, log_step, b_re, b_im, c_re, c_im, d_skip, w_glu,
              ffn_norm, ffn_w_up, ffn_conv_w, ffn_w_down):
    cos, sin = rope_tables(x.shape[1])
    for layer in range(DEPTH):
        i = layer // 2
        if layer % 2 == 0:
            x = x + mla_shortconv_mixer(rms_norm(x, attn_norm[i]), mix_w_in[i], cq_norm[i], ckv_norm[i],
                                        w_uq[i], w_ukv[i], q_gain[i], k_gain[i], sconv_w[i], mix_w_out[i],
                                        cos, sin)
        else:
            x = x + s5_mixer(rms_norm(x, ssm_norm[i]), ssm_w_in[i], lambda_re[i], lambda_im[i], log_step[i],
                             b_re[i], b_im[i], c_re[i], c_im[i], d_skip[i], w_glu[i]).astype(x.dtype)
        x = x + conv_ffn(rms_norm(x, ffn_norm[layer]), ffn_w_up[layer], ffn_conv_w[layer], ffn_w_down[layer])
    return x


import jax as _jax
import jax.numpy as _jnp

TWIN_FORMAT = 'train_step'
FWD_PARAMS = ['x', 'attn_norm', 'mix_w_in', 'cq_norm', 'ckv_norm', 'w_uq', 'w_ukv', 'q_gain', 'k_gain', 'sconv_w', 'mix_w_out', 'ssm_norm', 'ssm_w_in', 'lambda_re', 'lambda_im', 'log_step', 'b_re', 'b_im', 'c_re', 'c_im', 'd_skip', 'w_glu', 'ffn_norm', 'ffn_w_up', 'ffn_conv_w', 'ffn_w_down']
TWIN_WEIGHTS = ['attn_norm', 'mix_w_in', 'cq_norm', 'ckv_norm', 'w_uq', 'w_ukv', 'q_gain', 'k_gain', 'sconv_w', 'mix_w_out', 'ssm_norm', 'ssm_w_in', 'lambda_re', 'lambda_im', 'log_step', 'b_re', 'b_im', 'c_re', 'c_im', 'd_skip', 'w_glu', 'ffn_norm', 'ffn_w_up', 'ffn_conv_w', 'ffn_w_down']
TWIN_DIFF_INPUT = 'x'
TWIN_INPUTS = ['x', 'attn_norm', 'mix_w_in', 'cq_norm', 'ckv_norm', 'w_uq', 'w_ukv', 'q_gain', 'k_gain', 'sconv_w', 'mix_w_out', 'ssm_norm', 'ssm_w_in', 'lambda_re', 'lambda_im', 'log_step', 'b_re', 'b_im', 'c_re', 'c_im', 'd_skip', 'w_glu', 'ffn_norm', 'ffn_w_up', 'ffn_conv_w', 'ffn_w_down', 'loss_target', 'm_attn_norm', 'm_mix_w_in', 'm_cq_norm', 'm_ckv_norm', 'm_w_uq', 'm_w_ukv', 'm_q_gain', 'm_k_gain', 'm_sconv_w', 'm_mix_w_out', 'm_ssm_norm', 'm_ssm_w_in', 'm_lambda_re', 'm_lambda_im', 'm_log_step', 'm_b_re', 'm_b_im', 'm_c_re', 'm_c_im', 'm_d_skip', 'm_w_glu', 'm_ffn_norm', 'm_ffn_w_up', 'm_ffn_conv_w', 'm_ffn_w_down', 'v_attn_norm', 'v_mix_w_in', 'v_cq_norm', 'v_ckv_norm', 'v_w_uq', 'v_w_ukv', 'v_q_gain', 'v_k_gain', 'v_sconv_w', 'v_mix_w_out', 'v_ssm_norm', 'v_ssm_w_in', 'v_lambda_re', 'v_lambda_im', 'v_log_step', 'v_b_re', 'v_b_im', 'v_c_re', 'v_c_im', 'v_d_skip', 'v_w_glu', 'v_ffn_norm', 'v_ffn_w_up', 'v_ffn_conv_w', 'v_ffn_w_down']
TWIN_OUTPUTS = ['loss', 'grad_x', 'grad_attn_norm', 'grad_mix_w_in', 'grad_cq_norm', 'grad_ckv_norm', 'grad_w_uq', 'grad_w_ukv', 'grad_q_gain', 'grad_k_gain', 'grad_sconv_w', 'grad_mix_w_out', 'grad_ssm_norm', 'grad_ssm_w_in', 'grad_lambda_re', 'grad_lambda_im', 'grad_log_step', 'grad_b_re', 'grad_b_im', 'grad_c_re', 'grad_c_im', 'grad_d_skip', 'grad_w_glu', 'grad_ffn_norm', 'grad_ffn_w_up', 'grad_ffn_conv_w', 'grad_ffn_w_down', 'delta_attn_norm', 'delta_mix_w_in', 'delta_cq_norm', 'delta_ckv_norm', 'delta_w_uq', 'delta_w_ukv', 'delta_q_gain', 'delta_k_gain', 'delta_sconv_w', 'delta_mix_w_out', 'delta_ssm_norm', 'delta_ssm_w_in', 'delta_lambda_re', 'delta_lambda_im', 'delta_log_step', 'delta_b_re', 'delta_b_im', 'delta_c_re', 'delta_c_im', 'delta_d_skip', 'delta_w_glu', 'delta_ffn_norm', 'delta_ffn_w_up', 'delta_ffn_conv_w', 'delta_ffn_w_down', 'new_m_attn_norm', 'new_m_mix_w_in', 'new_m_cq_norm', 'new_m_ckv_norm', 'new_m_w_uq', 'new_m_w_ukv', 'new_m_q_gain', 'new_m_k_gain', 'new_m_sconv_w', 'new_m_mix_w_out', 'new_m_ssm_norm', 'new_m_ssm_w_in', 'new_m_lambda_re', 'new_m_lambda_im', 'new_m_log_step', 'new_m_b_re', 'new_m_b_im', 'new_m_c_re', 'new_m_c_im', 'new_m_d_skip', 'new_m_w_glu', 'new_m_ffn_norm', 'new_m_ffn_w_up', 'new_m_ffn_conv_w', 'new_m_ffn_w_down', 'new_v_attn_norm', 'new_v_mix_w_in', 'new_v_cq_norm', 'new_v_ckv_norm', 'new_v_w_uq', 'new_v_w_ukv', 'new_v_q_gain', 'new_v_k_gain', 'new_v_sconv_w', 'new_v_mix_w_out', 'new_v_ssm_norm', 'new_v_ssm_w_in', 'new_v_lambda_re', 'new_v_lambda_im', 'new_v_log_step', 'new_v_b_re', 'new_v_b_im', 'new_v_c_re', 'new_v_c_im', 'new_v_d_skip', 'new_v_w_glu', 'new_v_ffn_norm', 'new_v_ffn_w_up', 'new_v_ffn_conv_w', 'new_v_ffn_w_down']
TWIN_LEAF_KINDS = {'loss': 'loss', 'grad_x': 'grad_x', 'grad_attn_norm': 'grad_w', 'grad_mix_w_in': 'grad_w', 'grad_cq_norm': 'grad_w', 'grad_ckv_norm': 'grad_w', 'grad_w_uq': 'grad_w', 'grad_w_ukv': 'grad_w', 'grad_q_gain': 'grad_w', 'grad_k_gain': 'grad_w', 'grad_sconv_w': 'grad_w', 'grad_mix_w_out': 'grad_w', 'grad_ssm_norm': 'grad_w', 'grad_ssm_w_in': 'grad_w', 'grad_lambda_re': 'grad_w', 'grad_lambda_im': 'grad_w', 'grad_log_step': 'grad_w', 'grad_b_re': 'grad_w', 'grad_b_im': 'grad_w', 'grad_c_re': 'grad_w', 'grad_c_im': 'grad_w', 'grad_d_skip': 'grad_w', 'grad_w_glu': 'grad_w', 'grad_ffn_norm': 'grad_w', 'grad_ffn_w_up': 'grad_w', 'grad_ffn_conv_w': 'grad_w', 'grad_ffn_w_down': 'grad_w', 'delta_attn_norm': 'delta_w', 'delta_mix_w_in': 'delta_w', 'delta_cq_norm': 'delta_w', 'delta_ckv_norm': 'delta_w', 'delta_w_uq': 'delta_w', 'delta_w_ukv': 'delta_w', 'delta_q_gain': 'delta_w', 'delta_k_gain': 'delta_w', 'delta_sconv_w': 'delta_w', 'delta_mix_w_out': 'delta_w', 'delta_ssm_norm': 'delta_w', 'delta_ssm_w_in': 'delta_w', 'delta_lambda_re': 'delta_w', 'delta_lambda_im': 'delta_w', 'delta_log_step': 'delta_w', 'delta_b_re': 'delta_w', 'delta_b_im': 'delta_w', 'delta_c_re': 'delta_w', 'delta_c_im': 'delta_w', 'delta_d_skip': 'delta_w', 'delta_w_glu': 'delta_w', 'delta_ffn_norm': 'delta_w', 'delta_ffn_w_up': 'delta_w', 'delta_ffn_conv_w': 'delta_w', 'delta_ffn_w_down': 'delta_w', 'new_m_attn_norm': 'new_m', 'new_m_mix_w_in': 'new_m', 'new_m_cq_norm': 'new_m', 'new_m_ckv_norm': 'new_m', 'new_m_w_uq': 'new_m', 'new_m_w_ukv': 'new_m', 'new_m_q_gain': 'new_m', 'new_m_k_gain': 'new_m', 'new_m_sconv_w': 'new_m', 'new_m_mix_w_out': 'new_m', 'new_m_ssm_norm': 'new_m', 'new_m_ssm_w_in': 'new_m', 'new_m_lambda_re': 'new_m', 'new_m_lambda_im': 'new_m', 'new_m_log_step': 'new_m', 'new_m_b_re': 'new_m', 'new_m_b_im': 'new_m', 'new_m_c_re': 'new_m', 'new_m_c_im': 'new_m', 'new_m_d_skip': 'new_m', 'new_m_w_glu': 'new_m', 'new_m_ffn_norm': 'new_m', 'new_m_ffn_w_up': 'new_m', 'new_m_ffn_conv_w': 'new_m', 'new_m_ffn_w_down': 'new_m', 'new_v_attn_norm': 'new_v', 'new_v_mix_w_in': 'new_v', 'new_v_cq_norm': 'new_v', 'new_v_ckv_norm': 'new_v', 'new_v_w_uq': 'new_v', 'new_v_w_ukv': 'new_v', 'new_v_q_gain': 'new_v', 'new_v_k_gain': 'new_v', 'new_v_sconv_w': 'new_v', 'new_v_mix_w_out': 'new_v', 'new_v_ssm_norm': 'new_v', 'new_v_ssm_w_in': 'new_v', 'new_v_lambda_re': 'new_v', 'new_v_lambda_im': 'new_v', 'new_v_log_step': 'new_v', 'new_v_b_re': 'new_v', 'new_v_b_im': 'new_v', 'new_v_c_re': 'new_v', 'new_v_c_im': 'new_v', 'new_v_d_skip': 'new_v', 'new_v_w_glu': 'new_v', 'new_v_ffn_norm': 'new_v', 'new_v_ffn_w_up': 'new_v', 'new_v_ffn_conv_w': 'new_v', 'new_v_ffn_w_down': 'new_v'}


def _forward(args):
    return _fwd_reference(*[args[k] for k in FWD_PARAMS])


def _output_shape():
    out = _jax.eval_shape(lambda: _forward(_fwd_setup_inputs(0)))
    return out.shape, out.dtype

N_MICROBATCH = 1
ADAM_LR = 0.001
ADAM_B1 = 0.9
ADAM_B2 = 0.999
ADAM_EPS = 1e-08
ADAM_WD = 0.01
ADAM_STEP = 10
PER_EXAMPLE_BATCH_AXIS = {'x': 0, 'loss_target': 0}
SHARED_INPUTS = []
_WEIGHT_DTYPES = {'attn_norm': _jnp.float32, 'mix_w_in': _jnp.float32, 'cq_norm': _jnp.float32, 'ckv_norm': _jnp.float32, 'w_uq': _jnp.float32, 'w_ukv': _jnp.float32, 'q_gain': _jnp.float32, 'k_gain': _jnp.float32, 'sconv_w': _jnp.float32, 'mix_w_out': _jnp.float32, 'ssm_norm': _jnp.float32, 'ssm_w_in': _jnp.float32, 'lambda_re': _jnp.float32, 'lambda_im': _jnp.float32, 'log_step': _jnp.float32, 'b_re': _jnp.float32, 'b_im': _jnp.float32, 'c_re': _jnp.float32, 'c_im': _jnp.float32, 'd_skip': _jnp.float32, 'w_glu': _jnp.float32, 'ffn_norm': _jnp.float32, 'ffn_w_up': _jnp.float32, 'ffn_conv_w': _jnp.float32, 'ffn_w_down': _jnp.float32}
MOMENT_SCALE = {'attn_norm': 4.876932e+01, 'mix_w_in': 1.259055e+00, 'cq_norm': 2.579862e-01, 'ckv_norm': 9.296200e-01, 'w_uq': 1.510051e-01, 'w_ukv': 3.709374e-01, 'q_gain': 8.807045e-01, 'k_gain': 8.804712e-01, 'sconv_w': 1.805318e+01, 'mix_w_out': 1.054397e+00, 'ssm_norm': 4.387883e+00, 'ssm_w_in': 5.447982e-01, 'lambda_re': 3.067966e-02, 'lambda_im': 3.269508e-02, 'log_step': 8.771110e+00, 'b_re': 2.057673e-02, 'b_im': 1.863127e-02, 'c_re': 3.605854e-02, 'c_im': 3.758976e-02, 'd_skip': 5.155706e+00, 'w_glu': 1.454226e+00, 'ffn_norm': 2.577386e+01, 'ffn_w_up': 4.482581e-01, 'ffn_conv_w': 3.396477e+00, 'ffn_w_down': 5.709523e-01}


def _to_microbatches(a, axis):
    t = _jnp.moveaxis(a, axis, 0)
    t = t.reshape((N_MICROBATCH, t.shape[0] // N_MICROBATCH) + t.shape[1:])
    return _jnp.moveaxis(t, 1, axis + 1)


def setup_inputs(seed: int = 0) -> dict:
    inp = _fwd_setup_inputs(seed)
    key = _jax.random.fold_in(_jax.random.key(seed), 7919)
    shape, _ = _output_shape()
    out = dict(inp)
    out["loss_target"] = _jax.random.normal(_jax.random.fold_in(key, 0), shape, _jnp.float32)
    for i, name in enumerate(TWIN_WEIGHTS):
        w = inp[name].astype(_jnp.float32)
        if MOMENT_SCALE is None:
            s = _jnp.sqrt(_jnp.mean(_jnp.square(w)) + 1e-30)
        else:
            s = MOMENT_SCALE[name]
        km, kv = _jax.random.split(_jax.random.fold_in(key, i + 1))
        out[name] = w
        out["m_" + name] = s * _jax.random.normal(km, w.shape, _jnp.float32)
        out["v_" + name] = (s * s) * _jax.random.uniform(kv, w.shape, _jnp.float32, 0.5, 1.5)
    if N_MICROBATCH > 1:
        for name, axis in PER_EXAMPLE_BATCH_AXIS.items():
            out[name] = _to_microbatches(out[name], axis)
    return {'x': out['x'], 'attn_norm': out['attn_norm'], 'mix_w_in': out['mix_w_in'], 'cq_norm': out['cq_norm'], 'ckv_norm': out['ckv_norm'], 'w_uq': out['w_uq'], 'w_ukv': out['w_ukv'], 'q_gain': out['q_gain'], 'k_gain': out['k_gain'], 'sconv_w': out['sconv_w'], 'mix_w_out': out['mix_w_out'], 'ssm_norm': out['ssm_norm'], 'ssm_w_in': out['ssm_w_in'], 'lambda_re': out['lambda_re'], 'lambda_im': out['lambda_im'], 'log_step': out['log_step'], 'b_re': out['b_re'], 'b_im': out['b_im'], 'c_re': out['c_re'], 'c_im': out['c_im'], 'd_skip': out['d_skip'], 'w_glu': out['w_glu'], 'ffn_norm': out['ffn_norm'], 'ffn_w_up': out['ffn_w_up'], 'ffn_conv_w': out['ffn_conv_w'], 'ffn_w_down': out['ffn_w_down'], 'loss_target': out['loss_target'], 'm_attn_norm': out['m_attn_norm'], 'm_mix_w_in': out['m_mix_w_in'], 'm_cq_norm': out['m_cq_norm'], 'm_ckv_norm': out['m_ckv_norm'], 'm_w_uq': out['m_w_uq'], 'm_w_ukv': out['m_w_ukv'], 'm_q_gain': out['m_q_gain'], 'm_k_gain': out['m_k_gain'], 'm_sconv_w': out['m_sconv_w'], 'm_mix_w_out': out['m_mix_w_out'], 'm_ssm_norm': out['m_ssm_norm'], 'm_ssm_w_in': out['m_ssm_w_in'], 'm_lambda_re': out['m_lambda_re'], 'm_lambda_im': out['m_lambda_im'], 'm_log_step': out['m_log_step'], 'm_b_re': out['m_b_re'], 'm_b_im': out['m_b_im'], 'm_c_re': out['m_c_re'], 'm_c_im': out['m_c_im'], 'm_d_skip': out['m_d_skip'], 'm_w_glu': out['m_w_glu'], 'm_ffn_norm': out['m_ffn_norm'], 'm_ffn_w_up': out['m_ffn_w_up'], 'm_ffn_conv_w': out['m_ffn_conv_w'], 'm_ffn_w_down': out['m_ffn_w_down'], 'v_attn_norm': out['v_attn_norm'], 'v_mix_w_in': out['v_mix_w_in'], 'v_cq_norm': out['v_cq_norm'], 'v_ckv_norm': out['v_ckv_norm'], 'v_w_uq': out['v_w_uq'], 'v_w_ukv': out['v_w_ukv'], 'v_q_gain': out['v_q_gain'], 'v_k_gain': out['v_k_gain'], 'v_sconv_w': out['v_sconv_w'], 'v_mix_w_out': out['v_mix_w_out'], 'v_ssm_norm': out['v_ssm_norm'], 'v_ssm_w_in': out['v_ssm_w_in'], 'v_lambda_re': out['v_lambda_re'], 'v_lambda_im': out['v_lambda_im'], 'v_log_step': out['v_log_step'], 'v_b_re': out['v_b_re'], 'v_b_im': out['v_b_im'], 'v_c_re': out['v_c_re'], 'v_c_im': out['v_c_im'], 'v_d_skip': out['v_d_skip'], 'v_w_glu': out['v_w_glu'], 'v_ffn_norm': out['v_ffn_norm'], 'v_ffn_w_up': out['v_ffn_w_up'], 'v_ffn_conv_w': out['v_ffn_conv_w'], 'v_ffn_w_down': out['v_ffn_w_down']}


def _loss(weights, diff, rest, loss_target):
    with _jax.named_scope("forward"):
        args = {**rest, TWIN_DIFF_INPUT: diff, **{k: w.astype(_WEIGHT_DTYPES[k]) for k, w in weights.items()}}
        y = _forward(args)
    with _jax.named_scope("loss_head"):
        err = _jnp.square(y.astype(_jnp.float32) - loss_target)
        return 0.5 * _jnp.sum(_jnp.mean(err, axis=-1)) if err.ndim else 0.5 * err


def _adamw(w, g, m, v):
    m = ADAM_B1 * m + (1.0 - ADAM_B1) * g
    v = ADAM_B2 * v + (1.0 - ADAM_B2) * _jnp.square(g)
    m_hat = m / (1.0 - ADAM_B1 ** ADAM_STEP)
    v_hat = v / (1.0 - ADAM_B2 ** ADAM_STEP)
    delta = -ADAM_LR * (m_hat / (_jnp.sqrt(v_hat) + ADAM_EPS) + ADAM_WD * w)
    return delta, m, v


def reference(x, attn_norm, mix_w_in, cq_norm, ckv_norm, w_uq, w_ukv, q_gain, k_gain, sconv_w, mix_w_out, ssm_norm, ssm_w_in, lambda_re, lambda_im, log_step, b_re, b_im, c_re, c_im, d_skip, w_glu, ffn_norm, ffn_w_up, ffn_conv_w, ffn_w_down, loss_target, m_attn_norm, m_mix_w_in, m_cq_norm, m_ckv_norm, m_w_uq, m_w_ukv, m_q_gain, m_k_gain, m_sconv_w, m_mix_w_out, m_ssm_norm, m_ssm_w_in, m_lambda_re, m_lambda_im, m_log_step, m_b_re, m_b_im, m_c_re, m_c_im, m_d_skip, m_w_glu, m_ffn_norm, m_ffn_w_up, m_ffn_conv_w, m_ffn_w_down, v_attn_norm, v_mix_w_in, v_cq_norm, v_ckv_norm, v_w_uq, v_w_ukv, v_q_gain, v_k_gain, v_sconv_w, v_mix_w_out, v_ssm_norm, v_ssm_w_in, v_lambda_re, v_lambda_im, v_log_step, v_b_re, v_b_im, v_c_re, v_c_im, v_d_skip, v_w_glu, v_ffn_norm, v_ffn_w_up, v_ffn_conv_w, v_ffn_w_down):
    given = dict(x=x, attn_norm=attn_norm, mix_w_in=mix_w_in, cq_norm=cq_norm, ckv_norm=ckv_norm, w_uq=w_uq, w_ukv=w_ukv, q_gain=q_gain, k_gain=k_gain, sconv_w=sconv_w, mix_w_out=mix_w_out, ssm_norm=ssm_norm, ssm_w_in=ssm_w_in, lambda_re=lambda_re, lambda_im=lambda_im, log_step=log_step, b_re=b_re, b_im=b_im, c_re=c_re, c_im=c_im, d_skip=d_skip, w_glu=w_glu, ffn_norm=ffn_norm, ffn_w_up=ffn_w_up, ffn_conv_w=ffn_conv_w, ffn_w_down=ffn_w_down, loss_target=loss_target, m_attn_norm=m_attn_norm, m_mix_w_in=m_mix_w_in, m_cq_norm=m_cq_norm, m_ckv_norm=m_ckv_norm, m_w_uq=m_w_uq, m_w_ukv=m_w_ukv, m_q_gain=m_q_gain, m_k_gain=m_k_gain, m_sconv_w=m_sconv_w, m_mix_w_out=m_mix_w_out, m_ssm_norm=m_ssm_norm, m_ssm_w_in=m_ssm_w_in, m_lambda_re=m_lambda_re, m_lambda_im=m_lambda_im, m_log_step=m_log_step, m_b_re=m_b_re, m_b_im=m_b_im, m_c_re=m_c_re, m_c_im=m_c_im, m_d_skip=m_d_skip, m_w_glu=m_w_glu, m_ffn_norm=m_ffn_norm, m_ffn_w_up=m_ffn_w_up, m_ffn_conv_w=m_ffn_conv_w, m_ffn_w_down=m_ffn_w_down, v_attn_norm=v_attn_norm, v_mix_w_in=v_mix_w_in, v_cq_norm=v_cq_norm, v_ckv_norm=v_ckv_norm, v_w_uq=v_w_uq, v_w_ukv=v_w_ukv, v_q_gain=v_q_gain, v_k_gain=v_k_gain, v_sconv_w=v_sconv_w, v_mix_w_out=v_mix_w_out, v_ssm_norm=v_ssm_norm, v_ssm_w_in=v_ssm_w_in, v_lambda_re=v_lambda_re, v_lambda_im=v_lambda_im, v_log_step=v_log_step, v_b_re=v_b_re, v_b_im=v_b_im, v_c_re=v_c_re, v_c_im=v_c_im, v_d_skip=v_d_skip, v_w_glu=v_w_glu, v_ffn_norm=v_ffn_norm, v_ffn_w_up=v_ffn_w_up, v_ffn_conv_w=v_ffn_conv_w, v_ffn_w_down=v_ffn_w_down)
    weights = {n: given[n] for n in TWIN_WEIGHTS}
    shared = {n: given[n] for n in SHARED_INPUTS}
    per_example = {n: given[n] for n in ['x']}
    grad_fn = _jax.value_and_grad(_loss, argnums=(0, 1))

    def one_microbatch(ex, loss_target):
        ex = dict(ex)
        diff = ex.pop(TWIN_DIFF_INPUT)
        return grad_fn(weights, diff, {**shared, **ex}, loss_target)

    if N_MICROBATCH == 1:
        loss, (grad_w, grad_x) = one_microbatch(per_example, given["loss_target"])
    else:
        def body(carry, xs):
            loss_sum, grad_sum = carry
            l_k, (gw_k, gx_k) = one_microbatch(xs[0], xs[1])
            with _jax.named_scope("update"):
                return (loss_sum + l_k, _jax.tree.map(_jnp.add, grad_sum, gw_k)), gx_k

        init = (_jnp.zeros((), _jnp.float32), _jax.tree.map(_jnp.zeros_like, weights))
        (loss, grad_w), grad_x = _jax.lax.scan(body, init, (per_example, given["loss_target"]))
    with _jax.named_scope("update"):
        delta_w, new_m, new_v = {}, {}, {}
        for n in TWIN_WEIGHTS:
            delta_w[n], new_m[n], new_v[n] = _adamw(weights[n], grad_w[n], given["m_" + n], given["v_" + n])
    return (loss, grad_x, *[grad_w[n] for n in TWIN_WEIGHTS], *[delta_w[n] for n in TWIN_WEIGHTS],
            *[new_m[n] for n in TWIN_WEIGHTS], *[new_v[n] for n in TWIN_WEIGHTS])
```

```python
import math

import numpy as np
import jax
import jax.numpy as jnp
from jax import lax
from jax.experimental import pallas as pl
from jax.experimental.pallas import tpu as pltpu

f32, bf16 = jnp.float32, jnp.bfloat16

N_DEV = 8
D = 1024
HEADS = 8
NOPE, ROPE, QK = 64, 32, 96
HEAD_PAD = 128
LORA = 256
CONV_CH = 512
MIX_IN_PAD = 2176
FFN_H = 2816
GROUPS, GROUP, STATE = 64, 16, 64
EPS = 1e-6
ROPE_THETA = 10000.0
ADAM_LR, ADAM_B1, ADAM_B2, ADAM_EPS, ADAM_WD, ADAM_STEP = 0.001, 0.9, 0.999, 1e-08, 0.01, 10
LANES = 128
VMEM_LIMIT = 56 << 20
NEG = -1e30

BIG = (
    ("mix_w_in", (2, 1024, 260), 2), ("w_uq", (2, 256, 96), 2), ("w_ukv", (2, 256, 128), 2),
    ("mix_w_out", (2, 128, 1024), 1), ("ssm_w_in", (2, 128, 1024), 1), ("w_glu", (2, 1024, 256), 2),
    ("ffn_w_up", (4, 1024, 704), 2), ("ffn_w_down", (4, 352, 1024), 1))
REPL = (("attn_norm", (2, 1024)), ("cq_norm", (2, 256)), ("ckv_norm", (2, 256)), ("q_gain", (2, 96)),
        ("k_gain", (2, 96)), ("lambda_re", (2, 64, 64)), ("lambda_im", (2, 64, 64)), ("log_step", (2, 64)),
        ("b_re", (2, 64, 64, 16)), ("b_im", (2, 64, 64, 16)), ("c_re", (2, 64, 16, 64)), ("c_im", (2, 64, 16, 64)),
        ("ffn_norm", (4, 1024)))
SMALL = (("sconv_w", (2, 3, 64), 2), ("ssm_norm", (2, 128), 1), ("d_skip", (2, 128), 1), ("ffn_conv_w", (4, 3, 704), 2))
WEIGHTS = ['attn_norm', 'mix_w_in', 'cq_norm', 'ckv_norm', 'w_uq', 'w_ukv', 'q_gain', 'k_gain', 'sconv_w', 'mix_w_out',
           'ssm_norm', 'ssm_w_in', 'lambda_re', 'lambda_im', 'log_step', 'b_re', 'b_im', 'c_re', 'c_im', 'd_skip',
           'w_glu', 'ffn_norm', 'ffn_w_up', 'ffn_conv_w', 'ffn_w_down']
BIG_ROWS = 5888
SMALL_FWD_ROWS = 80
SMALL_ROWS = 640


def _cparams(sem=None, **kw):
    return pltpu.CompilerParams(dimension_semantics=sem, vmem_limit_bytes=VMEM_LIMIT, **kw)


def _tile(n, target):
    best = 0
    for t in range(LANES, min(n, target) + 1, LANES):
        if n % t == 0:
            best = t
    return best if best else n


def _mm(a, b, *, ta=False, tb=False, out_dtype=f32, add=None, name, tm=1408, tn=1536, tk=512):
    m, k = (a.shape[1], a.shape[0]) if ta else a.shape
    n = b.shape[0] if tb else b.shape[1]
    assert (b.shape[1] if tb else b.shape[0]) == k
    tm, tk = _tile(m, tm), _tile(k, tk)
    tn_ = _tile(n, tn)
    tn = n if (tn_ < 256 and n <= 2304) else tn_
    nk = k // tk
    dn = (((0 if ta else 1,), (1 if tb else 0,)), ((), ()))

    def body(*refs):
        if add is None:
            a_ref, b_ref, o_ref, acc = refs
        else:
            a_ref, b_ref, add_ref, o_ref, acc = refs
        kk = pl.program_id(2)

        @pl.when(kk == 0)
        def _():
            acc[...] = jnp.zeros_like(acc)

        acc[...] += lax.dot_general(a_ref[...].astype(bf16), b_ref[...].astype(bf16), dn, preferred_element_type=f32)

        @pl.when(kk == nk - 1)
        def _():
            r = acc[...]
            if add is not None:
                r = r + add_ref[...].astype(f32)
            o_ref[...] = r.astype(out_dtype)

    a_spec = pl.BlockSpec((tk, tm), lambda i, j, kk: (kk, i)) if ta else pl.BlockSpec((tm, tk), lambda i, j, kk: (i, kk))
    b_spec = pl.BlockSpec((tn, tk), lambda i, j, kk: (j, kk)) if tb else pl.BlockSpec((tk, tn), lambda i, j, kk: (kk, j))
    in_specs, args = [a_spec, b_spec], [a, b]
    if add is not None:
        in_specs.append(pl.BlockSpec((tm, tn), lambda i, j, kk: (i, j)))
        args.append(add)
    return pl.pallas_call(
        body, name=name, grid=(m // tm, n // tn, nk), in_specs=in_specs,
        out_specs=pl.BlockSpec((tm, tn), lambda i, j, kk: (i, j)),
        out_shape=jax.ShapeDtypeStruct((m, n), out_dtype),
        scratch_shapes=[pltpu.VMEM((tm, tn), f32)],
        compiler_params=_cparams(("parallel", "parallel", "arbitrary")))(*args)


def _bd_nn(a, w, *, out_dtype=f32, name, ts=512):
    s = a.shape[0]
    nb, ka, no = w.shape
    ts = min(ts, s)

    def body(a_ref, w_ref, o_ref):
        o_ref[...] = jnp.dot(a_ref[...].astype(bf16), w_ref[0].astype(bf16), preferred_element_type=f32).astype(out_dtype)

    return pl.pallas_call(
        body, name=name, grid=(nb, s // ts),
        in_specs=[pl.BlockSpec((ts, ka), lambda b, i: (i, b)), pl.BlockSpec((1, ka, no), lambda b, i: (b, 0, 0))],
        out_specs=pl.BlockSpec((ts, no), lambda b, i: (i, b)),
        out_shape=jax.ShapeDtypeStruct((s, nb * no), out_dtype),
        compiler_params=_cparams(("parallel", "parallel")))(a, w)


def _bd_tn(a, g, ka, no, *, name, ts=512):
    s = a.shape[0]
    nb = a.shape[1] // ka
    ts = min(ts, s)

    def body(a_ref, g_ref, o_ref):
        @pl.when(pl.program_id(1) == 0)
        def _():
            o_ref[...] = jnp.zeros_like(o_ref)

        o_ref[0] += lax.dot_general(a_ref[...].astype(bf16), g_ref[...].astype(bf16), (((0,), (0,)), ((), ())),
                                    preferred_element_type=f32)

    return pl.pallas_call(
        body, name=name, grid=(nb, s // ts),
        in_specs=[pl.BlockSpec((ts, ka), lambda b, i: (i, b)), pl.BlockSpec((ts, no), lambda b, i: (i, b))],
        out_specs=pl.BlockSpec((1, ka, no), lambda b, i: (b, 0, 0)),
        out_shape=jax.ShapeDtypeStruct((nb, ka, no), f32),
        compiler_params=_cparams(("parallel", "arbitrary")))(a, g)


def _rms_fwd(x, g, *, col=0, name, ts=512):
    s, d = x.shape[0], g.shape[1]
    ts = min(ts, s)

    def body(x_ref, g_ref, o_ref):
        xv = x_ref[...].astype(f32)
        r = lax.rsqrt(jnp.mean(xv * xv, axis=-1, keepdims=True) + EPS)
        o_ref[...] = (xv * r * g_ref[...]).astype(bf16)

    return pl.pallas_call(
        body, name=name, grid=(s // ts,),
        in_specs=[pl.BlockSpec((ts, d), lambda i: (i, col)), pl.BlockSpec((1, d), lambda i: (0, 0))],
        out_specs=pl.BlockSpec((ts, d), lambda i: (i, 0)),
        out_shape=jax.ShapeDtypeStruct((s, d), bf16),
        compiler_params=_cparams(("parallel",)))(x, g)


def _rms_bwd(dy, x, g, *, col=0, res=None, out_dtype=f32, name, ts=512):
    s, d = dy.shape
    ts = min(ts, s)

    def body(*refs):
        if res is None:
            dy_ref, x_ref, g_ref, dx_ref, dg_ref = refs
        else:
            dy_ref, x_ref, g_ref, res_ref, dx_ref, dg_ref = refs

        @pl.when(pl.program_id(0) == 0)
        def _():
            dg_ref[...] = jnp.zeros_like(dg_ref)

        xv, dyv = x_ref[...].astype(f32), dy_ref[...].astype(f32)
        r = lax.rsqrt(jnp.mean(xv * xv, axis=-1, keepdims=True) + EPS)
        dyg = dyv * g_ref[...]
        dx = r * dyg - xv * (r * r * r) * jnp.mean(xv * dyg, axis=-1, keepdims=True)
        if res is not None:
            dx = dx + res_ref[...]
        dx_ref[...] = dx.astype(out_dtype)
        dg_ref[...] += jnp.sum(dyv * xv * r, axis=0, keepdims=True)

    in_specs = [pl.BlockSpec((ts, d), lambda i: (i, 0)), pl.BlockSpec((ts, d), lambda i: (i, col)),
                pl.BlockSpec((1, d), lambda i: (0, 0))]
    args = [dy, x, g]
    if res is not None:
        in_specs.append(pl.BlockSpec((ts, d), lambda i: (i, 0)))
        args.append(res)
    return pl.pallas_call(
        body, name=name, grid=(s // ts,), in_specs=in_specs,
        out_specs=[pl.BlockSpec((ts, d), lambda i: (i, 0)), pl.BlockSpec((1, d), lambda i: (0, 0))],
        out_shape=[jax.ShapeDtypeStruct((s, d), out_dtype), jax.ShapeDtypeStruct((1, d), f32)],
        compiler_params=_cparams(("arbitrary",)))(*args)


def _swap_halves(z):
    lane = lax.broadcasted_iota(jnp.int32, z.shape, 1)
    return jnp.where(lane < NOPE + ROPE // 2, pltpu.roll(z, LANES - ROPE // 2, axis=1), pltpu.roll(z, ROPE // 2, axis=1))


def _rope_tables(s):
    inv_freq = 1.0 / (ROPE_THETA ** (jnp.arange(0, ROPE, 2, dtype=f32) / ROPE))
    ang = jnp.arange(s, dtype=f32)[:, None] * inv_freq[None, :]
    cos, sin = jnp.cos(ang), jnp.sin(ang)
    one, zero = jnp.ones((s, NOPE), f32), jnp.zeros((s, NOPE), f32)
    pad1, pad0 = jnp.ones((s, HEAD_PAD - QK), f32), jnp.zeros((s, HEAD_PAD - QK), f32)
    return jnp.concatenate([one, cos, cos, pad1], 1), jnp.concatenate([zero, -sin, sin, pad0], 1)


def _qk_prep_fwd(q_raw, kv_raw, proj, qg, kg, cos_t, sin_t, *, name, ts=512):
    s = q_raw.shape[0]
    ts = min(ts, s)
    rope_blk = (MIX_IN_PAD - HEAD_PAD) // HEAD_PAD

    def body(q_ref, kv_ref, kr_ref, qg_ref, kg_ref, c_ref, s_ref, qo_ref, ko_ref, vo_ref):
        lane = lax.broadcasted_iota(jnp.int32, (ts, HEAD_PAD), 1)
        cosv, sinv = c_ref[...], s_ref[...]

        def norm_rope(z, gain):
            r = lax.rsqrt(jnp.sum(z * z, axis=-1, keepdims=True) * (1.0 / QK) + EPS)
            zn = z * r * gain
            return zn * cosv + _swap_halves(zn) * sinv

        kvv = kv_ref[...]
        qo_ref[...] = norm_rope(q_ref[...], qg_ref[...]).astype(bf16)
        ko_ref[...] = norm_rope(jnp.where(lane < NOPE, kvv, kr_ref[...]), kg_ref[...]).astype(bf16)
        vo_ref[...] = jnp.where(lane >= NOPE, kvv, 0.0).astype(bf16)

    head = pl.BlockSpec((ts, HEAD_PAD), lambda i, h: (i, h))
    row = pl.BlockSpec((ts, HEAD_PAD), lambda i, h: (i, 0))
    vec = pl.BlockSpec((1, HEAD_PAD), lambda i, h: (0, 0))
    out = jax.ShapeDtypeStruct((s, HEADS * HEAD_PAD), bf16)
    return pl.pallas_call(
        body, name=name, grid=(s // ts, HEADS),
        in_specs=[head, head, pl.BlockSpec((ts, HEAD_PAD), lambda i, h: (i, rope_blk)), vec, vec, row, row],
        out_specs=[head, head, head], out_shape=[out, out, out],
        compiler_params=_cparams(("parallel", "parallel")))(q_raw, kv_raw, proj, qg, kg, cos_t, sin_t)


def _qk_prep_bwd(dq, dk, dv, q_raw, kv_raw, proj, qg, kg, cos_t, sin_t, *, name, ts=512):
    s = q_raw.shape[0]
    ts = min(ts, s)
    rope_blk = (MIX_IN_PAD - HEAD_PAD) // HEAD_PAD

    def body(dq_ref, dk_ref, dv_ref, q_ref, kv_ref, kr_ref, qg_ref, kg_ref, c_ref, s_ref,
             dqr_ref, dkvr_ref, dkr_ref, dqg_ref, dkg_ref):
        i, h = pl.program_id(0), pl.program_id(1)
        lane = lax.broadcasted_iota(jnp.int32, (ts, HEAD_PAD), 1)
        is_rope = (lane >= NOPE) & (lane < QK)
        cosv, sinv = c_ref[...], s_ref[...]

        @pl.when((i == 0) & (h == 0))
        def _():
            dqg_ref[...] = jnp.zeros_like(dqg_ref)
            dkg_ref[...] = jnp.zeros_like(dkg_ref)

        @pl.when(h == 0)
        def _():
            dkr_ref[...] = jnp.zeros_like(dkr_ref)

        def back(dout, z, gain):
            dzn = dout * cosv + jnp.where(is_rope, _swap_halves(dout * sinv), 0.0)
            r = lax.rsqrt(jnp.sum(z * z, axis=-1, keepdims=True) * (1.0 / QK) + EPS)
            dzg = dzn * gain
            dz = r * dzg - z * (r * r * r) * (jnp.sum(z * dzg, axis=-1, keepdims=True) * (1.0 / QK))
            return dz, jnp.sum(dzn * z * r, axis=0, keepdims=True)

        dqz, dqg = back(dq_ref[...].astype(f32), q_ref[...], qg_ref[...])
        dqr_ref[...] = dqz.astype(bf16)
        dqg_ref[...] += dqg
        kvv = kv_ref[...]
        dkz, dkg = back(dk_ref[...].astype(f32), jnp.where(lane < NOPE, kvv, kr_ref[...]), kg_ref[...])
        dkg_ref[...] += dkg
        dkvr_ref[...] = jnp.where(lane < NOPE, dkz, dv_ref[...].astype(f32)).astype(bf16)
        dkr_ref[...] += jnp.where(is_rope, dkz, 0.0)

    head = pl.BlockSpec((ts, HEAD_PAD), lambda i, h: (i, h))
    row = pl.BlockSpec((ts, HEAD_PAD), lambda i, h: (i, 0))
    vec = pl.BlockSpec((1, HEAD_PAD), lambda i, h: (0, 0))
    wide = jax.ShapeDtypeStruct((s, HEADS * HEAD_PAD), bf16)
    return pl.pallas_call(
        body, name=name, grid=(s // ts, HEADS),
        in_specs=[head, head, head, head, head, pl.BlockSpec((ts, HEAD_PAD), lambda i, h: (i, rope_blk)), vec, vec, row, row],
        out_specs=[head, head, row, vec, vec],
        out_shape=[wide, wide, jax.ShapeDtypeStruct((s, HEAD_PAD), f32), jax.ShapeDtypeStruct((1, HEAD_PAD), f32),
                   jax.ShapeDtypeStruct((1, HEAD_PAD), f32)],
        compiler_params=_cparams(("arbitrary", "arbitrary")))(dq, dk, dv, q_raw, kv_raw, proj, qg, kg, cos_t, sin_t)


_NT = (((1,), (1,)), ((), ()))
_SCALE = QK ** -0.5


def _flash_fwd(q, k, v, *, name, tq=512):
    s = q.shape[0]
    tq = min(tq, s)

    def body(q_ref, k_ref, v_ref, o_ref, lse_ref):
        i = pl.program_id(1)
        qv = q_ref[...]

        def step(j, carry, masked):
            m, l, acc = carry
            st = pl.multiple_of(j * tq, tq)
            kj, vj = k_ref[pl.ds(st, tq), :], v_ref[pl.ds(st, tq), :]
            sc = lax.dot_general(qv, kj, _NT, preferred_element_type=f32) * _SCALE
            if masked:
                rr = lax.broadcasted_iota(jnp.int32, (tq, tq), 0)
                cc = lax.broadcasted_iota(jnp.int32, (tq, tq), 1)
                sc = jnp.where(cc <= rr, sc, NEG)
            m_new = jnp.maximum(m, jnp.max(sc, axis=-1, keepdims=True))
            p = jnp.exp(sc - m_new)
            alpha = jnp.exp(m - m_new)
            l = alpha * l + jnp.sum(p, axis=-1, keepdims=True)
            acc = alpha * acc + jnp.dot(p.astype(bf16), vj, preferred_element_type=f32)
            return m_new, l, acc

        init = (jnp.full((tq, 1), NEG, f32), jnp.zeros((tq, 1), f32), jnp.zeros((tq, HEAD_PAD), f32))
        carry = lax.fori_loop(0, i, lambda j, c: step(j, c, False), init)
        m, l, acc = step(i, carry, True)
        o_ref[...] = (acc / l).astype(bf16)
        lse_ref[0] = m + jnp.log(l)

    blk = pl.BlockSpec((tq, HEAD_PAD), lambda h, i: (i, h))
    full = pl.BlockSpec((s, HEAD_PAD), lambda h, i: (0, h))
    return pl.pallas_call(
        body, name=name, grid=(HEADS, s // tq), in_specs=[blk, full, full],
        out_specs=[blk, pl.BlockSpec((1, tq, 1), lambda h, i: (h, i, 0))],
        out_shape=[jax.ShapeDtypeStruct((s, HEADS * HEAD_PAD), bf16), jax.ShapeDtypeStruct((HEADS, s, 1), f32)],
        compiler_params=_cparams(("parallel", "arbitrary")))(q, k, v)


def _flash_bwd_dq(q, k, v, o, do, lse, *, name, tq=512):
    s = q.shape[0]
    tq = min(tq, s)

    def body(q_ref, k_ref, v_ref, o_ref, do_ref, lse_ref, dq_ref, dl_ref):
        i = pl.program_id(1)
        qv = q_ref[...]
        dov = do_ref[...].astype(f32)
        delta = jnp.sum(dov * o_ref[...].astype(f32), axis=-1, keepdims=True)
        dob = dov.astype(bf16)
        lsev = lse_ref[0]

        def step(j, acc, masked):
            st = pl.multiple_of(j * tq, tq)
            kj, vj = k_ref[pl.ds(st, tq), :], v_ref[pl.ds(st, tq), :]
            sc = lax.dot_general(qv, kj, _NT, preferred_element_type=f32) * _SCALE
            p = jnp.exp(sc - lsev)
            if masked:
                rr = lax.broadcasted_iota(jnp.int32, (tq, tq), 0)
                cc = lax.broadcasted_iota(jnp.int32, (tq, tq), 1)
                p = jnp.where(cc <= rr, p, 0.0)
            dp = lax.dot_general(dob, vj, _NT, preferred_element_type=f32)
            ds = p * (dp - delta)
            return acc + jnp.dot(ds.astype(bf16), kj, preferred_element_type=f32)

        acc = lax.fori_loop(0, i, lambda j, c: step(j, c, False), jnp.zeros((tq, HEAD_PAD), f32))
        dq_ref[...] = step(i, acc, True) * _SCALE
        dl_ref[0] = delta

    blk = pl.BlockSpec((tq, HEAD_PAD), lambda h, i: (i, h))
    full = pl.BlockSpec((s, HEAD_PAD), lambda h, i: (0, h))
    col = pl.BlockSpec((1, tq, 1), lambda h, i: (h, i, 0))
    return pl.pallas_call(
        body, name=name, grid=(HEADS, s // tq), in_specs=[blk, full, full, blk, blk, col],
        out_specs=[blk, col],
        out_shape=[jax.ShapeDtypeStruct((s, HEADS * HEAD_PAD), f32), jax.ShapeDtypeStruct((HEADS, s, 1), f32)],
        compiler_params=_cparams(("parallel", "arbitrary")))(q, k, v, o, do, lse)


def _flash_bwd_dkv(q, k, v, do, lse_row, delta_row, *, name, tk=512):
    s = q.shape[0]
    tk = min(tk, s)
    nblk = s // tk

    def body(q_ref, k_ref, v_ref, do_ref, lse_ref, dl_ref, dk_ref, dv_ref):
        j = pl.program_id(1)
        kv_, vv = k_ref[...], v_ref[...]

        def step(i, carry, masked):
            dk, dv = carry
            st = pl.multiple_of(i * tk, tk)
            qi = q_ref[pl.ds(st, tk), :]
            doi = do_ref[pl.ds(st, tk), :].astype(bf16)
            lse_i = lse_ref[0, :, pl.ds(st, tk)]
            dl_i = dl_ref[0, :, pl.ds(st, tk)]
            st_ = lax.dot_general(kv_, qi, _NT, preferred_element_type=f32) * _SCALE
            pt = jnp.exp(st_ - lse_i)
            if masked:
                kk = lax.broadcasted_iota(jnp.int32, (tk, tk), 0)
                qq = lax.broadcasted_iota(jnp.int32, (tk, tk), 1)
                pt = jnp.where(kk <= qq, pt, 0.0)
            dv = dv + jnp.dot(pt.astype(bf16), doi, preferred_element_type=f32)
            dpt = lax.dot_general(vv, doi, _NT, preferred_element_type=f32)
            dst = pt * (dpt - dl_i)
            dk = dk + jnp.dot(dst.astype(bf16), qi, preferred_element_type=f32)
            return dk, dv

        zero = jnp.zeros((tk, HEAD_PAD), f32)
        carry = step(j, (zero, zero), True)
        dk, dv = lax.fori_loop(j + 1, nblk, lambda i, c: step(i, c, False), carry)
        dk_ref[...] = dk * _SCALE
        dv_ref[...] = dv

    blk = pl.BlockSpec((tk, HEAD_PAD), lambda h, j: (j, h))
    full = pl.BlockSpec((s, HEAD_PAD), lambda h, j: (0, h))
    rowv = pl.BlockSpec((1, 1, s), lambda h, j: (h, 0, 0))
    out = jax.ShapeDtypeStruct((s, HEADS * HEAD_PAD), f32)
    return pl.pallas_call(
        body, name=name, grid=(HEADS, nblk), in_specs=[full, blk, blk, full, rowv, rowv],
        out_specs=[blk, blk], out_shape=[out, out],
        compiler_params=_cparams(("parallel", "arbitrary")))(q, k, v, do, lse_row, delta_row)


def _shift_down(x, d):
    t = lax.broadcasted_iota(jnp.int32, x.shape, 0)
    return jnp.where(t < d, 0.0, pltpu.roll(x, d, axis=0))


def _shift_up(x, d):
    s = x.shape[0]
    t = lax.broadcasted_iota(jnp.int32, x.shape, 0)
    return jnp.where(t >= s - d, 0.0, pltpu.roll(x, s - d, axis=0))


def _taps(w_ref):
    return w_ref[0:1, :], w_ref[1:2, :], w_ref[2:3, :]


def _conv3(u, w):
    return w[0] * _shift_down(u, 2) + w[1] * _shift_down(u, 1) + w[2] * u


def _conv3_t(g, w):
    return w[2] * g + w[1] * _shift_up(g, 1) + w[0] * _shift_up(g, 2)


def _conv3_dw(dw_ref, g, u):
    dw_ref[0:1, :] = jnp.sum(g * _shift_down(u, 2), axis=0, keepdims=True)
    dw_ref[1:2, :] = jnp.sum(g * _shift_down(u, 1), axis=0, keepdims=True)
    dw_ref[2:3, :] = jnp.sum(g * u, axis=0, keepdims=True)


_GB, _GC, _CI = 512 // LANES, 1024 // LANES, 1536 // LANES


def _sconv_fwd(proj, w, *, name):
    s = proj.shape[0]

    def body(gb_ref, gc_ref, ci_ref, w_ref, o_ref):
        o_ref[...] = (gb_ref[...] * _conv3(gc_ref[...] * ci_ref[...], _taps(w_ref))).astype(bf16)

    col = lambda off: pl.BlockSpec((s, LANES), lambda j: (0, off + j))
    return pl.pallas_call(
        body, name=name, grid=(CONV_CH // LANES,),
        in_specs=[col(_GB), col(_GC), col(_CI), pl.BlockSpec((3, LANES), lambda j: (0, j))],
        out_specs=pl.BlockSpec((s, LANES), lambda j: (0, j)),
        out_shape=jax.ShapeDtypeStruct((s, CONV_CH), bf16),
        compiler_params=_cparams(("parallel",)))(proj, proj, proj, w)


def _sconv_bwd(dmix, proj, w, *, name):
    s = proj.shape[0]

    def body(do_ref, gb_ref, gc_ref, ci_ref, w_ref, dgb_ref, dgc_ref, dci_ref, dw_ref):
        wv, gc, ci, do = _taps(w_ref), gc_ref[...], ci_ref[...], do_ref[...].astype(f32)
        u = gc * ci
        dgb_ref[...] = (do * _conv3(u, wv)).astype(bf16)
        dc = do * gb_ref[...]
        du = _conv3_t(dc, wv)
        dgc_ref[...] = (du * ci).astype(bf16)
        dci_ref[...] = (du * gc).astype(bf16)
        _conv3_dw(dw_ref, dc, u)

    col = lambda off: pl.BlockSpec((s, LANES), lambda j: (0, off + j))
    out = jax.ShapeDtypeStruct((s, CONV_CH), bf16)
    return pl.pallas_call(
        body, name=name, grid=(CONV_CH // LANES,),
        in_specs=[col(HEADS), col(_GB), col(_GC), col(_CI), pl.BlockSpec((3, LANES), lambda j: (0, j))],
        out_specs=[col(0), col(0), col(0), pl.BlockSpec((3, LANES), lambda j: (0, j))],
        out_shape=[out, out, out, jax.ShapeDtypeStruct((3, CONV_CH), f32)],
        compiler_params=_cparams(("parallel",)))(dmix, proj, proj, proj, w)


def _ffn_act_fwd(zg, zv, cwg, cwv, *, name):
    s, f = zg.shape

    def body(zg_ref, zv_ref, wg_ref, wv_ref, o_ref):
        o_ref[...] = (jax.nn.silu(_conv3(zg_ref[...], _taps(wg_ref))) * _conv3(zv_ref[...], _taps(wv_ref))).astype(bf16)

    col = pl.BlockSpec((s, LANES), lambda j: (0, j))
    wsp = pl.BlockSpec((3, LANES), lambda j: (0, j))
    return pl.pallas_call(
        body, name=name, grid=(f // LANES,), in_specs=[col, col, wsp, wsp], out_specs=col,
        out_shape=jax.ShapeDtypeStruct((s, f), bf16), compiler_params=_cparams(("parallel",)))(zg, zv, cwg, cwv)


def _ffn_act_bwd(da, zg, zv, cwg, cwv, *, name):
    s, f = zg.shape

    def body(da_ref, zg_ref, zv_ref, wg_ref, wv_ref, dzg_ref, dzv_ref, dwg_ref, dwv_ref):
        wg, wv, zgv, zvv, dav = _taps(wg_ref), _taps(wv_ref), zg_ref[...], zv_ref[...], da_ref[...].astype(f32)
        ug, uv = _conv3(zgv, wg), _conv3(zvv, wv)
        sg = jax.nn.sigmoid(ug)
        dug = dav * uv * (sg * (1.0 + ug * (1.0 - sg)))
        duv = dav * (ug * sg)
        dzg_ref[...] = _conv3_t(dug, wg).astype(bf16)
        dzv_ref[...] = _conv3_t(duv, wv).astype(bf16)
        _conv3_dw(dwg_ref, dug, zgv)
        _conv3_dw(dwv_ref, duv, zvv)

    col = pl.BlockSpec((s, LANES), lambda j: (0, j))
    wsp = pl.BlockSpec((3, LANES), lambda j: (0, j))
    act, wsh = jax.ShapeDtypeStruct((s, f), bf16), jax.ShapeDtypeStruct((3, f), f32)
    return pl.pallas_call(
        body, name=name, grid=(f // LANES,), in_specs=[col, col, col, wsp, wsp], out_specs=[col, col, wsp, wsp],
        out_shape=[act, act, wsh, wsh], compiler_params=_cparams(("parallel",)))(da, zg, zv, cwg, cwv)


def _expand_mat():
    return jnp.asarray(np.kron(np.eye(STATE, dtype=np.float32), np.ones((1, GROUP), np.float32)))


def _disc_fn(lr, li, ls, br, bi, e):
    dt = jnp.exp(ls)
    mag = jnp.exp(lr * dt)
    ar, ai = mag * jnp.cos(li * dt), mag * jnp.sin(li * dt)
    nr, ni = ar - 1.0, ai
    den = lr * lr + li * li
    zr, zi = (nr * lr + ni * li) / den, (ni * lr - nr * li) / den
    zrr = jnp.dot(zr, e, precision=lax.Precision.HIGHEST, preferred_element_type=f32)
    zir = jnp.dot(zi, e, precision=lax.Precision.HIGHEST, preferred_element_type=f32)
    return ar, ai, zrr * br - zir * bi, zrr * bi + zir * br


def _disc_fwd(lr, li, ls, br, bi, *, name):
    def body(lr_ref, li_ref, ls_ref, br_ref, bi_ref, e_ref, ar_ref, ai_ref, bbr_ref, bbi_ref):
        ar, ai, bbr, bbi = _disc_fn(lr_ref[...], li_ref[...], ls_ref[...], br_ref[...], bi_ref[...], e_ref[...])
        ar_ref[...], ai_ref[...], bbr_ref[...], bbi_ref[...] = ar, ai, bbr, bbi

    sq, wide = jax.ShapeDtypeStruct((GROUPS, STATE), f32), jax.ShapeDtypeStruct((GROUPS, STATE * GROUP), f32)
    return pl.pallas_call(body, name=name, out_shape=[sq, sq, wide, wide],
                          compiler_params=_cparams())(lr, li, ls, br, bi, _expand_mat())


def _disc_bwd(lr, li, ls, br, bi, dar, dai, dbbr, dbbi, *, name):
    def body(lr_ref, li_ref, ls_ref, br_ref, bi_ref, e_ref, dar_ref, dai_ref, dbbr_ref, dbbi_ref,
             dlr_ref, dli_ref, dls_ref, dbr_ref, dbi_ref):
        ev = e_ref[...]
        _, vjp = jax.vjp(lambda a, b, c, d_, e_: _disc_fn(a, b, c, d_, e_, ev),
                         lr_ref[...], li_ref[...], ls_ref[...], br_ref[...], bi_ref[...])
        dlr, dli, dls, dbr, dbi = vjp((dar_ref[...], dai_ref[...], dbbr_ref[...], dbbi_ref[...]))
        dlr_ref[...], dli_ref[...], dls_ref[...], dbr_ref[...], dbi_ref[...] = dlr, dli, dls, dbr, dbi

    sq, wide = jax.ShapeDtypeStruct((GROUPS, STATE), f32), jax.ShapeDtypeStruct((GROUPS, STATE * GROUP), f32)
    return pl.pallas_call(body, name=name, out_shape=[sq, sq, jax.ShapeDtypeStruct((GROUPS, 1), f32), wide, wide],
                          compiler_params=_cparams())(lr, li, ls, br, bi, _expand_mat(), dar, dai, dbbr, dbbi)


def _scan_steps(x, a1, a2, reverse):
    s = x.shape[0]
    d = 1
    while d < s:
        xs = _shift_up(x, d) if reverse else _shift_down(x, d)
        x = x + a1 * xs + a2 * pltpu.roll(xs, STATE, axis=1)
        a1, a2 = a1 * a1 - a2 * a2, 2.0 * a1 * a2
        d *= 2
    return x


def _scan_fwd(bu, a1, a2, *, name):
    s = bu.shape[0]

    def body(bu_ref, a1_ref, a2_ref, x_ref):
        x_ref[...] = _scan_steps(bu_ref[...], a1_ref[0], a2_ref[0], False)

    col = pl.BlockSpec((s, LANES), lambda g: (0, g))
    vec = pl.BlockSpec((1, 1, LANES), lambda g: (g, 0, 0))
    return pl.pallas_call(body, name=name, grid=(GROUPS,), in_specs=[col, vec, vec], out_specs=col,
                          out_shape=jax.ShapeDtypeStruct(bu.shape, f32), compiler_params=_cparams(("parallel",)))(bu, a1, a2)


def _scan_bwd(dx, x, a1, a2, *, name):
    s = dx.shape[0]

    def body(dx_ref, x_ref, a1_ref, a2_ref, g_ref, t1_ref, t2_ref):
        g = _scan_steps(dx_ref[...], a1_ref[0], -a2_ref[0], True)
        g_ref[...] = g.astype(bf16)
        xp = _shift_down(x_ref[...], 1)
        t1_ref[0] = jnp.sum(g * xp, axis=0, keepdims=True)
        t2_ref[0] = jnp.sum(g * pltpu.roll(xp, STATE, axis=1), axis=0, keepdims=True)

    col = pl.BlockSpec((s, LANES), lambda g: (0, g))
    vec = pl.BlockSpec((1, 1, LANES), lambda g: (g, 0, 0))
    vsh = jax.ShapeDtypeStruct((GROUPS, 1, LANES), f32)
    return pl.pallas_call(body, name=name, grid=(GROUPS,), in_specs=[col, col, vec, vec], out_specs=[col, vec, vec],
                          out_shape=[jax.ShapeDtypeStruct(dx.shape, bf16), vsh, vsh],
                          compiler_params=_cparams(("parallel",)))(dx, x, a1, a2)


_GELU_C = math.sqrt(2.0 / math.pi)


def _gelu_fwd(y, u, dsk, *, name, ts=512):
    s, d = y.shape
    ts = min(ts, s)

    def body(y_ref, u_ref, d_ref, o_ref):
        o_ref[...] = jax.nn.gelu(y_ref[...] + d_ref[...] * u_ref[...]).astype(bf16)

    row, vec = pl.BlockSpec((ts, d), lambda i: (i, 0)), pl.BlockSpec((1, d), lambda i: (0, 0))
    return pl.pallas_call(body, name=name, grid=(s // ts,), in_specs=[row, row, vec], out_specs=row,
                          out_shape=jax.ShapeDtypeStruct((s, d), bf16), compiler_params=_cparams(("parallel",)))(y, u, dsk)


def _gelu_bwd(dg, y, u, dsk, *, name, ts=512):
    s, d = y.shape
    ts = min(ts, s)

    def body(dg_ref, y_ref, u_ref, d_ref, dy_ref, du_ref, dd_ref):
        @pl.when(pl.program_id(0) == 0)
        def _():
            dd_ref[...] = jnp.zeros_like(dd_ref)

        uv, dv = u_ref[...], d_ref[...]
        z = y_ref[...] + dv * uv
        th = jnp.tanh(_GELU_C * (z + 0.044715 * z * z * z))
        dz = dg_ref[...] * (0.5 * (1.0 + th) + 0.5 * z * (1.0 - th * th) * _GELU_C * (1.0 + 3 * 0.044715 * z * z))
        dy_ref[...] = dz.astype(bf16)
        du_ref[...] = dz * dv
        dd_ref[...] += jnp.sum(dz * uv, axis=0, keepdims=True)

    row, vec = pl.BlockSpec((ts, d), lambda i: (i, 0)), pl.BlockSpec((1, d), lambda i: (0, 0))
    return pl.pallas_call(
        body, name=name, grid=(s // ts,), in_specs=[row, row, row, vec], out_specs=[row, row, vec],
        out_shape=[jax.ShapeDtypeStruct((s, d), bf16), jax.ShapeDtypeStruct((s, d), f32), jax.ShapeDtypeStruct((1, d), f32)],
        compiler_params=_cparams(("arbitrary",)))(dg, y, u, dsk)


def _glu_fwd(x, a, b, *, name, ts=512):
    s, d = x.shape
    ts = min(ts, s)

    def body(x_ref, a_ref, b_ref, o_ref):
        o_ref[...] = x_ref[...] + a_ref[...] * jax.nn.sigmoid(b_ref[...])

    row = pl.BlockSpec((ts, d), lambda i: (i, 0))
    return pl.pallas_call(body, name=name, grid=(s // ts,), in_specs=[row, row, row], out_specs=row,
                          out_shape=jax.ShapeDtypeStruct((s, d), f32), compiler_params=_cparams(("parallel",)))(x, a, b)


def _glu_bwd(dx, a, b, *, name, ts=512):
    s, d = dx.shape
    ts = min(ts, s)

    def body(dx_ref, a_ref, b_ref, da_ref, db_ref):
        sg = jax.nn.sigmoid(b_ref[...])
        dxv = dx_ref[...]
        da_ref[...] = (dxv * sg).astype(bf16)
        db_ref[...] = (dxv * a_ref[...] * sg * (1.0 - sg)).astype(bf16)

    row = pl.BlockSpec((ts, d), lambda i: (i, 0))
    out = jax.ShapeDtypeStruct((s, d), bf16)
    return pl.pallas_call(body, name=name, grid=(s // ts,), in_specs=[row, row, row], out_specs=[row, row],
                          out_shape=[out, out], compiler_params=_cparams(("parallel",)))(dx, a, b)


def _add(a, b, *, name, ts=512):
    s, d = a.shape
    ts = min(ts, s)

    def body(a_ref, b_ref, o_ref):
        o_ref[...] = (a_ref[...].astype(f32) + b_ref[...].astype(f32)).astype(bf16)

    row = pl.BlockSpec((ts, d), lambda i: (i, 0))
    return pl.pallas_call(body, name=name, grid=(s // ts,), in_specs=[row, row], out_specs=row,
                          out_shape=jax.ShapeDtypeStruct((s, d), bf16), compiler_params=_cparams(("parallel",)))(a, b)


def _loss_head(y, target, *, name, ts=512):
    s, d = y.shape
    ts = min(ts, s)

    def body(y_ref, t_ref, dy_ref, l_ref):
        @pl.when(pl.program_id(0) == 0)
        def _():
            l_ref[...] = jnp.zeros_like(l_ref)

        e = y_ref[...] - t_ref[...]
        dy_ref[...] = e * (1.0 / d)
        l_ref[...] += 0.5 * jnp.sum(jnp.mean(e * e, axis=-1, keepdims=True))

    row = pl.BlockSpec((ts, d), lambda i: (i, 0))
    return pl.pallas_call(
        body, name=name, grid=(s // ts,), in_specs=[row, row], out_specs=[row, pl.BlockSpec((8, LANES), lambda i: (0, 0))],
        out_shape=[jax.ShapeDtypeStruct((s, d), f32), jax.ShapeDtypeStruct((8, LANES), f32)],
        compiler_params=_cparams(("arbitrary",)))(y, target)


def _adamw(w, g, m, v, *, name, tr=128):
    r, c = w.shape

    def body(w_ref, g_ref, m_ref, v_ref, d_ref, mo_ref, vo_ref):
        gv = g_ref[...]
        mn = ADAM_B1 * m_ref[...] + (1.0 - ADAM_B1) * gv
        vn = ADAM_B2 * v_ref[...] + (1.0 - ADAM_B2) * (gv * gv)
        m_hat = mn / (1.0 - ADAM_B1 ** ADAM_STEP)
        v_hat = vn / (1.0 - ADAM_B2 ** ADAM_STEP)
        d_ref[...] = -ADAM_LR * (m_hat / (jnp.sqrt(v_hat) + ADAM_EPS) + ADAM_WD * w_ref[...])
        mo_ref[...] = mn
        vo_ref[...] = vn

    row = pl.BlockSpec((tr, c), lambda i: (i, 0))
    out = jax.ShapeDtypeStruct((r, c), f32)
    return pl.pallas_call(body, name=name, grid=(r // tr,), in_specs=[row] * 4, out_specs=[row] * 3,
                          out_shape=[out, out, out], compiler_params=_cparams(("parallel",)))(w, g, m, v)


def _sum8(land, *, name, tr=128):
    _, r, c = land.shape

    def body(l_ref, o_ref):
        acc = l_ref[0].astype(f32)
        for dev in range(1, N_DEV):
            acc = acc + l_ref[dev].astype(f32)
        o_ref[...] = acc

    return pl.pallas_call(body, name=name, grid=(r // tr,), in_specs=[pl.BlockSpec((N_DEV, tr, c), lambda i: (0, i, 0))],
                          out_specs=pl.BlockSpec((tr, c), lambda i: (i, 0)), out_shape=jax.ShapeDtypeStruct((r, c), f32),
                          compiler_params=_cparams(("parallel",)))(land)


def _exchange(x, *, scatter, name):
    shape = x.shape[-2:]

    def body(x_ref, o_ref, send_sems, recv_sems, local_sem):
        xx, yy, cc = lax.axis_index("x"), lax.axis_index("y"), lax.axis_index("c")
        me = 4 * xx + 2 * yy + cc

        def peer(k):
            px = 1 - xx if k & 4 else xx
            py = 1 - yy if k & 2 else yy
            pc = 1 - cc if k & 1 else cc
            return (px, py, pc), 4 * px + 2 * py + pc

        def src(idx):
            return x_ref.at[idx] if scatter else x_ref

        local = pltpu.make_async_copy(src(me), o_ref.at[me], local_sem)
        local.start()
        sends = []
        for k in range(1, N_DEV):
            pid, pidx = peer(k)
            cp = pltpu.make_async_remote_copy(src_ref=src(pidx), dst_ref=o_ref.at[me], send_sem=send_sems.at[k - 1],
                                              recv_sem=recv_sems.at[k - 1], device_id=pid, device_id_type=pl.DeviceIdType.MESH)
            cp.start()
            sends.append(cp)
        for k in range(1, N_DEV):
            pid, pidx = peer(k)
            pltpu.make_async_remote_copy(src_ref=src(pidx), dst_ref=o_ref.at[pidx], send_sem=send_sems.at[k - 1],
                                         recv_sem=recv_sems.at[k - 1], device_id=pid,
                                         device_id_type=pl.DeviceIdType.MESH).wait_recv()
        for cp in sends:
            cp.wait_send()
        local.wait()

    hbm = pl.BlockSpec(memory_space=pltpu.HBM)
    return pl.pallas_call(
        body, name=name, in_specs=[hbm], out_specs=hbm, out_shape=jax.ShapeDtypeStruct((N_DEV,) + shape, x.dtype),
        scratch_shapes=[pltpu.SemaphoreType.DMA((N_DEV - 1,)), pltpu.SemaphoreType.DMA((N_DEV - 1,)), pltpu.SemaphoreType.DMA],
    )(x)


def _pack_rows(parts, rows):
    flat = jnp.concatenate([p.reshape(-1) for p in parts])
    return jnp.pad(flat, (0, rows * D - flat.shape[0])).reshape(rows, D)


def _unpack_rows(slab, shapes):
    flat, out, off = slab.reshape(-1), [], 0
    for shp in shapes:
        n = int(np.prod(shp))
        out.append(flat[off:off + n].reshape(shp))
        off += n
    return out


def _full_shape(shard, axis):
    return tuple(d * N_DEV if i == axis else d for i, d in enumerate(shard))


def _gather_full(slabs, shard, axis):
    t = jnp.moveaxis(slabs.reshape((N_DEV,) + shard), 0, axis)
    return t.reshape(_full_shape(shard, axis))


def _split_full(full, shard, axis):
    shp = shard[:axis] + (N_DEV, shard[axis]) + shard[axis + 1:]
    return jnp.moveaxis(full.reshape(shp), axis, 0).reshape(N_DEV, -1)


def _row(v):
    return v.reshape(1, -1).astype(f32)


def _pad_gain(g):
    return jnp.pad(g.astype(f32), (0, HEAD_PAD - QK)).reshape(1, HEAD_PAD)


def _ffn_fwd(x, p, tag):
    h = _rms_fwd(x, p["norm"], name=f"ffn_norm_{tag}")
    zg = _mm(h, p["wg"], name=f"ffn_up_g_{tag}")
    zv = _mm(h, p["wv"], name=f"ffn_up_v_{tag}")
    a = _ffn_act_fwd(zg, zv, p["cwg"], p["cwv"], name=f"ffn_act_{tag}")
    y = _mm(a, p["wd"], add=x, name=f"ffn_down_{tag}")
    return y, (x, h, zg, zv, a)


def _ffn_bwd(dy, p, saved, tag):
    x, h, zg, zv, a = saved
    g = {}
    da = _mm(dy, p["wd"], tb=True, out_dtype=bf16, name=f"ffn_down_dx_{tag}")
    g["wd"] = _mm(a, dy, ta=True, name=f"ffn_down_dw_{tag}")
    dzg, dzv, g["cwg"], g["cwv"] = _ffn_act_bwd(da, zg, zv, p["cwg"], p["cwv"], name=f"ffn_act_bwd_{tag}")
    g["wg"] = _mm(h, dzg, ta=True, name=f"ffn_up_g_dw_{tag}")
    g["wv"] = _mm(h, dzv, ta=True, name=f"ffn_up_v_dw_{tag}")
    dh = _mm(dzg, p["wg"], tb=True, name=f"ffn_up_g_dx_{tag}")
    dh = _mm(dzv, p["wv"], tb=True, add=dh, name=f"ffn_up_v_dx_{tag}")
    dx, g["norm"] = _rms_bwd(dh, x, p["norm"], res=dy, name=f"ffn_norm_bwd_{tag}")
    return dx, g


def _mla_fwd(x, p, tabs, tag):
    cos_t, sin_t = tabs
    h = _rms_fwd(x, p["norm"], name=f"attn_norm_{tag}")
    proj = _mm(h, p["w_in"], name=f"mix_in_{tag}")
    cqn = _rms_fwd(proj, p["cq_norm"], col=0, name=f"cq_norm_{tag}")
    ckvn = _rms_fwd(proj, p["ckv_norm"], col=1, name=f"ckv_norm_{tag}")
    q_raw = _mm(cqn, p["w_uq"], name=f"uq_{tag}")
    kv_raw = _mm(ckvn, p["w_ukv"], name=f"ukv_{tag}")
    q, k, v = _qk_prep_fwd(q_raw, kv_raw, proj, p["q_gain"], p["k_gain"], cos_t, sin_t, name=f"qk_prep_{tag}")
    o, lse = _flash_fwd(q, k, v, name=f"flash_fwd_{tag}")
    conv = _sconv_fwd(proj, p["sconv_w"], name=f"sconv_{tag}")
    mix = jnp.concatenate([o, conv], axis=1)
    y = _mm(mix, p["w_out"], add=x, name=f"mix_out_{tag}")
    return y, (x, h, proj, cqn, ckvn, q_raw, kv_raw, q, k, v, o, lse, mix)


def _mla_bwd(dy, p, tabs, saved, tag):
    cos_t, sin_t = tabs
    x, h, proj, cqn, ckvn, q_raw, kv_raw, q, k, v, o, lse, mix = saved
    s = x.shape[0]
    g = {}
    dmix = _mm(dy, p["w_out"], tb=True, name=f"mix_out_dx_{tag}")
    g["w_out"] = _mm(mix, dy, ta=True, name=f"mix_out_dw_{tag}")
    dgb, dgc, dci, g["sconv_w"] = _sconv_bwd(dmix, proj, p["sconv_w"], name=f"sconv_bwd_{tag}")
    dq, delta = _flash_bwd_dq(q, k, v, o, dmix, lse, name=f"flash_dq_{tag}")
    dk, dv = _flash_bwd_dkv(q, k, v, dmix, lse.reshape(HEADS, 1, s), delta.reshape(HEADS, 1, s), name=f"flash_dkv_{tag}")
    dq_raw, dkv_raw, dkr, g["q_gain"], g["k_gain"] = _qk_prep_bwd(
        dq, dk, dv, q_raw, kv_raw, proj, p["q_gain"], p["k_gain"], cos_t, sin_t, name=f"qk_prep_bwd_{tag}")
    dcqn = _mm(dq_raw, p["w_uq"], tb=True, name=f"uq_dx_{tag}")
    g["w_uq"] = _mm(cqn, dq_raw, ta=True, name=f"uq_dw_{tag}")
    dckvn = _mm(dkv_raw, p["w_ukv"], tb=True, name=f"ukv_dx_{tag}")
    g["w_ukv"] = _mm(ckvn, dkv_raw, ta=True, name=f"ukv_dw_{tag}")
    dcq, g["cq_norm"] = _rms_bwd(dcqn, proj, p["cq_norm"], col=0, out_dtype=bf16, name=f"cq_norm_bwd_{tag}")
    dckv, g["ckv_norm"] = _rms_bwd(dckvn, proj, p["ckv_norm"], col=1, out_dtype=bf16, name=f"ckv_norm_bwd_{tag}")
    dproj = jnp.concatenate([dcq, dckv, dgb, dgc, dci, dkr.astype(bf16)], axis=1)
    dh = _mm(dproj, p["w_in"], tb=True, name=f"mix_in_dx_{tag}")
    g["w_in"] = _mm(h, dproj, ta=True, name=f"mix_in_dw_{tag}")
    dx, g["norm"] = _rms_bwd(dh, x, p["norm"], res=dy, name=f"attn_norm_bwd_{tag}")
    return dx, g


def _block_diag(wg):
    nb, ng, r, c = wg.shape
    eye = jnp.eye(ng, dtype=wg.dtype)
    return (wg[:, :, :, None, :] * eye[None, :, None, :, None]).reshape(nb, ng * r, ng * c)


def _block_diag_extract(wbd, r, c):
    nb = wbd.shape[0]
    ng = wbd.shape[1] // r
    eye = jnp.eye(ng, dtype=wbd.dtype)
    return jnp.sum(wbd.reshape(nb, ng, r, ng, c) * eye[None, :, None, :, None], axis=3)


def _s5_mats(bbr, bbi, c_re, c_im):
    nb = GROUPS // 8
    b4 = jnp.stack([bbr.reshape(GROUPS, STATE, GROUP), bbi.reshape(GROUPS, STATE, GROUP)], axis=1)
    wg = jnp.transpose(b4, (0, 3, 1, 2)).reshape(nb, 8, GROUP, 2 * STATE)
    cg = jnp.stack([c_re, -c_im], axis=1)
    cg = jnp.transpose(cg, (0, 1, 3, 2)).reshape(nb, 8, 2 * STATE, GROUP)
    return _block_diag(wg), _block_diag(cg)


def _s5_fwd(x, p, tag):
    h = _rms_fwd(x, p["norm"], name=f"ssm_norm_{tag}")
    u = _mm(h, p["w_in"], name=f"ssm_in_{tag}")
    ar, ai, bbr, bbi = _disc_fwd(p["lr"], p["li"], p["ls"], p["br"], p["bi"], name=f"disc_{tag}")
    wb, cb = _s5_mats(bbr, bbi, p["c_re"], p["c_im"])
    a1 = jnp.concatenate([ar, ar], axis=1).reshape(GROUPS, 1, LANES)
    a2 = jnp.concatenate([-ai, ai], axis=1).reshape(GROUPS, 1, LANES)
    bu = _bd_nn(u, wb.astype(bf16), name=f"ssm_bu_{tag}")
    xs = _scan_fwd(bu, a1, a2, name=f"ssm_scan_{tag}")
    y = _bd_nn(xs, cb.astype(bf16), name=f"ssm_y_{tag}")
    g = _gelu_fwd(y, u, p["d_skip"], name=f"ssm_gelu_{tag}")
    a = _mm(g, p["wga"], name=f"glu_a_{tag}")
    b = _mm(g, p["wgb"], name=f"glu_b_{tag}")
    out = _glu_fwd(x, a, b, name=f"glu_{tag}")
    return out, (x, h, u, wb, cb, a1, a2, xs, y, g, a, b)


def _s5_bwd(dout, p, saved, tag):
    x, h, u, wb, cb, a1, a2, xs, y, g, a, b = saved
    gr = {}
    da, db = _glu_bwd(dout, a, b, name=f"glu_bwd_{tag}")
    dg = _mm(da, p["wga"], tb=True, name=f"glu_a_dx_{tag}")
    dg = _mm(db, p["wgb"], tb=True, add=dg, name=f"glu_b_dx_{tag}")
    gr["wga"] = _mm(g, da, ta=True, name=f"glu_a_dw_{tag}")
    gr["wgb"] = _mm(g, db, ta=True, name=f"glu_b_dw_{tag}")
    dy, du1, gr["d_skip"] = _gelu_bwd(dg, y, u, p["d_skip"], name=f"ssm_gelu_bwd_{tag}")
    dxs = _bd_nn(dy, jnp.swapaxes(cb, 1, 2).astype(bf16), name=f"ssm_y_dx_{tag}")
    dcb = _bd_tn(xs, dy, 8 * LANES, LANES, name=f"ssm_y_dw_{tag}")
    gs, t1, t2 = _scan_bwd(dxs, xs, a1, a2, name=f"ssm_scan_bwd_{tag}")
    du2 = _bd_nn(gs, jnp.swapaxes(wb, 1, 2).astype(bf16), name=f"ssm_bu_dx_{tag}")
    dwb = _bd_tn(u, gs, LANES, 8 * LANES, name=f"ssm_bu_dw_{tag}")
    du = _add(du1, du2, name=f"ssm_du_{tag}")
    dh = _mm(du, p["w_in"], tb=True, name=f"ssm_in_dx_{tag}")
    gr["w_in"] = _mm(h, du, ta=True, name=f"ssm_in_dw_{tag}")
    dx, gr["norm"] = _rms_bwd(dh, x, p["norm"], res=dout, name=f"ssm_norm_bwd_{tag}")
    t1, t2 = t1.reshape(GROUPS, LANES), t2.reshape(GROUPS, LANES)
    dar = t1[:, :STATE] + t1[:, STATE:]
    dai = t2[:, STATE:] - t2[:, :STATE]
    dwg = _block_diag_extract(dwb, GROUP, 2 * STATE).reshape(GROUPS, GROUP, 2, STATE)
    dbb = jnp.transpose(dwg, (2, 0, 3, 1)).reshape(2, GROUPS, STATE * GROUP)
    dcg = _block_diag_extract(dcb, 2 * STATE, GROUP).reshape(GROUPS, 2, STATE, GROUP)
    gr["c_re"] = jnp.transpose(dcg[:, 0], (0, 2, 1))
    gr["c_im"] = -jnp.transpose(dcg[:, 1], (0, 2, 1))
    dlr, dli, dls, dbr, dbi = _disc_bwd(p["lr"], p["li"], p["ls"], p["br"], p["bi"], dar, dai, dbb[0], dbb[1],
                                        name=f"disc_bwd_{tag}")
    gr["lr"], gr["li"], gr["ls"] = dlr, dli, dls.reshape(GROUPS)
    gr["br"], gr["bi"] = dbr.reshape(GROUPS, STATE, GROUP), dbi.reshape(GROUPS, STATE, GROUP)
    return dx, gr


def _mix_in_pad(w):
    z = lambda n: jnp.zeros((w.shape[0], n), w.dtype)
    return jnp.concatenate([w[:, :512], w[:, 544:2080], z(NOPE), w[:, 512:544], z(HEAD_PAD - QK)], axis=1)


def _mix_in_unpad(g):
    return jnp.concatenate([g[:, :512], g[:, 2048 + NOPE:2048 + QK], g[:, 512:2048]], axis=1)


def _uq_pad(w):
    return jnp.pad(w.reshape(LORA, HEADS, QK), ((0, 0), (0, 0), (0, HEAD_PAD - QK))).reshape(LORA, HEADS * HEAD_PAD)


def _uq_unpad(g):
    return g.reshape(LORA, HEADS, HEAD_PAD)[:, :, :QK].reshape(LORA, HEADS * QK)


def _mix_out_pad(w):
    att = jnp.pad(w[:512].reshape(HEADS, NOPE, D), ((0, 0), (NOPE, 0), (0, 0))).reshape(HEADS * HEAD_PAD, D)
    return jnp.concatenate([att, w[512:]], axis=0)


def _mix_out_unpad(g):
    att = g[:HEADS * HEAD_PAD].reshape(HEADS, HEAD_PAD, D)[:, NOPE:, :].reshape(HEADS * NOPE, D)
    return jnp.concatenate([att, g[HEADS * HEAD_PAD:]], axis=0)


def _layer_params(w, layer):
    i = layer // 2
    ffn = dict(norm=_row(w["ffn_norm"][layer]), wg=w["ffn_w_up"][layer][:, :FFN_H], wv=w["ffn_w_up"][layer][:, FFN_H:],
               cwg=w["ffn_conv_w"][layer][:, :FFN_H], cwv=w["ffn_conv_w"][layer][:, FFN_H:], wd=w["ffn_w_down"][layer])
    if layer % 2 == 0:
        mixer = dict(norm=_row(w["attn_norm"][i]), w_in=_mix_in_pad(w["mix_w_in"][i]), cq_norm=_row(w["cq_norm"][i]),
                     ckv_norm=_row(w["ckv_norm"][i]), w_uq=_uq_pad(w["w_uq"][i]), w_ukv=w["w_ukv"][i],
                     q_gain=_pad_gain(w["q_gain"][i]), k_gain=_pad_gain(w["k_gain"][i]), sconv_w=w["sconv_w"][i],
                     w_out=_mix_out_pad(w["mix_w_out"][i]))
    else:
        mixer = dict(norm=_row(w["ssm_norm"][i]), w_in=w["ssm_w_in"][i], lr=w["lambda_re"][i], li=w["lambda_im"][i],
                     ls=w["log_step"][i].reshape(GROUPS, 1), br=w["b_re"][i].reshape(GROUPS, STATE * GROUP),
                     bi=w["b_im"][i].reshape(GROUPS, STATE * GROUP), c_re=w["c_re"][i], c_im=w["c_im"][i],
                     d_skip=_row(w["d_skip"][i]), wga=w["w_glu"][i][:, :D], wgb=w["w_glu"][i][:, D:])
    return mixer, ffn


def _collect_grads(gm, gf):
    st = lambda xs: jnp.stack(xs, axis=0)
    ev, od = (0, 2), (1, 3)
    out = {
        "attn_norm": st([gm[l]["norm"].reshape(D) for l in ev]),
        "mix_w_in": st([_mix_in_unpad(gm[l]["w_in"]) for l in ev]),
        "cq_norm": st([gm[l]["cq_norm"].reshape(LORA) for l in ev]),
        "ckv_norm": st([gm[l]["ckv_norm"].reshape(LORA) for l in ev]),
        "w_uq": st([_uq_unpad(gm[l]["w_uq"]) for l in ev]),
        "w_ukv": st([gm[l]["w_ukv"] for l in ev]),
        "q_gain": st([gm[l]["q_gain"].reshape(HEAD_PAD)[:QK] for l in ev]),
        "k_gain": st([gm[l]["k_gain"].reshape(HEAD_PAD)[:QK] for l in ev]),
        "sconv_w": st([gm[l]["sconv_w"] for l in ev]),
        "mix_w_out": st([_mix_out_unpad(gm[l]["w_out"]) for l in ev]),
        "ssm_norm": st([gm[l]["norm"].reshape(D) for l in od]),
        "ssm_w_in": st([gm[l]["w_in"] for l in od]),
        "lambda_re": st([gm[l]["lr"] for l in od]), "lambda_im": st([gm[l]["li"] for l in od]),
        "log_step": st([gm[l]["ls"] for l in od]),
        "b_re": st([gm[l]["br"] for l in od]), "b_im": st([gm[l]["bi"] for l in od]),
        "c_re": st([gm[l]["c_re"] for l in od]), "c_im": st([gm[l]["c_im"] for l in od]),
        "d_skip": st([gm[l]["d_skip"].reshape(D) for l in od]),
        "w_glu": st([jnp.concatenate([gm[l]["wga"], gm[l]["wgb"]], axis=1) for l in od]),
        "ffn_norm": st([gf[l]["norm"].reshape(D) for l in range(4)]),
        "ffn_w_up": st([jnp.concatenate([gf[l]["wg"], gf[l]["wv"]], axis=1) for l in range(4)]),
        "ffn_conv_w": st([jnp.concatenate([gf[l]["cwg"], gf[l]["cwv"]], axis=1) for l in range(4)]),
        "ffn_w_down": st([gf[l]["wd"] for l in range(4)]),
    }
    return out


def _local_step(x, target, w):
    s = x.shape[0]
    tabs = _rope_tables(s)
    saved, params = [], []
    for layer in range(4):
        mixer, ffn = _layer_params(w, layer)
        params.append((mixer, ffn))
        if layer % 2 == 0:
            x, sm = _mla_fwd(x, mixer, tabs, f"l{layer}")
        else:
            x, sm = _s5_fwd(x, mixer, f"l{layer}")
        x, sf = _ffn_fwd(x, ffn, f"l{layer}")
        saved.append((sm, sf))
    dx, loss = _loss_head(x, target, name="loss_head")
    gm, gf = [None] * 4, [None] * 4
    for layer in reversed(range(4)):
        mixer, ffn = params[layer]
        sm, sf = saved[layer]
        dx, gf[layer] = _ffn_bwd(dx, ffn, sf, f"l{layer}")
        if layer % 2 == 0:
            dx, gm[layer] = _mla_bwd(dx, mixer, tabs, sm, f"l{layer}")
        else:
            dx, gm[layer] = _s5_bwd(dx, mixer, sm, f"l{layer}")
    return loss, dx, _collect_grads(gm, gf)


def kernel(x, attn_norm, mix_w_in, cq_norm, ckv_norm, w_uq, w_ukv, q_gain, k_gain, sconv_w, mix_w_out, ssm_norm, ssm_w_in, lambda_re, lambda_im, log_step, b_re, b_im, c_re, c_im, d_skip, w_glu, ffn_norm, ffn_w_up, ffn_conv_w, ffn_w_down, loss_target, m_attn_norm, m_mix_w_in, m_cq_norm, m_ckv_norm, m_w_uq, m_w_ukv, m_q_gain, m_k_gain, m_sconv_w, m_mix_w_out, m_ssm_norm, m_ssm_w_in, m_lambda_re, m_lambda_im, m_log_step, m_b_re, m_b_im, m_c_re, m_c_im, m_d_skip, m_w_glu, m_ffn_norm, m_ffn_w_up, m_ffn_conv_w, m_ffn_w_down, v_attn_norm, v_mix_w_in, v_cq_norm, v_ckv_norm, v_w_uq, v_w_ukv, v_q_gain, v_k_gain, v_sconv_w, v_mix_w_out, v_ssm_norm, v_ssm_w_in, v_lambda_re, v_lambda_im, v_log_step, v_b_re, v_b_im, v_c_re, v_c_im, v_d_skip, v_w_glu, v_ffn_norm, v_ffn_w_up, v_ffn_conv_w, v_ffn_w_down):
    args = dict(locals())
    wsh = {n: args[n] for n in WEIGHTS}
    msh = {n: args["m_" + n] for n in WEIGHTS}
    vsh = {n: args["v_" + n] for n in WEIGHTS}
    me = 4 * lax.axis_index("x") + 2 * lax.axis_index("y") + lax.axis_index("c")

    big_shard = _pack_rows([wsh[n].astype(bf16) for n, _, _ in BIG], BIG_ROWS)
    big_all = _exchange(big_shard, scatter=False, name="gather_weights").reshape(N_DEV, -1)
    w, off = {}, 0
    for n, shard, axis in BIG:
        cnt = int(np.prod(shard))
        w[n] = _gather_full(big_all[:, off:off + cnt], shard, axis)
        off += cnt
    placed = []
    for n, shard, axis in SMALL:
        start = [0] * len(shard)
        start[axis] = me * shard[axis]
        placed.append(lax.dynamic_update_slice(jnp.zeros(_full_shape(shard, axis), f32), wsh[n], start))
    small_all = _sum8(_exchange(_pack_rows(placed, SMALL_FWD_ROWS), scatter=False, name="gather_small"),
                      name="sum_small_fwd", tr=SMALL_FWD_ROWS)
    for (n, shard, axis), full in zip(SMALL, _unpack_rows(small_all, [_full_shape(sh, ax) for _, sh, ax in SMALL])):
        w[n] = full
    for n, _ in REPL:
        w[n] = wsh[n]

    loss8, grad_x, grads = _local_step(x[0], loss_target[0], w)

    big_parts = jnp.concatenate([_split_full(grads[n].astype(bf16), shard, axis) for n, shard, axis in BIG], axis=1)
    big_parts = jnp.pad(big_parts, ((0, 0), (0, BIG_ROWS * D - big_parts.shape[1]))).reshape(N_DEV, BIG_ROWS, D)
    g_big = _sum8(_exchange(big_parts, scatter=True, name="scatter_grads"), name="sum_grads")
    small_vec = _pack_rows([grads[n] for n, _ in REPL] + [grads[n] for n, _, _ in SMALL] + [loss8[0, :1]], SMALL_ROWS)
    small_sum = _sum8(_exchange(small_vec, scatter=False, name="gather_small_grads"), name="sum_small_grads")
    parts = _unpack_rows(small_sum, [sh for _, sh in REPL] + [_full_shape(sh, ax) for _, sh, ax in SMALL] + [(1,)])
    g = {}
    for (n, _), val in zip(REPL, parts):
        g[n] = val
    for (n, shard, axis), val in zip(SMALL, parts[len(REPL):]):
        start = [0] * len(shard)
        start[axis] = me * shard[axis]
        g[n] = lax.dynamic_slice(val, start, shard)
    loss = parts[-1].reshape(())
    for (n, shard, _), val in zip(BIG, _unpack_rows(g_big, [sh for _, sh, _ in BIG])):
        g[n] = val

    big_names = [n for n, _, _ in BIG]
    small_names = [n for n, _ in REPL] + [n for n, _, _ in SMALL]
    delta, new_m, new_v = {}, {}, {}
    for names, rows, tag in ((big_names, BIG_ROWS, "big"), (small_names, SMALL_ROWS, "small")):
        outs = _adamw(*[_pack_rows([src[n] for n in names], rows) for src in (wsh, g, msh, vsh)], name=f"adamw_{tag}")
        shapes = [wsh[n].shape for n in names]
        for dst, slab in zip((delta, new_m, new_v), outs):
            for n, val in zip(names, _unpack_rows(slab, shapes)):
                dst[n] = val

    return (loss, grad_x[None], *[g[n] for n in WEIGHTS], *[delta[n] for n in WEIGHTS],
            *[new_m[n] for n in WEIGHTS], *[new_v[n] for n in WEIGHTS])
```

```python
import math

import numpy as np
import jax
import jax.numpy as jnp
from jax import lax
from jax.experimental import pallas as pl
from jax.experimental.pallas import tpu as pltpu

f32, bf16 = jnp.float32, jnp.bfloat16

N_DEV = 8
D = 1024
HEADS = 8
NOPE, ROPE, QK = 64, 32, 96
HEAD_PAD = 128
LORA = 256
CONV_CH = 512
MIX_IN_PAD = 2176
FFN_H = 2816
GROUPS, GROUP, STATE = 64, 16, 64
EPS = 1e-6
ROPE_THETA = 10000.0
ADAM_LR, ADAM_B1, ADAM_B2, ADAM_EPS, ADAM_WD, ADAM_STEP = 0.001, 0.9, 0.999, 1e-08, 0.01, 10
LANES = 128
VMEM_LIMIT = 56 << 20
MM_VMEM_BUDGET = 40 << 20
NEG = -1e30

BIG = (
    ("mix_w_in", (2, 1024, 260), 2), ("w_uq", (2, 256, 96), 2), ("w_ukv", (2, 256, 128), 2),
    ("mix_w_out", (2, 128, 1024), 1), ("ssm_w_in", (2, 128, 1024), 1), ("w_glu", (2, 1024, 256), 2),
    ("ffn_w_up", (4, 1024, 704), 2), ("ffn_w_down", (4, 352, 1024), 1))
REPL = (("attn_norm", (2, 1024)), ("cq_norm", (2, 256)), ("ckv_norm", (2, 256)), ("q_gain", (2, 96)),
        ("k_gain", (2, 96)), ("lambda_re", (2, 64, 64)), ("lambda_im", (2, 64, 64)), ("log_step", (2, 64)),
        ("b_re", (2, 64, 64, 16)), ("b_im", (2, 64, 64, 16)), ("c_re", (2, 64, 16, 64)), ("c_im", (2, 64, 16, 64)),
        ("ffn_norm", (4, 1024)))
SMALL = (("sconv_w", (2, 3, 64), 2), ("ssm_norm", (2, 128), 1), ("d_skip", (2, 128), 1), ("ffn_conv_w", (4, 3, 704), 2))
WEIGHTS = ['attn_norm', 'mix_w_in', 'cq_norm', 'ckv_norm', 'w_uq', 'w_ukv', 'q_gain', 'k_gain', 'sconv_w', 'mix_w_out',
           'ssm_norm', 'ssm_w_in', 'lambda_re', 'lambda_im', 'log_step', 'b_re', 'b_im', 'c_re', 'c_im', 'd_skip',
           'w_glu', 'ffn_norm', 'ffn_w_up', 'ffn_conv_w', 'ffn_w_down']
BIG_ROWS = 5888
SMALL_FWD_ROWS = 80
SMALL_ROWS = 640


def _cparams(sem=None, **kw):
    return pltpu.CompilerParams(dimension_semantics=sem, vmem_limit_bytes=VMEM_LIMIT, **kw)


def _tile(n, target):
    best = 0
    for t in range(LANES, min(n, target) + 1, LANES):
        if n % t == 0:
            best = t
    return best if best else n


def _mm(a, b, *, ta=False, tb=False, out_dtype=f32, add=None, name, tm=1024, tn=1536):
    m, k = (a.shape[1], a.shape[0]) if ta else a.shape
    n = b.shape[0] if tb else b.shape[1]
    assert (b.shape[1] if tb else b.shape[0]) == k
    tm = _tile(m, tm)
    tn_ = _tile(n, tn)
    tn = n if (tn_ < 256 and n <= 2304) else tn_

    def vmem_bytes(t):
        io = 2 * (tm * t * a.dtype.itemsize + t * tn * b.dtype.itemsize + tm * tn * jnp.dtype(out_dtype).itemsize)
        return io + (2 * tm * tn * 4 if add is not None else 0) + (tm * tn * 4 if t < k else 0)

    tk = next((t for t in [k] + [t for t in range(k - LANES, 0, -LANES) if k % t == 0] if vmem_bytes(t) <= MM_VMEM_BUDGET), LANES)
    nk = k // tk
    dn = (((0 if ta else 1,), (1 if tb else 0,)), ((), ()))

    def body(*refs):
        a_ref, b_ref = refs[:2]
        add_ref = refs[2] if add is not None else None
        o_ref = refs[3] if add is not None else refs[2]
        part = lax.dot_general(a_ref[...].astype(bf16), b_ref[...].astype(bf16), dn, preferred_element_type=f32)

        def finish(r):
            if add is not None:
                r = r + add_ref[...].astype(f32)
            o_ref[...] = r.astype(out_dtype)

        if nk == 1:
            finish(part)
            return
        acc = refs[-1]
        kk = pl.program_id(2)

        @pl.when(kk == 0)
        def _():
            acc[...] = part

        @pl.when(kk > 0)
        def _():
            acc[...] += part

        @pl.when(kk == nk - 1)
        def _():
            finish(acc[...])

    a_spec = pl.BlockSpec((tk, tm), lambda i, j, kk: (kk, i)) if ta else pl.BlockSpec((tm, tk), lambda i, j, kk: (i, kk))
    b_spec = pl.BlockSpec((tn, tk), lambda i, j, kk: (j, kk)) if tb else pl.BlockSpec((tk, tn), lambda i, j, kk: (kk, j))
    in_specs, args = [a_spec, b_spec], [a, b]
    if add is not None:
        in_specs.append(pl.BlockSpec((tm, tn), lambda i, j, kk: (i, j)))
        args.append(add)
    return pl.pallas_call(
        body, name=name, grid=(m // tm, n // tn, nk), in_specs=in_specs,
        out_specs=pl.BlockSpec((tm, tn), lambda i, j, kk: (i, j)),
        out_shape=jax.ShapeDtypeStruct((m, n), out_dtype),
        scratch_shapes=[pltpu.VMEM((tm, tn), f32)] if nk > 1 else [],
        compiler_params=_cparams(("parallel", "parallel", "arbitrary")))(*args)


def _bd_nn(a, w, *, out_dtype=f32, name, ts=512):
    s = a.shape[0]
    nb, ka, no = w.shape
    ts = min(ts, s)

    def body(a_ref, w_ref, o_ref):
        o_ref[...] = jnp.dot(a_ref[...].astype(bf16), w_ref[0].astype(bf16), preferred_element_type=f32).astype(out_dtype)

    return pl.pallas_call(
        body, name=name, grid=(nb, s // ts),
        in_specs=[pl.BlockSpec((ts, ka), lambda b, i: (i, b)), pl.BlockSpec((1, ka, no), lambda b, i: (b, 0, 0))],
        out_specs=pl.BlockSpec((ts, no), lambda b, i: (i, b)),
        out_shape=jax.ShapeDtypeStruct((s, nb * no), out_dtype),
        compiler_params=_cparams(("parallel", "parallel")))(a, w)


def _bd_tn_diag(a, g, *, name, ts=512):
    s = a.shape[0]
    nb = a.shape[1] // LANES
    ts = min(ts, s)
    ni = s // ts

    def body(a_ref, g_ref, o_ref, acc):
        i = pl.program_id(1)
        part = lax.dot_general(a_ref[...].astype(bf16), g_ref[...].astype(bf16), (((0,), (0,)), ((), ())),
                               preferred_element_type=f32)

        @pl.when(i == 0)
        def _():
            acc[...] = part

        @pl.when(i > 0)
        def _():
            acc[...] += part

        @pl.when(i == ni - 1)
        def _():
            for j in range(8):
                o_ref[0, j] = acc[j * GROUP:(j + 1) * GROUP, j * LANES:(j + 1) * LANES]

    return pl.pallas_call(
        body, name=name, grid=(nb, ni),
        in_specs=[pl.BlockSpec((ts, LANES), lambda b, i: (i, b)), pl.BlockSpec((ts, 8 * LANES), lambda b, i: (i, b))],
        out_specs=pl.BlockSpec((1, 8, GROUP, LANES), lambda b, i: (b, 0, 0, 0)),
        out_shape=jax.ShapeDtypeStruct((nb, 8, GROUP, LANES), f32),
        scratch_shapes=[pltpu.VMEM((LANES, 8 * LANES), f32)],
        compiler_params=_cparams(("parallel", "arbitrary")))(a, g)


def _rms_fwd(x, g, *, col=0, name, ts=512):
    s, d = x.shape[0], g.shape[1]
    ts = min(ts, s)

    def body(x_ref, g_ref, o_ref):
        xv = x_ref[...].astype(f32)
        r = lax.rsqrt(jnp.mean(xv * xv, axis=-1, keepdims=True) + EPS)
        o_ref[...] = (xv * r * g_ref[...]).astype(bf16)

    return pl.pallas_call(
        body, name=name, grid=(s // ts,),
        in_specs=[pl.BlockSpec((ts, d), lambda i: (i, col)), pl.BlockSpec((1, d), lambda i: (0, 0))],
        out_specs=pl.BlockSpec((ts, d), lambda i: (i, 0)),
        out_shape=jax.ShapeDtypeStruct((s, d), bf16),
        compiler_params=_cparams(("parallel",)))(x, g)


def _rms_bwd(dy, x, g, *, col=0, res=None, out_dtype=f32, name, ts=512):
    s, d = dy.shape
    ts = min(ts, s)
    twin = res is not None

    def body(*refs):
        if twin:
            dy_ref, x_ref, g_ref, res_ref, dx_ref, dxb_ref, dg_ref = refs
        else:
            dy_ref, x_ref, g_ref, dx_ref, dg_ref = refs

        @pl.when(pl.program_id(0) == 0)
        def _():
            dg_ref[...] = jnp.zeros_like(dg_ref)

        xv, dyv = x_ref[...].astype(f32), dy_ref[...].astype(f32)
        r = lax.rsqrt(jnp.mean(xv * xv, axis=-1, keepdims=True) + EPS)
        dyg = dyv * g_ref[...]
        dx = r * dyg - xv * (r * r * r) * jnp.mean(xv * dyg, axis=-1, keepdims=True)
        if twin:
            dx = dx + res_ref[...]
            dxb_ref[...] = dx.astype(bf16)
        dx_ref[...] = dx.astype(out_dtype)
        dg_ref[...] += jnp.sum(dyv * xv * r, axis=0, keepdims=True)

    row, vec = pl.BlockSpec((ts, d), lambda i: (i, 0)), pl.BlockSpec((1, d), lambda i: (0, 0))
    in_specs, args = [row, pl.BlockSpec((ts, d), lambda i: (i, col)), vec], [dy, x, g]
    out_specs, out_shape = [row], [jax.ShapeDtypeStruct((s, d), out_dtype)]
    if twin:
        in_specs.append(row)
        args.append(res)
        out_specs.append(row)
        out_shape.append(jax.ShapeDtypeStruct((s, d), bf16))
    return pl.pallas_call(
        body, name=name, grid=(s // ts,), in_specs=in_specs, out_specs=out_specs + [vec],
        out_shape=out_shape + [jax.ShapeDtypeStruct((1, d), f32)],
        compiler_params=_cparams(("arbitrary",)))(*args)


def _swap_halves(z):
    lane = lax.broadcasted_iota(jnp.int32, z.shape, 1)
    return jnp.where(lane < NOPE + ROPE // 2, pltpu.roll(z, LANES - ROPE // 2, axis=1), pltpu.roll(z, ROPE // 2, axis=1))


def _rope_tables(s):
    inv_freq = 1.0 / (ROPE_THETA ** (jnp.arange(0, ROPE, 2, dtype=f32) / ROPE))
    ang = jnp.arange(s, dtype=f32)[:, None] * inv_freq[None, :]
    cos, sin = jnp.cos(ang), jnp.sin(ang)
    one, zero = jnp.ones((s, NOPE), f32), jnp.zeros((s, NOPE), f32)
    pad1, pad0 = jnp.ones((s, HEAD_PAD - QK), f32), jnp.zeros((s, HEAD_PAD - QK), f32)
    return jnp.concatenate([one, cos, cos, pad1], 1), jnp.concatenate([zero, -sin, sin, pad0], 1)


def _qk_prep_fwd(q_raw, kv_raw, proj, qg, kg, cos_t, sin_t, *, name, ts=512):
    s = q_raw.shape[0]
    ts = min(ts, s)
    rope_blk = (MIX_IN_PAD - HEAD_PAD) // HEAD_PAD

    def body(q_ref, kv_ref, kr_ref, qg_ref, kg_ref, c_ref, s_ref, qo_ref, ko_ref, vo_ref):
        lane = lax.broadcasted_iota(jnp.int32, (ts, HEAD_PAD), 1)
        cosv, sinv = c_ref[...], s_ref[...]

        def norm_rope(z, gain):
            r = lax.rsqrt(jnp.sum(z * z, axis=-1, keepdims=True) * (1.0 / QK) + EPS)
            zn = z * r * gain
            return zn * cosv + _swap_halves(zn) * sinv

        kvv = kv_ref[...]
        qo_ref[...] = norm_rope(q_ref[...], qg_ref[...]).astype(bf16)
        ko_ref[...] = norm_rope(jnp.where(lane < NOPE, kvv, kr_ref[...]), kg_ref[...]).astype(bf16)
        vo_ref[...] = jnp.where(lane >= NOPE, kvv, 0.0).astype(bf16)

    head = pl.BlockSpec((ts, HEAD_PAD), lambda i, h: (i, h))
    row = pl.BlockSpec((ts, HEAD_PAD), lambda i, h: (i, 0))
    vec = pl.BlockSpec((1, HEAD_PAD), lambda i, h: (0, 0))
    out = jax.ShapeDtypeStruct((s, HEADS * HEAD_PAD), bf16)
    return pl.pallas_call(
        body, name=name, grid=(s // ts, HEADS),
        in_specs=[head, head, pl.BlockSpec((ts, HEAD_PAD), lambda i, h: (i, rope_blk)), vec, vec, row, row],
        out_specs=[head, head, head], out_shape=[out, out, out],
        compiler_params=_cparams(("parallel", "parallel")))(q_raw, kv_raw, proj, qg, kg, cos_t, sin_t)


def _qk_prep_bwd(dq, dk, dv, q_raw, kv_raw, proj, qg, kg, cos_t, sin_t, *, name, ts=512):
    s = q_raw.shape[0]
    ts = min(ts, s)
    rope_blk = (MIX_IN_PAD - HEAD_PAD) // HEAD_PAD

    def body(dq_ref, dk_ref, dv_ref, q_ref, kv_ref, kr_ref, qg_ref, kg_ref, c_ref, s_ref,
             dqr_ref, dkvr_ref, dkr_ref, dqg_ref, dkg_ref):
        i, h = pl.program_id(0), pl.program_id(1)
        lane = lax.broadcasted_iota(jnp.int32, (ts, HEAD_PAD), 1)
        is_rope = (lane >= NOPE) & (lane < QK)
        cosv, sinv = c_ref[...], s_ref[...]

        @pl.when((i == 0) & (h == 0))
        def _():
            dqg_ref[...] = jnp.zeros_like(dqg_ref)
            dkg_ref[...] = jnp.zeros_like(dkg_ref)

        @pl.when(h == 0)
        def _():
            dkr_ref[...] = jnp.zeros_like(dkr_ref)

        def back(dout, z, gain):
            dzn = dout * cosv + jnp.where(is_rope, _swap_halves(dout * sinv), 0.0)
            r = lax.rsqrt(jnp.sum(z * z, axis=-1, keepdims=True) * (1.0 / QK) + EPS)
            dzg = dzn * gain
            dz = r * dzg - z * (r * r * r) * (jnp.sum(z * dzg, axis=-1, keepdims=True) * (1.0 / QK))
            return dz, jnp.sum(dzn * z * r, axis=0, keepdims=True)

        dqz, dqg = back(dq_ref[...].astype(f32), q_ref[...], qg_ref[...])
        dqr_ref[...] = dqz.astype(bf16)
        dqg_ref[...] += dqg
        kvv = kv_ref[...]
        dkz, dkg = back(dk_ref[...].astype(f32), jnp.where(lane < NOPE, kvv, kr_ref[...]), kg_ref[...])
        dkg_ref[...] += dkg
        dkvr_ref[...] = jnp.where(lane < NOPE, dkz, dv_ref[...].astype(f32)).astype(bf16)
        dkr_ref[...] += jnp.where(is_rope, dkz, 0.0)

    head = pl.BlockSpec((ts, HEAD_PAD), lambda i, h: (i, h))
    row = pl.BlockSpec((ts, HEAD_PAD), lambda i, h: (i, 0))
    vec = pl.BlockSpec((1, HEAD_PAD), lambda i, h: (0, 0))
    wide = jax.ShapeDtypeStruct((s, HEADS * HEAD_PAD), bf16)
    return pl.pallas_call(
        body, name=name, grid=(s // ts, HEADS),
        in_specs=[head, head, head, head, head, pl.BlockSpec((ts, HEAD_PAD), lambda i, h: (i, rope_blk)), vec, vec, row, row],
        out_specs=[head, head, row, vec, vec],
        out_shape=[wide, wide, jax.ShapeDtypeStruct((s, HEAD_PAD), f32), jax.ShapeDtypeStruct((1, HEAD_PAD), f32),
                   jax.ShapeDtypeStruct((1, HEAD_PAD), f32)],
        compiler_params=_cparams(("arbitrary", "arbitrary")))(dq, dk, dv, q_raw, kv_raw, proj, qg, kg, cos_t, sin_t)


_NT = (((1,), (1,)), ((), ()))
_SCALE = QK ** -0.5


def _flash_fwd(q, k, v, *, name, tq=512):
    s = q.shape[0]
    tq = min(tq, s)

    def body(q_ref, k_ref, v_ref, o_ref, lse_ref):
        i = pl.program_id(1)
        qv = q_ref[...]

        def step(j, carry, masked):
            m, l, acc = carry
            st = pl.multiple_of(j * tq, tq)
            kj, vj = k_ref[pl.ds(st, tq), :], v_ref[pl.ds(st, tq), :]
            sc = lax.dot_general(qv, kj, _NT, preferred_element_type=f32) * _SCALE
            if masked:
                rr = lax.broadcasted_iota(jnp.int32, (tq, tq), 0)
                cc = lax.broadcasted_iota(jnp.int32, (tq, tq), 1)
                sc = jnp.where(cc <= rr, sc, NEG)
            m_new = jnp.maximum(m, jnp.max(sc, axis=-1, keepdims=True))
            p = jnp.exp(sc - m_new)
            alpha = jnp.exp(m - m_new)
            l = alpha * l + jnp.sum(p, axis=-1, keepdims=True)
            acc = alpha * acc + jnp.dot(p.astype(bf16), vj, preferred_element_type=f32)
            return m_new, l, acc

        init = (jnp.full((tq, 1), NEG, f32), jnp.zeros((tq, 1), f32), jnp.zeros((tq, HEAD_PAD), f32))
        carry = lax.fori_loop(0, i, lambda j, c: step(j, c, False), init)
        m, l, acc = step(i, carry, True)
        o_ref[...] = (acc / l).astype(bf16)
        lse_ref[0] = m + jnp.log(l)

    blk = pl.BlockSpec((tq, HEAD_PAD), lambda h, i: (i, h))
    full = pl.BlockSpec((s, HEAD_PAD), lambda h, i: (0, h))
    return pl.pallas_call(
        body, name=name, grid=(HEADS, s // tq), in_specs=[blk, full, full],
        out_specs=[blk, pl.BlockSpec((1, tq, 1), lambda h, i: (h, i, 0))],
        out_shape=[jax.ShapeDtypeStruct((s, HEADS * HEAD_PAD), bf16), jax.ShapeDtypeStruct((HEADS, s, 1), f32)],
        compiler_params=_cparams(("parallel", "arbitrary")))(q, k, v)


def _flash_bwd_dq(q, k, v, o, do, lse, *, name, tq=512):
    s = q.shape[0]
    tq = min(tq, s)

    def body(q_ref, k_ref, v_ref, o_ref, do_ref, lse_ref, dq_ref, dl_ref):
        i = pl.program_id(1)
        qv = q_ref[...]
        dov = do_ref[...].astype(f32)
        delta = jnp.sum(dov * o_ref[...].astype(f32), axis=-1, keepdims=True)
        dob = dov.astype(bf16)
        lsev = lse_ref[0]

        def step(j, acc, masked):
            st = pl.multiple_of(j * tq, tq)
            kj, vj = k_ref[pl.ds(st, tq), :], v_ref[pl.ds(st, tq), :]
            sc = lax.dot_general(qv, kj, _NT, preferred_element_type=f32) * _SCALE
            p = jnp.exp(sc - lsev)
            if masked:
                rr = lax.broadcasted_iota(jnp.int32, (tq, tq), 0)
                cc = lax.broadcasted_iota(jnp.int32, (tq, tq), 1)
                p = jnp.where(cc <= rr, p, 0.0)
            dp = lax.dot_general(dob, vj, _NT, preferred_element_type=f32)
            ds = p * (dp - delta)
            return acc + jnp.dot(ds.astype(bf16), kj, preferred_element_type=f32)

        acc = lax.fori_loop(0, i, lambda j, c: step(j, c, False), jnp.zeros((tq, HEAD_PAD), f32))
        dq_ref[...] = step(i, acc, True) * _SCALE
        dl_ref[0] = delta

    blk = pl.BlockSpec((tq, HEAD_PAD), lambda h, i: (i, h))
    full = pl.BlockSpec((s, HEAD_PAD), lambda h, i: (0, h))
    col = pl.BlockSpec((1, tq, 1), lambda h, i: (h, i, 0))
    return pl.pallas_call(
        body, name=name, grid=(HEADS, s // tq), in_specs=[blk, full, full, blk, blk, col],
        out_specs=[blk, col],
        out_shape=[jax.ShapeDtypeStruct((s, HEADS * HEAD_PAD), f32), jax.ShapeDtypeStruct((HEADS, s, 1), f32)],
        compiler_params=_cparams(("parallel", "arbitrary")))(q, k, v, o, do, lse)


def _flash_bwd_dkv(q, k, v, do, lse_row, delta_row, *, name, tk=512):
    s = q.shape[0]
    tk = min(tk, s)
    nblk = s // tk

    def body(q_ref, k_ref, v_ref, do_ref, lse_ref, dl_ref, dk_ref, dv_ref):
        j = pl.program_id(1)
        kv_, vv = k_ref[...], v_ref[...]

        def step(i, carry, masked):
            dk, dv = carry
            st = pl.multiple_of(i * tk, tk)
            qi = q_ref[pl.ds(st, tk), :]
            doi = do_ref[pl.ds(st, tk), :].astype(bf16)
            lse_i = lse_ref[0, :, pl.ds(st, tk)]
            dl_i = dl_ref[0, :, pl.ds(st, tk)]
            st_ = lax.dot_general(kv_, qi, _NT, preferred_element_type=f32) * _SCALE
            pt = jnp.exp(st_ - lse_i)
            if masked:
                kk = lax.broadcasted_iota(jnp.int32, (tk, tk), 0)
                qq = lax.broadcasted_iota(jnp.int32, (tk, tk), 1)
                pt = jnp.where(kk <= qq, pt, 0.0)
            dv = dv + jnp.dot(pt.astype(bf16), doi, preferred_element_type=f32)
            dpt = lax.dot_general(vv, doi, _NT, preferred_element_type=f32)
            dst = pt * (dpt - dl_i)
            dk = dk + jnp.dot(dst.astype(bf16), qi, preferred_element_type=f32)
            return dk, dv

        zero = jnp.zeros((tk, HEAD_PAD), f32)
        carry = step(j, (zero, zero), True)
        dk, dv = lax.fori_loop(j + 1, nblk, lambda i, c: step(i, c, False), carry)
        dk_ref[...] = dk * _SCALE
        dv_ref[...] = dv

    blk = pl.BlockSpec((tk, HEAD_PAD), lambda h, j: (j, h))
    full = pl.BlockSpec((s, HEAD_PAD), lambda h, j: (0, h))
    rowv = pl.BlockSpec((1, 1, s), lambda h, j: (h, 0, 0))
    out = jax.ShapeDtypeStruct((s, HEADS * HEAD_PAD), f32)
    return pl.pallas_call(
        body, name=name, grid=(HEADS, nblk), in_specs=[full, blk, blk, full, rowv, rowv],
        out_specs=[blk, blk], out_shape=[out, out],
        compiler_params=_cparams(("parallel", "arbitrary")))(q, k, v, do, lse_row, delta_row)


def _shift_down(x, d):
    t = lax.broadcasted_iota(jnp.int32, x.shape, 0)
    return jnp.where(t < d, 0.0, pltpu.roll(x, d, axis=0))


def _shift_up(x, d):
    s = x.shape[0]
    t = lax.broadcasted_iota(jnp.int32, x.shape, 0)
    return jnp.where(t >= s - d, 0.0, pltpu.roll(x, s - d, axis=0))


def _taps(w_ref):
    return w_ref[0:1, :], w_ref[1:2, :], w_ref[2:3, :]


def _conv3(u, w):
    return w[0] * _shift_down(u, 2) + w[1] * _shift_down(u, 1) + w[2] * u


def _conv3_t(g, w):
    return w[2] * g + w[1] * _shift_up(g, 1) + w[0] * _shift_up(g, 2)


def _conv3_dw(dw_ref, g, u):
    dw_ref[0:1, :] = jnp.sum(g * _shift_down(u, 2), axis=0, keepdims=True)
    dw_ref[1:2, :] = jnp.sum(g * _shift_down(u, 1), axis=0, keepdims=True)
    dw_ref[2:3, :] = jnp.sum(g * u, axis=0, keepdims=True)


_GB, _GC, _CI = 512 // LANES, 1024 // LANES, 1536 // LANES


def _sconv_fwd(proj, w, *, name):
    s = proj.shape[0]

    def body(gb_ref, gc_ref, ci_ref, w_ref, o_ref):
        o_ref[...] = (gb_ref[...] * _conv3(gc_ref[...] * ci_ref[...], _taps(w_ref))).astype(bf16)

    col = lambda off: pl.BlockSpec((s, LANES), lambda j: (0, off + j))
    return pl.pallas_call(
        body, name=name, grid=(CONV_CH // LANES,),
        in_specs=[col(_GB), col(_GC), col(_CI), pl.BlockSpec((3, LANES), lambda j: (0, j))],
        out_specs=pl.BlockSpec((s, LANES), lambda j: (0, j)),
        out_shape=jax.ShapeDtypeStruct((s, CONV_CH), bf16),
        compiler_params=_cparams(("parallel",)))(proj, proj, proj, w)


def _sconv_bwd(dmix, proj, w, *, name):
    s = proj.shape[0]

    def body(do_ref, gb_ref, gc_ref, ci_ref, w_ref, dgb_ref, dgc_ref, dci_ref, dw_ref):
        wv, gc, ci, do = _taps(w_ref), gc_ref[...], ci_ref[...], do_ref[...].astype(f32)
        u = gc * ci
        dgb_ref[...] = (do * _conv3(u, wv)).astype(bf16)
        dc = do * gb_ref[...]
        du = _conv3_t(dc, wv)
        dgc_ref[...] = (du * ci).astype(bf16)
        dci_ref[...] = (du * gc).astype(bf16)
        _conv3_dw(dw_ref, dc, u)

    col = lambda off: pl.BlockSpec((s, LANES), lambda j: (0, off + j))
    out = jax.ShapeDtypeStruct((s, CONV_CH), bf16)
    return pl.pallas_call(
        body, name=name, grid=(CONV_CH // LANES,),
        in_specs=[col(HEADS), col(_GB), col(_GC), col(_CI), pl.BlockSpec((3, LANES), lambda j: (0, j))],
        out_specs=[col(0), col(0), col(0), pl.BlockSpec((3, LANES), lambda j: (0, j))],
        out_shape=[out, out, out, jax.ShapeDtypeStruct((3, CONV_CH), f32)],
        compiler_params=_cparams(("parallel",)))(dmix, proj, proj, proj, w)


def _ffn_act_fwd(zg, zv, cwg, cwv, *, name):
    s, f = zg.shape

    def body(zg_ref, zv_ref, wg_ref, wv_ref, o_ref):
        o_ref[...] = (jax.nn.silu(_conv3(zg_ref[...], _taps(wg_ref))) * _conv3(zv_ref[...], _taps(wv_ref))).astype(bf16)

    col = pl.BlockSpec((s, LANES), lambda j: (0, j))
    wsp = pl.BlockSpec((3, LANES), lambda j: (0, j))
    return pl.pallas_call(
        body, name=name, grid=(f // LANES,), in_specs=[col, col, wsp, wsp], out_specs=col,
        out_shape=jax.ShapeDtypeStruct((s, f), bf16), compiler_params=_cparams(("parallel",)))(zg, zv, cwg, cwv)


def _ffn_act_bwd(da, zg, zv, cwg, cwv, *, name):
    s, f = zg.shape

    def body(da_ref, zg_ref, zv_ref, wg_ref, wv_ref, dzg_ref, dzv_ref, dwg_ref, dwv_ref):
        wg, wv, zgv, zvv, dav = _taps(wg_ref), _taps(wv_ref), zg_ref[...], zv_ref[...], da_ref[...].astype(f32)
        ug, uv = _conv3(zgv, wg), _conv3(zvv, wv)
        sg = jax.nn.sigmoid(ug)
        dug = dav * uv * (sg * (1.0 + ug * (1.0 - sg)))
        duv = dav * (ug * sg)
        dzg_ref[...] = _conv3_t(dug, wg).astype(bf16)
        dzv_ref[...] = _conv3_t(duv, wv).astype(bf16)
        _conv3_dw(dwg_ref, dug, zgv)
        _conv3_dw(dwv_ref, duv, zvv)

    col = pl.BlockSpec((s, LANES), lambda j: (0, j))
    wsp = pl.BlockSpec((3, LANES), lambda j: (0, j))
    act, wsh = jax.ShapeDtypeStruct((s, f), bf16), jax.ShapeDtypeStruct((3, f), f32)
    return pl.pallas_call(
        body, name=name, grid=(f // LANES,), in_specs=[col, col, col, wsp, wsp], out_specs=[col, col, wsp, wsp],
        out_shape=[act, act, wsh, wsh], compiler_params=_cparams(("parallel",)))(da, zg, zv, cwg, cwv)


def _expand_mat():
    return jnp.asarray(np.kron(np.eye(STATE, dtype=np.float32), np.ones((1, GROUP), np.float32)))


def _disc_fn(lr, li, ls, br, bi, e):
    dt = jnp.exp(ls)
    mag = jnp.exp(lr * dt)
    ar, ai = mag * jnp.cos(li * dt), mag * jnp.sin(li * dt)
    nr, ni = ar - 1.0, ai
    den = lr * lr + li * li
    zr, zi = (nr * lr + ni * li) / den, (ni * lr - nr * li) / den
    zrr = jnp.dot(zr, e, precision=lax.Precision.HIGHEST, preferred_element_type=f32)
    zir = jnp.dot(zi, e, precision=lax.Precision.HIGHEST, preferred_element_type=f32)
    return ar, ai, zrr * br - zir * bi, zrr * bi + zir * br


def _disc_fwd(lr, li, ls, br, bi, *, name):
    def body(lr_ref, li_ref, ls_ref, br_ref, bi_ref, e_ref, ar_ref, ai_ref, bbr_ref, bbi_ref):
        ar, ai, bbr, bbi = _disc_fn(lr_ref[...], li_ref[...], ls_ref[...], br_ref[...], bi_ref[...], e_ref[...])
        ar_ref[...], ai_ref[...], bbr_ref[...], bbi_ref[...] = ar, ai, bbr, bbi

    sq, wide = jax.ShapeDtypeStruct((GROUPS, STATE), f32), jax.ShapeDtypeStruct((GROUPS, STATE * GROUP), f32)
    return pl.pallas_call(body, name=name, out_shape=[sq, sq, wide, wide],
                          compiler_params=_cparams())(lr, li, ls, br, bi, _expand_mat())


def _disc_bwd(lr, li, ls, br, bi, dar, dai, dbbr, dbbi, *, name):
    def body(lr_ref, li_ref, ls_ref, br_ref, bi_ref, e_ref, dar_ref, dai_ref, dbbr_ref, dbbi_ref,
             dlr_ref, dli_ref, dls_ref, dbr_ref, dbi_ref):
        ev = e_ref[...]
        _, vjp = jax.vjp(lambda a, b, c, d_, e_: _disc_fn(a, b, c, d_, e_, ev),
                         lr_ref[...], li_ref[...], ls_ref[...], br_ref[...], bi_ref[...])
        dlr, dli, dls, dbr, dbi = vjp((dar_ref[...], dai_ref[...], dbbr_ref[...], dbbi_ref[...]))
        dlr_ref[...], dli_ref[...], dls_ref[...], dbr_ref[...], dbi_ref[...] = dlr, dli, dls, dbr, dbi

    sq, wide = jax.ShapeDtypeStruct((GROUPS, STATE), f32), jax.ShapeDtypeStruct((GROUPS, STATE * GROUP), f32)
    return pl.pallas_call(body, name=name, out_shape=[sq, sq, jax.ShapeDtypeStruct((GROUPS, 1), f32), wide, wide],
                          compiler_params=_cparams())(lr, li, ls, br, bi, _expand_mat(), dar, dai, dbbr, dbbi)


SCAN_TILE = 64


def _cmul(p1, p2, x, xr):
    return p1 * x + p2 * xr


def _tile_shift(v, d, reverse):
    if d % 8:
        return _shift_up(v, d) if reverse else _shift_down(v, d)
    z = jnp.zeros((d, v.shape[1]), v.dtype)
    return jnp.concatenate([v[d:], z], axis=0) if reverse else jnp.concatenate([z, v[:v.shape[0] - d]], axis=0)


def _tile_scan(v, pows, reverse):
    d = 1
    for b1, b2 in pows:
        vs = _tile_shift(v, d, reverse)
        v = v + _cmul(b1, b2, vs, pltpu.roll(vs, STATE, axis=1))
        d *= 2
    return v


def _scan_setup(a1, a2, reverse):
    if reverse:
        a2 = -a2
    pows, b1, b2, d = [], a1, a2, 1
    while d < SCAN_TILE:
        pows.append((b1, b2))
        b1, b2, d = b1 * b1 - b2 * b2, 2.0 * b1 * b2, 2 * d
    lane = lax.broadcasted_iota(jnp.int32, (SCAN_TILE, LANES), 1)
    row = lax.broadcasted_iota(jnp.int32, (SCAN_TILE, LANES), 0)
    a_c = jnp.where(lane < STATE, a1, a2)
    pc = _tile_scan(jnp.where(row == (SCAN_TILE - 1 if reverse else 0), a_c, 0.0), pows, reverse)
    pr = pltpu.roll(pc, STATE, axis=1)
    return pows, jnp.where(lane < STATE, pc, pr), jnp.where(lane < STATE, -pr, pc)


def _carry_in(p1, p2, c):
    shape = (SCAN_TILE, LANES)
    return _cmul(p1, p2, jnp.broadcast_to(c, shape), jnp.broadcast_to(pltpu.roll(c, STATE, axis=1), shape))


def _scan_fwd(bu, a1, a2, *, name):
    s = bu.shape[0]
    nt = s // SCAN_TILE

    def body(bu_ref, a1_ref, a2_ref, x_ref):
        pows, p1, p2 = _scan_setup(a1_ref[0], a2_ref[0], False)

        def step(k, c):
            rows = pl.ds(pl.multiple_of(k * SCAN_TILE, SCAN_TILE), SCAN_TILE)
            v = _tile_scan(bu_ref[rows, :], pows, False) + _carry_in(p1, p2, c)
            x_ref[rows, :] = v
            return v[SCAN_TILE - 1:SCAN_TILE, :]

        lax.fori_loop(0, nt, step, jnp.zeros((1, LANES), f32))

    col = pl.BlockSpec((s, LANES), lambda g: (0, g))
    vec = pl.BlockSpec((1, 1, LANES), lambda g: (g, 0, 0))
    return pl.pallas_call(body, name=name, grid=(GROUPS,), in_specs=[col, vec, vec], out_specs=col,
                          out_shape=jax.ShapeDtypeStruct(bu.shape, f32), compiler_params=_cparams(("parallel",)))(bu, a1, a2)


def _scan_bwd(dx, x, a1, a2, *, name):
    s = dx.shape[0]
    nt = s // SCAN_TILE

    def fold(v):
        out = v[0:8]
        for r in range(8, SCAN_TILE, 8):
            out = out + v[r:r + 8]
        return out

    def body(dx_ref, x_ref, a1_ref, a2_ref, g_ref, t1_ref, t2_ref):
        pows, p1, p2 = _scan_setup(a1_ref[0], a2_ref[0], True)
        row = lax.broadcasted_iota(jnp.int32, (SCAN_TILE, LANES), 0)

        def step(kk, carry):
            c, acc1, acc2 = carry
            k = nt - 1 - kk
            start = pl.multiple_of(k * SCAN_TILE, SCAN_TILE)
            rows = pl.ds(start, SCAN_TILE)
            g = _tile_scan(dx_ref[rows, :], pows, True) + _carry_in(p1, p2, c)
            g_ref[rows, :] = g.astype(bf16)
            before = x_ref[pl.ds(pl.multiple_of(jnp.maximum(start - 8, 0), 8), 8), :][7:8, :]
            before = jnp.where(k > 0, before, 0.0)
            xp = jnp.where(row == 0, before, pltpu.roll(x_ref[rows, :], 1, axis=0))
            return g[0:1, :], acc1 + fold(g * xp), acc2 + fold(g * pltpu.roll(xp, STATE, axis=1))

        zero8 = jnp.zeros((8, LANES), f32)
        _, acc1, acc2 = lax.fori_loop(0, nt, step, (jnp.zeros((1, LANES), f32), zero8, zero8))
        t1_ref[0] = jnp.sum(acc1, axis=0, keepdims=True)
        t2_ref[0] = jnp.sum(acc2, axis=0, keepdims=True)

    col = pl.BlockSpec((s, LANES), lambda g: (0, g))
    vec = pl.BlockSpec((1, 1, LANES), lambda g: (g, 0, 0))
    vsh = jax.ShapeDtypeStruct((GROUPS, 1, LANES), f32)
    return pl.pallas_call(body, name=name, grid=(GROUPS,), in_specs=[col, col, vec, vec], out_specs=[col, vec, vec],
                          out_shape=[jax.ShapeDtypeStruct(dx.shape, bf16), vsh, vsh],
                          compiler_params=_cparams(("parallel",)))(dx, x, a1, a2)


_GELU_C = math.sqrt(2.0 / math.pi)


def _gelu_fwd(y, u, dsk, *, name, ts=512):
    s, d = y.shape
    ts = min(ts, s)

    def body(y_ref, u_ref, d_ref, o_ref):
        o_ref[...] = jax.nn.gelu(y_ref[...] + d_ref[...] * u_ref[...]).astype(bf16)

    row, vec = pl.BlockSpec((ts, d), lambda i: (i, 0)), pl.BlockSpec((1, d), lambda i: (0, 0))
    return pl.pallas_call(body, name=name, grid=(s // ts,), in_specs=[row, row, vec], out_specs=row,
                          out_shape=jax.ShapeDtypeStruct((s, d), bf16), compiler_params=_cparams(("parallel",)))(y, u, dsk)


def _gelu_bwd(dg, y, u, dsk, *, name, ts=512):
    s, d = y.shape
    ts = min(ts, s)

    def body(dg_ref, y_ref, u_ref, d_ref, dy_ref, du_ref, dd_ref):
        @pl.when(pl.program_id(0) == 0)
        def _():
            dd_ref[...] = jnp.zeros_like(dd_ref)

        uv, dv = u_ref[...], d_ref[...]
        z = y_ref[...] + dv * uv
        th = jnp.tanh(_GELU_C * (z + 0.044715 * z * z * z))
        dz = dg_ref[...] * (0.5 * (1.0 + th) + 0.5 * z * (1.0 - th * th) * _GELU_C * (1.0 + 3 * 0.044715 * z * z))
        dy_ref[...] = dz.astype(bf16)
        du_ref[...] = dz * dv
        dd_ref[...] += jnp.sum(dz * uv, axis=0, keepdims=True)

    row, vec = pl.BlockSpec((ts, d), lambda i: (i, 0)), pl.BlockSpec((1, d), lambda i: (0, 0))
    return pl.pallas_call(
        body, name=name, grid=(s // ts,), in_specs=[row, row, row, vec], out_specs=[row, row, vec],
        out_shape=[jax.ShapeDtypeStruct((s, d), bf16), jax.ShapeDtypeStruct((s, d), f32), jax.ShapeDtypeStruct((1, d), f32)],
        compiler_params=_cparams(("arbitrary",)))(dg, y, u, dsk)


def _glu_fwd(x, a, b, *, name, ts=512):
    s, d = x.shape
    ts = min(ts, s)

    def body(x_ref, a_ref, b_ref, o_ref):
        o_ref[...] = x_ref[...] + a_ref[...] * jax.nn.sigmoid(b_ref[...])

    row = pl.BlockSpec((ts, d), lambda i: (i, 0))
    return pl.pallas_call(body, name=name, grid=(s // ts,), in_specs=[row, row, row], out_specs=row,
                          out_shape=jax.ShapeDtypeStruct((s, d), f32), compiler_params=_cparams(("parallel",)))(x, a, b)


def _glu_bwd(dx, a, b, *, name, ts=512):
    s, d = dx.shape
    ts = min(ts, s)

    def body(dx_ref, a_ref, b_ref, da_ref, db_ref):
        sg = jax.nn.sigmoid(b_ref[...])
        dxv = dx_ref[...]
        da_ref[...] = (dxv * sg).astype(bf16)
        db_ref[...] = (dxv * a_ref[...] * sg * (1.0 - sg)).astype(bf16)

    row = pl.BlockSpec((ts, d), lambda i: (i, 0))
    out = jax.ShapeDtypeStruct((s, d), bf16)
    return pl.pallas_call(body, name=name, grid=(s // ts,), in_specs=[row, row, row], out_specs=[row, row],
                          out_shape=[out, out], compiler_params=_cparams(("parallel",)))(dx, a, b)


def _add(a, b, *, name, ts=512):
    s, d = a.shape
    ts = min(ts, s)

    def body(a_ref, b_ref, o_ref):
        o_ref[...] = (a_ref[...].astype(f32) + b_ref[...].astype(f32)).astype(bf16)

    row = pl.BlockSpec((ts, d), lambda i: (i, 0))
    return pl.pallas_call(body, name=name, grid=(s // ts,), in_specs=[row, row], out_specs=row,
                          out_shape=jax.ShapeDtypeStruct((s, d), bf16), compiler_params=_cparams(("parallel",)))(a, b)


def _loss_head(y, target, *, name, ts=512):
    s, d = y.shape
    ts = min(ts, s)

    def body(y_ref, t_ref, dy_ref, dyb_ref, l_ref):
        @pl.when(pl.program_id(0) == 0)
        def _():
            l_ref[...] = jnp.zeros_like(l_ref)

        e = y_ref[...] - t_ref[...]
        dy = e * (1.0 / d)
        dy_ref[...] = dy
        dyb_ref[...] = dy.astype(bf16)
        l_ref[...] += 0.5 * jnp.sum(jnp.mean(e * e, axis=-1, keepdims=True))

    row = pl.BlockSpec((ts, d), lambda i: (i, 0))
    return pl.pallas_call(
        body, name=name, grid=(s // ts,), in_specs=[row, row],
        out_specs=[row, row, pl.BlockSpec((8, LANES), lambda i: (0, 0))],
        out_shape=[jax.ShapeDtypeStruct((s, d), f32), jax.ShapeDtypeStruct((s, d), bf16), jax.ShapeDtypeStruct((8, LANES), f32)],
        compiler_params=_cparams(("arbitrary",)))(y, target)


def _adamw(w, g, m, v, *, name, tr=128):
    r, c = w.shape

    def body(w_ref, g_ref, m_ref, v_ref, d_ref, mo_ref, vo_ref):
        gv = g_ref[...]
        mn = ADAM_B1 * m_ref[...] + (1.0 - ADAM_B1) * gv
        vn = ADAM_B2 * v_ref[...] + (1.0 - ADAM_B2) * (gv * gv)
        m_hat = mn / (1.0 - ADAM_B1 ** ADAM_STEP)
        v_hat = vn / (1.0 - ADAM_B2 ** ADAM_STEP)
        d_ref[...] = -ADAM_LR * (m_hat / (jnp.sqrt(v_hat) + ADAM_EPS) + ADAM_WD * w_ref[...])
        mo_ref[...] = mn
        vo_ref[...] = vn

    row = pl.BlockSpec((tr, c), lambda i: (i, 0))
    out = jax.ShapeDtypeStruct((r, c), f32)
    return pl.pallas_call(body, name=name, grid=(r // tr,), in_specs=[row] * 4, out_specs=[row] * 3,
                          out_shape=[out, out, out], compiler_params=_cparams(("parallel",)))(w, g, m, v)


def _sum_slabs(land, *, name, tr=128):
    n, r, c = land.shape

    def body(l_ref, o_ref):
        acc = l_ref[0].astype(f32)
        for i in range(1, n):
            acc = acc + l_ref[i].astype(f32)
        o_ref[...] = acc

    return pl.pallas_call(body, name=name, grid=(r // tr,), in_specs=[pl.BlockSpec((n, tr, c), lambda i: (0, i, 0))],
                          out_specs=pl.BlockSpec((tr, c), lambda i: (i, 0)), out_shape=jax.ShapeDtypeStruct((r, c), f32),
                          compiler_params=_cparams(("parallel",)))(land)


def _add_slabs(a, b, *, name, tr=256):
    n, r, c = a.shape

    def body(a_ref, b_ref, o_ref):
        o_ref[...] = (a_ref[...].astype(f32) + b_ref[...].astype(f32)).astype(bf16)

    blk = pl.BlockSpec((1, tr, c), lambda j, i: (j, i, 0))
    return pl.pallas_call(body, name=name, grid=(n, r // tr), in_specs=[blk, blk], out_specs=blk,
                          out_shape=jax.ShapeDtypeStruct(a.shape, bf16), compiler_params=_cparams(("parallel", "parallel")))(a, b)


_MESH = pl.DeviceIdType.MESH
_HBM = pl.BlockSpec(memory_space=pltpu.HBM)
N_CHIP = N_DEV // 2


def _position():
    return lax.axis_index("x"), lax.axis_index("y"), lax.axis_index("c")


def _gather8(x, *, name):
    def body(x_ref, o_ref, send_sems, recv_sems, local_sem):
        xx, yy, cc = _position()
        me, sibling = (xx, yy, cc), (xx, yy, 1 - cc)
        chips = [(1 - xx, yy), (xx, 1 - yy), (1 - xx, 1 - yy)]

        def slab(px, py, pc):
            return o_ref.at[4 * px + 2 * py + pc]

        def copy(k, block, to, src=None):
            return pltpu.make_async_remote_copy(src_ref=slab(*block) if src is None else src, dst_ref=slab(*block),
                                                send_sem=send_sems.at[k], recv_sem=recv_sems.at[k], device_id=to,
                                                device_id_type=_MESH)

        mine = pltpu.make_async_copy(x_ref, slab(*me), local_sem)
        mine.start()
        first = [copy(0, me, sibling, src=x_ref)] + [copy(1 + j, me, (*chip, cc), src=x_ref) for j, chip in enumerate(chips)]
        for cp in first:
            cp.start()
        passed = [copy(4 + j, (*chip, cc), sibling) for j, chip in enumerate(chips)]
        for j, chip in enumerate(chips):
            copy(1 + j, (*chip, cc), me).wait_recv()
            passed[j].start()
        copy(0, sibling, me).wait_recv()
        for j, chip in enumerate(chips):
            copy(4 + j, (*chip, 1 - cc), me).wait_recv()
        for cp in first + passed:
            cp.wait_send()
        mine.wait()

    return pl.pallas_call(
        body, name=name, in_specs=[_HBM], out_specs=_HBM, out_shape=jax.ShapeDtypeStruct((N_DEV,) + x.shape, x.dtype),
        scratch_shapes=[pltpu.SemaphoreType.DMA((N_DEV - 1,)), pltpu.SemaphoreType.DMA((N_DEV - 1,)), pltpu.SemaphoreType.DMA],
    )(x)


def _pair_exchange(g, *, name):
    shape = (N_CHIP,) + g.shape[1:]

    def body(g_ref, own_ref, land_ref, send_sems, recv_sems, local_sems):
        xx, yy, cc = _position()
        sibling = (xx, yy, 1 - cc)
        copies = []
        for j in range(N_CHIP):
            lc = pltpu.make_async_copy(g_ref.at[2 * j + cc], own_ref.at[j], local_sems.at[j])
            rc = pltpu.make_async_remote_copy(src_ref=g_ref.at[2 * j + 1 - cc], dst_ref=land_ref.at[j], send_sem=send_sems.at[j],
                                              recv_sem=recv_sems.at[j], device_id=sibling, device_id_type=_MESH)
            lc.start()
            rc.start()
            copies.append((lc, rc))
        for lc, rc in copies:
            rc.wait_recv()
        for lc, rc in copies:
            rc.wait_send()
            lc.wait()

    sems = pltpu.SemaphoreType.DMA((N_CHIP,))
    return pl.pallas_call(
        body, name=name, in_specs=[_HBM], out_specs=[_HBM, _HBM],
        out_shape=[jax.ShapeDtypeStruct(shape, g.dtype), jax.ShapeDtypeStruct(shape, g.dtype)],
        scratch_shapes=[sems, sems, sems])(g)


def _cross_exchange(p, *, name):
    def body(p_ref, o_ref, send_sems, recv_sems, local_sem):
        xx, yy, cc = _position()
        my_chip = 2 * xx + yy
        local = pltpu.make_async_copy(p_ref.at[my_chip], o_ref.at[my_chip], local_sem)
        local.start()
        chips = [(1 - xx, yy), (xx, 1 - yy), (1 - xx, 1 - yy)]
        sends = []
        for k, (px, py) in enumerate(chips):
            cp = pltpu.make_async_remote_copy(src_ref=p_ref.at[2 * px + py], dst_ref=o_ref.at[my_chip], send_sem=send_sems.at[k],
                                              recv_sem=recv_sems.at[k], device_id=(px, py, cc), device_id_type=_MESH)
            cp.start()
            sends.append(cp)
        for k, (px, py) in enumerate(chips):
            pltpu.make_async_remote_copy(src_ref=p_ref.at[2 * px + py], dst_ref=o_ref.at[2 * px + py], send_sem=send_sems.at[k],
                                         recv_sem=recv_sems.at[k], device_id=(px, py, cc), device_id_type=_MESH).wait_recv()
        for cp in sends:
            cp.wait_send()
        local.wait()

    sems = pltpu.SemaphoreType.DMA((N_CHIP - 1,))
    return pl.pallas_call(body, name=name, in_specs=[_HBM], out_specs=_HBM, out_shape=jax.ShapeDtypeStruct(p.shape, p.dtype),
                          scratch_shapes=[sems, sems, pltpu.SemaphoreType.DMA])(p)


def _all_sum(x, *, name):
    return _sum_slabs(_gather8(x, name=f"gather_{name}"), name=f"sum_{name}", tr=min(128, x.shape[0]))


def _pack_slabs(parts, rows, axis=0):
    lead = parts[0].shape[:axis]
    slabs = [p.reshape(lead + (-1, D)) for p in parts]
    used = sum(sl.shape[axis] for sl in slabs)
    return jnp.concatenate(slabs + [jnp.zeros(lead + (rows - used, D), slabs[0].dtype)], axis=axis)


def _unpack_slabs(slab, shapes):
    lead, out, off = slab.shape[:-2], [], 0
    for shp in shapes:
        n = int(np.prod(shp)) // D
        out.append(slab[..., off:off + n, :].reshape(lead + tuple(shp)))
        off += n
    return out


def _pack_rows(parts, rows):
    flat = jnp.concatenate([p.reshape(-1) for p in parts])
    return jnp.pad(flat, (0, rows * D - flat.shape[0])).reshape(rows, D)


def _unpack_rows(slab, shapes):
    flat, out, off = slab.reshape(-1), [], 0
    for shp in shapes:
        n = int(np.prod(shp))
        out.append(flat[off:off + n].reshape(shp))
        off += n
    return out


def _full_shape(shard, axis):
    return tuple(d * N_DEV if i == axis else d for i, d in enumerate(shard))


def _gather_full(shards, shard, axis):
    return jnp.moveaxis(shards, 0, axis).reshape(_full_shape(shard, axis))


def _split_full(full, shard, axis):
    shp = shard[:axis] + (N_DEV, shard[axis]) + shard[axis + 1:]
    return jnp.moveaxis(full.reshape(shp), axis, 0)


def _row(v):
    return v.reshape(1, -1).astype(f32)


def _pad_gain(g):
    return jnp.pad(g.astype(f32), (0, HEAD_PAD - QK)).reshape(1, HEAD_PAD)


def _ffn_fwd(x, p, tag):
    h = _rms_fwd(x, p["norm"], name=f"ffn_norm_{tag}")
    zg = _mm(h, p["wg"], name=f"ffn_up_g_{tag}")
    zv = _mm(h, p["wv"], name=f"ffn_up_v_{tag}")
    a = _ffn_act_fwd(zg, zv, p["cwg"], p["cwv"], name=f"ffn_act_{tag}")
    y = _mm(a, p["wd"], add=x, name=f"ffn_down_{tag}")
    return y, (x, h, zg, zv, a)


def _ffn_bwd(dy, dyb, p, saved, tag):
    x, h, zg, zv, a = saved
    g = {}
    da = _mm(dyb, p["wd"], tb=True, out_dtype=bf16, name=f"ffn_down_dx_{tag}")
    g["wd"] = _mm(a, dyb, ta=True, name=f"ffn_down_dw_{tag}")
    dzg, dzv, g["cwg"], g["cwv"] = _ffn_act_bwd(da, zg, zv, p["cwg"], p["cwv"], name=f"ffn_act_bwd_{tag}")
    g["wg"] = _mm(h, dzg, ta=True, name=f"ffn_up_g_dw_{tag}")
    g["wv"] = _mm(h, dzv, ta=True, name=f"ffn_up_v_dw_{tag}")
    dh = _mm(dzg, p["wg"], tb=True, name=f"ffn_up_g_dx_{tag}")
    dh = _mm(dzv, p["wv"], tb=True, add=dh, name=f"ffn_up_v_dx_{tag}")
    dx, dxb, g["norm"] = _rms_bwd(dh, x, p["norm"], res=dy, name=f"ffn_norm_bwd_{tag}")
    return dx, dxb, g


def _mla_fwd(x, p, tabs, tag):
    cos_t, sin_t = tabs
    h = _rms_fwd(x, p["norm"], name=f"attn_norm_{tag}")
    proj = _mm(h, p["w_in"], name=f"mix_in_{tag}")
    cqn = _rms_fwd(proj, p["cq_norm"], col=0, name=f"cq_norm_{tag}")
    ckvn = _rms_fwd(proj, p["ckv_norm"], col=1, name=f"ckv_norm_{tag}")
    q_raw = _mm(cqn, p["w_uq"], name=f"uq_{tag}")
    kv_raw = _mm(ckvn, p["w_ukv"], name=f"ukv_{tag}")
    q, k, v = _qk_prep_fwd(q_raw, kv_raw, proj, p["q_gain"], p["k_gain"], cos_t, sin_t, name=f"qk_prep_{tag}")
    o, lse = _flash_fwd(q, k, v, name=f"flash_fwd_{tag}")
    conv = _sconv_fwd(proj, p["sconv_w"], name=f"sconv_{tag}")
    mix = jnp.concatenate([o, conv], axis=1)
    y = _mm(mix, p["w_out"], add=x, name=f"mix_out_{tag}")
    return y, (x, h, proj, cqn, ckvn, q_raw, kv_raw, q, k, v, o, lse, mix)


def _mla_bwd(dy, dyb, p, tabs, saved, tag):
    cos_t, sin_t = tabs
    x, h, proj, cqn, ckvn, q_raw, kv_raw, q, k, v, o, lse, mix = saved
    s = x.shape[0]
    g = {}
    dmix = _mm(dyb, p["w_out"], tb=True, name=f"mix_out_dx_{tag}")
    g["w_out"] = _mm(mix, dyb, ta=True, name=f"mix_out_dw_{tag}")
    dgb, dgc, dci, g["sconv_w"] = _sconv_bwd(dmix, proj, p["sconv_w"], name=f"sconv_bwd_{tag}")
    dq, delta = _flash_bwd_dq(q, k, v, o, dmix, lse, name=f"flash_dq_{tag}")
    dk, dv = _flash_bwd_dkv(q, k, v, dmix, lse.reshape(HEADS, 1, s), delta.reshape(HEADS, 1, s), name=f"flash_dkv_{tag}")
    dq_raw, dkv_raw, dkr, g["q_gain"], g["k_gain"] = _qk_prep_bwd(
        dq, dk, dv, q_raw, kv_raw, proj, p["q_gain"], p["k_gain"], cos_t, sin_t, name=f"qk_prep_bwd_{tag}")
    dcqn = _mm(dq_raw, p["w_uq"], tb=True, name=f"uq_dx_{tag}")
    g["w_uq"] = _mm(cqn, dq_raw, ta=True, name=f"uq_dw_{tag}")
    dckvn = _mm(dkv_raw, p["w_ukv"], tb=True, name=f"ukv_dx_{tag}")
    g["w_ukv"] = _mm(ckvn, dkv_raw, ta=True, name=f"ukv_dw_{tag}")
    dcq, g["cq_norm"] = _rms_bwd(dcqn, proj, p["cq_norm"], col=0, out_dtype=bf16, name=f"cq_norm_bwd_{tag}")
    dckv, g["ckv_norm"] = _rms_bwd(dckvn, proj, p["ckv_norm"], col=1, out_dtype=bf16, name=f"ckv_norm_bwd_{tag}")
    dproj = jnp.concatenate([dcq, dckv, dgb, dgc, dci, dkr.astype(bf16)], axis=1)
    dh = _mm(dproj, p["w_in"], tb=True, name=f"mix_in_dx_{tag}")
    g["w_in"] = _mm(h, dproj, ta=True, name=f"mix_in_dw_{tag}")
    dx, dxb, g["norm"] = _rms_bwd(dh, x, p["norm"], res=dy, name=f"attn_norm_bwd_{tag}")
    return dx, dxb, g


def _block_diag(wg):
    nb, ng, r, c = wg.shape
    eye = jnp.eye(ng, dtype=wg.dtype)
    return (wg[:, :, :, None, :] * eye[None, :, None, :, None]).reshape(nb, ng * r, ng * c)


def _s5_mats(bbr, bbi, c_re, c_im):
    nb = GROUPS // 8
    b4 = jnp.stack([bbr.reshape(GROUPS, STATE, GROUP), bbi.reshape(GROUPS, STATE, GROUP)], axis=1)
    wg = jnp.transpose(b4, (0, 3, 1, 2)).reshape(nb, 8, GROUP, 2 * STATE)
    cg = jnp.stack([c_re, -c_im], axis=1)
    cg = jnp.transpose(cg, (0, 1, 3, 2)).reshape(nb, 8, 2 * STATE, GROUP)
    return _block_diag(wg), _block_diag(cg)


def _s5_fwd(x, p, tag):
    h = _rms_fwd(x, p["norm"], name=f"ssm_norm_{tag}")
    u = _mm(h, p["w_in"], name=f"ssm_in_{tag}")
    ar, ai, bbr, bbi = _disc_fwd(p["lr"], p["li"], p["ls"], p["br"], p["bi"], name=f"disc_{tag}")
    wb, cb = _s5_mats(bbr, bbi, p["c_re"], p["c_im"])
    a1 = jnp.concatenate([ar, ar], axis=1).reshape(GROUPS, 1, LANES)
    a2 = jnp.concatenate([-ai, ai], axis=1).reshape(GROUPS, 1, LANES)
    bu = _bd_nn(u, wb.astype(bf16), name=f"ssm_bu_{tag}")
    xs = _scan_fwd(bu, a1, a2, name=f"ssm_scan_{tag}")
    y = _bd_nn(xs, cb.astype(bf16), name=f"ssm_y_{tag}")
    g = _gelu_fwd(y, u, p["d_skip"], name=f"ssm_gelu_{tag}")
    a = _mm(g, p["wga"], name=f"glu_a_{tag}")
    b = _mm(g, p["wgb"], name=f"glu_b_{tag}")
    out = _glu_fwd(x, a, b, name=f"glu_{tag}")
    return out, (x, h, u, wb, cb, a1, a2, xs, y, g, a, b)


def _s5_bwd(dout, p, saved, tag):
    x, h, u, wb, cb, a1, a2, xs, y, g, a, b = saved
    gr = {}
    da, db = _glu_bwd(dout, a, b, name=f"glu_bwd_{tag}")
    dg = _mm(da, p["wga"], tb=True, name=f"glu_a_dx_{tag}")
    dg = _mm(db, p["wgb"], tb=True, add=dg, name=f"glu_b_dx_{tag}")
    gr["wga"] = _mm(g, da, ta=True, name=f"glu_a_dw_{tag}")
    gr["wgb"] = _mm(g, db, ta=True, name=f"glu_b_dw_{tag}")
    dy, du1, gr["d_skip"] = _gelu_bwd(dg, y, u, p["d_skip"], name=f"ssm_gelu_bwd_{tag}")
    dxs = _bd_nn(dy, jnp.swapaxes(cb, 1, 2).astype(bf16), name=f"ssm_y_dx_{tag}")
    dct = _bd_tn_diag(dy, xs, name=f"ssm_y_dw_{tag}").reshape(GROUPS, GROUP, 2 * STATE)
    gs, t1, t2 = _scan_bwd(dxs, xs, a1, a2, name=f"ssm_scan_bwd_{tag}")
    du2 = _bd_nn(gs, jnp.swapaxes(wb, 1, 2).astype(bf16), name=f"ssm_bu_dx_{tag}")
    dwg = _bd_tn_diag(u, gs, name=f"ssm_bu_dw_{tag}").reshape(GROUPS, GROUP, 2, STATE)
    du = _add(du1, du2, name=f"ssm_du_{tag}")
    dh = _mm(du, p["w_in"], tb=True, name=f"ssm_in_dx_{tag}")
    gr["w_in"] = _mm(h, du, ta=True, name=f"ssm_in_dw_{tag}")
    dx, dxb, gr["norm"] = _rms_bwd(dh, x, p["norm"], res=dout, name=f"ssm_norm_bwd_{tag}")
    t1, t2 = t1.reshape(GROUPS, LANES), t2.reshape(GROUPS, LANES)
    dar = t1[:, :STATE] + t1[:, STATE:]
    dai = t2[:, STATE:] - t2[:, :STATE]
    dbb = jnp.transpose(dwg, (2, 0, 3, 1)).reshape(2, GROUPS, STATE * GROUP)
    gr["c_re"] = dct[:, :, :STATE]
    gr["c_im"] = -dct[:, :, STATE:]
    dlr, dli, dls, dbr, dbi = _disc_bwd(p["lr"], p["li"], p["ls"], p["br"], p["bi"], dar, dai, dbb[0], dbb[1],
                                        name=f"disc_bwd_{tag}")
    gr["lr"], gr["li"], gr["ls"] = dlr, dli, dls.reshape(GROUPS)
    gr["br"], gr["bi"] = dbr.reshape(GROUPS, STATE, GROUP), dbi.reshape(GROUPS, STATE, GROUP)
    return dx, dxb, gr


def _mix_in_pad(w):
    z = lambda n: jnp.zeros((w.shape[0], n), w.dtype)
    return jnp.concatenate([w[:, :512], w[:, 544:2080], z(NOPE), w[:, 512:544], z(HEAD_PAD - QK)], axis=1)


def _mix_in_unpad(g):
    return jnp.concatenate([g[:, :512], g[:, 2048 + NOPE:2048 + QK], g[:, 512:2048]], axis=1)


def _uq_pad(w):
    return jnp.pad(w.reshape(LORA, HEADS, QK), ((0, 0), (0, 0), (0, HEAD_PAD - QK))).reshape(LORA, HEADS * HEAD_PAD)


def _uq_unpad(g):
    return g.reshape(LORA, HEADS, HEAD_PAD)[:, :, :QK].reshape(LORA, HEADS * QK)


def _mix_out_pad(w):
    att = jnp.pad(w[:512].reshape(HEADS, NOPE, D), ((0, 0), (NOPE, 0), (0, 0))).reshape(HEADS * HEAD_PAD, D)
    return jnp.concatenate([att, w[512:]], axis=0)


def _mix_out_unpad(g):
    att = g[:HEADS * HEAD_PAD].reshape(HEADS, HEAD_PAD, D)[:, NOPE:, :].reshape(HEADS * NOPE, D)
    return jnp.concatenate([att, g[HEADS * HEAD_PAD:]], axis=0)


def _layer_params(w, layer):
    i = layer // 2
    ffn = dict(norm=_row(w["ffn_norm"][layer]), wg=w["ffn_w_up"][layer][:, :FFN_H], wv=w["ffn_w_up"][layer][:, FFN_H:],
               cwg=w["ffn_conv_w"][layer][:, :FFN_H], cwv=w["ffn_conv_w"][layer][:, FFN_H:], wd=w["ffn_w_down"][layer])
    if layer % 2 == 0:
        mixer = dict(norm=_row(w["attn_norm"][i]), w_in=_mix_in_pad(w["mix_w_in"][i]), cq_norm=_row(w["cq_norm"][i]),
                     ckv_norm=_row(w["ckv_norm"][i]), w_uq=_uq_pad(w["w_uq"][i]), w_ukv=w["w_ukv"][i],
                     q_gain=_pad_gain(w["q_gain"][i]), k_gain=_pad_gain(w["k_gain"][i]), sconv_w=w["sconv_w"][i],
                     w_out=_mix_out_pad(w["mix_w_out"][i]))
    else:
        mixer = dict(norm=_row(w["ssm_norm"][i]), w_in=w["ssm_w_in"][i], lr=w["lambda_re"][i], li=w["lambda_im"][i],
                     ls=w["log_step"][i].reshape(GROUPS, 1), br=w["b_re"][i].reshape(GROUPS, STATE * GROUP),
                     bi=w["b_im"][i].reshape(GROUPS, STATE * GROUP), c_re=w["c_re"][i], c_im=w["c_im"][i],
                     d_skip=_row(w["d_skip"][i]), wga=w["w_glu"][i][:, :D], wgb=w["w_glu"][i][:, D:])
    return mixer, ffn


def _collect_grads(gm, gf):
    st = lambda xs: jnp.stack(xs, axis=0)
    ev, od = (0, 2), (1, 3)
    out = {
        "attn_norm": st([gm[l]["norm"].reshape(D) for l in ev]),
        "mix_w_in": st([_mix_in_unpad(gm[l]["w_in"]) for l in ev]),
        "cq_norm": st([gm[l]["cq_norm"].reshape(LORA) for l in ev]),
        "ckv_norm": st([gm[l]["ckv_norm"].reshape(LORA) for l in ev]),
        "w_uq": st([_uq_unpad(gm[l]["w_uq"]) for l in ev]),
        "w_ukv": st([gm[l]["w_ukv"] for l in ev]),
        "q_gain": st([gm[l]["q_gain"].reshape(HEAD_PAD)[:QK] for l in ev]),
        "k_gain": st([gm[l]["k_gain"].reshape(HEAD_PAD)[:QK] for l in ev]),
        "sconv_w": st([gm[l]["sconv_w"] for l in ev]),
        "mix_w_out": st([_mix_out_unpad(gm[l]["w_out"]) for l in ev]),
        "ssm_norm": st([gm[l]["norm"].reshape(D) for l in od]),
        "ssm_w_in": st([gm[l]["w_in"] for l in od]),
        "lambda_re": st([gm[l]["lr"] for l in od]), "lambda_im": st([gm[l]["li"] for l in od]),
        "log_step": st([gm[l]["ls"] for l in od]),
        "b_re": st([gm[l]["br"] for l in od]), "b_im": st([gm[l]["bi"] for l in od]),
        "c_re": st([gm[l]["c_re"] for l in od]), "c_im": st([gm[l]["c_im"] for l in od]),
        "d_skip": st([gm[l]["d_skip"].reshape(D) for l in od]),
        "w_glu": st([jnp.concatenate([gm[l]["wga"], gm[l]["wgb"]], axis=1) for l in od]),
        "ffn_norm": st([gf[l]["norm"].reshape(D) for l in range(4)]),
        "ffn_w_up": st([jnp.concatenate([gf[l]["wg"], gf[l]["wv"]], axis=1) for l in range(4)]),
        "ffn_conv_w": st([jnp.concatenate([gf[l]["cwg"], gf[l]["cwv"]], axis=1) for l in range(4)]),
        "ffn_w_down": st([gf[l]["wd"] for l in range(4)]),
    }
    return out


def _local_step(x, target, w):
    s = x.shape[0]
    tabs = _rope_tables(s)
    saved, params = [], []
    for layer in range(4):
        mixer, ffn = _layer_params(w, layer)
        params.append((mixer, ffn))
        if layer % 2 == 0:
            x, sm = _mla_fwd(x, mixer, tabs, f"l{layer}")
        else:
            x, sm = _s5_fwd(x, mixer, f"l{layer}")
        x, sf = _ffn_fwd(x, ffn, f"l{layer}")
        saved.append((sm, sf))
    dx, dxb, loss = _loss_head(x, target, name="loss_head")
    gm, gf = [None] * 4, [None] * 4
    for layer in reversed(range(4)):
        mixer, ffn = params[layer]
        sm, sf = saved[layer]
        dx, dxb, gf[layer] = _ffn_bwd(dx, dxb, ffn, sf, f"l{layer}")
        if layer % 2 == 0:
            dx, dxb, gm[layer] = _mla_bwd(dx, dxb, mixer, tabs, sm, f"l{layer}")
        else:
            dx, dxb, gm[layer] = _s5_bwd(dx, mixer, sm, f"l{layer}")
    return loss, dx, _collect_grads(gm, gf)


def kernel(x, attn_norm, mix_w_in, cq_norm, ckv_norm, w_uq, w_ukv, q_gain, k_gain, sconv_w, mix_w_out, ssm_norm, ssm_w_in, lambda_re, lambda_im, log_step, b_re, b_im, c_re, c_im, d_skip, w_glu, ffn_norm, ffn_w_up, ffn_conv_w, ffn_w_down, loss_target, m_attn_norm, m_mix_w_in, m_cq_norm, m_ckv_norm, m_w_uq, m_w_ukv, m_q_gain, m_k_gain, m_sconv_w, m_mix_w_out, m_ssm_norm, m_ssm_w_in, m_lambda_re, m_lambda_im, m_log_step, m_b_re, m_b_im, m_c_re, m_c_im, m_d_skip, m_w_glu, m_ffn_norm, m_ffn_w_up, m_ffn_conv_w, m_ffn_w_down, v_attn_norm, v_mix_w_in, v_cq_norm, v_ckv_norm, v_w_uq, v_w_ukv, v_q_gain, v_k_gain, v_sconv_w, v_mix_w_out, v_ssm_norm, v_ssm_w_in, v_lambda_re, v_lambda_im, v_log_step, v_b_re, v_b_im, v_c_re, v_c_im, v_d_skip, v_w_glu, v_ffn_norm, v_ffn_w_up, v_ffn_conv_w, v_ffn_w_down):
    args = dict(locals())
    wsh = {n: args[n] for n in WEIGHTS}
    msh = {n: args["m_" + n] for n in WEIGHTS}
    vsh = {n: args["v_" + n] for n in WEIGHTS}
    me = 4 * lax.axis_index("x") + 2 * lax.axis_index("y") + lax.axis_index("c")
    big_names, big_shapes = [n for n, _, _ in BIG], [sh for _, sh, _ in BIG]
    small_names = [n for n, _ in REPL] + [n for n, _, _ in SMALL]

    big_all = _gather8(_pack_slabs([wsh[n].astype(bf16) for n in big_names], BIG_ROWS), name="gather_weights")
    w = {n: _gather_full(val, shard, axis) for (n, shard, axis), val in zip(BIG, _unpack_slabs(big_all, big_shapes))}
    placed = []
    for n, shard, axis in SMALL:
        start = [0] * len(shard)
        start[axis] = me * shard[axis]
        placed.append(lax.dynamic_update_slice(jnp.zeros(_full_shape(shard, axis), f32), wsh[n], start))
    small_all = _all_sum(_pack_rows(placed, SMALL_FWD_ROWS), name="small_params")
    for (n, shard, axis), full in zip(SMALL, _unpack_rows(small_all, [_full_shape(sh, ax) for _, sh, ax in SMALL])):
        w[n] = full
    for n, _ in REPL:
        w[n] = wsh[n]

    loss8, grad_x, grads = _local_step(x[0], loss_target[0], w)

    contrib = _pack_slabs([_split_full(grads[n].astype(bf16), shard, axis) for n, shard, axis in BIG], BIG_ROWS, axis=1)
    own, theirs = _pair_exchange(contrib, name="grads_pair_exchange")
    chip_sum = _add_slabs(own, theirs, name="grads_pair_sum")
    g_big = _sum_slabs(_cross_exchange(chip_sum, name="grads_cross_exchange"), name="grads_chip_sum")
    small_vec = _pack_rows([grads[n] for n, _ in REPL] + [grads[n] for n, _, _ in SMALL] + [loss8[0, :1]], SMALL_ROWS)
    small_sum = _all_sum(small_vec, name="small_grads")
    parts = _unpack_rows(small_sum, [sh for _, sh in REPL] + [_full_shape(sh, ax) for _, sh, ax in SMALL] + [(1,)])
    g = {n: val for (n, _), val in zip(REPL, parts)}
    for (n, shard, axis), val in zip(SMALL, parts[len(REPL):]):
        start = [0] * len(shard)
        start[axis] = me * shard[axis]
        g[n] = lax.dynamic_slice(val, start, shard)
    loss = parts[-1].reshape(())
    g.update(zip(big_names, _unpack_slabs(g_big, big_shapes)))

    delta, new_m, new_v = {}, {}, {}
    big_state = [_pack_slabs([src[n] for n in big_names], BIG_ROWS) for src in (wsh, msh, vsh)]
    for dst, slab in zip((delta, new_m, new_v), _adamw(big_state[0], g_big, big_state[1], big_state[2], name="adamw_big")):
        dst.update(zip(big_names, _unpack_slabs(slab, big_shapes)))
    small_state = [_pack_rows([src[n] for n in small_names], SMALL_ROWS) for src in (wsh, g, msh, vsh)]
    for dst, slab in zip((delta, new_m, new_v), _adamw(*small_state, name="adamw_small")):
        dst.update(zip(small_names, _unpack_rows(slab, [wsh[n].shape for n in small_names])))

    return (loss, grad_x[None], *[g[n] for n in WEIGHTS], *[delta[n] for n in WEIGHTS],
            *[new_m[n] for n in WEIGHTS], *[new_v[n] for n in WEIGHTS])
```

```python
import math

import numpy as np
import jax
import jax.numpy as jnp
from jax import lax
from jax.experimental import pallas as pl
from jax.experimental.pallas import tpu as pltpu

f32, bf16 = jnp.float32, jnp.bfloat16

N_DEV = 8
D = 1024
HEADS = 8
NOPE, ROPE, QK = 64, 32, 96
HEAD_PAD = 128
LORA = 256
CONV_CH = 512
MIX_IN_PAD = 2176
FFN_H = 2816
GROUPS, GROUP, STATE = 64, 16, 64
EPS = 1e-6
ROPE_THETA = 10000.0
ADAM_LR, ADAM_B1, ADAM_B2, ADAM_EPS, ADAM_WD, ADAM_STEP = 0.001, 0.9, 0.999, 1e-08, 0.01, 10
LANES = 128
PAIR_LANES = 2 * LANES
VMEM_LIMIT = 56 << 20
MM_VMEM_BUDGET = 40 << 20
NEG = -1e30

BIG = (
    ("mix_w_in", (2, 1024, 260), 2), ("w_uq", (2, 256, 96), 2), ("w_ukv", (2, 256, 128), 2),
    ("mix_w_out", (2, 128, 1024), 1), ("ssm_w_in", (2, 128, 1024), 1), ("w_glu", (2, 1024, 256), 2),
    ("ffn_w_up", (4, 1024, 704), 2), ("ffn_w_down", (4, 352, 1024), 1))
REPL = (("attn_norm", (2, 1024)), ("cq_norm", (2, 256)), ("ckv_norm", (2, 256)), ("q_gain", (2, 96)),
        ("k_gain", (2, 96)), ("lambda_re", (2, 64, 64)), ("lambda_im", (2, 64, 64)), ("log_step", (2, 64)),
        ("b_re", (2, 64, 64, 16)), ("b_im", (2, 64, 64, 16)), ("c_re", (2, 64, 16, 64)), ("c_im", (2, 64, 16, 64)),
        ("ffn_norm", (4, 1024)))
SMALL = (("sconv_w", (2, 3, 64), 2), ("ssm_norm", (2, 128), 1), ("d_skip", (2, 128), 1), ("ffn_conv_w", (4, 3, 704), 2))
WEIGHTS = ['attn_norm', 'mix_w_in', 'cq_norm', 'ckv_norm', 'w_uq', 'w_ukv', 'q_gain', 'k_gain', 'sconv_w', 'mix_w_out',
           'ssm_norm', 'ssm_w_in', 'lambda_re', 'lambda_im', 'log_step', 'b_re', 'b_im', 'c_re', 'c_im', 'd_skip',
           'w_glu', 'ffn_norm', 'ffn_w_up', 'ffn_conv_w', 'ffn_w_down']
BIG_ROWS = 5888
SMALL_FWD_ROWS = 80
SMALL_ROWS = 640


def _cparams(sem=None, **kw):
    return pltpu.CompilerParams(dimension_semantics=sem, vmem_limit_bytes=VMEM_LIMIT, **kw)


def _tile(n, target):
    best = 0
    for t in range(LANES, min(n, target) + 1, LANES):
        if n % t == 0:
            best = t
    return best if best else n


def _mm(a, b, *, ta=False, tb=False, out_dtype=f32, add=None, name, tm=1024, tn=1536):
    m, k = (a.shape[1], a.shape[0]) if ta else a.shape
    n = b.shape[0] if tb else b.shape[1]
    assert (b.shape[1] if tb else b.shape[0]) == k
    tm = _tile(m, tm)
    tn_ = _tile(n, tn)
    tn = n if (tn_ < 256 and n <= 2304) else tn_

    def vmem_bytes(t):
        io = 2 * (tm * t * a.dtype.itemsize + t * tn * b.dtype.itemsize + tm * tn * jnp.dtype(out_dtype).itemsize)
        return io + (2 * tm * tn * 4 if add is not None else 0) + (tm * tn * 4 if t < k else 0)

    tk = next((t for t in [k] + [t for t in range(k - LANES, 0, -LANES) if k % t == 0] if vmem_bytes(t) <= MM_VMEM_BUDGET), LANES)
    nk = k // tk
    dn = (((0 if ta else 1,), (1 if tb else 0,)), ((), ()))

    def body(*refs):
        a_ref, b_ref = refs[:2]
        add_ref = refs[2] if add is not None else None
        o_ref = refs[3] if add is not None else refs[2]
        part = lax.dot_general(a_ref[...].astype(bf16), b_ref[...].astype(bf16), dn, preferred_element_type=f32)

        def finish(r):
            if add is not None:
                r = r + add_ref[...].astype(f32)
            o_ref[...] = r.astype(out_dtype)

        if nk == 1:
            finish(part)
            return
        acc = refs[-1]
        kk = pl.program_id(2)

        @pl.when(kk == 0)
        def _():
            acc[...] = part

        @pl.when(kk > 0)
        def _():
            acc[...] += part

        @pl.when(kk == nk - 1)
        def _():
            finish(acc[...])

    a_spec = pl.BlockSpec((tk, tm), lambda i, j, kk: (kk, i)) if ta else pl.BlockSpec((tm, tk), lambda i, j, kk: (i, kk))
    b_spec = pl.BlockSpec((tn, tk), lambda i, j, kk: (j, kk)) if tb else pl.BlockSpec((tk, tn), lambda i, j, kk: (kk, j))
    in_specs, args = [a_spec, b_spec], [a, b]
    if add is not None:
        in_specs.append(pl.BlockSpec((tm, tn), lambda i, j, kk: (i, j)))
        args.append(add)
    return pl.pallas_call(
        body, name=name, grid=(m // tm, n // tn, nk), in_specs=in_specs,
        out_specs=pl.BlockSpec((tm, tn), lambda i, j, kk: (i, j)),
        out_shape=jax.ShapeDtypeStruct((m, n), out_dtype),
        scratch_shapes=[pltpu.VMEM((tm, tn), f32)] if nk > 1 else [],
        compiler_params=_cparams(("parallel", "parallel", "arbitrary")))(*args)


def _bd_nn(a, w, *, out_dtype=f32, name, ts=512):
    s = a.shape[0]
    nb, ka, no = w.shape
    ts = min(ts, s)

    def body(a_ref, w_ref, o_ref):
        o_ref[...] = jnp.dot(a_ref[...].astype(bf16), w_ref[0].astype(bf16), preferred_element_type=f32).astype(out_dtype)

    return pl.pallas_call(
        body, name=name, grid=(nb, s // ts),
        in_specs=[pl.BlockSpec((ts, ka), lambda b, i: (i, b)), pl.BlockSpec((1, ka, no), lambda b, i: (b, 0, 0))],
        out_specs=pl.BlockSpec((ts, no), lambda b, i: (i, b)),
        out_shape=jax.ShapeDtypeStruct((s, nb * no), out_dtype),
        compiler_params=_cparams(("parallel", "parallel")))(a, w)


def _bd_tn_diag(a, g, *, name, ts=512):
    s = a.shape[0]
    nb = a.shape[1] // LANES
    ts = min(ts, s)
    ni = s // ts

    def body(a_ref, g_ref, o_ref, acc):
        i = pl.program_id(1)
        part = lax.dot_general(a_ref[...].astype(bf16), g_ref[...].astype(bf16), (((0,), (0,)), ((), ())),
                               preferred_element_type=f32)

        @pl.when(i == 0)
        def _():
            acc[...] = part

        @pl.when(i > 0)
        def _():
            acc[...] += part

        @pl.when(i == ni - 1)
        def _():
            for j in range(8):
                o_ref[0, j] = acc[j * GROUP:(j + 1) * GROUP, (j // 2) * PAIR_LANES:(j // 2 + 1) * PAIR_LANES]

    return pl.pallas_call(
        body, name=name, grid=(nb, ni),
        in_specs=[pl.BlockSpec((ts, LANES), lambda b, i: (i, b)), pl.BlockSpec((ts, 8 * LANES), lambda b, i: (i, b))],
        out_specs=pl.BlockSpec((1, 8, GROUP, PAIR_LANES), lambda b, i: (b, 0, 0, 0)),
        out_shape=jax.ShapeDtypeStruct((nb, 8, GROUP, PAIR_LANES), f32),
        scratch_shapes=[pltpu.VMEM((LANES, 8 * LANES), f32)],
        compiler_params=_cparams(("parallel", "arbitrary")))(a, g)


def _rms_fwd(x, g, *, col=0, name, ts=512):
    s, d = x.shape[0], g.shape[1]
    ts = min(ts, s)

    def body(x_ref, g_ref, o_ref):
        xv = x_ref[...].astype(f32)
        r = lax.rsqrt(jnp.mean(xv * xv, axis=-1, keepdims=True) + EPS)
        o_ref[...] = (xv * r * g_ref[...]).astype(bf16)

    return pl.pallas_call(
        body, name=name, grid=(s // ts,),
        in_specs=[pl.BlockSpec((ts, d), lambda i: (i, col)), pl.BlockSpec((1, d), lambda i: (0, 0))],
        out_specs=pl.BlockSpec((ts, d), lambda i: (i, 0)),
        out_shape=jax.ShapeDtypeStruct((s, d), bf16),
        compiler_params=_cparams(("parallel",)))(x, g)


def _rms_bwd(dy, x, g, *, col=0, res=None, out_dtype=f32, name, ts=512):
    s, d = dy.shape
    ts = min(ts, s)
    twin = res is not None

    def body(*refs):
        if twin:
            dy_ref, x_ref, g_ref, res_ref, dx_ref, dxb_ref, dg_ref = refs
        else:
            dy_ref, x_ref, g_ref, dx_ref, dg_ref = refs

        @pl.when(pl.program_id(0) == 0)
        def _():
            dg_ref[...] = jnp.zeros_like(dg_ref)

        xv, dyv = x_ref[...].astype(f32), dy_ref[...].astype(f32)
        r = lax.rsqrt(jnp.mean(xv * xv, axis=-1, keepdims=True) + EPS)
        dyg = dyv * g_ref[...]
        dx = r * dyg - xv * (r * r * r) * jnp.mean(xv * dyg, axis=-1, keepdims=True)
        if twin:
            dx = dx + res_ref[...]
            dxb_ref[...] = dx.astype(bf16)
        dx_ref[...] = dx.astype(out_dtype)
        dg_ref[...] += jnp.sum(dyv * xv * r, axis=0, keepdims=True)

    row, vec = pl.BlockSpec((ts, d), lambda i: (i, 0)), pl.BlockSpec((1, d), lambda i: (0, 0))
    in_specs, args = [row, pl.BlockSpec((ts, d), lambda i: (i, col)), vec], [dy, x, g]
    out_specs, out_shape = [row], [jax.ShapeDtypeStruct((s, d), out_dtype)]
    if twin:
        in_specs.append(row)
        args.append(res)
        out_specs.append(row)
        out_shape.append(jax.ShapeDtypeStruct((s, d), bf16))
    return pl.pallas_call(
        body, name=name, grid=(s // ts,), in_specs=in_specs, out_specs=out_specs + [vec],
        out_shape=out_shape + [jax.ShapeDtypeStruct((1, d), f32)],
        compiler_params=_cparams(("arbitrary",)))(*args)


def _swap_halves(z):
    lane = lax.broadcasted_iota(jnp.int32, z.shape, 1)
    return jnp.where(lane < NOPE + ROPE // 2, pltpu.roll(z, LANES - ROPE // 2, axis=1), pltpu.roll(z, ROPE // 2, axis=1))


def _rope_tables(s):
    inv_freq = 1.0 / (ROPE_THETA ** (jnp.arange(0, ROPE, 2, dtype=f32) / ROPE))
    ang = jnp.arange(s, dtype=f32)[:, None] * inv_freq[None, :]
    cos, sin = jnp.cos(ang), jnp.sin(ang)
    one, zero = jnp.ones((s, NOPE), f32), jnp.zeros((s, NOPE), f32)
    pad1, pad0 = jnp.ones((s, HEAD_PAD - QK), f32), jnp.zeros((s, HEAD_PAD - QK), f32)
    return jnp.concatenate([one, cos, cos, pad1], 1), jnp.concatenate([zero, -sin, sin, pad0], 1)


def _qk_prep_fwd(q_raw, kv_raw, proj, qg, kg, cos_t, sin_t, *, name, ts=512):
    s = q_raw.shape[0]
    ts = min(ts, s)
    rope_blk = (MIX_IN_PAD - HEAD_PAD) // HEAD_PAD

    def body(q_ref, kv_ref, kr_ref, qg_ref, kg_ref, c_ref, s_ref, qo_ref, ko_ref, vo_ref):
        lane = lax.broadcasted_iota(jnp.int32, (ts, HEAD_PAD), 1)
        cosv, sinv = c_ref[...], s_ref[...]

        def norm_rope(z, gain):
            r = lax.rsqrt(jnp.sum(z * z, axis=-1, keepdims=True) * (1.0 / QK) + EPS)
            zn = z * r * gain
            return zn * cosv + _swap_halves(zn) * sinv

        kvv = kv_ref[...]
        qo_ref[...] = norm_rope(q_ref[...], qg_ref[...]).astype(bf16)
        ko_ref[...] = norm_rope(jnp.where(lane < NOPE, kvv, kr_ref[...]), kg_ref[...]).astype(bf16)
        vo_ref[...] = jnp.where(lane >= NOPE, kvv, 0.0).astype(bf16)

    head = pl.BlockSpec((ts, HEAD_PAD), lambda i, h: (i, h))
    row = pl.BlockSpec((ts, HEAD_PAD), lambda i, h: (i, 0))
    vec = pl.BlockSpec((1, HEAD_PAD), lambda i, h: (0, 0))
    out = jax.ShapeDtypeStruct((s, HEADS * HEAD_PAD), bf16)
    return pl.pallas_call(
        body, name=name, grid=(s // ts, HEADS),
        in_specs=[head, head, pl.BlockSpec((ts, HEAD_PAD), lambda i, h: (i, rope_blk)), vec, vec, row, row],
        out_specs=[head, head, head], out_shape=[out, out, out],
        compiler_params=_cparams(("parallel", "parallel")))(q_raw, kv_raw, proj, qg, kg, cos_t, sin_t)


def _qk_prep_bwd(dq, dk, dv, q_raw, kv_raw, proj, qg, kg, cos_t, sin_t, *, name, ts=512):
    s = q_raw.shape[0]
    ts = min(ts, s)
    rope_blk = (MIX_IN_PAD - HEAD_PAD) // HEAD_PAD

    def body(dq_ref, dk_ref, dv_ref, q_ref, kv_ref, kr_ref, qg_ref, kg_ref, c_ref, s_ref,
             dqr_ref, dkvr_ref, dkr_ref, dqg_ref, dkg_ref):
        i, h = pl.program_id(0), pl.program_id(1)
        lane = lax.broadcasted_iota(jnp.int32, (ts, HEAD_PAD), 1)
        is_rope = (lane >= NOPE) & (lane < QK)
        cosv, sinv = c_ref[...], s_ref[...]

        @pl.when((i == 0) & (h == 0))
        def _():
            dqg_ref[...] = jnp.zeros_like(dqg_ref)
            dkg_ref[...] = jnp.zeros_like(dkg_ref)

        @pl.when(h == 0)
        def _():
            dkr_ref[...] = jnp.zeros_like(dkr_ref)

        def back(dout, z, gain):
            dzn = dout * cosv + jnp.where(is_rope, _swap_halves(dout * sinv), 0.0)
            r = lax.rsqrt(jnp.sum(z * z, axis=-1, keepdims=True) * (1.0 / QK) + EPS)
            dzg = dzn * gain
            dz = r * dzg - z * (r * r * r) * (jnp.sum(z * dzg, axis=-1, keepdims=True) * (1.0 / QK))
            return dz, jnp.sum(dzn * z * r, axis=0, keepdims=True)

        dqz, dqg = back(dq_ref[...].astype(f32), q_ref[...], qg_ref[...])
        dqr_ref[...] = dqz.astype(bf16)
        dqg_ref[...] += dqg
        kvv = kv_ref[...]
        dkz, dkg = back(dk_ref[...].astype(f32), jnp.where(lane < NOPE, kvv, kr_ref[...]), kg_ref[...])
        dkg_ref[...] += dkg
        dkvr_ref[...] = jnp.where(lane < NOPE, dkz, dv_ref[...].astype(f32)).astype(bf16)
        dkr_ref[...] += jnp.where(is_rope, dkz, 0.0)

    head = pl.BlockSpec((ts, HEAD_PAD), lambda i, h: (i, h))
    row = pl.BlockSpec((ts, HEAD_PAD), lambda i, h: (i, 0))
    vec = pl.BlockSpec((1, HEAD_PAD), lambda i, h: (0, 0))
    wide = jax.ShapeDtypeStruct((s, HEADS * HEAD_PAD), bf16)
    return pl.pallas_call(
        body, name=name, grid=(s // ts, HEADS),
        in_specs=[head, head, head, head, head, pl.BlockSpec((ts, HEAD_PAD), lambda i, h: (i, rope_blk)), vec, vec, row, row],
        out_specs=[head, head, row, vec, vec],
        out_shape=[wide, wide, jax.ShapeDtypeStruct((s, HEAD_PAD), f32), jax.ShapeDtypeStruct((1, HEAD_PAD), f32),
                   jax.ShapeDtypeStruct((1, HEAD_PAD), f32)],
        compiler_params=_cparams(("arbitrary", "arbitrary")))(dq, dk, dv, q_raw, kv_raw, proj, qg, kg, cos_t, sin_t)


_NT = (((1,), (1,)), ((), ()))
_SCALE = QK ** -0.5


def _flash_fwd(q, k, v, *, name, tq=512):
    s = q.shape[0]
    tq = min(tq, s)

    def body(q_ref, k_ref, v_ref, o_ref, lse_ref):
        i = pl.program_id(1)
        qv = q_ref[...]

        def step(j, carry, masked):
            m, l, acc = carry
            st = pl.multiple_of(j * tq, tq)
            kj, vj = k_ref[pl.ds(st, tq), :], v_ref[pl.ds(st, tq), :]
            sc = lax.dot_general(qv, kj, _NT, preferred_element_type=f32) * _SCALE
            if masked:
                rr = lax.broadcasted_iota(jnp.int32, (tq, tq), 0)
                cc = lax.broadcasted_iota(jnp.int32, (tq, tq), 1)
                sc = jnp.where(cc <= rr, sc, NEG)
            m_new = jnp.maximum(m, jnp.max(sc, axis=-1, keepdims=True))
            p = jnp.exp(sc - m_new)
            alpha = jnp.exp(m - m_new)
            l = alpha * l + jnp.sum(p, axis=-1, keepdims=True)
            acc = alpha * acc + jnp.dot(p.astype(bf16), vj, preferred_element_type=f32)
            return m_new, l, acc

        init = (jnp.full((tq, 1), NEG, f32), jnp.zeros((tq, 1), f32), jnp.zeros((tq, HEAD_PAD), f32))
        carry = lax.fori_loop(0, i, lambda j, c: step(j, c, False), init)
        m, l, acc = step(i, carry, True)
        o_ref[...] = (acc / l).astype(bf16)
        lse_ref[0] = m + jnp.log(l)

    blk = pl.BlockSpec((tq, HEAD_PAD), lambda h, i: (i, h))
    full = pl.BlockSpec((s, HEAD_PAD), lambda h, i: (0, h))
    return pl.pallas_call(
        body, name=name, grid=(HEADS, s // tq), in_specs=[blk, full, full],
        out_specs=[blk, pl.BlockSpec((1, tq, 1), lambda h, i: (h, i, 0))],
        out_shape=[jax.ShapeDtypeStruct((s, HEADS * HEAD_PAD), bf16), jax.ShapeDtypeStruct((HEADS, s, 1), f32)],
        compiler_params=_cparams(("parallel", "arbitrary")))(q, k, v)


def _flash_bwd_dq(q, k, v, o, do, lse, *, name, tq=512):
    s = q.shape[0]
    tq = min(tq, s)

    def body(q_ref, k_ref, v_ref, o_ref, do_ref, lse_ref, dq_ref, dl_ref):
        i = pl.program_id(1)
        qv = q_ref[...]
        dov = do_ref[...].astype(f32)
        delta = jnp.sum(dov * o_ref[...].astype(f32), axis=-1, keepdims=True)
        dob = dov.astype(bf16)
        lsev = lse_ref[0]

        def step(j, acc, masked):
            st = pl.multiple_of(j * tq, tq)
            kj, vj = k_ref[pl.ds(st, tq), :], v_ref[pl.ds(st, tq), :]
            sc = lax.dot_general(qv, kj, _NT, preferred_element_type=f32) * _SCALE
            p = jnp.exp(sc - lsev)
            if masked:
                rr = lax.broadcasted_iota(jnp.int32, (tq, tq), 0)
                cc = lax.broadcasted_iota(jnp.int32, (tq, tq), 1)
                p = jnp.where(cc <= rr, p, 0.0)
            dp = lax.dot_general(dob, vj, _NT, preferred_element_type=f32)
            ds = p * (dp - delta)
            return acc + jnp.dot(ds.astype(bf16), kj, preferred_element_type=f32)

        acc = lax.fori_loop(0, i, lambda j, c: step(j, c, False), jnp.zeros((tq, HEAD_PAD), f32))
        dq_ref[...] = step(i, acc, True) * _SCALE
        dl_ref[0] = delta

    blk = pl.BlockSpec((tq, HEAD_PAD), lambda h, i: (i, h))
    full = pl.BlockSpec((s, HEAD_PAD), lambda h, i: (0, h))
    col = pl.BlockSpec((1, tq, 1), lambda h, i: (h, i, 0))
    return pl.pallas_call(
        body, name=name, grid=(HEADS, s // tq), in_specs=[blk, full, full, blk, blk, col],
        out_specs=[blk, col],
        out_shape=[jax.ShapeDtypeStruct((s, HEADS * HEAD_PAD), f32), jax.ShapeDtypeStruct((HEADS, s, 1), f32)],
        compiler_params=_cparams(("parallel", "arbitrary")))(q, k, v, o, do, lse)


def _flash_bwd_dkv(q, k, v, do, lse_row, delta_row, *, name, tk=512):
    s = q.shape[0]
    tk = min(tk, s)
    nblk = s // tk

    def body(q_ref, k_ref, v_ref, do_ref, lse_ref, dl_ref, dk_ref, dv_ref):
        j = pl.program_id(1)
        kv_, vv = k_ref[...], v_ref[...]

        def step(i, carry, masked):
            dk, dv = carry
            st = pl.multiple_of(i * tk, tk)
            qi = q_ref[pl.ds(st, tk), :]
            doi = do_ref[pl.ds(st, tk), :].astype(bf16)
            lse_i = lse_ref[0, :, pl.ds(st, tk)]
            dl_i = dl_ref[0, :, pl.ds(st, tk)]
            st_ = lax.dot_general(kv_, qi, _NT, preferred_element_type=f32) * _SCALE
            pt = jnp.exp(st_ - lse_i)
            if masked:
                kk = lax.broadcasted_iota(jnp.int32, (tk, tk), 0)
                qq = lax.broadcasted_iota(jnp.int32, (tk, tk), 1)
                pt = jnp.where(kk <= qq, pt, 0.0)
            dv = dv + jnp.dot(pt.astype(bf16), doi, preferred_element_type=f32)
            dpt = lax.dot_general(vv, doi, _NT, preferred_element_type=f32)
            dst = pt * (dpt - dl_i)
            dk = dk + jnp.dot(dst.astype(bf16), qi, preferred_element_type=f32)
            return dk, dv

        zero = jnp.zeros((tk, HEAD_PAD), f32)
        carry = step(j, (zero, zero), True)
        dk, dv = lax.fori_loop(j + 1, nblk, lambda i, c: step(i, c, False), carry)
        dk_ref[...] = dk * _SCALE
        dv_ref[...] = dv

    blk = pl.BlockSpec((tk, HEAD_PAD), lambda h, j: (j, h))
    full = pl.BlockSpec((s, HEAD_PAD), lambda h, j: (0, h))
    rowv = pl.BlockSpec((1, 1, s), lambda h, j: (h, 0, 0))
    out = jax.ShapeDtypeStruct((s, HEADS * HEAD_PAD), f32)
    return pl.pallas_call(
        body, name=name, grid=(HEADS, nblk), in_specs=[full, blk, blk, full, rowv, rowv],
        out_specs=[blk, blk], out_shape=[out, out],
        compiler_params=_cparams(("parallel", "arbitrary")))(q, k, v, do, lse_row, delta_row)


SUBLANES = 8


def _shift_down(x, d):
    r = pltpu.roll(x, d, axis=0)
    t = lax.broadcasted_iota(jnp.int32, (SUBLANES, x.shape[1]), 0)
    head = jnp.where(t < d, 0.0, r[:SUBLANES])
    return head if x.shape[0] == SUBLANES else jnp.concatenate([head, r[SUBLANES:]], axis=0)


def _shift_up(x, d):
    s = x.shape[0]
    r = pltpu.roll(x, s - d, axis=0)
    t = lax.broadcasted_iota(jnp.int32, (SUBLANES, x.shape[1]), 0)
    tail = jnp.where(t >= SUBLANES - d, 0.0, r[s - SUBLANES:])
    return tail if s == SUBLANES else jnp.concatenate([r[:s - SUBLANES], tail], axis=0)


def _taps(w_ref):
    return w_ref[0:1, :], w_ref[1:2, :], w_ref[2:3, :]


def _conv3(u, w):
    u1, u2 = _shift_down(u, 1), _shift_down(u, 2)
    return w[0] * u2 + w[1] * u1 + w[2] * u, (u1, u2)


def _conv3_t(g, w):
    return w[2] * g + w[1] * _shift_up(g, 1) + w[0] * _shift_up(g, 2)


def _conv3_dw(dw_ref, g, u, shifted):
    dw_ref[0:1, :] = jnp.sum(g * shifted[1], axis=0, keepdims=True)
    dw_ref[1:2, :] = jnp.sum(g * shifted[0], axis=0, keepdims=True)
    dw_ref[2:3, :] = jnp.sum(g * u, axis=0, keepdims=True)


_GB, _GC, _CI = 512 // LANES, 1024 // LANES, 1536 // LANES


def _sconv_fwd(proj, w, *, name):
    s = proj.shape[0]

    def body(gb_ref, gc_ref, ci_ref, w_ref, o_ref):
        o_ref[...] = (gb_ref[...] * _conv3(gc_ref[...] * ci_ref[...], _taps(w_ref))[0]).astype(bf16)

    col = lambda off: pl.BlockSpec((s, LANES), lambda j: (0, off + j))
    return pl.pallas_call(
        body, name=name, grid=(CONV_CH // LANES,),
        in_specs=[col(_GB), col(_GC), col(_CI), pl.BlockSpec((3, LANES), lambda j: (0, j))],
        out_specs=pl.BlockSpec((s, LANES), lambda j: (0, j)),
        out_shape=jax.ShapeDtypeStruct((s, CONV_CH), bf16),
        compiler_params=_cparams(("parallel",)))(proj, proj, proj, w)


def _sconv_bwd(dmix, proj, w, *, name):
    s = proj.shape[0]

    def body(do_ref, gb_ref, gc_ref, ci_ref, w_ref, dgb_ref, dgc_ref, dci_ref, dw_ref):
        wv, gc, ci, do = _taps(w_ref), gc_ref[...], ci_ref[...], do_ref[...].astype(f32)
        u = gc * ci
        conv, shifted = _conv3(u, wv)
        dgb_ref[...] = (do * conv).astype(bf16)
        dc = do * gb_ref[...]
        du = _conv3_t(dc, wv)
        dgc_ref[...] = (du * ci).astype(bf16)
        dci_ref[...] = (du * gc).astype(bf16)
        _conv3_dw(dw_ref, dc, u, shifted)

    col = lambda off: pl.BlockSpec((s, LANES), lambda j: (0, off + j))
    out = jax.ShapeDtypeStruct((s, CONV_CH), bf16)
    return pl.pallas_call(
        body, name=name, grid=(CONV_CH // LANES,),
        in_specs=[col(HEADS), col(_GB), col(_GC), col(_CI), pl.BlockSpec((3, LANES), lambda j: (0, j))],
        out_specs=[col(0), col(0), col(0), pl.BlockSpec((3, LANES), lambda j: (0, j))],
        out_shape=[out, out, out, jax.ShapeDtypeStruct((3, CONV_CH), f32)],
        compiler_params=_cparams(("parallel",)))(dmix, proj, proj, proj, w)


def _ffn_act_fwd(zg, zv, cwg, cwv, *, name):
    s, f = zg.shape

    def body(zg_ref, zv_ref, wg_ref, wv_ref, o_ref):
        o_ref[...] = (jax.nn.silu(_conv3(zg_ref[...], _taps(wg_ref))[0]) * _conv3(zv_ref[...], _taps(wv_ref))[0]).astype(bf16)

    col = pl.BlockSpec((s, LANES), lambda j: (0, j))
    wsp = pl.BlockSpec((3, LANES), lambda j: (0, j))
    return pl.pallas_call(
        body, name=name, grid=(f // LANES,), in_specs=[col, col, wsp, wsp], out_specs=col,
        out_shape=jax.ShapeDtypeStruct((s, f), bf16), compiler_params=_cparams(("parallel",)))(zg, zv, cwg, cwv)


def _ffn_act_bwd(da, zg, zv, cwg, cwv, *, name):
    s, f = zg.shape

    def body(da_ref, zg_ref, zv_ref, wg_ref, wv_ref, dzg_ref, dzv_ref, dwg_ref, dwv_ref):
        wg, wv, zgv, zvv, dav = _taps(wg_ref), _taps(wv_ref), zg_ref[...], zv_ref[...], da_ref[...].astype(f32)
        (ug, zg_shifted), (uv, zv_shifted) = _conv3(zgv, wg), _conv3(zvv, wv)
        sg = jax.nn.sigmoid(ug)
        dug = dav * uv * (sg * (1.0 + ug * (1.0 - sg)))
        duv = dav * (ug * sg)
        dzg_ref[...] = _conv3_t(dug, wg).astype(bf16)
        dzv_ref[...] = _conv3_t(duv, wv).astype(bf16)
        _conv3_dw(dwg_ref, dug, zgv, zg_shifted)
        _conv3_dw(dwv_ref, duv, zvv, zv_shifted)

    col = pl.BlockSpec((s, LANES), lambda j: (0, j))
    wsp = pl.BlockSpec((3, LANES), lambda j: (0, j))
    act, wsh = jax.ShapeDtypeStruct((s, f), bf16), jax.ShapeDtypeStruct((3, f), f32)
    return pl.pallas_call(
        body, name=name, grid=(f // LANES,), in_specs=[col, col, col, wsp, wsp], out_specs=[col, col, wsp, wsp],
        out_shape=[act, act, wsh, wsh], compiler_params=_cparams(("parallel",)))(da, zg, zv, cwg, cwv)


def _expand_mat():
    return jnp.asarray(np.kron(np.eye(STATE, dtype=np.float32), np.ones((1, GROUP), np.float32)))


def _disc_fn(lr, li, ls, br, bi, e):
    dt = jnp.exp(ls)
    mag = jnp.exp(lr * dt)
    ar, ai = mag * jnp.cos(li * dt), mag * jnp.sin(li * dt)
    nr, ni = ar - 1.0, ai
    den = lr * lr + li * li
    zr, zi = (nr * lr + ni * li) / den, (ni * lr - nr * li) / den
    zrr = jnp.dot(zr, e, precision=lax.Precision.HIGHEST, preferred_element_type=f32)
    zir = jnp.dot(zi, e, precision=lax.Precision.HIGHEST, preferred_element_type=f32)
    return ar, ai, zrr * br - zir * bi, zrr * bi + zir * br


def _disc_fwd(lr, li, ls, br, bi, *, name):
    def body(lr_ref, li_ref, ls_ref, br_ref, bi_ref, e_ref, ar_ref, ai_ref, bbr_ref, bbi_ref):
        ar, ai, bbr, bbi = _disc_fn(lr_ref[...], li_ref[...], ls_ref[...], br_ref[...], bi_ref[...], e_ref[...])
        ar_ref[...], ai_ref[...], bbr_ref[...], bbi_ref[...] = ar, ai, bbr, bbi

    sq, wide = jax.ShapeDtypeStruct((GROUPS, STATE), f32), jax.ShapeDtypeStruct((GROUPS, STATE * GROUP), f32)
    return pl.pallas_call(body, name=name, out_shape=[sq, sq, wide, wide],
                          compiler_params=_cparams())(lr, li, ls, br, bi, _expand_mat())


def _disc_bwd(lr, li, ls, br, bi, dar, dai, dbbr, dbbi, *, name):
    def body(lr_ref, li_ref, ls_ref, br_ref, bi_ref, e_ref, dar_ref, dai_ref, dbbr_ref, dbbi_ref,
             dlr_ref, dli_ref, dls_ref, dbr_ref, dbi_ref):
        ev = e_ref[...]
        _, vjp = jax.vjp(lambda a, b, c, d_, e_: _disc_fn(a, b, c, d_, e_, ev),
                         lr_ref[...], li_ref[...], ls_ref[...], br_ref[...], bi_ref[...])
        dlr, dli, dls, dbr, dbi = vjp((dar_ref[...], dai_ref[...], dbbr_ref[...], dbbi_ref[...]))
        dlr_ref[...], dli_ref[...], dls_ref[...], dbr_ref[...], dbi_ref[...] = dlr, dli, dls, dbr, dbi

    sq, wide = jax.ShapeDtypeStruct((GROUPS, STATE), f32), jax.ShapeDtypeStruct((GROUPS, STATE * GROUP), f32)
    return pl.pallas_call(body, name=name, out_shape=[sq, sq, jax.ShapeDtypeStruct((GROUPS, 1), f32), wide, wide],
                          compiler_params=_cparams())(lr, li, ls, br, bi, _expand_mat(), dar, dai, dbbr, dbbi)


SCAN_TILE = 64
SCAN_PAIRS = 2


def _tile_shift(v, d, reverse):
    if d % 8:
        return _shift_up(v, d) if reverse else _shift_down(v, d)
    z = jnp.zeros((d, v.shape[1]), v.dtype)
    return jnp.concatenate([v[d:], z], axis=0) if reverse else jnp.concatenate([z, v[:v.shape[0] - d]], axis=0)


def _tile_scan(r, i, pows, reverse):
    d = 1
    for br, bi in pows:
        rs, is_ = _tile_shift(r, d, reverse), _tile_shift(i, d, reverse)
        r, i = r + br * rs - bi * is_, i + br * is_ + bi * rs
        d *= 2
    return r, i


def _scan_setup(ar, ai, reverse):
    if reverse:
        ai = -ai
    pows, br, bi, d = [], ar, ai, 1
    while d < SCAN_TILE:
        pows.append((br, bi))
        br, bi, d = br * br - bi * bi, 2.0 * br * bi, 2 * d
    row = lax.broadcasted_iota(jnp.int32, (SCAN_TILE, LANES), 0)
    hit = row == (SCAN_TILE - 1 if reverse else 0)
    pr, pi = _tile_scan(jnp.where(hit, ar, 0.0), jnp.where(hit, ai, 0.0), pows, reverse)
    return pows, pr, pi


def _carry_in(r, i, pr, pi, cr, ci):
    crb, cib = jnp.broadcast_to(cr, r.shape), jnp.broadcast_to(ci, i.shape)
    return r + pr * crb - pi * cib, i + pr * cib + pi * crb


def _pair_cols(q):
    return slice(q * PAIR_LANES, q * PAIR_LANES + LANES), slice(q * PAIR_LANES + LANES, (q + 1) * PAIR_LANES)


_SCAN_W = SCAN_PAIRS * PAIR_LANES


def _scan_fwd(bu, ar, ai, *, name):
    s = bu.shape[0]
    nt = s // SCAN_TILE

    def body(bu_ref, ar_ref, ai_ref, x_ref):
        setups = [_scan_setup(ar_ref[q], ai_ref[q], False) for q in range(SCAN_PAIRS)]

        def step(k, carry):
            rows = pl.ds(pl.multiple_of(k * SCAN_TILE, SCAN_TILE), SCAN_TILE)
            out = []
            for q, (pows, pr, pi) in enumerate(setups):
                rc, ic = _pair_cols(q)
                r, i = _tile_scan(bu_ref[rows, rc], bu_ref[rows, ic], pows, False)
                r, i = _carry_in(r, i, pr, pi, carry[2 * q], carry[2 * q + 1])
                x_ref[rows, rc] = r
                x_ref[rows, ic] = i
                out += [r[SCAN_TILE - 1:SCAN_TILE, :], i[SCAN_TILE - 1:SCAN_TILE, :]]
            return tuple(out)

        lax.fori_loop(0, nt, step, tuple(jnp.zeros((1, LANES), f32) for _ in range(2 * SCAN_PAIRS)))

    col = pl.BlockSpec((s, _SCAN_W), lambda g: (0, g))
    vec = pl.BlockSpec((SCAN_PAIRS, 1, LANES), lambda g: (g, 0, 0))
    return pl.pallas_call(body, name=name, grid=(bu.shape[1] // _SCAN_W,), in_specs=[col, vec, vec], out_specs=col,
                          out_shape=jax.ShapeDtypeStruct(bu.shape, f32), compiler_params=_cparams(("parallel",)))(bu, ar, ai)


def _scan_bwd(dx, x, ar, ai, *, name):
    s = dx.shape[0]
    nt = s // SCAN_TILE

    def fold(v):
        out = v[0:8]
        for r in range(8, SCAN_TILE, 8):
            out = out + v[r:r + 8]
        return out

    def body(dx_ref, x_ref, ar_ref, ai_ref, g_ref, dar_ref, dai_ref):
        setups = [_scan_setup(ar_ref[q], ai_ref[q], True) for q in range(SCAN_PAIRS)]
        row = lax.broadcasted_iota(jnp.int32, (SCAN_TILE, LANES), 0)

        def step(kk, carry):
            k = nt - 1 - kk
            start = pl.multiple_of(k * SCAN_TILE, SCAN_TILE)
            rows = pl.ds(start, SCAN_TILE)
            prev8 = pl.ds(pl.multiple_of(jnp.maximum(start - 8, 0), 8), 8)

            def before(cols):
                first = jnp.where(k > 0, x_ref[prev8, cols][7:8, :], 0.0)
                return jnp.where(row == 0, first, pltpu.roll(x_ref[rows, cols], 1, axis=0))

            out = []
            for q, (pows, pr, pi) in enumerate(setups):
                rc, ic = _pair_cols(q)
                cr, ci, acc_r, acc_i = carry[4 * q:4 * q + 4]
                gr, gi = _tile_scan(dx_ref[rows, rc], dx_ref[rows, ic], pows, True)
                gr, gi = _carry_in(gr, gi, pr, pi, cr, ci)
                g_ref[rows, rc] = gr.astype(bf16)
                g_ref[rows, ic] = gi.astype(bf16)
                xr, xi = before(rc), before(ic)
                out += [gr[0:1, :], gi[0:1, :], acc_r + fold(gr * xr + gi * xi), acc_i + fold(gi * xr - gr * xi)]
            return tuple(out)

        init = (jnp.zeros((1, LANES), f32), jnp.zeros((1, LANES), f32), jnp.zeros((8, LANES), f32), jnp.zeros((8, LANES), f32))
        res = lax.fori_loop(0, nt, step, init * SCAN_PAIRS)
        for q in range(SCAN_PAIRS):
            dar_ref[q] = jnp.sum(res[4 * q + 2], axis=0, keepdims=True)
            dai_ref[q] = jnp.sum(res[4 * q + 3], axis=0, keepdims=True)

    col = pl.BlockSpec((s, _SCAN_W), lambda g: (0, g))
    vec = pl.BlockSpec((SCAN_PAIRS, 1, LANES), lambda g: (g, 0, 0))
    vsh = jax.ShapeDtypeStruct((GROUPS // 2, 1, LANES), f32)
    return pl.pallas_call(body, name=name, grid=(dx.shape[1] // _SCAN_W,), in_specs=[col, col, vec, vec],
                          out_specs=[col, vec, vec], out_shape=[jax.ShapeDtypeStruct(dx.shape, bf16), vsh, vsh],
                          compiler_params=_cparams(("parallel",)))(dx, x, ar, ai)


_GELU_C = math.sqrt(2.0 / math.pi)


def _gelu_fwd(y, u, dsk, *, name, ts=512):
    s, d = y.shape
    ts = min(ts, s)

    def body(y_ref, u_ref, d_ref, o_ref):
        o_ref[...] = jax.nn.gelu(y_ref[...] + d_ref[...] * u_ref[...]).astype(bf16)

    row, vec = pl.BlockSpec((ts, d), lambda i: (i, 0)), pl.BlockSpec((1, d), lambda i: (0, 0))
    return pl.pallas_call(body, name=name, grid=(s // ts,), in_specs=[row, row, vec], out_specs=row,
                          out_shape=jax.ShapeDtypeStruct((s, d), bf16), compiler_params=_cparams(("parallel",)))(y, u, dsk)


def _gelu_bwd(dg, y, u, dsk, *, name, ts=512):
    s, d = y.shape
    ts = min(ts, s)

    def body(dg_ref, y_ref, u_ref, d_ref, dy_ref, du_ref, dd_ref):
        @pl.when(pl.program_id(0) == 0)
        def _():
            dd_ref[...] = jnp.zeros_like(dd_ref)

        uv, dv = u_ref[...], d_ref[...]
        z = y_ref[...] + dv * uv
        th = jnp.tanh(_GELU_C * (z + 0.044715 * z * z * z))
        dz = dg_ref[...] * (0.5 * (1.0 + th) + 0.5 * z * (1.0 - th * th) * _GELU_C * (1.0 + 3 * 0.044715 * z * z))
        dy_ref[...] = dz.astype(bf16)
        du_ref[...] = dz * dv
        dd_ref[...] += jnp.sum(dz * uv, axis=0, keepdims=True)

    row, vec = pl.BlockSpec((ts, d), lambda i: (i, 0)), pl.BlockSpec((1, d), lambda i: (0, 0))
    return pl.pallas_call(
        body, name=name, grid=(s // ts,), in_specs=[row, row, row, vec], out_specs=[row, row, vec],
        out_shape=[jax.ShapeDtypeStruct((s, d), bf16), jax.ShapeDtypeStruct((s, d), f32), jax.ShapeDtypeStruct((1, d), f32)],
        compiler_params=_cparams(("arbitrary",)))(dg, y, u, dsk)


def _glu_fwd(x, a, b, *, name, ts=512):
    s, d = x.shape
    ts = min(ts, s)

    def body(x_ref, a_ref, b_ref, o_ref):
        o_ref[...] = x_ref[...] + a_ref[...] * jax.nn.sigmoid(b_ref[...])

    row = pl.BlockSpec((ts, d), lambda i: (i, 0))
    return pl.pallas_call(body, name=name, grid=(s // ts,), in_specs=[row, row, row], out_specs=row,
                          out_shape=jax.ShapeDtypeStruct((s, d), f32), compiler_params=_cparams(("parallel",)))(x, a, b)


def _glu_bwd(dx, a, b, *, name, ts=512):
    s, d = dx.shape
    ts = min(ts, s)

    def body(dx_ref, a_ref, b_ref, da_ref, db_ref):
        sg = jax.nn.sigmoid(b_ref[...])
        dxv = dx_ref[...]
        da_ref[...] = (dxv * sg).astype(bf16)
        db_ref[...] = (dxv * a_ref[...] * sg * (1.0 - sg)).astype(bf16)

    row = pl.BlockSpec((ts, d), lambda i: (i, 0))
    out = jax.ShapeDtypeStruct((s, d), bf16)
    return pl.pallas_call(body, name=name, grid=(s // ts,), in_specs=[row, row, row], out_specs=[row, row],
                          out_shape=[out, out], compiler_params=_cparams(("parallel",)))(dx, a, b)


def _add(a, b, *, name, ts=512):
    s, d = a.shape
    ts = min(ts, s)

    def body(a_ref, b_ref, o_ref):
        o_ref[...] = (a_ref[...].astype(f32) + b_ref[...].astype(f32)).astype(bf16)

    row = pl.BlockSpec((ts, d), lambda i: (i, 0))
    return pl.pallas_call(body, name=name, grid=(s // ts,), in_specs=[row, row], out_specs=row,
                          out_shape=jax.ShapeDtypeStruct((s, d), bf16), compiler_params=_cparams(("parallel",)))(a, b)


def _loss_head(y, target, *, name, ts=512):
    s, d = y.shape
    ts = min(ts, s)

    def body(y_ref, t_ref, dy_ref, dyb_ref, l_ref):
        @pl.when(pl.program_id(0) == 0)
        def _():
            l_ref[...] = jnp.zeros_like(l_ref)

        e = y_ref[...] - t_ref[...]
        dy = e * (1.0 / d)
        dy_ref[...] = dy
        dyb_ref[...] = dy.astype(bf16)
        l_ref[...] += 0.5 * jnp.sum(jnp.mean(e * e, axis=-1, keepdims=True))

    row = pl.BlockSpec((ts, d), lambda i: (i, 0))
    return pl.pallas_call(
        body, name=name, grid=(s // ts,), in_specs=[row, row],
        out_specs=[row, row, pl.BlockSpec((8, LANES), lambda i: (0, 0))],
        out_shape=[jax.ShapeDtypeStruct((s, d), f32), jax.ShapeDtypeStruct((s, d), bf16), jax.ShapeDtypeStruct((8, LANES), f32)],
        compiler_params=_cparams(("arbitrary",)))(y, target)


def _adamw(w, g, m, v, *, name, tr=128):
    r, c = w.shape

    def body(w_ref, g_ref, m_ref, v_ref, d_ref, mo_ref, vo_ref):
        gv = g_ref[...]
        mn = ADAM_B1 * m_ref[...] + (1.0 - ADAM_B1) * gv
        vn = ADAM_B2 * v_ref[...] + (1.0 - ADAM_B2) * (gv * gv)
        m_hat = mn / (1.0 - ADAM_B1 ** ADAM_STEP)
        v_hat = vn / (1.0 - ADAM_B2 ** ADAM_STEP)
        d_ref[...] = -ADAM_LR * (m_hat / (jnp.sqrt(v_hat) + ADAM_EPS) + ADAM_WD * w_ref[...])
        mo_ref[...] = mn
        vo_ref[...] = vn

    row = pl.BlockSpec((tr, c), lambda i: (i, 0))
    out = jax.ShapeDtypeStruct((r, c), f32)
    return pl.pallas_call(body, name=name, grid=(r // tr,), in_specs=[row] * 4, out_specs=[row] * 3,
                          out_shape=[out, out, out], compiler_params=_cparams(("parallel",)))(w, g, m, v)


def _sum_slabs(land, *, name, tr=128):
    n, r, c = land.shape

    def body(l_ref, o_ref):
        acc = l_ref[0].astype(f32)
        for i in range(1, n):
            acc = acc + l_ref[i].astype(f32)
        o_ref[...] = acc

    return pl.pallas_call(body, name=name, grid=(r // tr,), in_specs=[pl.BlockSpec((n, tr, c), lambda i: (0, i, 0))],
                          out_specs=pl.BlockSpec((tr, c), lambda i: (i, 0)), out_shape=jax.ShapeDtypeStruct((r, c), f32),
                          compiler_params=_cparams(("parallel",)))(land)


def _pair_sum(g, theirs, *, name, tr=256):
    n, r, c = theirs.shape

    def body(c_ref, g_ref, t_ref, o_ref):
        o_ref[...] = (g_ref[...].astype(f32) + t_ref[...].astype(f32)).astype(bf16)

    blk = pl.BlockSpec((1, tr, c), lambda j, i, c_ref: (j, i, 0))
    mine = pl.BlockSpec((1, tr, c), lambda j, i, c_ref: (2 * j + c_ref[0], i, 0))
    return pl.pallas_call(
        body, name=name,
        grid_spec=pltpu.PrefetchScalarGridSpec(num_scalar_prefetch=1, grid=(n, r // tr), in_specs=[mine, blk], out_specs=blk),
        out_shape=jax.ShapeDtypeStruct(theirs.shape, bf16),
        compiler_params=_cparams(("parallel", "parallel")))(lax.axis_index("c").astype(jnp.int32).reshape(1), g, theirs)


_MESH = pl.DeviceIdType.MESH
_HBM = pl.BlockSpec(memory_space=pltpu.HBM)
N_CHIP = N_DEV // 2


def _position():
    return lax.axis_index("x"), lax.axis_index("y"), lax.axis_index("c")


def _gather8(x, *, name):
    def body(x_ref, o_ref, send_sems, recv_sems, local_sem):
        xx, yy, cc = _position()
        me, sibling = (xx, yy, cc), (xx, yy, 1 - cc)
        chips = [(1 - xx, yy), (xx, 1 - yy), (1 - xx, 1 - yy)]

        def slab(px, py, pc):
            return o_ref.at[4 * px + 2 * py + pc]

        def copy(k, block, to, src=None):
            return pltpu.make_async_remote_copy(src_ref=slab(*block) if src is None else src, dst_ref=slab(*block),
                                                send_sem=send_sems.at[k], recv_sem=recv_sems.at[k], device_id=to,
                                                device_id_type=_MESH)

        mine = pltpu.make_async_copy(x_ref, slab(*me), local_sem)
        mine.start()
        first = [copy(0, me, sibling, src=x_ref)] + [copy(1 + j, me, (*chip, cc), src=x_ref) for j, chip in enumerate(chips)]
        for cp in first:
            cp.start()
        passed = [copy(4 + j, (*chip, cc), sibling) for j, chip in enumerate(chips)]
        for j, chip in enumerate(chips):
            copy(1 + j, (*chip, cc), me).wait_recv()
            passed[j].start()
        copy(0, sibling, me).wait_recv()
        for j, chip in enumerate(chips):
            copy(4 + j, (*chip, 1 - cc), me).wait_recv()
        for cp in first + passed:
            cp.wait_send()
        mine.wait()

    return pl.pallas_call(
        body, name=name, in_specs=[_HBM], out_specs=_HBM, out_shape=jax.ShapeDtypeStruct((N_DEV,) + x.shape, x.dtype),
        scratch_shapes=[pltpu.SemaphoreType.DMA((N_DEV - 1,)), pltpu.SemaphoreType.DMA((N_DEV - 1,)), pltpu.SemaphoreType.DMA],
    )(x)


def _pair_exchange(g, *, name):
    def body(g_ref, land_ref, send_sems, recv_sems):
        xx, yy, cc = _position()
        copies = []
        for j in range(N_CHIP):
            cp = pltpu.make_async_remote_copy(src_ref=g_ref.at[2 * j + 1 - cc], dst_ref=land_ref.at[j], send_sem=send_sems.at[j],
                                              recv_sem=recv_sems.at[j], device_id=(xx, yy, 1 - cc), device_id_type=_MESH)
            cp.start()
            copies.append(cp)
        for cp in copies:
            cp.wait_recv()
        for cp in copies:
            cp.wait_send()

    sems = pltpu.SemaphoreType.DMA((N_CHIP,))
    return pl.pallas_call(body, name=name, in_specs=[_HBM], out_specs=_HBM,
                          out_shape=jax.ShapeDtypeStruct((N_CHIP,) + g.shape[1:], g.dtype), scratch_shapes=[sems, sems])(g)


def _cross_exchange(p, *, name):
    def body(p_ref, o_ref, send_sems, recv_sems, local_sem):
        xx, yy, cc = _position()
        my_chip = 2 * xx + yy
        local = pltpu.make_async_copy(p_ref.at[my_chip], o_ref.at[my_chip], local_sem)
        local.start()
        chips = [(1 - xx, yy), (xx, 1 - yy), (1 - xx, 1 - yy)]
        sends = []
        for k, (px, py) in enumerate(chips):
            cp = pltpu.make_async_remote_copy(src_ref=p_ref.at[2 * px + py], dst_ref=o_ref.at[my_chip], send_sem=send_sems.at[k],
                                              recv_sem=recv_sems.at[k], device_id=(px, py, cc), device_id_type=_MESH)
            cp.start()
            sends.append(cp)
        for k, (px, py) in enumerate(chips):
            pltpu.make_async_remote_copy(src_ref=p_ref.at[2 * px + py], dst_ref=o_ref.at[2 * px + py], send_sem=send_sems.at[k],
                                         recv_sem=recv_sems.at[k], device_id=(px, py, cc), device_id_type=_MESH).wait_recv()
        for cp in sends:
            cp.wait_send()
        local.wait()

    sems = pltpu.SemaphoreType.DMA((N_CHIP - 1,))
    return pl.pallas_call(body, name=name, in_specs=[_HBM], out_specs=_HBM, out_shape=jax.ShapeDtypeStruct(p.shape, p.dtype),
                          scratch_shapes=[sems, sems, pltpu.SemaphoreType.DMA])(p)


def _all_sum(x, *, name):
    return _sum_slabs(_gather8(x, name=f"gather_{name}"), name=f"sum_{name}", tr=min(128, x.shape[0]))


def _pack_slabs(parts, rows, axis=0):
    lead = parts[0].shape[:axis]
    slabs = [p.reshape(lead + (-1, D)) for p in parts]
    used = sum(sl.shape[axis] for sl in slabs)
    return jnp.concatenate(slabs + [jnp.zeros(lead + (rows - used, D), slabs[0].dtype)], axis=axis)


def _unpack_slabs(slab, shapes):
    lead, out, off = slab.shape[:-2], [], 0
    for shp in shapes:
        n = int(np.prod(shp)) // D
        out.append(slab[..., off:off + n, :].reshape(lead + tuple(shp)))
        off += n
    return out


def _pack_rows(parts, rows):
    flat = jnp.concatenate([p.reshape(-1) for p in parts])
    return jnp.pad(flat, (0, rows * D - flat.shape[0])).reshape(rows, D)


def _unpack_rows(slab, shapes):
    flat, out, off = slab.reshape(-1), [], 0
    for shp in shapes:
        n = int(np.prod(shp))
        out.append(flat[off:off + n].reshape(shp))
        off += n
    return out


def _full_shape(shard, axis):
    return tuple(d * N_DEV if i == axis else d for i, d in enumerate(shard))


def _gather_full(shards, shard, axis):
    return jnp.moveaxis(shards, 0, axis).reshape(_full_shape(shard, axis))


def _split_full(full, shard, axis):
    shp = shard[:axis] + (N_DEV, shard[axis]) + shard[axis + 1:]
    return jnp.moveaxis(full.reshape(shp), axis, 0)


def _row(v):
    return v.reshape(1, -1).astype(f32)


def _pad_gain(g):
    return jnp.pad(g.astype(f32), (0, HEAD_PAD - QK)).reshape(1, HEAD_PAD)


def _ffn_fwd(x, p, tag):
    h = _rms_fwd(x, p["norm"], name=f"ffn_norm_{tag}")
    zg = _mm(h, p["wg"], name=f"ffn_up_g_{tag}")
    zv = _mm(h, p["wv"], name=f"ffn_up_v_{tag}")
    a = _ffn_act_fwd(zg, zv, p["cwg"], p["cwv"], name=f"ffn_act_{tag}")
    y = _mm(a, p["wd"], add=x, name=f"ffn_down_{tag}")
    return y, (x, h, zg, zv, a)


def _ffn_bwd(dy, dyb, p, saved, tag):
    x, h, zg, zv, a = saved
    g = {}
    da = _mm(dyb, p["wd"], tb=True, out_dtype=bf16, name=f"ffn_down_dx_{tag}")
    g["wd"] = _mm(a, dyb, ta=True, name=f"ffn_down_dw_{tag}")
    dzg, dzv, g["cwg"], g["cwv"] = _ffn_act_bwd(da, zg, zv, p["cwg"], p["cwv"], name=f"ffn_act_bwd_{tag}")
    g["wg"] = _mm(h, dzg, ta=True, name=f"ffn_up_g_dw_{tag}")
    g["wv"] = _mm(h, dzv, ta=True, name=f"ffn_up_v_dw_{tag}")
    dh = _mm(dzg, p["wg"], tb=True, name=f"ffn_up_g_dx_{tag}")
    dh = _mm(dzv, p["wv"], tb=True, add=dh, name=f"ffn_up_v_dx_{tag}")
    dx, dxb, g["norm"] = _rms_bwd(dh, x, p["norm"], res=dy, name=f"ffn_norm_bwd_{tag}")
    return dx, dxb, g


def _mla_fwd(x, p, tabs, tag):
    cos_t, sin_t = tabs
    h = _rms_fwd(x, p["norm"], name=f"attn_norm_{tag}")
    proj = _mm(h, p["w_in"], name=f"mix_in_{tag}")
    cqn = _rms_fwd(proj, p["cq_norm"], col=0, name=f"cq_norm_{tag}")
    ckvn = _rms_fwd(proj, p["ckv_norm"], col=1, name=f"ckv_norm_{tag}")
    q_raw = _mm(cqn, p["w_uq"], name=f"uq_{tag}")
    kv_raw = _mm(ckvn, p["w_ukv"], name=f"ukv_{tag}")
    q, k, v = _qk_prep_fwd(q_raw, kv_raw, proj, p["q_gain"], p["k_gain"], cos_t, sin_t, name=f"qk_prep_{tag}")
    o, lse = _flash_fwd(q, k, v, name=f"flash_fwd_{tag}")
    conv = _sconv_fwd(proj, p["sconv_w"], name=f"sconv_{tag}")
    mix = jnp.concatenate([o, conv], axis=1)
    y = _mm(mix, p["w_out"], add=x, name=f"mix_out_{tag}")
    return y, (x, h, proj, cqn, ckvn, q_raw, kv_raw, q, k, v, o, lse, mix)


def _mla_bwd(dy, dyb, p, tabs, saved, tag):
    cos_t, sin_t = tabs
    x, h, proj, cqn, ckvn, q_raw, kv_raw, q, k, v, o, lse, mix = saved
    s = x.shape[0]
    g = {}
    dmix = _mm(dyb, p["w_out"], tb=True, name=f"mix_out_dx_{tag}")
    g["w_out"] = _mm(mix, dyb, ta=True, name=f"mix_out_dw_{tag}")
    dgb, dgc, dci, g["sconv_w"] = _sconv_bwd(dmix, proj, p["sconv_w"], name=f"sconv_bwd_{tag}")
    dq, delta = _flash_bwd_dq(q, k, v, o, dmix, lse, name=f"flash_dq_{tag}")
    dk, dv = _flash_bwd_dkv(q, k, v, dmix, lse.reshape(HEADS, 1, s), delta.reshape(HEADS, 1, s), name=f"flash_dkv_{tag}")
    dq_raw, dkv_raw, dkr, g["q_gain"], g["k_gain"] = _qk_prep_bwd(
        dq, dk, dv, q_raw, kv_raw, proj, p["q_gain"], p["k_gain"], cos_t, sin_t, name=f"qk_prep_bwd_{tag}")
    dcqn = _mm(dq_raw, p["w_uq"], tb=True, name=f"uq_dx_{tag}")
    g["w_uq"] = _mm(cqn, dq_raw, ta=True, name=f"uq_dw_{tag}")
    dckvn = _mm(dkv_raw, p["w_ukv"], tb=True, name=f"ukv_dx_{tag}")
    g["w_ukv"] = _mm(ckvn, dkv_raw, ta=True, name=f"ukv_dw_{tag}")
    dcq, g["cq_norm"] = _rms_bwd(dcqn, proj, p["cq_norm"], col=0, out_dtype=bf16, name=f"cq_norm_bwd_{tag}")
    dckv, g["ckv_norm"] = _rms_bwd(dckvn, proj, p["ckv_norm"], col=1, out_dtype=bf16, name=f"ckv_norm_bwd_{tag}")
    dproj = jnp.concatenate([dcq, dckv, dgb, dgc, dci, dkr.astype(bf16)], axis=1)
    dh = _mm(dproj, p["w_in"], tb=True, name=f"mix_in_dx_{tag}")
    g["w_in"] = _mm(h, dproj, ta=True, name=f"mix_in_dw_{tag}")
    dx, dxb, g["norm"] = _rms_bwd(dh, x, p["norm"], res=dy, name=f"attn_norm_bwd_{tag}")
    return dx, dxb, g


def _block_diag(wg):
    nb, ng, r, c = wg.shape
    eye = jnp.eye(ng, dtype=wg.dtype)
    return (wg[:, :, :, None, :] * eye[None, :, None, :, None]).reshape(nb, ng * r, ng * c)


def _s5_mats(bbr, bbi, c_re, c_im):
    nb = GROUPS // 8
    b4 = jnp.stack([bbr.reshape(GROUPS, STATE, GROUP), bbi.reshape(GROUPS, STATE, GROUP)], axis=1)
    wg = jnp.transpose(b4, (0, 3, 1, 2)).reshape(nb, 8, GROUP, 2 * STATE)
    cg = jnp.stack([c_re, -c_im], axis=1)
    cg = jnp.transpose(cg, (0, 1, 3, 2)).reshape(nb, 8, 2 * STATE, GROUP)
    return _state_layout(_block_diag(wg), 2), _state_layout(_block_diag(cg), 1)


def _state_layout(m, axis):
    shp = m.shape
    m = m.reshape(shp[:axis] + (4, 2, 2, STATE) + shp[axis + 1:])
    return jnp.swapaxes(m, axis + 1, axis + 2).reshape(shp)


def _group_blocks(d):
    d = d.reshape(GROUPS // 2, 2, GROUP, 2, 2, STATE)
    return jnp.stack([d[:, 0, :, :, 0, :], d[:, 1, :, :, 1, :]], axis=1).reshape(GROUPS, GROUP, 2, STATE)


def _s5_fwd(x, p, tag):
    h = _rms_fwd(x, p["norm"], name=f"ssm_norm_{tag}")
    u = _mm(h, p["w_in"], name=f"ssm_in_{tag}")
    ar, ai, bbr, bbi = _disc_fwd(p["lr"], p["li"], p["ls"], p["br"], p["bi"], name=f"disc_{tag}")
    wb, cb = _s5_mats(bbr, bbi, p["c_re"], p["c_im"])
    a1, a2 = ar.reshape(GROUPS // 2, 1, LANES), ai.reshape(GROUPS // 2, 1, LANES)
    bu = _bd_nn(u, wb.astype(bf16), name=f"ssm_bu_{tag}")
    xs = _scan_fwd(bu, a1, a2, name=f"ssm_scan_{tag}")
    y = _bd_nn(xs, cb.astype(bf16), name=f"ssm_y_{tag}")
    g = _gelu_fwd(y, u, p["d_skip"], name=f"ssm_gelu_{tag}")
    a = _mm(g, p["wga"], name=f"glu_a_{tag}")
    b = _mm(g, p["wgb"], name=f"glu_b_{tag}")
    out = _glu_fwd(x, a, b, name=f"glu_{tag}")
    return out, (x, h, u, wb, cb, a1, a2, xs, y, g, a, b)


def _s5_bwd(dout, p, saved, tag):
    x, h, u, wb, cb, a1, a2, xs, y, g, a, b = saved
    gr = {}
    da, db = _glu_bwd(dout, a, b, name=f"glu_bwd_{tag}")
    dg = _mm(da, p["wga"], tb=True, name=f"glu_a_dx_{tag}")
    dg = _mm(db, p["wgb"], tb=True, add=dg, name=f"glu_b_dx_{tag}")
    gr["wga"] = _mm(g, da, ta=True, name=f"glu_a_dw_{tag}")
    gr["wgb"] = _mm(g, db, ta=True, name=f"glu_b_dw_{tag}")
    dy, du1, gr["d_skip"] = _gelu_bwd(dg, y, u, p["d_skip"], name=f"ssm_gelu_bwd_{tag}")
    dxs = _bd_nn(dy, jnp.swapaxes(cb, 1, 2).astype(bf16), name=f"ssm_y_dx_{tag}")
    dct = _group_blocks(_bd_tn_diag(dy, xs, name=f"ssm_y_dw_{tag}"))
    gs, dar, dai = _scan_bwd(dxs, xs, a1, a2, name=f"ssm_scan_bwd_{tag}")
    du2 = _bd_nn(gs, jnp.swapaxes(wb, 1, 2).astype(bf16), name=f"ssm_bu_dx_{tag}")
    dwg = _group_blocks(_bd_tn_diag(u, gs, name=f"ssm_bu_dw_{tag}"))
    du = _add(du1, du2, name=f"ssm_du_{tag}")
    dh = _mm(du, p["w_in"], tb=True, name=f"ssm_in_dx_{tag}")
    gr["w_in"] = _mm(h, du, ta=True, name=f"ssm_in_dw_{tag}")
    dx, dxb, gr["norm"] = _rms_bwd(dh, x, p["norm"], res=dout, name=f"ssm_norm_bwd_{tag}")
    dbb = jnp.transpose(dwg, (2, 0, 3, 1)).reshape(2, GROUPS, STATE * GROUP)
    gr["c_re"] = dct[:, :, 0, :]
    gr["c_im"] = -dct[:, :, 1, :]
    dlr, dli, dls, dbr, dbi = _disc_bwd(p["lr"], p["li"], p["ls"], p["br"], p["bi"], dar.reshape(GROUPS, STATE),
                                        dai.reshape(GROUPS, STATE), dbb[0], dbb[1], name=f"disc_bwd_{tag}")
    gr["lr"], gr["li"], gr["ls"] = dlr, dli, dls.reshape(GROUPS)
    gr["br"], gr["bi"] = dbr.reshape(GROUPS, STATE, GROUP), dbi.reshape(GROUPS, STATE, GROUP)
    return dx, dxb, gr


def _mix_in_pad(w):
    z = lambda n: jnp.zeros((w.shape[0], n), w.dtype)
    return jnp.concatenate([w[:, :512], w[:, 544:2080], z(NOPE), w[:, 512:544], z(HEAD_PAD - QK)], axis=1)


def _mix_in_unpad(g):
    return jnp.concatenate([g[:, :512], g[:, 2048 + NOPE:2048 + QK], g[:, 512:2048]], axis=1)


def _uq_pad(w):
    return jnp.pad(w.reshape(LORA, HEADS, QK), ((0, 0), (0, 0), (0, HEAD_PAD - QK))).reshape(LORA, HEADS * HEAD_PAD)


def _uq_unpad(g):
    return g.reshape(LORA, HEADS, HEAD_PAD)[:, :, :QK].reshape(LORA, HEADS * QK)


def _mix_out_pad(w):
    att = jnp.pad(w[:512].reshape(HEADS, NOPE, D), ((0, 0), (NOPE, 0), (0, 0))).reshape(HEADS * HEAD_PAD, D)
    return jnp.concatenate([att, w[512:]], axis=0)


def _mix_out_unpad(g):
    att = g[:HEADS * HEAD_PAD].reshape(HEADS, HEAD_PAD, D)[:, NOPE:, :].reshape(HEADS * NOPE, D)
    return jnp.concatenate([att, g[HEADS * HEAD_PAD:]], axis=0)


def _layer_params(w, layer):
    i = layer // 2
    ffn = dict(norm=_row(w["ffn_norm"][layer]), wg=w["ffn_w_up"][layer][:, :FFN_H], wv=w["ffn_w_up"][layer][:, FFN_H:],
               cwg=w["ffn_conv_w"][layer][:, :FFN_H], cwv=w["ffn_conv_w"][layer][:, FFN_H:], wd=w["ffn_w_down"][layer])
    if layer % 2 == 0:
        mixer = dict(norm=_row(w["attn_norm"][i]), w_in=_mix_in_pad(w["mix_w_in"][i]), cq_norm=_row(w["cq_norm"][i]),
                     ckv_norm=_row(w["ckv_norm"][i]), w_uq=_uq_pad(w["w_uq"][i]), w_ukv=w["w_ukv"][i],
                     q_gain=_pad_gain(w["q_gain"][i]), k_gain=_pad_gain(w["k_gain"][i]), sconv_w=w["sconv_w"][i],
                     w_out=_mix_out_pad(w["mix_w_out"][i]))
    else:
        mixer = dict(norm=_row(w["ssm_norm"][i]), w_in=w["ssm_w_in"][i], lr=w["lambda_re"][i], li=w["lambda_im"][i],
                     ls=w["log_step"][i].reshape(GROUPS, 1), br=w["b_re"][i].reshape(GROUPS, STATE * GROUP),
                     bi=w["b_im"][i].reshape(GROUPS, STATE * GROUP), c_re=w["c_re"][i], c_im=w["c_im"][i],
                     d_skip=_row(w["d_skip"][i]), wga=w["w_glu"][i][:, :D], wgb=w["w_glu"][i][:, D:])
    return mixer, ffn


def _collect_grads(gm, gf):
    st = lambda xs: jnp.stack(xs, axis=0)
    ev, od = (0, 2), (1, 3)
    out = {
        "attn_norm": st([gm[l]["norm"].reshape(D) for l in ev]),
        "mix_w_in": st([_mix_in_unpad(gm[l]["w_in"]) for l in ev]),
        "cq_norm": st([gm[l]["cq_norm"].reshape(LORA) for l in ev]),
        "ckv_norm": st([gm[l]["ckv_norm"].reshape(LORA) for l in ev]),
        "w_uq": st([_uq_unpad(gm[l]["w_uq"]) for l in ev]),
        "w_ukv": st([gm[l]["w_ukv"] for l in ev]),
        "q_gain": st([gm[l]["q_gain"].reshape(HEAD_PAD)[:QK] for l in ev]),
        "k_gain": st([gm[l]["k_gain"].reshape(HEAD_PAD)[:QK] for l in ev]),
        "sconv_w": st([gm[l]["sconv_w"] for l in ev]),
        "mix_w_out": st([_mix_out_unpad(gm[l]["w_out"]) for l in ev]),
        "ssm_norm": st([gm[l]["norm"].reshape(D) for l in od]),
        "ssm_w_in": st([gm[l]["w_in"] for l in od]),
        "lambda_re": st([gm[l]["lr"] for l in od]), "lambda_im": st([gm[l]["li"] for l in od]),
        "log_step": st([gm[l]["ls"] for l in od]),
        "b_re": st([gm[l]["br"] for l in od]), "b_im": st([gm[l]["bi"] for l in od]),
        "c_re": st([gm[l]["c_re"] for l in od]), "c_im": st([gm[l]["c_im"] for l in od]),
        "d_skip": st([gm[l]["d_skip"].reshape(D) for l in od]),
        "w_glu": st([jnp.concatenate([gm[l]["wga"], gm[l]["wgb"]], axis=1) for l in od]),
        "ffn_norm": st([gf[l]["norm"].reshape(D) for l in range(4)]),
        "ffn_w_up": st([jnp.concatenate([gf[l]["wg"], gf[l]["wv"]], axis=1) for l in range(4)]),
        "ffn_conv_w": st([jnp.concatenate([gf[l]["cwg"], gf[l]["cwv"]], axis=1) for l in range(4)]),
        "ffn_w_down": st([gf[l]["wd"] for l in range(4)]),
    }
    return out


def _local_step(x, target, w):
    s = x.shape[0]
    tabs = _rope_tables(s)
    saved, params = [], []
    for layer in range(4):
        mixer, ffn = _layer_params(w, layer)
        params.append((mixer, ffn))
        if layer % 2 == 0:
            x, sm = _mla_fwd(x, mixer, tabs, f"l{layer}")
        else:
            x, sm = _s5_fwd(x, mixer, f"l{layer}")
        x, sf = _ffn_fwd(x, ffn, f"l{layer}")
        saved.append((sm, sf))
    dx, dxb, loss = _loss_head(x, target, name="loss_head")
    gm, gf = [None] * 4, [None] * 4
    for layer in reversed(range(4)):
        mixer, ffn = params[layer]
        sm, sf = saved[layer]
        dx, dxb, gf[layer] = _ffn_bwd(dx, dxb, ffn, sf, f"l{layer}")
        if layer % 2 == 0:
            dx, dxb, gm[layer] = _mla_bwd(dx, dxb, mixer, tabs, sm, f"l{layer}")
        else:
            dx, dxb, gm[layer] = _s5_bwd(dx, mixer, sm, f"l{layer}")
    return loss, dx, _collect_grads(gm, gf)


def kernel(x, attn_norm, mix_w_in, cq_norm, ckv_norm, w_uq, w_ukv, q_gain, k_gain, sconv_w, mix_w_out, ssm_norm, ssm_w_in, lambda_re, lambda_im, log_step, b_re, b_im, c_re, c_im, d_skip, w_glu, ffn_norm, ffn_w_up, ffn_conv_w, ffn_w_down, loss_target, m_attn_norm, m_mix_w_in, m_cq_norm, m_ckv_norm, m_w_uq, m_w_ukv, m_q_gain, m_k_gain, m_sconv_w, m_mix_w_out, m_ssm_norm, m_ssm_w_in, m_lambda_re, m_lambda_im, m_log_step, m_b_re, m_b_im, m_c_re, m_c_im, m_d_skip, m_w_glu, m_ffn_norm, m_ffn_w_up, m_ffn_conv_w, m_ffn_w_down, v_attn_norm, v_mix_w_in, v_cq_norm, v_ckv_norm, v_w_uq, v_w_ukv, v_q_gain, v_k_gain, v_sconv_w, v_mix_w_out, v_ssm_norm, v_ssm_w_in, v_lambda_re, v_lambda_im, v_log_step, v_b_re, v_b_im, v_c_re, v_c_im, v_d_skip, v_w_glu, v_ffn_norm, v_ffn_w_up, v_ffn_conv_w, v_ffn_w_down):
    args = dict(locals())
    wsh = {n: args[n] for n in WEIGHTS}
    msh = {n: args["m_" + n] for n in WEIGHTS}
    vsh = {n: args["v_" + n] for n in WEIGHTS}
    me = 4 * lax.axis_index("x") + 2 * lax.axis_index("y") + lax.axis_index("c")
    big_names, big_shapes = [n for n, _, _ in BIG], [sh for _, sh, _ in BIG]
    small_names = [n for n, _ in REPL] + [n for n, _, _ in SMALL]

    big_all = _gather8(_pack_slabs([wsh[n].astype(bf16) for n in big_names], BIG_ROWS), name="gather_weights")
    w = {n: _gather_full(val, shard, axis) for (n, shard, axis), val in zip(BIG, _unpack_slabs(big_all, big_shapes))}
    placed = []
    for n, shard, axis in SMALL:
        start = [0] * len(shard)
        start[axis] = me * shard[axis]
        placed.append(lax.dynamic_update_slice(jnp.zeros(_full_shape(shard, axis), f32), wsh[n], start))
    small_all = _all_sum(_pack_rows(placed, SMALL_FWD_ROWS), name="small_params")
    for (n, shard, axis), full in zip(SMALL, _unpack_rows(small_all, [_full_shape(sh, ax) for _, sh, ax in SMALL])):
        w[n] = full
    for n, _ in REPL:
        w[n] = wsh[n]

    loss8, grad_x, grads = _local_step(x[0], loss_target[0], w)

    contrib = _pack_slabs([_split_full(grads[n].astype(bf16), shard, axis) for n, shard, axis in BIG], BIG_ROWS, axis=1)
    chip_sum = _pair_sum(contrib, _pair_exchange(contrib, name="grads_pair_exchange"), name="grads_pair_sum")
    g_big = _sum_slabs(_cross_exchange(chip_sum, name="grads_cross_exchange"), name="grads_chip_sum")
    small_vec = _pack_rows([grads[n] for n, _ in REPL] + [grads[n] for n, _, _ in SMALL] + [loss8[0, :1]], SMALL_ROWS)
    small_sum = _all_sum(small_vec, name="small_grads")
    parts = _unpack_rows(small_sum, [sh for _, sh in REPL] + [_full_shape(sh, ax) for _, sh, ax in SMALL] + [(1,)])
    g = {n: val for (n, _), val in zip(REPL, parts)}
    for (n, shard, axis), val in zip(SMALL, parts[len(REPL):]):
        start = [0] * len(shard)
        start[axis] = me * shard[axis]
        g[n] = lax.dynamic_slice(val, start, shard)
    loss = parts[-1].reshape(())
    g.update(zip(big_names, _unpack_slabs(g_big, big_shapes)))

    delta, new_m, new_v = {}, {}, {}
    big_state = [_pack_slabs([src[n] for n in big_names], BIG_ROWS) for src in (wsh, msh, vsh)]
    for dst, slab in zip((delta, new_m, new_v), _adamw(big_state[0], g_big, big_state[1], big_state[2], name="adamw_big")):
        dst.update(zip(big_names, _unpack_slabs(slab, big_shapes)))
    small_state = [_pack_rows([src[n] for n in small_names], SMALL_ROWS) for src in (wsh, g, msh, vsh)]
    for dst, slab in zip((delta, new_m, new_v), _adamw(*small_state, name="adamw_small")):
        dst.update(zip(small_names, _unpack_rows(slab, [wsh[n].shape for n in small_names])))

    return (loss, grad_x[None], *[g[n] for n in WEIGHTS], *[delta[n] for n in WEIGHTS],
            *[new_m[n] for n in WEIGHTS], *[new_v[n] for n in WEIGHTS])
```

```python
import math

import numpy as np
import jax
import jax.numpy as jnp
from jax import lax
from jax.experimental import pallas as pl
from jax.experimental.pallas import tpu as pltpu

f32, bf16 = jnp.float32, jnp.bfloat16

N_DEV = 8
D = 1024
HEADS = 8
NOPE, ROPE, QK = 64, 32, 96
HEAD_PAD = 128
LORA = 256
CONV_CH = 512
MIX_IN_PAD = 2176
FFN_H = 2816
GROUPS, GROUP, STATE = 64, 16, 64
EPS = 1e-6
ROPE_THETA = 10000.0
ADAM_LR, ADAM_B1, ADAM_B2, ADAM_EPS, ADAM_WD, ADAM_STEP = 0.001, 0.9, 0.999, 1e-08, 0.01, 10
LANES = 128
PAIR_LANES = 2 * LANES
VMEM_LIMIT = 56 << 20
MM_VMEM_BUDGET = 40 << 20
NEG = -1e30

BIG = (
    ("mix_w_in", (2, 1024, 260), 2), ("w_uq", (2, 256, 96), 2), ("w_ukv", (2, 256, 128), 2),
    ("mix_w_out", (2, 128, 1024), 1), ("ssm_w_in", (2, 128, 1024), 1), ("w_glu", (2, 1024, 256), 2),
    ("ffn_w_up", (4, 1024, 704), 2), ("ffn_w_down", (4, 352, 1024), 1))
REPL = (("attn_norm", (2, 1024)), ("cq_norm", (2, 256)), ("ckv_norm", (2, 256)), ("q_gain", (2, 96)),
        ("k_gain", (2, 96)), ("lambda_re", (2, 64, 64)), ("lambda_im", (2, 64, 64)), ("log_step", (2, 64)),
        ("b_re", (2, 64, 64, 16)), ("b_im", (2, 64, 64, 16)), ("c_re", (2, 64, 16, 64)), ("c_im", (2, 64, 16, 64)),
        ("ffn_norm", (4, 1024)))
SMALL = (("sconv_w", (2, 3, 64), 2), ("ssm_norm", (2, 128), 1), ("d_skip", (2, 128), 1), ("ffn_conv_w", (4, 3, 704), 2))
WEIGHTS = ['attn_norm', 'mix_w_in', 'cq_norm', 'ckv_norm', 'w_uq', 'w_ukv', 'q_gain', 'k_gain', 'sconv_w', 'mix_w_out',
           'ssm_norm', 'ssm_w_in', 'lambda_re', 'lambda_im', 'log_step', 'b_re', 'b_im', 'c_re', 'c_im', 'd_skip',
           'w_glu', 'ffn_norm', 'ffn_w_up', 'ffn_conv_w', 'ffn_w_down']
BIG_ROWS = 5888
SMALL_FWD_ROWS = 80
SMALL_ROWS = 640


def _cparams(sem=None, **kw):
    return pltpu.CompilerParams(dimension_semantics=sem, vmem_limit_bytes=VMEM_LIMIT, **kw)


def _tile(n, target):
    best = 0
    for t in range(LANES, min(n, target) + 1, LANES):
        if n % t == 0:
            best = t
    return best if best else n


def _mm(a, b, *, ta=False, tb=False, out_dtype=f32, add=None, twin=False, name, tm=1024, tn=1536):
    m, k = (a.shape[1], a.shape[0]) if ta else a.shape
    n = b.shape[0] if tb else b.shape[1]
    assert (b.shape[1] if tb else b.shape[0]) == k
    tm = _tile(m, tm)
    tn_ = _tile(n, tn)
    tn = n if (tn_ < 256 and n <= 2304) else tn_

    def vmem_bytes(t):
        io = 2 * (tm * t * a.dtype.itemsize + t * tn * b.dtype.itemsize + tm * tn * (jnp.dtype(out_dtype).itemsize + 2 * twin))
        return io + (2 * tm * tn * 4 if add is not None else 0) + (tm * tn * 4 if t < k else 0)

    tk = next((t for t in [k] + [t for t in range(k - LANES, 0, -LANES) if k % t == 0] if vmem_bytes(t) <= MM_VMEM_BUDGET), LANES)
    nk = k // tk
    dn = (((0 if ta else 1,), (1 if tb else 0,)), ((), ()))

    def body(*refs):
        a_ref, b_ref = refs[:2]
        add_ref = refs[2] if add is not None else None
        o_ref = refs[3] if add is not None else refs[2]
        twin_ref = refs[4 if add is not None else 3] if twin else None
        part = lax.dot_general(a_ref[...].astype(bf16), b_ref[...].astype(bf16), dn, preferred_element_type=f32)

        def finish(r):
            if add is not None:
                r = r + add_ref[...].astype(f32)
            o_ref[...] = r.astype(out_dtype)
            if twin:
                twin_ref[...] = r.astype(bf16)

        if nk == 1:
            finish(part)
            return
        acc = refs[-1]
        kk = pl.program_id(2)

        @pl.when(kk == 0)
        def _():
            acc[...] = part

        @pl.when(kk > 0)
        def _():
            acc[...] += part

        @pl.when(kk == nk - 1)
        def _():
            finish(acc[...])

    a_spec = pl.BlockSpec((tk, tm), lambda i, j, kk: (kk, i)) if ta else pl.BlockSpec((tm, tk), lambda i, j, kk: (i, kk))
    b_spec = pl.BlockSpec((tn, tk), lambda i, j, kk: (j, kk)) if tb else pl.BlockSpec((tk, tn), lambda i, j, kk: (kk, j))
    in_specs, args = [a_spec, b_spec], [a, b]
    if add is not None:
        in_specs.append(pl.BlockSpec((tm, tn), lambda i, j, kk: (i, j)))
        args.append(add)
    o_spec, o_shape = pl.BlockSpec((tm, tn), lambda i, j, kk: (i, j)), jax.ShapeDtypeStruct((m, n), out_dtype)
    return pl.pallas_call(
        body, name=name, grid=(m // tm, n // tn, nk), in_specs=in_specs,
        out_specs=[o_spec, o_spec] if twin else o_spec,
        out_shape=[o_shape, jax.ShapeDtypeStruct((m, n), bf16)] if twin else o_shape,
        scratch_shapes=[pltpu.VMEM((tm, tn), f32)] if nk > 1 else [],
        compiler_params=_cparams(("parallel", "parallel", "arbitrary")))(*args)


def _bd_nn(a, w, *, out_dtype=f32, name, ts=512):
    s = a.shape[0]
    nb, ka, no = w.shape
    ts = min(ts, s)

    def body(a_ref, w_ref, o_ref):
        o_ref[...] = jnp.dot(a_ref[...].astype(bf16), w_ref[0].astype(bf16), preferred_element_type=f32).astype(out_dtype)

    return pl.pallas_call(
        body, name=name, grid=(nb, s // ts),
        in_specs=[pl.BlockSpec((ts, ka), lambda b, i: (i, b)), pl.BlockSpec((1, ka, no), lambda b, i: (b, 0, 0))],
        out_specs=pl.BlockSpec((ts, no), lambda b, i: (i, b)),
        out_shape=jax.ShapeDtypeStruct((s, nb * no), out_dtype),
        compiler_params=_cparams(("parallel", "parallel")))(a, w)


def _bd_tn_diag(a, g, *, name, ts=512):
    s = a.shape[0]
    nb = a.shape[1] // LANES
    ts = min(ts, s)
    ni = s // ts

    def body(a_ref, g_ref, o_ref, acc):
        i = pl.program_id(1)
        part = lax.dot_general(a_ref[...].astype(bf16), g_ref[...].astype(bf16), (((0,), (0,)), ((), ())),
                               preferred_element_type=f32)

        @pl.when(i == 0)
        def _():
            acc[...] = part

        @pl.when(i > 0)
        def _():
            acc[...] += part

        @pl.when(i == ni - 1)
        def _():
            for j in range(8):
                o_ref[0, j] = acc[j * GROUP:(j + 1) * GROUP, (j // 2) * PAIR_LANES:(j // 2 + 1) * PAIR_LANES]

    return pl.pallas_call(
        body, name=name, grid=(nb, ni),
        in_specs=[pl.BlockSpec((ts, LANES), lambda b, i: (i, b)), pl.BlockSpec((ts, 8 * LANES), lambda b, i: (i, b))],
        out_specs=pl.BlockSpec((1, 8, GROUP, PAIR_LANES), lambda b, i: (b, 0, 0, 0)),
        out_shape=jax.ShapeDtypeStruct((nb, 8, GROUP, PAIR_LANES), f32),
        scratch_shapes=[pltpu.VMEM((LANES, 8 * LANES), f32)],
        compiler_params=_cparams(("parallel", "arbitrary")))(a, g)


def _rms_fwd(x, g, *, col=0, name, ts=512):
    s, d = x.shape[0], g.shape[1]
    ts = min(ts, s)

    def body(x_ref, g_ref, o_ref):
        xv = x_ref[...].astype(f32)
        r = lax.rsqrt(jnp.mean(xv * xv, axis=-1, keepdims=True) + EPS)
        o_ref[...] = (xv * r * g_ref[...]).astype(bf16)

    return pl.pallas_call(
        body, name=name, grid=(s // ts,),
        in_specs=[pl.BlockSpec((ts, d), lambda i: (i, col)), pl.BlockSpec((1, d), lambda i: (0, 0))],
        out_specs=pl.BlockSpec((ts, d), lambda i: (i, 0)),
        out_shape=jax.ShapeDtypeStruct((s, d), bf16),
        compiler_params=_cparams(("parallel",)))(x, g)


def _rms_bwd(dy, x, g, *, col=0, res=None, out_dtype=f32, name, ts=512):
    s, d = dy.shape
    ts = min(ts, s)
    twin = res is not None

    def body(*refs):
        if twin:
            dy_ref, x_ref, g_ref, res_ref, dx_ref, dxb_ref, dg_ref = refs
        else:
            dy_ref, x_ref, g_ref, dx_ref, dg_ref = refs

        @pl.when(pl.program_id(0) == 0)
        def _():
            dg_ref[...] = jnp.zeros_like(dg_ref)

        xv, dyv = x_ref[...].astype(f32), dy_ref[...].astype(f32)
        r = lax.rsqrt(jnp.mean(xv * xv, axis=-1, keepdims=True) + EPS)
        dyg = dyv * g_ref[...]
        dx = r * dyg - xv * (r * r * r) * jnp.mean(xv * dyg, axis=-1, keepdims=True)
        if twin:
            dx = dx + res_ref[...]
            dxb_ref[...] = dx.astype(bf16)
        dx_ref[...] = dx.astype(out_dtype)
        dg_ref[...] += jnp.sum(dyv * xv * r, axis=0, keepdims=True)

    row, vec = pl.BlockSpec((ts, d), lambda i: (i, 0)), pl.BlockSpec((1, d), lambda i: (0, 0))
    in_specs, args = [row, pl.BlockSpec((ts, d), lambda i: (i, col)), vec], [dy, x, g]
    out_specs, out_shape = [row], [jax.ShapeDtypeStruct((s, d), out_dtype)]
    if twin:
        in_specs.append(row)
        args.append(res)
        out_specs.append(row)
        out_shape.append(jax.ShapeDtypeStruct((s, d), bf16))
    return pl.pallas_call(
        body, name=name, grid=(s // ts,), in_specs=in_specs, out_specs=out_specs + [vec],
        out_shape=out_shape + [jax.ShapeDtypeStruct((1, d), f32)],
        compiler_params=_cparams(("arbitrary",)))(*args)


def _swap_halves(z):
    lane = lax.broadcasted_iota(jnp.int32, z.shape, 1)
    return jnp.where(lane < NOPE + ROPE // 2, pltpu.roll(z, LANES - ROPE // 2, axis=1), pltpu.roll(z, ROPE // 2, axis=1))


def _rope_tables(s):
    inv_freq = 1.0 / (ROPE_THETA ** (jnp.arange(0, ROPE, 2, dtype=f32) / ROPE))
    ang = jnp.arange(s, dtype=f32)[:, None] * inv_freq[None, :]
    cos, sin = jnp.cos(ang), jnp.sin(ang)
    one, zero = jnp.ones((s, NOPE), f32), jnp.zeros((s, NOPE), f32)
    pad1, pad0 = jnp.ones((s, HEAD_PAD - QK), f32), jnp.zeros((s, HEAD_PAD - QK), f32)
    return jnp.concatenate([one, cos, cos, pad1], 1), jnp.concatenate([zero, -sin, sin, pad0], 1)


def _qk_prep_fwd(q_raw, kv_raw, proj, qg, kg, cos_t, sin_t, *, name, ts=512):
    s = q_raw.shape[0]
    ts = min(ts, s)
    rope_blk = (MIX_IN_PAD - HEAD_PAD) // HEAD_PAD

    def body(q_ref, kv_ref, kr_ref, qg_ref, kg_ref, c_ref, s_ref, qo_ref, ko_ref, vo_ref):
        lane = lax.broadcasted_iota(jnp.int32, (ts, HEAD_PAD), 1)
        cosv, sinv = c_ref[...], s_ref[...]

        def norm_rope(z, gain):
            r = lax.rsqrt(jnp.sum(z * z, axis=-1, keepdims=True) * (1.0 / QK) + EPS)
            zn = z * r * gain
            return zn * cosv + _swap_halves(zn) * sinv

        kvv = kv_ref[...]
        qo_ref[...] = (norm_rope(q_ref[...], qg_ref[...]) * _Q_FOLD).astype(bf16)
        ko_ref[...] = norm_rope(jnp.where(lane < NOPE, kvv, kr_ref[...]), kg_ref[...]).astype(bf16)
        vo_ref[...] = jnp.where(lane >= NOPE, kvv, 0.0).astype(bf16)

    head = pl.BlockSpec((ts, HEAD_PAD), lambda i, h: (i, h))
    row = pl.BlockSpec((ts, HEAD_PAD), lambda i, h: (i, 0))
    vec = pl.BlockSpec((1, HEAD_PAD), lambda i, h: (0, 0))
    out = jax.ShapeDtypeStruct((s, HEADS * HEAD_PAD), bf16)
    return pl.pallas_call(
        body, name=name, grid=(s // ts, HEADS),
        in_specs=[head, head, pl.BlockSpec((ts, HEAD_PAD), lambda i, h: (i, rope_blk)), vec, vec, row, row],
        out_specs=[head, head, head], out_shape=[out, out, out],
        compiler_params=_cparams(("parallel", "parallel")))(q_raw, kv_raw, proj, qg, kg, cos_t, sin_t)


def _qk_prep_bwd(dq, dk, dv, q_raw, kv_raw, proj, qg, kg, cos_t, sin_t, *, name, ts=512):
    s = q_raw.shape[0]
    ts = min(ts, s)
    rope_blk = (MIX_IN_PAD - HEAD_PAD) // HEAD_PAD

    def body(dq_ref, dk_ref, dv_ref, q_ref, kv_ref, kr_ref, qg_ref, kg_ref, c_ref, s_ref,
             dqr_ref, dkvr_ref, dkr_ref, dqg_ref, dkg_ref):
        i, h = pl.program_id(0), pl.program_id(1)
        lane = lax.broadcasted_iota(jnp.int32, (ts, HEAD_PAD), 1)
        is_rope = (lane >= NOPE) & (lane < QK)
        cosv, sinv = c_ref[...], s_ref[...]

        @pl.when((i == 0) & (h == 0))
        def _():
            dqg_ref[...] = jnp.zeros_like(dqg_ref)
            dkg_ref[...] = jnp.zeros_like(dkg_ref)

        @pl.when(h == 0)
        def _():
            dkr_ref[...] = jnp.zeros_like(dkr_ref)

        def back(dout, z, gain):
            dzn = dout * cosv + jnp.where(is_rope, _swap_halves(dout * sinv), 0.0)
            r = lax.rsqrt(jnp.sum(z * z, axis=-1, keepdims=True) * (1.0 / QK) + EPS)
            dzg = dzn * gain
            dz = r * dzg - z * (r * r * r) * (jnp.sum(z * dzg, axis=-1, keepdims=True) * (1.0 / QK))
            return dz, jnp.sum(dzn * z * r, axis=0, keepdims=True)

        dqz, dqg = back(dq_ref[...].astype(f32), q_ref[...], qg_ref[...])
        dqr_ref[...] = dqz.astype(bf16)
        dqg_ref[...] += dqg
        kvv = kv_ref[...]
        dkz, dkg = back(dk_ref[...].astype(f32), jnp.where(lane < NOPE, kvv, kr_ref[...]), kg_ref[...])
        dkg_ref[...] += dkg
        dkvr_ref[...] = jnp.where(lane < NOPE, dkz, dv_ref[...].astype(f32)).astype(bf16)
        dkr_ref[...] += jnp.where(is_rope, dkz, 0.0)

    head = pl.BlockSpec((ts, HEAD_PAD), lambda i, h: (i, h))
    row = pl.BlockSpec((ts, HEAD_PAD), lambda i, h: (i, 0))
    vec = pl.BlockSpec((1, HEAD_PAD), lambda i, h: (0, 0))
    wide = jax.ShapeDtypeStruct((s, HEADS * HEAD_PAD), bf16)
    return pl.pallas_call(
        body, name=name, grid=(s // ts, HEADS),
        in_specs=[head, head, head, head, head, pl.BlockSpec((ts, HEAD_PAD), lambda i, h: (i, rope_blk)), vec, vec, row, row],
        out_specs=[head, head, row, vec, vec],
        out_shape=[wide, wide, jax.ShapeDtypeStruct((s, HEAD_PAD), f32), jax.ShapeDtypeStruct((1, HEAD_PAD), f32),
                   jax.ShapeDtypeStruct((1, HEAD_PAD), f32)],
        compiler_params=_cparams(("arbitrary", "arbitrary")))(dq, dk, dv, q_raw, kv_raw, proj, qg, kg, cos_t, sin_t)


_NT = (((1,), (1,)), ((), ()))
_SCALE = QK ** -0.5
_LOG2E = math.log2(math.e)
_Q_FOLD = _SCALE * _LOG2E
FLASH_TILE = 1024


def _flash_fwd(q, k, v, *, name, tq=FLASH_TILE):
    s = q.shape[0]
    tq = min(tq, s)

    def body(q_ref, k_ref, v_ref, o_ref, lse_ref):
        i = pl.program_id(1)
        qv = q_ref[...]

        def step(j, carry, masked):
            m, l, acc = carry
            st = pl.multiple_of(j * tq, tq)
            kj, vj = k_ref[pl.ds(st, tq), :], v_ref[pl.ds(st, tq), :]
            sc = lax.dot_general(qv, kj, _NT, preferred_element_type=f32)
            if masked:
                rr = lax.broadcasted_iota(jnp.int32, (tq, tq), 0)
                cc = lax.broadcasted_iota(jnp.int32, (tq, tq), 1)
                sc = jnp.where(cc <= rr, sc, NEG)
            m_new = jnp.maximum(m, jnp.max(sc, axis=-1, keepdims=True))
            p = jnp.exp2(sc - m_new)
            alpha = jnp.exp2(m - m_new)
            l = alpha * l + jnp.sum(p, axis=-1, keepdims=True)
            acc = alpha * acc + jnp.dot(p.astype(bf16), vj, preferred_element_type=f32)
            return m_new, l, acc

        init = (jnp.full((tq, 1), NEG, f32), jnp.zeros((tq, 1), f32), jnp.zeros((tq, HEAD_PAD), f32))
        carry = lax.fori_loop(0, i, lambda j, c: step(j, c, False), init)
        m, l, acc = step(i, carry, True)
        o_ref[...] = (acc / l).astype(bf16)
        lse_ref[0] = m + jnp.log2(l)

    blk = pl.BlockSpec((tq, HEAD_PAD), lambda h, i: (i, h))
    full = pl.BlockSpec((s, HEAD_PAD), lambda h, i: (0, h))
    return pl.pallas_call(
        body, name=name, grid=(HEADS, s // tq), in_specs=[blk, full, full],
        out_specs=[blk, pl.BlockSpec((1, tq, 1), lambda h, i: (h, i, 0))],
        out_shape=[jax.ShapeDtypeStruct((s, HEADS * HEAD_PAD), bf16), jax.ShapeDtypeStruct((HEADS, s, 1), f32)],
        compiler_params=_cparams(("parallel", "arbitrary")))(q, k, v)


def _flash_bwd_dq(q, k, v, o, do, lse, *, name, tq=FLASH_TILE):
    s = q.shape[0]
    tq = min(tq, s)

    def body(q_ref, k_ref, v_ref, o_ref, do_ref, lse_ref, dq_ref, dl_ref):
        i = pl.program_id(1)
        qv = q_ref[...]
        dov = do_ref[...].astype(f32)
        delta = jnp.sum(dov * o_ref[...].astype(f32), axis=-1, keepdims=True)
        dob = dov.astype(bf16)
        lsev = lse_ref[0]

        def step(j, acc, masked):
            st = pl.multiple_of(j * tq, tq)
            kj, vj = k_ref[pl.ds(st, tq), :], v_ref[pl.ds(st, tq), :]
            sc = lax.dot_general(qv, kj, _NT, preferred_element_type=f32)
            p = jnp.exp2(sc - lsev)
            if masked:
                rr = lax.broadcasted_iota(jnp.int32, (tq, tq), 0)
                cc = lax.broadcasted_iota(jnp.int32, (tq, tq), 1)
                p = jnp.where(cc <= rr, p, 0.0)
            dp = lax.dot_general(dob, vj, _NT, preferred_element_type=f32)
            ds = p * (dp - delta)
            return acc + jnp.dot(ds.astype(bf16), kj, preferred_element_type=f32)

        acc = lax.fori_loop(0, i, lambda j, c: step(j, c, False), jnp.zeros((tq, HEAD_PAD), f32))
        dq_ref[...] = step(i, acc, True) * _SCALE
        dl_ref[0] = delta

    blk = pl.BlockSpec((tq, HEAD_PAD), lambda h, i: (i, h))
    full = pl.BlockSpec((s, HEAD_PAD), lambda h, i: (0, h))
    col = pl.BlockSpec((1, tq, 1), lambda h, i: (h, i, 0))
    return pl.pallas_call(
        body, name=name, grid=(HEADS, s // tq), in_specs=[blk, full, full, blk, blk, col],
        out_specs=[blk, col],
        out_shape=[jax.ShapeDtypeStruct((s, HEADS * HEAD_PAD), f32), jax.ShapeDtypeStruct((HEADS, s, 1), f32)],
        compiler_params=_cparams(("parallel", "arbitrary")))(q, k, v, o, do, lse)


def _flash_bwd_dkv(q, k, v, do, lse_row, delta_row, *, name, tk=FLASH_TILE):
    s = q.shape[0]
    tk = min(tk, s)
    nblk = s // tk

    def body(q_ref, k_ref, v_ref, do_ref, lse_ref, dl_ref, dk_ref, dv_ref):
        j = pl.program_id(1)
        kv_, vv = k_ref[...], v_ref[...]

        def step(i, carry, masked):
            dk, dv = carry
            st = pl.multiple_of(i * tk, tk)
            qi = q_ref[pl.ds(st, tk), :]
            doi = do_ref[pl.ds(st, tk), :].astype(bf16)
            lse_i = lse_ref[0, :, pl.ds(st, tk)]
            dl_i = dl_ref[0, :, pl.ds(st, tk)]
            st_ = lax.dot_general(kv_, qi, _NT, preferred_element_type=f32)
            pt = jnp.exp2(st_ - lse_i)
            if masked:
                kk = lax.broadcasted_iota(jnp.int32, (tk, tk), 0)
                qq = lax.broadcasted_iota(jnp.int32, (tk, tk), 1)
                pt = jnp.where(kk <= qq, pt, 0.0)
            dv = dv + jnp.dot(pt.astype(bf16), doi, preferred_element_type=f32)
            dpt = lax.dot_general(vv, doi, _NT, preferred_element_type=f32)
            dst = pt * (dpt - dl_i)
            dk = dk + jnp.dot(dst.astype(bf16), qi, preferred_element_type=f32)
            return dk, dv

        zero = jnp.zeros((tk, HEAD_PAD), f32)
        carry = step(j, (zero, zero), True)
        dk, dv = lax.fori_loop(j + 1, nblk, lambda i, c: step(i, c, False), carry)
        dk_ref[...] = dk * (1.0 / _LOG2E)
        dv_ref[...] = dv

    blk = pl.BlockSpec((tk, HEAD_PAD), lambda h, j: (j, h))
    full = pl.BlockSpec((s, HEAD_PAD), lambda h, j: (0, h))
    rowv = pl.BlockSpec((1, 1, s), lambda h, j: (h, 0, 0))
    out = jax.ShapeDtypeStruct((s, HEADS * HEAD_PAD), f32)
    return pl.pallas_call(
        body, name=name, grid=(HEADS, nblk), in_specs=[full, blk, blk, full, rowv, rowv],
        out_specs=[blk, blk], out_shape=[out, out],
        compiler_params=_cparams(("parallel", "arbitrary")))(q, k, v, do, lse_row, delta_row)


SUBLANES = 8


def _shift_down(x, d):
    r = pltpu.roll(x, d, axis=0)
    t = lax.broadcasted_iota(jnp.int32, (SUBLANES, x.shape[1]), 0)
    head = jnp.where(t < d, 0.0, r[:SUBLANES])
    return head if x.shape[0] == SUBLANES else jnp.concatenate([head, r[SUBLANES:]], axis=0)


def _shift_up(x, d):
    s = x.shape[0]
    r = pltpu.roll(x, s - d, axis=0)
    t = lax.broadcasted_iota(jnp.int32, (SUBLANES, x.shape[1]), 0)
    tail = jnp.where(t >= SUBLANES - d, 0.0, r[s - SUBLANES:])
    return tail if s == SUBLANES else jnp.concatenate([r[:s - SUBLANES], tail], axis=0)


def _taps(w_ref):
    return w_ref[0:1, :], w_ref[1:2, :], w_ref[2:3, :]


def _conv3(u, w):
    u1, u2 = _shift_down(u, 1), _shift_down(u, 2)
    return w[0] * u2 + w[1] * u1 + w[2] * u, (u1, u2)


def _conv3_t(g, w):
    return w[2] * g + w[1] * _shift_up(g, 1) + w[0] * _shift_up(g, 2)


def _conv3_dw(dw_ref, g, u, shifted):
    dw_ref[0:1, :] = jnp.sum(g * shifted[1], axis=0, keepdims=True)
    dw_ref[1:2, :] = jnp.sum(g * shifted[0], axis=0, keepdims=True)
    dw_ref[2:3, :] = jnp.sum(g * u, axis=0, keepdims=True)


_GB, _GC, _CI = 512 // LANES, 1024 // LANES, 1536 // LANES


def _sconv_fwd(proj, w, *, name):
    s = proj.shape[0]

    def body(gb_ref, gc_ref, ci_ref, w_ref, o_ref):
        o_ref[...] = (gb_ref[...] * _conv3(gc_ref[...] * ci_ref[...], _taps(w_ref))[0]).astype(bf16)

    col = lambda off: pl.BlockSpec((s, LANES), lambda j: (0, off + j))
    return pl.pallas_call(
        body, name=name, grid=(CONV_CH // LANES,),
        in_specs=[col(_GB), col(_GC), col(_CI), pl.BlockSpec((3, LANES), lambda j: (0, j))],
        out_specs=pl.BlockSpec((s, LANES), lambda j: (0, j)),
        out_shape=jax.ShapeDtypeStruct((s, CONV_CH), bf16),
        compiler_params=_cparams(("parallel",)))(proj, proj, proj, w)


def _sconv_bwd(dmix, proj, w, *, name):
    s = proj.shape[0]

    def body(do_ref, gb_ref, gc_ref, ci_ref, w_ref, dgb_ref, dgc_ref, dci_ref, dw_ref):
        wv, gc, ci, do = _taps(w_ref), gc_ref[...], ci_ref[...], do_ref[...].astype(f32)
        u = gc * ci
        conv, shifted = _conv3(u, wv)
        dgb_ref[...] = (do * conv).astype(bf16)
        dc = do * gb_ref[...]
        du = _conv3_t(dc, wv)
        dgc_ref[...] = (du * ci).astype(bf16)
        dci_ref[...] = (du * gc).astype(bf16)
        _conv3_dw(dw_ref, dc, u, shifted)

    col = lambda off: pl.BlockSpec((s, LANES), lambda j: (0, off + j))
    out = jax.ShapeDtypeStruct((s, CONV_CH), bf16)
    return pl.pallas_call(
        body, name=name, grid=(CONV_CH // LANES,),
        in_specs=[col(HEADS), col(_GB), col(_GC), col(_CI), pl.BlockSpec((3, LANES), lambda j: (0, j))],
        out_specs=[col(0), col(0), col(0), pl.BlockSpec((3, LANES), lambda j: (0, j))],
        out_shape=[out, out, out, jax.ShapeDtypeStruct((3, CONV_CH), f32)],
        compiler_params=_cparams(("parallel",)))(dmix, proj, proj, proj, w)


def _ffn_act_fwd(zg, zv, cwg, cwv, *, name):
    s, f = zg.shape

    def body(zg_ref, zv_ref, wg_ref, wv_ref, o_ref):
        o_ref[...] = (jax.nn.silu(_conv3(zg_ref[...], _taps(wg_ref))[0]) * _conv3(zv_ref[...], _taps(wv_ref))[0]).astype(bf16)

    col = pl.BlockSpec((s, LANES), lambda j: (0, j))
    wsp = pl.BlockSpec((3, LANES), lambda j: (0, j))
    return pl.pallas_call(
        body, name=name, grid=(f // LANES,), in_specs=[col, col, wsp, wsp], out_specs=col,
        out_shape=jax.ShapeDtypeStruct((s, f), bf16), compiler_params=_cparams(("parallel",)))(zg, zv, cwg, cwv)


def _ffn_act_bwd(da, zg, zv, cwg, cwv, *, name):
    s, f = zg.shape

    def body(da_ref, zg_ref, zv_ref, wg_ref, wv_ref, dzg_ref, dzv_ref, dwg_ref, dwv_ref):
        wg, wv, zgv, zvv, dav = _taps(wg_ref), _taps(wv_ref), zg_ref[...], zv_ref[...], da_ref[...].astype(f32)
        (ug, zg_shifted), (uv, zv_shifted) = _conv3(zgv, wg), _conv3(zvv, wv)
        sg = jax.nn.sigmoid(ug)
        dug = dav * uv * (sg * (1.0 + ug * (1.0 - sg)))
        duv = dav * (ug * sg)
        dzg_ref[...] = _conv3_t(dug, wg).astype(bf16)
        dzv_ref[...] = _conv3_t(duv, wv).astype(bf16)
        _conv3_dw(dwg_ref, dug, zgv, zg_shifted)
        _conv3_dw(dwv_ref, duv, zvv, zv_shifted)

    col = pl.BlockSpec((s, LANES), lambda j: (0, j))
    wsp = pl.BlockSpec((3, LANES), lambda j: (0, j))
    act, wsh = jax.ShapeDtypeStruct((s, f), bf16), jax.ShapeDtypeStruct((3, f), f32)
    return pl.pallas_call(
        body, name=name, grid=(f // LANES,), in_specs=[col, col, col, wsp, wsp], out_specs=[col, col, wsp, wsp],
        out_shape=[act, act, wsh, wsh], compiler_params=_cparams(("parallel",)))(da, zg, zv, cwg, cwv)


def _expand_mat():
    return jnp.asarray(np.kron(np.eye(STATE, dtype=np.float32), np.ones((1, GROUP), np.float32)))


def _disc_fn(lr, li, ls, br, bi, e):
    dt = jnp.exp(ls)
    mag = jnp.exp(lr * dt)
    ar, ai = mag * jnp.cos(li * dt), mag * jnp.sin(li * dt)
    nr, ni = ar - 1.0, ai
    den = lr * lr + li * li
    zr, zi = (nr * lr + ni * li) / den, (ni * lr - nr * li) / den
    zrr = jnp.dot(zr, e, precision=lax.Precision.HIGHEST, preferred_element_type=f32)
    zir = jnp.dot(zi, e, precision=lax.Precision.HIGHEST, preferred_element_type=f32)
    return ar, ai, zrr * br - zir * bi, zrr * bi + zir * br


def _disc_fwd(lr, li, ls, br, bi, *, name):
    def body(lr_ref, li_ref, ls_ref, br_ref, bi_ref, e_ref, ar_ref, ai_ref, bbr_ref, bbi_ref):
        ar, ai, bbr, bbi = _disc_fn(lr_ref[...], li_ref[...], ls_ref[...], br_ref[...], bi_ref[...], e_ref[...])
        ar_ref[...], ai_ref[...], bbr_ref[...], bbi_ref[...] = ar, ai, bbr, bbi

    sq, wide = jax.ShapeDtypeStruct((GROUPS, STATE), f32), jax.ShapeDtypeStruct((GROUPS, STATE * GROUP), f32)
    return pl.pallas_call(body, name=name, out_shape=[sq, sq, wide, wide],
                          compiler_params=_cparams())(lr, li, ls, br, bi, _expand_mat())


def _disc_bwd(lr, li, ls, br, bi, dar, dai, dbbr, dbbi, *, name):
    def body(lr_ref, li_ref, ls_ref, br_ref, bi_ref, e_ref, dar_ref, dai_ref, dbbr_ref, dbbi_ref,
             dlr_ref, dli_ref, dls_ref, dbr_ref, dbi_ref):
        ev = e_ref[...]
        _, vjp = jax.vjp(lambda a, b, c, d_, e_: _disc_fn(a, b, c, d_, e_, ev),
                         lr_ref[...], li_ref[...], ls_ref[...], br_ref[...], bi_ref[...])
        dlr, dli, dls, dbr, dbi = vjp((dar_ref[...], dai_ref[...], dbbr_ref[...], dbbi_ref[...]))
        dlr_ref[...], dli_ref[...], dls_ref[...], dbr_ref[...], dbi_ref[...] = dlr, dli, dls, dbr, dbi

    sq, wide = jax.ShapeDtypeStruct((GROUPS, STATE), f32), jax.ShapeDtypeStruct((GROUPS, STATE * GROUP), f32)
    return pl.pallas_call(body, name=name, out_shape=[sq, sq, jax.ShapeDtypeStruct((GROUPS, 1), f32), wide, wide],
                          compiler_params=_cparams())(lr, li, ls, br, bi, _expand_mat(), dar, dai, dbbr, dbbi)


SCAN_TILE = 64
SCAN_PAIRS = 2


def _tile_shift(v, d, reverse):
    if d % 8:
        return _shift_up(v, d) if reverse else _shift_down(v, d)
    z = jnp.zeros((d, v.shape[1]), v.dtype)
    return jnp.concatenate([v[d:], z], axis=0) if reverse else jnp.concatenate([z, v[:v.shape[0] - d]], axis=0)


def _tile_scan(r, i, pows, reverse):
    d = 1
    for br, bi in pows:
        rs, is_ = _tile_shift(r, d, reverse), _tile_shift(i, d, reverse)
        r, i = r + br * rs - bi * is_, i + br * is_ + bi * rs
        d *= 2
    return r, i


def _scan_setup(ar, ai, reverse):
    if reverse:
        ai = -ai
    pows, br, bi, d = [], ar, ai, 1
    while d < SCAN_TILE:
        pows.append((br, bi))
        br, bi, d = br * br - bi * bi, 2.0 * br * bi, 2 * d
    row = lax.broadcasted_iota(jnp.int32, (SCAN_TILE, LANES), 0)
    hit = row == (SCAN_TILE - 1 if reverse else 0)
    pr, pi = _tile_scan(jnp.where(hit, ar, 0.0), jnp.where(hit, ai, 0.0), pows, reverse)
    return pows, pr, pi


def _carry_in(r, i, pr, pi, cr, ci):
    crb, cib = jnp.broadcast_to(cr, r.shape), jnp.broadcast_to(ci, i.shape)
    return r + pr * crb - pi * cib, i + pr * cib + pi * crb


def _pair_cols(q):
    return slice(q * PAIR_LANES, q * PAIR_LANES + LANES), slice(q * PAIR_LANES + LANES, (q + 1) * PAIR_LANES)


_SCAN_W = SCAN_PAIRS * PAIR_LANES


def _scan_specs(s, w):
    per = w.shape[2] // _SCAN_W
    src = pl.BlockSpec((s, LANES), lambda g: (0, g // per))
    mat = pl.BlockSpec((1, LANES, _SCAN_W), lambda g: (g // per, 0, g % per))
    col = pl.BlockSpec((s, _SCAN_W), lambda g: (0, g))
    vec = pl.BlockSpec((SCAN_PAIRS, 1, LANES), lambda g: (g, 0, 0))
    return src, mat, col, vec, (w.shape[0] * per,)


def _scan_fwd(u, wb, ar, ai, *, name):
    s = u.shape[0]
    nt = s // SCAN_TILE

    def body(u_ref, w_ref, ar_ref, ai_ref, x_ref):
        setups = [_scan_setup(ar_ref[q], ai_ref[q], False) for q in range(SCAN_PAIRS)]
        wv = w_ref[0]

        def tile_rows(k):
            return pl.ds(pl.multiple_of(k * SCAN_TILE, SCAN_TILE), SCAN_TILE)

        def tile_in(k):
            return jnp.dot(u_ref[tile_rows(k), :].astype(bf16), wv, preferred_element_type=f32)

        def step(k, carry):
            rows, bu = tile_rows(k), carry[-1]
            ahead = tile_in(jnp.minimum(k + 1, nt - 1))
            out = []
            for q, (pows, pr, pi) in enumerate(setups):
                rc, ic = _pair_cols(q)
                r, i = _tile_scan(bu[:, rc], bu[:, ic], pows, False)
                r, i = _carry_in(r, i, pr, pi, carry[2 * q], carry[2 * q + 1])
                x_ref[rows, rc] = r
                x_ref[rows, ic] = i
                out += [r[SCAN_TILE - 1:SCAN_TILE, :], i[SCAN_TILE - 1:SCAN_TILE, :]]
            return tuple(out) + (ahead,)

        lax.fori_loop(0, nt, step, tuple(jnp.zeros((1, LANES), f32) for _ in range(2 * SCAN_PAIRS)) + (tile_in(0),))

    src, mat, col, vec, grid = _scan_specs(s, wb)
    return pl.pallas_call(body, name=name, grid=grid, in_specs=[src, mat, vec, vec], out_specs=col,
                          out_shape=jax.ShapeDtypeStruct((s, wb.shape[0] * wb.shape[2]), f32),
                          compiler_params=_cparams(("parallel",)))(u, wb, ar, ai)


def _scan_bwd(dy, cbt, x, ar, ai, *, name):
    s = dy.shape[0]
    nt = s // SCAN_TILE

    def fold(v):
        out = v[0:8]
        for r in range(8, SCAN_TILE, 8):
            out = out + v[r:r + 8]
        return out

    def body(dy_ref, w_ref, x_ref, ar_ref, ai_ref, g_ref, dar_ref, dai_ref):
        setups = [_scan_setup(ar_ref[q], ai_ref[q], True) for q in range(SCAN_PAIRS)]
        row = lax.broadcasted_iota(jnp.int32, (SCAN_TILE, LANES), 0)
        wv = w_ref[0]

        def tile_in(k):
            return jnp.dot(dy_ref[pl.ds(pl.multiple_of(k * SCAN_TILE, SCAN_TILE), SCAN_TILE), :], wv, preferred_element_type=f32)

        def step(kk, carry):
            k = nt - 1 - kk
            start = pl.multiple_of(k * SCAN_TILE, SCAN_TILE)
            rows = pl.ds(start, SCAN_TILE)
            prev8 = pl.ds(pl.multiple_of(jnp.maximum(start - 8, 0), 8), 8)
            dx = carry[-1]
            ahead = tile_in(jnp.maximum(k - 1, 0))

            def before(cols):
                first = jnp.where(k > 0, x_ref[prev8, cols][7:8, :], 0.0)
                return jnp.where(row == 0, first, pltpu.roll(x_ref[rows, cols], 1, axis=0))

            out = []
            for q, (pows, pr, pi) in enumerate(setups):
                rc, ic = _pair_cols(q)
                cr, ci, acc_r, acc_i = carry[4 * q:4 * q + 4]
                gr, gi = _tile_scan(dx[:, rc], dx[:, ic], pows, True)
                gr, gi = _carry_in(gr, gi, pr, pi, cr, ci)
                g_ref[rows, rc] = gr.astype(bf16)
                g_ref[rows, ic] = gi.astype(bf16)
                xr, xi = before(rc), before(ic)
                out += [gr[0:1, :], gi[0:1, :], acc_r + fold(gr * xr + gi * xi), acc_i + fold(gi * xr - gr * xi)]
            return tuple(out) + (ahead,)

        init = (jnp.zeros((1, LANES), f32), jnp.zeros((1, LANES), f32), jnp.zeros((8, LANES), f32), jnp.zeros((8, LANES), f32))
        res = lax.fori_loop(0, nt, step, init * SCAN_PAIRS + (tile_in(nt - 1),))
        for q in range(SCAN_PAIRS):
            dar_ref[q] = jnp.sum(res[4 * q + 2], axis=0, keepdims=True)
            dai_ref[q] = jnp.sum(res[4 * q + 3], axis=0, keepdims=True)

    src, mat, col, vec, grid = _scan_specs(s, cbt)
    vsh = jax.ShapeDtypeStruct((GROUPS // 2, 1, LANES), f32)
    return pl.pallas_call(body, name=name, grid=grid, in_specs=[src, mat, col, vec, vec],
                          out_specs=[col, vec, vec], out_shape=[jax.ShapeDtypeStruct(x.shape, bf16), vsh, vsh],
                          compiler_params=_cparams(("parallel",)))(dy, cbt, x, ar, ai)


_GELU_C = math.sqrt(2.0 / math.pi)


def _gelu_fwd(y, u, dsk, *, name, ts=512):
    s, d = y.shape
    ts = min(ts, s)

    def body(y_ref, u_ref, d_ref, o_ref):
        o_ref[...] = jax.nn.gelu(y_ref[...] + d_ref[...] * u_ref[...]).astype(bf16)

    row, vec = pl.BlockSpec((ts, d), lambda i: (i, 0)), pl.BlockSpec((1, d), lambda i: (0, 0))
    return pl.pallas_call(body, name=name, grid=(s // ts,), in_specs=[row, row, vec], out_specs=row,
                          out_shape=jax.ShapeDtypeStruct((s, d), bf16), compiler_params=_cparams(("parallel",)))(y, u, dsk)


def _gelu_bwd(dg, y, u, dsk, *, name, ts=512):
    s, d = y.shape
    ts = min(ts, s)

    def body(dg_ref, y_ref, u_ref, d_ref, dy_ref, du_ref, dd_ref):
        @pl.when(pl.program_id(0) == 0)
        def _():
            dd_ref[...] = jnp.zeros_like(dd_ref)

        uv, dv = u_ref[...], d_ref[...]
        z = y_ref[...] + dv * uv
        th = jnp.tanh(_GELU_C * (z + 0.044715 * z * z * z))
        dz = dg_ref[...] * (0.5 * (1.0 + th) + 0.5 * z * (1.0 - th * th) * _GELU_C * (1.0 + 3 * 0.044715 * z * z))
        dy_ref[...] = dz.astype(bf16)
        du_ref[...] = dz * dv
        dd_ref[...] += jnp.sum(dz * uv, axis=0, keepdims=True)

    row, vec = pl.BlockSpec((ts, d), lambda i: (i, 0)), pl.BlockSpec((1, d), lambda i: (0, 0))
    return pl.pallas_call(
        body, name=name, grid=(s // ts,), in_specs=[row, row, row, vec], out_specs=[row, row, vec],
        out_shape=[jax.ShapeDtypeStruct((s, d), bf16), jax.ShapeDtypeStruct((s, d), f32), jax.ShapeDtypeStruct((1, d), f32)],
        compiler_params=_cparams(("arbitrary",)))(dg, y, u, dsk)


def _glu_fwd(x, a, b, *, name, ts=512):
    s, d = x.shape
    ts = min(ts, s)

    def body(x_ref, a_ref, b_ref, o_ref):
        o_ref[...] = x_ref[...] + a_ref[...] * jax.nn.sigmoid(b_ref[...])

    row = pl.BlockSpec((ts, d), lambda i: (i, 0))
    return pl.pallas_call(body, name=name, grid=(s // ts,), in_specs=[row, row, row], out_specs=row,
                          out_shape=jax.ShapeDtypeStruct((s, d), f32), compiler_params=_cparams(("parallel",)))(x, a, b)


def _glu_bwd(dx, a, b, *, name, ts=512):
    s, d = dx.shape
    ts = min(ts, s)

    def body(dx_ref, a_ref, b_ref, da_ref, db_ref):
        sg = jax.nn.sigmoid(b_ref[...])
        dxv = dx_ref[...]
        da_ref[...] = (dxv * sg).astype(bf16)
        db_ref[...] = (dxv * a_ref[...] * sg * (1.0 - sg)).astype(bf16)

    row = pl.BlockSpec((ts, d), lambda i: (i, 0))
    out = jax.ShapeDtypeStruct((s, d), bf16)
    return pl.pallas_call(body, name=name, grid=(s // ts,), in_specs=[row, row, row], out_specs=[row, row],
                          out_shape=[out, out], compiler_params=_cparams(("parallel",)))(dx, a, b)


def _add(a, b, *, name, ts=512):
    s, d = a.shape
    ts = min(ts, s)

    def body(a_ref, b_ref, o_ref):
        o_ref[...] = (a_ref[...].astype(f32) + b_ref[...].astype(f32)).astype(bf16)

    row = pl.BlockSpec((ts, d), lambda i: (i, 0))
    return pl.pallas_call(body, name=name, grid=(s // ts,), in_specs=[row, row], out_specs=row,
                          out_shape=jax.ShapeDtypeStruct((s, d), bf16), compiler_params=_cparams(("parallel",)))(a, b)


def _loss_head(y, target, *, name, ts=512):
    s, d = y.shape
    ts = min(ts, s)

    def body(y_ref, t_ref, dy_ref, dyb_ref, l_ref):
        @pl.when(pl.program_id(0) == 0)
        def _():
            l_ref[...] = jnp.zeros_like(l_ref)

        e = y_ref[...] - t_ref[...]
        dy = e * (1.0 / d)
        dy_ref[...] = dy
        dyb_ref[...] = dy.astype(bf16)
        l_ref[...] += 0.5 * jnp.sum(jnp.mean(e * e, axis=-1, keepdims=True))

    row = pl.BlockSpec((ts, d), lambda i: (i, 0))
    return pl.pallas_call(
        body, name=name, grid=(s // ts,), in_specs=[row, row],
        out_specs=[row, row, pl.BlockSpec((8, LANES), lambda i: (0, 0))],
        out_shape=[jax.ShapeDtypeStruct((s, d), f32), jax.ShapeDtypeStruct((s, d), bf16), jax.ShapeDtypeStruct((8, LANES), f32)],
        compiler_params=_cparams(("arbitrary",)))(y, target)


def _adamw(w, g, m, v, *, name, tr=128):
    r, c = w.shape

    def body(w_ref, g_ref, m_ref, v_ref, d_ref, mo_ref, vo_ref):
        gv = g_ref[...]
        mn = ADAM_B1 * m_ref[...] + (1.0 - ADAM_B1) * gv
        vn = ADAM_B2 * v_ref[...] + (1.0 - ADAM_B2) * (gv * gv)
        m_hat = mn / (1.0 - ADAM_B1 ** ADAM_STEP)
        v_hat = vn / (1.0 - ADAM_B2 ** ADAM_STEP)
        d_ref[...] = -ADAM_LR * (m_hat / (jnp.sqrt(v_hat) + ADAM_EPS) + ADAM_WD * w_ref[...])
        mo_ref[...] = mn
        vo_ref[...] = vn

    row = pl.BlockSpec((tr, c), lambda i: (i, 0))
    out = jax.ShapeDtypeStruct((r, c), f32)
    return pl.pallas_call(body, name=name, grid=(r // tr,), in_specs=[row] * 4, out_specs=[row] * 3,
                          out_shape=[out, out, out], compiler_params=_cparams(("parallel",)))(w, g, m, v)


def _sum_slabs(land, *, name, tr=128):
    n, r, c = land.shape

    def body(l_ref, o_ref):
        acc = l_ref[0].astype(f32)
        for i in range(1, n):
            acc = acc + l_ref[i].astype(f32)
        o_ref[...] = acc

    return pl.pallas_call(body, name=name, grid=(r // tr,), in_specs=[pl.BlockSpec((n, tr, c), lambda i: (0, i, 0))],
                          out_specs=pl.BlockSpec((tr, c), lambda i: (i, 0)), out_shape=jax.ShapeDtypeStruct((r, c), f32),
                          compiler_params=_cparams(("parallel",)))(land)


def _pair_sum(g, theirs, *, name, tr=256):
    n, r, c = theirs.shape

    def body(c_ref, g_ref, t_ref, o_ref):
        o_ref[...] = (g_ref[...].astype(f32) + t_ref[...].astype(f32)).astype(bf16)

    blk = pl.BlockSpec((1, tr, c), lambda j, i, c_ref: (j, i, 0))
    mine = pl.BlockSpec((1, tr, c), lambda j, i, c_ref: (2 * j + c_ref[0], i, 0))
    return pl.pallas_call(
        body, name=name,
        grid_spec=pltpu.PrefetchScalarGridSpec(num_scalar_prefetch=1, grid=(n, r // tr), in_specs=[mine, blk], out_specs=blk),
        out_shape=jax.ShapeDtypeStruct(theirs.shape, bf16),
        compiler_params=_cparams(("parallel", "parallel")))(lax.axis_index("c").astype(jnp.int32).reshape(1), g, theirs)


_MESH = pl.DeviceIdType.MESH
_HBM = pl.BlockSpec(memory_space=pltpu.HBM)
N_CHIP = N_DEV // 2


def _position():
    return lax.axis_index("x"), lax.axis_index("y"), lax.axis_index("c")


def _gather8(x, *, name):
    def body(x_ref, o_ref, send_sems, recv_sems, local_sem):
        xx, yy, cc = _position()
        me, sibling = (xx, yy, cc), (xx, yy, 1 - cc)
        chips = [(1 - xx, yy), (xx, 1 - yy), (1 - xx, 1 - yy)]

        def slab(px, py, pc):
            return o_ref.at[4 * px + 2 * py + pc]

        def copy(k, block, to, src=None):
            return pltpu.make_async_remote_copy(src_ref=slab(*block) if src is None else src, dst_ref=slab(*block),
                                                send_sem=send_sems.at[k], recv_sem=recv_sems.at[k], device_id=to,
                                                device_id_type=_MESH)

        mine = pltpu.make_async_copy(x_ref, slab(*me), local_sem)
        mine.start()
        first = [copy(0, me, sibling, src=x_ref)] + [copy(1 + j, me, (*chip, cc), src=x_ref) for j, chip in enumerate(chips)]
        for cp in first:
            cp.start()
        passed = [copy(4 + j, (*chip, cc), sibling) for j, chip in enumerate(chips)]
        for j, chip in enumerate(chips):
            copy(1 + j, (*chip, cc), me).wait_recv()
            passed[j].start()
        copy(0, sibling, me).wait_recv()
        for j, chip in enumerate(chips):
            copy(4 + j, (*chip, 1 - cc), me).wait_recv()
        for cp in first + passed:
            cp.wait_send()
        mine.wait()

    return pl.pallas_call(
        body, name=name, in_specs=[_HBM], out_specs=_HBM, out_shape=jax.ShapeDtypeStruct((N_DEV,) + x.shape, x.dtype),
        scratch_shapes=[pltpu.SemaphoreType.DMA((N_DEV - 1,)), pltpu.SemaphoreType.DMA((N_DEV - 1,)), pltpu.SemaphoreType.DMA],
    )(x)


def _pair_exchange(g, *, name):
    def body(g_ref, land_ref, send_sems, recv_sems):
        xx, yy, cc = _position()
        copies = []
        for j in range(N_CHIP):
            cp = pltpu.make_async_remote_copy(src_ref=g_ref.at[2 * j + 1 - cc], dst_ref=land_ref.at[j], send_sem=send_sems.at[j],
                                              recv_sem=recv_sems.at[j], device_id=(xx, yy, 1 - cc), device_id_type=_MESH)
            cp.start()
            copies.append(cp)
        for cp in copies:
            cp.wait_recv()
        for cp in copies:
            cp.wait_send()

    sems = pltpu.SemaphoreType.DMA((N_CHIP,))
    return pl.pallas_call(body, name=name, in_specs=[_HBM], out_specs=_HBM,
                          out_shape=jax.ShapeDtypeStruct((N_CHIP,) + g.shape[1:], g.dtype), scratch_shapes=[sems, sems])(g)


def _cross_exchange(p, *, name):
    def body(p_ref, o_ref, send_sems, recv_sems, local_sem):
        xx, yy, cc = _position()
        my_chip = 2 * xx + yy
        local = pltpu.make_async_copy(p_ref.at[my_chip], o_ref.at[my_chip], local_sem)
        local.start()
        chips = [(1 - xx, yy), (xx, 1 - yy), (1 - xx, 1 - yy)]
        sends = []
        for k, (px, py) in enumerate(chips):
            cp = pltpu.make_async_remote_copy(src_ref=p_ref.at[2 * px + py], dst_ref=o_ref.at[my_chip], send_sem=send_sems.at[k],
                                              recv_sem=recv_sems.at[k], device_id=(px, py, cc), device_id_type=_MESH)
            cp.start()
            sends.append(cp)
        for k, (px, py) in enumerate(chips):
            pltpu.make_async_remote_copy(src_ref=p_ref.at[2 * px + py], dst_ref=o_ref.at[2 * px + py], send_sem=send_sems.at[k],
                                         recv_sem=recv_sems.at[k], device_id=(px, py, cc), device_id_type=_MESH).wait_recv()
        for cp in sends:
            cp.wait_send()
        local.wait()

    sems = pltpu.SemaphoreType.DMA((N_CHIP - 1,))
    return pl.pallas_call(body, name=name, in_specs=[_HBM], out_specs=_HBM, out_shape=jax.ShapeDtypeStruct(p.shape, p.dtype),
                          scratch_shapes=[sems, sems, pltpu.SemaphoreType.DMA])(p)


def _all_sum(x, *, name):
    return _sum_slabs(_gather8(x, name=f"gather_{name}"), name=f"sum_{name}", tr=min(128, x.shape[0]))


def _pack_slabs(parts, rows, axis=0):
    lead = parts[0].shape[:axis]
    slabs = [p.reshape(lead + (-1, D)) for p in parts]
    used = sum(sl.shape[axis] for sl in slabs)
    return jnp.concatenate(slabs + [jnp.zeros(lead + (rows - used, D), slabs[0].dtype)], axis=axis)


def _unpack_slabs(slab, shapes):
    lead, out, off = slab.shape[:-2], [], 0
    for shp in shapes:
        n = int(np.prod(shp)) // D
        out.append(slab[..., off:off + n, :].reshape(lead + tuple(shp)))
        off += n
    return out


def _pack_rows(parts, rows):
    flat = jnp.concatenate([p.reshape(-1) for p in parts])
    return jnp.pad(flat, (0, rows * D - flat.shape[0])).reshape(rows, D)


def _unpack_rows(slab, shapes):
    flat, out, off = slab.reshape(-1), [], 0
    for shp in shapes:
        n = int(np.prod(shp))
        out.append(flat[off:off + n].reshape(shp))
        off += n
    return out


def _full_shape(shard, axis):
    return tuple(d * N_DEV if i == axis else d for i, d in enumerate(shard))


def _gather_full(shards, shard, axis):
    return jnp.moveaxis(shards, 0, axis).reshape(_full_shape(shard, axis))


def _split_full(full, shard, axis):
    shp = shard[:axis] + (N_DEV, shard[axis]) + shard[axis + 1:]
    return jnp.moveaxis(full.reshape(shp), axis, 0)


def _row(v):
    return v.reshape(1, -1).astype(f32)


def _pad_gain(g):
    return jnp.pad(g.astype(f32), (0, HEAD_PAD - QK)).reshape(1, HEAD_PAD)


def _ffn_fwd(x, p, tag):
    h = _rms_fwd(x, p["norm"], name=f"ffn_norm_{tag}")
    zg = _mm(h, p["wg"], name=f"ffn_up_g_{tag}")
    zv = _mm(h, p["wv"], name=f"ffn_up_v_{tag}")
    a = _ffn_act_fwd(zg, zv, p["cwg"], p["cwv"], name=f"ffn_act_{tag}")
    y = _mm(a, p["wd"], add=x, name=f"ffn_down_{tag}")
    return y, (x, h, zg, zv, a)


def _ffn_bwd(dy, dyb, p, saved, tag):
    x, h, zg, zv, a = saved
    g = {}
    da = _mm(dyb, p["wd"], tb=True, out_dtype=bf16, name=f"ffn_down_dx_{tag}")
    g["wd"] = _mm(a, dyb, ta=True, out_dtype=bf16, name=f"ffn_down_dw_{tag}")
    dzg, dzv, g["cwg"], g["cwv"] = _ffn_act_bwd(da, zg, zv, p["cwg"], p["cwv"], name=f"ffn_act_bwd_{tag}")
    g["wg"] = _mm(h, dzg, ta=True, out_dtype=bf16, name=f"ffn_up_g_dw_{tag}")
    g["wv"] = _mm(h, dzv, ta=True, out_dtype=bf16, name=f"ffn_up_v_dw_{tag}")
    dh = _mm(dzg, p["wg"], tb=True, name=f"ffn_up_g_dx_{tag}")
    dh = _mm(dzv, p["wv"], tb=True, add=dh, name=f"ffn_up_v_dx_{tag}")
    dx, dxb, g["norm"] = _rms_bwd(dh, x, p["norm"], res=dy, name=f"ffn_norm_bwd_{tag}")
    return dx, dxb, g


def _mla_fwd(x, p, tabs, tag):
    cos_t, sin_t = tabs
    h = _rms_fwd(x, p["norm"], name=f"attn_norm_{tag}")
    proj = _mm(h, p["w_in"], name=f"mix_in_{tag}")
    cqn = _rms_fwd(proj, p["cq_norm"], col=0, name=f"cq_norm_{tag}")
    ckvn = _rms_fwd(proj, p["ckv_norm"], col=1, name=f"ckv_norm_{tag}")
    q_raw = _mm(cqn, p["w_uq"], name=f"uq_{tag}")
    kv_raw = _mm(ckvn, p["w_ukv"], name=f"ukv_{tag}")
    q, k, v = _qk_prep_fwd(q_raw, kv_raw, proj, p["q_gain"], p["k_gain"], cos_t, sin_t, name=f"qk_prep_{tag}")
    o, lse = _flash_fwd(q, k, v, name=f"flash_fwd_{tag}")
    conv = _sconv_fwd(proj, p["sconv_w"], name=f"sconv_{tag}")
    mix = jnp.concatenate([o, conv], axis=1)
    y = _mm(mix, p["w_out"], add=x, name=f"mix_out_{tag}")
    return y, (x, h, proj, cqn, ckvn, q_raw, kv_raw, q, k, v, o, lse, mix)


def _mla_bwd(dy, dyb, p, tabs, saved, tag):
    cos_t, sin_t = tabs
    x, h, proj, cqn, ckvn, q_raw, kv_raw, q, k, v, o, lse, mix = saved
    s = x.shape[0]
    g = {}
    dmix = _mm(dyb, p["w_out"], tb=True, name=f"mix_out_dx_{tag}")
    g["w_out"] = _mm(mix, dyb, ta=True, out_dtype=bf16, name=f"mix_out_dw_{tag}")
    dgb, dgc, dci, g["sconv_w"] = _sconv_bwd(dmix, proj, p["sconv_w"], name=f"sconv_bwd_{tag}")
    dq, delta = _flash_bwd_dq(q, k, v, o, dmix, lse, name=f"flash_dq_{tag}")
    dk, dv = _flash_bwd_dkv(q, k, v, dmix, lse.reshape(HEADS, 1, s), delta.reshape(HEADS, 1, s), name=f"flash_dkv_{tag}")
    dq_raw, dkv_raw, dkr, g["q_gain"], g["k_gain"] = _qk_prep_bwd(
        dq, dk, dv, q_raw, kv_raw, proj, p["q_gain"], p["k_gain"], cos_t, sin_t, name=f"qk_prep_bwd_{tag}")
    dcqn = _mm(dq_raw, p["w_uq"], tb=True, name=f"uq_dx_{tag}")
    g["w_uq"] = _mm(cqn, dq_raw, ta=True, out_dtype=bf16, name=f"uq_dw_{tag}")
    dckvn = _mm(dkv_raw, p["w_ukv"], tb=True, name=f"ukv_dx_{tag}")
    g["w_ukv"] = _mm(ckvn, dkv_raw, ta=True, out_dtype=bf16, name=f"ukv_dw_{tag}")
    dcq, g["cq_norm"] = _rms_bwd(dcqn, proj, p["cq_norm"], col=0, out_dtype=bf16, name=f"cq_norm_bwd_{tag}")
    dckv, g["ckv_norm"] = _rms_bwd(dckvn, proj, p["ckv_norm"], col=1, out_dtype=bf16, name=f"ckv_norm_bwd_{tag}")
    dproj = jnp.concatenate([dcq, dckv, dgb, dgc, dci, dkr.astype(bf16)], axis=1)
    dh = _mm(dproj, p["w_in"], tb=True, name=f"mix_in_dx_{tag}")
    g["w_in"] = _mm(h, dproj, ta=True, out_dtype=bf16, name=f"mix_in_dw_{tag}")
    dx, dxb, g["norm"] = _rms_bwd(dh, x, p["norm"], res=dy, name=f"attn_norm_bwd_{tag}")
    return dx, dxb, g


def _block_diag(wg):
    nb, ng, r, c = wg.shape
    eye = jnp.eye(ng, dtype=wg.dtype)
    return (wg[:, :, :, None, :] * eye[None, :, None, :, None]).reshape(nb, ng * r, ng * c)


def _s5_mats(bbr, bbi, c_re, c_im):
    nb = GROUPS // 8
    b4 = jnp.stack([bbr.reshape(GROUPS, STATE, GROUP), bbi.reshape(GROUPS, STATE, GROUP)], axis=1)
    wg = jnp.transpose(b4, (0, 3, 1, 2)).reshape(nb, 8, GROUP, 2 * STATE)
    cg = jnp.stack([c_re, -c_im], axis=1)
    cg = jnp.transpose(cg, (0, 1, 3, 2)).reshape(nb, 8, 2 * STATE, GROUP)
    return _state_layout(_block_diag(wg), 2), _state_layout(_block_diag(cg), 1)


def _state_layout(m, axis):
    shp = m.shape
    m = m.reshape(shp[:axis] + (4, 2, 2, STATE) + shp[axis + 1:])
    return jnp.swapaxes(m, axis + 1, axis + 2).reshape(shp)


def _group_blocks(d):
    d = d.reshape(GROUPS // 2, 2, GROUP, 2, 2, STATE)
    return jnp.stack([d[:, 0, :, :, 0, :], d[:, 1, :, :, 1, :]], axis=1).reshape(GROUPS, GROUP, 2, STATE)


def _s5_fwd(x, p, tag):
    h = _rms_fwd(x, p["norm"], name=f"ssm_norm_{tag}")
    u, ub = _mm(h, p["w_in"], twin=True, name=f"ssm_in_{tag}")
    ar, ai, bbr, bbi = _disc_fwd(p["lr"], p["li"], p["ls"], p["br"], p["bi"], name=f"disc_{tag}")
    wb, cb = _s5_mats(bbr, bbi, p["c_re"], p["c_im"])
    a1, a2 = ar.reshape(GROUPS // 2, 1, LANES), ai.reshape(GROUPS // 2, 1, LANES)
    xs = _scan_fwd(ub, wb.astype(bf16), a1, a2, name=f"ssm_scan_{tag}")
    y = _bd_nn(xs, cb.astype(bf16), name=f"ssm_y_{tag}")
    g = _gelu_fwd(y, u, p["d_skip"], name=f"ssm_gelu_{tag}")
    a = _mm(g, p["wga"], name=f"glu_a_{tag}")
    b = _mm(g, p["wgb"], name=f"glu_b_{tag}")
    out = _glu_fwd(x, a, b, name=f"glu_{tag}")
    return out, (x, h, u, ub, wb, cb, a1, a2, xs, y, g, a, b)


def _s5_bwd(dout, p, saved, tag):
    x, h, u, ub, wb, cb, a1, a2, xs, y, g, a, b = saved
    gr = {}
    da, db = _glu_bwd(dout, a, b, name=f"glu_bwd_{tag}")
    dg = _mm(da, p["wga"], tb=True, name=f"glu_a_dx_{tag}")
    dg = _mm(db, p["wgb"], tb=True, add=dg, name=f"glu_b_dx_{tag}")
    gr["wga"] = _mm(g, da, ta=True, out_dtype=bf16, name=f"glu_a_dw_{tag}")
    gr["wgb"] = _mm(g, db, ta=True, out_dtype=bf16, name=f"glu_b_dw_{tag}")
    dy, du1, gr["d_skip"] = _gelu_bwd(dg, y, u, p["d_skip"], name=f"ssm_gelu_bwd_{tag}")
    dct = _group_blocks(_bd_tn_diag(dy, xs, name=f"ssm_y_dw_{tag}"))
    gs, dar, dai = _scan_bwd(dy, jnp.swapaxes(cb, 1, 2).astype(bf16), xs, a1, a2, name=f"ssm_scan_bwd_{tag}")
    du2 = _bd_nn(gs, jnp.swapaxes(wb, 1, 2).astype(bf16), name=f"ssm_bu_dx_{tag}")
    dwg = _group_blocks(_bd_tn_diag(ub, gs, name=f"ssm_bu_dw_{tag}"))
    du = _add(du1, du2, name=f"ssm_du_{tag}")
    dh = _mm(du, p["w_in"], tb=True, name=f"ssm_in_dx_{tag}")
    gr["w_in"] = _mm(h, du, ta=True, out_dtype=bf16, name=f"ssm_in_dw_{tag}")
    dx, dxb, gr["norm"] = _rms_bwd(dh, x, p["norm"], res=dout, name=f"ssm_norm_bwd_{tag}")
    dbb = jnp.transpose(dwg, (2, 0, 3, 1)).reshape(2, GROUPS, STATE * GROUP)
    gr["c_re"] = dct[:, :, 0, :]
    gr["c_im"] = -dct[:, :, 1, :]
    dlr, dli, dls, dbr, dbi = _disc_bwd(p["lr"], p["li"], p["ls"], p["br"], p["bi"], dar.reshape(GROUPS, STATE),
                                        dai.reshape(GROUPS, STATE), dbb[0], dbb[1], name=f"disc_bwd_{tag}")
    gr["lr"], gr["li"], gr["ls"] = dlr, dli, dls.reshape(GROUPS)
    gr["br"], gr["bi"] = dbr.reshape(GROUPS, STATE, GROUP), dbi.reshape(GROUPS, STATE, GROUP)
    return dx, dxb, gr


def _mix_in_pad(w):
    z = lambda n: jnp.zeros((w.shape[0], n), w.dtype)
    return jnp.concatenate([w[:, :512], w[:, 544:2080], z(NOPE), w[:, 512:544], z(HEAD_PAD - QK)], axis=1)


def _mix_in_unpad(g):
    return jnp.concatenate([g[:, :512], g[:, 2048 + NOPE:2048 + QK], g[:, 512:2048]], axis=1)


def _uq_pad(w):
    return jnp.pad(w.reshape(LORA, HEADS, QK), ((0, 0), (0, 0), (0, HEAD_PAD - QK))).reshape(LORA, HEADS * HEAD_PAD)


def _uq_unpad(g):
    return g.reshape(LORA, HEADS, HEAD_PAD)[:, :, :QK].reshape(LORA, HEADS * QK)


def _mix_out_pad(w):
    att = jnp.pad(w[:512].reshape(HEADS, NOPE, D), ((0, 0), (NOPE, 0), (0, 0))).reshape(HEADS * HEAD_PAD, D)
    return jnp.concatenate([att, w[512:]], axis=0)


def _mix_out_unpad(g):
    att = g[:HEADS * HEAD_PAD].reshape(HEADS, HEAD_PAD, D)[:, NOPE:, :].reshape(HEADS * NOPE, D)
    return jnp.concatenate([att, g[HEADS * HEAD_PAD:]], axis=0)


def _layer_params(w, layer):
    i = layer // 2
    ffn = dict(norm=_row(w["ffn_norm"][layer]), wg=w["ffn_w_up"][layer][:, :FFN_H], wv=w["ffn_w_up"][layer][:, FFN_H:],
               cwg=w["ffn_conv_w"][layer][:, :FFN_H], cwv=w["ffn_conv_w"][layer][:, FFN_H:], wd=w["ffn_w_down"][layer])
    if layer % 2 == 0:
        mixer = dict(norm=_row(w["attn_norm"][i]), w_in=_mix_in_pad(w["mix_w_in"][i]), cq_norm=_row(w["cq_norm"][i]),
                     ckv_norm=_row(w["ckv_norm"][i]), w_uq=_uq_pad(w["w_uq"][i]), w_ukv=w["w_ukv"][i],
                     q_gain=_pad_gain(w["q_gain"][i]), k_gain=_pad_gain(w["k_gain"][i]), sconv_w=w["sconv_w"][i],
                     w_out=_mix_out_pad(w["mix_w_out"][i]))
    else:
        mixer = dict(norm=_row(w["ssm_norm"][i]), w_in=w["ssm_w_in"][i], lr=w["lambda_re"][i], li=w["lambda_im"][i],
                     ls=w["log_step"][i].reshape(GROUPS, 1), br=w["b_re"][i].reshape(GROUPS, STATE * GROUP),
                     bi=w["b_im"][i].reshape(GROUPS, STATE * GROUP), c_re=w["c_re"][i], c_im=w["c_im"][i],
                     d_skip=_row(w["d_skip"][i]), wga=w["w_glu"][i][:, :D], wgb=w["w_glu"][i][:, D:])
    return mixer, ffn


def _collect_grads(gm, gf):
    st = lambda xs: jnp.stack(xs, axis=0)
    ev, od = (0, 2), (1, 3)
    out = {
        "attn_norm": st([gm[l]["norm"].reshape(D) for l in ev]),
        "mix_w_in": st([_mix_in_unpad(gm[l]["w_in"]) for l in ev]),
        "cq_norm": st([gm[l]["cq_norm"].reshape(LORA) for l in ev]),
        "ckv_norm": st([gm[l]["ckv_norm"].reshape(LORA) for l in ev]),
        "w_uq": st([_uq_unpad(gm[l]["w_uq"]) for l in ev]),
        "w_ukv": st([gm[l]["w_ukv"] for l in ev]),
        "q_gain": st([gm[l]["q_gain"].reshape(HEAD_PAD)[:QK] for l in ev]),
        "k_gain": st([gm[l]["k_gain"].reshape(HEAD_PAD)[:QK] for l in ev]),
        "sconv_w": st([gm[l]["sconv_w"] for l in ev]),
        "mix_w_out": st([_mix_out_unpad(gm[l]["w_out"]) for l in ev]),
        "ssm_norm": st([gm[l]["norm"].reshape(D) for l in od]),
        "ssm_w_in": st([gm[l]["w_in"] for l in od]),
        "lambda_re": st([gm[l]["lr"] for l in od]), "lambda_im": st([gm[l]["li"] for l in od]),
        "log_step": st([gm[l]["ls"] for l in od]),
        "b_re": st([gm[l]["br"] for l in od]), "b_im": st([gm[l]["bi"] for l in od]),
        "c_re": st([gm[l]["c_re"] for l in od]), "c_im": st([gm[l]["c_im"] for l in od]),
        "d_skip": st([gm[l]["d_skip"].reshape(D) for l in od]),
        "w_glu": st([jnp.concatenate([gm[l]["wga"], gm[l]["wgb"]], axis=1) for l in od]),
        "ffn_norm": st([gf[l]["norm"].reshape(D) for l in range(4)]),
        "ffn_w_up": st([jnp.concatenate([gf[l]["wg"], gf[l]["wv"]], axis=1) for l in range(4)]),
        "ffn_conv_w": st([jnp.concatenate([gf[l]["cwg"], gf[l]["cwv"]], axis=1) for l in range(4)]),
        "ffn_w_down": st([gf[l]["wd"] for l in range(4)]),
    }
    return out


def _local_step(x, target, w):
    s = x.shape[0]
    tabs = _rope_tables(s)
    saved, params = [], []
    for layer in range(4):
        mixer, ffn = _layer_params(w, layer)
        params.append((mixer, ffn))
        if layer % 2 == 0:
            x, sm = _mla_fwd(x, mixer, tabs, f"l{layer}")
        else:
            x, sm = _s5_fwd(x, mixer, f"l{layer}")
        x, sf = _ffn_fwd(x, ffn, f"l{layer}")
        saved.append((sm, sf))
    dx, dxb, loss = _loss_head(x, target, name="loss_head")
    gm, gf = [None] * 4, [None] * 4
    for layer in reversed(range(4)):
        mixer, ffn = params[layer]
        sm, sf = saved[layer]
        dx, dxb, gf[layer] = _ffn_bwd(dx, dxb, ffn, sf, f"l{layer}")
        if layer % 2 == 0:
            dx, dxb, gm[layer] = _mla_bwd(dx, dxb, mixer, tabs, sm, f"l{layer}")
        else:
            dx, dxb, gm[layer] = _s5_bwd(dx, mixer, sm, f"l{layer}")
    return loss, dx, _collect_grads(gm, gf)


def kernel(x, attn_norm, mix_w_in, cq_norm, ckv_norm, w_uq, w_ukv, q_gain, k_gain, sconv_w, mix_w_out, ssm_norm, ssm_w_in, lambda_re, lambda_im, log_step, b_re, b_im, c_re, c_im, d_skip, w_glu, ffn_norm, ffn_w_up, ffn_conv_w, ffn_w_down, loss_target, m_attn_norm, m_mix_w_in, m_cq_norm, m_ckv_norm, m_w_uq, m_w_ukv, m_q_gain, m_k_gain, m_sconv_w, m_mix_w_out, m_ssm_norm, m_ssm_w_in, m_lambda_re, m_lambda_im, m_log_step, m_b_re, m_b_im, m_c_re, m_c_im, m_d_skip, m_w_glu, m_ffn_norm, m_ffn_w_up, m_ffn_conv_w, m_ffn_w_down, v_attn_norm, v_mix_w_in, v_cq_norm, v_ckv_norm, v_w_uq, v_w_ukv, v_q_gain, v_k_gain, v_sconv_w, v_mix_w_out, v_ssm_norm, v_ssm_w_in, v_lambda_re, v_lambda_im, v_log_step, v_b_re, v_b_im, v_c_re, v_c_im, v_d_skip, v_w_glu, v_ffn_norm, v_ffn_w_up, v_ffn_conv_w, v_ffn_w_down):
    args = dict(locals())
    wsh = {n: args[n] for n in WEIGHTS}
    msh = {n: args["m_" + n] for n in WEIGHTS}
    vsh = {n: args["v_" + n] for n in WEIGHTS}
    me = 4 * lax.axis_index("x") + 2 * lax.axis_index("y") + lax.axis_index("c")
    big_names, big_shapes = [n for n, _, _ in BIG], [sh for _, sh, _ in BIG]
    small_names = [n for n, _ in REPL] + [n for n, _, _ in SMALL]

    big_all = _gather8(_pack_slabs([wsh[n].astype(bf16) for n in big_names], BIG_ROWS), name="gather_weights")
    w = {n: _gather_full(val, shard, axis) for (n, shard, axis), val in zip(BIG, _unpack_slabs(big_all, big_shapes))}
    placed = []
    for n, shard, axis in SMALL:
        start = [0] * len(shard)
        start[axis] = me * shard[axis]
        placed.append(lax.dynamic_update_slice(jnp.zeros(_full_shape(shard, axis), f32), wsh[n], start))
    small_all = _all_sum(_pack_rows(placed, SMALL_FWD_ROWS), name="small_params")
    for (n, shard, axis), full in zip(SMALL, _unpack_rows(small_all, [_full_shape(sh, ax) for _, sh, ax in SMALL])):
        w[n] = full
    for n, _ in REPL:
        w[n] = wsh[n]

    loss8, grad_x, grads = _local_step(x[0], loss_target[0], w)

    contrib = _pack_slabs([_split_full(grads[n].astype(bf16), shard, axis) for n, shard, axis in BIG], BIG_ROWS, axis=1)
    chip_sum = _pair_sum(contrib, _pair_exchange(contrib, name="grads_pair_exchange"), name="grads_pair_sum")
    g_big = _sum_slabs(_cross_exchange(chip_sum, name="grads_cross_exchange"), name="grads_chip_sum")
    small_vec = _pack_rows([grads[n] for n, _ in REPL] + [grads[n] for n, _, _ in SMALL] + [loss8[0, :1]], SMALL_ROWS)
    small_sum = _all_sum(small_vec, name="small_grads")
    parts = _unpack_rows(small_sum, [sh for _, sh in REPL] + [_full_shape(sh, ax) for _, sh, ax in SMALL] + [(1,)])
    g = {n: val for (n, _), val in zip(REPL, parts)}
    for (n, shard, axis), val in zip(SMALL, parts[len(REPL):]):
        start = [0] * len(shard)
        start[axis] = me * shard[axis]
        g[n] = lax.dynamic_slice(val, start, shard)
    loss = parts[-1].reshape(())
    g.update(zip(big_names, _unpack_slabs(g_big, big_shapes)))

    delta, new_m, new_v = {}, {}, {}
    big_state = [_pack_slabs([src[n] for n in big_names], BIG_ROWS) for src in (wsh, msh, vsh)]
    for dst, slab in zip((delta, new_m, new_v), _adamw(big_state[0], g_big, big_state[1], big_state[2], name="adamw_big")):
        dst.update(zip(big_names, _unpack_slabs(slab, big_shapes)))
    small_state = [_pack_rows([src[n] for n in small_names], SMALL_ROWS) for src in (wsh, g, msh, vsh)]
    for dst, slab in zip((delta, new_m, new_v), _adamw(*small_state, name="adamw_small")):
        dst.update(zip(small_names, _unpack_rows(slab, [wsh[n].shape for n in small_names])))

    return (loss, grad_x[None], *[g[n] for n in WEIGHTS], *[delta[n] for n in WEIGHTS],
            *[new_m[n] for n in WEIGHTS], *[new_v[n] for n in WEIGHTS])
```

```python
import math

import numpy as np
import jax
import jax.numpy as jnp
from jax import lax
from jax.experimental import pallas as pl
from jax.experimental.pallas import tpu as pltpu

f32, bf16 = jnp.float32, jnp.bfloat16

N_DEV = 8
D = 1024
HEADS = 8
NOPE, ROPE, QK = 64, 32, 96
HEAD_PAD = 128
LORA = 256
CONV_CH = 512
MIX_IN_PAD = 2176
FFN_H = 2816
GROUPS, GROUP, STATE = 64, 16, 64
EPS = 1e-6
ROPE_THETA = 10000.0
ADAM_LR, ADAM_B1, ADAM_B2, ADAM_EPS, ADAM_WD, ADAM_STEP = 0.001, 0.9, 0.999, 1e-08, 0.01, 10
LANES = 128
PAIR_LANES = 2 * LANES
VMEM_LIMIT = 56 << 20
MM_VMEM_BUDGET = 40 << 20
NEG = -1e30

BIG = (
    ("ffn_w_up", (4, 1024, 704), 2), ("ffn_w_down", (4, 352, 1024), 1), ("w_glu", (2, 1024, 256), 2),
    ("mix_w_out", (2, 128, 1024), 1), ("ssm_w_in", (2, 128, 1024), 1), ("w_ukv", (2, 256, 128), 2),
    ("w_uq", (2, 256, 96), 2), ("mix_w_in", (2, 1024, 260), 2))
REPL = (("attn_norm", (2, 1024)), ("cq_norm", (2, 256)), ("ckv_norm", (2, 256)), ("q_gain", (2, 96)),
        ("k_gain", (2, 96)), ("lambda_re", (2, 64, 64)), ("lambda_im", (2, 64, 64)), ("log_step", (2, 64)),
        ("b_re", (2, 64, 64, 16)), ("b_im", (2, 64, 64, 16)), ("c_re", (2, 64, 16, 64)), ("c_im", (2, 64, 16, 64)),
        ("ffn_norm", (4, 1024)))
SMALL = (("sconv_w", (2, 3, 64), 2), ("ssm_norm", (2, 128), 1), ("d_skip", (2, 128), 1), ("ffn_conv_w", (4, 3, 704), 2))
WEIGHTS = ['attn_norm', 'mix_w_in', 'cq_norm', 'ckv_norm', 'w_uq', 'w_ukv', 'q_gain', 'k_gain', 'sconv_w', 'mix_w_out',
           'ssm_norm', 'ssm_w_in', 'lambda_re', 'lambda_im', 'log_step', 'b_re', 'b_im', 'c_re', 'c_im', 'd_skip',
           'w_glu', 'ffn_norm', 'ffn_w_up', 'ffn_conv_w', 'ffn_w_down']
BIG_ROWS = 5888
SMALL_FWD_ROWS = 80
SMALL_ROWS = 640


def _cparams(sem=None, **kw):
    return pltpu.CompilerParams(dimension_semantics=sem, vmem_limit_bytes=VMEM_LIMIT, **kw)


def _tile(n, target):
    best = 0
    for t in range(LANES, min(n, target) + 1, LANES):
        if n % t == 0:
            best = t
    return best if best else n


def _mm(a, b, *, ta=False, tb=False, out_dtype=f32, add=None, twin=False, name, tm=1024, tn=1536):
    m, k = (a.shape[1], a.shape[0]) if ta else a.shape
    n = b.shape[0] if tb else b.shape[1]
    assert (b.shape[1] if tb else b.shape[0]) == k
    tm = _tile(m, tm)
    tn_ = _tile(n, tn)
    tn = n if (tn_ < 256 and n <= 2304) else tn_

    def vmem_bytes(t):
        io = 2 * (tm * t * a.dtype.itemsize + t * tn * b.dtype.itemsize + tm * tn * (jnp.dtype(out_dtype).itemsize + 2 * twin))
        return io + (2 * tm * tn * 4 if add is not None else 0) + (tm * tn * 4 if t < k else 0)

    tk = next((t for t in [k] + [t for t in range(k - LANES, 0, -LANES) if k % t == 0] if vmem_bytes(t) <= MM_VMEM_BUDGET), LANES)
    nk = k // tk
    dn = (((0 if ta else 1,), (1 if tb else 0,)), ((), ()))

    def body(*refs):
        a_ref, b_ref = refs[:2]
        add_ref = refs[2] if add is not None else None
        o_ref = refs[3] if add is not None else refs[2]
        twin_ref = refs[4 if add is not None else 3] if twin else None
        part = lax.dot_general(a_ref[...].astype(bf16), b_ref[...].astype(bf16), dn, preferred_element_type=f32)

        def finish(r):
            if add is not None:
                r = r + add_ref[...].astype(f32)
            o_ref[...] = r.astype(out_dtype)
            if twin:
                twin_ref[...] = r.astype(bf16)

        if nk == 1:
            finish(part)
            return
        acc = refs[-1]
        kk = pl.program_id(2)

        @pl.when(kk == 0)
        def _():
            acc[...] = part

        @pl.when(kk > 0)
        def _():
            acc[...] += part

        @pl.when(kk == nk - 1)
        def _():
            finish(acc[...])

    a_spec = pl.BlockSpec((tk, tm), lambda i, j, kk: (kk, i)) if ta else pl.BlockSpec((tm, tk), lambda i, j, kk: (i, kk))
    b_spec = pl.BlockSpec((tn, tk), lambda i, j, kk: (j, kk)) if tb else pl.BlockSpec((tk, tn), lambda i, j, kk: (kk, j))
    in_specs, args = [a_spec, b_spec], [a, b]
    if add is not None:
        in_specs.append(pl.BlockSpec((tm, tn), lambda i, j, kk: (i, j)))
        args.append(add)
    o_spec, o_shape = pl.BlockSpec((tm, tn), lambda i, j, kk: (i, j)), jax.ShapeDtypeStruct((m, n), out_dtype)
    return pl.pallas_call(
        body, name=name, grid=(m // tm, n // tn, nk), in_specs=in_specs,
        out_specs=[o_spec, o_spec] if twin else o_spec,
        out_shape=[o_shape, jax.ShapeDtypeStruct((m, n), bf16)] if twin else o_shape,
        scratch_shapes=[pltpu.VMEM((tm, tn), f32)] if nk > 1 else [],
        compiler_params=_cparams(("parallel", "parallel", "arbitrary")))(*args)


def _bd_nn(a, w, *, out_dtype=f32, name, ts=512):
    s = a.shape[0]
    nb, ka, no = w.shape
    ts = min(ts, s)

    def body(a_ref, w_ref, o_ref):
        o_ref[...] = jnp.dot(a_ref[...].astype(bf16), w_ref[0].astype(bf16), preferred_element_type=f32).astype(out_dtype)

    return pl.pallas_call(
        body, name=name, grid=(nb, s // ts),
        in_specs=[pl.BlockSpec((ts, ka), lambda b, i: (i, b)), pl.BlockSpec((1, ka, no), lambda b, i: (b, 0, 0))],
        out_specs=pl.BlockSpec((ts, no), lambda b, i: (i, b)),
        out_shape=jax.ShapeDtypeStruct((s, nb * no), out_dtype),
        compiler_params=_cparams(("parallel", "parallel")))(a, w)


def _bd_tn_diag(a, g, *, name, ts=512):
    s = a.shape[0]
    nb = a.shape[1] // LANES
    ts = min(ts, s)
    ni = s // ts

    def body(a_ref, g_ref, o_ref, acc):
        i = pl.program_id(1)
        part = lax.dot_general(a_ref[...].astype(bf16), g_ref[...].astype(bf16), (((0,), (0,)), ((), ())),
                               preferred_element_type=f32)

        @pl.when(i == 0)
        def _():
            acc[...] = part

        @pl.when(i > 0)
        def _():
            acc[...] += part

        @pl.when(i == ni - 1)
        def _():
            for j in range(8):
                o_ref[0, j] = acc[j * GROUP:(j + 1) * GROUP, (j // 2) * PAIR_LANES:(j // 2 + 1) * PAIR_LANES]

    return pl.pallas_call(
        body, name=name, grid=(nb, ni),
        in_specs=[pl.BlockSpec((ts, LANES), lambda b, i: (i, b)), pl.BlockSpec((ts, 8 * LANES), lambda b, i: (i, b))],
        out_specs=pl.BlockSpec((1, 8, GROUP, PAIR_LANES), lambda b, i: (b, 0, 0, 0)),
        out_shape=jax.ShapeDtypeStruct((nb, 8, GROUP, PAIR_LANES), f32),
        scratch_shapes=[pltpu.VMEM((LANES, 8 * LANES), f32)],
        compiler_params=_cparams(("parallel", "arbitrary")))(a, g)


def _rms_fwd(x, g, *, col=0, name, ts=512):
    s, d = x.shape[0], g.shape[1]
    ts = min(ts, s)

    def body(x_ref, g_ref, o_ref):
        xv = x_ref[...].astype(f32)
        r = lax.rsqrt(jnp.mean(xv * xv, axis=-1, keepdims=True) + EPS)
        o_ref[...] = (xv * r * g_ref[...]).astype(bf16)

    return pl.pallas_call(
        body, name=name, grid=(s // ts,),
        in_specs=[pl.BlockSpec((ts, d), lambda i: (i, col)), pl.BlockSpec((1, d), lambda i: (0, 0))],
        out_specs=pl.BlockSpec((ts, d), lambda i: (i, 0)),
        out_shape=jax.ShapeDtypeStruct((s, d), bf16),
        compiler_params=_cparams(("parallel",)))(x, g)


def _rms_bwd(dy, x, g, *, col=0, res=None, out_dtype=f32, name, ts=512):
    s, d = dy.shape
    ts = min(ts, s)
    twin = res is not None

    def body(*refs):
        if twin:
            dy_ref, x_ref, g_ref, res_ref, dx_ref, dxb_ref, dg_ref = refs
        else:
            dy_ref, x_ref, g_ref, dx_ref, dg_ref = refs

        @pl.when(pl.program_id(0) == 0)
        def _():
            dg_ref[...] = jnp.zeros_like(dg_ref)

        xv, dyv = x_ref[...].astype(f32), dy_ref[...].astype(f32)
        r = lax.rsqrt(jnp.mean(xv * xv, axis=-1, keepdims=True) + EPS)
        dyg = dyv * g_ref[...]
        dx = r * dyg - xv * (r * r * r) * jnp.mean(xv * dyg, axis=-1, keepdims=True)
        if twin:
            dx = dx + res_ref[...]
            dxb_ref[...] = dx.astype(bf16)
        dx_ref[...] = dx.astype(out_dtype)
        dg_ref[...] += jnp.sum(dyv * xv * r, axis=0, keepdims=True)

    row, vec = pl.BlockSpec((ts, d), lambda i: (i, 0)), pl.BlockSpec((1, d), lambda i: (0, 0))
    in_specs, args = [row, pl.BlockSpec((ts, d), lambda i: (i, col)), vec], [dy, x, g]
    out_specs, out_shape = [row], [jax.ShapeDtypeStruct((s, d), out_dtype)]
    if twin:
        in_specs.append(row)
        args.append(res)
        out_specs.append(row)
        out_shape.append(jax.ShapeDtypeStruct((s, d), bf16))
    return pl.pallas_call(
        body, name=name, grid=(s // ts,), in_specs=in_specs, out_specs=out_specs + [vec],
        out_shape=out_shape + [jax.ShapeDtypeStruct((1, d), f32)],
        compiler_params=_cparams(("arbitrary",)))(*args)


def _swap_halves(z):
    lane = lax.broadcasted_iota(jnp.int32, z.shape, 1)
    return jnp.where(lane < NOPE + ROPE // 2, pltpu.roll(z, LANES - ROPE // 2, axis=1), pltpu.roll(z, ROPE // 2, axis=1))


def _rope_tables(s):
    inv_freq = 1.0 / (ROPE_THETA ** (jnp.arange(0, ROPE, 2, dtype=f32) / ROPE))
    ang = jnp.arange(s, dtype=f32)[:, None] * inv_freq[None, :]
    cos, sin = jnp.cos(ang), jnp.sin(ang)
    one, zero = jnp.ones((s, NOPE), f32), jnp.zeros((s, NOPE), f32)
    pad1, pad0 = jnp.ones((s, HEAD_PAD - QK), f32), jnp.zeros((s, HEAD_PAD - QK), f32)
    return jnp.concatenate([one, cos, cos, pad1], 1), jnp.concatenate([zero, -sin, sin, pad0], 1)


def _qk_prep_fwd(q_raw, kv_raw, proj, qg, kg, cos_t, sin_t, *, name, ts=512):
    s = q_raw.shape[0]
    ts = min(ts, s)
    rope_blk = (MIX_IN_PAD - HEAD_PAD) // HEAD_PAD

    def body(q_ref, kv_ref, kr_ref, qg_ref, kg_ref, c_ref, s_ref, qo_ref, ko_ref, vo_ref):
        lane = lax.broadcasted_iota(jnp.int32, (ts, HEAD_PAD), 1)
        cosv, sinv = c_ref[...], s_ref[...]

        def norm_rope(z, gain):
            r = lax.rsqrt(jnp.sum(z * z, axis=-1, keepdims=True) * (1.0 / QK) + EPS)
            zn = z * r * gain
            return zn * cosv + _swap_halves(zn) * sinv

        kvv = kv_ref[...]
        qo_ref[...] = (norm_rope(q_ref[...], qg_ref[...]) * _Q_FOLD).astype(bf16)
        ko_ref[...] = norm_rope(jnp.where(lane < NOPE, kvv, kr_ref[...]), kg_ref[...]).astype(bf16)
        vo_ref[...] = jnp.where(lane >= NOPE, kvv, 0.0).astype(bf16)

    head = pl.BlockSpec((ts, HEAD_PAD), lambda i, h: (i, h))
    row = pl.BlockSpec((ts, HEAD_PAD), lambda i, h: (i, 0))
    vec = pl.BlockSpec((1, HEAD_PAD), lambda i, h: (0, 0))
    out = jax.ShapeDtypeStruct((s, HEADS * HEAD_PAD), bf16)
    return pl.pallas_call(
        body, name=name, grid=(s // ts, HEADS),
        in_specs=[head, head, pl.BlockSpec((ts, HEAD_PAD), lambda i, h: (i, rope_blk)), vec, vec, row, row],
        out_specs=[head, head, head], out_shape=[out, out, out],
        compiler_params=_cparams(("parallel", "parallel")))(q_raw, kv_raw, proj, qg, kg, cos_t, sin_t)


def _qk_prep_bwd(dq, dk, dv, q_raw, kv_raw, proj, qg, kg, cos_t, sin_t, *, name, ts=512):
    s = q_raw.shape[0]
    ts = min(ts, s)
    rope_blk = (MIX_IN_PAD - HEAD_PAD) // HEAD_PAD

    def body(dq_ref, dk_ref, dv_ref, q_ref, kv_ref, kr_ref, qg_ref, kg_ref, c_ref, s_ref,
             dqr_ref, dkvr_ref, dkr_ref, dqg_ref, dkg_ref):
        i, h = pl.program_id(0), pl.program_id(1)
        lane = lax.broadcasted_iota(jnp.int32, (ts, HEAD_PAD), 1)
        is_rope = (lane >= NOPE) & (lane < QK)
        cosv, sinv = c_ref[...], s_ref[...]

        @pl.when((i == 0) & (h == 0))
        def _():
            dqg_ref[...] = jnp.zeros_like(dqg_ref)
            dkg_ref[...] = jnp.zeros_like(dkg_ref)

        @pl.when(h == 0)
        def _():
            dkr_ref[...] = jnp.zeros_like(dkr_ref)

        def back(dout, z, gain):
            dzn = dout * cosv + jnp.where(is_rope, _swap_halves(dout * sinv), 0.0)
            r = lax.rsqrt(jnp.sum(z * z, axis=-1, keepdims=True) * (1.0 / QK) + EPS)
            dzg = dzn * gain
            dz = r * dzg - z * (r * r * r) * (jnp.sum(z * dzg, axis=-1, keepdims=True) * (1.0 / QK))
            return dz, jnp.sum(dzn * z * r, axis=0, keepdims=True)

        dqz, dqg = back(dq_ref[...].astype(f32), q_ref[...], qg_ref[...])
        dqr_ref[...] = dqz.astype(bf16)
        dqg_ref[...] += dqg
        kvv = kv_ref[...]
        dkz, dkg = back(dk_ref[...].astype(f32), jnp.where(lane < NOPE, kvv, kr_ref[...]), kg_ref[...])
        dkg_ref[...] += dkg
        dkvr_ref[...] = jnp.where(lane < NOPE, dkz, dv_ref[...].astype(f32)).astype(bf16)
        dkr_ref[...] += jnp.where(is_rope, dkz, 0.0)

    head = pl.BlockSpec((ts, HEAD_PAD), lambda i, h: (i, h))
    row = pl.BlockSpec((ts, HEAD_PAD), lambda i, h: (i, 0))
    vec = pl.BlockSpec((1, HEAD_PAD), lambda i, h: (0, 0))
    wide = jax.ShapeDtypeStruct((s, HEADS * HEAD_PAD), bf16)
    return pl.pallas_call(
        body, name=name, grid=(s // ts, HEADS),
        in_specs=[head, head, head, head, head, pl.BlockSpec((ts, HEAD_PAD), lambda i, h: (i, rope_blk)), vec, vec, row, row],
        out_specs=[head, head, row, vec, vec],
        out_shape=[wide, wide, jax.ShapeDtypeStruct((s, HEAD_PAD), f32), jax.ShapeDtypeStruct((1, HEAD_PAD), f32),
                   jax.ShapeDtypeStruct((1, HEAD_PAD), f32)],
        compiler_params=_cparams(("arbitrary", "arbitrary")))(dq, dk, dv, q_raw, kv_raw, proj, qg, kg, cos_t, sin_t)


_NT = (((1,), (1,)), ((), ()))
_SCALE = QK ** -0.5
_LOG2E = math.log2(math.e)
_Q_FOLD = _SCALE * _LOG2E
FLASH_TILE = 1024


def _flash_fwd(q, k, v, *, name, tq=FLASH_TILE):
    s = q.shape[0]
    tq = min(tq, s)

    def body(q_ref, k_ref, v_ref, o_ref, lse_ref):
        i = pl.program_id(1)
        qv = q_ref[...]

        def step(j, carry, masked):
            m, l, acc = carry
            st = pl.multiple_of(j * tq, tq)
            kj, vj = k_ref[pl.ds(st, tq), :], v_ref[pl.ds(st, tq), :]
            sc = lax.dot_general(qv, kj, _NT, preferred_element_type=f32)
            if masked:
                rr = lax.broadcasted_iota(jnp.int32, (tq, tq), 0)
                cc = lax.broadcasted_iota(jnp.int32, (tq, tq), 1)
                sc = jnp.where(cc <= rr, sc, NEG)
            m_new = jnp.maximum(m, jnp.max(sc, axis=-1, keepdims=True))
            p = jnp.exp2(sc - m_new)
            alpha = jnp.exp2(m - m_new)
            l = alpha * l + jnp.sum(p, axis=-1, keepdims=True)
            acc = alpha * acc + jnp.dot(p.astype(bf16), vj, preferred_element_type=f32)
            return m_new, l, acc

        init = (jnp.full((tq, 1), NEG, f32), jnp.zeros((tq, 1), f32), jnp.zeros((tq, HEAD_PAD), f32))
        carry = lax.fori_loop(0, i, lambda j, c: step(j, c, False), init)
        m, l, acc = step(i, carry, True)
        o_ref[...] = (acc / l).astype(bf16)
        lse_ref[0] = m + jnp.log2(l)

    blk = pl.BlockSpec((tq, HEAD_PAD), lambda h, i: (i, h))
    full = pl.BlockSpec((s, HEAD_PAD), lambda h, i: (0, h))
    return pl.pallas_call(
        body, name=name, grid=(HEADS, s // tq), in_specs=[blk, full, full],
        out_specs=[blk, pl.BlockSpec((1, tq, 1), lambda h, i: (h, i, 0))],
        out_shape=[jax.ShapeDtypeStruct((s, HEADS * HEAD_PAD), bf16), jax.ShapeDtypeStruct((HEADS, s, 1), f32)],
        compiler_params=_cparams(("parallel", "arbitrary")))(q, k, v)


def _flash_bwd_dq(q, k, v, o, do, lse, *, name, tq=FLASH_TILE):
    s = q.shape[0]
    tq = min(tq, s)

    def body(q_ref, k_ref, v_ref, o_ref, do_ref, lse_ref, dq_ref, dl_ref):
        i = pl.program_id(1)
        qv = q_ref[...]
        dov = do_ref[...].astype(f32)
        delta = jnp.sum(dov * o_ref[...].astype(f32), axis=-1, keepdims=True)
        dob = dov.astype(bf16)
        lsev = lse_ref[0]

        def step(j, acc, masked):
            st = pl.multiple_of(j * tq, tq)
            kj, vj = k_ref[pl.ds(st, tq), :], v_ref[pl.ds(st, tq), :]
            sc = lax.dot_general(qv, kj, _NT, preferred_element_type=f32)
            p = jnp.exp2(sc - lsev)
            if masked:
                rr = lax.broadcasted_iota(jnp.int32, (tq, tq), 0)
                cc = lax.broadcasted_iota(jnp.int32, (tq, tq), 1)
                p = jnp.where(cc <= rr, p, 0.0)
            dp = lax.dot_general(dob, vj, _NT, preferred_element_type=f32)
            ds = p * (dp - delta)
            return acc + jnp.dot(ds.astype(bf16), kj, preferred_element_type=f32)

        acc = lax.fori_loop(0, i, lambda j, c: step(j, c, False), jnp.zeros((tq, HEAD_PAD), f32))
        dq_ref[...] = step(i, acc, True) * _SCALE
        dl_ref[0] = delta

    blk = pl.BlockSpec((tq, HEAD_PAD), lambda h, i: (i, h))
    full = pl.BlockSpec((s, HEAD_PAD), lambda h, i: (0, h))
    col = pl.BlockSpec((1, tq, 1), lambda h, i: (h, i, 0))
    return pl.pallas_call(
        body, name=name, grid=(HEADS, s // tq), in_specs=[blk, full, full, blk, blk, col],
        out_specs=[blk, col],
        out_shape=[jax.ShapeDtypeStruct((s, HEADS * HEAD_PAD), f32), jax.ShapeDtypeStruct((HEADS, s, 1), f32)],
        compiler_params=_cparams(("parallel", "arbitrary")))(q, k, v, o, do, lse)


def _flash_bwd_dkv(q, k, v, do, lse_row, delta_row, *, name, tk=FLASH_TILE):
    s = q.shape[0]
    tk = min(tk, s)
    nblk = s // tk

    def body(q_ref, k_ref, v_ref, do_ref, lse_ref, dl_ref, dk_ref, dv_ref):
        j = pl.program_id(1)
        kv_, vv = k_ref[...], v_ref[...]

        def step(i, carry, masked):
            dk, dv = carry
            st = pl.multiple_of(i * tk, tk)
            qi = q_ref[pl.ds(st, tk), :]
            doi = do_ref[pl.ds(st, tk), :].astype(bf16)
            lse_i = lse_ref[0, :, pl.ds(st, tk)]
            dl_i = dl_ref[0, :, pl.ds(st, tk)]
            st_ = lax.dot_general(kv_, qi, _NT, preferred_element_type=f32)
            pt = jnp.exp2(st_ - lse_i)
            if masked:
                kk = lax.broadcasted_iota(jnp.int32, (tk, tk), 0)
                qq = lax.broadcasted_iota(jnp.int32, (tk, tk), 1)
                pt = jnp.where(kk <= qq, pt, 0.0)
            dv = dv + jnp.dot(pt.astype(bf16), doi, preferred_element_type=f32)
            dpt = lax.dot_general(vv, doi, _NT, preferred_element_type=f32)
            dst = pt * (dpt - dl_i)
            dk = dk + jnp.dot(dst.astype(bf16), qi, preferred_element_type=f32)
            return dk, dv

        zero = jnp.zeros((tk, HEAD_PAD), f32)
        carry = step(j, (zero, zero), True)
        dk, dv = lax.fori_loop(j + 1, nblk, lambda i, c: step(i, c, False), carry)
        dk_ref[...] = dk * (1.0 / _LOG2E)
        dv_ref[...] = dv

    blk = pl.BlockSpec((tk, HEAD_PAD), lambda h, j: (j, h))
    full = pl.BlockSpec((s, HEAD_PAD), lambda h, j: (0, h))
    rowv = pl.BlockSpec((1, 1, s), lambda h, j: (h, 0, 0))
    out = jax.ShapeDtypeStruct((s, HEADS * HEAD_PAD), f32)
    return pl.pallas_call(
        body, name=name, grid=(HEADS, nblk), in_specs=[full, blk, blk, full, rowv, rowv],
        out_specs=[blk, blk], out_shape=[out, out],
        compiler_params=_cparams(("parallel", "arbitrary")))(q, k, v, do, lse_row, delta_row)


SUBLANES = 8


def _shift_down(x, d):
    r = pltpu.roll(x, d, axis=0)
    t = lax.broadcasted_iota(jnp.int32, (SUBLANES, x.shape[1]), 0)
    head = jnp.where(t < d, 0.0, r[:SUBLANES])
    return head if x.shape[0] == SUBLANES else jnp.concatenate([head, r[SUBLANES:]], axis=0)


def _shift_up(x, d):
    s = x.shape[0]
    r = pltpu.roll(x, s - d, axis=0)
    t = lax.broadcasted_iota(jnp.int32, (SUBLANES, x.shape[1]), 0)
    tail = jnp.where(t >= SUBLANES - d, 0.0, r[s - SUBLANES:])
    return tail if s == SUBLANES else jnp.concatenate([r[:s - SUBLANES], tail], axis=0)


def _taps(w_ref):
    return w_ref[0:1, :], w_ref[1:2, :], w_ref[2:3, :]


def _conv3(u, w):
    u1, u2 = _shift_down(u, 1), _shift_down(u, 2)
    return w[0] * u2 + w[1] * u1 + w[2] * u, (u1, u2)


def _conv3_t(g, w):
    return w[2] * g + w[1] * _shift_up(g, 1) + w[0] * _shift_up(g, 2)


def _conv3_dw(dw_ref, g, u, shifted):
    dw_ref[0:1, :] = jnp.sum(g * shifted[1], axis=0, keepdims=True)
    dw_ref[1:2, :] = jnp.sum(g * shifted[0], axis=0, keepdims=True)
    dw_ref[2:3, :] = jnp.sum(g * u, axis=0, keepdims=True)


_GB, _GC, _CI = 512 // LANES, 1024 // LANES, 1536 // LANES


def _sconv_fwd(proj, w, *, name):
    s = proj.shape[0]

    def body(gb_ref, gc_ref, ci_ref, w_ref, o_ref):
        o_ref[...] = (gb_ref[...] * _conv3(gc_ref[...] * ci_ref[...], _taps(w_ref))[0]).astype(bf16)

    col = lambda off: pl.BlockSpec((s, LANES), lambda j: (0, off + j))
    return pl.pallas_call(
        body, name=name, grid=(CONV_CH // LANES,),
        in_specs=[col(_GB), col(_GC), col(_CI), pl.BlockSpec((3, LANES), lambda j: (0, j))],
        out_specs=pl.BlockSpec((s, LANES), lambda j: (0, j)),
        out_shape=jax.ShapeDtypeStruct((s, CONV_CH), bf16),
        compiler_params=_cparams(("parallel",)))(proj, proj, proj, w)


def _sconv_bwd(dmix, proj, w, *, name):
    s = proj.shape[0]

    def body(do_ref, gb_ref, gc_ref, ci_ref, w_ref, dgb_ref, dgc_ref, dci_ref, dw_ref):
        wv, gc, ci, do = _taps(w_ref), gc_ref[...], ci_ref[...], do_ref[...].astype(f32)
        u = gc * ci
        conv, shifted = _conv3(u, wv)
        dgb_ref[...] = (do * conv).astype(bf16)
        dc = do * gb_ref[...]
        du = _conv3_t(dc, wv)
        dgc_ref[...] = (du * ci).astype(bf16)
        dci_ref[...] = (du * gc).astype(bf16)
        _conv3_dw(dw_ref, dc, u, shifted)

    col = lambda off: pl.BlockSpec((s, LANES), lambda j: (0, off + j))
    out = jax.ShapeDtypeStruct((s, CONV_CH), bf16)
    return pl.pallas_call(
        body, name=name, grid=(CONV_CH // LANES,),
        in_specs=[col(HEADS), col(_GB), col(_GC), col(_CI), pl.BlockSpec((3, LANES), lambda j: (0, j))],
        out_specs=[col(0), col(0), col(0), pl.BlockSpec((3, LANES), lambda j: (0, j))],
        out_shape=[out, out, out, jax.ShapeDtypeStruct((3, CONV_CH), f32)],
        compiler_params=_cparams(("parallel",)))(dmix, proj, proj, proj, w)


def _ffn_act_fwd(zg, zv, cwg, cwv, *, name):
    s, f = zg.shape

    def body(zg_ref, zv_ref, wg_ref, wv_ref, o_ref):
        o_ref[...] = (jax.nn.silu(_conv3(zg_ref[...], _taps(wg_ref))[0]) * _conv3(zv_ref[...], _taps(wv_ref))[0]).astype(bf16)

    col = pl.BlockSpec((s, LANES), lambda j: (0, j))
    wsp = pl.BlockSpec((3, LANES), lambda j: (0, j))
    return pl.pallas_call(
        body, name=name, grid=(f // LANES,), in_specs=[col, col, wsp, wsp], out_specs=col,
        out_shape=jax.ShapeDtypeStruct((s, f), bf16), compiler_params=_cparams(("parallel",)))(zg, zv, cwg, cwv)


def _ffn_act_bwd(da, zg, zv, cwg, cwv, *, name):
    s, f = zg.shape

    def body(da_ref, zg_ref, zv_ref, wg_ref, wv_ref, dzg_ref, dzv_ref, dwg_ref, dwv_ref):
        wg, wv, zgv, zvv, dav = _taps(wg_ref), _taps(wv_ref), zg_ref[...], zv_ref[...], da_ref[...].astype(f32)
        (ug, zg_shifted), (uv, zv_shifted) = _conv3(zgv, wg), _conv3(zvv, wv)
        sg = jax.nn.sigmoid(ug)
        dug = dav * uv * (sg * (1.0 + ug * (1.0 - sg)))
        duv = dav * (ug * sg)
        dzg_ref[...] = _conv3_t(dug, wg).astype(bf16)
        dzv_ref[...] = _conv3_t(duv, wv).astype(bf16)
        _conv3_dw(dwg_ref, dug, zgv, zg_shifted)
        _conv3_dw(dwv_ref, duv, zvv, zv_shifted)

    col = pl.BlockSpec((s, LANES), lambda j: (0, j))
    wsp = pl.BlockSpec((3, LANES), lambda j: (0, j))
    act, wsh = jax.ShapeDtypeStruct((s, f), bf16), jax.ShapeDtypeStruct((3, f), f32)
    return pl.pallas_call(
        body, name=name, grid=(f // LANES,), in_specs=[col, col, col, wsp, wsp], out_specs=[col, col, wsp, wsp],
        out_shape=[act, act, wsh, wsh], compiler_params=_cparams(("parallel",)))(da, zg, zv, cwg, cwv)


def _expand_mat():
    return jnp.asarray(np.kron(np.eye(STATE, dtype=np.float32), np.ones((1, GROUP), np.float32)))


def _disc_fn(lr, li, ls, br, bi, e):
    dt = jnp.exp(ls)
    mag = jnp.exp(lr * dt)
    ar, ai = mag * jnp.cos(li * dt), mag * jnp.sin(li * dt)
    nr, ni = ar - 1.0, ai
    den = lr * lr + li * li
    zr, zi = (nr * lr + ni * li) / den, (ni * lr - nr * li) / den
    zrr = jnp.dot(zr, e, precision=lax.Precision.HIGHEST, preferred_element_type=f32)
    zir = jnp.dot(zi, e, precision=lax.Precision.HIGHEST, preferred_element_type=f32)
    return ar, ai, zrr * br - zir * bi, zrr * bi + zir * br


def _disc_fwd(lr, li, ls, br, bi, *, name):
    def body(lr_ref, li_ref, ls_ref, br_ref, bi_ref, e_ref, ar_ref, ai_ref, bbr_ref, bbi_ref):
        ar, ai, bbr, bbi = _disc_fn(lr_ref[...], li_ref[...], ls_ref[...], br_ref[...], bi_ref[...], e_ref[...])
        ar_ref[...], ai_ref[...], bbr_ref[...], bbi_ref[...] = ar, ai, bbr, bbi

    sq, wide = jax.ShapeDtypeStruct((GROUPS, STATE), f32), jax.ShapeDtypeStruct((GROUPS, STATE * GROUP), f32)
    return pl.pallas_call(body, name=name, out_shape=[sq, sq, wide, wide],
                          compiler_params=_cparams())(lr, li, ls, br, bi, _expand_mat())


def _disc_bwd(lr, li, ls, br, bi, dar, dai, dbbr, dbbi, *, name):
    def body(lr_ref, li_ref, ls_ref, br_ref, bi_ref, e_ref, dar_ref, dai_ref, dbbr_ref, dbbi_ref,
             dlr_ref, dli_ref, dls_ref, dbr_ref, dbi_ref):
        ev = e_ref[...]
        _, vjp = jax.vjp(lambda a, b, c, d_, e_: _disc_fn(a, b, c, d_, e_, ev),
                         lr_ref[...], li_ref[...], ls_ref[...], br_ref[...], bi_ref[...])
        dlr, dli, dls, dbr, dbi = vjp((dar_ref[...], dai_ref[...], dbbr_ref[...], dbbi_ref[...]))
        dlr_ref[...], dli_ref[...], dls_ref[...], dbr_ref[...], dbi_ref[...] = dlr, dli, dls, dbr, dbi

    sq, wide = jax.ShapeDtypeStruct((GROUPS, STATE), f32), jax.ShapeDtypeStruct((GROUPS, STATE * GROUP), f32)
    return pl.pallas_call(body, name=name, out_shape=[sq, sq, jax.ShapeDtypeStruct((GROUPS, 1), f32), wide, wide],
                          compiler_params=_cparams())(lr, li, ls, br, bi, _expand_mat(), dar, dai, dbbr, dbbi)


SCAN_TILE = 64
SCAN_PAIRS = 2


def _tile_shift(v, d, reverse):
    if d % 8:
        return _shift_up(v, d) if reverse else _shift_down(v, d)
    z = jnp.zeros((d, v.shape[1]), v.dtype)
    return jnp.concatenate([v[d:], z], axis=0) if reverse else jnp.concatenate([z, v[:v.shape[0] - d]], axis=0)


def _tile_scan(r, i, pows, reverse):
    d = 1
    for br, bi in pows:
        rs, is_ = _tile_shift(r, d, reverse), _tile_shift(i, d, reverse)
        r, i = r + br * rs - bi * is_, i + br * is_ + bi * rs
        d *= 2
    return r, i


def _scan_setup(ar, ai, reverse):
    if reverse:
        ai = -ai
    pows, br, bi, d = [], ar, ai, 1
    while d < SCAN_TILE:
        pows.append((br, bi))
        br, bi, d = br * br - bi * bi, 2.0 * br * bi, 2 * d
    row = lax.broadcasted_iota(jnp.int32, (SCAN_TILE, LANES), 0)
    hit = row == (SCAN_TILE - 1 if reverse else 0)
    pr, pi = _tile_scan(jnp.where(hit, ar, 0.0), jnp.where(hit, ai, 0.0), pows, reverse)
    return pows, pr, pi


def _carry_in(r, i, pr, pi, cr, ci):
    crb, cib = jnp.broadcast_to(cr, r.shape), jnp.broadcast_to(ci, i.shape)
    return r + pr * crb - pi * cib, i + pr * cib + pi * crb


def _pair_cols(q):
    return slice(q * PAIR_LANES, q * PAIR_LANES + LANES), slice(q * PAIR_LANES + LANES, (q + 1) * PAIR_LANES)


_SCAN_W = SCAN_PAIRS * PAIR_LANES


def _scan_specs(s, w):
    per = w.shape[2] // _SCAN_W
    src = pl.BlockSpec((s, LANES), lambda g: (0, g // per))
    mat = pl.BlockSpec((1, LANES, _SCAN_W), lambda g: (g // per, 0, g % per))
    col = pl.BlockSpec((s, _SCAN_W), lambda g: (0, g))
    vec = pl.BlockSpec((SCAN_PAIRS, 1, LANES), lambda g: (g, 0, 0))
    return src, mat, col, vec, (w.shape[0] * per,)


def _scan_fwd(u, wb, ar, ai, *, name):
    s = u.shape[0]
    nt = s // SCAN_TILE

    def body(u_ref, w_ref, ar_ref, ai_ref, x_ref):
        setups = [_scan_setup(ar_ref[q], ai_ref[q], False) for q in range(SCAN_PAIRS)]
        wv = w_ref[0]

        def tile_rows(k):
            return pl.ds(pl.multiple_of(k * SCAN_TILE, SCAN_TILE), SCAN_TILE)

        def tile_in(k):
            return jnp.dot(u_ref[tile_rows(k), :].astype(bf16), wv, preferred_element_type=f32)

        def step(k, carry):
            rows, bu = tile_rows(k), carry[-1]
            ahead = tile_in(jnp.minimum(k + 1, nt - 1))
            out = []
            for q, (pows, pr, pi) in enumerate(setups):
                rc, ic = _pair_cols(q)
                r, i = _tile_scan(bu[:, rc], bu[:, ic], pows, False)
                r, i = _carry_in(r, i, pr, pi, carry[2 * q], carry[2 * q + 1])
                x_ref[rows, rc] = r
                x_ref[rows, ic] = i
                out += [r[SCAN_TILE - 1:SCAN_TILE, :], i[SCAN_TILE - 1:SCAN_TILE, :]]
            return tuple(out) + (ahead,)

        lax.fori_loop(0, nt, step, tuple(jnp.zeros((1, LANES), f32) for _ in range(2 * SCAN_PAIRS)) + (tile_in(0),))

    src, mat, col, vec, grid = _scan_specs(s, wb)
    return pl.pallas_call(body, name=name, grid=grid, in_specs=[src, mat, vec, vec], out_specs=col,
                          out_shape=jax.ShapeDtypeStruct((s, wb.shape[0] * wb.shape[2]), f32),
                          compiler_params=_cparams(("parallel",)))(u, wb, ar, ai)


def _scan_bwd(dy, cbt, x, ar, ai, *, name):
    s = dy.shape[0]
    nt = s // SCAN_TILE

    def fold(v):
        out = v[0:8]
        for r in range(8, SCAN_TILE, 8):
            out = out + v[r:r + 8]
        return out

    def body(dy_ref, w_ref, x_ref, ar_ref, ai_ref, g_ref, dar_ref, dai_ref):
        setups = [_scan_setup(ar_ref[q], ai_ref[q], True) for q in range(SCAN_PAIRS)]
        row = lax.broadcasted_iota(jnp.int32, (SCAN_TILE, LANES), 0)
        wv = w_ref[0]

        def tile_in(k):
            return jnp.dot(dy_ref[pl.ds(pl.multiple_of(k * SCAN_TILE, SCAN_TILE), SCAN_TILE), :], wv, preferred_element_type=f32)

        def step(kk, carry):
            k = nt - 1 - kk
            start = pl.multiple_of(k * SCAN_TILE, SCAN_TILE)
            rows = pl.ds(start, SCAN_TILE)
            prev8 = pl.ds(pl.multiple_of(jnp.maximum(start - 8, 0), 8), 8)
            dx = carry[-1]
            ahead = tile_in(jnp.maximum(k - 1, 0))

            def before(cols):
                first = jnp.where(k > 0, x_ref[prev8, cols][7:8, :], 0.0)
                return jnp.where(row == 0, first, pltpu.roll(x_ref[rows, cols], 1, axis=0))

            out = []
            for q, (pows, pr, pi) in enumerate(setups):
                rc, ic = _pair_cols(q)
                cr, ci, acc_r, acc_i = carry[4 * q:4 * q + 4]
                gr, gi = _tile_scan(dx[:, rc], dx[:, ic], pows, True)
                gr, gi = _carry_in(gr, gi, pr, pi, cr, ci)
                g_ref[rows, rc] = gr.astype(bf16)
                g_ref[rows, ic] = gi.astype(bf16)
                xr, xi = before(rc), before(ic)
                out += [gr[0:1, :], gi[0:1, :], acc_r + fold(gr * xr + gi * xi), acc_i + fold(gi * xr - gr * xi)]
            return tuple(out) + (ahead,)

        init = (jnp.zeros((1, LANES), f32), jnp.zeros((1, LANES), f32), jnp.zeros((8, LANES), f32), jnp.zeros((8, LANES), f32))
        res = lax.fori_loop(0, nt, step, init * SCAN_PAIRS + (tile_in(nt - 1),))
        for q in range(SCAN_PAIRS):
            dar_ref[q] = jnp.sum(res[4 * q + 2], axis=0, keepdims=True)
            dai_ref[q] = jnp.sum(res[4 * q + 3], axis=0, keepdims=True)

    src, mat, col, vec, grid = _scan_specs(s, cbt)
    vsh = jax.ShapeDtypeStruct((GROUPS // 2, 1, LANES), f32)
    return pl.pallas_call(body, name=name, grid=grid, in_specs=[src, mat, col, vec, vec],
                          out_specs=[col, vec, vec], out_shape=[jax.ShapeDtypeStruct(x.shape, bf16), vsh, vsh],
                          compiler_params=_cparams(("parallel",)))(dy, cbt, x, ar, ai)


_GELU_C = math.sqrt(2.0 / math.pi)


def _gelu_fwd(y, u, dsk, *, name, ts=512):
    s, d = y.shape
    ts = min(ts, s)

    def body(y_ref, u_ref, d_ref, o_ref):
        o_ref[...] = jax.nn.gelu(y_ref[...] + d_ref[...] * u_ref[...]).astype(bf16)

    row, vec = pl.BlockSpec((ts, d), lambda i: (i, 0)), pl.BlockSpec((1, d), lambda i: (0, 0))
    return pl.pallas_call(body, name=name, grid=(s // ts,), in_specs=[row, row, vec], out_specs=row,
                          out_shape=jax.ShapeDtypeStruct((s, d), bf16), compiler_params=_cparams(("parallel",)))(y, u, dsk)


def _gelu_bwd(dg, y, u, dsk, *, name, ts=512):
    s, d = y.shape
    ts = min(ts, s)

    def body(dg_ref, y_ref, u_ref, d_ref, dy_ref, du_ref, dd_ref):
        @pl.when(pl.program_id(0) == 0)
        def _():
            dd_ref[...] = jnp.zeros_like(dd_ref)

        uv, dv = u_ref[...], d_ref[...]
        z = y_ref[...] + dv * uv
        th = jnp.tanh(_GELU_C * (z + 0.044715 * z * z * z))
        dz = dg_ref[...] * (0.5 * (1.0 + th) + 0.5 * z * (1.0 - th * th) * _GELU_C * (1.0 + 3 * 0.044715 * z * z))
        dy_ref[...] = dz.astype(bf16)
        du_ref[...] = dz * dv
        dd_ref[...] += jnp.sum(dz * uv, axis=0, keepdims=True)

    row, vec = pl.BlockSpec((ts, d), lambda i: (i, 0)), pl.BlockSpec((1, d), lambda i: (0, 0))
    return pl.pallas_call(
        body, name=name, grid=(s // ts,), in_specs=[row, row, row, vec], out_specs=[row, row, vec],
        out_shape=[jax.ShapeDtypeStruct((s, d), bf16), jax.ShapeDtypeStruct((s, d), f32), jax.ShapeDtypeStruct((1, d), f32)],
        compiler_params=_cparams(("arbitrary",)))(dg, y, u, dsk)


def _glu_fwd(x, a, b, *, name, ts=512):
    s, d = x.shape
    ts = min(ts, s)

    def body(x_ref, a_ref, b_ref, o_ref):
        o_ref[...] = x_ref[...] + a_ref[...] * jax.nn.sigmoid(b_ref[...])

    row = pl.BlockSpec((ts, d), lambda i: (i, 0))
    return pl.pallas_call(body, name=name, grid=(s // ts,), in_specs=[row, row, row], out_specs=row,
                          out_shape=jax.ShapeDtypeStruct((s, d), f32), compiler_params=_cparams(("parallel",)))(x, a, b)


def _glu_bwd(dx, a, b, *, name, ts=512):
    s, d = dx.shape
    ts = min(ts, s)

    def body(dx_ref, a_ref, b_ref, da_ref, db_ref):
        sg = jax.nn.sigmoid(b_ref[...])
        dxv = dx_ref[...]
        da_ref[...] = (dxv * sg).astype(bf16)
        db_ref[...] = (dxv * a_ref[...] * sg * (1.0 - sg)).astype(bf16)

    row = pl.BlockSpec((ts, d), lambda i: (i, 0))
    out = jax.ShapeDtypeStruct((s, d), bf16)
    return pl.pallas_call(body, name=name, grid=(s // ts,), in_specs=[row, row, row], out_specs=[row, row],
                          out_shape=[out, out], compiler_params=_cparams(("parallel",)))(dx, a, b)


def _add(a, b, *, name, ts=512):
    s, d = a.shape
    ts = min(ts, s)

    def body(a_ref, b_ref, o_ref):
        o_ref[...] = (a_ref[...].astype(f32) + b_ref[...].astype(f32)).astype(bf16)

    row = pl.BlockSpec((ts, d), lambda i: (i, 0))
    return pl.pallas_call(body, name=name, grid=(s // ts,), in_specs=[row, row], out_specs=row,
                          out_shape=jax.ShapeDtypeStruct((s, d), bf16), compiler_params=_cparams(("parallel",)))(a, b)


def _loss_head(y, target, *, name, ts=512):
    s, d = y.shape
    ts = min(ts, s)

    def body(y_ref, t_ref, dy_ref, dyb_ref, l_ref):
        @pl.when(pl.program_id(0) == 0)
        def _():
            l_ref[...] = jnp.zeros_like(l_ref)

        e = y_ref[...] - t_ref[...]
        dy = e * (1.0 / d)
        dy_ref[...] = dy
        dyb_ref[...] = dy.astype(bf16)
        l_ref[...] += 0.5 * jnp.sum(jnp.mean(e * e, axis=-1, keepdims=True))

    row = pl.BlockSpec((ts, d), lambda i: (i, 0))
    return pl.pallas_call(
        body, name=name, grid=(s // ts,), in_specs=[row, row],
        out_specs=[row, row, pl.BlockSpec((8, LANES), lambda i: (0, 0))],
        out_shape=[jax.ShapeDtypeStruct((s, d), f32), jax.ShapeDtypeStruct((s, d), bf16), jax.ShapeDtypeStruct((8, LANES), f32)],
        compiler_params=_cparams(("arbitrary",)))(y, target)


def _adamw(w, g, m, v, *, name, tr=128):
    r, c = w.shape

    def body(w_ref, g_ref, m_ref, v_ref, d_ref, mo_ref, vo_ref):
        gv = g_ref[...]
        mn = ADAM_B1 * m_ref[...] + (1.0 - ADAM_B1) * gv
        vn = ADAM_B2 * v_ref[...] + (1.0 - ADAM_B2) * (gv * gv)
        m_hat = mn / (1.0 - ADAM_B1 ** ADAM_STEP)
        v_hat = vn / (1.0 - ADAM_B2 ** ADAM_STEP)
        d_ref[...] = -ADAM_LR * (m_hat / (jnp.sqrt(v_hat) + ADAM_EPS) + ADAM_WD * w_ref[...])
        mo_ref[...] = mn
        vo_ref[...] = vn

    row = pl.BlockSpec((tr, c), lambda i: (i, 0))
    out = jax.ShapeDtypeStruct((r, c), f32)
    return pl.pallas_call(body, name=name, grid=(r // tr,), in_specs=[row] * 4, out_specs=[row] * 3,
                          out_shape=[out, out, out], compiler_params=_cparams(("parallel",)))(w, g, m, v)


def _sum_slabs(land, *, name, tr=128):
    n, r, c = land.shape

    def body(l_ref, o_ref):
        acc = l_ref[0].astype(f32)
        for i in range(1, n):
            acc = acc + l_ref[i].astype(f32)
        o_ref[...] = acc

    return pl.pallas_call(body, name=name, grid=(r // tr,), in_specs=[pl.BlockSpec((n, tr, c), lambda i: (0, i, 0))],
                          out_specs=pl.BlockSpec((tr, c), lambda i: (i, 0)), out_shape=jax.ShapeDtypeStruct((r, c), f32),
                          compiler_params=_cparams(("parallel",)))(land)


def _pair_sum(g, theirs, *, name, tr=256):
    n, r, c = theirs.shape

    def body(c_ref, g_ref, t_ref, o_ref):
        o_ref[...] = (g_ref[...].astype(f32) + t_ref[...].astype(f32)).astype(bf16)

    blk = pl.BlockSpec((1, tr, c), lambda j, i, c_ref: (j, i, 0))
    mine = pl.BlockSpec((1, tr, c), lambda j, i, c_ref: (2 * j + c_ref[0], i, 0))
    return pl.pallas_call(
        body, name=name,
        grid_spec=pltpu.PrefetchScalarGridSpec(num_scalar_prefetch=1, grid=(n, r // tr), in_specs=[mine, blk], out_specs=blk),
        out_shape=jax.ShapeDtypeStruct(theirs.shape, bf16),
        compiler_params=_cparams(("parallel", "parallel")))(lax.axis_index("c").astype(jnp.int32).reshape(1), g, theirs)


_MESH = pl.DeviceIdType.MESH
_HBM = pl.BlockSpec(memory_space=pltpu.HBM)
N_CHIP = N_DEV // 2


def _position():
    return lax.axis_index("x"), lax.axis_index("y"), lax.axis_index("c")


def _gather8(x, *, name):
    def body(x_ref, o_ref, send_sems, recv_sems, local_sem):
        xx, yy, cc = _position()
        me, sibling = (xx, yy, cc), (xx, yy, 1 - cc)
        chips = [(1 - xx, yy), (xx, 1 - yy), (1 - xx, 1 - yy)]

        def slab(px, py, pc):
            return o_ref.at[4 * px + 2 * py + pc]

        def copy(k, block, to, src=None):
            return pltpu.make_async_remote_copy(src_ref=slab(*block) if src is None else src, dst_ref=slab(*block),
                                                send_sem=send_sems.at[k], recv_sem=recv_sems.at[k], device_id=to,
                                                device_id_type=_MESH)

        mine = pltpu.make_async_copy(x_ref, slab(*me), local_sem)
        mine.start()
        first = [copy(0, me, sibling, src=x_ref)] + [copy(1 + j, me, (*chip, cc), src=x_ref) for j, chip in enumerate(chips)]
        for cp in first:
            cp.start()
        passed = [copy(4 + j, (*chip, cc), sibling) for j, chip in enumerate(chips)]
        for j, chip in enumerate(chips):
            copy(1 + j, (*chip, cc), me).wait_recv()
            passed[j].start()
        copy(0, sibling, me).wait_recv()
        for j, chip in enumerate(chips):
            copy(4 + j, (*chip, 1 - cc), me).wait_recv()
        for cp in first + passed:
            cp.wait_send()
        mine.wait()

    return pl.pallas_call(
        body, name=name, in_specs=[_HBM], out_specs=_HBM, out_shape=jax.ShapeDtypeStruct((N_DEV,) + x.shape, x.dtype),
        scratch_shapes=[pltpu.SemaphoreType.DMA((N_DEV - 1,)), pltpu.SemaphoreType.DMA((N_DEV - 1,)), pltpu.SemaphoreType.DMA],
    )(x)


def _pair_exchange(g, *, name):
    def body(g_ref, land_ref, send_sems, recv_sems):
        xx, yy, cc = _position()
        copies = []
        for j in range(N_CHIP):
            cp = pltpu.make_async_remote_copy(src_ref=g_ref.at[2 * j + 1 - cc], dst_ref=land_ref.at[j], send_sem=send_sems.at[j],
                                              recv_sem=recv_sems.at[j], device_id=(xx, yy, 1 - cc), device_id_type=_MESH)
            cp.start()
            copies.append(cp)
        for cp in copies:
            cp.wait_recv()
        for cp in copies:
            cp.wait_send()

    sems = pltpu.SemaphoreType.DMA((N_CHIP,))
    return pl.pallas_call(body, name=name, in_specs=[_HBM], out_specs=_HBM,
                          out_shape=jax.ShapeDtypeStruct((N_CHIP,) + g.shape[1:], g.dtype), scratch_shapes=[sems, sems])(g)


def _cross_exchange(p, *, name):
    def body(p_ref, o_ref, send_sems, recv_sems, local_sem):
        xx, yy, cc = _position()
        my_chip = 2 * xx + yy
        local = pltpu.make_async_copy(p_ref.at[my_chip], o_ref.at[my_chip], local_sem)
        local.start()
        chips = [(1 - xx, yy), (xx, 1 - yy), (1 - xx, 1 - yy)]
        sends = []
        for k, (px, py) in enumerate(chips):
            cp = pltpu.make_async_remote_copy(src_ref=p_ref.at[2 * px + py], dst_ref=o_ref.at[my_chip], send_sem=send_sems.at[k],
                                              recv_sem=recv_sems.at[k], device_id=(px, py, cc), device_id_type=_MESH)
            cp.start()
            sends.append(cp)
        for k, (px, py) in enumerate(chips):
            pltpu.make_async_remote_copy(src_ref=p_ref.at[2 * px + py], dst_ref=o_ref.at[2 * px + py], send_sem=send_sems.at[k],
                                         recv_sem=recv_sems.at[k], device_id=(px, py, cc), device_id_type=_MESH).wait_recv()
        for cp in sends:
            cp.wait_send()
        local.wait()

    sems = pltpu.SemaphoreType.DMA((N_CHIP - 1,))
    return pl.pallas_call(body, name=name, in_specs=[_HBM], out_specs=_HBM, out_shape=jax.ShapeDtypeStruct(p.shape, p.dtype),
                          scratch_shapes=[sems, sems, pltpu.SemaphoreType.DMA])(p)


def _all_sum(x, *, name):
    return _sum_slabs(_gather8(x, name=f"gather_{name}"), name=f"sum_{name}", tr=min(128, x.shape[0]))


def _pack_slabs(parts, rows, axis=0):
    lead = parts[0].shape[:axis]
    slabs = [p.reshape(lead + (-1, D)) for p in parts]
    used = sum(sl.shape[axis] for sl in slabs)
    return jnp.concatenate(slabs + [jnp.zeros(lead + (rows - used, D), slabs[0].dtype)], axis=axis)


def _unpack_slabs(slab, shapes):
    lead, out, off = slab.shape[:-2], [], 0
    for shp in shapes:
        n = int(np.prod(shp)) // D
        out.append(slab[..., off:off + n, :].reshape(lead + tuple(shp)))
        off += n
    return out


def _pack_rows(parts, rows):
    flat = jnp.concatenate([p.reshape(-1) for p in parts])
    return jnp.pad(flat, (0, rows * D - flat.shape[0])).reshape(rows, D)


def _unpack_rows(slab, shapes):
    flat, out, off = slab.reshape(-1), [], 0
    for shp in shapes:
        n = int(np.prod(shp))
        out.append(flat[off:off + n].reshape(shp))
        off += n
    return out


def _full_shape(shard, axis):
    return tuple(d * N_DEV if i == axis else d for i, d in enumerate(shard))


def _row(v):
    return v.reshape(1, -1).astype(f32)


def _pad_gain(g):
    return jnp.pad(g.astype(f32), (0, HEAD_PAD - QK)).reshape(1, HEAD_PAD)


def _ffn_fwd(x, p, tag):
    h = _rms_fwd(x, p["norm"], name=f"ffn_norm_{tag}")
    zg = _mm(h, p["wgT"], tb=True, name=f"ffn_up_g_{tag}")
    zv = _mm(h, p["wvT"], tb=True, name=f"ffn_up_v_{tag}")
    a = _ffn_act_fwd(zg, zv, p["cwg"], p["cwv"], name=f"ffn_act_{tag}")
    y = _mm(a, p["wd"], add=x, name=f"ffn_down_{tag}")
    return y, (x, h, zg, zv, a)


def _ffn_bwd(dy, dyb, p, saved, tag):
    x, h, zg, zv, a = saved
    g = {}
    da = _mm(dyb, p["wd"], tb=True, out_dtype=bf16, name=f"ffn_down_dx_{tag}")
    g["wd"] = _mm(a, dyb, ta=True, out_dtype=bf16, name=f"ffn_down_dw_{tag}")
    dzg, dzv, g["cwg"], g["cwv"] = _ffn_act_bwd(da, zg, zv, p["cwg"], p["cwv"], name=f"ffn_act_bwd_{tag}")
    g["wgT"] = _mm(dzg, h, ta=True, out_dtype=bf16, name=f"ffn_up_g_dw_{tag}")
    g["wvT"] = _mm(dzv, h, ta=True, out_dtype=bf16, name=f"ffn_up_v_dw_{tag}")
    dh = _mm(dzg, p["wgT"], name=f"ffn_up_g_dx_{tag}")
    dh = _mm(dzv, p["wvT"], add=dh, name=f"ffn_up_v_dx_{tag}")
    dx, dxb, g["norm"] = _rms_bwd(dh, x, p["norm"], res=dy, name=f"ffn_norm_bwd_{tag}")
    return dx, dxb, g


def _mla_fwd(x, p, tabs, tag):
    cos_t, sin_t = tabs
    h = _rms_fwd(x, p["norm"], name=f"attn_norm_{tag}")
    proj = _mm(h, p["w_inT"], tb=True, name=f"mix_in_{tag}")
    cqn = _rms_fwd(proj, p["cq_norm"], col=0, name=f"cq_norm_{tag}")
    ckvn = _rms_fwd(proj, p["ckv_norm"], col=1, name=f"ckv_norm_{tag}")
    q_raw = _mm(cqn, p["w_uqT"], tb=True, name=f"uq_{tag}")
    kv_raw = _mm(ckvn, p["w_ukvT"], tb=True, name=f"ukv_{tag}")
    q, k, v = _qk_prep_fwd(q_raw, kv_raw, proj, p["q_gain"], p["k_gain"], cos_t, sin_t, name=f"qk_prep_{tag}")
    o, lse = _flash_fwd(q, k, v, name=f"flash_fwd_{tag}")
    conv = _sconv_fwd(proj, p["sconv_w"], name=f"sconv_{tag}")
    mix = jnp.concatenate([o, conv], axis=1)
    y = _mm(mix, p["w_out"], add=x, name=f"mix_out_{tag}")
    return y, (x, h, proj, cqn, ckvn, q_raw, kv_raw, q, k, v, o, lse, mix)


def _mla_bwd(dy, dyb, p, tabs, saved, tag):
    cos_t, sin_t = tabs
    x, h, proj, cqn, ckvn, q_raw, kv_raw, q, k, v, o, lse, mix = saved
    s = x.shape[0]
    g = {}
    dmix = _mm(dyb, p["w_out"], tb=True, name=f"mix_out_dx_{tag}")
    g["w_out"] = _mm(mix, dyb, ta=True, out_dtype=bf16, name=f"mix_out_dw_{tag}")
    dgb, dgc, dci, g["sconv_w"] = _sconv_bwd(dmix, proj, p["sconv_w"], name=f"sconv_bwd_{tag}")
    dq, delta = _flash_bwd_dq(q, k, v, o, dmix, lse, name=f"flash_dq_{tag}")
    dk, dv = _flash_bwd_dkv(q, k, v, dmix, lse.reshape(HEADS, 1, s), delta.reshape(HEADS, 1, s), name=f"flash_dkv_{tag}")
    dq_raw, dkv_raw, dkr, g["q_gain"], g["k_gain"] = _qk_prep_bwd(
        dq, dk, dv, q_raw, kv_raw, proj, p["q_gain"], p["k_gain"], cos_t, sin_t, name=f"qk_prep_bwd_{tag}")
    dcqn = _mm(dq_raw, p["w_uqT"], name=f"uq_dx_{tag}")
    g["w_uqT"] = _mm(dq_raw, cqn, ta=True, out_dtype=bf16, name=f"uq_dw_{tag}")
    dckvn = _mm(dkv_raw, p["w_ukvT"], name=f"ukv_dx_{tag}")
    g["w_ukvT"] = _mm(dkv_raw, ckvn, ta=True, out_dtype=bf16, name=f"ukv_dw_{tag}")
    dcq, g["cq_norm"] = _rms_bwd(dcqn, proj, p["cq_norm"], col=0, out_dtype=bf16, name=f"cq_norm_bwd_{tag}")
    dckv, g["ckv_norm"] = _rms_bwd(dckvn, proj, p["ckv_norm"], col=1, out_dtype=bf16, name=f"ckv_norm_bwd_{tag}")
    dproj = jnp.concatenate([dcq, dckv, dgb, dgc, dci, dkr.astype(bf16)], axis=1)
    dh = _mm(dproj, p["w_inT"], name=f"mix_in_dx_{tag}")
    g["w_inT"] = _mm(dproj, h, ta=True, out_dtype=bf16, name=f"mix_in_dw_{tag}")
    dx, dxb, g["norm"] = _rms_bwd(dh, x, p["norm"], res=dy, name=f"attn_norm_bwd_{tag}")
    return dx, dxb, g


def _block_diag(wg):
    nb, ng, r, c = wg.shape
    eye = jnp.eye(ng, dtype=wg.dtype)
    return (wg[:, :, :, None, :] * eye[None, :, None, :, None]).reshape(nb, ng * r, ng * c)


def _s5_mats(bbr, bbi, c_re, c_im):
    nb = GROUPS // 8
    b4 = jnp.stack([bbr.reshape(GROUPS, STATE, GROUP), bbi.reshape(GROUPS, STATE, GROUP)], axis=1)
    wg = jnp.transpose(b4, (0, 3, 1, 2)).reshape(nb, 8, GROUP, 2 * STATE)
    cg = jnp.stack([c_re, -c_im], axis=1)
    cg = jnp.transpose(cg, (0, 1, 3, 2)).reshape(nb, 8, 2 * STATE, GROUP)
    return _state_layout(_block_diag(wg), 2), _state_layout(_block_diag(cg), 1)


def _state_layout(m, axis):
    shp = m.shape
    m = m.reshape(shp[:axis] + (4, 2, 2, STATE) + shp[axis + 1:])
    return jnp.swapaxes(m, axis + 1, axis + 2).reshape(shp)


def _group_blocks(d):
    d = d.reshape(GROUPS // 2, 2, GROUP, 2, 2, STATE)
    return jnp.stack([d[:, 0, :, :, 0, :], d[:, 1, :, :, 1, :]], axis=1).reshape(GROUPS, GROUP, 2, STATE)


def _s5_fwd(x, p, tag):
    h = _rms_fwd(x, p["norm"], name=f"ssm_norm_{tag}")
    u, ub = _mm(h, p["w_in"], twin=True, name=f"ssm_in_{tag}")
    ar, ai, bbr, bbi = _disc_fwd(p["lr"], p["li"], p["ls"], p["br"], p["bi"], name=f"disc_{tag}")
    wb, cb = _s5_mats(bbr, bbi, p["c_re"], p["c_im"])
    a1, a2 = ar.reshape(GROUPS // 2, 1, LANES), ai.reshape(GROUPS // 2, 1, LANES)
    xs = _scan_fwd(ub, wb.astype(bf16), a1, a2, name=f"ssm_scan_{tag}")
    y = _bd_nn(xs, cb.astype(bf16), name=f"ssm_y_{tag}")
    g = _gelu_fwd(y, u, p["d_skip"], name=f"ssm_gelu_{tag}")
    a = _mm(g, p["wgaT"], tb=True, name=f"glu_a_{tag}")
    b = _mm(g, p["wgbT"], tb=True, name=f"glu_b_{tag}")
    out = _glu_fwd(x, a, b, name=f"glu_{tag}")
    return out, (x, h, u, ub, wb, cb, a1, a2, xs, y, g, a, b)


def _s5_bwd(dout, p, saved, tag):
    x, h, u, ub, wb, cb, a1, a2, xs, y, g, a, b = saved
    gr = {}
    da, db = _glu_bwd(dout, a, b, name=f"glu_bwd_{tag}")
    dg = _mm(da, p["wgaT"], name=f"glu_a_dx_{tag}")
    dg = _mm(db, p["wgbT"], add=dg, name=f"glu_b_dx_{tag}")
    gr["wgaT"] = _mm(da, g, ta=True, out_dtype=bf16, name=f"glu_a_dw_{tag}")
    gr["wgbT"] = _mm(db, g, ta=True, out_dtype=bf16, name=f"glu_b_dw_{tag}")
    dy, du1, gr["d_skip"] = _gelu_bwd(dg, y, u, p["d_skip"], name=f"ssm_gelu_bwd_{tag}")
    dct = _group_blocks(_bd_tn_diag(dy, xs, name=f"ssm_y_dw_{tag}"))
    gs, dar, dai = _scan_bwd(dy, jnp.swapaxes(cb, 1, 2).astype(bf16), xs, a1, a2, name=f"ssm_scan_bwd_{tag}")
    du2 = _bd_nn(gs, jnp.swapaxes(wb, 1, 2).astype(bf16), name=f"ssm_bu_dx_{tag}")
    dwg = _group_blocks(_bd_tn_diag(ub, gs, name=f"ssm_bu_dw_{tag}"))
    du = _add(du1, du2, name=f"ssm_du_{tag}")
    dh = _mm(du, p["w_in"], tb=True, name=f"ssm_in_dx_{tag}")
    gr["w_in"] = _mm(h, du, ta=True, out_dtype=bf16, name=f"ssm_in_dw_{tag}")
    dx, dxb, gr["norm"] = _rms_bwd(dh, x, p["norm"], res=dout, name=f"ssm_norm_bwd_{tag}")
    dbb = jnp.transpose(dwg, (2, 0, 3, 1)).reshape(2, GROUPS, STATE * GROUP)
    gr["c_re"] = dct[:, :, 0, :]
    gr["c_im"] = -dct[:, :, 1, :]
    dlr, dli, dls, dbr, dbi = _disc_bwd(p["lr"], p["li"], p["ls"], p["br"], p["bi"], dar.reshape(GROUPS, STATE),
                                        dai.reshape(GROUPS, STATE), dbb[0], dbb[1], name=f"disc_bwd_{tag}")
    gr["lr"], gr["li"], gr["ls"] = dlr, dli, dls.reshape(GROUPS)
    gr["br"], gr["bi"] = dbr.reshape(GROUPS, STATE, GROUP), dbi.reshape(GROUPS, STATE, GROUP)
    return dx, dxb, gr


def _slab_shape(shard, axis):
    return (shard[0], shard[2], shard[1]) if axis == 2 else shard


def _to_slab(w, axis):
    return jnp.swapaxes(w, 1, 2) if axis == 2 else w


def _mix_in_pad(wt):
    z = lambda n: jnp.zeros((n, wt.shape[1]), wt.dtype)
    return jnp.concatenate([wt[:512], wt[544:2080], z(NOPE), wt[512:544], z(HEAD_PAD - QK)], axis=0)


def _mix_in_unpad(g):
    return jnp.concatenate([g[:512], g[2048 + NOPE:2048 + QK], g[512:2048]], axis=0)


def _mix_out_pad(w):
    att = jnp.pad(w[:512].reshape(HEADS, NOPE, D), ((0, 0), (NOPE, 0), (0, 0))).reshape(HEADS * HEAD_PAD, D)
    return jnp.concatenate([att, w[512:]], axis=0)


def _mix_out_unpad(g):
    att = g[:HEADS * HEAD_PAD].reshape(HEADS, HEAD_PAD, D)[:, NOPE:, :].reshape(HEADS * NOPE, D)
    return jnp.concatenate([att, g[HEADS * HEAD_PAD:]], axis=0)


def _layer_params(wl, ws, layer):
    i = layer // 2
    half = N_DEV // 2
    up = wl["ffn_w_up"][:, layer]
    ffn = dict(norm=_row(ws["ffn_norm"][layer]), wgT=up[:half].reshape(FFN_H, D), wvT=up[half:].reshape(FFN_H, D),
               cwg=ws["ffn_conv_w"][layer][:, :FFN_H], cwv=ws["ffn_conv_w"][layer][:, FFN_H:],
               wd=wl["ffn_w_down"][:, layer].reshape(FFN_H, D))
    if layer % 2 == 0:
        uq = jnp.pad(wl["w_uq"][:, i], ((0, 0), (0, HEAD_PAD - QK), (0, 0)))
        mixer = dict(norm=_row(ws["attn_norm"][i]), w_inT=_mix_in_pad(wl["mix_w_in"][:, i].reshape(-1, D)),
                     cq_norm=_row(ws["cq_norm"][i]), ckv_norm=_row(ws["ckv_norm"][i]),
                     w_uqT=uq.reshape(HEADS * HEAD_PAD, LORA), w_ukvT=wl["w_ukv"][:, i].reshape(HEADS * HEAD_PAD, LORA),
                     q_gain=_pad_gain(ws["q_gain"][i]), k_gain=_pad_gain(ws["k_gain"][i]), sconv_w=ws["sconv_w"][i],
                     w_out=_mix_out_pad(wl["mix_w_out"][:, i].reshape(D, D)))
    else:
        glu = wl["w_glu"][:, i]
        mixer = dict(norm=_row(ws["ssm_norm"][i]), w_in=wl["ssm_w_in"][:, i].reshape(D, D), lr=ws["lambda_re"][i],
                     li=ws["lambda_im"][i], ls=ws["log_step"][i].reshape(GROUPS, 1),
                     br=ws["b_re"][i].reshape(GROUPS, STATE * GROUP), bi=ws["b_im"][i].reshape(GROUPS, STATE * GROUP),
                     c_re=ws["c_re"][i], c_im=ws["c_im"][i], d_skip=_row(ws["d_skip"][i]),
                     wgaT=glu[:half].reshape(D, D), wgbT=glu[half:].reshape(D, D))
    return mixer, ffn


def _collect_grads(gm, gf):
    ev, od, half = (0, 2), (1, 3), N_DEV // 2
    st = lambda xs: jnp.stack(xs, axis=0)
    per_dev = lambda xs: jnp.stack(xs, axis=1)
    halves = lambda a, b, rows: jnp.concatenate([a.reshape(half, rows, D), b.reshape(half, rows, D)], axis=0)
    big = {
        "ffn_w_up": per_dev([halves(gf[l]["wgT"], gf[l]["wvT"], FFN_H // half) for l in range(4)]),
        "ffn_w_down": per_dev([gf[l]["wd"].reshape(N_DEV, -1, D) for l in range(4)]),
        "w_glu": per_dev([halves(gm[l]["wgaT"], gm[l]["wgbT"], D // half) for l in od]),
        "mix_w_out": per_dev([_mix_out_unpad(gm[l]["w_out"]).reshape(N_DEV, -1, D) for l in ev]),
        "ssm_w_in": per_dev([gm[l]["w_in"].reshape(N_DEV, -1, D) for l in od]),
        "w_ukv": per_dev([gm[l]["w_ukvT"].reshape(N_DEV, HEAD_PAD, LORA) for l in ev]),
        "w_uq": per_dev([gm[l]["w_uqT"].reshape(N_DEV, HEAD_PAD, LORA)[:, :QK] for l in ev]),
        "mix_w_in": per_dev([_mix_in_unpad(gm[l]["w_inT"]).reshape(N_DEV, -1, D) for l in ev]),
    }
    small = {
        "attn_norm": st([gm[l]["norm"].reshape(D) for l in ev]),
        "cq_norm": st([gm[l]["cq_norm"].reshape(LORA) for l in ev]),
        "ckv_norm": st([gm[l]["ckv_norm"].reshape(LORA) for l in ev]),
        "q_gain": st([gm[l]["q_gain"].reshape(HEAD_PAD)[:QK] for l in ev]),
        "k_gain": st([gm[l]["k_gain"].reshape(HEAD_PAD)[:QK] for l in ev]),
        "sconv_w": st([gm[l]["sconv_w"] for l in ev]),
        "ssm_norm": st([gm[l]["norm"].reshape(D) for l in od]),
        "lambda_re": st([gm[l]["lr"] for l in od]), "lambda_im": st([gm[l]["li"] for l in od]),
        "log_step": st([gm[l]["ls"] for l in od]),
        "b_re": st([gm[l]["br"] for l in od]), "b_im": st([gm[l]["bi"] for l in od]),
        "c_re": st([gm[l]["c_re"] for l in od]), "c_im": st([gm[l]["c_im"] for l in od]),
        "d_skip": st([gm[l]["d_skip"].reshape(D) for l in od]),
        "ffn_norm": st([gf[l]["norm"].reshape(D) for l in range(4)]),
        "ffn_conv_w": st([jnp.concatenate([gf[l]["cwg"], gf[l]["cwv"]], axis=1) for l in range(4)]),
    }
    return big, small


def _local_step(x, target, wl, ws):
    s = x.shape[0]
    tabs = _rope_tables(s)
    saved, params = [], []
    for layer in range(4):
        mixer, ffn = _layer_params(wl, ws, layer)
        params.append((mixer, ffn))
        if layer % 2 == 0:
            x, sm = _mla_fwd(x, mixer, tabs, f"l{layer}")
        else:
            x, sm = _s5_fwd(x, mixer, f"l{layer}")
        x, sf = _ffn_fwd(x, ffn, f"l{layer}")
        saved.append((sm, sf))
    dx, dxb, loss = _loss_head(x, target, name="loss_head")
    gm, gf = [None] * 4, [None] * 4
    for layer in reversed(range(4)):
        mixer, ffn = params[layer]
        sm, sf = saved[layer]
        dx, dxb, gf[layer] = _ffn_bwd(dx, dxb, ffn, sf, f"l{layer}")
        if layer % 2 == 0:
            dx, dxb, gm[layer] = _mla_bwd(dx, dxb, mixer, tabs, sm, f"l{layer}")
        else:
            dx, dxb, gm[layer] = _s5_bwd(dx, mixer, sm, f"l{layer}")
    return loss, dx, _collect_grads(gm, gf)


def kernel(x, attn_norm, mix_w_in, cq_norm, ckv_norm, w_uq, w_ukv, q_gain, k_gain, sconv_w, mix_w_out, ssm_norm, ssm_w_in, lambda_re, lambda_im, log_step, b_re, b_im, c_re, c_im, d_skip, w_glu, ffn_norm, ffn_w_up, ffn_conv_w, ffn_w_down, loss_target, m_attn_norm, m_mix_w_in, m_cq_norm, m_ckv_norm, m_w_uq, m_w_ukv, m_q_gain, m_k_gain, m_sconv_w, m_mix_w_out, m_ssm_norm, m_ssm_w_in, m_lambda_re, m_lambda_im, m_log_step, m_b_re, m_b_im, m_c_re, m_c_im, m_d_skip, m_w_glu, m_ffn_norm, m_ffn_w_up, m_ffn_conv_w, m_ffn_w_down, v_attn_norm, v_mix_w_in, v_cq_norm, v_ckv_norm, v_w_uq, v_w_ukv, v_q_gain, v_k_gain, v_sconv_w, v_mix_w_out, v_ssm_norm, v_ssm_w_in, v_lambda_re, v_lambda_im, v_log_step, v_b_re, v_b_im, v_c_re, v_c_im, v_d_skip, v_w_glu, v_ffn_norm, v_ffn_w_up, v_ffn_conv_w, v_ffn_w_down):
    args = dict(locals())
    wsh = {n: args[n] for n in WEIGHTS}
    msh = {n: args["m_" + n] for n in WEIGHTS}
    vsh = {n: args["v_" + n] for n in WEIGHTS}
    me = 4 * lax.axis_index("x") + 2 * lax.axis_index("y") + lax.axis_index("c")
    big_names = [n for n, _, _ in BIG]
    slab_shapes = [_slab_shape(sh, ax) for _, sh, ax in BIG]
    small_names = [n for n, _ in REPL] + [n for n, _, _ in SMALL]

    mine = _pack_slabs([_to_slab(wsh[n], ax).astype(bf16) for n, _, ax in BIG], BIG_ROWS)
    wl = dict(zip(big_names, _unpack_slabs(_gather8(mine, name="gather_weights"), slab_shapes)))
    placed = []
    for n, shard, axis in SMALL:
        start = [0] * len(shard)
        start[axis] = me * shard[axis]
        placed.append(lax.dynamic_update_slice(jnp.zeros(_full_shape(shard, axis), f32), wsh[n], start))
    small_all = _all_sum(_pack_rows(placed, SMALL_FWD_ROWS), name="small_params")
    ws = dict(zip([n for n, _, _ in SMALL], _unpack_rows(small_all, [_full_shape(sh, ax) for _, sh, ax in SMALL])))
    ws.update({n: wsh[n] for n, _ in REPL})

    loss8, grad_x, (big_grads, grads) = _local_step(x[0], loss_target[0], wl, ws)

    contrib = _pack_slabs([big_grads[n] for n in big_names], BIG_ROWS, axis=1)
    chip_sum = _pair_sum(contrib, _pair_exchange(contrib, name="grads_pair_exchange"), name="grads_pair_sum")
    g_big = _sum_slabs(_cross_exchange(chip_sum, name="grads_cross_exchange"), name="grads_chip_sum")
    small_vec = _pack_rows([grads[n] for n, _ in REPL] + [grads[n] for n, _, _ in SMALL] + [loss8[0, :1]], SMALL_ROWS)
    small_sum = _all_sum(small_vec, name="small_grads")
    parts = _unpack_rows(small_sum, [sh for _, sh in REPL] + [_full_shape(sh, ax) for _, sh, ax in SMALL] + [(1,)])
    g = {n: val for (n, _), val in zip(REPL, parts)}
    for (n, shard, axis), val in zip(SMALL, parts[len(REPL):]):
        start = [0] * len(shard)
        start[axis] = me * shard[axis]
        g[n] = lax.dynamic_slice(val, start, shard)
    loss = parts[-1].reshape(())
    for (n, _, axis), val in zip(BIG, _unpack_slabs(g_big, slab_shapes)):
        g[n] = _to_slab(val, axis)

    delta, new_m, new_v = {}, {}, {}
    for n, shard, _ in BIG:
        flat = lambda a: a.reshape(-1, shard[-1])
        outs = _adamw(flat(wsh[n]), flat(g[n]), flat(msh[n]), flat(vsh[n]), name=f"adamw_{n}")
        delta[n], new_m[n], new_v[n] = [o.reshape(shard) for o in outs]
    small_state = [_pack_rows([src[n] for n in small_names], SMALL_ROWS) for src in (wsh, g, msh, vsh)]
    for dst, slab in zip((delta, new_m, new_v), _adamw(*small_state, name="adamw_small")):
        dst.update(zip(small_names, _unpack_rows(slab, [wsh[n].shape for n in small_names])))

    return (loss, grad_x[None], *[g[n] for n in WEIGHTS], *[delta[n] for n in WEIGHTS],
            *[new_m[n] for n in WEIGHTS], *[new_v[n] for n in WEIGHTS])
```

```python
import math

import numpy as np
import jax
import jax.numpy as jnp
from jax import lax
from jax.experimental import pallas as pl
from jax.experimental.pallas import tpu as pltpu

f32, bf16 = jnp.float32, jnp.bfloat16

N_DEV = 8
D = 1024
HEADS = 8
NOPE, ROPE, QK = 64, 32, 96
HEAD_PAD = 128
LORA = 256
CONV_CH = 512
MIX_IN_PAD = 2176
FFN_H = 2816
GROUPS, GROUP, STATE = 64, 16, 64
EPS = 1e-6
ROPE_THETA = 10000.0
ADAM_LR, ADAM_B1, ADAM_B2, ADAM_EPS, ADAM_WD, ADAM_STEP = 0.001, 0.9, 0.999, 1e-08, 0.01, 10
LANES = 128
PAIR_LANES = 2 * LANES
VMEM_LIMIT = 56 << 20
MM_VMEM_BUDGET = 40 << 20
NEG = -1e30

BIG = (
    ("ffn_w_up", (4, 1024, 704), 2), ("ffn_w_down", (4, 352, 1024), 1), ("w_glu", (2, 1024, 256), 2),
    ("mix_w_out", (2, 128, 1024), 1), ("ssm_w_in", (2, 128, 1024), 1), ("w_ukv", (2, 256, 128), 2),
    ("w_uq", (2, 256, 96), 2), ("mix_w_in", (2, 1024, 260), 2))
REPL = (("attn_norm", (2, 1024)), ("cq_norm", (2, 256)), ("ckv_norm", (2, 256)), ("q_gain", (2, 96)),
        ("k_gain", (2, 96)), ("lambda_re", (2, 64, 64)), ("lambda_im", (2, 64, 64)), ("log_step", (2, 64)),
        ("b_re", (2, 64, 64, 16)), ("b_im", (2, 64, 64, 16)), ("c_re", (2, 64, 16, 64)), ("c_im", (2, 64, 16, 64)),
        ("ffn_norm", (4, 1024)))
SMALL = (("sconv_w", (2, 3, 64), 2), ("ssm_norm", (2, 128), 1), ("d_skip", (2, 128), 1), ("ffn_conv_w", (4, 3, 704), 2))
WEIGHTS = ['attn_norm', 'mix_w_in', 'cq_norm', 'ckv_norm', 'w_uq', 'w_ukv', 'q_gain', 'k_gain', 'sconv_w', 'mix_w_out',
           'ssm_norm', 'ssm_w_in', 'lambda_re', 'lambda_im', 'log_step', 'b_re', 'b_im', 'c_re', 'c_im', 'd_skip',
           'w_glu', 'ffn_norm', 'ffn_w_up', 'ffn_conv_w', 'ffn_w_down']
BIG_ROWS = 5888
SMALL_FWD_ROWS = 80
SMALL_ROWS = 640


def _cparams(sem=None, **kw):
    return pltpu.CompilerParams(dimension_semantics=sem, vmem_limit_bytes=VMEM_LIMIT, **kw)


def _tile(n, target):
    best = 0
    for t in range(LANES, min(n, target) + 1, LANES):
        if n % t == 0:
            best = t
    return best if best else n


def _mm(a, b, *, ta=False, tb=False, out_dtype=f32, add=None, twin=False, name, tm=1024, tn=1536):
    m, k = (a.shape[1], a.shape[0]) if ta else a.shape
    n = b.shape[0] if tb else b.shape[1]
    assert (b.shape[1] if tb else b.shape[0]) == k
    tm = _tile(m, tm)
    tn_ = _tile(n, tn)
    tn = n if (tn_ < 256 and n <= 2304) else tn_

    def vmem_bytes(t):
        io = 2 * (tm * t * a.dtype.itemsize + t * tn * b.dtype.itemsize + tm * tn * (jnp.dtype(out_dtype).itemsize + 2 * twin))
        return io + (2 * tm * tn * 4 if add is not None else 0) + (tm * tn * 4 if t < k else 0)

    tk = next((t for t in [k] + [t for t in range(k - LANES, 0, -LANES) if k % t == 0] if vmem_bytes(t) <= MM_VMEM_BUDGET), LANES)
    nk = k // tk
    dn = (((0 if ta else 1,), (1 if tb else 0,)), ((), ()))

    def body(*refs):
        a_ref, b_ref = refs[:2]
        add_ref = refs[2] if add is not None else None
        o_ref = refs[3] if add is not None else refs[2]
        twin_ref = refs[4 if add is not None else 3] if twin else None
        part = lax.dot_general(a_ref[...].astype(bf16), b_ref[...].astype(bf16), dn, preferred_element_type=f32)

        def finish(r):
            if add is not None:
                r = r + add_ref[...].astype(f32)
            o_ref[...] = r.astype(out_dtype)
            if twin:
                twin_ref[...] = r.astype(bf16)

        if nk == 1:
            finish(part)
            return
        acc = refs[-1]
        kk = pl.program_id(2)

        @pl.when(kk == 0)
        def _():
            acc[...] = part

        @pl.when(kk > 0)
        def _():
            acc[...] += part

        @pl.when(kk == nk - 1)
        def _():
            finish(acc[...])

    a_spec = pl.BlockSpec((tk, tm), lambda i, j, kk: (kk, i)) if ta else pl.BlockSpec((tm, tk), lambda i, j, kk: (i, kk))
    b_spec = pl.BlockSpec((tn, tk), lambda i, j, kk: (j, kk)) if tb else pl.BlockSpec((tk, tn), lambda i, j, kk: (kk, j))
    in_specs, args = [a_spec, b_spec], [a, b]
    if add is not None:
        in_specs.append(pl.BlockSpec((tm, tn), lambda i, j, kk: (i, j)))
        args.append(add)
    o_spec, o_shape = pl.BlockSpec((tm, tn), lambda i, j, kk: (i, j)), jax.ShapeDtypeStruct((m, n), out_dtype)
    return pl.pallas_call(
        body, name=name, grid=(m // tm, n // tn, nk), in_specs=in_specs,
        out_specs=[o_spec, o_spec] if twin else o_spec,
        out_shape=[o_shape, jax.ShapeDtypeStruct((m, n), bf16)] if twin else o_shape,
        scratch_shapes=[pltpu.VMEM((tm, tn), f32)] if nk > 1 else [],
        compiler_params=_cparams(("parallel", "parallel", "arbitrary")))(*args)


def _bd_nn(a, w, *, out_dtype=f32, name, ts=512):
    s = a.shape[0]
    nb, ka, no = w.shape
    ts = min(ts, s)

    def body(a_ref, w_ref, o_ref):
        o_ref[...] = jnp.dot(a_ref[...].astype(bf16), w_ref[0].astype(bf16), preferred_element_type=f32).astype(out_dtype)

    return pl.pallas_call(
        body, name=name, grid=(nb, s // ts),
        in_specs=[pl.BlockSpec((ts, ka), lambda b, i: (i, b)), pl.BlockSpec((1, ka, no), lambda b, i: (b, 0, 0))],
        out_specs=pl.BlockSpec((ts, no), lambda b, i: (i, b)),
        out_shape=jax.ShapeDtypeStruct((s, nb * no), out_dtype),
        compiler_params=_cparams(("parallel", "parallel")))(a, w)


def _bd_tn_diag(a, g, *, name, ts=512):
    s = a.shape[0]
    nb = a.shape[1] // LANES
    ts = min(ts, s)
    ni = s // ts

    def body(a_ref, g_ref, o_ref, acc):
        i = pl.program_id(1)
        part = lax.dot_general(a_ref[...].astype(bf16), g_ref[...].astype(bf16), (((0,), (0,)), ((), ())),
                               preferred_element_type=f32)

        @pl.when(i == 0)
        def _():
            acc[...] = part

        @pl.when(i > 0)
        def _():
            acc[...] += part

        @pl.when(i == ni - 1)
        def _():
            for j in range(8):
                o_ref[0, j] = acc[j * GROUP:(j + 1) * GROUP, (j // 2) * PAIR_LANES:(j // 2 + 1) * PAIR_LANES]

    return pl.pallas_call(
        body, name=name, grid=(nb, ni),
        in_specs=[pl.BlockSpec((ts, LANES), lambda b, i: (i, b)), pl.BlockSpec((ts, 8 * LANES), lambda b, i: (i, b))],
        out_specs=pl.BlockSpec((1, 8, GROUP, PAIR_LANES), lambda b, i: (b, 0, 0, 0)),
        out_shape=jax.ShapeDtypeStruct((nb, 8, GROUP, PAIR_LANES), f32),
        scratch_shapes=[pltpu.VMEM((LANES, 8 * LANES), f32)],
        compiler_params=_cparams(("parallel", "arbitrary")))(a, g)


def _rms_fwd(x, g, *, col=0, name, ts=512):
    s, d = x.shape[0], g.shape[1]
    ts = min(ts, s)

    def body(x_ref, g_ref, o_ref):
        xv = x_ref[...].astype(f32)
        r = lax.rsqrt(jnp.mean(xv * xv, axis=-1, keepdims=True) + EPS)
        o_ref[...] = (xv * r * g_ref[...]).astype(bf16)

    return pl.pallas_call(
        body, name=name, grid=(s // ts,),
        in_specs=[pl.BlockSpec((ts, d), lambda i: (i, col)), pl.BlockSpec((1, d), lambda i: (0, 0))],
        out_specs=pl.BlockSpec((ts, d), lambda i: (i, 0)),
        out_shape=jax.ShapeDtypeStruct((s, d), bf16),
        compiler_params=_cparams(("parallel",)))(x, g)


def _rms_bwd(dy, x, g, *, col=0, res=None, out_dtype=f32, name, ts=512):
    s, d = dy.shape
    ts = min(ts, s)
    twin = res is not None

    def body(*refs):
        if twin:
            dy_ref, x_ref, g_ref, res_ref, dx_ref, dxb_ref, dg_ref = refs
        else:
            dy_ref, x_ref, g_ref, dx_ref, dg_ref = refs

        @pl.when(pl.program_id(0) == 0)
        def _():
            dg_ref[...] = jnp.zeros_like(dg_ref)

        xv, dyv = x_ref[...].astype(f32), dy_ref[...].astype(f32)
        r = lax.rsqrt(jnp.mean(xv * xv, axis=-1, keepdims=True) + EPS)
        dyg = dyv * g_ref[...]
        dx = r * dyg - xv * (r * r * r) * jnp.mean(xv * dyg, axis=-1, keepdims=True)
        if twin:
            dx = dx + res_ref[...]
            dxb_ref[...] = dx.astype(bf16)
        dx_ref[...] = dx.astype(out_dtype)
        dg_ref[...] += jnp.sum(dyv * xv * r, axis=0, keepdims=True)

    row, vec = pl.BlockSpec((ts, d), lambda i: (i, 0)), pl.BlockSpec((1, d), lambda i: (0, 0))
    in_specs, args = [row, pl.BlockSpec((ts, d), lambda i: (i, col)), vec], [dy, x, g]
    out_specs, out_shape = [row], [jax.ShapeDtypeStruct((s, d), out_dtype)]
    if twin:
        in_specs.append(row)
        args.append(res)
        out_specs.append(row)
        out_shape.append(jax.ShapeDtypeStruct((s, d), bf16))
    return pl.pallas_call(
        body, name=name, grid=(s // ts,), in_specs=in_specs, out_specs=out_specs + [vec],
        out_shape=out_shape + [jax.ShapeDtypeStruct((1, d), f32)],
        compiler_params=_cparams(("arbitrary",)))(*args)


def _swap_halves(z):
    lane = lax.broadcasted_iota(jnp.int32, z.shape, 1)
    return jnp.where(lane < NOPE + ROPE // 2, pltpu.roll(z, LANES - ROPE // 2, axis=1), pltpu.roll(z, ROPE // 2, axis=1))


def _rope_tables(s):
    inv_freq = 1.0 / (ROPE_THETA ** (jnp.arange(0, ROPE, 2, dtype=f32) / ROPE))
    ang = jnp.arange(s, dtype=f32)[:, None] * inv_freq[None, :]
    cos, sin = jnp.cos(ang), jnp.sin(ang)
    one, zero = jnp.ones((s, NOPE), f32), jnp.zeros((s, NOPE), f32)
    pad1, pad0 = jnp.ones((s, HEAD_PAD - QK), f32), jnp.zeros((s, HEAD_PAD - QK), f32)
    return jnp.concatenate([one, cos, cos, pad1], 1), jnp.concatenate([zero, -sin, sin, pad0], 1)


def _qk_prep_fwd(q_raw, kv_raw, proj, qg, kg, cos_t, sin_t, *, name, ts=512):
    s = q_raw.shape[0]
    ts = min(ts, s)
    rope_blk = (MIX_IN_PAD - HEAD_PAD) // HEAD_PAD

    def body(q_ref, kv_ref, kr_ref, qg_ref, kg_ref, c_ref, s_ref, qo_ref, ko_ref, vo_ref):
        lane = lax.broadcasted_iota(jnp.int32, (ts, HEAD_PAD), 1)
        cosv, sinv = c_ref[...], s_ref[...]

        def norm_rope(z, gain):
            r = lax.rsqrt(jnp.sum(z * z, axis=-1, keepdims=True) * (1.0 / QK) + EPS)
            zn = z * r * gain
            return zn * cosv + _swap_halves(zn) * sinv

        kvv = kv_ref[...]
        qo_ref[...] = (norm_rope(q_ref[...], qg_ref[...]) * _Q_FOLD).astype(bf16)
        ko_ref[...] = norm_rope(jnp.where(lane < NOPE, kvv, kr_ref[...]), kg_ref[...]).astype(bf16)
        vo_ref[...] = jnp.where(lane >= NOPE, kvv, 0.0).astype(bf16)

    head = pl.BlockSpec((ts, HEAD_PAD), lambda i, h: (i, h))
    row = pl.BlockSpec((ts, HEAD_PAD), lambda i, h: (i, 0))
    vec = pl.BlockSpec((1, HEAD_PAD), lambda i, h: (0, 0))
    out = jax.ShapeDtypeStruct((s, HEADS * HEAD_PAD), bf16)
    return pl.pallas_call(
        body, name=name, grid=(s // ts, HEADS),
        in_specs=[head, head, pl.BlockSpec((ts, HEAD_PAD), lambda i, h: (i, rope_blk)), vec, vec, row, row],
        out_specs=[head, head, head], out_shape=[out, out, out],
        compiler_params=_cparams(("parallel", "parallel")))(q_raw, kv_raw, proj, qg, kg, cos_t, sin_t)


def _qk_prep_bwd(dq, dk, dv, q_raw, kv_raw, proj, qg, kg, cos_t, sin_t, *, name, ts=512):
    s = q_raw.shape[0]
    ts = min(ts, s)
    rope_blk = (MIX_IN_PAD - HEAD_PAD) // HEAD_PAD

    def body(dq_ref, dk_ref, dv_ref, q_ref, kv_ref, kr_ref, qg_ref, kg_ref, c_ref, s_ref,
             dqr_ref, dkvr_ref, dkr_ref, dqg_ref, dkg_ref):
        i, h = pl.program_id(0), pl.program_id(1)
        lane = lax.broadcasted_iota(jnp.int32, (ts, HEAD_PAD), 1)
        is_rope = (lane >= NOPE) & (lane < QK)
        cosv, sinv = c_ref[...], s_ref[...]

        @pl.when((i == 0) & (h == 0))
        def _():
            dqg_ref[...] = jnp.zeros_like(dqg_ref)
            dkg_ref[...] = jnp.zeros_like(dkg_ref)

        @pl.when(h == 0)
        def _():
            dkr_ref[...] = jnp.zeros_like(dkr_ref)

        def back(dout, z, gain):
            dzn = dout * cosv + jnp.where(is_rope, _swap_halves(dout * sinv), 0.0)
            r = lax.rsqrt(jnp.sum(z * z, axis=-1, keepdims=True) * (1.0 / QK) + EPS)
            dzg = dzn * gain
            dz = r * dzg - z * (r * r * r) * (jnp.sum(z * dzg, axis=-1, keepdims=True) * (1.0 / QK))
            return dz, jnp.sum(dzn * z * r, axis=0, keepdims=True)

        dqz, dqg = back(dq_ref[...].astype(f32), q_ref[...], qg_ref[...])
        dqr_ref[...] = dqz.astype(bf16)
        dqg_ref[...] += dqg
        kvv = kv_ref[...]
        dkz, dkg = back(dk_ref[...].astype(f32), jnp.where(lane < NOPE, kvv, kr_ref[...]), kg_ref[...])
        dkg_ref[...] += dkg
        dkvr_ref[...] = jnp.where(lane < NOPE, dkz, dv_ref[...].astype(f32)).astype(bf16)
        dkr_ref[...] += jnp.where(is_rope, dkz, 0.0)

    head = pl.BlockSpec((ts, HEAD_PAD), lambda i, h: (i, h))
    row = pl.BlockSpec((ts, HEAD_PAD), lambda i, h: (i, 0))
    vec = pl.BlockSpec((1, HEAD_PAD), lambda i, h: (0, 0))
    wide = jax.ShapeDtypeStruct((s, HEADS * HEAD_PAD), bf16)
    return pl.pallas_call(
        body, name=name, grid=(s // ts, HEADS),
        in_specs=[head, head, head, head, head, pl.BlockSpec((ts, HEAD_PAD), lambda i, h: (i, rope_blk)), vec, vec, row, row],
        out_specs=[head, head, row, vec, vec],
        out_shape=[wide, wide, jax.ShapeDtypeStruct((s, HEAD_PAD), f32), jax.ShapeDtypeStruct((1, HEAD_PAD), f32),
                   jax.ShapeDtypeStruct((1, HEAD_PAD), f32)],
        compiler_params=_cparams(("arbitrary", "arbitrary")))(dq, dk, dv, q_raw, kv_raw, proj, qg, kg, cos_t, sin_t)


_NT = (((1,), (1,)), ((), ()))
_SCALE = QK ** -0.5
_LOG2E = math.log2(math.e)
_Q_FOLD = _SCALE * _LOG2E
FLASH_TILE = 1024


def _flash_fwd(q, k, v, *, name, tq=FLASH_TILE):
    s = q.shape[0]
    tq = min(tq, s)

    def body(q_ref, k_ref, v_ref, o_ref, lse_ref):
        i = pl.program_id(1)
        qv = q_ref[...]

        def step(j, carry, masked):
            m, l, acc = carry
            st = pl.multiple_of(j * tq, tq)
            kj, vj = k_ref[pl.ds(st, tq), :], v_ref[pl.ds(st, tq), :]
            sc = lax.dot_general(qv, kj, _NT, preferred_element_type=f32)
            if masked:
                rr = lax.broadcasted_iota(jnp.int32, (tq, tq), 0)
                cc = lax.broadcasted_iota(jnp.int32, (tq, tq), 1)
                sc = jnp.where(cc <= rr, sc, NEG)
            m_new = jnp.maximum(m, jnp.max(sc, axis=-1, keepdims=True))
            p = jnp.exp2(sc - m_new)
            alpha = jnp.exp2(m - m_new)
            l = alpha * l + jnp.sum(p, axis=-1, keepdims=True)
            acc = alpha * acc + jnp.dot(p.astype(bf16), vj, preferred_element_type=f32)
            return m_new, l, acc

        init = (jnp.full((tq, 1), NEG, f32), jnp.zeros((tq, 1), f32), jnp.zeros((tq, HEAD_PAD), f32))
        carry = lax.fori_loop(0, i, lambda j, c: step(j, c, False), init)
        m, l, acc = step(i, carry, True)
        o_ref[...] = (acc / l).astype(bf16)
        lse_ref[0] = m + jnp.log2(l)

    blk = pl.BlockSpec((tq, HEAD_PAD), lambda h, i: (i, h))
    full = pl.BlockSpec((s, HEAD_PAD), lambda h, i: (0, h))
    return pl.pallas_call(
        body, name=name, grid=(HEADS, s // tq), in_specs=[blk, full, full],
        out_specs=[blk, pl.BlockSpec((1, tq, 1), lambda h, i: (h, i, 0))],
        out_shape=[jax.ShapeDtypeStruct((s, HEADS * HEAD_PAD), bf16), jax.ShapeDtypeStruct((HEADS, s, 1), f32)],
        compiler_params=_cparams(("parallel", "arbitrary")))(q, k, v)


def _flash_bwd_dq(q, k, v, o, do, lse, *, name, tq=FLASH_TILE):
    s = q.shape[0]
    tq = min(tq, s)

    def body(q_ref, k_ref, v_ref, o_ref, do_ref, lse_ref, dq_ref, dl_ref):
        i = pl.program_id(1)
        qv = q_ref[...]
        dov = do_ref[...].astype(f32)
        delta = jnp.sum(dov * o_ref[...].astype(f32), axis=-1, keepdims=True)
        dob = dov.astype(bf16)
        lsev = lse_ref[0]

        def step(j, acc, masked):
            st = pl.multiple_of(j * tq, tq)
            kj, vj = k_ref[pl.ds(st, tq), :], v_ref[pl.ds(st, tq), :]
            sc = lax.dot_general(qv, kj, _NT, preferred_element_type=f32)
            p = jnp.exp2(sc - lsev)
            if masked:
                rr = lax.broadcasted_iota(jnp.int32, (tq, tq), 0)
                cc = lax.broadcasted_iota(jnp.int32, (tq, tq), 1)
                p = jnp.where(cc <= rr, p, 0.0)
            dp = lax.dot_general(dob, vj, _NT, preferred_element_type=f32)
            ds = p * (dp - delta)
            return acc + jnp.dot(ds.astype(bf16), kj, preferred_element_type=f32)

        acc = lax.fori_loop(0, i, lambda j, c: step(j, c, False), jnp.zeros((tq, HEAD_PAD), f32))
        dq_ref[...] = step(i, acc, True) * _SCALE
        dl_ref[0] = delta

    blk = pl.BlockSpec((tq, HEAD_PAD), lambda h, i: (i, h))
    full = pl.BlockSpec((s, HEAD_PAD), lambda h, i: (0, h))
    col = pl.BlockSpec((1, tq, 1), lambda h, i: (h, i, 0))
    return pl.pallas_call(
        body, name=name, grid=(HEADS, s // tq), in_specs=[blk, full, full, blk, blk, col],
        out_specs=[blk, col],
        out_shape=[jax.ShapeDtypeStruct((s, HEADS * HEAD_PAD), f32), jax.ShapeDtypeStruct((HEADS, s, 1), f32)],
        compiler_params=_cparams(("parallel", "arbitrary")))(q, k, v, o, do, lse)


def _flash_bwd_dkv(q, k, v, do, lse_row, delta_row, *, name, tk=FLASH_TILE):
    s = q.shape[0]
    tk = min(tk, s)
    nblk = s // tk

    def body(q_ref, k_ref, v_ref, do_ref, lse_ref, dl_ref, dk_ref, dv_ref):
        j = pl.program_id(1)
        kv_, vv = k_ref[...], v_ref[...]

        def step(i, carry, masked):
            dk, dv = carry
            st = pl.multiple_of(i * tk, tk)
            qi = q_ref[pl.ds(st, tk), :]
            doi = do_ref[pl.ds(st, tk), :].astype(bf16)
            lse_i = lse_ref[0, :, pl.ds(st, tk)]
            dl_i = dl_ref[0, :, pl.ds(st, tk)]
            st_ = lax.dot_general(kv_, qi, _NT, preferred_element_type=f32)
            pt = jnp.exp2(st_ - lse_i)
            if masked:
                kk = lax.broadcasted_iota(jnp.int32, (tk, tk), 0)
                qq = lax.broadcasted_iota(jnp.int32, (tk, tk), 1)
                pt = jnp.where(kk <= qq, pt, 0.0)
            dv = dv + jnp.dot(pt.astype(bf16), doi, preferred_element_type=f32)
            dpt = lax.dot_general(vv, doi, _NT, preferred_element_type=f32)
            dst = pt * (dpt - dl_i)
            dk = dk + jnp.dot(dst.astype(bf16), qi, preferred_element_type=f32)
            return dk, dv

        zero = jnp.zeros((tk, HEAD_PAD), f32)
        carry = step(j, (zero, zero), True)
        dk, dv = lax.fori_loop(j + 1, nblk, lambda i, c: step(i, c, False), carry)
        dk_ref[...] = dk * (1.0 / _LOG2E)
        dv_ref[...] = dv

    blk = pl.BlockSpec((tk, HEAD_PAD), lambda h, j: (j, h))
    full = pl.BlockSpec((s, HEAD_PAD), lambda h, j: (0, h))
    rowv = pl.BlockSpec((1, 1, s), lambda h, j: (h, 0, 0))
    out = jax.ShapeDtypeStruct((s, HEADS * HEAD_PAD), f32)
    return pl.pallas_call(
        body, name=name, grid=(HEADS, nblk), in_specs=[full, blk, blk, full, rowv, rowv],
        out_specs=[blk, blk], out_shape=[out, out],
        compiler_params=_cparams(("parallel", "arbitrary")))(q, k, v, do, lse_row, delta_row)


SUBLANES = 8


def _shift_down(x, d):
    r = pltpu.roll(x, d, axis=0)
    t = lax.broadcasted_iota(jnp.int32, (SUBLANES, x.shape[1]), 0)
    head = jnp.where(t < d, 0.0, r[:SUBLANES])
    return head if x.shape[0] == SUBLANES else jnp.concatenate([head, r[SUBLANES:]], axis=0)


def _shift_up(x, d):
    s = x.shape[0]
    r = pltpu.roll(x, s - d, axis=0)
    t = lax.broadcasted_iota(jnp.int32, (SUBLANES, x.shape[1]), 0)
    tail = jnp.where(t >= SUBLANES - d, 0.0, r[s - SUBLANES:])
    return tail if s == SUBLANES else jnp.concatenate([r[:s - SUBLANES], tail], axis=0)


def _taps(w_ref):
    return w_ref[0:1, :], w_ref[1:2, :], w_ref[2:3, :]


def _conv3(u, w):
    u1, u2 = _shift_down(u, 1), _shift_down(u, 2)
    return w[0] * u2 + w[1] * u1 + w[2] * u, (u1, u2)


def _conv3_t(g, w):
    return w[2] * g + w[1] * _shift_up(g, 1) + w[0] * _shift_up(g, 2)


def _conv3_dw(dw_ref, g, u, shifted):
    dw_ref[0:1, :] = jnp.sum(g * shifted[1], axis=0, keepdims=True)
    dw_ref[1:2, :] = jnp.sum(g * shifted[0], axis=0, keepdims=True)
    dw_ref[2:3, :] = jnp.sum(g * u, axis=0, keepdims=True)


_GB, _GC, _CI = 512 // LANES, 1024 // LANES, 1536 // LANES


def _sconv_fwd(proj, w, *, name):
    s = proj.shape[0]

    def body(gb_ref, gc_ref, ci_ref, w_ref, o_ref):
        o_ref[...] = (gb_ref[...] * _conv3(gc_ref[...] * ci_ref[...], _taps(w_ref))[0]).astype(bf16)

    col = lambda off: pl.BlockSpec((s, LANES), lambda j: (0, off + j))
    return pl.pallas_call(
        body, name=name, grid=(CONV_CH // LANES,),
        in_specs=[col(_GB), col(_GC), col(_CI), pl.BlockSpec((3, LANES), lambda j: (0, j))],
        out_specs=pl.BlockSpec((s, LANES), lambda j: (0, j)),
        out_shape=jax.ShapeDtypeStruct((s, CONV_CH), bf16),
        compiler_params=_cparams(("parallel",)))(proj, proj, proj, w)


def _sconv_bwd(dmix, proj, w, *, name):
    s = proj.shape[0]

    def body(do_ref, gb_ref, gc_ref, ci_ref, w_ref, dgb_ref, dgc_ref, dci_ref, dw_ref):
        wv, gc, ci, do = _taps(w_ref), gc_ref[...], ci_ref[...], do_ref[...].astype(f32)
        u = gc * ci
        conv, shifted = _conv3(u, wv)
        dgb_ref[...] = (do * conv).astype(bf16)
        dc = do * gb_ref[...]
        du = _conv3_t(dc, wv)
        dgc_ref[...] = (du * ci).astype(bf16)
        dci_ref[...] = (du * gc).astype(bf16)
        _conv3_dw(dw_ref, dc, u, shifted)

    col = lambda off: pl.BlockSpec((s, LANES), lambda j: (0, off + j))
    out = jax.ShapeDtypeStruct((s, CONV_CH), bf16)
    return pl.pallas_call(
        body, name=name, grid=(CONV_CH // LANES,),
        in_specs=[col(HEADS), col(_GB), col(_GC), col(_CI), pl.BlockSpec((3, LANES), lambda j: (0, j))],
        out_specs=[col(0), col(0), col(0), pl.BlockSpec((3, LANES), lambda j: (0, j))],
        out_shape=[out, out, out, jax.ShapeDtypeStruct((3, CONV_CH), f32)],
        compiler_params=_cparams(("parallel",)))(dmix, proj, proj, proj, w)


def _ffn_act_fwd(zg, zv, cwg, cwv, *, name):
    s, f = zg.shape

    def body(zg_ref, zv_ref, wg_ref, wv_ref, o_ref):
        o_ref[...] = (jax.nn.silu(_conv3(zg_ref[...], _taps(wg_ref))[0]) * _conv3(zv_ref[...], _taps(wv_ref))[0]).astype(bf16)

    col = pl.BlockSpec((s, LANES), lambda j: (0, j))
    wsp = pl.BlockSpec((3, LANES), lambda j: (0, j))
    return pl.pallas_call(
        body, name=name, grid=(f // LANES,), in_specs=[col, col, wsp, wsp], out_specs=col,
        out_shape=jax.ShapeDtypeStruct((s, f), bf16), compiler_params=_cparams(("parallel",)))(zg, zv, cwg, cwv)


def _ffn_act_bwd(da, zg, zv, cwg, cwv, *, name):
    s, f = zg.shape

    def body(da_ref, zg_ref, zv_ref, wg_ref, wv_ref, dzg_ref, dzv_ref, dwg_ref, dwv_ref):
        wg, wv, zgv, zvv, dav = _taps(wg_ref), _taps(wv_ref), zg_ref[...], zv_ref[...], da_ref[...].astype(f32)
        (ug, zg_shifted), (uv, zv_shifted) = _conv3(zgv, wg), _conv3(zvv, wv)
        sg = jax.nn.sigmoid(ug)
        dug = dav * uv * (sg * (1.0 + ug * (1.0 - sg)))
        duv = dav * (ug * sg)
        dzg_ref[...] = _conv3_t(dug, wg).astype(bf16)
        dzv_ref[...] = _conv3_t(duv, wv).astype(bf16)
        _conv3_dw(dwg_ref, dug, zgv, zg_shifted)
        _conv3_dw(dwv_ref, duv, zvv, zv_shifted)

    col = pl.BlockSpec((s, LANES), lambda j: (0, j))
    wsp = pl.BlockSpec((3, LANES), lambda j: (0, j))
    act, wsh = jax.ShapeDtypeStruct((s, f), bf16), jax.ShapeDtypeStruct((3, f), f32)
    return pl.pallas_call(
        body, name=name, grid=(f // LANES,), in_specs=[col, col, col, wsp, wsp], out_specs=[col, col, wsp, wsp],
        out_shape=[act, act, wsh, wsh], compiler_params=_cparams(("parallel",)))(da, zg, zv, cwg, cwv)


def _expand_mat():
    return jnp.asarray(np.kron(np.eye(STATE, dtype=np.float32), np.ones((1, GROUP), np.float32)))


def _disc_fn(lr, li, ls, br, bi, e):
    dt = jnp.exp(ls)
    mag = jnp.exp(lr * dt)
    ar, ai = mag * jnp.cos(li * dt), mag * jnp.sin(li * dt)
    nr, ni = ar - 1.0, ai
    den = lr * lr + li * li
    zr, zi = (nr * lr + ni * li) / den, (ni * lr - nr * li) / den
    zrr = jnp.dot(zr, e, precision=lax.Precision.HIGHEST, preferred_element_type=f32)
    zir = jnp.dot(zi, e, precision=lax.Precision.HIGHEST, preferred_element_type=f32)
    return ar, ai, zrr * br - zir * bi, zrr * bi + zir * br


def _disc_fwd(lr, li, ls, br, bi, *, name):
    def body(lr_ref, li_ref, ls_ref, br_ref, bi_ref, e_ref, ar_ref, ai_ref, bbr_ref, bbi_ref):
        ar, ai, bbr, bbi = _disc_fn(lr_ref[...], li_ref[...], ls_ref[...], br_ref[...], bi_ref[...], e_ref[...])
        ar_ref[...], ai_ref[...], bbr_ref[...], bbi_ref[...] = ar, ai, bbr, bbi

    sq, wide = jax.ShapeDtypeStruct((GROUPS, STATE), f32), jax.ShapeDtypeStruct((GROUPS, STATE * GROUP), f32)
    return pl.pallas_call(body, name=name, out_shape=[sq, sq, wide, wide],
                          compiler_params=_cparams())(lr, li, ls, br, bi, _expand_mat())


def _disc_bwd(lr, li, ls, br, bi, dar, dai, dbbr, dbbi, *, name):
    def body(lr_ref, li_ref, ls_ref, br_ref, bi_ref, e_ref, dar_ref, dai_ref, dbbr_ref, dbbi_ref,
             dlr_ref, dli_ref, dls_ref, dbr_ref, dbi_ref):
        ev = e_ref[...]
        _, vjp = jax.vjp(lambda a, b, c, d_, e_: _disc_fn(a, b, c, d_, e_, ev),
                         lr_ref[...], li_ref[...], ls_ref[...], br_ref[...], bi_ref[...])
        dlr, dli, dls, dbr, dbi = vjp((dar_ref[...], dai_ref[...], dbbr_ref[...], dbbi_ref[...]))
        dlr_ref[...], dli_ref[...], dls_ref[...], dbr_ref[...], dbi_ref[...] = dlr, dli, dls, dbr, dbi

    sq, wide = jax.ShapeDtypeStruct((GROUPS, STATE), f32), jax.ShapeDtypeStruct((GROUPS, STATE * GROUP), f32)
    return pl.pallas_call(body, name=name, out_shape=[sq, sq, jax.ShapeDtypeStruct((GROUPS, 1), f32), wide, wide],
                          compiler_params=_cparams())(lr, li, ls, br, bi, _expand_mat(), dar, dai, dbbr, dbbi)


SCAN_TILE = 64
SCAN_PAIRS = 2


def _tile_shift(v, d, reverse):
    if d % 8:
        return _shift_up(v, d) if reverse else _shift_down(v, d)
    z = jnp.zeros((d, v.shape[1]), v.dtype)
    return jnp.concatenate([v[d:], z], axis=0) if reverse else jnp.concatenate([z, v[:v.shape[0] - d]], axis=0)


def _tile_scan(r, i, pows, reverse):
    d = 1
    for br, bi in pows:
        rs, is_ = _tile_shift(r, d, reverse), _tile_shift(i, d, reverse)
        r, i = r + br * rs - bi * is_, i + br * is_ + bi * rs
        d *= 2
    return r, i


def _scan_setup(ar, ai, reverse):
    if reverse:
        ai = -ai
    pows, br, bi, d = [], ar, ai, 1
    while d < SCAN_TILE:
        pows.append((br, bi))
        br, bi, d = br * br - bi * bi, 2.0 * br * bi, 2 * d
    row = lax.broadcasted_iota(jnp.int32, (SCAN_TILE, LANES), 0)
    hit = row == (SCAN_TILE - 1 if reverse else 0)
    pr, pi = _tile_scan(jnp.where(hit, ar, 0.0), jnp.where(hit, ai, 0.0), pows, reverse)
    return pows, pr, pi


def _carry_in(r, i, pr, pi, cr, ci):
    crb, cib = jnp.broadcast_to(cr, r.shape), jnp.broadcast_to(ci, i.shape)
    return r + pr * crb - pi * cib, i + pr * cib + pi * crb


def _pair_cols(q):
    return slice(q * PAIR_LANES, q * PAIR_LANES + LANES), slice(q * PAIR_LANES + LANES, (q + 1) * PAIR_LANES)


_SCAN_W = SCAN_PAIRS * PAIR_LANES


def _scan_specs(s, w):
    per = w.shape[2] // _SCAN_W
    src = pl.BlockSpec((s, LANES), lambda g: (0, g // per))
    mat = pl.BlockSpec((1, LANES, _SCAN_W), lambda g: (g // per, 0, g % per))
    col = pl.BlockSpec((s, _SCAN_W), lambda g: (0, g))
    vec = pl.BlockSpec((SCAN_PAIRS, 1, LANES), lambda g: (g, 0, 0))
    return src, mat, col, vec, (w.shape[0] * per,)


def _scan_fwd(u, wb, ar, ai, *, name):
    s = u.shape[0]
    nt = s // SCAN_TILE

    def body(u_ref, w_ref, ar_ref, ai_ref, x_ref):
        setups = [_scan_setup(ar_ref[q], ai_ref[q], False) for q in range(SCAN_PAIRS)]
        wv = w_ref[0]

        def tile_rows(k):
            return pl.ds(pl.multiple_of(k * SCAN_TILE, SCAN_TILE), SCAN_TILE)

        def tile_in(k):
            return jnp.dot(u_ref[tile_rows(k), :].astype(bf16), wv, preferred_element_type=f32)

        def step(k, carry):
            rows, bu = tile_rows(k), carry[-1]
            ahead = tile_in(jnp.minimum(k + 1, nt - 1))
            out = []
            for q, (pows, pr, pi) in enumerate(setups):
                rc, ic = _pair_cols(q)
                r, i = _tile_scan(bu[:, rc], bu[:, ic], pows, False)
                r, i = _carry_in(r, i, pr, pi, carry[2 * q], carry[2 * q + 1])
                x_ref[rows, rc] = r.astype(bf16)
                x_ref[rows, ic] = i.astype(bf16)
                out += [r[SCAN_TILE - 1:SCAN_TILE, :], i[SCAN_TILE - 1:SCAN_TILE, :]]
            return tuple(out) + (ahead,)

        lax.fori_loop(0, nt, step, tuple(jnp.zeros((1, LANES), f32) for _ in range(2 * SCAN_PAIRS)) + (tile_in(0),))

    src, mat, col, vec, grid = _scan_specs(s, wb)
    return pl.pallas_call(body, name=name, grid=grid, in_specs=[src, mat, vec, vec], out_specs=col,
                          out_shape=jax.ShapeDtypeStruct((s, wb.shape[0] * wb.shape[2]), bf16),
                          compiler_params=_cparams(("parallel",)))(u, wb, ar, ai)


def _scan_bwd(dy, cbt, x, ar, ai, *, name):
    s = dy.shape[0]
    nt = s // SCAN_TILE

    def fold(v):
        out = v[0:8]
        for r in range(8, SCAN_TILE, 8):
            out = out + v[r:r + 8]
        return out

    def body(dy_ref, w_ref, x_ref, ar_ref, ai_ref, g_ref, dar_ref, dai_ref):
        setups = [_scan_setup(ar_ref[q], ai_ref[q], True) for q in range(SCAN_PAIRS)]
        row = lax.broadcasted_iota(jnp.int32, (SCAN_TILE, LANES), 0)
        wv = w_ref[0]

        def tile_in(k):
            return jnp.dot(dy_ref[pl.ds(pl.multiple_of(k * SCAN_TILE, SCAN_TILE), SCAN_TILE), :], wv, preferred_element_type=f32)

        def step(kk, carry):
            k = nt - 1 - kk
            start = pl.multiple_of(k * SCAN_TILE, SCAN_TILE)
            rows = pl.ds(start, SCAN_TILE)
            prev16 = pl.ds(pl.multiple_of(jnp.maximum(start - 16, 0), 16), 16)
            dx = carry[-1]
            ahead = tile_in(jnp.maximum(k - 1, 0))

            def before(cols):
                first = jnp.where(k > 0, x_ref[prev16, cols][15:16, :].astype(f32), 0.0)
                return jnp.where(row == 0, first, pltpu.roll(x_ref[rows, cols].astype(f32), 1, axis=0))

            out = []
            for q, (pows, pr, pi) in enumerate(setups):
                rc, ic = _pair_cols(q)
                cr, ci, acc_r, acc_i = carry[4 * q:4 * q + 4]
                gr, gi = _tile_scan(dx[:, rc], dx[:, ic], pows, True)
                gr, gi = _carry_in(gr, gi, pr, pi, cr, ci)
                g_ref[rows, rc] = gr.astype(bf16)
                g_ref[rows, ic] = gi.astype(bf16)
                xr, xi = before(rc), before(ic)
                out += [gr[0:1, :], gi[0:1, :], acc_r + fold(gr * xr + gi * xi), acc_i + fold(gi * xr - gr * xi)]
            return tuple(out) + (ahead,)

        init = (jnp.zeros((1, LANES), f32), jnp.zeros((1, LANES), f32), jnp.zeros((8, LANES), f32), jnp.zeros((8, LANES), f32))
        res = lax.fori_loop(0, nt, step, init * SCAN_PAIRS + (tile_in(nt - 1),))
        for q in range(SCAN_PAIRS):
            dar_ref[q] = jnp.sum(res[4 * q + 2], axis=0, keepdims=True)
            dai_ref[q] = jnp.sum(res[4 * q + 3], axis=0, keepdims=True)

    src, mat, col, vec, grid = _scan_specs(s, cbt)
    vsh = jax.ShapeDtypeStruct((GROUPS // 2, 1, LANES), f32)
    return pl.pallas_call(body, name=name, grid=grid, in_specs=[src, mat, col, vec, vec],
                          out_specs=[col, vec, vec], out_shape=[jax.ShapeDtypeStruct(x.shape, bf16), vsh, vsh],
                          compiler_params=_cparams(("parallel",)))(dy, cbt, x, ar, ai)


_GELU_C = math.sqrt(2.0 / math.pi)


def _gelu_fwd(y, u, dsk, *, name, ts=512):
    s, d = y.shape
    ts = min(ts, s)

    def body(y_ref, u_ref, d_ref, o_ref):
        o_ref[...] = jax.nn.gelu(y_ref[...] + d_ref[...] * u_ref[...]).astype(bf16)

    row, vec = pl.BlockSpec((ts, d), lambda i: (i, 0)), pl.BlockSpec((1, d), lambda i: (0, 0))
    return pl.pallas_call(body, name=name, grid=(s // ts,), in_specs=[row, row, vec], out_specs=row,
                          out_shape=jax.ShapeDtypeStruct((s, d), bf16), compiler_params=_cparams(("parallel",)))(y, u, dsk)


def _gelu_bwd(dg, y, u, dsk, *, name, ts=512):
    s, d = y.shape
    ts = min(ts, s)

    def body(dg_ref, y_ref, u_ref, d_ref, dy_ref, du_ref, dd_ref):
        @pl.when(pl.program_id(0) == 0)
        def _():
            dd_ref[...] = jnp.zeros_like(dd_ref)

        uv, dv = u_ref[...], d_ref[...]
        z = y_ref[...] + dv * uv
        th = jnp.tanh(_GELU_C * (z + 0.044715 * z * z * z))
        dz = dg_ref[...] * (0.5 * (1.0 + th) + 0.5 * z * (1.0 - th * th) * _GELU_C * (1.0 + 3 * 0.044715 * z * z))
        dy_ref[...] = dz.astype(bf16)
        du_ref[...] = dz * dv
        dd_ref[...] += jnp.sum(dz * uv, axis=0, keepdims=True)

    row, vec = pl.BlockSpec((ts, d), lambda i: (i, 0)), pl.BlockSpec((1, d), lambda i: (0, 0))
    return pl.pallas_call(
        body, name=name, grid=(s // ts,), in_specs=[row, row, row, vec], out_specs=[row, row, vec],
        out_shape=[jax.ShapeDtypeStruct((s, d), bf16), jax.ShapeDtypeStruct((s, d), f32), jax.ShapeDtypeStruct((1, d), f32)],
        compiler_params=_cparams(("arbitrary",)))(dg, y, u, dsk)


def _glu_fwd(x, a, b, *, name, ts=512):
    s, d = x.shape
    ts = min(ts, s)

    def body(x_ref, a_ref, b_ref, o_ref):
        o_ref[...] = x_ref[...] + a_ref[...] * jax.nn.sigmoid(b_ref[...])

    row = pl.BlockSpec((ts, d), lambda i: (i, 0))
    return pl.pallas_call(body, name=name, grid=(s // ts,), in_specs=[row, row, row], out_specs=row,
                          out_shape=jax.ShapeDtypeStruct((s, d), f32), compiler_params=_cparams(("parallel",)))(x, a, b)


def _glu_bwd(dx, a, b, *, name, ts=512):
    s, d = dx.shape
    ts = min(ts, s)

    def body(dx_ref, a_ref, b_ref, da_ref, db_ref):
        sg = jax.nn.sigmoid(b_ref[...])
        dxv = dx_ref[...]
        da_ref[...] = (dxv * sg).astype(bf16)
        db_ref[...] = (dxv * a_ref[...] * sg * (1.0 - sg)).astype(bf16)

    row = pl.BlockSpec((ts, d), lambda i: (i, 0))
    out = jax.ShapeDtypeStruct((s, d), bf16)
    return pl.pallas_call(body, name=name, grid=(s // ts,), in_specs=[row, row, row], out_specs=[row, row],
                          out_shape=[out, out], compiler_params=_cparams(("parallel",)))(dx, a, b)


def _add(a, b, *, name, ts=512):
    s, d = a.shape
    ts = min(ts, s)

    def body(a_ref, b_ref, o_ref):
        o_ref[...] = (a_ref[...].astype(f32) + b_ref[...].astype(f32)).astype(bf16)

    row = pl.BlockSpec((ts, d), lambda i: (i, 0))
    return pl.pallas_call(body, name=name, grid=(s // ts,), in_specs=[row, row], out_specs=row,
                          out_shape=jax.ShapeDtypeStruct((s, d), bf16), compiler_params=_cparams(("parallel",)))(a, b)


def _loss_head(y, target, *, name, ts=512):
    s, d = y.shape
    ts = min(ts, s)

    def body(y_ref, t_ref, dy_ref, dyb_ref, l_ref):
        @pl.when(pl.program_id(0) == 0)
        def _():
            l_ref[...] = jnp.zeros_like(l_ref)

        e = y_ref[...] - t_ref[...]
        dy = e * (1.0 / d)
        dy_ref[...] = dy
        dyb_ref[...] = dy.astype(bf16)
        l_ref[...] += 0.5 * jnp.sum(jnp.mean(e * e, axis=-1, keepdims=True))

    row = pl.BlockSpec((ts, d), lambda i: (i, 0))
    return pl.pallas_call(
        body, name=name, grid=(s // ts,), in_specs=[row, row],
        out_specs=[row, row, pl.BlockSpec((8, LANES), lambda i: (0, 0))],
        out_shape=[jax.ShapeDtypeStruct((s, d), f32), jax.ShapeDtypeStruct((s, d), bf16), jax.ShapeDtypeStruct((8, LANES), f32)],
        compiler_params=_cparams(("arbitrary",)))(y, target)


def _adamw(w, g, m, v, *, name, tr=128):
    r, c = w.shape

    def body(w_ref, g_ref, m_ref, v_ref, d_ref, mo_ref, vo_ref):
        gv = g_ref[...]
        mn = ADAM_B1 * m_ref[...] + (1.0 - ADAM_B1) * gv
        vn = ADAM_B2 * v_ref[...] + (1.0 - ADAM_B2) * (gv * gv)
        m_hat = mn / (1.0 - ADAM_B1 ** ADAM_STEP)
        v_hat = vn / (1.0 - ADAM_B2 ** ADAM_STEP)
        d_ref[...] = -ADAM_LR * (m_hat / (jnp.sqrt(v_hat) + ADAM_EPS) + ADAM_WD * w_ref[...])
        mo_ref[...] = mn
        vo_ref[...] = vn

    row = pl.BlockSpec((tr, c), lambda i: (i, 0))
    out = jax.ShapeDtypeStruct((r, c), f32)
    return pl.pallas_call(body, name=name, grid=(r // tr,), in_specs=[row] * 4, out_specs=[row] * 3,
                          out_shape=[out, out, out], compiler_params=_cparams(("parallel",)))(w, g, m, v)


def _sum_slabs(land, *, name, tr=128):
    n, r, c = land.shape

    def body(l_ref, o_ref):
        acc = l_ref[0].astype(f32)
        for i in range(1, n):
            acc = acc + l_ref[i].astype(f32)
        o_ref[...] = acc

    return pl.pallas_call(body, name=name, grid=(r // tr,), in_specs=[pl.BlockSpec((n, tr, c), lambda i: (0, i, 0))],
                          out_specs=pl.BlockSpec((tr, c), lambda i: (i, 0)), out_shape=jax.ShapeDtypeStruct((r, c), f32),
                          compiler_params=_cparams(("parallel",)))(land)


def _pair_sum(g, theirs, *, name, tr=256):
    n, r, c = theirs.shape

    def body(c_ref, g_ref, t_ref, o_ref):
        o_ref[...] = (g_ref[...].astype(f32) + t_ref[...].astype(f32)).astype(bf16)

    blk = pl.BlockSpec((1, tr, c), lambda j, i, c_ref: (j, i, 0))
    mine = pl.BlockSpec((1, tr, c), lambda j, i, c_ref: (2 * j + c_ref[0], i, 0))
    return pl.pallas_call(
        body, name=name,
        grid_spec=pltpu.PrefetchScalarGridSpec(num_scalar_prefetch=1, grid=(n, r // tr), in_specs=[mine, blk], out_specs=blk),
        out_shape=jax.ShapeDtypeStruct(theirs.shape, bf16),
        compiler_params=_cparams(("parallel", "parallel")))(lax.axis_index("c").astype(jnp.int32).reshape(1), g, theirs)


_MESH = pl.DeviceIdType.MESH
_HBM = pl.BlockSpec(memory_space=pltpu.HBM)
N_CHIP = N_DEV // 2


def _position():
    return lax.axis_index("x"), lax.axis_index("y"), lax.axis_index("c")


def _gather8(x, *, name):
    half = x.shape[0] // 2

    def body(x_ref, o_ref, send_sems, recv_sems, local_sem):
        xx, yy, cc = _position()
        me, sibling = (xx, yy, cc), (xx, yy, 1 - cc)
        here, xn, yn, dg = (xx, yy), (1 - xx, yy), (xx, 1 - yy), (1 - xx, 1 - yy)
        first, second = pl.ds(0, half), pl.ds(half, half)

        def slab(chip, pc, rows=None):
            ref = o_ref.at[4 * chip[0] + 2 * chip[1] + pc]
            return ref if rows is None else ref.at[rows]

        def copy(k, ref, to, src=None):
            return pltpu.make_async_remote_copy(src_ref=ref if src is None else src, dst_ref=ref, send_sem=send_sems.at[k],
                                                recv_sem=recv_sems.at[k], device_id=to, device_id_type=_MESH)

        mine = pltpu.make_async_copy(x_ref, slab(here, cc), local_sem)
        mine.start()
        sends = [copy(0, slab(here, cc), sibling, src=x_ref), copy(1, slab(here, cc), (*xn, cc), src=x_ref),
                 copy(2, slab(here, cc), (*yn, cc), src=x_ref)]
        for cp in sends:
            cp.start()
        copy(1, slab(xn, cc), me).wait_recv()
        sends += [copy(3, slab(xn, cc, first), (*yn, cc)), copy(5, slab(xn, cc), sibling)]
        copy(2, slab(yn, cc), me).wait_recv()
        sends += [copy(4, slab(yn, cc, second), (*xn, cc)), copy(6, slab(yn, cc), sibling)]
        for cp in sends[3:]:
            cp.start()
        copy(3, slab(dg, cc, first), me).wait_recv()
        copy(4, slab(dg, cc, second), me).wait_recv()
        sends.append(copy(7, slab(dg, cc), sibling))
        sends[-1].start()
        for k, chip in ((0, here), (5, xn), (6, yn), (7, dg)):
            copy(k, slab(chip, 1 - cc), me).wait_recv()
        for cp in sends:
            cp.wait_send()
        mine.wait()

    return pl.pallas_call(
        body, name=name, in_specs=[_HBM], out_specs=_HBM, out_shape=jax.ShapeDtypeStruct((N_DEV,) + x.shape, x.dtype),
        scratch_shapes=[pltpu.SemaphoreType.DMA((N_DEV,)), pltpu.SemaphoreType.DMA((N_DEV,)), pltpu.SemaphoreType.DMA],
    )(x)


def _pair_exchange(g, *, name):
    def body(g_ref, land_ref, send_sems, recv_sems):
        xx, yy, cc = _position()
        copies = []
        for j in range(N_CHIP):
            cp = pltpu.make_async_remote_copy(src_ref=g_ref.at[2 * j + 1 - cc], dst_ref=land_ref.at[j], send_sem=send_sems.at[j],
                                              recv_sem=recv_sems.at[j], device_id=(xx, yy, 1 - cc), device_id_type=_MESH)
            cp.start()
            copies.append(cp)
        for cp in copies:
            cp.wait_recv()
        for cp in copies:
            cp.wait_send()

    sems = pltpu.SemaphoreType.DMA((N_CHIP,))
    return pl.pallas_call(body, name=name, in_specs=[_HBM], out_specs=_HBM,
                          out_shape=jax.ShapeDtypeStruct((N_CHIP,) + g.shape[1:], g.dtype), scratch_shapes=[sems, sems])(g)


def _cross_exchange(p, *, name):
    half = p.shape[1] // 2

    def body(p_ref, o_ref, relay_ref, send_sems, recv_sems, local_sem):
        xx, yy, cc = _position()
        me = (xx, yy, cc)
        xn, yn, dg = (1 - xx, yy), (xx, 1 - yy), (1 - xx, 1 - yy)
        idx = lambda chip: 2 * chip[0] + chip[1]
        mine = idx((xx, yy))
        first, second = pl.ds(0, half), pl.ds(half, half)

        def copy(k, src, dst, to):
            return pltpu.make_async_remote_copy(src_ref=src, dst_ref=dst, send_sem=send_sems.at[k], recv_sem=recv_sems.at[k],
                                                device_id=to, device_id_type=_MESH)

        local = pltpu.make_async_copy(p_ref.at[mine], o_ref.at[mine], local_sem)
        local.start()
        sends = [copy(0, p_ref.at[idx(xn)], o_ref.at[mine], (*xn, cc)),
                 copy(1, p_ref.at[idx(dg)].at[first], relay_ref.at[0], (*xn, cc)),
                 copy(2, p_ref.at[idx(yn)], o_ref.at[mine], (*yn, cc)),
                 copy(3, p_ref.at[idx(dg)].at[second], relay_ref.at[1], (*yn, cc))]
        for cp in sends:
            cp.start()
        copy(1, relay_ref.at[0], relay_ref.at[0], me).wait_recv()
        sends.append(copy(4, relay_ref.at[0], o_ref.at[idx(xn)].at[first], (*yn, cc)))
        sends[-1].start()
        copy(3, relay_ref.at[1], relay_ref.at[1], me).wait_recv()
        sends.append(copy(5, relay_ref.at[1], o_ref.at[idx(yn)].at[second], (*xn, cc)))
        sends[-1].start()
        for k, dst in ((0, o_ref.at[idx(xn)]), (2, o_ref.at[idx(yn)]), (4, o_ref.at[idx(dg)].at[first]),
                       (5, o_ref.at[idx(dg)].at[second])):
            copy(k, dst, dst, me).wait_recv()
        for cp in sends:
            cp.wait_send()
        local.wait()

    sems = pltpu.SemaphoreType.DMA((6,))
    relay = jax.ShapeDtypeStruct((2, half) + p.shape[2:], p.dtype)
    return pl.pallas_call(body, name=name, in_specs=[_HBM], out_specs=[_HBM, _HBM],
                          out_shape=[jax.ShapeDtypeStruct(p.shape, p.dtype), relay],
                          scratch_shapes=[sems, sems, pltpu.SemaphoreType.DMA])(p)[0]


def _all_sum(x, *, name):
    return _sum_slabs(_gather8(x, name=f"gather_{name}"), name=f"sum_{name}", tr=min(128, x.shape[0]))


def _pack_slabs(parts, rows, axis=0):
    lead = parts[0].shape[:axis]
    slabs = [p.reshape(lead + (-1, D)) for p in parts]
    used = sum(sl.shape[axis] for sl in slabs)
    return jnp.concatenate(slabs + [jnp.zeros(lead + (rows - used, D), slabs[0].dtype)], axis=axis)


def _unpack_slabs(slab, shapes):
    lead, out, off = slab.shape[:-2], [], 0
    for shp in shapes:
        n = int(np.prod(shp)) // D
        out.append(slab[..., off:off + n, :].reshape(lead + tuple(shp)))
        off += n
    return out


def _pack_rows(parts, rows):
    flat = jnp.concatenate([p.reshape(-1) for p in parts])
    return jnp.pad(flat, (0, rows * D - flat.shape[0])).reshape(rows, D)


def _unpack_rows(slab, shapes):
    flat, out, off = slab.reshape(-1), [], 0
    for shp in shapes:
        n = int(np.prod(shp))
        out.append(flat[off:off + n].reshape(shp))
        off += n
    return out


def _full_shape(shard, axis):
    return tuple(d * N_DEV if i == axis else d for i, d in enumerate(shard))


def _row(v):
    return v.reshape(1, -1).astype(f32)


def _pad_gain(g):
    return jnp.pad(g.astype(f32), (0, HEAD_PAD - QK)).reshape(1, HEAD_PAD)


def _ffn_fwd(x, p, tag):
    h = _rms_fwd(x, p["norm"], name=f"ffn_norm_{tag}")
    zg = _mm(h, p["wgT"], tb=True, name=f"ffn_up_g_{tag}")
    zv = _mm(h, p["wvT"], tb=True, name=f"ffn_up_v_{tag}")
    a = _ffn_act_fwd(zg, zv, p["cwg"], p["cwv"], name=f"ffn_act_{tag}")
    y = _mm(a, p["wd"], add=x, name=f"ffn_down_{tag}")
    return y, (x, h, zg, zv, a)


def _ffn_bwd(dy, dyb, p, saved, tag):
    x, h, zg, zv, a = saved
    g = {}
    da = _mm(dyb, p["wd"], tb=True, out_dtype=bf16, name=f"ffn_down_dx_{tag}")
    g["wd"] = _mm(a, dyb, ta=True, out_dtype=bf16, name=f"ffn_down_dw_{tag}")
    dzg, dzv, g["cwg"], g["cwv"] = _ffn_act_bwd(da, zg, zv, p["cwg"], p["cwv"], name=f"ffn_act_bwd_{tag}")
    g["wgT"] = _mm(dzg, h, ta=True, out_dtype=bf16, name=f"ffn_up_g_dw_{tag}")
    g["wvT"] = _mm(dzv, h, ta=True, out_dtype=bf16, name=f"ffn_up_v_dw_{tag}")
    dh = _mm(dzg, p["wgT"], name=f"ffn_up_g_dx_{tag}")
    dh = _mm(dzv, p["wvT"], add=dh, name=f"ffn_up_v_dx_{tag}")
    dx, dxb, g["norm"] = _rms_bwd(dh, x, p["norm"], res=dy, name=f"ffn_norm_bwd_{tag}")
    return dx, dxb, g


def _mla_fwd(x, p, tabs, tag):
    cos_t, sin_t = tabs
    h = _rms_fwd(x, p["norm"], name=f"attn_norm_{tag}")
    proj = _mm(h, p["w_inT"], tb=True, name=f"mix_in_{tag}")
    cqn = _rms_fwd(proj, p["cq_norm"], col=0, name=f"cq_norm_{tag}")
    ckvn = _rms_fwd(proj, p["ckv_norm"], col=1, name=f"ckv_norm_{tag}")
    q_raw = _mm(cqn, p["w_uqT"], tb=True, name=f"uq_{tag}")
    kv_raw = _mm(ckvn, p["w_ukvT"], tb=True, name=f"ukv_{tag}")
    q, k, v = _qk_prep_fwd(q_raw, kv_raw, proj, p["q_gain"], p["k_gain"], cos_t, sin_t, name=f"qk_prep_{tag}")
    o, lse = _flash_fwd(q, k, v, name=f"flash_fwd_{tag}")
    conv = _sconv_fwd(proj, p["sconv_w"], name=f"sconv_{tag}")
    mix = jnp.concatenate([o, conv], axis=1)
    y = _mm(mix, p["w_out"], add=x, name=f"mix_out_{tag}")
    return y, (x, h, proj, cqn, ckvn, q_raw, kv_raw, q, k, v, o, lse, mix)


def _mla_bwd(dy, dyb, p, tabs, saved, tag):
    cos_t, sin_t = tabs
    x, h, proj, cqn, ckvn, q_raw, kv_raw, q, k, v, o, lse, mix = saved
    s = x.shape[0]
    g = {}
    dmix = _mm(dyb, p["w_out"], tb=True, name=f"mix_out_dx_{tag}")
    g["w_out"] = _mm(mix, dyb, ta=True, out_dtype=bf16, name=f"mix_out_dw_{tag}")
    dgb, dgc, dci, g["sconv_w"] = _sconv_bwd(dmix, proj, p["sconv_w"], name=f"sconv_bwd_{tag}")
    dq, delta = _flash_bwd_dq(q, k, v, o, dmix, lse, name=f"flash_dq_{tag}")
    dk, dv = _flash_bwd_dkv(q, k, v, dmix, lse.reshape(HEADS, 1, s), delta.reshape(HEADS, 1, s), name=f"flash_dkv_{tag}")
    dq_raw, dkv_raw, dkr, g["q_gain"], g["k_gain"] = _qk_prep_bwd(
        dq, dk, dv, q_raw, kv_raw, proj, p["q_gain"], p["k_gain"], cos_t, sin_t, name=f"qk_prep_bwd_{tag}")
    dcqn = _mm(dq_raw, p["w_uqT"], name=f"uq_dx_{tag}")
    g["w_uqT"] = _mm(dq_raw, cqn, ta=True, out_dtype=bf16, name=f"uq_dw_{tag}")
    dckvn = _mm(dkv_raw, p["w_ukvT"], name=f"ukv_dx_{tag}")
    g["w_ukvT"] = _mm(dkv_raw, ckvn, ta=True, out_dtype=bf16, name=f"ukv_dw_{tag}")
    dcq, g["cq_norm"] = _rms_bwd(dcqn, proj, p["cq_norm"], col=0, out_dtype=bf16, name=f"cq_norm_bwd_{tag}")
    dckv, g["ckv_norm"] = _rms_bwd(dckvn, proj, p["ckv_norm"], col=1, out_dtype=bf16, name=f"ckv_norm_bwd_{tag}")
    dproj = jnp.concatenate([dcq, dckv, dgb, dgc, dci, dkr.astype(bf16)], axis=1)
    dh = _mm(dproj, p["w_inT"], name=f"mix_in_dx_{tag}")
    g["w_inT"] = _mm(dproj, h, ta=True, out_dtype=bf16, name=f"mix_in_dw_{tag}")
    dx, dxb, g["norm"] = _rms_bwd(dh, x, p["norm"], res=dy, name=f"attn_norm_bwd_{tag}")
    return dx, dxb, g


def _block_diag(wg):
    nb, ng, r, c = wg.shape
    eye = jnp.eye(ng, dtype=wg.dtype)
    return (wg[:, :, :, None, :] * eye[None, :, None, :, None]).reshape(nb, ng * r, ng * c)


def _s5_mats(bbr, bbi, c_re, c_im):
    nb = GROUPS // 8
    b4 = jnp.stack([bbr.reshape(GROUPS, STATE, GROUP), bbi.reshape(GROUPS, STATE, GROUP)], axis=1)
    wg = jnp.transpose(b4, (0, 3, 1, 2)).reshape(nb, 8, GROUP, 2 * STATE)
    cg = jnp.stack([c_re, -c_im], axis=1)
    cg = jnp.transpose(cg, (0, 1, 3, 2)).reshape(nb, 8, 2 * STATE, GROUP)
    return _state_layout(_block_diag(wg), 2), _state_layout(_block_diag(cg), 1)


def _state_layout(m, axis):
    shp = m.shape
    m = m.reshape(shp[:axis] + (4, 2, 2, STATE) + shp[axis + 1:])
    return jnp.swapaxes(m, axis + 1, axis + 2).reshape(shp)


def _group_blocks(d):
    d = d.reshape(GROUPS // 2, 2, GROUP, 2, 2, STATE)
    return jnp.stack([d[:, 0, :, :, 0, :], d[:, 1, :, :, 1, :]], axis=1).reshape(GROUPS, GROUP, 2, STATE)


def _s5_fwd(x, p, tag):
    h = _rms_fwd(x, p["norm"], name=f"ssm_norm_{tag}")
    u, ub = _mm(h, p["w_in"], twin=True, name=f"ssm_in_{tag}")
    ar, ai, bbr, bbi = _disc_fwd(p["lr"], p["li"], p["ls"], p["br"], p["bi"], name=f"disc_{tag}")
    wb, cb = _s5_mats(bbr, bbi, p["c_re"], p["c_im"])
    a1, a2 = ar.reshape(GROUPS // 2, 1, LANES), ai.reshape(GROUPS // 2, 1, LANES)
    xs = _scan_fwd(ub, wb.astype(bf16), a1, a2, name=f"ssm_scan_{tag}")
    y = _bd_nn(xs, cb.astype(bf16), name=f"ssm_y_{tag}")
    g = _gelu_fwd(y, u, p["d_skip"], name=f"ssm_gelu_{tag}")
    a = _mm(g, p["wgaT"], tb=True, name=f"glu_a_{tag}")
    b = _mm(g, p["wgbT"], tb=True, name=f"glu_b_{tag}")
    out = _glu_fwd(x, a, b, name=f"glu_{tag}")
    return out, (x, h, u, ub, wb, cb, a1, a2, xs, y, g, a, b)


def _s5_bwd(dout, p, saved, tag):
    x, h, u, ub, wb, cb, a1, a2, xs, y, g, a, b = saved
    gr = {}
    da, db = _glu_bwd(dout, a, b, name=f"glu_bwd_{tag}")
    dg = _mm(da, p["wgaT"], name=f"glu_a_dx_{tag}")
    dg = _mm(db, p["wgbT"], add=dg, name=f"glu_b_dx_{tag}")
    gr["wgaT"] = _mm(da, g, ta=True, out_dtype=bf16, name=f"glu_a_dw_{tag}")
    gr["wgbT"] = _mm(db, g, ta=True, out_dtype=bf16, name=f"glu_b_dw_{tag}")
    dy, du1, gr["d_skip"] = _gelu_bwd(dg, y, u, p["d_skip"], name=f"ssm_gelu_bwd_{tag}")
    dct = _group_blocks(_bd_tn_diag(dy, xs, name=f"ssm_y_dw_{tag}"))
    gs, dar, dai = _scan_bwd(dy, jnp.swapaxes(cb, 1, 2).astype(bf16), xs, a1, a2, name=f"ssm_scan_bwd_{tag}")
    du2 = _bd_nn(gs, jnp.swapaxes(wb, 1, 2).astype(bf16), name=f"ssm_bu_dx_{tag}")
    dwg = _group_blocks(_bd_tn_diag(ub, gs, name=f"ssm_bu_dw_{tag}"))
    du = _add(du1, du2, name=f"ssm_du_{tag}")
    dh = _mm(du, p["w_in"], tb=True, name=f"ssm_in_dx_{tag}")
    gr["w_in"] = _mm(h, du, ta=True, out_dtype=bf16, name=f"ssm_in_dw_{tag}")
    dx, dxb, gr["norm"] = _rms_bwd(dh, x, p["norm"], res=dout, name=f"ssm_norm_bwd_{tag}")
    dbb = jnp.transpose(dwg, (2, 0, 3, 1)).reshape(2, GROUPS, STATE * GROUP)
    gr["c_re"] = dct[:, :, 0, :]
    gr["c_im"] = -dct[:, :, 1, :]
    dlr, dli, dls, dbr, dbi = _disc_bwd(p["lr"], p["li"], p["ls"], p["br"], p["bi"], dar.reshape(GROUPS, STATE),
                                        dai.reshape(GROUPS, STATE), dbb[0], dbb[1], name=f"disc_bwd_{tag}")
    gr["lr"], gr["li"], gr["ls"] = dlr, dli, dls.reshape(GROUPS)
    gr["br"], gr["bi"] = dbr.reshape(GROUPS, STATE, GROUP), dbi.reshape(GROUPS, STATE, GROUP)
    return dx, dxb, gr


def _slab_shape(shard, axis):
    return (shard[0], shard[2], shard[1]) if axis == 2 else shard


def _to_slab(w, axis):
    return jnp.swapaxes(w, 1, 2) if axis == 2 else w


def _mix_in_pad(wt):
    z = lambda n: jnp.zeros((n, wt.shape[1]), wt.dtype)
    return jnp.concatenate([wt[:512], wt[544:2080], z(NOPE), wt[512:544], z(HEAD_PAD - QK)], axis=0)


def _mix_in_unpad(g):
    return jnp.concatenate([g[:512], g[2048 + NOPE:2048 + QK], g[512:2048]], axis=0)


def _mix_out_pad(w):
    att = jnp.pad(w[:512].reshape(HEADS, NOPE, D), ((0, 0), (NOPE, 0), (0, 0))).reshape(HEADS * HEAD_PAD, D)
    return jnp.concatenate([att, w[512:]], axis=0)


def _mix_out_unpad(g):
    att = g[:HEADS * HEAD_PAD].reshape(HEADS, HEAD_PAD, D)[:, NOPE:, :].reshape(HEADS * NOPE, D)
    return jnp.concatenate([att, g[HEADS * HEAD_PAD:]], axis=0)


def _layer_params(wl, ws, layer):
    i = layer // 2
    half = N_DEV // 2
    up = wl["ffn_w_up"][:, layer]
    ffn = dict(norm=_row(ws["ffn_norm"][layer]), wgT=up[:half].reshape(FFN_H, D), wvT=up[half:].reshape(FFN_H, D),
               cwg=ws["ffn_conv_w"][layer][:, :FFN_H], cwv=ws["ffn_conv_w"][layer][:, FFN_H:],
               wd=wl["ffn_w_down"][:, layer].reshape(FFN_H, D))
    if layer % 2 == 0:
        uq = jnp.pad(wl["w_uq"][:, i], ((0, 0), (0, HEAD_PAD - QK), (0, 0)))
        mixer = dict(norm=_row(ws["attn_norm"][i]), w_inT=_mix_in_pad(wl["mix_w_in"][:, i].reshape(-1, D)),
                     cq_norm=_row(ws["cq_norm"][i]), ckv_norm=_row(ws["ckv_norm"][i]),
                     w_uqT=uq.reshape(HEADS * HEAD_PAD, LORA), w_ukvT=wl["w_ukv"][:, i].reshape(HEADS * HEAD_PAD, LORA),
                     q_gain=_pad_gain(ws["q_gain"][i]), k_gain=_pad_gain(ws["k_gain"][i]), sconv_w=ws["sconv_w"][i],
                     w_out=_mix_out_pad(wl["mix_w_out"][:, i].reshape(D, D)))
    else:
        glu = wl["w_glu"][:, i]
        mixer = dict(norm=_row(ws["ssm_norm"][i]), w_in=wl["ssm_w_in"][:, i].reshape(D, D), lr=ws["lambda_re"][i],
                     li=ws["lambda_im"][i], ls=ws["log_step"][i].reshape(GROUPS, 1),
                     br=ws["b_re"][i].reshape(GROUPS, STATE * GROUP), bi=ws["b_im"][i].reshape(GROUPS, STATE * GROUP),
                     c_re=ws["c_re"][i], c_im=ws["c_im"][i], d_skip=_row(ws["d_skip"][i]),
                     wgaT=glu[:half].reshape(D, D), wgbT=glu[half:].reshape(D, D))
    return mixer, ffn


def _collect_grads(gm, gf):
    ev, od, half = (0, 2), (1, 3), N_DEV // 2
    st = lambda xs: jnp.stack(xs, axis=0)
    per_dev = lambda xs: jnp.stack(xs, axis=1)
    halves = lambda a, b, rows: jnp.concatenate([a.reshape(half, rows, D), b.reshape(half, rows, D)], axis=0)
    big = {
        "ffn_w_up": per_dev([halves(gf[l]["wgT"], gf[l]["wvT"], FFN_H // half) for l in range(4)]),
        "ffn_w_down": per_dev([gf[l]["wd"].reshape(N_DEV, -1, D) for l in range(4)]),
        "w_glu": per_dev([halves(gm[l]["wgaT"], gm[l]["wgbT"], D // half) for l in od]),
        "mix_w_out": per_dev([_mix_out_unpad(gm[l]["w_out"]).reshape(N_DEV, -1, D) for l in ev]),
        "ssm_w_in": per_dev([gm[l]["w_in"].reshape(N_DEV, -1, D) for l in od]),
        "w_ukv": per_dev([gm[l]["w_ukvT"].reshape(N_DEV, HEAD_PAD, LORA) for l in ev]),
        "w_uq": per_dev([gm[l]["w_uqT"].reshape(N_DEV, HEAD_PAD, LORA)[:, :QK] for l in ev]),
        "mix_w_in": per_dev([_mix_in_unpad(gm[l]["w_inT"]).reshape(N_DEV, -1, D) for l in ev]),
    }
    small = {
        "attn_norm": st([gm[l]["norm"].reshape(D) for l in ev]),
        "cq_norm": st([gm[l]["cq_norm"].reshape(LORA) for l in ev]),
        "ckv_norm": st([gm[l]["ckv_norm"].reshape(LORA) for l in ev]),
        "q_gain": st([gm[l]["q_gain"].reshape(HEAD_PAD)[:QK] for l in ev]),
        "k_gain": st([gm[l]["k_gain"].reshape(HEAD_PAD)[:QK] for l in ev]),
        "sconv_w": st([gm[l]["sconv_w"] for l in ev]),
        "ssm_norm": st([gm[l]["norm"].reshape(D) for l in od]),
        "lambda_re": st([gm[l]["lr"] for l in od]), "lambda_im": st([gm[l]["li"] for l in od]),
        "log_step": st([gm[l]["ls"] for l in od]),
        "b_re": st([gm[l]["br"] for l in od]), "b_im": st([gm[l]["bi"] for l in od]),
        "c_re": st([gm[l]["c_re"] for l in od]), "c_im": st([gm[l]["c_im"] for l in od]),
        "d_skip": st([gm[l]["d_skip"].reshape(D) for l in od]),
        "ffn_norm": st([gf[l]["norm"].reshape(D) for l in range(4)]),
        "ffn_conv_w": st([jnp.concatenate([gf[l]["cwg"], gf[l]["cwv"]], axis=1) for l in range(4)]),
    }
    return big, small


def _local_step(x, target, wl, ws):
    s = x.shape[0]
    tabs = _rope_tables(s)
    saved, params = [], []
    for layer in range(4):
        mixer, ffn = _layer_params(wl, ws, layer)
        params.append((mixer, ffn))
        if layer % 2 == 0:
            x, sm = _mla_fwd(x, mixer, tabs, f"l{layer}")
        else:
            x, sm = _s5_fwd(x, mixer, f"l{layer}")
        x, sf = _ffn_fwd(x, ffn, f"l{layer}")
        saved.append((sm, sf))
    dx, dxb, loss = _loss_head(x, target, name="loss_head")
    gm, gf = [None] * 4, [None] * 4
    for layer in reversed(range(4)):
        mixer, ffn = params[layer]
        sm, sf = saved[layer]
        dx, dxb, gf[layer] = _ffn_bwd(dx, dxb, ffn, sf, f"l{layer}")
        if layer % 2 == 0:
            dx, dxb, gm[layer] = _mla_bwd(dx, dxb, mixer, tabs, sm, f"l{layer}")
        else:
            dx, dxb, gm[layer] = _s5_bwd(dx, mixer, sm, f"l{layer}")
    return loss, dx, _collect_grads(gm, gf)


def kernel(x, attn_norm, mix_w_in, cq_norm, ckv_norm, w_uq, w_ukv, q_gain, k_gain, sconv_w, mix_w_out, ssm_norm, ssm_w_in, lambda_re, lambda_im, log_step, b_re, b_im, c_re, c_im, d_skip, w_glu, ffn_norm, ffn_w_up, ffn_conv_w, ffn_w_down, loss_target, m_attn_norm, m_mix_w_in, m_cq_norm, m_ckv_norm, m_w_uq, m_w_ukv, m_q_gain, m_k_gain, m_sconv_w, m_mix_w_out, m_ssm_norm, m_ssm_w_in, m_lambda_re, m_lambda_im, m_log_step, m_b_re, m_b_im, m_c_re, m_c_im, m_d_skip, m_w_glu, m_ffn_norm, m_ffn_w_up, m_ffn_conv_w, m_ffn_w_down, v_attn_norm, v_mix_w_in, v_cq_norm, v_ckv_norm, v_w_uq, v_w_ukv, v_q_gain, v_k_gain, v_sconv_w, v_mix_w_out, v_ssm_norm, v_ssm_w_in, v_lambda_re, v_lambda_im, v_log_step, v_b_re, v_b_im, v_c_re, v_c_im, v_d_skip, v_w_glu, v_ffn_norm, v_ffn_w_up, v_ffn_conv_w, v_ffn_w_down):
    args = dict(locals())
    wsh = {n: args[n] for n in WEIGHTS}
    msh = {n: args["m_" + n] for n in WEIGHTS}
    vsh = {n: args["v_" + n] for n in WEIGHTS}
    me = 4 * lax.axis_index("x") + 2 * lax.axis_index("y") + lax.axis_index("c")
    big_names = [n for n, _, _ in BIG]
    slab_shapes = [_slab_shape(sh, ax) for _, sh, ax in BIG]
    small_names = [n for n, _ in REPL] + [n for n, _, _ in SMALL]

    mine = _pack_slabs([_to_slab(wsh[n], ax).astype(bf16) for n, _, ax in BIG], BIG_ROWS)
    wl = dict(zip(big_names, _unpack_slabs(_gather8(mine, name="gather_weights"), slab_shapes)))
    placed = []
    for n, shard, axis in SMALL:
        start = [0] * len(shard)
        start[axis] = me * shard[axis]
        placed.append(lax.dynamic_update_slice(jnp.zeros(_full_shape(shard, axis), f32), wsh[n], start))
    small_all = _all_sum(_pack_rows(placed, SMALL_FWD_ROWS), name="small_params")
    ws = dict(zip([n for n, _, _ in SMALL], _unpack_rows(small_all, [_full_shape(sh, ax) for _, sh, ax in SMALL])))
    ws.update({n: wsh[n] for n, _ in REPL})

    loss8, grad_x, (big_grads, grads) = _local_step(x[0], loss_target[0], wl, ws)

    contrib = _pack_slabs([big_grads[n] for n in big_names], BIG_ROWS, axis=1)
    chip_sum = _pair_sum(contrib, _pair_exchange(contrib, name="grads_pair_exchange"), name="grads_pair_sum")
    g_big = _sum_slabs(_cross_exchange(chip_sum, name="grads_cross_exchange"), name="grads_chip_sum")
    small_vec = _pack_rows([grads[n] for n, _ in REPL] + [grads[n] for n, _, _ in SMALL] + [loss8[0, :1]], SMALL_ROWS)
    small_sum = _all_sum(small_vec, name="small_grads")
    parts = _unpack_rows(small_sum, [sh for _, sh in REPL] + [_full_shape(sh, ax) for _, sh, ax in SMALL] + [(1,)])
    g = {n: val for (n, _), val in zip(REPL, parts)}
    for (n, shard, axis), val in zip(SMALL, parts[len(REPL):]):
        start = [0] * len(shard)
        start[axis] = me * shard[axis]
        g[n] = lax.dynamic_slice(val, start, shard)
    loss = parts[-1].reshape(())
    for (n, _, axis), val in zip(BIG, _unpack_slabs(g_big, slab_shapes)):
        g[n] = _to_slab(val, axis)

    delta, new_m, new_v = {}, {}, {}
    for n, shard, _ in BIG:
        flat = lambda a: a.reshape(-1, shard[-1])
        outs = _adamw(flat(wsh[n]), flat(g[n]), flat(msh[n]), flat(vsh[n]), name=f"adamw_{n}")
        delta[n], new_m[n], new_v[n] = [o.reshape(shard) for o in outs]
    small_state = [_pack_rows([src[n] for n in small_names], SMALL_ROWS) for src in (wsh, g, msh, vsh)]
    for dst, slab in zip((delta, new_m, new_v), _adamw(*small_state, name="adamw_small")):
        dst.update(zip(small_names, _unpack_rows(slab, [wsh[n].shape for n in small_names])))

    return (loss, grad_x[None], *[g[n] for n in WEIGHTS], *[delta[n] for n in WEIGHTS],
            *[new_m[n] for n in WEIGHTS], *[new_v[n] for n in WEIGHTS])
```

```python
import math

import numpy as np
import jax
import jax.numpy as jnp
from jax import lax
from jax.experimental import pallas as pl
from jax.experimental.pallas import tpu as pltpu

f32, bf16 = jnp.float32, jnp.bfloat16

N_DEV = 8
D = 1024
HEADS = 8
NOPE, ROPE, QK = 64, 32, 96
HEAD_PAD = 128
LORA = 256
CONV_CH = 512
MIX_IN_PAD = 2176
FFN_H = 2816
GROUPS, GROUP, STATE = 64, 16, 64
EPS = 1e-6
ROPE_THETA = 10000.0
ADAM_LR, ADAM_B1, ADAM_B2, ADAM_EPS, ADAM_WD, ADAM_STEP = 0.001, 0.9, 0.999, 1e-08, 0.01, 10
LANES = 128
PAIR_LANES = 2 * LANES
VMEM_LIMIT = 56 << 20
MM_VMEM_BUDGET = 40 << 20
NEG = -1e30

BIG = (
    ("ffn_w_up", (4, 1024, 704), 2), ("ffn_w_down", (4, 352, 1024), 1), ("w_glu", (2, 1024, 256), 2),
    ("mix_w_out", (2, 128, 1024), 1), ("ssm_w_in", (2, 128, 1024), 1), ("w_ukv", (2, 256, 128), 2),
    ("w_uq", (2, 256, 96), 2), ("mix_w_in", (2, 1024, 260), 2))
REPL = (("attn_norm", (2, 1024)), ("cq_norm", (2, 256)), ("ckv_norm", (2, 256)), ("q_gain", (2, 96)),
        ("k_gain", (2, 96)), ("lambda_re", (2, 64, 64)), ("lambda_im", (2, 64, 64)), ("log_step", (2, 64)),
        ("b_re", (2, 64, 64, 16)), ("b_im", (2, 64, 64, 16)), ("c_re", (2, 64, 16, 64)), ("c_im", (2, 64, 16, 64)),
        ("ffn_norm", (4, 1024)))
SMALL = (("sconv_w", (2, 3, 64), 2), ("ssm_norm", (2, 128), 1), ("d_skip", (2, 128), 1), ("ffn_conv_w", (4, 3, 704), 2))
WEIGHTS = ['attn_norm', 'mix_w_in', 'cq_norm', 'ckv_norm', 'w_uq', 'w_ukv', 'q_gain', 'k_gain', 'sconv_w', 'mix_w_out',
           'ssm_norm', 'ssm_w_in', 'lambda_re', 'lambda_im', 'log_step', 'b_re', 'b_im', 'c_re', 'c_im', 'd_skip',
           'w_glu', 'ffn_norm', 'ffn_w_up', 'ffn_conv_w', 'ffn_w_down']
BIG_ROWS = 5888
SMALL_FWD_ROWS = 80
SMALL_ROWS = 640


def _cparams(sem=None, **kw):
    return pltpu.CompilerParams(dimension_semantics=sem, vmem_limit_bytes=VMEM_LIMIT, **kw)


def _tile(n, target):
    best = 0
    for t in range(LANES, min(n, target) + 1, LANES):
        if n % t == 0:
            best = t
    return best if best else n


def _mm(a, b, *, ta=False, tb=False, out_dtype=f32, add=None, twin=False, name, tm=1024, tn=1536):
    m, k = (a.shape[1], a.shape[0]) if ta else a.shape
    n = b.shape[0] if tb else b.shape[1]
    assert (b.shape[1] if tb else b.shape[0]) == k
    tm = _tile(m, tm)
    tn_ = _tile(n, tn)
    tn = n if (tn_ < 256 and n <= 2304) else tn_

    def vmem_bytes(t):
        io = 2 * (tm * t * a.dtype.itemsize + t * tn * b.dtype.itemsize + tm * tn * (jnp.dtype(out_dtype).itemsize + 2 * twin))
        return io + (2 * tm * tn * 4 if add is not None else 0) + (tm * tn * 4 if t < k else 0)

    tk = next((t for t in [k] + [t for t in range(k - LANES, 0, -LANES) if k % t == 0] if vmem_bytes(t) <= MM_VMEM_BUDGET), LANES)
    nk = k // tk
    dn = (((0 if ta else 1,), (1 if tb else 0,)), ((), ()))

    def body(*refs):
        a_ref, b_ref = refs[:2]
        add_ref = refs[2] if add is not None else None
        o_ref = refs[3] if add is not None else refs[2]
        twin_ref = refs[4 if add is not None else 3] if twin else None
        part = lax.dot_general(a_ref[...].astype(bf16), b_ref[...].astype(bf16), dn, preferred_element_type=f32)

        def finish(r):
            if add is not None:
                r = r + add_ref[...].astype(f32)
            o_ref[...] = r.astype(out_dtype)
            if twin:
                twin_ref[...] = r.astype(bf16)

        if nk == 1:
            finish(part)
            return
        acc = refs[-1]
        kk = pl.program_id(2)

        @pl.when(kk == 0)
        def _():
            acc[...] = part

        @pl.when(kk > 0)
        def _():
            acc[...] += part

        @pl.when(kk == nk - 1)
        def _():
            finish(acc[...])

    a_spec = pl.BlockSpec((tk, tm), lambda i, j, kk: (kk, i)) if ta else pl.BlockSpec((tm, tk), lambda i, j, kk: (i, kk))
    b_spec = pl.BlockSpec((tn, tk), lambda i, j, kk: (j, kk)) if tb else pl.BlockSpec((tk, tn), lambda i, j, kk: (kk, j))
    in_specs, args = [a_spec, b_spec], [a, b]
    if add is not None:
        in_specs.append(pl.BlockSpec((tm, tn), lambda i, j, kk: (i, j)))
        args.append(add)
    o_spec, o_shape = pl.BlockSpec((tm, tn), lambda i, j, kk: (i, j)), jax.ShapeDtypeStruct((m, n), out_dtype)
    return pl.pallas_call(
        body, name=name, grid=(m // tm, n // tn, nk), in_specs=in_specs,
        out_specs=[o_spec, o_spec] if twin else o_spec,
        out_shape=[o_shape, jax.ShapeDtypeStruct((m, n), bf16)] if twin else o_shape,
        scratch_shapes=[pltpu.VMEM((tm, tn), f32)] if nk > 1 else [],
        compiler_params=_cparams(("parallel", "parallel", "arbitrary")))(*args)


def _bd_nn(a, w, *, out_dtype=f32, add=None, name, ts=512):
    s = a.shape[0]
    nb, ka, no = w.shape
    ts = min(ts, s)

    def body(a_ref, w_ref, *rest):
        r = jnp.dot(a_ref[...].astype(bf16), w_ref[0].astype(bf16), preferred_element_type=f32)
        if add is not None:
            r = r + rest[0][...].astype(f32)
        rest[-1][...] = r.astype(out_dtype)

    o_spec = pl.BlockSpec((ts, no), lambda b, i: (i, b))
    return pl.pallas_call(
        body, name=name, grid=(nb, s // ts),
        in_specs=[pl.BlockSpec((ts, ka), lambda b, i: (i, b)), pl.BlockSpec((1, ka, no), lambda b, i: (b, 0, 0))]
        + ([o_spec] if add is not None else []),
        out_specs=o_spec, out_shape=jax.ShapeDtypeStruct((s, nb * no), out_dtype),
        compiler_params=_cparams(("parallel", "parallel")))(a, w, *([add] if add is not None else []))


def _bd_tn_diag(a, g, *, name, ts=512):
    s = a.shape[0]
    nb = a.shape[1] // LANES
    ts = min(ts, s)
    ni = s // ts

    def body(a_ref, g_ref, o_ref, acc):
        i = pl.program_id(1)
        part = lax.dot_general(a_ref[...].astype(bf16), g_ref[...].astype(bf16), (((0,), (0,)), ((), ())),
                               preferred_element_type=f32)

        @pl.when(i == 0)
        def _():
            acc[...] = part

        @pl.when(i > 0)
        def _():
            acc[...] += part

        @pl.when(i == ni - 1)
        def _():
            for j in range(8):
                o_ref[0, j] = acc[j * GROUP:(j + 1) * GROUP, (j // 2) * PAIR_LANES:(j // 2 + 1) * PAIR_LANES]

    return pl.pallas_call(
        body, name=name, grid=(nb, ni),
        in_specs=[pl.BlockSpec((ts, LANES), lambda b, i: (i, b)), pl.BlockSpec((ts, 8 * LANES), lambda b, i: (i, b))],
        out_specs=pl.BlockSpec((1, 8, GROUP, PAIR_LANES), lambda b, i: (b, 0, 0, 0)),
        out_shape=jax.ShapeDtypeStruct((nb, 8, GROUP, PAIR_LANES), f32),
        scratch_shapes=[pltpu.VMEM((LANES, 8 * LANES), f32)],
        compiler_params=_cparams(("parallel", "arbitrary")))(a, g)


def _rms_fwd(x, g, *, col=0, name, ts=512):
    s, d = x.shape[0], g.shape[1]
    ts = min(ts, s)

    def body(x_ref, g_ref, o_ref):
        xv = x_ref[...].astype(f32)
        r = lax.rsqrt(jnp.mean(xv * xv, axis=-1, keepdims=True) + EPS)
        o_ref[...] = (xv * r * g_ref[...]).astype(bf16)

    return pl.pallas_call(
        body, name=name, grid=(s // ts,),
        in_specs=[pl.BlockSpec((ts, d), lambda i: (i, col)), pl.BlockSpec((1, d), lambda i: (0, 0))],
        out_specs=pl.BlockSpec((ts, d), lambda i: (i, 0)),
        out_shape=jax.ShapeDtypeStruct((s, d), bf16),
        compiler_params=_cparams(("parallel",)))(x, g)


def _rms_bwd(dy, x, g, *, col=0, res=None, out_dtype=f32, name, ts=512):
    s, d = dy.shape
    ts = min(ts, s)
    twin = res is not None

    def body(*refs):
        if twin:
            dy_ref, x_ref, g_ref, res_ref, dx_ref, dxb_ref, dg_ref = refs
        else:
            dy_ref, x_ref, g_ref, dx_ref, dg_ref = refs

        @pl.when(pl.program_id(0) == 0)
        def _():
            dg_ref[...] = jnp.zeros_like(dg_ref)

        xv, dyv = x_ref[...].astype(f32), dy_ref[...].astype(f32)
        r = lax.rsqrt(jnp.mean(xv * xv, axis=-1, keepdims=True) + EPS)
        dyg = dyv * g_ref[...]
        dx = r * dyg - xv * (r * r * r) * jnp.mean(xv * dyg, axis=-1, keepdims=True)
        if twin:
            dx = dx + res_ref[...]
            dxb_ref[...] = dx.astype(bf16)
        dx_ref[...] = dx.astype(out_dtype)
        dg_ref[...] += jnp.sum(dyv * xv * r, axis=0, keepdims=True)

    row, vec = pl.BlockSpec((ts, d), lambda i: (i, 0)), pl.BlockSpec((1, d), lambda i: (0, 0))
    in_specs, args = [row, pl.BlockSpec((ts, d), lambda i: (i, col)), vec], [dy, x, g]
    out_specs, out_shape = [row], [jax.ShapeDtypeStruct((s, d), out_dtype)]
    if twin:
        in_specs.append(row)
        args.append(res)
        out_specs.append(row)
        out_shape.append(jax.ShapeDtypeStruct((s, d), bf16))
    return pl.pallas_call(
        body, name=name, grid=(s // ts,), in_specs=in_specs, out_specs=out_specs + [vec],
        out_shape=out_shape + [jax.ShapeDtypeStruct((1, d), f32)],
        compiler_params=_cparams(("arbitrary",)))(*args)


def _swap_halves(z):
    lane = lax.broadcasted_iota(jnp.int32, z.shape, 1)
    return jnp.where(lane < NOPE + ROPE // 2, pltpu.roll(z, LANES - ROPE // 2, axis=1), pltpu.roll(z, ROPE // 2, axis=1))


def _rope_tables(s):
    inv_freq = 1.0 / (ROPE_THETA ** (jnp.arange(0, ROPE, 2, dtype=f32) / ROPE))
    ang = jnp.arange(s, dtype=f32)[:, None] * inv_freq[None, :]
    cos, sin = jnp.cos(ang), jnp.sin(ang)
    one, zero = jnp.ones((s, NOPE), f32), jnp.zeros((s, NOPE), f32)
    pad1, pad0 = jnp.ones((s, HEAD_PAD - QK), f32), jnp.zeros((s, HEAD_PAD - QK), f32)
    return jnp.concatenate([one, cos, cos, pad1], 1), jnp.concatenate([zero, -sin, sin, pad0], 1)


def _qk_prep_fwd(q_raw, kv_raw, proj, qg, kg, cos_t, sin_t, *, name, ts=1024):
    s = q_raw.shape[0]
    ts = min(ts, s)
    rope_blk = (MIX_IN_PAD - HEAD_PAD) // HEAD_PAD

    def body(q_ref, kv_ref, kr_ref, qg_ref, kg_ref, c_ref, s_ref, qo_ref, ko_ref, vo_ref):
        lane = lax.broadcasted_iota(jnp.int32, (ts, HEAD_PAD), 1)
        cosv, sinv = c_ref[...], s_ref[...]

        def norm_rope(z, gain):
            r = lax.rsqrt(jnp.sum(z * z, axis=-1, keepdims=True) * (1.0 / QK) + EPS)
            zn = z * r * gain
            return zn * cosv + _swap_halves(zn) * sinv

        kvv = kv_ref[...]
        qo_ref[...] = (norm_rope(q_ref[...], qg_ref[...]) * _Q_FOLD).astype(bf16)
        ko_ref[...] = norm_rope(jnp.where(lane < NOPE, kvv, kr_ref[...]), kg_ref[...]).astype(bf16)
        vo_ref[...] = jnp.where(lane >= NOPE, kvv, 0.0).astype(bf16)

    head = pl.BlockSpec((ts, HEAD_PAD), lambda i, h: (i, h))
    row = pl.BlockSpec((ts, HEAD_PAD), lambda i, h: (i, 0))
    vec = pl.BlockSpec((1, HEAD_PAD), lambda i, h: (0, 0))
    out = jax.ShapeDtypeStruct((s, HEADS * HEAD_PAD), bf16)
    return pl.pallas_call(
        body, name=name, grid=(s // ts, HEADS),
        in_specs=[head, head, pl.BlockSpec((ts, HEAD_PAD), lambda i, h: (i, rope_blk)), vec, vec, row, row],
        out_specs=[head, head, head], out_shape=[out, out, out],
        compiler_params=_cparams(("parallel", "parallel")))(q_raw, kv_raw, proj, qg, kg, cos_t, sin_t)


def _qk_prep_bwd(dq, dk, dv, q_raw, kv_raw, proj, qg, kg, cos_t, sin_t, *, name, ts=1024):
    s = q_raw.shape[0]
    ts = min(ts, s)
    rope_blk = (MIX_IN_PAD - HEAD_PAD) // HEAD_PAD

    def body(dq_ref, dk_ref, dv_ref, q_ref, kv_ref, kr_ref, qg_ref, kg_ref, c_ref, s_ref,
             dqr_ref, dkvr_ref, dkr_ref, dqg_ref, dkg_ref):
        i, h = pl.program_id(0), pl.program_id(1)
        lane = lax.broadcasted_iota(jnp.int32, (ts, HEAD_PAD), 1)
        is_rope = (lane >= NOPE) & (lane < QK)
        cosv, sinv = c_ref[...], s_ref[...]

        @pl.when((i == 0) & (h == 0))
        def _():
            dqg_ref[...] = jnp.zeros_like(dqg_ref)
            dkg_ref[...] = jnp.zeros_like(dkg_ref)

        @pl.when(h == 0)
        def _():
            dkr_ref[...] = jnp.zeros_like(dkr_ref)

        def back(dout, z, gain):
            dzn = dout * cosv + jnp.where(is_rope, _swap_halves(dout * sinv), 0.0)
            r = lax.rsqrt(jnp.sum(z * z, axis=-1, keepdims=True) * (1.0 / QK) + EPS)
            dzg = dzn * gain
            dz = r * dzg - z * (r * r * r) * (jnp.sum(z * dzg, axis=-1, keepdims=True) * (1.0 / QK))
            return dz, jnp.sum(dzn * z * r, axis=0, keepdims=True)

        dqz, dqg = back(dq_ref[...].astype(f32), q_ref[...], qg_ref[...])
        dqr_ref[...] = dqz.astype(bf16)
        dqg_ref[...] += dqg
        kvv = kv_ref[...]
        dkz, dkg = back(dk_ref[...].astype(f32), jnp.where(lane < NOPE, kvv, kr_ref[...]), kg_ref[...])
        dkg_ref[...] += dkg
        dkvr_ref[...] = jnp.where(lane < NOPE, dkz, dv_ref[...].astype(f32)).astype(bf16)
        dkr_ref[...] += jnp.where(is_rope, dkz, 0.0)

    head = pl.BlockSpec((ts, HEAD_PAD), lambda i, h: (i, h))
    row = pl.BlockSpec((ts, HEAD_PAD), lambda i, h: (i, 0))
    vec = pl.BlockSpec((1, HEAD_PAD), lambda i, h: (0, 0))
    wide = jax.ShapeDtypeStruct((s, HEADS * HEAD_PAD), bf16)
    return pl.pallas_call(
        body, name=name, grid=(s // ts, HEADS),
        in_specs=[head, head, head, head, head, pl.BlockSpec((ts, HEAD_PAD), lambda i, h: (i, rope_blk)), vec, vec, row, row],
        out_specs=[head, head, row, vec, vec],
        out_shape=[wide, wide, jax.ShapeDtypeStruct((s, HEAD_PAD), f32), jax.ShapeDtypeStruct((1, HEAD_PAD), f32),
                   jax.ShapeDtypeStruct((1, HEAD_PAD), f32)],
        compiler_params=_cparams(("arbitrary", "arbitrary")))(dq, dk, dv, q_raw, kv_raw, proj, qg, kg, cos_t, sin_t)


_NT = (((1,), (1,)), ((), ()))
_SCALE = QK ** -0.5
_LOG2E = math.log2(math.e)
_Q_FOLD = _SCALE * _LOG2E
FLASH_TILE = 1024


def _flash_fwd(q, k, v, *, name, tq=FLASH_TILE):
    s = q.shape[0]
    tq = min(tq, s)

    def body(q_ref, k_ref, v_ref, o_ref, lse_ref):
        i = pl.program_id(1)
        qv = q_ref[...]

        def step(j, carry, masked):
            m, l, acc = carry
            st = pl.multiple_of(j * tq, tq)
            kj, vj = k_ref[pl.ds(st, tq), :], v_ref[pl.ds(st, tq), :]
            sc = lax.dot_general(qv, kj, _NT, preferred_element_type=f32)
            if masked:
                rr = lax.broadcasted_iota(jnp.int32, (tq, tq), 0)
                cc = lax.broadcasted_iota(jnp.int32, (tq, tq), 1)
                sc = jnp.where(cc <= rr, sc, NEG)
            m_new = jnp.maximum(m, jnp.max(sc, axis=-1, keepdims=True))
            p = jnp.exp2(sc - m_new)
            alpha = jnp.exp2(m - m_new)
            l = alpha * l + jnp.sum(p, axis=-1, keepdims=True)
            acc = alpha * acc + jnp.dot(p.astype(bf16), vj, preferred_element_type=f32)
            return m_new, l, acc

        init = (jnp.full((tq, 1), NEG, f32), jnp.zeros((tq, 1), f32), jnp.zeros((tq, HEAD_PAD), f32))
        carry = lax.fori_loop(0, i, lambda j, c: step(j, c, False), init)
        m, l, acc = step(i, carry, True)
        o_ref[...] = (acc / l).astype(bf16)
        lse_ref[0] = m + jnp.log2(l)

    blk = pl.BlockSpec((tq, HEAD_PAD), lambda h, i: (i, h))
    full = pl.BlockSpec((s, HEAD_PAD), lambda h, i: (0, h))
    return pl.pallas_call(
        body, name=name, grid=(HEADS, s // tq), in_specs=[blk, full, full],
        out_specs=[blk, pl.BlockSpec((1, tq, 1), lambda h, i: (h, i, 0))],
        out_shape=[jax.ShapeDtypeStruct((s, HEADS * HEAD_PAD), bf16), jax.ShapeDtypeStruct((HEADS, s, 1), f32)],
        compiler_params=_cparams(("parallel", "arbitrary")))(q, k, v)


def _flash_bwd_dq(q, k, v, o, do, lse, *, name, tq=FLASH_TILE):
    s = q.shape[0]
    tq = min(tq, s)

    def body(q_ref, k_ref, v_ref, o_ref, do_ref, lse_ref, dq_ref, dl_ref):
        i = pl.program_id(1)
        qv = q_ref[...]
        dov = do_ref[...].astype(f32)
        delta = jnp.sum(dov * o_ref[...].astype(f32), axis=-1, keepdims=True)
        dob = dov.astype(bf16)
        lsev = lse_ref[0]

        def step(j, acc, masked):
            st = pl.multiple_of(j * tq, tq)
            kj, vj = k_ref[pl.ds(st, tq), :], v_ref[pl.ds(st, tq), :]
            sc = lax.dot_general(qv, kj, _NT, preferred_element_type=f32)
            p = jnp.exp2(sc - lsev)
            if masked:
                rr = lax.broadcasted_iota(jnp.int32, (tq, tq), 0)
                cc = lax.broadcasted_iota(jnp.int32, (tq, tq), 1)
                p = jnp.where(cc <= rr, p, 0.0)
            dp = lax.dot_general(dob, vj, _NT, preferred_element_type=f32)
            ds = p * (dp - delta)
            return acc + jnp.dot(ds.astype(bf16), kj, preferred_element_type=f32)

        acc = lax.fori_loop(0, i, lambda j, c: step(j, c, False), jnp.zeros((tq, HEAD_PAD), f32))
        dq_ref[...] = step(i, acc, True) * _SCALE
        dl_ref[0] = delta

    blk = pl.BlockSpec((tq, HEAD_PAD), lambda h, i: (i, h))
    full = pl.BlockSpec((s, HEAD_PAD), lambda h, i: (0, h))
    col = pl.BlockSpec((1, tq, 1), lambda h, i: (h, i, 0))
    return pl.pallas_call(
        body, name=name, grid=(HEADS, s // tq), in_specs=[blk, full, full, blk, blk, col],
        out_specs=[blk, col],
        out_shape=[jax.ShapeDtypeStruct((s, HEADS * HEAD_PAD), f32), jax.ShapeDtypeStruct((HEADS, s, 1), f32)],
        compiler_params=_cparams(("parallel", "arbitrary")))(q, k, v, o, do, lse)


def _flash_bwd_dkv(q, k, v, do, lse_row, delta_row, *, name, tk=FLASH_TILE):
    s = q.shape[0]
    tk = min(tk, s)
    nblk = s // tk

    def body(q_ref, k_ref, v_ref, do_ref, lse_ref, dl_ref, dk_ref, dv_ref):
        j = pl.program_id(1)
        kv_, vv = k_ref[...], v_ref[...]

        def step(i, carry, masked):
            dk, dv = carry
            st = pl.multiple_of(i * tk, tk)
            qi = q_ref[pl.ds(st, tk), :]
            doi = do_ref[pl.ds(st, tk), :].astype(bf16)
            lse_i = lse_ref[0, :, pl.ds(st, tk)]
            dl_i = dl_ref[0, :, pl.ds(st, tk)]
            st_ = lax.dot_general(kv_, qi, _NT, preferred_element_type=f32)
            pt = jnp.exp2(st_ - lse_i)
            if masked:
                kk = lax.broadcasted_iota(jnp.int32, (tk, tk), 0)
                qq = lax.broadcasted_iota(jnp.int32, (tk, tk), 1)
                pt = jnp.where(kk <= qq, pt, 0.0)
            dv = dv + jnp.dot(pt.astype(bf16), doi, preferred_element_type=f32)
            dpt = lax.dot_general(vv, doi, _NT, preferred_element_type=f32)
            dst = pt * (dpt - dl_i)
            dk = dk + jnp.dot(dst.astype(bf16), qi, preferred_element_type=f32)
            return dk, dv

        zero = jnp.zeros((tk, HEAD_PAD), f32)
        carry = step(j, (zero, zero), True)
        dk, dv = lax.fori_loop(j + 1, nblk, lambda i, c: step(i, c, False), carry)
        dk_ref[...] = dk * (1.0 / _LOG2E)
        dv_ref[...] = dv

    blk = pl.BlockSpec((tk, HEAD_PAD), lambda h, j: (j, h))
    full = pl.BlockSpec((s, HEAD_PAD), lambda h, j: (0, h))
    rowv = pl.BlockSpec((1, 1, s), lambda h, j: (h, 0, 0))
    out = jax.ShapeDtypeStruct((s, HEADS * HEAD_PAD), f32)
    return pl.pallas_call(
        body, name=name, grid=(HEADS, nblk), in_specs=[full, blk, blk, full, rowv, rowv],
        out_specs=[blk, blk], out_shape=[out, out],
        compiler_params=_cparams(("parallel", "arbitrary")))(q, k, v, do, lse_row, delta_row)


SUBLANES = 8


def _shift_down(x, d):
    r = pltpu.roll(x, d, axis=0)
    t = lax.broadcasted_iota(jnp.int32, (SUBLANES, x.shape[1]), 0)
    head = jnp.where(t < d, 0.0, r[:SUBLANES])
    return head if x.shape[0] == SUBLANES else jnp.concatenate([head, r[SUBLANES:]], axis=0)


def _shift_up(x, d):
    s = x.shape[0]
    r = pltpu.roll(x, s - d, axis=0)
    t = lax.broadcasted_iota(jnp.int32, (SUBLANES, x.shape[1]), 0)
    tail = jnp.where(t >= SUBLANES - d, 0.0, r[s - SUBLANES:])
    return tail if s == SUBLANES else jnp.concatenate([r[:s - SUBLANES], tail], axis=0)


def _taps(w_ref):
    return w_ref[0:1, :], w_ref[1:2, :], w_ref[2:3, :]


def _conv3(u, w):
    u1, u2 = _shift_down(u, 1), _shift_down(u, 2)
    return w[0] * u2 + w[1] * u1 + w[2] * u, (u1, u2)


def _conv3_t(g, w):
    return w[2] * g + w[1] * _shift_up(g, 1) + w[0] * _shift_up(g, 2)


def _conv3_dw(dw_ref, g, u, shifted):
    dw_ref[0:1, :] = jnp.sum(g * shifted[1], axis=0, keepdims=True)
    dw_ref[1:2, :] = jnp.sum(g * shifted[0], axis=0, keepdims=True)
    dw_ref[2:3, :] = jnp.sum(g * u, axis=0, keepdims=True)


_GB, _GC, _CI = 512 // LANES, 1024 // LANES, 1536 // LANES


def _sconv_fwd(proj, w, *, name):
    s = proj.shape[0]

    def body(gb_ref, gc_ref, ci_ref, w_ref, o_ref):
        o_ref[...] = (gb_ref[...] * _conv3(gc_ref[...] * ci_ref[...], _taps(w_ref))[0]).astype(bf16)

    col = lambda off: pl.BlockSpec((s, LANES), lambda j: (0, off + j))
    return pl.pallas_call(
        body, name=name, grid=(CONV_CH // LANES,),
        in_specs=[col(_GB), col(_GC), col(_CI), pl.BlockSpec((3, LANES), lambda j: (0, j))],
        out_specs=pl.BlockSpec((s, LANES), lambda j: (0, j)),
        out_shape=jax.ShapeDtypeStruct((s, CONV_CH), bf16),
        compiler_params=_cparams(("parallel",)))(proj, proj, proj, w)


def _sconv_bwd(dmix, proj, w, *, name):
    s = proj.shape[0]

    def body(do_ref, gb_ref, gc_ref, ci_ref, w_ref, dgb_ref, dgc_ref, dci_ref, dw_ref):
        wv, gc, ci, do = _taps(w_ref), gc_ref[...], ci_ref[...], do_ref[...].astype(f32)
        u = gc * ci
        conv, shifted = _conv3(u, wv)
        dgb_ref[...] = (do * conv).astype(bf16)
        dc = do * gb_ref[...]
        du = _conv3_t(dc, wv)
        dgc_ref[...] = (du * ci).astype(bf16)
        dci_ref[...] = (du * gc).astype(bf16)
        _conv3_dw(dw_ref, dc, u, shifted)

    col = lambda off: pl.BlockSpec((s, LANES), lambda j: (0, off + j))
    out = jax.ShapeDtypeStruct((s, CONV_CH), bf16)
    return pl.pallas_call(
        body, name=name, grid=(CONV_CH // LANES,),
        in_specs=[col(HEADS), col(_GB), col(_GC), col(_CI), pl.BlockSpec((3, LANES), lambda j: (0, j))],
        out_specs=[col(0), col(0), col(0), pl.BlockSpec((3, LANES), lambda j: (0, j))],
        out_shape=[out, out, out, jax.ShapeDtypeStruct((3, CONV_CH), f32)],
        compiler_params=_cparams(("parallel",)))(dmix, proj, proj, proj, w)


def _ffn_act_fwd(zg, zv, cwg, cwv, *, name):
    s, f = zg.shape

    def body(zg_ref, zv_ref, wg_ref, wv_ref, o_ref):
        o_ref[...] = (jax.nn.silu(_conv3(zg_ref[...], _taps(wg_ref))[0]) * _conv3(zv_ref[...], _taps(wv_ref))[0]).astype(bf16)

    col = pl.BlockSpec((s, LANES), lambda j: (0, j))
    wsp = pl.BlockSpec((3, LANES), lambda j: (0, j))
    return pl.pallas_call(
        body, name=name, grid=(f // LANES,), in_specs=[col, col, wsp, wsp], out_specs=col,
        out_shape=jax.ShapeDtypeStruct((s, f), bf16), compiler_params=_cparams(("parallel",)))(zg, zv, cwg, cwv)


def _ffn_act_bwd(da, zg, zv, cwg, cwv, *, name):
    s, f = zg.shape

    def body(da_ref, zg_ref, zv_ref, wg_ref, wv_ref, dzg_ref, dzv_ref, dwg_ref, dwv_ref):
        wg, wv, zgv, zvv, dav = _taps(wg_ref), _taps(wv_ref), zg_ref[...], zv_ref[...], da_ref[...].astype(f32)
        (ug, zg_shifted), (uv, zv_shifted) = _conv3(zgv, wg), _conv3(zvv, wv)
        sg = jax.nn.sigmoid(ug)
        dug = dav * uv * (sg * (1.0 + ug * (1.0 - sg)))
        duv = dav * (ug * sg)
        dzg_ref[...] = _conv3_t(dug, wg).astype(bf16)
        dzv_ref[...] = _conv3_t(duv, wv).astype(bf16)
        _conv3_dw(dwg_ref, dug, zgv, zg_shifted)
        _conv3_dw(dwv_ref, duv, zvv, zv_shifted)

    col = pl.BlockSpec((s, LANES), lambda j: (0, j))
    wsp = pl.BlockSpec((3, LANES), lambda j: (0, j))
    act, wsh = jax.ShapeDtypeStruct((s, f), bf16), jax.ShapeDtypeStruct((3, f), f32)
    return pl.pallas_call(
        body, name=name, grid=(f // LANES,), in_specs=[col, col, col, wsp, wsp], out_specs=[col, col, wsp, wsp],
        out_shape=[act, act, wsh, wsh], compiler_params=_cparams(("parallel",)))(da, zg, zv, cwg, cwv)


def _expand_mat():
    return jnp.asarray(np.kron(np.eye(STATE, dtype=np.float32), np.ones((1, GROUP), np.float32)))


def _disc_fn(lr, li, ls, br, bi, e):
    dt = jnp.exp(ls)
    mag = jnp.exp(lr * dt)
    ar, ai = mag * jnp.cos(li * dt), mag * jnp.sin(li * dt)
    nr, ni = ar - 1.0, ai
    den = lr * lr + li * li
    zr, zi = (nr * lr + ni * li) / den, (ni * lr - nr * li) / den
    zrr = jnp.dot(zr, e, precision=lax.Precision.HIGHEST, preferred_element_type=f32)
    zir = jnp.dot(zi, e, precision=lax.Precision.HIGHEST, preferred_element_type=f32)
    return ar, ai, zrr * br - zir * bi, zrr * bi + zir * br


def _disc_fwd(lr, li, ls, br, bi, *, name):
    def body(lr_ref, li_ref, ls_ref, br_ref, bi_ref, e_ref, ar_ref, ai_ref, bbr_ref, bbi_ref):
        ar, ai, bbr, bbi = _disc_fn(lr_ref[...], li_ref[...], ls_ref[...], br_ref[...], bi_ref[...], e_ref[...])
        ar_ref[...], ai_ref[...], bbr_ref[...], bbi_ref[...] = ar, ai, bbr, bbi

    sq, wide = jax.ShapeDtypeStruct((GROUPS, STATE), f32), jax.ShapeDtypeStruct((GROUPS, STATE * GROUP), f32)
    return pl.pallas_call(body, name=name, out_shape=[sq, sq, wide, wide],
                          compiler_params=_cparams())(lr, li, ls, br, bi, _expand_mat())


def _disc_bwd(lr, li, ls, br, bi, dar, dai, dbbr, dbbi, *, name):
    def body(lr_ref, li_ref, ls_ref, br_ref, bi_ref, e_ref, dar_ref, dai_ref, dbbr_ref, dbbi_ref,
             dlr_ref, dli_ref, dls_ref, dbr_ref, dbi_ref):
        ev = e_ref[...]
        _, vjp = jax.vjp(lambda a, b, c, d_, e_: _disc_fn(a, b, c, d_, e_, ev),
                         lr_ref[...], li_ref[...], ls_ref[...], br_ref[...], bi_ref[...])
        dlr, dli, dls, dbr, dbi = vjp((dar_ref[...], dai_ref[...], dbbr_ref[...], dbbi_ref[...]))
        dlr_ref[...], dli_ref[...], dls_ref[...], dbr_ref[...], dbi_ref[...] = dlr, dli, dls, dbr, dbi

    sq, wide = jax.ShapeDtypeStruct((GROUPS, STATE), f32), jax.ShapeDtypeStruct((GROUPS, STATE * GROUP), f32)
    return pl.pallas_call(body, name=name, out_shape=[sq, sq, jax.ShapeDtypeStruct((GROUPS, 1), f32), wide, wide],
                          compiler_params=_cparams())(lr, li, ls, br, bi, _expand_mat(), dar, dai, dbbr, dbbi)


SCAN_TILE = 64
SCAN_PAIRS = 2


def _tile_shift(v, d, reverse):
    if d % 8:
        return _shift_up(v, d) if reverse else _shift_down(v, d)
    z = jnp.zeros((d, v.shape[1]), v.dtype)
    return jnp.concatenate([v[d:], z], axis=0) if reverse else jnp.concatenate([z, v[:v.shape[0] - d]], axis=0)


def _tile_scan(r, i, pows, reverse):
    d = 1
    for br, bi in pows:
        rs, is_ = _tile_shift(r, d, reverse), _tile_shift(i, d, reverse)
        r, i = r + br * rs - bi * is_, i + br * is_ + bi * rs
        d *= 2
    return r, i


def _scan_setup(ar, ai, reverse):
    if reverse:
        ai = -ai
    pows, br, bi, d = [], ar, ai, 1
    while d < SCAN_TILE:
        pows.append((br, bi))
        br, bi, d = br * br - bi * bi, 2.0 * br * bi, 2 * d
    row = lax.broadcasted_iota(jnp.int32, (SCAN_TILE, LANES), 0)
    hit = row == (SCAN_TILE - 1 if reverse else 0)
    pr, pi = _tile_scan(jnp.where(hit, ar, 0.0), jnp.where(hit, ai, 0.0), pows, reverse)
    return pows, pr, pi


def _carry_in(r, i, pr, pi, cr, ci):
    crb, cib = jnp.broadcast_to(cr, r.shape), jnp.broadcast_to(ci, i.shape)
    return r + pr * crb - pi * cib, i + pr * cib + pi * crb


def _pair_cols(q):
    return slice(q * PAIR_LANES, q * PAIR_LANES + LANES), slice(q * PAIR_LANES + LANES, (q + 1) * PAIR_LANES)


_SCAN_W = SCAN_PAIRS * PAIR_LANES


def _scan_specs(s, w):
    per = w.shape[2] // _SCAN_W
    src = pl.BlockSpec((s, LANES), lambda g: (0, g // per))
    mat = pl.BlockSpec((1, LANES, _SCAN_W), lambda g: (g // per, 0, g % per))
    col = pl.BlockSpec((s, _SCAN_W), lambda g: (0, g))
    vec = pl.BlockSpec((SCAN_PAIRS, 1, LANES), lambda g: (g, 0, 0))
    return src, mat, col, vec, (w.shape[0] * per,)


def _scan_fwd(u, wb, ar, ai, *, name):
    s = u.shape[0]
    nt = s // SCAN_TILE

    def body(u_ref, w_ref, ar_ref, ai_ref, x_ref):
        setups = [_scan_setup(ar_ref[q], ai_ref[q], False) for q in range(SCAN_PAIRS)]
        wv = w_ref[0]

        def tile_rows(k):
            return pl.ds(pl.multiple_of(k * SCAN_TILE, SCAN_TILE), SCAN_TILE)

        def tile_in(k):
            return jnp.dot(u_ref[tile_rows(k), :].astype(bf16), wv, preferred_element_type=f32)

        def step(k, carry):
            rows, bu = tile_rows(k), carry[-1]
            ahead = tile_in(jnp.minimum(k + 1, nt - 1))
            out = []
            for q, (pows, pr, pi) in enumerate(setups):
                rc, ic = _pair_cols(q)
                r, i = _tile_scan(bu[:, rc], bu[:, ic], pows, False)
                r, i = _carry_in(r, i, pr, pi, carry[2 * q], carry[2 * q + 1])
                x_ref[rows, rc] = r.astype(bf16)
                x_ref[rows, ic] = i.astype(bf16)
                out += [r[SCAN_TILE - 1:SCAN_TILE, :], i[SCAN_TILE - 1:SCAN_TILE, :]]
            return tuple(out) + (ahead,)

        lax.fori_loop(0, nt, step, tuple(jnp.zeros((1, LANES), f32) for _ in range(2 * SCAN_PAIRS)) + (tile_in(0),))

    src, mat, col, vec, grid = _scan_specs(s, wb)
    return pl.pallas_call(body, name=name, grid=grid, in_specs=[src, mat, vec, vec], out_specs=col,
                          out_shape=jax.ShapeDtypeStruct((s, wb.shape[0] * wb.shape[2]), bf16),
                          compiler_params=_cparams(("parallel",)))(u, wb, ar, ai)


def _scan_bwd(dy, cbt, x, ar, ai, *, name):
    s = dy.shape[0]
    nt = s // SCAN_TILE

    def fold(v):
        out = v[0:8]
        for r in range(8, SCAN_TILE, 8):
            out = out + v[r:r + 8]
        return out

    def body(dy_ref, w_ref, x_ref, ar_ref, ai_ref, g_ref, dar_ref, dai_ref):
        setups = [_scan_setup(ar_ref[q], ai_ref[q], True) for q in range(SCAN_PAIRS)]
        row = lax.broadcasted_iota(jnp.int32, (SCAN_TILE, LANES), 0)
        wv = w_ref[0]

        def tile_in(k):
            return jnp.dot(dy_ref[pl.ds(pl.multiple_of(k * SCAN_TILE, SCAN_TILE), SCAN_TILE), :], wv, preferred_element_type=f32)

        def step(kk, carry):
            k = nt - 1 - kk
            start = pl.multiple_of(k * SCAN_TILE, SCAN_TILE)
            rows = pl.ds(start, SCAN_TILE)
            prev16 = pl.ds(pl.multiple_of(jnp.maximum(start - 16, 0), 16), 16)
            dx = carry[-1]
            ahead = tile_in(jnp.maximum(k - 1, 0))

            def before(cols):
                first = jnp.where(k > 0, x_ref[prev16, cols][15:16, :].astype(f32), 0.0)
                return jnp.where(row == 0, first, pltpu.roll(x_ref[rows, cols].astype(f32), 1, axis=0))

            out = []
            for q, (pows, pr, pi) in enumerate(setups):
                rc, ic = _pair_cols(q)
                cr, ci, acc_r, acc_i = carry[4 * q:4 * q + 4]
                gr, gi = _tile_scan(dx[:, rc], dx[:, ic], pows, True)
                gr, gi = _carry_in(gr, gi, pr, pi, cr, ci)
                g_ref[rows, rc] = gr.astype(bf16)
                g_ref[rows, ic] = gi.astype(bf16)
                xr, xi = before(rc), before(ic)
                out += [gr[0:1, :], gi[0:1, :], acc_r + fold(gr * xr + gi * xi), acc_i + fold(gi * xr - gr * xi)]
            return tuple(out) + (ahead,)

        init = (jnp.zeros((1, LANES), f32), jnp.zeros((1, LANES), f32), jnp.zeros((8, LANES), f32), jnp.zeros((8, LANES), f32))
        res = lax.fori_loop(0, nt, step, init * SCAN_PAIRS + (tile_in(nt - 1),))
        for q in range(SCAN_PAIRS):
            dar_ref[q] = jnp.sum(res[4 * q + 2], axis=0, keepdims=True)
            dai_ref[q] = jnp.sum(res[4 * q + 3], axis=0, keepdims=True)

    src, mat, col, vec, grid = _scan_specs(s, cbt)
    vsh = jax.ShapeDtypeStruct((GROUPS // 2, 1, LANES), f32)
    return pl.pallas_call(body, name=name, grid=grid, in_specs=[src, mat, col, vec, vec],
                          out_specs=[col, vec, vec], out_shape=[jax.ShapeDtypeStruct(x.shape, bf16), vsh, vsh],
                          compiler_params=_cparams(("parallel",)))(dy, cbt, x, ar, ai)


_GELU_C = math.sqrt(2.0 / math.pi)


def _gelu_fwd(y, u, dsk, *, name, ts=512):
    s, d = y.shape
    ts = min(ts, s)

    def body(y_ref, u_ref, d_ref, o_ref):
        o_ref[...] = jax.nn.gelu(y_ref[...] + d_ref[...] * u_ref[...]).astype(bf16)

    row, vec = pl.BlockSpec((ts, d), lambda i: (i, 0)), pl.BlockSpec((1, d), lambda i: (0, 0))
    return pl.pallas_call(body, name=name, grid=(s // ts,), in_specs=[row, row, vec], out_specs=row,
                          out_shape=jax.ShapeDtypeStruct((s, d), bf16), compiler_params=_cparams(("parallel",)))(y, u, dsk)


def _gelu_bwd(dg, y, u, dsk, *, name, ts=512):
    s, d = y.shape
    ts = min(ts, s)

    def body(dg_ref, y_ref, u_ref, d_ref, dy_ref, du_ref, dd_ref):
        @pl.when(pl.program_id(0) == 0)
        def _():
            dd_ref[...] = jnp.zeros_like(dd_ref)

        uv, dv = u_ref[...], d_ref[...]
        z = y_ref[...] + dv * uv
        th = jnp.tanh(_GELU_C * (z + 0.044715 * z * z * z))
        dz = dg_ref[...] * (0.5 * (1.0 + th) + 0.5 * z * (1.0 - th * th) * _GELU_C * (1.0 + 3 * 0.044715 * z * z))
        dy_ref[...] = dz.astype(bf16)
        du_ref[...] = dz * dv
        dd_ref[...] += jnp.sum(dz * uv, axis=0, keepdims=True)

    row, vec = pl.BlockSpec((ts, d), lambda i: (i, 0)), pl.BlockSpec((1, d), lambda i: (0, 0))
    return pl.pallas_call(
        body, name=name, grid=(s // ts,), in_specs=[row, row, row, vec], out_specs=[row, row, vec],
        out_shape=[jax.ShapeDtypeStruct((s, d), bf16), jax.ShapeDtypeStruct((s, d), f32), jax.ShapeDtypeStruct((1, d), f32)],
        compiler_params=_cparams(("arbitrary",)))(dg, y, u, dsk)


def _glu_fwd(x, a, b, *, name, ts=512):
    s, d = x.shape
    ts = min(ts, s)

    def body(x_ref, a_ref, b_ref, o_ref):
        o_ref[...] = x_ref[...] + a_ref[...] * jax.nn.sigmoid(b_ref[...])

    row = pl.BlockSpec((ts, d), lambda i: (i, 0))
    return pl.pallas_call(body, name=name, grid=(s // ts,), in_specs=[row, row, row], out_specs=row,
                          out_shape=jax.ShapeDtypeStruct((s, d), f32), compiler_params=_cparams(("parallel",)))(x, a, b)


def _glu_bwd(dx, a, b, *, name, ts=512):
    s, d = dx.shape
    ts = min(ts, s)

    def body(dx_ref, a_ref, b_ref, da_ref, db_ref):
        sg = jax.nn.sigmoid(b_ref[...])
        dxv = dx_ref[...]
        da_ref[...] = (dxv * sg).astype(bf16)
        db_ref[...] = (dxv * a_ref[...] * sg * (1.0 - sg)).astype(bf16)

    row = pl.BlockSpec((ts, d), lambda i: (i, 0))
    out = jax.ShapeDtypeStruct((s, d), bf16)
    return pl.pallas_call(body, name=name, grid=(s // ts,), in_specs=[row, row, row], out_specs=[row, row],
                          out_shape=[out, out], compiler_params=_cparams(("parallel",)))(dx, a, b)


def _loss_head(y, target, *, name, ts=512):
    s, d = y.shape
    ts = min(ts, s)

    def body(y_ref, t_ref, dy_ref, dyb_ref, l_ref):
        @pl.when(pl.program_id(0) == 0)
        def _():
            l_ref[...] = jnp.zeros_like(l_ref)

        e = y_ref[...] - t_ref[...]
        dy = e * (1.0 / d)
        dy_ref[...] = dy
        dyb_ref[...] = dy.astype(bf16)
        l_ref[...] += 0.5 * jnp.sum(jnp.mean(e * e, axis=-1, keepdims=True))

    row = pl.BlockSpec((ts, d), lambda i: (i, 0))
    return pl.pallas_call(
        body, name=name, grid=(s // ts,), in_specs=[row, row],
        out_specs=[row, row, pl.BlockSpec((8, LANES), lambda i: (0, 0))],
        out_shape=[jax.ShapeDtypeStruct((s, d), f32), jax.ShapeDtypeStruct((s, d), bf16), jax.ShapeDtypeStruct((8, LANES), f32)],
        compiler_params=_cparams(("arbitrary",)))(y, target)


def _adamw(w, g, m, v, *, name, tr=128):
    r, c = w.shape

    def body(w_ref, g_ref, m_ref, v_ref, d_ref, mo_ref, vo_ref):
        gv = g_ref[...]
        mn = ADAM_B1 * m_ref[...] + (1.0 - ADAM_B1) * gv
        vn = ADAM_B2 * v_ref[...] + (1.0 - ADAM_B2) * (gv * gv)
        m_hat = mn / (1.0 - ADAM_B1 ** ADAM_STEP)
        v_hat = vn / (1.0 - ADAM_B2 ** ADAM_STEP)
        d_ref[...] = -ADAM_LR * (m_hat / (jnp.sqrt(v_hat) + ADAM_EPS) + ADAM_WD * w_ref[...])
        mo_ref[...] = mn
        vo_ref[...] = vn

    row = pl.BlockSpec((tr, c), lambda i: (i, 0))
    out = jax.ShapeDtypeStruct((r, c), f32)
    return pl.pallas_call(body, name=name, grid=(r // tr,), in_specs=[row] * 4, out_specs=[row] * 3,
                          out_shape=[out, out, out], compiler_params=_cparams(("parallel",)))(w, g, m, v)


def _sum_slabs(land, *, name, tr=128):
    n, r, c = land.shape

    def body(l_ref, o_ref):
        acc = l_ref[0].astype(f32)
        for i in range(1, n):
            acc = acc + l_ref[i].astype(f32)
        o_ref[...] = acc

    return pl.pallas_call(body, name=name, grid=(r // tr,), in_specs=[pl.BlockSpec((n, tr, c), lambda i: (0, i, 0))],
                          out_specs=pl.BlockSpec((tr, c), lambda i: (i, 0)), out_shape=jax.ShapeDtypeStruct((r, c), f32),
                          compiler_params=_cparams(("parallel",)))(land)


def _pair_sum(g, theirs, *, name, tr=256):
    n, r, c = theirs.shape

    def body(c_ref, g_ref, t_ref, o_ref):
        o_ref[...] = (g_ref[...].astype(f32) + t_ref[...].astype(f32)).astype(bf16)

    blk = pl.BlockSpec((1, tr, c), lambda j, i, c_ref: (j, i, 0))
    mine = pl.BlockSpec((1, tr, c), lambda j, i, c_ref: (2 * j + c_ref[0], i, 0))
    return pl.pallas_call(
        body, name=name,
        grid_spec=pltpu.PrefetchScalarGridSpec(num_scalar_prefetch=1, grid=(n, r // tr), in_specs=[mine, blk], out_specs=blk),
        out_shape=jax.ShapeDtypeStruct(theirs.shape, bf16),
        compiler_params=_cparams(("parallel", "parallel")))(lax.axis_index("c").astype(jnp.int32).reshape(1), g, theirs)


_MESH = pl.DeviceIdType.MESH
_HBM = pl.BlockSpec(memory_space=pltpu.HBM)
N_CHIP = N_DEV // 2


def _position():
    return lax.axis_index("x"), lax.axis_index("y"), lax.axis_index("c")


def _gather8(x, *, name):
    half = x.shape[0] // 2

    def body(x_ref, o_ref, send_sems, recv_sems, local_sem):
        xx, yy, cc = _position()
        me, sibling = (xx, yy, cc), (xx, yy, 1 - cc)
        here, xn, yn, dg = (xx, yy), (1 - xx, yy), (xx, 1 - yy), (1 - xx, 1 - yy)
        first, second = pl.ds(0, half), pl.ds(half, half)

        def slab(chip, pc, rows=None):
            ref = o_ref.at[4 * chip[0] + 2 * chip[1] + pc]
            return ref if rows is None else ref.at[rows]

        def copy(k, ref, to, src=None):
            return pltpu.make_async_remote_copy(src_ref=ref if src is None else src, dst_ref=ref, send_sem=send_sems.at[k],
                                                recv_sem=recv_sems.at[k], device_id=to, device_id_type=_MESH)

        mine = pltpu.make_async_copy(x_ref, slab(here, cc), local_sem)
        mine.start()
        sends = [copy(0, slab(here, cc), sibling, src=x_ref), copy(1, slab(here, cc), (*xn, cc), src=x_ref),
                 copy(2, slab(here, cc), (*yn, cc), src=x_ref)]
        for cp in sends:
            cp.start()
        copy(1, slab(xn, cc), me).wait_recv()
        sends += [copy(3, slab(xn, cc, first), (*yn, cc)), copy(5, slab(xn, cc), sibling)]
        copy(2, slab(yn, cc), me).wait_recv()
        sends += [copy(4, slab(yn, cc, second), (*xn, cc)), copy(6, slab(yn, cc), sibling)]
        for cp in sends[3:]:
            cp.start()
        copy(3, slab(dg, cc, first), me).wait_recv()
        copy(4, slab(dg, cc, second), me).wait_recv()
        sends.append(copy(7, slab(dg, cc), sibling))
        sends[-1].start()
        for k, chip in ((0, here), (5, xn), (6, yn), (7, dg)):
            copy(k, slab(chip, 1 - cc), me).wait_recv()
        for cp in sends:
            cp.wait_send()
        mine.wait()

    return pl.pallas_call(
        body, name=name, in_specs=[_HBM], out_specs=_HBM, out_shape=jax.ShapeDtypeStruct((N_DEV,) + x.shape, x.dtype),
        scratch_shapes=[pltpu.SemaphoreType.DMA((N_DEV,)), pltpu.SemaphoreType.DMA((N_DEV,)), pltpu.SemaphoreType.DMA],
    )(x)


def _pair_exchange(g, *, name):
    def body(g_ref, land_ref, send_sems, recv_sems):
        xx, yy, cc = _position()
        copies = []
        for j in range(N_CHIP):
            cp = pltpu.make_async_remote_copy(src_ref=g_ref.at[2 * j + 1 - cc], dst_ref=land_ref.at[j], send_sem=send_sems.at[j],
                                              recv_sem=recv_sems.at[j], device_id=(xx, yy, 1 - cc), device_id_type=_MESH)
            cp.start()
            copies.append(cp)
        for cp in copies:
            cp.wait_recv()
        for cp in copies:
            cp.wait_send()

    sems = pltpu.SemaphoreType.DMA((N_CHIP,))
    return pl.pallas_call(body, name=name, in_specs=[_HBM], out_specs=_HBM,
                          out_shape=jax.ShapeDtypeStruct((N_CHIP,) + g.shape[1:], g.dtype), scratch_shapes=[sems, sems])(g)


def _cross_exchange(p, *, name):
    half = p.shape[1] // 2

    def body(p_ref, o_ref, relay_ref, send_sems, recv_sems, local_sem):
        xx, yy, cc = _position()
        me = (xx, yy, cc)
        xn, yn, dg = (1 - xx, yy), (xx, 1 - yy), (1 - xx, 1 - yy)
        idx = lambda chip: 2 * chip[0] + chip[1]
        mine = idx((xx, yy))
        first, second = pl.ds(0, half), pl.ds(half, half)

        def copy(k, src, dst, to):
            return pltpu.make_async_remote_copy(src_ref=src, dst_ref=dst, send_sem=send_sems.at[k], recv_sem=recv_sems.at[k],
                                                device_id=to, device_id_type=_MESH)

        local = pltpu.make_async_copy(p_ref.at[mine], o_ref.at[mine], local_sem)
        local.start()
        sends = [copy(0, p_ref.at[idx(xn)], o_ref.at[mine], (*xn, cc)),
                 copy(1, p_ref.at[idx(dg)].at[first], relay_ref.at[0], (*xn, cc)),
                 copy(2, p_ref.at[idx(yn)], o_ref.at[mine], (*yn, cc)),
                 copy(3, p_ref.at[idx(dg)].at[second], relay_ref.at[1], (*yn, cc))]
        for cp in sends:
            cp.start()
        copy(1, relay_ref.at[0], relay_ref.at[0], me).wait_recv()
        sends.append(copy(4, relay_ref.at[0], o_ref.at[idx(xn)].at[first], (*yn, cc)))
        sends[-1].start()
        copy(3, relay_ref.at[1], relay_ref.at[1], me).wait_recv()
        sends.append(copy(5, relay_ref.at[1], o_ref.at[idx(yn)].at[second], (*xn, cc)))
        sends[-1].start()
        for k, dst in ((0, o_ref.at[idx(xn)]), (2, o_ref.at[idx(yn)]), (4, o_ref.at[idx(dg)].at[first]),
                       (5, o_ref.at[idx(dg)].at[second])):
            copy(k, dst, dst, me).wait_recv()
        for cp in sends:
            cp.wait_send()
        local.wait()

    sems = pltpu.SemaphoreType.DMA((6,))
    relay = jax.ShapeDtypeStruct((2, half) + p.shape[2:], p.dtype)
    return pl.pallas_call(body, name=name, in_specs=[_HBM], out_specs=[_HBM, _HBM],
                          out_shape=[jax.ShapeDtypeStruct(p.shape, p.dtype), relay],
                          scratch_shapes=[sems, sems, pltpu.SemaphoreType.DMA])(p)[0]


def _all_sum(x, *, name):
    return _sum_slabs(_gather8(x, name=f"gather_{name}"), name=f"sum_{name}", tr=min(128, x.shape[0]))


def _pack_slabs(parts, rows, axis=0):
    lead = parts[0].shape[:axis]
    slabs = [p.reshape(lead + (-1, D)) for p in parts]
    used = sum(sl.shape[axis] for sl in slabs)
    return jnp.concatenate(slabs + [jnp.zeros(lead + (rows - used, D), slabs[0].dtype)], axis=axis)


def _unpack_slabs(slab, shapes):
    lead, out, off = slab.shape[:-2], [], 0
    for shp in shapes:
        n = int(np.prod(shp)) // D
        out.append(slab[..., off:off + n, :].reshape(lead + tuple(shp)))
        off += n
    return out


def _pack_rows(parts, rows):
    flat = jnp.concatenate([p.reshape(-1) for p in parts])
    return jnp.pad(flat, (0, rows * D - flat.shape[0])).reshape(rows, D)


def _unpack_rows(slab, shapes):
    flat, out, off = slab.reshape(-1), [], 0
    for shp in shapes:
        n = int(np.prod(shp))
        out.append(flat[off:off + n].reshape(shp))
        off += n
    return out


def _full_shape(shard, axis):
    return tuple(d * N_DEV if i == axis else d for i, d in enumerate(shard))


def _row(v):
    return v.reshape(1, -1).astype(f32)


def _pad_gain(g):
    return jnp.pad(g.astype(f32), (0, HEAD_PAD - QK)).reshape(1, HEAD_PAD)


def _ffn_fwd(x, p, tag):
    h = _rms_fwd(x, p["norm"], name=f"ffn_norm_{tag}")
    zg = _mm(h, p["wgT"], tb=True, tn=FFN_H, name=f"ffn_up_g_{tag}")
    zv = _mm(h, p["wvT"], tb=True, tn=FFN_H, name=f"ffn_up_v_{tag}")
    a = _ffn_act_fwd(zg, zv, p["cwg"], p["cwv"], name=f"ffn_act_{tag}")
    y = _mm(a, p["wd"], add=x, name=f"ffn_down_{tag}")
    return y, (x, h, zg, zv, a)


def _ffn_bwd(dy, dyb, p, saved, tag):
    x, h, zg, zv, a = saved
    g = {}
    da = _mm(dyb, p["wd"], tb=True, out_dtype=bf16, name=f"ffn_down_dx_{tag}")
    g["wd"] = _mm(a, dyb, ta=True, out_dtype=bf16, name=f"ffn_down_dw_{tag}")
    dzg, dzv, g["cwg"], g["cwv"] = _ffn_act_bwd(da, zg, zv, p["cwg"], p["cwv"], name=f"ffn_act_bwd_{tag}")
    g["wgT"] = _mm(dzg, h, ta=True, out_dtype=bf16, name=f"ffn_up_g_dw_{tag}")
    g["wvT"] = _mm(dzv, h, ta=True, out_dtype=bf16, name=f"ffn_up_v_dw_{tag}")
    dh = _mm(dzg, p["wgT"], name=f"ffn_up_g_dx_{tag}")
    dh = _mm(dzv, p["wvT"], add=dh, name=f"ffn_up_v_dx_{tag}")
    dx, dxb, g["norm"] = _rms_bwd(dh, x, p["norm"], res=dy, name=f"ffn_norm_bwd_{tag}")
    return dx, dxb, g


def _mla_fwd(x, p, tabs, tag):
    cos_t, sin_t = tabs
    h = _rms_fwd(x, p["norm"], name=f"attn_norm_{tag}")
    proj = _mm(h, p["w_inT"], tb=True, name=f"mix_in_{tag}")
    cqn = _rms_fwd(proj, p["cq_norm"], col=0, name=f"cq_norm_{tag}")
    ckvn = _rms_fwd(proj, p["ckv_norm"], col=1, name=f"ckv_norm_{tag}")
    q_raw = _mm(cqn, p["w_uqT"], tb=True, name=f"uq_{tag}")
    kv_raw = _mm(ckvn, p["w_ukvT"], tb=True, name=f"ukv_{tag}")
    q, k, v = _qk_prep_fwd(q_raw, kv_raw, proj, p["q_gain"], p["k_gain"], cos_t, sin_t, name=f"qk_prep_{tag}")
    o, lse = _flash_fwd(q, k, v, name=f"flash_fwd_{tag}")
    conv = _sconv_fwd(proj, p["sconv_w"], name=f"sconv_{tag}")
    y = _mm(conv, p["w_out"][HEADS * HEAD_PAD:], add=x, name=f"mix_out_conv_{tag}")
    y = _mm(o, p["w_out"][:HEADS * HEAD_PAD], add=y, name=f"mix_out_{tag}")
    return y, (x, h, proj, cqn, ckvn, q_raw, kv_raw, q, k, v, o, lse, conv)


def _mla_bwd(dy, dyb, p, tabs, saved, tag):
    cos_t, sin_t = tabs
    x, h, proj, cqn, ckvn, q_raw, kv_raw, q, k, v, o, lse, conv = saved
    s = x.shape[0]
    g = {}
    dmix = _mm(dyb, p["w_out"], tb=True, name=f"mix_out_dx_{tag}")
    g["w_out"] = jnp.concatenate([_mm(o, dyb, ta=True, out_dtype=bf16, name=f"mix_out_dw_{tag}"),
                                  _mm(conv, dyb, ta=True, out_dtype=bf16, name=f"mix_out_conv_dw_{tag}")], axis=0)
    dgb, dgc, dci, g["sconv_w"] = _sconv_bwd(dmix, proj, p["sconv_w"], name=f"sconv_bwd_{tag}")
    dq, delta = _flash_bwd_dq(q, k, v, o, dmix, lse, name=f"flash_dq_{tag}")
    dk, dv = _flash_bwd_dkv(q, k, v, dmix, lse.reshape(HEADS, 1, s), delta.reshape(HEADS, 1, s), name=f"flash_dkv_{tag}")
    dq_raw, dkv_raw, dkr, g["q_gain"], g["k_gain"] = _qk_prep_bwd(
        dq, dk, dv, q_raw, kv_raw, proj, p["q_gain"], p["k_gain"], cos_t, sin_t, name=f"qk_prep_bwd_{tag}")
    dcqn = _mm(dq_raw, p["w_uqT"], name=f"uq_dx_{tag}")
    g["w_uqT"] = _mm(dq_raw, cqn, ta=True, out_dtype=bf16, name=f"uq_dw_{tag}")
    dckvn = _mm(dkv_raw, p["w_ukvT"], name=f"ukv_dx_{tag}")
    g["w_ukvT"] = _mm(dkv_raw, ckvn, ta=True, out_dtype=bf16, name=f"ukv_dw_{tag}")
    dcq, g["cq_norm"] = _rms_bwd(dcqn, proj, p["cq_norm"], col=0, out_dtype=bf16, name=f"cq_norm_bwd_{tag}")
    dckv, g["ckv_norm"] = _rms_bwd(dckvn, proj, p["ckv_norm"], col=1, out_dtype=bf16, name=f"ckv_norm_bwd_{tag}")
    dproj = jnp.concatenate([dcq, dckv, dgb, dgc, dci, dkr.astype(bf16)], axis=1)
    dh = _mm(dproj, p["w_inT"], name=f"mix_in_dx_{tag}")
    g["w_inT"] = _mm(dproj, h, ta=True, out_dtype=bf16, name=f"mix_in_dw_{tag}")
    dx, dxb, g["norm"] = _rms_bwd(dh, x, p["norm"], res=dy, name=f"attn_norm_bwd_{tag}")
    return dx, dxb, g


def _block_diag(wg):
    nb, ng, r, c = wg.shape
    eye = jnp.eye(ng, dtype=wg.dtype)
    return (wg[:, :, :, None, :] * eye[None, :, None, :, None]).reshape(nb, ng * r, ng * c)


def _s5_mats(bbr, bbi, c_re, c_im):
    nb = GROUPS // 8
    b4 = jnp.stack([bbr.reshape(GROUPS, STATE, GROUP), bbi.reshape(GROUPS, STATE, GROUP)], axis=1)
    wg = jnp.transpose(b4, (0, 3, 1, 2)).reshape(nb, 8, GROUP, 2 * STATE)
    cg = jnp.stack([c_re, -c_im], axis=1)
    cg = jnp.transpose(cg, (0, 1, 3, 2)).reshape(nb, 8, 2 * STATE, GROUP)
    return _state_layout(_block_diag(wg), 2), _state_layout(_block_diag(cg), 1)


def _state_layout(m, axis):
    shp = m.shape
    m = m.reshape(shp[:axis] + (4, 2, 2, STATE) + shp[axis + 1:])
    return jnp.swapaxes(m, axis + 1, axis + 2).reshape(shp)


def _group_blocks(d):
    d = d.reshape(GROUPS // 2, 2, GROUP, 2, 2, STATE)
    return jnp.stack([d[:, 0, :, :, 0, :], d[:, 1, :, :, 1, :]], axis=1).reshape(GROUPS, GROUP, 2, STATE)


def _s5_fwd(x, p, tag):
    h = _rms_fwd(x, p["norm"], name=f"ssm_norm_{tag}")
    u, ub = _mm(h, p["w_in"], twin=True, name=f"ssm_in_{tag}")
    ar, ai, bbr, bbi = _disc_fwd(p["lr"], p["li"], p["ls"], p["br"], p["bi"], name=f"disc_{tag}")
    wb, cb = _s5_mats(bbr, bbi, p["c_re"], p["c_im"])
    a1, a2 = ar.reshape(GROUPS // 2, 1, LANES), ai.reshape(GROUPS // 2, 1, LANES)
    xs = _scan_fwd(ub, wb.astype(bf16), a1, a2, name=f"ssm_scan_{tag}")
    y = _bd_nn(xs, cb.astype(bf16), name=f"ssm_y_{tag}")
    g = _gelu_fwd(y, u, p["d_skip"], name=f"ssm_gelu_{tag}")
    a = _mm(g, p["wgaT"], tb=True, name=f"glu_a_{tag}")
    b = _mm(g, p["wgbT"], tb=True, name=f"glu_b_{tag}")
    out = _glu_fwd(x, a, b, name=f"glu_{tag}")
    return out, (x, h, u, ub, wb, cb, a1, a2, xs, y, g, a, b)


def _s5_bwd(dout, p, saved, tag):
    x, h, u, ub, wb, cb, a1, a2, xs, y, g, a, b = saved
    gr = {}
    da, db = _glu_bwd(dout, a, b, name=f"glu_bwd_{tag}")
    dg = _mm(da, p["wgaT"], name=f"glu_a_dx_{tag}")
    dg = _mm(db, p["wgbT"], add=dg, name=f"glu_b_dx_{tag}")
    gr["wgaT"] = _mm(da, g, ta=True, out_dtype=bf16, name=f"glu_a_dw_{tag}")
    gr["wgbT"] = _mm(db, g, ta=True, out_dtype=bf16, name=f"glu_b_dw_{tag}")
    dy, du1, gr["d_skip"] = _gelu_bwd(dg, y, u, p["d_skip"], name=f"ssm_gelu_bwd_{tag}")
    dct = _group_blocks(_bd_tn_diag(dy, xs, name=f"ssm_y_dw_{tag}"))
    gs, dar, dai = _scan_bwd(dy, jnp.swapaxes(cb, 1, 2).astype(bf16), xs, a1, a2, name=f"ssm_scan_bwd_{tag}")
    du = _bd_nn(gs, jnp.swapaxes(wb, 1, 2).astype(bf16), add=du1, out_dtype=bf16, name=f"ssm_bu_dx_{tag}")
    dwg = _group_blocks(_bd_tn_diag(ub, gs, name=f"ssm_bu_dw_{tag}"))
    dh = _mm(du, p["w_in"], tb=True, name=f"ssm_in_dx_{tag}")
    gr["w_in"] = _mm(h, du, ta=True, out_dtype=bf16, name=f"ssm_in_dw_{tag}")
    dx, dxb, gr["norm"] = _rms_bwd(dh, x, p["norm"], res=dout, name=f"ssm_norm_bwd_{tag}")
    dbb = jnp.transpose(dwg, (2, 0, 3, 1)).reshape(2, GROUPS, STATE * GROUP)
    gr["c_re"] = dct[:, :, 0, :]
    gr["c_im"] = -dct[:, :, 1, :]
    dlr, dli, dls, dbr, dbi = _disc_bwd(p["lr"], p["li"], p["ls"], p["br"], p["bi"], dar.reshape(GROUPS, STATE),
                                        dai.reshape(GROUPS, STATE), dbb[0], dbb[1], name=f"disc_bwd_{tag}")
    gr["lr"], gr["li"], gr["ls"] = dlr, dli, dls.reshape(GROUPS)
    gr["br"], gr["bi"] = dbr.reshape(GROUPS, STATE, GROUP), dbi.reshape(GROUPS, STATE, GROUP)
    return dx, dxb, gr


def _slab_shape(shard, axis):
    return (shard[0], shard[2], shard[1]) if axis == 2 else shard


def _to_slab(w, axis):
    return jnp.swapaxes(w, 1, 2) if axis == 2 else w


def _mix_in_pad(wt):
    z = lambda n: jnp.zeros((n, wt.shape[1]), wt.dtype)
    return jnp.concatenate([wt[:512], wt[544:2080], z(NOPE), wt[512:544], z(HEAD_PAD - QK)], axis=0)


def _mix_in_unpad(g):
    return jnp.concatenate([g[:512], g[2048 + NOPE:2048 + QK], g[512:2048]], axis=0)


def _mix_out_pad(w):
    att = jnp.pad(w[:512].reshape(HEADS, NOPE, D), ((0, 0), (NOPE, 0), (0, 0))).reshape(HEADS * HEAD_PAD, D)
    return jnp.concatenate([att, w[512:]], axis=0)


def _mix_out_unpad(g):
    att = g[:HEADS * HEAD_PAD].reshape(HEADS, HEAD_PAD, D)[:, NOPE:, :].reshape(HEADS * NOPE, D)
    return jnp.concatenate([att, g[HEADS * HEAD_PAD:]], axis=0)


def _layer_params(wl, ws, layer):
    i = layer // 2
    half = N_DEV // 2
    up = wl["ffn_w_up"][layer]
    ffn = dict(norm=_row(ws["ffn_norm"][layer]), wgT=up[:half].reshape(FFN_H, D), wvT=up[half:].reshape(FFN_H, D),
               cwg=ws["ffn_conv_w"][layer][:, :FFN_H], cwv=ws["ffn_conv_w"][layer][:, FFN_H:],
               wd=wl["ffn_w_down"][layer].reshape(FFN_H, D))
    if layer % 2 == 0:
        uq = jnp.pad(wl["w_uq"][i], ((0, 0), (0, HEAD_PAD - QK), (0, 0)))
        mixer = dict(norm=_row(ws["attn_norm"][i]), w_inT=_mix_in_pad(wl["mix_w_in"][i].reshape(-1, D)),
                     cq_norm=_row(ws["cq_norm"][i]), ckv_norm=_row(ws["ckv_norm"][i]),
                     w_uqT=uq.reshape(HEADS * HEAD_PAD, LORA), w_ukvT=wl["w_ukv"][i].reshape(HEADS * HEAD_PAD, LORA),
                     q_gain=_pad_gain(ws["q_gain"][i]), k_gain=_pad_gain(ws["k_gain"][i]), sconv_w=ws["sconv_w"][i],
                     w_out=_mix_out_pad(wl["mix_w_out"][i].reshape(D, D)))
    else:
        glu = wl["w_glu"][i]
        mixer = dict(norm=_row(ws["ssm_norm"][i]), w_in=wl["ssm_w_in"][i].reshape(D, D), lr=ws["lambda_re"][i],
                     li=ws["lambda_im"][i], ls=ws["log_step"][i].reshape(GROUPS, 1),
                     br=ws["b_re"][i].reshape(GROUPS, STATE * GROUP), bi=ws["b_im"][i].reshape(GROUPS, STATE * GROUP),
                     c_re=ws["c_re"][i], c_im=ws["c_im"][i], d_skip=_row(ws["d_skip"][i]),
                     wgaT=glu[:half].reshape(D, D), wgbT=glu[half:].reshape(D, D))
    return mixer, ffn


def _collect_grads(gm, gf):
    ev, od, half = (0, 2), (1, 3), N_DEV // 2
    st = lambda xs: jnp.stack(xs, axis=0)
    per_dev = list
    halves = lambda a, b, rows: jnp.concatenate([a.reshape(half, rows, D), b.reshape(half, rows, D)], axis=0)
    big = {
        "ffn_w_up": per_dev([halves(gf[l]["wgT"], gf[l]["wvT"], FFN_H // half) for l in range(4)]),
        "ffn_w_down": per_dev([gf[l]["wd"].reshape(N_DEV, -1, D) for l in range(4)]),
        "w_glu": per_dev([halves(gm[l]["wgaT"], gm[l]["wgbT"], D // half) for l in od]),
        "mix_w_out": per_dev([_mix_out_unpad(gm[l]["w_out"]).reshape(N_DEV, -1, D) for l in ev]),
        "ssm_w_in": per_dev([gm[l]["w_in"].reshape(N_DEV, -1, D) for l in od]),
        "w_ukv": per_dev([gm[l]["w_ukvT"].reshape(N_DEV, HEAD_PAD, LORA) for l in ev]),
        "w_uq": per_dev([gm[l]["w_uqT"].reshape(N_DEV, HEAD_PAD, LORA)[:, :QK] for l in ev]),
        "mix_w_in": per_dev([_mix_in_unpad(gm[l]["w_inT"]).reshape(N_DEV, -1, D) for l in ev]),
    }
    small = {
        "attn_norm": st([gm[l]["norm"].reshape(D) for l in ev]),
        "cq_norm": st([gm[l]["cq_norm"].reshape(LORA) for l in ev]),
        "ckv_norm": st([gm[l]["ckv_norm"].reshape(LORA) for l in ev]),
        "q_gain": st([gm[l]["q_gain"].reshape(HEAD_PAD)[:QK] for l in ev]),
        "k_gain": st([gm[l]["k_gain"].reshape(HEAD_PAD)[:QK] for l in ev]),
        "sconv_w": st([gm[l]["sconv_w"] for l in ev]),
        "ssm_norm": st([gm[l]["norm"].reshape(D) for l in od]),
        "lambda_re": st([gm[l]["lr"] for l in od]), "lambda_im": st([gm[l]["li"] for l in od]),
        "log_step": st([gm[l]["ls"] for l in od]),
        "b_re": st([gm[l]["br"] for l in od]), "b_im": st([gm[l]["bi"] for l in od]),
        "c_re": st([gm[l]["c_re"] for l in od]), "c_im": st([gm[l]["c_im"] for l in od]),
        "d_skip": st([gm[l]["d_skip"].reshape(D) for l in od]),
        "ffn_norm": st([gf[l]["norm"].reshape(D) for l in range(4)]),
        "ffn_conv_w": st([jnp.concatenate([gf[l]["cwg"], gf[l]["cwv"]], axis=1) for l in range(4)]),
    }
    return big, small


def _local_step(x, target, wl, ws):
    s = x.shape[0]
    tabs = _rope_tables(s)
    saved, params = [], []
    for layer in range(4):
        mixer, ffn = _layer_params(wl, ws, layer)
        params.append((mixer, ffn))
        if layer % 2 == 0:
            x, sm = _mla_fwd(x, mixer, tabs, f"l{layer}")
        else:
            x, sm = _s5_fwd(x, mixer, f"l{layer}")
        x, sf = _ffn_fwd(x, ffn, f"l{layer}")
        saved.append((sm, sf))
    dx, dxb, loss = _loss_head(x, target, name="loss_head")
    gm, gf = [None] * 4, [None] * 4
    for layer in reversed(range(4)):
        mixer, ffn = params[layer]
        sm, sf = saved[layer]
        dx, dxb, gf[layer] = _ffn_bwd(dx, dxb, ffn, sf, f"l{layer}")
        if layer % 2 == 0:
            dx, dxb, gm[layer] = _mla_bwd(dx, dxb, mixer, tabs, sm, f"l{layer}")
        else:
            dx, dxb, gm[layer] = _s5_bwd(dx, mixer, sm, f"l{layer}")
    return loss, dx, _collect_grads(gm, gf)


def kernel(x, attn_norm, mix_w_in, cq_norm, ckv_norm, w_uq, w_ukv, q_gain, k_gain, sconv_w, mix_w_out, ssm_norm, ssm_w_in, lambda_re, lambda_im, log_step, b_re, b_im, c_re, c_im, d_skip, w_glu, ffn_norm, ffn_w_up, ffn_conv_w, ffn_w_down, loss_target, m_attn_norm, m_mix_w_in, m_cq_norm, m_ckv_norm, m_w_uq, m_w_ukv, m_q_gain, m_k_gain, m_sconv_w, m_mix_w_out, m_ssm_norm, m_ssm_w_in, m_lambda_re, m_lambda_im, m_log_step, m_b_re, m_b_im, m_c_re, m_c_im, m_d_skip, m_w_glu, m_ffn_norm, m_ffn_w_up, m_ffn_conv_w, m_ffn_w_down, v_attn_norm, v_mix_w_in, v_cq_norm, v_ckv_norm, v_w_uq, v_w_ukv, v_q_gain, v_k_gain, v_sconv_w, v_mix_w_out, v_ssm_norm, v_ssm_w_in, v_lambda_re, v_lambda_im, v_log_step, v_b_re, v_b_im, v_c_re, v_c_im, v_d_skip, v_w_glu, v_ffn_norm, v_ffn_w_up, v_ffn_conv_w, v_ffn_w_down):
    args = dict(locals())
    wsh = {n: args[n] for n in WEIGHTS}
    msh = {n: args["m_" + n] for n in WEIGHTS}
    vsh = {n: args["v_" + n] for n in WEIGHTS}
    me = 4 * lax.axis_index("x") + 2 * lax.axis_index("y") + lax.axis_index("c")
    big_names = [n for n, _, _ in BIG]
    slab_shapes = [_slab_shape(sh, ax) for _, sh, ax in BIG]
    small_names = [n for n, _ in REPL] + [n for n, _, _ in SMALL]

    mine = _pack_slabs([_to_slab(wsh[n], ax).astype(bf16) for n, _, ax in BIG], BIG_ROWS)
    gathered, wl, off = _gather8(mine, name="gather_weights"), {}, 0
    for n, (layers, rows, inner) in zip(big_names, slab_shapes):
        per = rows * inner // D
        wl[n] = [gathered[:, off + l * per:off + (l + 1) * per, :].reshape(N_DEV, rows, inner) for l in range(layers)]
        off += layers * per
    placed = []
    for n, shard, axis in SMALL:
        start = [0] * len(shard)
        start[axis] = me * shard[axis]
        placed.append(lax.dynamic_update_slice(jnp.zeros(_full_shape(shard, axis), f32), wsh[n], start))
    small_all = _all_sum(_pack_rows(placed, SMALL_FWD_ROWS), name="small_params")
    ws = dict(zip([n for n, _, _ in SMALL], _unpack_rows(small_all, [_full_shape(sh, ax) for _, sh, ax in SMALL])))
    ws.update({n: wsh[n] for n, _ in REPL})

    loss8, grad_x, (big_grads, grads) = _local_step(x[0], loss_target[0], wl, ws)

    pieces = []
    for n in big_names:
        layers = big_grads[n]
        tiled = (layers[0].shape[1] * layers[0].shape[2] // D) % 16 == 0
        pieces += layers if tiled else [jnp.stack(layers, axis=1)]
    contrib = _pack_slabs(pieces, BIG_ROWS, axis=1)
    chip_sum = _pair_sum(contrib, _pair_exchange(contrib, name="grads_pair_exchange"), name="grads_pair_sum")
    g_big = _sum_slabs(_cross_exchange(chip_sum, name="grads_cross_exchange"), name="grads_chip_sum")
    small_vec = _pack_rows([grads[n] for n, _ in REPL] + [grads[n] for n, _, _ in SMALL] + [loss8[0, :1]], SMALL_ROWS)
    small_sum = _all_sum(small_vec, name="small_grads")
    parts = _unpack_rows(small_sum, [sh for _, sh in REPL] + [_full_shape(sh, ax) for _, sh, ax in SMALL] + [(1,)])
    g = {n: val for (n, _), val in zip(REPL, parts)}
    for (n, shard, axis), val in zip(SMALL, parts[len(REPL):]):
        start = [0] * len(shard)
        start[axis] = me * shard[axis]
        g[n] = lax.dynamic_slice(val, start, shard)
    loss = parts[-1].reshape(())
    for (n, _, axis), val in zip(BIG, _unpack_slabs(g_big, slab_shapes)):
        g[n] = _to_slab(val, axis)

    delta, new_m, new_v = {}, {}, {}
    for n, shard, _ in BIG:
        flat = lambda a: a.reshape(-1, shard[-1])
        outs = _adamw(flat(wsh[n]), flat(g[n]), flat(msh[n]), flat(vsh[n]), name=f"adamw_{n}")
        delta[n], new_m[n], new_v[n] = [o.reshape(shard) for o in outs]
    small_state = [_pack_rows([src[n] for n in small_names], SMALL_ROWS) for src in (wsh, g, msh, vsh)]
    for dst, slab in zip((delta, new_m, new_v), _adamw(*small_state, name="adamw_small")):
        dst.update(zip(small_names, _unpack_rows(slab, [wsh[n].shape for n in small_names])))

    return (loss, grad_x[None], *[g[n] for n in WEIGHTS], *[delta[n] for n in WEIGHTS],
            *[new_m[n] for n in WEIGHTS], *[new_v[n] for n in WEIGHTS])
```

```python
import math

import numpy as np
import jax
import jax.numpy as jnp
from jax import lax
from jax.experimental import pallas as pl
from jax.experimental.pallas import tpu as pltpu

f32, bf16 = jnp.float32, jnp.bfloat16

N_DEV = 8
D = 1024
HEADS = 8
NOPE, ROPE, QK = 64, 32, 96
HEAD_PAD = 128
LORA = 256
CONV_CH = 512
MIX_IN_PAD = 2176
FFN_H = 2816
GROUPS, GROUP, STATE = 64, 16, 64
EPS = 1e-6
ROPE_THETA = 10000.0
ADAM_LR, ADAM_B1, ADAM_B2, ADAM_EPS, ADAM_WD, ADAM_STEP = 0.001, 0.9, 0.999, 1e-08, 0.01, 10
LANES = 128
PAIR_LANES = 2 * LANES
VMEM_LIMIT = 56 << 20
MM_VMEM_BUDGET = 40 << 20
NEG = -1e30

BIG = (
    ("ffn_w_up", (4, 1024, 704), 2), ("ffn_w_down", (4, 352, 1024), 1), ("w_glu", (2, 1024, 256), 2),
    ("mix_w_out", (2, 128, 1024), 1), ("ssm_w_in", (2, 128, 1024), 1), ("w_ukv", (2, 256, 128), 2),
    ("w_uq", (2, 256, 96), 2), ("mix_w_in", (2, 1024, 260), 2))
REPL = (("attn_norm", (2, 1024)), ("cq_norm", (2, 256)), ("ckv_norm", (2, 256)), ("q_gain", (2, 96)),
        ("k_gain", (2, 96)), ("lambda_re", (2, 64, 64)), ("lambda_im", (2, 64, 64)), ("log_step", (2, 64)),
        ("b_re", (2, 64, 64, 16)), ("b_im", (2, 64, 64, 16)), ("c_re", (2, 64, 16, 64)), ("c_im", (2, 64, 16, 64)),
        ("ffn_norm", (4, 1024)))
SMALL = (("sconv_w", (2, 3, 64), 2), ("ssm_norm", (2, 128), 1), ("d_skip", (2, 128), 1), ("ffn_conv_w", (4, 3, 704), 2))
WEIGHTS = ['attn_norm', 'mix_w_in', 'cq_norm', 'ckv_norm', 'w_uq', 'w_ukv', 'q_gain', 'k_gain', 'sconv_w', 'mix_w_out',
           'ssm_norm', 'ssm_w_in', 'lambda_re', 'lambda_im', 'log_step', 'b_re', 'b_im', 'c_re', 'c_im', 'd_skip',
           'w_glu', 'ffn_norm', 'ffn_w_up', 'ffn_conv_w', 'ffn_w_down']
BIG_ROWS = 5888
SMALL_FWD_ROWS = 80
SMALL_ROWS = 640


def _cparams(sem=None, **kw):
    return pltpu.CompilerParams(dimension_semantics=sem, vmem_limit_bytes=VMEM_LIMIT, **kw)


def _tile(n, target):
    best = 0
    for t in range(LANES, min(n, target) + 1, LANES):
        if n % t == 0:
            best = t
    return best if best else n


def _mm(a, b, *, ta=False, tb=False, out_dtype=f32, add=None, twin=False, name, tm=1024, tn=1536):
    m, k = (a.shape[1], a.shape[0]) if ta else a.shape
    n = b.shape[0] if tb else b.shape[1]
    assert (b.shape[1] if tb else b.shape[0]) == k
    tm = _tile(m, tm)
    tn_ = _tile(n, tn)
    tn = n if (tn_ < 256 and n <= 2304) else tn_

    def vmem_bytes(t):
        io = 2 * (tm * t * a.dtype.itemsize + t * tn * b.dtype.itemsize + tm * tn * (jnp.dtype(out_dtype).itemsize + 2 * twin))
        return io + (2 * tm * tn * 4 if add is not None else 0) + (tm * tn * 4 if t < k else 0)

    tk = next((t for t in [k] + [t for t in range(k - LANES, 0, -LANES) if k % t == 0] if vmem_bytes(t) <= MM_VMEM_BUDGET), LANES)
    nk = k // tk
    dn = (((0 if ta else 1,), (1 if tb else 0,)), ((), ()))

    def body(*refs):
        a_ref, b_ref = refs[:2]
        add_ref = refs[2] if add is not None else None
        o_ref = refs[3] if add is not None else refs[2]
        twin_ref = refs[4 if add is not None else 3] if twin else None
        part = lax.dot_general(a_ref[...].astype(bf16), b_ref[...].astype(bf16), dn, preferred_element_type=f32)

        def finish(r):
            if add is not None:
                r = r + add_ref[...].astype(f32)
            o_ref[...] = r.astype(out_dtype)
            if twin:
                twin_ref[...] = r.astype(bf16)

        if nk == 1:
            finish(part)
            return
        acc = refs[-1]
        kk = pl.program_id(2)

        @pl.when(kk == 0)
        def _():
            acc[...] = part

        @pl.when(kk > 0)
        def _():
            acc[...] += part

        @pl.when(kk == nk - 1)
        def _():
            finish(acc[...])

    a_spec = pl.BlockSpec((tk, tm), lambda i, j, kk: (kk, i)) if ta else pl.BlockSpec((tm, tk), lambda i, j, kk: (i, kk))
    b_spec = pl.BlockSpec((tn, tk), lambda i, j, kk: (j, kk)) if tb else pl.BlockSpec((tk, tn), lambda i, j, kk: (kk, j))
    in_specs, args = [a_spec, b_spec], [a, b]
    if add is not None:
        in_specs.append(pl.BlockSpec((tm, tn), lambda i, j, kk: (i, j)))
        args.append(add)
    o_spec, o_shape = pl.BlockSpec((tm, tn), lambda i, j, kk: (i, j)), jax.ShapeDtypeStruct((m, n), out_dtype)
    return pl.pallas_call(
        body, name=name, grid=(m // tm, n // tn, nk), in_specs=in_specs,
        out_specs=[o_spec, o_spec] if twin else o_spec,
        out_shape=[o_shape, jax.ShapeDtypeStruct((m, n), bf16)] if twin else o_shape,
        scratch_shapes=[pltpu.VMEM((tm, tn), f32)] if nk > 1 else [],
        compiler_params=_cparams(("parallel", "parallel", "arbitrary")))(*args)


def _bd_nn(a, w, *, out_dtype=f32, add=None, name, ts=512):
    s = a.shape[0]
    nb, ka, no = w.shape
    ts = min(ts, s)

    def body(a_ref, w_ref, *rest):
        r = jnp.dot(a_ref[...].astype(bf16), w_ref[0].astype(bf16), preferred_element_type=f32)
        if add is not None:
            r = r + rest[0][...].astype(f32)
        rest[-1][...] = r.astype(out_dtype)

    o_spec = pl.BlockSpec((ts, no), lambda b, i: (i, b))
    return pl.pallas_call(
        body, name=name, grid=(nb, s // ts),
        in_specs=[pl.BlockSpec((ts, ka), lambda b, i: (i, b)), pl.BlockSpec((1, ka, no), lambda b, i: (b, 0, 0))]
        + ([o_spec] if add is not None else []),
        out_specs=o_spec, out_shape=jax.ShapeDtypeStruct((s, nb * no), out_dtype),
        compiler_params=_cparams(("parallel", "parallel")))(a, w, *([add] if add is not None else []))


def _bd_tn_diag(a, g, *, name, ts=512):
    s = a.shape[0]
    nb = a.shape[1] // LANES
    ts = min(ts, s)
    ni = s // ts

    def body(a_ref, g_ref, o_ref, acc):
        i = pl.program_id(1)
        part = lax.dot_general(a_ref[...].astype(bf16), g_ref[...].astype(bf16), (((0,), (0,)), ((), ())),
                               preferred_element_type=f32)

        @pl.when(i == 0)
        def _():
            acc[...] = part

        @pl.when(i > 0)
        def _():
            acc[...] += part

        @pl.when(i == ni - 1)
        def _():
            for j in range(8):
                o_ref[0, j] = acc[j * GROUP:(j + 1) * GROUP, (j // 2) * PAIR_LANES:(j // 2 + 1) * PAIR_LANES]

    return pl.pallas_call(
        body, name=name, grid=(nb, ni),
        in_specs=[pl.BlockSpec((ts, LANES), lambda b, i: (i, b)), pl.BlockSpec((ts, 8 * LANES), lambda b, i: (i, b))],
        out_specs=pl.BlockSpec((1, 8, GROUP, PAIR_LANES), lambda b, i: (b, 0, 0, 0)),
        out_shape=jax.ShapeDtypeStruct((nb, 8, GROUP, PAIR_LANES), f32),
        scratch_shapes=[pltpu.VMEM((LANES, 8 * LANES), f32)],
        compiler_params=_cparams(("parallel", "arbitrary")))(a, g)


def _rms_fwd(x, g, *, col=0, name, ts=512):
    s, d = x.shape[0], g.shape[1]
    ts = min(ts, s)

    def body(x_ref, g_ref, o_ref):
        xv = x_ref[...].astype(f32)
        r = lax.rsqrt(jnp.mean(xv * xv, axis=-1, keepdims=True) + EPS)
        o_ref[...] = (xv * r * g_ref[...]).astype(bf16)

    return pl.pallas_call(
        body, name=name, grid=(s // ts,),
        in_specs=[pl.BlockSpec((ts, d), lambda i: (i, col)), pl.BlockSpec((1, d), lambda i: (0, 0))],
        out_specs=pl.BlockSpec((ts, d), lambda i: (i, 0)),
        out_shape=jax.ShapeDtypeStruct((s, d), bf16),
        compiler_params=_cparams(("parallel",)))(x, g)


def _rms_bwd(dy, x, g, *, col=0, res=None, out_dtype=f32, name, ts=512):
    s, d = dy.shape
    ts = min(ts, s)
    twin = res is not None

    def body(*refs):
        if twin:
            dy_ref, x_ref, g_ref, res_ref, dx_ref, dxb_ref, dg_ref = refs
        else:
            dy_ref, x_ref, g_ref, dx_ref, dg_ref = refs

        @pl.when(pl.program_id(0) == 0)
        def _():
            dg_ref[...] = jnp.zeros_like(dg_ref)

        xv, dyv = x_ref[...].astype(f32), dy_ref[...].astype(f32)
        r = lax.rsqrt(jnp.mean(xv * xv, axis=-1, keepdims=True) + EPS)
        dyg = dyv * g_ref[...]
        dx = r * dyg - xv * (r * r * r) * jnp.mean(xv * dyg, axis=-1, keepdims=True)
        if twin:
            dx = dx + res_ref[...]
            dxb_ref[...] = dx.astype(bf16)
        dx_ref[...] = dx.astype(out_dtype)
        dg_ref[...] += jnp.sum(dyv * xv * r, axis=0, keepdims=True)

    row, vec = pl.BlockSpec((ts, d), lambda i: (i, 0)), pl.BlockSpec((1, d), lambda i: (0, 0))
    in_specs, args = [row, pl.BlockSpec((ts, d), lambda i: (i, col)), vec], [dy, x, g]
    out_specs, out_shape = [row], [jax.ShapeDtypeStruct((s, d), out_dtype)]
    if twin:
        in_specs.append(row)
        args.append(res)
        out_specs.append(row)
        out_shape.append(jax.ShapeDtypeStruct((s, d), bf16))
    return pl.pallas_call(
        body, name=name, grid=(s // ts,), in_specs=in_specs, out_specs=out_specs + [vec],
        out_shape=out_shape + [jax.ShapeDtypeStruct((1, d), f32)],
        compiler_params=_cparams(("arbitrary",)))(*args)


def _swap_halves(z):
    lane = lax.broadcasted_iota(jnp.int32, z.shape, 1)
    return jnp.where(lane < NOPE + ROPE // 2, pltpu.roll(z, LANES - ROPE // 2, axis=1), pltpu.roll(z, ROPE // 2, axis=1))


def _rope_tables(s):
    inv_freq = 1.0 / (ROPE_THETA ** (jnp.arange(0, ROPE, 2, dtype=f32) / ROPE))
    ang = jnp.arange(s, dtype=f32)[:, None] * inv_freq[None, :]
    cos, sin = jnp.cos(ang), jnp.sin(ang)
    one, zero = jnp.ones((s, NOPE), f32), jnp.zeros((s, NOPE), f32)
    pad1, pad0 = jnp.ones((s, HEAD_PAD - QK), f32), jnp.zeros((s, HEAD_PAD - QK), f32)
    return jnp.concatenate([one, cos, cos, pad1], 1), jnp.concatenate([zero, -sin, sin, pad0], 1)


def _qk_prep_fwd(q_raw, kv_raw, proj, qg, kg, cos_t, sin_t, *, name, ts=1024):
    s = q_raw.shape[0]
    ts = min(ts, s)
    rope_blk = (MIX_IN_PAD - HEAD_PAD) // HEAD_PAD

    def body(q_ref, kv_ref, kr_ref, qg_ref, kg_ref, c_ref, s_ref, qo_ref, ko_ref, vo_ref):
        lane = lax.broadcasted_iota(jnp.int32, (ts, HEAD_PAD), 1)
        cosv, sinv = c_ref[...], s_ref[...]

        def norm_rope(z, gain):
            r = lax.rsqrt(jnp.sum(z * z, axis=-1, keepdims=True) * (1.0 / QK) + EPS)
            zn = z * r * gain
            return zn * cosv + _swap_halves(zn) * sinv

        kvv = kv_ref[...]
        qo_ref[...] = (norm_rope(q_ref[...], qg_ref[...]) * _Q_FOLD).astype(bf16)
        ko_ref[...] = norm_rope(jnp.where(lane < NOPE, kvv, kr_ref[...]), kg_ref[...]).astype(bf16)
        vo_ref[...] = jnp.where(lane >= NOPE, kvv, 0.0).astype(bf16)

    head = pl.BlockSpec((ts, HEAD_PAD), lambda i, h: (i, h))
    row = pl.BlockSpec((ts, HEAD_PAD), lambda i, h: (i, 0))
    vec = pl.BlockSpec((1, HEAD_PAD), lambda i, h: (0, 0))
    out = jax.ShapeDtypeStruct((s, HEADS * HEAD_PAD), bf16)
    return pl.pallas_call(
        body, name=name, grid=(s // ts, HEADS),
        in_specs=[head, head, pl.BlockSpec((ts, HEAD_PAD), lambda i, h: (i, rope_blk)), vec, vec, row, row],
        out_specs=[head, head, head], out_shape=[out, out, out],
        compiler_params=_cparams(("parallel", "parallel")))(q_raw, kv_raw, proj, qg, kg, cos_t, sin_t)


def _qk_prep_bwd(dq, dk, dv, q_raw, kv_raw, proj, qg, kg, cos_t, sin_t, *, name, ts=1024):
    s = q_raw.shape[0]
    ts = min(ts, s)
    rope_blk = (MIX_IN_PAD - HEAD_PAD) // HEAD_PAD

    def body(dq_ref, dk_ref, dv_ref, q_ref, kv_ref, kr_ref, qg_ref, kg_ref, c_ref, s_ref,
             dqr_ref, dkvr_ref, dkr_ref, dqg_ref, dkg_ref):
        i, h = pl.program_id(0), pl.program_id(1)
        lane = lax.broadcasted_iota(jnp.int32, (ts, HEAD_PAD), 1)
        is_rope = (lane >= NOPE) & (lane < QK)
        cosv, sinv = c_ref[...], s_ref[...]

        @pl.when((i == 0) & (h == 0))
        def _():
            dqg_ref[...] = jnp.zeros_like(dqg_ref)
            dkg_ref[...] = jnp.zeros_like(dkg_ref)

        @pl.when(h == 0)
        def _():
            dkr_ref[...] = jnp.zeros_like(dkr_ref)

        def back(dout, z, gain):
            dzn = dout * cosv + jnp.where(is_rope, _swap_halves(dout * sinv), 0.0)
            r = lax.rsqrt(jnp.sum(z * z, axis=-1, keepdims=True) * (1.0 / QK) + EPS)
            dzg = dzn * gain
            dz = r * dzg - z * (r * r * r) * (jnp.sum(z * dzg, axis=-1, keepdims=True) * (1.0 / QK))
            return dz, jnp.sum(dzn * z * r, axis=0, keepdims=True)

        dqz, dqg = back(dq_ref[...].astype(f32), q_ref[...], qg_ref[...])
        dqr_ref[...] = dqz.astype(bf16)
        dqg_ref[...] += dqg
        kvv = kv_ref[...]
        dkz, dkg = back(dk_ref[...].astype(f32), jnp.where(lane < NOPE, kvv, kr_ref[...]), kg_ref[...])
        dkg_ref[...] += dkg
        dkvr_ref[...] = jnp.where(lane < NOPE, dkz, dv_ref[...].astype(f32)).astype(bf16)
        dkr_ref[...] += jnp.where(is_rope, dkz, 0.0)

    head = pl.BlockSpec((ts, HEAD_PAD), lambda i, h: (i, h))
    row = pl.BlockSpec((ts, HEAD_PAD), lambda i, h: (i, 0))
    vec = pl.BlockSpec((1, HEAD_PAD), lambda i, h: (0, 0))
    wide = jax.ShapeDtypeStruct((s, HEADS * HEAD_PAD), bf16)
    return pl.pallas_call(
        body, name=name, grid=(s // ts, HEADS),
        in_specs=[head, head, head, head, head, pl.BlockSpec((ts, HEAD_PAD), lambda i, h: (i, rope_blk)), vec, vec, row, row],
        out_specs=[head, head, row, vec, vec],
        out_shape=[wide, wide, jax.ShapeDtypeStruct((s, HEAD_PAD), f32), jax.ShapeDtypeStruct((1, HEAD_PAD), f32),
                   jax.ShapeDtypeStruct((1, HEAD_PAD), f32)],
        compiler_params=_cparams(("arbitrary", "arbitrary")))(dq, dk, dv, q_raw, kv_raw, proj, qg, kg, cos_t, sin_t)


_NT = (((1,), (1,)), ((), ()))
_SCALE = QK ** -0.5
_LOG2E = math.log2(math.e)
_Q_FOLD = _SCALE * _LOG2E
FLASH_TILE = 1024


def _flash_fwd(q, k, v, *, name, tq=FLASH_TILE):
    s = q.shape[0]
    tq = min(tq, s)

    def body(q_ref, k_ref, v_ref, o_ref, lse_ref):
        i = pl.program_id(1)
        qv = q_ref[...]

        def step(j, carry, masked):
            m, l, acc = carry
            st = pl.multiple_of(j * tq, tq)
            kj, vj = k_ref[pl.ds(st, tq), :], v_ref[pl.ds(st, tq), :]
            sc = lax.dot_general(qv, kj, _NT, preferred_element_type=f32)
            if masked:
                rr = lax.broadcasted_iota(jnp.int32, (tq, tq), 0)
                cc = lax.broadcasted_iota(jnp.int32, (tq, tq), 1)
                sc = jnp.where(cc <= rr, sc, NEG)
            m_new = jnp.maximum(m, jnp.max(sc, axis=-1, keepdims=True))
            p = jnp.exp2(sc - m_new)
            alpha = jnp.exp2(m - m_new)
            l = alpha * l + jnp.sum(p, axis=-1, keepdims=True)
            acc = alpha * acc + jnp.dot(p.astype(bf16), vj, preferred_element_type=f32)
            return m_new, l, acc

        init = (jnp.full((tq, 1), NEG, f32), jnp.zeros((tq, 1), f32), jnp.zeros((tq, HEAD_PAD), f32))
        carry = lax.fori_loop(0, i, lambda j, c: step(j, c, False), init)
        m, l, acc = step(i, carry, True)
        o_ref[...] = (acc / l).astype(bf16)
        lse_ref[0] = m + jnp.log2(l)

    blk = pl.BlockSpec((tq, HEAD_PAD), lambda h, i: (i, h))
    full = pl.BlockSpec((s, HEAD_PAD), lambda h, i: (0, h))
    return pl.pallas_call(
        body, name=name, grid=(HEADS, s // tq), in_specs=[blk, full, full],
        out_specs=[blk, pl.BlockSpec((1, tq, 1), lambda h, i: (h, i, 0))],
        out_shape=[jax.ShapeDtypeStruct((s, HEADS * HEAD_PAD), bf16), jax.ShapeDtypeStruct((HEADS, s, 1), f32)],
        compiler_params=_cparams(("parallel", "arbitrary")))(q, k, v)


def _flash_bwd_dq(q, k, v, o, do, lse, *, name, tq=FLASH_TILE):
    s = q.shape[0]
    tq = min(tq, s)

    def body(q_ref, k_ref, v_ref, o_ref, do_ref, lse_ref, dq_ref, dl_ref):
        i = pl.program_id(1)
        qv = q_ref[...]
        dov = do_ref[...].astype(f32)
        delta = jnp.sum(dov * o_ref[...].astype(f32), axis=-1, keepdims=True)
        dob = dov.astype(bf16)
        lsev = lse_ref[0]

        def step(j, acc, masked):
            st = pl.multiple_of(j * tq, tq)
            kj, vj = k_ref[pl.ds(st, tq), :], v_ref[pl.ds(st, tq), :]
            sc = lax.dot_general(qv, kj, _NT, preferred_element_type=f32)
            p = jnp.exp2(sc - lsev)
            if masked:
                rr = lax.broadcasted_iota(jnp.int32, (tq, tq), 0)
                cc = lax.broadcasted_iota(jnp.int32, (tq, tq), 1)
                p = jnp.where(cc <= rr, p, 0.0)
            dp = lax.dot_general(dob, vj, _NT, preferred_element_type=f32)
            ds = p * (dp - delta)
            return acc + jnp.dot(ds.astype(bf16), kj, preferred_element_type=f32)

        acc = lax.fori_loop(0, i, lambda j, c: step(j, c, False), jnp.zeros((tq, HEAD_PAD), f32))
        dq_ref[...] = step(i, acc, True) * _SCALE
        dl_ref[0] = delta

    blk = pl.BlockSpec((tq, HEAD_PAD), lambda h, i: (i, h))
    full = pl.BlockSpec((s, HEAD_PAD), lambda h, i: (0, h))
    col = pl.BlockSpec((1, tq, 1), lambda h, i: (h, i, 0))
    return pl.pallas_call(
        body, name=name, grid=(HEADS, s // tq), in_specs=[blk, full, full, blk, blk, col],
        out_specs=[blk, col],
        out_shape=[jax.ShapeDtypeStruct((s, HEADS * HEAD_PAD), f32), jax.ShapeDtypeStruct((HEADS, s, 1), f32)],
        compiler_params=_cparams(("parallel", "arbitrary")))(q, k, v, o, do, lse)


def _flash_bwd_dkv(q, k, v, do, lse_row, delta_row, *, name, tk=FLASH_TILE):
    s = q.shape[0]
    tk = min(tk, s)
    nblk = s // tk

    def body(q_ref, k_ref, v_ref, do_ref, lse_ref, dl_ref, dk_ref, dv_ref):
        j = pl.program_id(1)
        kv_, vv = k_ref[...], v_ref[...]

        def step(i, carry, masked):
            dk, dv = carry
            st = pl.multiple_of(i * tk, tk)
            qi = q_ref[pl.ds(st, tk), :]
            doi = do_ref[pl.ds(st, tk), :].astype(bf16)
            lse_i = lse_ref[0, :, pl.ds(st, tk)]
            dl_i = dl_ref[0, :, pl.ds(st, tk)]
            st_ = lax.dot_general(kv_, qi, _NT, preferred_element_type=f32)
            pt = jnp.exp2(st_ - lse_i)
            if masked:
                kk = lax.broadcasted_iota(jnp.int32, (tk, tk), 0)
                qq = lax.broadcasted_iota(jnp.int32, (tk, tk), 1)
                pt = jnp.where(kk <= qq, pt, 0.0)
            dv = dv + jnp.dot(pt.astype(bf16), doi, preferred_element_type=f32)
            dpt = lax.dot_general(vv, doi, _NT, preferred_element_type=f32)
            dst = pt * (dpt - dl_i)
            dk = dk + jnp.dot(dst.astype(bf16), qi, preferred_element_type=f32)
            return dk, dv

        zero = jnp.zeros((tk, HEAD_PAD), f32)
        carry = step(j, (zero, zero), True)
        dk, dv = lax.fori_loop(j + 1, nblk, lambda i, c: step(i, c, False), carry)
        dk_ref[...] = dk * (1.0 / _LOG2E)
        dv_ref[...] = dv

    blk = pl.BlockSpec((tk, HEAD_PAD), lambda h, j: (j, h))
    full = pl.BlockSpec((s, HEAD_PAD), lambda h, j: (0, h))
    rowv = pl.BlockSpec((1, 1, s), lambda h, j: (h, 0, 0))
    out = jax.ShapeDtypeStruct((s, HEADS * HEAD_PAD), f32)
    return pl.pallas_call(
        body, name=name, grid=(HEADS, nblk), in_specs=[full, blk, blk, full, rowv, rowv],
        out_specs=[blk, blk], out_shape=[out, out],
        compiler_params=_cparams(("parallel", "arbitrary")))(q, k, v, do, lse_row, delta_row)


SUBLANES = 8


def _shift_down(x, d):
    r = pltpu.roll(x, d, axis=0)
    t = lax.broadcasted_iota(jnp.int32, (SUBLANES, x.shape[1]), 0)
    head = jnp.where(t < d, 0.0, r[:SUBLANES])
    return head if x.shape[0] == SUBLANES else jnp.concatenate([head, r[SUBLANES:]], axis=0)


def _shift_up(x, d):
    s = x.shape[0]
    r = pltpu.roll(x, s - d, axis=0)
    t = lax.broadcasted_iota(jnp.int32, (SUBLANES, x.shape[1]), 0)
    tail = jnp.where(t >= SUBLANES - d, 0.0, r[s - SUBLANES:])
    return tail if s == SUBLANES else jnp.concatenate([r[:s - SUBLANES], tail], axis=0)


def _taps(w_ref):
    return w_ref[0:1, :], w_ref[1:2, :], w_ref[2:3, :]


def _conv3(u, w):
    u1, u2 = _shift_down(u, 1), _shift_down(u, 2)
    return w[0] * u2 + w[1] * u1 + w[2] * u, (u1, u2)


def _ref_shift_down(ref, d):
    s = ref.shape[0]
    return jnp.concatenate([_shift_down(ref[0:SUBLANES, :], d), ref[pl.ds(SUBLANES - d, s - SUBLANES), :]], axis=0)


def _conv3_ref(ref, w):
    return w[0] * _ref_shift_down(ref, 2) + w[1] * _ref_shift_down(ref, 1) + w[2] * ref[...]


def _conv3_t(g, w):
    return w[2] * g + w[1] * _shift_up(g, 1) + w[0] * _shift_up(g, 2)


def _conv3_dw(dw_ref, g, u, shifted):
    dw_ref[0:1, :] = jnp.sum(g * shifted[1], axis=0, keepdims=True)
    dw_ref[1:2, :] = jnp.sum(g * shifted[0], axis=0, keepdims=True)
    dw_ref[2:3, :] = jnp.sum(g * u, axis=0, keepdims=True)


_GB, _GC, _CI = 512 // LANES, 1024 // LANES, 1536 // LANES


def _sconv_fwd(proj, w, *, name):
    s = proj.shape[0]

    def body(gb_ref, gc_ref, ci_ref, w_ref, o_ref):
        o_ref[...] = (gb_ref[...] * _conv3(gc_ref[...] * ci_ref[...], _taps(w_ref))[0]).astype(bf16)

    col = lambda off: pl.BlockSpec((s, LANES), lambda j: (0, off + j))
    return pl.pallas_call(
        body, name=name, grid=(CONV_CH // LANES,),
        in_specs=[col(_GB), col(_GC), col(_CI), pl.BlockSpec((3, LANES), lambda j: (0, j))],
        out_specs=pl.BlockSpec((s, LANES), lambda j: (0, j)),
        out_shape=jax.ShapeDtypeStruct((s, CONV_CH), bf16),
        compiler_params=_cparams(("parallel",)))(proj, proj, proj, w)


def _sconv_bwd(dmix, proj, w, *, name):
    s = proj.shape[0]

    def body(do_ref, gb_ref, gc_ref, ci_ref, w_ref, dgb_ref, dgc_ref, dci_ref, dw_ref):
        wv, gc, ci, do = _taps(w_ref), gc_ref[...], ci_ref[...], do_ref[...].astype(f32)
        u = gc * ci
        conv, shifted = _conv3(u, wv)
        dgb_ref[...] = (do * conv).astype(bf16)
        dc = do * gb_ref[...]
        du = _conv3_t(dc, wv)
        dgc_ref[...] = (du * ci).astype(bf16)
        dci_ref[...] = (du * gc).astype(bf16)
        _conv3_dw(dw_ref, dc, u, shifted)

    col = lambda off: pl.BlockSpec((s, LANES), lambda j: (0, off + j))
    out = jax.ShapeDtypeStruct((s, CONV_CH), bf16)
    return pl.pallas_call(
        body, name=name, grid=(CONV_CH // LANES,),
        in_specs=[col(HEADS), col(_GB), col(_GC), col(_CI), pl.BlockSpec((3, LANES), lambda j: (0, j))],
        out_specs=[col(0), col(0), col(0), pl.BlockSpec((3, LANES), lambda j: (0, j))],
        out_shape=[out, out, out, jax.ShapeDtypeStruct((3, CONV_CH), f32)],
        compiler_params=_cparams(("parallel",)))(dmix, proj, proj, proj, w)


def _ffn_act_fwd(zg, zv, cwg, cwv, *, name):
    s, f = zg.shape

    def body(zg_ref, zv_ref, wg_ref, wv_ref, o_ref):
        o_ref[...] = (jax.nn.silu(_conv3_ref(zg_ref, _taps(wg_ref))) * _conv3_ref(zv_ref, _taps(wv_ref))).astype(bf16)

    col = pl.BlockSpec((s, LANES), lambda j: (0, j))
    wsp = pl.BlockSpec((3, LANES), lambda j: (0, j))
    return pl.pallas_call(
        body, name=name, grid=(f // LANES,), in_specs=[col, col, wsp, wsp], out_specs=col,
        out_shape=jax.ShapeDtypeStruct((s, f), bf16), compiler_params=_cparams(("parallel",)))(zg, zv, cwg, cwv)


def _ffn_act_bwd(da, zg, zv, cwg, cwv, *, name):
    s, f = zg.shape

    def body(da_ref, zg_ref, zv_ref, wg_ref, wv_ref, dzg_ref, dzv_ref, dwg_ref, dwv_ref):
        wg, wv, dav = _taps(wg_ref), _taps(wv_ref), da_ref[...].astype(f32)
        ug, uv = _conv3_ref(zg_ref, wg), _conv3_ref(zv_ref, wv)
        sg = jax.nn.sigmoid(ug)
        dug = dav * uv * (sg * (1.0 + ug * (1.0 - sg)))
        duv = dav * (ug * sg)
        dzg_ref[...] = _conv3_t(dug, wg).astype(bf16)
        dzv_ref[...] = _conv3_t(duv, wv).astype(bf16)
        _conv3_dw(dwg_ref, dug, zg_ref[...], (_ref_shift_down(zg_ref, 1), _ref_shift_down(zg_ref, 2)))
        _conv3_dw(dwv_ref, duv, zv_ref[...], (_ref_shift_down(zv_ref, 1), _ref_shift_down(zv_ref, 2)))

    col = pl.BlockSpec((s, LANES), lambda j: (0, j))
    wsp = pl.BlockSpec((3, LANES), lambda j: (0, j))
    act, wsh = jax.ShapeDtypeStruct((s, f), bf16), jax.ShapeDtypeStruct((3, f), f32)
    return pl.pallas_call(
        body, name=name, grid=(f // LANES,), in_specs=[col, col, col, wsp, wsp], out_specs=[col, col, wsp, wsp],
        out_shape=[act, act, wsh, wsh], compiler_params=_cparams(("parallel",)))(da, zg, zv, cwg, cwv)


def _expand_mat():
    return jnp.asarray(np.kron(np.eye(STATE, dtype=np.float32), np.ones((1, GROUP), np.float32)))


def _disc_fn(lr, li, ls, br, bi, e):
    dt = jnp.exp(ls)
    mag = jnp.exp(lr * dt)
    ar, ai = mag * jnp.cos(li * dt), mag * jnp.sin(li * dt)
    nr, ni = ar - 1.0, ai
    den = lr * lr + li * li
    zr, zi = (nr * lr + ni * li) / den, (ni * lr - nr * li) / den
    zrr = jnp.dot(zr, e, precision=lax.Precision.HIGHEST, preferred_element_type=f32)
    zir = jnp.dot(zi, e, precision=lax.Precision.HIGHEST, preferred_element_type=f32)
    return ar, ai, zrr * br - zir * bi, zrr * bi + zir * br


def _disc_fwd(lr, li, ls, br, bi, *, name):
    def body(lr_ref, li_ref, ls_ref, br_ref, bi_ref, e_ref, ar_ref, ai_ref, bbr_ref, bbi_ref):
        ar, ai, bbr, bbi = _disc_fn(lr_ref[...], li_ref[...], ls_ref[...], br_ref[...], bi_ref[...], e_ref[...])
        ar_ref[...], ai_ref[...], bbr_ref[...], bbi_ref[...] = ar, ai, bbr, bbi

    sq, wide = jax.ShapeDtypeStruct((GROUPS, STATE), f32), jax.ShapeDtypeStruct((GROUPS, STATE * GROUP), f32)
    return pl.pallas_call(body, name=name, out_shape=[sq, sq, wide, wide],
                          compiler_params=_cparams())(lr, li, ls, br, bi, _expand_mat())


def _disc_bwd(lr, li, ls, br, bi, dar, dai, dbbr, dbbi, *, name):
    def body(lr_ref, li_ref, ls_ref, br_ref, bi_ref, e_ref, dar_ref, dai_ref, dbbr_ref, dbbi_ref,
             dlr_ref, dli_ref, dls_ref, dbr_ref, dbi_ref):
        ev = e_ref[...]
        _, vjp = jax.vjp(lambda a, b, c, d_, e_: _disc_fn(a, b, c, d_, e_, ev),
                         lr_ref[...], li_ref[...], ls_ref[...], br_ref[...], bi_ref[...])
        dlr, dli, dls, dbr, dbi = vjp((dar_ref[...], dai_ref[...], dbbr_ref[...], dbbi_ref[...]))
        dlr_ref[...], dli_ref[...], dls_ref[...], dbr_ref[...], dbi_ref[...] = dlr, dli, dls, dbr, dbi

    sq, wide = jax.ShapeDtypeStruct((GROUPS, STATE), f32), jax.ShapeDtypeStruct((GROUPS, STATE * GROUP), f32)
    return pl.pallas_call(body, name=name, out_shape=[sq, sq, jax.ShapeDtypeStruct((GROUPS, 1), f32), wide, wide],
                          compiler_params=_cparams())(lr, li, ls, br, bi, _expand_mat(), dar, dai, dbbr, dbbi)


SCAN_TILE = 64
SCAN_PAIRS = 2


def _tile_shift(v, d, reverse):
    if d % 8:
        return _shift_up(v, d) if reverse else _shift_down(v, d)
    z = jnp.zeros((d, v.shape[1]), v.dtype)
    return jnp.concatenate([v[d:], z], axis=0) if reverse else jnp.concatenate([z, v[:v.shape[0] - d]], axis=0)


def _tile_scan(r, i, pows, reverse):
    d = 1
    for br, bi in pows:
        rs, is_ = _tile_shift(r, d, reverse), _tile_shift(i, d, reverse)
        r, i = r + br * rs - bi * is_, i + br * is_ + bi * rs
        d *= 2
    return r, i


def _scan_setup(ar, ai, reverse):
    if reverse:
        ai = -ai
    pows, br, bi, d = [], ar, ai, 1
    while d < SCAN_TILE:
        pows.append((br, bi))
        br, bi, d = br * br - bi * bi, 2.0 * br * bi, 2 * d
    row = lax.broadcasted_iota(jnp.int32, (SCAN_TILE, LANES), 0)
    hit = row == (SCAN_TILE - 1 if reverse else 0)
    pr, pi = _tile_scan(jnp.where(hit, ar, 0.0), jnp.where(hit, ai, 0.0), pows, reverse)
    return pows, pr, pi


def _carry_in(r, i, pr, pi, cr, ci):
    crb, cib = jnp.broadcast_to(cr, r.shape), jnp.broadcast_to(ci, i.shape)
    return r + pr * crb - pi * cib, i + pr * cib + pi * crb


def _pair_cols(q):
    return slice(q * PAIR_LANES, q * PAIR_LANES + LANES), slice(q * PAIR_LANES + LANES, (q + 1) * PAIR_LANES)


_SCAN_W = SCAN_PAIRS * PAIR_LANES


def _scan_specs(s, w):
    per = w.shape[2] // _SCAN_W
    src = pl.BlockSpec((s, LANES), lambda g: (0, g // per))
    mat = pl.BlockSpec((1, LANES, _SCAN_W), lambda g: (g // per, 0, g % per))
    col = pl.BlockSpec((s, _SCAN_W), lambda g: (0, g))
    vec = pl.BlockSpec((SCAN_PAIRS, 1, LANES), lambda g: (g, 0, 0))
    return src, mat, col, vec, (w.shape[0] * per,)


def _scan_fwd(u, wb, ar, ai, *, name):
    s = u.shape[0]
    nt = s // SCAN_TILE

    def body(u_ref, w_ref, ar_ref, ai_ref, x_ref):
        setups = [_scan_setup(ar_ref[q], ai_ref[q], False) for q in range(SCAN_PAIRS)]
        wv = w_ref[0]

        def tile_rows(k):
            return pl.ds(pl.multiple_of(k * SCAN_TILE, SCAN_TILE), SCAN_TILE)

        def tile_in(k):
            return jnp.dot(u_ref[tile_rows(k), :].astype(bf16), wv, preferred_element_type=f32)

        def step(k, carry):
            rows, bu = tile_rows(k), carry[-1]
            ahead = tile_in(jnp.minimum(k + 1, nt - 1))
            out = []
            for q, (pows, pr, pi) in enumerate(setups):
                rc, ic = _pair_cols(q)
                r, i = _tile_scan(bu[:, rc], bu[:, ic], pows, False)
                r, i = _carry_in(r, i, pr, pi, carry[2 * q], carry[2 * q + 1])
                x_ref[rows, rc] = r.astype(bf16)
                x_ref[rows, ic] = i.astype(bf16)
                out += [r[SCAN_TILE - 1:SCAN_TILE, :], i[SCAN_TILE - 1:SCAN_TILE, :]]
            return tuple(out) + (ahead,)

        lax.fori_loop(0, nt, step, tuple(jnp.zeros((1, LANES), f32) for _ in range(2 * SCAN_PAIRS)) + (tile_in(0),))

    src, mat, col, vec, grid = _scan_specs(s, wb)
    return pl.pallas_call(body, name=name, grid=grid, in_specs=[src, mat, vec, vec], out_specs=col,
                          out_shape=jax.ShapeDtypeStruct((s, wb.shape[0] * wb.shape[2]), bf16),
                          compiler_params=_cparams(("parallel",)))(u, wb, ar, ai)


def _scan_bwd(dy, cbt, x, ar, ai, *, name):
    s = dy.shape[0]
    nt = s // SCAN_TILE

    def fold(v):
        out = v[0:8]
        for r in range(8, SCAN_TILE, 8):
            out = out + v[r:r + 8]
        return out

    def body(dy_ref, w_ref, x_ref, ar_ref, ai_ref, g_ref, dar_ref, dai_ref):
        setups = [_scan_setup(ar_ref[q], ai_ref[q], True) for q in range(SCAN_PAIRS)]
        row = lax.broadcasted_iota(jnp.int32, (SCAN_TILE, LANES), 0)
        wv = w_ref[0]

        def tile_in(k):
            return jnp.dot(dy_ref[pl.ds(pl.multiple_of(k * SCAN_TILE, SCAN_TILE), SCAN_TILE), :], wv, preferred_element_type=f32)

        def step(kk, carry):
            k = nt - 1 - kk
            start = pl.multiple_of(k * SCAN_TILE, SCAN_TILE)
            rows = pl.ds(start, SCAN_TILE)
            prev16 = pl.ds(pl.multiple_of(jnp.maximum(start - 16, 0), 16), 16)
            dx = carry[-1]
            ahead = tile_in(jnp.maximum(k - 1, 0))

            def before(cols):
                first = jnp.where(k > 0, x_ref[prev16, cols][15:16, :].astype(f32), 0.0)
                return jnp.where(row == 0, first, pltpu.roll(x_ref[rows, cols].astype(f32), 1, axis=0))

            out = []
            for q, (pows, pr, pi) in enumerate(setups):
                rc, ic = _pair_cols(q)
                cr, ci, acc_r, acc_i = carry[4 * q:4 * q + 4]
                gr, gi = _tile_scan(dx[:, rc], dx[:, ic], pows, True)
                gr, gi = _carry_in(gr, gi, pr, pi, cr, ci)
                g_ref[rows, rc] = gr.astype(bf16)
                g_ref[rows, ic] = gi.astype(bf16)
                xr, xi = before(rc), before(ic)
                out += [gr[0:1, :], gi[0:1, :], acc_r + fold(gr * xr + gi * xi), acc_i + fold(gi * xr - gr * xi)]
            return tuple(out) + (ahead,)

        init = (jnp.zeros((1, LANES), f32), jnp.zeros((1, LANES), f32), jnp.zeros((8, LANES), f32), jnp.zeros((8, LANES), f32))
        res = lax.fori_loop(0, nt, step, init * SCAN_PAIRS + (tile_in(nt - 1),))
        for q in range(SCAN_PAIRS):
            dar_ref[q] = jnp.sum(res[4 * q + 2], axis=0, keepdims=True)
            dai_ref[q] = jnp.sum(res[4 * q + 3], axis=0, keepdims=True)

    src, mat, col, vec, grid = _scan_specs(s, cbt)
    vsh = jax.ShapeDtypeStruct((GROUPS // 2, 1, LANES), f32)
    return pl.pallas_call(body, name=name, grid=grid, in_specs=[src, mat, col, vec, vec],
                          out_specs=[col, vec, vec], out_shape=[jax.ShapeDtypeStruct(x.shape, bf16), vsh, vsh],
                          compiler_params=_cparams(("parallel",)))(dy, cbt, x, ar, ai)


_GELU_C = math.sqrt(2.0 / math.pi)


def _gelu_fwd(y, u, dsk, *, name, ts=512):
    s, d = y.shape
    ts = min(ts, s)

    def body(y_ref, u_ref, d_ref, o_ref):
        o_ref[...] = jax.nn.gelu(y_ref[...] + d_ref[...] * u_ref[...]).astype(bf16)

    row, vec = pl.BlockSpec((ts, d), lambda i: (i, 0)), pl.BlockSpec((1, d), lambda i: (0, 0))
    return pl.pallas_call(body, name=name, grid=(s // ts,), in_specs=[row, row, vec], out_specs=row,
                          out_shape=jax.ShapeDtypeStruct((s, d), bf16), compiler_params=_cparams(("parallel",)))(y, u, dsk)


def _gelu_bwd(dg, y, u, dsk, *, name, ts=512):
    s, d = y.shape
    ts = min(ts, s)

    def body(dg_ref, y_ref, u_ref, d_ref, dy_ref, du_ref, dd_ref):
        @pl.when(pl.program_id(0) == 0)
        def _():
            dd_ref[...] = jnp.zeros_like(dd_ref)

        uv, dv = u_ref[...], d_ref[...]
        z = y_ref[...] + dv * uv
        th = jnp.tanh(_GELU_C * (z + 0.044715 * z * z * z))
        dz = dg_ref[...] * (0.5 * (1.0 + th) + 0.5 * z * (1.0 - th * th) * _GELU_C * (1.0 + 3 * 0.044715 * z * z))
        dy_ref[...] = dz.astype(bf16)
        du_ref[...] = dz * dv
        dd_ref[...] += jnp.sum(dz * uv, axis=0, keepdims=True)

    row, vec = pl.BlockSpec((ts, d), lambda i: (i, 0)), pl.BlockSpec((1, d), lambda i: (0, 0))
    return pl.pallas_call(
        body, name=name, grid=(s // ts,), in_specs=[row, row, row, vec], out_specs=[row, row, vec],
        out_shape=[jax.ShapeDtypeStruct((s, d), bf16), jax.ShapeDtypeStruct((s, d), f32), jax.ShapeDtypeStruct((1, d), f32)],
        compiler_params=_cparams(("arbitrary",)))(dg, y, u, dsk)


def _glu_fwd(x, a, b, *, name, ts=512):
    s, d = x.shape
    ts = min(ts, s)

    def body(x_ref, a_ref, b_ref, o_ref):
        o_ref[...] = x_ref[...] + a_ref[...] * jax.nn.sigmoid(b_ref[...])

    row = pl.BlockSpec((ts, d), lambda i: (i, 0))
    return pl.pallas_call(body, name=name, grid=(s // ts,), in_specs=[row, row, row], out_specs=row,
                          out_shape=jax.ShapeDtypeStruct((s, d), f32), compiler_params=_cparams(("parallel",)))(x, a, b)


def _glu_bwd(dx, a, b, *, name, ts=512):
    s, d = dx.shape
    ts = min(ts, s)

    def body(dx_ref, a_ref, b_ref, da_ref, db_ref):
        sg = jax.nn.sigmoid(b_ref[...])
        dxv = dx_ref[...]
        da_ref[...] = (dxv * sg).astype(bf16)
        db_ref[...] = (dxv * a_ref[...] * sg * (1.0 - sg)).astype(bf16)

    row = pl.BlockSpec((ts, d), lambda i: (i, 0))
    out = jax.ShapeDtypeStruct((s, d), bf16)
    return pl.pallas_call(body, name=name, grid=(s // ts,), in_specs=[row, row, row], out_specs=[row, row],
                          out_shape=[out, out], compiler_params=_cparams(("parallel",)))(dx, a, b)


def _loss_head(y, target, *, name, ts=512):
    s, d = y.shape
    ts = min(ts, s)

    def body(y_ref, t_ref, dy_ref, dyb_ref, l_ref):
        @pl.when(pl.program_id(0) == 0)
        def _():
            l_ref[...] = jnp.zeros_like(l_ref)

        e = y_ref[...] - t_ref[...]
        dy = e * (1.0 / d)
        dy_ref[...] = dy
        dyb_ref[...] = dy.astype(bf16)
        l_ref[...] += 0.5 * jnp.sum(jnp.mean(e * e, axis=-1, keepdims=True))

    row = pl.BlockSpec((ts, d), lambda i: (i, 0))
    return pl.pallas_call(
        body, name=name, grid=(s // ts,), in_specs=[row, row],
        out_specs=[row, row, pl.BlockSpec((8, LANES), lambda i: (0, 0))],
        out_shape=[jax.ShapeDtypeStruct((s, d), f32), jax.ShapeDtypeStruct((s, d), bf16), jax.ShapeDtypeStruct((8, LANES), f32)],
        compiler_params=_cparams(("arbitrary",)))(y, target)


def _adamw(w, g, m, v, *, name, tr=128):
    r, c = w.shape

    def body(w_ref, g_ref, m_ref, v_ref, d_ref, mo_ref, vo_ref):
        gv = g_ref[...]
        mn = ADAM_B1 * m_ref[...] + (1.0 - ADAM_B1) * gv
        vn = ADAM_B2 * v_ref[...] + (1.0 - ADAM_B2) * (gv * gv)
        m_hat = mn / (1.0 - ADAM_B1 ** ADAM_STEP)
        v_hat = vn / (1.0 - ADAM_B2 ** ADAM_STEP)
        d_ref[...] = -ADAM_LR * (m_hat / (jnp.sqrt(v_hat) + ADAM_EPS) + ADAM_WD * w_ref[...])
        mo_ref[...] = mn
        vo_ref[...] = vn

    row = pl.BlockSpec((tr, c), lambda i: (i, 0))
    out = jax.ShapeDtypeStruct((r, c), f32)
    return pl.pallas_call(body, name=name, grid=(r // tr,), in_specs=[row] * 4, out_specs=[row] * 3,
                          out_shape=[out, out, out], compiler_params=_cparams(("parallel",)))(w, g, m, v)


def _sum_slabs(land, *, name, tr=128):
    n, r, c = land.shape

    def body(l_ref, o_ref):
        acc = l_ref[0].astype(f32)
        for i in range(1, n):
            acc = acc + l_ref[i].astype(f32)
        o_ref[...] = acc

    return pl.pallas_call(body, name=name, grid=(r // tr,), in_specs=[pl.BlockSpec((n, tr, c), lambda i: (0, i, 0))],
                          out_specs=pl.BlockSpec((tr, c), lambda i: (i, 0)), out_shape=jax.ShapeDtypeStruct((r, c), f32),
                          compiler_params=_cparams(("parallel",)))(land)


def _pair_sum(g, theirs, *, name, tr=256):
    n, r, c = theirs.shape

    def body(c_ref, g_ref, t_ref, o_ref):
        o_ref[...] = (g_ref[...].astype(f32) + t_ref[...].astype(f32)).astype(bf16)

    blk = pl.BlockSpec((1, tr, c), lambda j, i, c_ref: (j, i, 0))
    mine = pl.BlockSpec((1, tr, c), lambda j, i, c_ref: (2 * j + c_ref[0], i, 0))
    return pl.pallas_call(
        body, name=name,
        grid_spec=pltpu.PrefetchScalarGridSpec(num_scalar_prefetch=1, grid=(n, r // tr), in_specs=[mine, blk], out_specs=blk),
        out_shape=jax.ShapeDtypeStruct(theirs.shape, bf16),
        compiler_params=_cparams(("parallel", "parallel")))(lax.axis_index("c").astype(jnp.int32).reshape(1), g, theirs)


_MESH = pl.DeviceIdType.MESH
_HBM = pl.BlockSpec(memory_space=pltpu.HBM)
N_CHIP = N_DEV // 2


def _position():
    return lax.axis_index("x"), lax.axis_index("y"), lax.axis_index("c")


def _gather8(x, *, name):
    half = x.shape[0] // 2

    def body(x_ref, o_ref, send_sems, recv_sems, local_sem):
        xx, yy, cc = _position()
        me, sibling = (xx, yy, cc), (xx, yy, 1 - cc)
        here, xn, yn, dg = (xx, yy), (1 - xx, yy), (xx, 1 - yy), (1 - xx, 1 - yy)
        first, second = pl.ds(0, half), pl.ds(half, half)

        def slab(chip, pc, rows=None):
            ref = o_ref.at[4 * chip[0] + 2 * chip[1] + pc]
            return ref if rows is None else ref.at[rows]

        def copy(k, ref, to, src=None):
            return pltpu.make_async_remote_copy(src_ref=ref if src is None else src, dst_ref=ref, send_sem=send_sems.at[k],
                                                recv_sem=recv_sems.at[k], device_id=to, device_id_type=_MESH)

        mine = pltpu.make_async_copy(x_ref, slab(here, cc), local_sem)
        mine.start()
        sends = [copy(0, slab(here, cc), sibling, src=x_ref), copy(1, slab(here, cc), (*xn, cc), src=x_ref),
                 copy(2, slab(here, cc), (*yn, cc), src=x_ref)]
        for cp in sends:
            cp.start()
        copy(1, slab(xn, cc), me).wait_recv()
        sends += [copy(3, slab(xn, cc, first), (*yn, cc)), copy(5, slab(xn, cc), sibling)]
        copy(2, slab(yn, cc), me).wait_recv()
        sends += [copy(4, slab(yn, cc, second), (*xn, cc)), copy(6, slab(yn, cc), sibling)]
        for cp in sends[3:]:
            cp.start()
        copy(3, slab(dg, cc, first), me).wait_recv()
        copy(4, slab(dg, cc, second), me).wait_recv()
        sends.append(copy(7, slab(dg, cc), sibling))
        sends[-1].start()
        for k, chip in ((0, here), (5, xn), (6, yn), (7, dg)):
            copy(k, slab(chip, 1 - cc), me).wait_recv()
        for cp in sends:
            cp.wait_send()
        mine.wait()

    return pl.pallas_call(
        body, name=name, in_specs=[_HBM], out_specs=_HBM, out_shape=jax.ShapeDtypeStruct((N_DEV,) + x.shape, x.dtype),
        scratch_shapes=[pltpu.SemaphoreType.DMA((N_DEV,)), pltpu.SemaphoreType.DMA((N_DEV,)), pltpu.SemaphoreType.DMA],
    )(x)


def _pair_exchange(g, *, name):
    def body(g_ref, land_ref, send_sems, recv_sems):
        xx, yy, cc = _position()
        copies = []
        for j in range(N_CHIP):
            cp = pltpu.make_async_remote_copy(src_ref=g_ref.at[2 * j + 1 - cc], dst_ref=land_ref.at[j], send_sem=send_sems.at[j],
                                              recv_sem=recv_sems.at[j], device_id=(xx, yy, 1 - cc), device_id_type=_MESH)
            cp.start()
            copies.append(cp)
        for cp in copies:
            cp.wait_recv()
        for cp in copies:
            cp.wait_send()

    sems = pltpu.SemaphoreType.DMA((N_CHIP,))
    return pl.pallas_call(body, name=name, in_specs=[_HBM], out_specs=_HBM,
                          out_shape=jax.ShapeDtypeStruct((N_CHIP,) + g.shape[1:], g.dtype), scratch_shapes=[sems, sems])(g)


def _cross_exchange(p, *, name):
    half = p.shape[1] // 2

    def body(p_ref, o_ref, relay_ref, send_sems, recv_sems, local_sem):
        xx, yy, cc = _position()
        me = (xx, yy, cc)
        xn, yn, dg = (1 - xx, yy), (xx, 1 - yy), (1 - xx, 1 - yy)
        idx = lambda chip: 2 * chip[0] + chip[1]
        mine = idx((xx, yy))
        first, second = pl.ds(0, half), pl.ds(half, half)

        def copy(k, src, dst, to):
            return pltpu.make_async_remote_copy(src_ref=src, dst_ref=dst, send_sem=send_sems.at[k], recv_sem=recv_sems.at[k],
                                                device_id=to, device_id_type=_MESH)

        local = pltpu.make_async_copy(p_ref.at[mine], o_ref.at[mine], local_sem)
        local.start()
        sends = [copy(0, p_ref.at[idx(xn)], o_ref.at[mine], (*xn, cc)),
                 copy(1, p_ref.at[idx(dg)].at[first], relay_ref.at[0], (*xn, cc)),
                 copy(2, p_ref.at[idx(yn)], o_ref.at[mine], (*yn, cc)),
                 copy(3, p_ref.at[idx(dg)].at[second], relay_ref.at[1], (*yn, cc))]
        for cp in sends:
            cp.start()
        copy(1, relay_ref.at[0], relay_ref.at[0], me).wait_recv()
        sends.append(copy(4, relay_ref.at[0], o_ref.at[idx(xn)].at[first], (*yn, cc)))
        sends[-1].start()
        copy(3, relay_ref.at[1], relay_ref.at[1], me).wait_recv()
        sends.append(copy(5, relay_ref.at[1], o_ref.at[idx(yn)].at[second], (*xn, cc)))
        sends[-1].start()
        for k, dst in ((0, o_ref.at[idx(xn)]), (2, o_ref.at[idx(yn)]), (4, o_ref.at[idx(dg)].at[first]),
                       (5, o_ref.at[idx(dg)].at[second])):
            copy(k, dst, dst, me).wait_recv()
        for cp in sends:
            cp.wait_send()
        local.wait()

    sems = pltpu.SemaphoreType.DMA((6,))
    relay = jax.ShapeDtypeStruct((2, half) + p.shape[2:], p.dtype)
    return pl.pallas_call(body, name=name, in_specs=[_HBM], out_specs=[_HBM, _HBM],
                          out_shape=[jax.ShapeDtypeStruct(p.shape, p.dtype), relay],
                          scratch_shapes=[sems, sems, pltpu.SemaphoreType.DMA])(p)[0]


def _all_sum(x, *, name):
    return _sum_slabs(_gather8(x, name=f"gather_{name}"), name=f"sum_{name}", tr=min(128, x.shape[0]))


def _pack_slabs(parts, rows, axis=0):
    lead = parts[0].shape[:axis]
    slabs = [p.reshape(lead + (-1, D)) for p in parts]
    used = sum(sl.shape[axis] for sl in slabs)
    return jnp.concatenate(slabs + [jnp.zeros(lead + (rows - used, D), slabs[0].dtype)], axis=axis)


def _unpack_slabs(slab, shapes):
    lead, out, off = slab.shape[:-2], [], 0
    for shp in shapes:
        n = int(np.prod(shp)) // D
        out.append(slab[..., off:off + n, :].reshape(lead + tuple(shp)))
        off += n
    return out


def _pack_rows(parts, rows):
    flat = jnp.concatenate([p.reshape(-1) for p in parts])
    return jnp.pad(flat, (0, rows * D - flat.shape[0])).reshape(rows, D)


def _unpack_rows(slab, shapes):
    flat, out, off = slab.reshape(-1), [], 0
    for shp in shapes:
        n = int(np.prod(shp))
        out.append(flat[off:off + n].reshape(shp))
        off += n
    return out


def _full_shape(shard, axis):
    return tuple(d * N_DEV if i == axis else d for i, d in enumerate(shard))


def _row(v):
    return v.reshape(1, -1).astype(f32)


def _pad_gain(g):
    return jnp.pad(g.astype(f32), (0, HEAD_PAD - QK)).reshape(1, HEAD_PAD)


def _ffn_fwd(x, p, tag):
    h = _rms_fwd(x, p["norm"], name=f"ffn_norm_{tag}")
    zg = _mm(h, p["wgT"], tb=True, tn=FFN_H, name=f"ffn_up_g_{tag}")
    zv = _mm(h, p["wvT"], tb=True, tn=FFN_H, name=f"ffn_up_v_{tag}")
    a = _ffn_act_fwd(zg, zv, p["cwg"], p["cwv"], name=f"ffn_act_{tag}")
    y = _mm(a, p["wd"], add=x, name=f"ffn_down_{tag}")
    return y, (x, h, zg, zv, a)


def _ffn_bwd(dy, dyb, p, saved, tag):
    x, h, zg, zv, a = saved
    g = {}
    da = _mm(dyb, p["wd"], tb=True, out_dtype=bf16, name=f"ffn_down_dx_{tag}")
    g["wd"] = _mm(a, dyb, ta=True, out_dtype=bf16, name=f"ffn_down_dw_{tag}")
    dzg, dzv, g["cwg"], g["cwv"] = _ffn_act_bwd(da, zg, zv, p["cwg"], p["cwv"], name=f"ffn_act_bwd_{tag}")
    g["wgT"] = _mm(dzg, h, ta=True, out_dtype=bf16, name=f"ffn_up_g_dw_{tag}")
    g["wvT"] = _mm(dzv, h, ta=True, out_dtype=bf16, name=f"ffn_up_v_dw_{tag}")
    dh = _mm(dzg, p["wgT"], name=f"ffn_up_g_dx_{tag}")
    dh = _mm(dzv, p["wvT"], add=dh, name=f"ffn_up_v_dx_{tag}")
    dx, dxb, g["norm"] = _rms_bwd(dh, x, p["norm"], res=dy, name=f"ffn_norm_bwd_{tag}")
    return dx, dxb, g


def _mla_fwd(x, p, tabs, tag):
    cos_t, sin_t = tabs
    h = _rms_fwd(x, p["norm"], name=f"attn_norm_{tag}")
    proj = _mm(h, p["w_inT"], tb=True, name=f"mix_in_{tag}")
    cqn = _rms_fwd(proj, p["cq_norm"], col=0, name=f"cq_norm_{tag}")
    ckvn = _rms_fwd(proj, p["ckv_norm"], col=1, name=f"ckv_norm_{tag}")
    q_raw = _mm(cqn, p["w_uqT"], tb=True, name=f"uq_{tag}")
    kv_raw = _mm(ckvn, p["w_ukvT"], tb=True, name=f"ukv_{tag}")
    q, k, v = _qk_prep_fwd(q_raw, kv_raw, proj, p["q_gain"], p["k_gain"], cos_t, sin_t, name=f"qk_prep_{tag}")
    o, lse = _flash_fwd(q, k, v, name=f"flash_fwd_{tag}")
    conv = _sconv_fwd(proj, p["sconv_w"], name=f"sconv_{tag}")
    y = _mm(conv, p["w_out"][HEADS * HEAD_PAD:], add=x, name=f"mix_out_conv_{tag}")
    y = _mm(o, p["w_out"][:HEADS * HEAD_PAD], add=y, name=f"mix_out_{tag}")
    return y, (x, h, proj, cqn, ckvn, q_raw, kv_raw, q, k, v, o, lse, conv)


def _mla_bwd(dy, dyb, p, tabs, saved, tag):
    cos_t, sin_t = tabs
    x, h, proj, cqn, ckvn, q_raw, kv_raw, q, k, v, o, lse, conv = saved
    s = x.shape[0]
    g = {}
    dmix = _mm(dyb, p["w_out"], tb=True, name=f"mix_out_dx_{tag}")
    g["w_out"] = jnp.concatenate([_mm(o, dyb, ta=True, out_dtype=bf16, name=f"mix_out_dw_{tag}"),
                                  _mm(conv, dyb, ta=True, out_dtype=bf16, name=f"mix_out_conv_dw_{tag}")], axis=0)
    dgb, dgc, dci, g["sconv_w"] = _sconv_bwd(dmix, proj, p["sconv_w"], name=f"sconv_bwd_{tag}")
    dq, delta = _flash_bwd_dq(q, k, v, o, dmix, lse, name=f"flash_dq_{tag}")
    dk, dv = _flash_bwd_dkv(q, k, v, dmix, lse.reshape(HEADS, 1, s), delta.reshape(HEADS, 1, s), name=f"flash_dkv_{tag}")
    dq_raw, dkv_raw, dkr, g["q_gain"], g["k_gain"] = _qk_prep_bwd(
        dq, dk, dv, q_raw, kv_raw, proj, p["q_gain"], p["k_gain"], cos_t, sin_t, name=f"qk_prep_bwd_{tag}")
    dcqn = _mm(dq_raw, p["w_uqT"], name=f"uq_dx_{tag}")
    g["w_uqT"] = _mm(dq_raw, cqn, ta=True, out_dtype=bf16, name=f"uq_dw_{tag}")
    dckvn = _mm(dkv_raw, p["w_ukvT"], name=f"ukv_dx_{tag}")
    g["w_ukvT"] = _mm(dkv_raw, ckvn, ta=True, out_dtype=bf16, name=f"ukv_dw_{tag}")
    dcq, g["cq_norm"] = _rms_bwd(dcqn, proj, p["cq_norm"], col=0, out_dtype=bf16, name=f"cq_norm_bwd_{tag}")
    dckv, g["ckv_norm"] = _rms_bwd(dckvn, proj, p["ckv_norm"], col=1, out_dtype=bf16, name=f"ckv_norm_bwd_{tag}")
    dproj = jnp.concatenate([dcq, dckv, dgb, dgc, dci, dkr.astype(bf16)], axis=1)
    dh = _mm(dproj, p["w_inT"], name=f"mix_in_dx_{tag}")
    g["w_inT"] = _mm(dproj, h, ta=True, out_dtype=bf16, name=f"mix_in_dw_{tag}")
    dx, dxb, g["norm"] = _rms_bwd(dh, x, p["norm"], res=dy, name=f"attn_norm_bwd_{tag}")
    return dx, dxb, g


def _block_diag(wg):
    nb, ng, r, c = wg.shape
    eye = jnp.eye(ng, dtype=wg.dtype)
    return (wg[:, :, :, None, :] * eye[None, :, None, :, None]).reshape(nb, ng * r, ng * c)


def _s5_mats(bbr, bbi, c_re, c_im):
    nb = GROUPS // 8
    b4 = jnp.stack([bbr.reshape(GROUPS, STATE, GROUP), bbi.reshape(GROUPS, STATE, GROUP)], axis=1)
    wg = jnp.transpose(b4, (0, 3, 1, 2)).reshape(nb, 8, GROUP, 2 * STATE)
    cg = jnp.stack([c_re, -c_im], axis=1)
    cg = jnp.transpose(cg, (0, 1, 3, 2)).reshape(nb, 8, 2 * STATE, GROUP)
    return _state_layout(_block_diag(wg), 2), _state_layout(_block_diag(cg), 1)


def _state_layout(m, axis):
    shp = m.shape
    m = m.reshape(shp[:axis] + (4, 2, 2, STATE) + shp[axis + 1:])
    return jnp.swapaxes(m, axis + 1, axis + 2).reshape(shp)


def _group_blocks(d):
    d = d.reshape(GROUPS // 2, 2, GROUP, 2, 2, STATE)
    return jnp.stack([d[:, 0, :, :, 0, :], d[:, 1, :, :, 1, :]], axis=1).reshape(GROUPS, GROUP, 2, STATE)


def _s5_fwd(x, p, tag):
    h = _rms_fwd(x, p["norm"], name=f"ssm_norm_{tag}")
    u, ub = _mm(h, p["w_in"], twin=True, name=f"ssm_in_{tag}")
    ar, ai, bbr, bbi = _disc_fwd(p["lr"], p["li"], p["ls"], p["br"], p["bi"], name=f"disc_{tag}")
    wb, cb = _s5_mats(bbr, bbi, p["c_re"], p["c_im"])
    a1, a2 = ar.reshape(GROUPS // 2, 1, LANES), ai.reshape(GROUPS // 2, 1, LANES)
    xs = _scan_fwd(ub, wb.astype(bf16), a1, a2, name=f"ssm_scan_{tag}")
    y = _bd_nn(xs, cb.astype(bf16), name=f"ssm_y_{tag}")
    g = _gelu_fwd(y, u, p["d_skip"], name=f"ssm_gelu_{tag}")
    a = _mm(g, p["wgaT"], tb=True, name=f"glu_a_{tag}")
    b = _mm(g, p["wgbT"], tb=True, name=f"glu_b_{tag}")
    out = _glu_fwd(x, a, b, name=f"glu_{tag}")
    return out, (x, h, u, ub, wb, cb, a1, a2, xs, y, g, a, b)


def _s5_bwd(dout, p, saved, tag):
    x, h, u, ub, wb, cb, a1, a2, xs, y, g, a, b = saved
    gr = {}
    da, db = _glu_bwd(dout, a, b, name=f"glu_bwd_{tag}")
    dg = _mm(da, p["wgaT"], name=f"glu_a_dx_{tag}")
    dg = _mm(db, p["wgbT"], add=dg, name=f"glu_b_dx_{tag}")
    gr["wgaT"] = _mm(da, g, ta=True, out_dtype=bf16, name=f"glu_a_dw_{tag}")
    gr["wgbT"] = _mm(db, g, ta=True, out_dtype=bf16, name=f"glu_b_dw_{tag}")
    dy, du1, gr["d_skip"] = _gelu_bwd(dg, y, u, p["d_skip"], name=f"ssm_gelu_bwd_{tag}")
    dct = _group_blocks(_bd_tn_diag(dy, xs, name=f"ssm_y_dw_{tag}"))
    gs, dar, dai = _scan_bwd(dy, jnp.swapaxes(cb, 1, 2).astype(bf16), xs, a1, a2, name=f"ssm_scan_bwd_{tag}")
    du = _bd_nn(gs, jnp.swapaxes(wb, 1, 2).astype(bf16), add=du1, out_dtype=bf16, name=f"ssm_bu_dx_{tag}")
    dwg = _group_blocks(_bd_tn_diag(ub, gs, name=f"ssm_bu_dw_{tag}"))
    dh = _mm(du, p["w_in"], tb=True, name=f"ssm_in_dx_{tag}")
    gr["w_in"] = _mm(h, du, ta=True, out_dtype=bf16, name=f"ssm_in_dw_{tag}")
    dx, dxb, gr["norm"] = _rms_bwd(dh, x, p["norm"], res=dout, name=f"ssm_norm_bwd_{tag}")
    dbb = jnp.transpose(dwg, (2, 0, 3, 1)).reshape(2, GROUPS, STATE * GROUP)
    gr["c_re"] = dct[:, :, 0, :]
    gr["c_im"] = -dct[:, :, 1, :]
    dlr, dli, dls, dbr, dbi = _disc_bwd(p["lr"], p["li"], p["ls"], p["br"], p["bi"], dar.reshape(GROUPS, STATE),
                                        dai.reshape(GROUPS, STATE), dbb[0], dbb[1], name=f"disc_bwd_{tag}")
    gr["lr"], gr["li"], gr["ls"] = dlr, dli, dls.reshape(GROUPS)
    gr["br"], gr["bi"] = dbr.reshape(GROUPS, STATE, GROUP), dbi.reshape(GROUPS, STATE, GROUP)
    return dx, dxb, gr


def _slab_shape(shard, axis):
    return (shard[0], shard[2], shard[1]) if axis == 2 else shard


def _to_slab(w, axis):
    return jnp.swapaxes(w, 1, 2) if axis == 2 else w


def _mix_in_pad(wt):
    z = lambda n: jnp.zeros((n, wt.shape[1]), wt.dtype)
    return jnp.concatenate([wt[:512], wt[544:2080], z(NOPE), wt[512:544], z(HEAD_PAD - QK)], axis=0)


def _mix_in_unpad(g):
    return jnp.concatenate([g[:512], g[2048 + NOPE:2048 + QK], g[512:2048]], axis=0)


def _mix_out_pad(w):
    att = jnp.pad(w[:512].reshape(HEADS, NOPE, D), ((0, 0), (NOPE, 0), (0, 0))).reshape(HEADS * HEAD_PAD, D)
    return jnp.concatenate([att, w[512:]], axis=0)


def _mix_out_unpad(g):
    att = g[:HEADS * HEAD_PAD].reshape(HEADS, HEAD_PAD, D)[:, NOPE:, :].reshape(HEADS * NOPE, D)
    return jnp.concatenate([att, g[HEADS * HEAD_PAD:]], axis=0)


def _layer_params(wl, ws, layer):
    i = layer // 2
    half = N_DEV // 2
    up = wl["ffn_w_up"][layer]
    ffn = dict(norm=_row(ws["ffn_norm"][layer]), wgT=up[:half].reshape(FFN_H, D), wvT=up[half:].reshape(FFN_H, D),
               cwg=ws["ffn_conv_w"][layer][:, :FFN_H], cwv=ws["ffn_conv_w"][layer][:, FFN_H:],
               wd=wl["ffn_w_down"][layer].reshape(FFN_H, D))
    if layer % 2 == 0:
        uq = jnp.pad(wl["w_uq"][i], ((0, 0), (0, HEAD_PAD - QK), (0, 0)))
        mixer = dict(norm=_row(ws["attn_norm"][i]), w_inT=_mix_in_pad(wl["mix_w_in"][i].reshape(-1, D)),
                     cq_norm=_row(ws["cq_norm"][i]), ckv_norm=_row(ws["ckv_norm"][i]),
                     w_uqT=uq.reshape(HEADS * HEAD_PAD, LORA), w_ukvT=wl["w_ukv"][i].reshape(HEADS * HEAD_PAD, LORA),
                     q_gain=_pad_gain(ws["q_gain"][i]), k_gain=_pad_gain(ws["k_gain"][i]), sconv_w=ws["sconv_w"][i],
                     w_out=_mix_out_pad(wl["mix_w_out"][i].reshape(D, D)))
    else:
        glu = wl["w_glu"][i]
        mixer = dict(norm=_row(ws["ssm_norm"][i]), w_in=wl["ssm_w_in"][i].reshape(D, D), lr=ws["lambda_re"][i],
                     li=ws["lambda_im"][i], ls=ws["log_step"][i].reshape(GROUPS, 1),
                     br=ws["b_re"][i].reshape(GROUPS, STATE * GROUP), bi=ws["b_im"][i].reshape(GROUPS, STATE * GROUP),
                     c_re=ws["c_re"][i], c_im=ws["c_im"][i], d_skip=_row(ws["d_skip"][i]),
                     wgaT=glu[:half].reshape(D, D), wgbT=glu[half:].reshape(D, D))
    return mixer, ffn


def _collect_grads(gm, gf):
    ev, od, half = (0, 2), (1, 3), N_DEV // 2
    st = lambda xs: jnp.stack(xs, axis=0)
    per_dev = list
    halves = lambda a, b, rows: jnp.concatenate([a.reshape(half, rows, D), b.reshape(half, rows, D)], axis=0)
    big = {
        "ffn_w_up": per_dev([halves(gf[l]["wgT"], gf[l]["wvT"], FFN_H // half) for l in range(4)]),
        "ffn_w_down": per_dev([gf[l]["wd"].reshape(N_DEV, -1, D) for l in range(4)]),
        "w_glu": per_dev([halves(gm[l]["wgaT"], gm[l]["wgbT"], D // half) for l in od]),
        "mix_w_out": per_dev([_mix_out_unpad(gm[l]["w_out"]).reshape(N_DEV, -1, D) for l in ev]),
        "ssm_w_in": per_dev([gm[l]["w_in"].reshape(N_DEV, -1, D) for l in od]),
        "w_ukv": per_dev([gm[l]["w_ukvT"].reshape(N_DEV, HEAD_PAD, LORA) for l in ev]),
        "w_uq": per_dev([gm[l]["w_uqT"].reshape(N_DEV, HEAD_PAD, LORA)[:, :QK] for l in ev]),
        "mix_w_in": per_dev([_mix_in_unpad(gm[l]["w_inT"]).reshape(N_DEV, -1, D) for l in ev]),
    }
    small = {
        "attn_norm": st([gm[l]["norm"].reshape(D) for l in ev]),
        "cq_norm": st([gm[l]["cq_norm"].reshape(LORA) for l in ev]),
        "ckv_norm": st([gm[l]["ckv_norm"].reshape(LORA) for l in ev]),
        "q_gain": st([gm[l]["q_gain"].reshape(HEAD_PAD)[:QK] for l in ev]),
        "k_gain": st([gm[l]["k_gain"].reshape(HEAD_PAD)[:QK] for l in ev]),
        "sconv_w": st([gm[l]["sconv_w"] for l in ev]),
        "ssm_norm": st([gm[l]["norm"].reshape(D) for l in od]),
        "lambda_re": st([gm[l]["lr"] for l in od]), "lambda_im": st([gm[l]["li"] for l in od]),
        "log_step": st([gm[l]["ls"] for l in od]),
        "b_re": st([gm[l]["br"] for l in od]), "b_im": st([gm[l]["bi"] for l in od]),
        "c_re": st([gm[l]["c_re"] for l in od]), "c_im": st([gm[l]["c_im"] for l in od]),
        "d_skip": st([gm[l]["d_skip"].reshape(D) for l in od]),
        "ffn_norm": st([gf[l]["norm"].reshape(D) for l in range(4)]),
        "ffn_conv_w": st([jnp.concatenate([gf[l]["cwg"], gf[l]["cwv"]], axis=1) for l in range(4)]),
    }
    return big, small


def _local_step(x, target, wl, ws):
    s = x.shape[0]
    tabs = _rope_tables(s)
    saved, params = [], []
    for layer in range(4):
        mixer, ffn = _layer_params(wl, ws, layer)
        params.append((mixer, ffn))
        if layer % 2 == 0:
            x, sm = _mla_fwd(x, mixer, tabs, f"l{layer}")
        else:
            x, sm = _s5_fwd(x, mixer, f"l{layer}")
        x, sf = _ffn_fwd(x, ffn, f"l{layer}")
        saved.append((sm, sf))
    dx, dxb, loss = _loss_head(x, target, name="loss_head")
    gm, gf = [None] * 4, [None] * 4
    for layer in reversed(range(4)):
        mixer, ffn = params[layer]
        sm, sf = saved[layer]
        dx, dxb, gf[layer] = _ffn_bwd(dx, dxb, ffn, sf, f"l{layer}")
        if layer % 2 == 0:
            dx, dxb, gm[layer] = _mla_bwd(dx, dxb, mixer, tabs, sm, f"l{layer}")
        else:
            dx, dxb, gm[layer] = _s5_bwd(dx, mixer, sm, f"l{layer}")
    return loss, dx, _collect_grads(gm, gf)


def kernel(x, attn_norm, mix_w_in, cq_norm, ckv_norm, w_uq, w_ukv, q_gain, k_gain, sconv_w, mix_w_out, ssm_norm, ssm_w_in, lambda_re, lambda_im, log_step, b_re, b_im, c_re, c_im, d_skip, w_glu, ffn_norm, ffn_w_up, ffn_conv_w, ffn_w_down, loss_target, m_attn_norm, m_mix_w_in, m_cq_norm, m_ckv_norm, m_w_uq, m_w_ukv, m_q_gain, m_k_gain, m_sconv_w, m_mix_w_out, m_ssm_norm, m_ssm_w_in, m_lambda_re, m_lambda_im, m_log_step, m_b_re, m_b_im, m_c_re, m_c_im, m_d_skip, m_w_glu, m_ffn_norm, m_ffn_w_up, m_ffn_conv_w, m_ffn_w_down, v_attn_norm, v_mix_w_in, v_cq_norm, v_ckv_norm, v_w_uq, v_w_ukv, v_q_gain, v_k_gain, v_sconv_w, v_mix_w_out, v_ssm_norm, v_ssm_w_in, v_lambda_re, v_lambda_im, v_log_step, v_b_re, v_b_im, v_c_re, v_c_im, v_d_skip, v_w_glu, v_ffn_norm, v_ffn_w_up, v_ffn_conv_w, v_ffn_w_down):
    args = dict(locals())
    wsh = {n: args[n] for n in WEIGHTS}
    msh = {n: args["m_" + n] for n in WEIGHTS}
    vsh = {n: args["v_" + n] for n in WEIGHTS}
    me = 4 * lax.axis_index("x") + 2 * lax.axis_index("y") + lax.axis_index("c")
    big_names = [n for n, _, _ in BIG]
    slab_shapes = [_slab_shape(sh, ax) for _, sh, ax in BIG]
    small_names = [n for n, _ in REPL] + [n for n, _, _ in SMALL]

    mine = _pack_slabs([_to_slab(wsh[n], ax).astype(bf16) for n, _, ax in BIG], BIG_ROWS)
    gathered, wl, off = _gather8(mine, name="gather_weights"), {}, 0
    for n, (layers, rows, inner) in zip(big_names, slab_shapes):
        per = rows * inner // D
        wl[n] = [gathered[:, off + l * per:off + (l + 1) * per, :].reshape(N_DEV, rows, inner) for l in range(layers)]
        off += layers * per
    placed = []
    for n, shard, axis in SMALL:
        start = [0] * len(shard)
        start[axis] = me * shard[axis]
        placed.append(lax.dynamic_update_slice(jnp.zeros(_full_shape(shard, axis), f32), wsh[n], start))
    small_all = _all_sum(_pack_rows(placed, SMALL_FWD_ROWS), name="small_params")
    ws = dict(zip([n for n, _, _ in SMALL], _unpack_rows(small_all, [_full_shape(sh, ax) for _, sh, ax in SMALL])))
    ws.update({n: wsh[n] for n, _ in REPL})

    loss8, grad_x, (big_grads, grads) = _local_step(x[0], loss_target[0], wl, ws)

    pieces = []
    for n in big_names:
        layers = big_grads[n]
        tiled = (layers[0].shape[1] * layers[0].shape[2] // D) % 16 == 0
        pieces += layers if tiled else [jnp.stack(layers, axis=1)]
    contrib = _pack_slabs(pieces, BIG_ROWS, axis=1)
    chip_sum = _pair_sum(contrib, _pair_exchange(contrib, name="grads_pair_exchange"), name="grads_pair_sum")
    g_big = _sum_slabs(_cross_exchange(chip_sum, name="grads_cross_exchange"), name="grads_chip_sum")
    small_vec = _pack_rows([grads[n] for n, _ in REPL] + [grads[n] for n, _, _ in SMALL] + [loss8[0, :1]], SMALL_ROWS)
    small_sum = _all_sum(small_vec, name="small_grads")
    parts = _unpack_rows(small_sum, [sh for _, sh in REPL] + [_full_shape(sh, ax) for _, sh, ax in SMALL] + [(1,)])
    g = {n: val for (n, _), val in zip(REPL, parts)}
    for (n, shard, axis), val in zip(SMALL, parts[len(REPL):]):
        start = [0] * len(shard)
        start[axis] = me * shard[axis]
        g[n] = lax.dynamic_slice(val, start, shard)
    loss = parts[-1].reshape(())
    for (n, _, axis), val in zip(BIG, _unpack_slabs(g_big, slab_shapes)):
        g[n] = _to_slab(val, axis)

    delta, new_m, new_v = {}, {}, {}
    for n, shard, _ in BIG:
        flat = lambda a: a.reshape(-1, shard[-1])
        outs = _adamw(flat(wsh[n]), flat(g[n]), flat(msh[n]), flat(vsh[n]), name=f"adamw_{n}")
        delta[n], new_m[n], new_v[n] = [o.reshape(shard) for o in outs]
    small_state = [_pack_rows([src[n] for n in small_names], SMALL_ROWS) for src in (wsh, g, msh, vsh)]
    for dst, slab in zip((delta, new_m, new_v), _adamw(*small_state, name="adamw_small")):
        dst.update(zip(small_names, _unpack_rows(slab, [wsh[n].shape for n in small_names])))

    return (loss, grad_x[None], *[g[n] for n in WEIGHTS], *[delta[n] for n in WEIGHTS],
            *[new_m[n] for n in WEIGHTS], *[new_v[n] for n in WEIGHTS])
```

```python
import math

import numpy as np
import jax
import jax.numpy as jnp
from jax import lax
from jax.experimental import pallas as pl
from jax.experimental.pallas import tpu as pltpu

f32, bf16 = jnp.float32, jnp.bfloat16

N_DEV = 8
D = 1024
HEADS = 8
NOPE, ROPE, QK = 64, 32, 96
HEAD_PAD = 128
LORA = 256
CONV_CH = 512
MIX_IN_PAD = 2176
FFN_H = 2816
GROUPS, GROUP, STATE = 64, 16, 64
EPS = 1e-6
ROPE_THETA = 10000.0
ADAM_LR, ADAM_B1, ADAM_B2, ADAM_EPS, ADAM_WD, ADAM_STEP = 0.001, 0.9, 0.999, 1e-08, 0.01, 10
LANES = 128
PAIR_LANES = 2 * LANES
VMEM_LIMIT = 56 << 20
MM_VMEM_BUDGET = 40 << 20
NEG = -1e30

BIG = (
    ("ffn_w_up", (4, 1024, 704), 2), ("ffn_w_down", (4, 352, 1024), 1), ("w_glu", (2, 1024, 256), 2),
    ("mix_w_out", (2, 128, 1024), 1), ("ssm_w_in", (2, 128, 1024), 1), ("w_ukv", (2, 256, 128), 2),
    ("w_uq", (2, 256, 96), 2), ("mix_w_in", (2, 1024, 260), 2))
REPL = (("attn_norm", (2, 1024)), ("cq_norm", (2, 256)), ("ckv_norm", (2, 256)), ("q_gain", (2, 96)),
        ("k_gain", (2, 96)), ("lambda_re", (2, 64, 64)), ("lambda_im", (2, 64, 64)), ("log_step", (2, 64)),
        ("b_re", (2, 64, 64, 16)), ("b_im", (2, 64, 64, 16)), ("c_re", (2, 64, 16, 64)), ("c_im", (2, 64, 16, 64)),
        ("ffn_norm", (4, 1024)))
SMALL = (("sconv_w", (2, 3, 64), 2), ("ssm_norm", (2, 128), 1), ("d_skip", (2, 128), 1), ("ffn_conv_w", (4, 3, 704), 2))
WEIGHTS = ['attn_norm', 'mix_w_in', 'cq_norm', 'ckv_norm', 'w_uq', 'w_ukv', 'q_gain', 'k_gain', 'sconv_w', 'mix_w_out',
           'ssm_norm', 'ssm_w_in', 'lambda_re', 'lambda_im', 'log_step', 'b_re', 'b_im', 'c_re', 'c_im', 'd_skip',
           'w_glu', 'ffn_norm', 'ffn_w_up', 'ffn_conv_w', 'ffn_w_down']
BIG_ROWS = 5888
SMALL_FWD_ROWS = 80
SMALL_ROWS = 640


def _cparams(sem=None, **kw):
    return pltpu.CompilerParams(dimension_semantics=sem, vmem_limit_bytes=VMEM_LIMIT, **kw)


def _tile(n, target):
    best = 0
    for t in range(LANES, min(n, target) + 1, LANES):
        if n % t == 0:
            best = t
    return best if best else n


def _mm(a, b, *, ta=False, tb=False, out_dtype=f32, add=None, twin=False, name, tm=1024, tn=1536):
    m, k = (a.shape[1], a.shape[0]) if ta else a.shape
    n = b.shape[0] if tb else b.shape[1]
    assert (b.shape[1] if tb else b.shape[0]) == k
    tm = _tile(m, tm)
    tn_ = _tile(n, tn)
    tn = n if (tn_ < 256 and n <= 2304) else tn_

    def vmem_bytes(t):
        io = 2 * (tm * t * a.dtype.itemsize + t * tn * b.dtype.itemsize + tm * tn * (jnp.dtype(out_dtype).itemsize + 2 * twin))
        return io + (2 * tm * tn * 4 if add is not None else 0) + (tm * tn * 4 if t < k else 0)

    tk = next((t for t in [k] + [t for t in range(k - LANES, 0, -LANES) if k % t == 0] if vmem_bytes(t) <= MM_VMEM_BUDGET), LANES)
    nk = k // tk
    dn = (((0 if ta else 1,), (1 if tb else 0,)), ((), ()))

    def body(*refs):
        a_ref, b_ref = refs[:2]
        add_ref = refs[2] if add is not None else None
        o_ref = refs[3] if add is not None else refs[2]
        twin_ref = refs[4 if add is not None else 3] if twin else None
        part = lax.dot_general(a_ref[...].astype(bf16), b_ref[...].astype(bf16), dn, preferred_element_type=f32)

        def finish(r):
            if add is not None:
                r = r + add_ref[...].astype(f32)
            o_ref[...] = r.astype(out_dtype)
            if twin:
                twin_ref[...] = r.astype(bf16)

        if nk == 1:
            finish(part)
            return
        acc = refs[-1]
        kk = pl.program_id(2)

        @pl.when(kk == 0)
        def _():
            acc[...] = part

        @pl.when(kk > 0)
        def _():
            acc[...] += part

        @pl.when(kk == nk - 1)
        def _():
            finish(acc[...])

    a_spec = pl.BlockSpec((tk, tm), lambda i, j, kk: (kk, i)) if ta else pl.BlockSpec((tm, tk), lambda i, j, kk: (i, kk))
    b_spec = pl.BlockSpec((tn, tk), lambda i, j, kk: (j, kk)) if tb else pl.BlockSpec((tk, tn), lambda i, j, kk: (kk, j))
    in_specs, args = [a_spec, b_spec], [a, b]
    if add is not None:
        in_specs.append(pl.BlockSpec((tm, tn), lambda i, j, kk: (i, j)))
        args.append(add)
    o_spec, o_shape = pl.BlockSpec((tm, tn), lambda i, j, kk: (i, j)), jax.ShapeDtypeStruct((m, n), out_dtype)
    return pl.pallas_call(
        body, name=name, grid=(m // tm, n // tn, nk), in_specs=in_specs,
        out_specs=[o_spec, o_spec] if twin else o_spec,
        out_shape=[o_shape, jax.ShapeDtypeStruct((m, n), bf16)] if twin else o_shape,
        scratch_shapes=[pltpu.VMEM((tm, tn), f32)] if nk > 1 else [],
        compiler_params=_cparams(("parallel", "parallel", "arbitrary")))(*args)


def _bd_nn(a, w, *, out_dtype=f32, add=None, name, ts=512):
    s = a.shape[0]
    nb, ka, no = w.shape
    ts = min(ts, s)

    def body(a_ref, w_ref, *rest):
        r = jnp.dot(a_ref[...].astype(bf16), w_ref[0].astype(bf16), preferred_element_type=f32)
        if add is not None:
            r = r + rest[0][...].astype(f32)
        rest[-1][...] = r.astype(out_dtype)

    o_spec = pl.BlockSpec((ts, no), lambda b, i: (i, b))
    return pl.pallas_call(
        body, name=name, grid=(nb, s // ts),
        in_specs=[pl.BlockSpec((ts, ka), lambda b, i: (i, b)), pl.BlockSpec((1, ka, no), lambda b, i: (b, 0, 0))]
        + ([o_spec] if add is not None else []),
        out_specs=o_spec, out_shape=jax.ShapeDtypeStruct((s, nb * no), out_dtype),
        compiler_params=_cparams(("parallel", "parallel")))(a, w, *([add] if add is not None else []))


def _bd_tn_diag(a, g, *, name, ts=512):
    s = a.shape[0]
    nb = a.shape[1] // LANES
    ts = min(ts, s)
    ni = s // ts

    def body(a_ref, g_ref, o_ref, acc):
        i = pl.program_id(1)
        part = lax.dot_general(a_ref[...].astype(bf16), g_ref[...].astype(bf16), (((0,), (0,)), ((), ())),
                               preferred_element_type=f32)

        @pl.when(i == 0)
        def _():
            acc[...] = part

        @pl.when(i > 0)
        def _():
            acc[...] += part

        @pl.when(i == ni - 1)
        def _():
            for j in range(8):
                o_ref[0, j] = acc[j * GROUP:(j + 1) * GROUP, (j // 2) * PAIR_LANES:(j // 2 + 1) * PAIR_LANES]

    return pl.pallas_call(
        body, name=name, grid=(nb, ni),
        in_specs=[pl.BlockSpec((ts, LANES), lambda b, i: (i, b)), pl.BlockSpec((ts, 8 * LANES), lambda b, i: (i, b))],
        out_specs=pl.BlockSpec((1, 8, GROUP, PAIR_LANES), lambda b, i: (b, 0, 0, 0)),
        out_shape=jax.ShapeDtypeStruct((nb, 8, GROUP, PAIR_LANES), f32),
        scratch_shapes=[pltpu.VMEM((LANES, 8 * LANES), f32)],
        compiler_params=_cparams(("parallel", "arbitrary")))(a, g)


def _rms_fwd(x, g, *, col=0, name, ts=512):
    s, d = x.shape[0], g.shape[1]
    ts = min(ts, s)

    def body(x_ref, g_ref, o_ref):
        xv = x_ref[...].astype(f32)
        r = lax.rsqrt(jnp.mean(xv * xv, axis=-1, keepdims=True) + EPS)
        o_ref[...] = (xv * r * g_ref[...]).astype(bf16)

    return pl.pallas_call(
        body, name=name, grid=(s // ts,),
        in_specs=[pl.BlockSpec((ts, d), lambda i: (i, col)), pl.BlockSpec((1, d), lambda i: (0, 0))],
        out_specs=pl.BlockSpec((ts, d), lambda i: (i, 0)),
        out_shape=jax.ShapeDtypeStruct((s, d), bf16),
        compiler_params=_cparams(("parallel",)))(x, g)


def _rms_bwd(dy, x, g, *, col=0, res=None, out_dtype=f32, name, ts=512):
    s, d = dy.shape
    ts = min(ts, s)
    twin = res is not None

    def body(*refs):
        if twin:
            dy_ref, x_ref, g_ref, res_ref, dx_ref, dxb_ref, dg_ref = refs
        else:
            dy_ref, x_ref, g_ref, dx_ref, dg_ref = refs

        @pl.when(pl.program_id(0) == 0)
        def _():
            dg_ref[...] = jnp.zeros_like(dg_ref)

        xv, dyv = x_ref[...].astype(f32), dy_ref[...].astype(f32)
        r = lax.rsqrt(jnp.mean(xv * xv, axis=-1, keepdims=True) + EPS)
        dyg = dyv * g_ref[...]
        dx = r * dyg - xv * (r * r * r) * jnp.mean(xv * dyg, axis=-1, keepdims=True)
        if twin:
            dx = dx + res_ref[...]
            dxb_ref[...] = dx.astype(bf16)
        dx_ref[...] = dx.astype(out_dtype)
        dg_ref[...] += jnp.sum(dyv * xv * r, axis=0, keepdims=True)

    row, vec = pl.BlockSpec((ts, d), lambda i: (i, 0)), pl.BlockSpec((1, d), lambda i: (0, 0))
    in_specs, args = [row, pl.BlockSpec((ts, d), lambda i: (i, col)), vec], [dy, x, g]
    out_specs, out_shape = [row], [jax.ShapeDtypeStruct((s, d), out_dtype)]
    if twin:
        in_specs.append(row)
        args.append(res)
        out_specs.append(row)
        out_shape.append(jax.ShapeDtypeStruct((s, d), bf16))
    return pl.pallas_call(
        body, name=name, grid=(s // ts,), in_specs=in_specs, out_specs=out_specs + [vec],
        out_shape=out_shape + [jax.ShapeDtypeStruct((1, d), f32)],
        compiler_params=_cparams(("arbitrary",)))(*args)


def _swap_halves(z):
    lane = lax.broadcasted_iota(jnp.int32, z.shape, 1)
    return jnp.where(lane < NOPE + ROPE // 2, pltpu.roll(z, LANES - ROPE // 2, axis=1), pltpu.roll(z, ROPE // 2, axis=1))


def _rope_tables(s):
    inv_freq = 1.0 / (ROPE_THETA ** (jnp.arange(0, ROPE, 2, dtype=f32) / ROPE))
    ang = jnp.arange(s, dtype=f32)[:, None] * inv_freq[None, :]
    cos, sin = jnp.cos(ang), jnp.sin(ang)
    one, zero = jnp.ones((s, NOPE), f32), jnp.zeros((s, NOPE), f32)
    pad1, pad0 = jnp.ones((s, HEAD_PAD - QK), f32), jnp.zeros((s, HEAD_PAD - QK), f32)
    return jnp.concatenate([one, cos, cos, pad1], 1), jnp.concatenate([zero, -sin, sin, pad0], 1)


def _qk_prep_fwd(q_raw, kv_raw, proj, qg, kg, cos_t, sin_t, *, name, ts=1024):
    s = q_raw.shape[0]
    ts = min(ts, s)
    rope_blk = (MIX_IN_PAD - HEAD_PAD) // HEAD_PAD

    def body(q_ref, kv_ref, kr_ref, qg_ref, kg_ref, c_ref, s_ref, qo_ref, ko_ref, vo_ref):
        lane = lax.broadcasted_iota(jnp.int32, (ts, HEAD_PAD), 1)
        cosv, sinv = c_ref[...], s_ref[...]

        def norm_rope(z, gain):
            r = lax.rsqrt(jnp.sum(z * z, axis=-1, keepdims=True) * (1.0 / QK) + EPS)
            zn = z * r * gain
            return zn * cosv + _swap_halves(zn) * sinv

        kvv = kv_ref[...]
        qo_ref[...] = (norm_rope(q_ref[...], qg_ref[...]) * _Q_FOLD).astype(bf16)
        ko_ref[...] = norm_rope(jnp.where(lane < NOPE, kvv, kr_ref[...]), kg_ref[...]).astype(bf16)
        vo_ref[...] = jnp.where(lane >= NOPE, kvv, 0.0).astype(bf16)

    head = pl.BlockSpec((ts, HEAD_PAD), lambda i, h: (i, h))
    row = pl.BlockSpec((ts, HEAD_PAD), lambda i, h: (i, 0))
    vec = pl.BlockSpec((1, HEAD_PAD), lambda i, h: (0, 0))
    out = jax.ShapeDtypeStruct((s, HEADS * HEAD_PAD), bf16)
    return pl.pallas_call(
        body, name=name, grid=(s // ts, HEADS),
        in_specs=[head, head, pl.BlockSpec((ts, HEAD_PAD), lambda i, h: (i, rope_blk)), vec, vec, row, row],
        out_specs=[head, head, head], out_shape=[out, out, out],
        compiler_params=_cparams(("parallel", "parallel")))(q_raw, kv_raw, proj, qg, kg, cos_t, sin_t)


def _qk_prep_bwd(dq, dk, dv, q_raw, kv_raw, proj, qg, kg, cos_t, sin_t, *, name, ts=1024):
    s = q_raw.shape[0]
    ts = min(ts, s)
    rope_blk = (MIX_IN_PAD - HEAD_PAD) // HEAD_PAD

    def body(dq_ref, dk_ref, dv_ref, q_ref, kv_ref, kr_ref, qg_ref, kg_ref, c_ref, s_ref,
             dqr_ref, dkvr_ref, dkr_ref, dqg_ref, dkg_ref):
        i, h = pl.program_id(0), pl.program_id(1)
        lane = lax.broadcasted_iota(jnp.int32, (ts, HEAD_PAD), 1)
        is_rope = (lane >= NOPE) & (lane < QK)
        cosv, sinv = c_ref[...], s_ref[...]

        @pl.when((i == 0) & (h == 0))
        def _():
            dqg_ref[...] = jnp.zeros_like(dqg_ref)
            dkg_ref[...] = jnp.zeros_like(dkg_ref)

        @pl.when(h == 0)
        def _():
            dkr_ref[...] = jnp.zeros_like(dkr_ref)

        def back(dout, z, gain):
            dzn = dout * cosv + jnp.where(is_rope, _swap_halves(dout * sinv), 0.0)
            r = lax.rsqrt(jnp.sum(z * z, axis=-1, keepdims=True) * (1.0 / QK) + EPS)
            dzg = dzn * gain
            dz = r * dzg - z * (r * r * r) * (jnp.sum(z * dzg, axis=-1, keepdims=True) * (1.0 / QK))
            return dz, jnp.sum(dzn * z * r, axis=0, keepdims=True)

        dqz, dqg = back(dq_ref[...].astype(f32), q_ref[...], qg_ref[...])
        dqr_ref[...] = dqz.astype(bf16)
        dqg_ref[...] += dqg
        kvv = kv_ref[...]
        dkz, dkg = back(dk_ref[...].astype(f32), jnp.where(lane < NOPE, kvv, kr_ref[...]), kg_ref[...])
        dkg_ref[...] += dkg
        dkvr_ref[...] = jnp.where(lane < NOPE, dkz, dv_ref[...].astype(f32)).astype(bf16)
        dkr_ref[...] += jnp.where(is_rope, dkz, 0.0)

    head = pl.BlockSpec((ts, HEAD_PAD), lambda i, h: (i, h))
    row = pl.BlockSpec((ts, HEAD_PAD), lambda i, h: (i, 0))
    vec = pl.BlockSpec((1, HEAD_PAD), lambda i, h: (0, 0))
    wide = jax.ShapeDtypeStruct((s, HEADS * HEAD_PAD), bf16)
    return pl.pallas_call(
        body, name=name, grid=(s // ts, HEADS),
        in_specs=[head, head, head, head, head, pl.BlockSpec((ts, HEAD_PAD), lambda i, h: (i, rope_blk)), vec, vec, row, row],
        out_specs=[head, head, row, vec, vec],
        out_shape=[wide, wide, jax.ShapeDtypeStruct((s, HEAD_PAD), f32), jax.ShapeDtypeStruct((1, HEAD_PAD), f32),
                   jax.ShapeDtypeStruct((1, HEAD_PAD), f32)],
        compiler_params=_cparams(("arbitrary", "arbitrary")))(dq, dk, dv, q_raw, kv_raw, proj, qg, kg, cos_t, sin_t)


_NT = (((1,), (1,)), ((), ()))
_SCALE = QK ** -0.5
_LOG2E = math.log2(math.e)
_Q_FOLD = _SCALE * _LOG2E
FLASH_TILE = 1024


def _flash_fwd(q, k, v, *, name, tq=FLASH_TILE):
    s = q.shape[0]
    tq = min(tq, s)

    def body(q_ref, k_ref, v_ref, o_ref, lse_ref):
        i = pl.program_id(1)
        qv = q_ref[...]

        def step(j, carry, masked):
            m, l, acc = carry
            st = pl.multiple_of(j * tq, tq)
            kj, vj = k_ref[pl.ds(st, tq), :], v_ref[pl.ds(st, tq), :]
            sc = lax.dot_general(qv, kj, _NT, preferred_element_type=f32)
            if masked:
                rr = lax.broadcasted_iota(jnp.int32, (tq, tq), 0)
                cc = lax.broadcasted_iota(jnp.int32, (tq, tq), 1)
                sc = jnp.where(cc <= rr, sc, NEG)
            m_new = jnp.maximum(m, jnp.max(sc, axis=-1, keepdims=True))
            p = jnp.exp2(sc - m_new)
            alpha = jnp.exp2(m - m_new)
            l = alpha * l + jnp.sum(p, axis=-1, keepdims=True)
            acc = alpha * acc + jnp.dot(p.astype(bf16), vj, preferred_element_type=f32)
            return m_new, l, acc

        init = (jnp.full((tq, 1), NEG, f32), jnp.zeros((tq, 1), f32), jnp.zeros((tq, HEAD_PAD), f32))
        carry = lax.fori_loop(0, i, lambda j, c: step(j, c, False), init)
        m, l, acc = step(i, carry, True)
        o_ref[...] = (acc / l).astype(bf16)
        lse_ref[0] = m + jnp.log2(l)

    blk = pl.BlockSpec((tq, HEAD_PAD), lambda h, i: (i, h))
    full = pl.BlockSpec((s, HEAD_PAD), lambda h, i: (0, h))
    return pl.pallas_call(
        body, name=name, grid=(HEADS, s // tq), in_specs=[blk, full, full],
        out_specs=[blk, pl.BlockSpec((1, tq, 1), lambda h, i: (h, i, 0))],
        out_shape=[jax.ShapeDtypeStruct((s, HEADS * HEAD_PAD), bf16), jax.ShapeDtypeStruct((HEADS, s, 1), f32)],
        compiler_params=_cparams(("parallel", "arbitrary")))(q, k, v)


def _flash_bwd_dq(q, k, v, o, do, lse, *, name, tq=FLASH_TILE):
    s = q.shape[0]
    tq = min(tq, s)

    def body(q_ref, k_ref, v_ref, o_ref, do_ref, lse_ref, dq_ref, dl_ref):
        i = pl.program_id(1)
        qv = q_ref[...]
        dov = do_ref[...].astype(f32)
        delta = jnp.sum(dov * o_ref[...].astype(f32), axis=-1, keepdims=True)
        dob = dov.astype(bf16)
        lsev = lse_ref[0]

        def step(j, acc, masked):
            st = pl.multiple_of(j * tq, tq)
            kj, vj = k_ref[pl.ds(st, tq), :], v_ref[pl.ds(st, tq), :]
            sc = lax.dot_general(qv, kj, _NT, preferred_element_type=f32)
            p = jnp.exp2(sc - lsev)
            if masked:
                rr = lax.broadcasted_iota(jnp.int32, (tq, tq), 0)
                cc = lax.broadcasted_iota(jnp.int32, (tq, tq), 1)
                p = jnp.where(cc <= rr, p, 0.0)
            dp = lax.dot_general(dob, vj, _NT, preferred_element_type=f32)
            ds = p * (dp - delta)
            return acc + jnp.dot(ds.astype(bf16), kj, preferred_element_type=f32)

        acc = lax.fori_loop(0, i, lambda j, c: step(j, c, False), jnp.zeros((tq, HEAD_PAD), f32))
        dq_ref[...] = step(i, acc, True) * _SCALE
        dl_ref[0] = delta

    blk = pl.BlockSpec((tq, HEAD_PAD), lambda h, i: (i, h))
    full = pl.BlockSpec((s, HEAD_PAD), lambda h, i: (0, h))
    col = pl.BlockSpec((1, tq, 1), lambda h, i: (h, i, 0))
    return pl.pallas_call(
        body, name=name, grid=(HEADS, s // tq), in_specs=[blk, full, full, blk, blk, col],
        out_specs=[blk, col],
        out_shape=[jax.ShapeDtypeStruct((s, HEADS * HEAD_PAD), f32), jax.ShapeDtypeStruct((HEADS, s, 1), f32)],
        compiler_params=_cparams(("parallel", "arbitrary")))(q, k, v, o, do, lse)


def _flash_bwd_dkv(q, k, v, do, lse_row, delta_row, *, name, tk=FLASH_TILE):
    s = q.shape[0]
    tk = min(tk, s)
    nblk = s // tk

    def body(q_ref, k_ref, v_ref, do_ref, lse_ref, dl_ref, dk_ref, dv_ref):
        j = pl.program_id(1)
        kv_, vv = k_ref[...], v_ref[...]

        def step(i, carry, masked):
            dk, dv = carry
            st = pl.multiple_of(i * tk, tk)
            qi = q_ref[pl.ds(st, tk), :]
            doi = do_ref[pl.ds(st, tk), :].astype(bf16)
            lse_i = lse_ref[0, :, pl.ds(st, tk)]
            dl_i = dl_ref[0, :, pl.ds(st, tk)]
            st_ = lax.dot_general(kv_, qi, _NT, preferred_element_type=f32)
            pt = jnp.exp2(st_ - lse_i)
            if masked:
                kk = lax.broadcasted_iota(jnp.int32, (tk, tk), 0)
                qq = lax.broadcasted_iota(jnp.int32, (tk, tk), 1)
                pt = jnp.where(kk <= qq, pt, 0.0)
            dv = dv + jnp.dot(pt.astype(bf16), doi, preferred_element_type=f32)
            dpt = lax.dot_general(vv, doi, _NT, preferred_element_type=f32)
            dst = pt * (dpt - dl_i)
            dk = dk + jnp.dot(dst.astype(bf16), qi, preferred_element_type=f32)
            return dk, dv

        zero = jnp.zeros((tk, HEAD_PAD), f32)
        carry = step(j, (zero, zero), True)
        dk, dv = lax.fori_loop(j + 1, nblk, lambda i, c: step(i, c, False), carry)
        dk_ref[...] = dk * (1.0 / _LOG2E)
        dv_ref[...] = dv

    blk = pl.BlockSpec((tk, HEAD_PAD), lambda h, j: (j, h))
    full = pl.BlockSpec((s, HEAD_PAD), lambda h, j: (0, h))
    rowv = pl.BlockSpec((1, 1, s), lambda h, j: (h, 0, 0))
    out = jax.ShapeDtypeStruct((s, HEADS * HEAD_PAD), f32)
    return pl.pallas_call(
        body, name=name, grid=(HEADS, nblk), in_specs=[full, blk, blk, full, rowv, rowv],
        out_specs=[blk, blk], out_shape=[out, out],
        compiler_params=_cparams(("parallel", "arbitrary")))(q, k, v, do, lse_row, delta_row)


SUBLANES = 8


def _shift_down(x, d):
    r = pltpu.roll(x, d, axis=0)
    t = lax.broadcasted_iota(jnp.int32, (SUBLANES, x.shape[1]), 0)
    head = jnp.where(t < d, 0.0, r[:SUBLANES])
    return head if x.shape[0] == SUBLANES else jnp.concatenate([head, r[SUBLANES:]], axis=0)


def _shift_up(x, d):
    s = x.shape[0]
    r = pltpu.roll(x, s - d, axis=0)
    t = lax.broadcasted_iota(jnp.int32, (SUBLANES, x.shape[1]), 0)
    tail = jnp.where(t >= SUBLANES - d, 0.0, r[s - SUBLANES:])
    return tail if s == SUBLANES else jnp.concatenate([r[:s - SUBLANES], tail], axis=0)


def _taps(w_ref):
    return w_ref[0:1, :], w_ref[1:2, :], w_ref[2:3, :]


def _conv3(u, w):
    u1, u2 = _shift_down(u, 1), _shift_down(u, 2)
    return w[0] * u2 + w[1] * u1 + w[2] * u, (u1, u2)


def _ref_shift_down(ref, d):
    s = ref.shape[0]
    return jnp.concatenate([_shift_down(ref[0:SUBLANES, :], d), ref[pl.ds(SUBLANES - d, s - SUBLANES), :]], axis=0)


def _conv3_ref(ref, w):
    return w[0] * _ref_shift_down(ref, 2) + w[1] * _ref_shift_down(ref, 1) + w[2] * ref[...]


def _conv3_t(g, w):
    return w[2] * g + w[1] * _shift_up(g, 1) + w[0] * _shift_up(g, 2)


def _conv3_dw(dw_ref, g, u, shifted):
    dw_ref[0:1, :] = jnp.sum(g * shifted[1], axis=0, keepdims=True)
    dw_ref[1:2, :] = jnp.sum(g * shifted[0], axis=0, keepdims=True)
    dw_ref[2:3, :] = jnp.sum(g * u, axis=0, keepdims=True)


_GB, _GC, _CI = 512 // LANES, 1024 // LANES, 1536 // LANES


def _sconv_fwd(proj, w, *, name):
    s = proj.shape[0]

    def body(gb_ref, gc_ref, ci_ref, w_ref, o_ref):
        o_ref[...] = (gb_ref[...] * _conv3(gc_ref[...] * ci_ref[...], _taps(w_ref))[0]).astype(bf16)

    col = lambda off: pl.BlockSpec((s, LANES), lambda j: (0, off + j))
    return pl.pallas_call(
        body, name=name, grid=(CONV_CH // LANES,),
        in_specs=[col(_GB), col(_GC), col(_CI), pl.BlockSpec((3, LANES), lambda j: (0, j))],
        out_specs=pl.BlockSpec((s, LANES), lambda j: (0, j)),
        out_shape=jax.ShapeDtypeStruct((s, CONV_CH), bf16),
        compiler_params=_cparams(("parallel",)))(proj, proj, proj, w)


def _sconv_bwd(dmix, proj, w, *, name):
    s = proj.shape[0]

    def body(do_ref, gb_ref, gc_ref, ci_ref, w_ref, dgb_ref, dgc_ref, dci_ref, dw_ref):
        wv, gc, ci, do = _taps(w_ref), gc_ref[...], ci_ref[...], do_ref[...].astype(f32)
        u = gc * ci
        conv, shifted = _conv3(u, wv)
        dgb_ref[...] = (do * conv).astype(bf16)
        dc = do * gb_ref[...]
        du = _conv3_t(dc, wv)
        dgc_ref[...] = (du * ci).astype(bf16)
        dci_ref[...] = (du * gc).astype(bf16)
        _conv3_dw(dw_ref, dc, u, shifted)

    col = lambda off: pl.BlockSpec((s, LANES), lambda j: (0, off + j))
    out = jax.ShapeDtypeStruct((s, CONV_CH), bf16)
    return pl.pallas_call(
        body, name=name, grid=(CONV_CH // LANES,),
        in_specs=[col(HEADS), col(_GB), col(_GC), col(_CI), pl.BlockSpec((3, LANES), lambda j: (0, j))],
        out_specs=[col(0), col(0), col(0), pl.BlockSpec((3, LANES), lambda j: (0, j))],
        out_shape=[out, out, out, jax.ShapeDtypeStruct((3, CONV_CH), f32)],
        compiler_params=_cparams(("parallel",)))(dmix, proj, proj, proj, w)


def _ffn_act_fwd(zg, zv, cwg, cwv, *, name):
    s, f = zg.shape

    def body(zg_ref, zv_ref, wg_ref, wv_ref, o_ref):
        o_ref[...] = (jax.nn.silu(_conv3_ref(zg_ref, _taps(wg_ref))) * _conv3_ref(zv_ref, _taps(wv_ref))).astype(bf16)

    col = pl.BlockSpec((s, LANES), lambda j: (0, j))
    wsp = pl.BlockSpec((3, LANES), lambda j: (0, j))
    return pl.pallas_call(
        body, name=name, grid=(f // LANES,), in_specs=[col, col, wsp, wsp], out_specs=col,
        out_shape=jax.ShapeDtypeStruct((s, f), bf16), compiler_params=_cparams(("parallel",)))(zg, zv, cwg, cwv)


def _ffn_act_bwd(da, zg, zv, cwg, cwv, *, name):
    s, f = zg.shape

    def body(da_ref, zg_ref, zv_ref, wg_ref, wv_ref, dzg_ref, dzv_ref, dwg_ref, dwv_ref):
        wg, wv, dav = _taps(wg_ref), _taps(wv_ref), da_ref[...].astype(f32)
        ug, uv = _conv3_ref(zg_ref, wg), _conv3_ref(zv_ref, wv)
        sg = jax.nn.sigmoid(ug)
        dug = dav * uv * (sg * (1.0 + ug * (1.0 - sg)))
        duv = dav * (ug * sg)
        dzg_ref[...] = _conv3_t(dug, wg).astype(bf16)
        dzv_ref[...] = _conv3_t(duv, wv).astype(bf16)
        _conv3_dw(dwg_ref, dug, zg_ref[...], (_ref_shift_down(zg_ref, 1), _ref_shift_down(zg_ref, 2)))
        _conv3_dw(dwv_ref, duv, zv_ref[...], (_ref_shift_down(zv_ref, 1), _ref_shift_down(zv_ref, 2)))

    col = pl.BlockSpec((s, LANES), lambda j: (0, j))
    wsp = pl.BlockSpec((3, LANES), lambda j: (0, j))
    act, wsh = jax.ShapeDtypeStruct((s, f), bf16), jax.ShapeDtypeStruct((3, f), f32)
    return pl.pallas_call(
        body, name=name, grid=(f // LANES,), in_specs=[col, col, col, wsp, wsp], out_specs=[col, col, wsp, wsp],
        out_shape=[act, act, wsh, wsh], compiler_params=_cparams(("parallel",)))(da, zg, zv, cwg, cwv)


def _expand_mat():
    return jnp.asarray(np.kron(np.eye(STATE, dtype=np.float32), np.ones((1, GROUP), np.float32)))


def _disc_fn(lr, li, ls, br, bi, e):
    dt = jnp.exp(ls)
    mag = jnp.exp(lr * dt)
    ar, ai = mag * jnp.cos(li * dt), mag * jnp.sin(li * dt)
    nr, ni = ar - 1.0, ai
    den = lr * lr + li * li
    zr, zi = (nr * lr + ni * li) / den, (ni * lr - nr * li) / den
    zrr = jnp.dot(zr, e, precision=lax.Precision.HIGHEST, preferred_element_type=f32)
    zir = jnp.dot(zi, e, precision=lax.Precision.HIGHEST, preferred_element_type=f32)
    return ar, ai, zrr * br - zir * bi, zrr * bi + zir * br


def _disc_fwd(lr, li, ls, br, bi, *, name):
    def body(lr_ref, li_ref, ls_ref, br_ref, bi_ref, e_ref, ar_ref, ai_ref, bbr_ref, bbi_ref):
        ar, ai, bbr, bbi = _disc_fn(lr_ref[...], li_ref[...], ls_ref[...], br_ref[...], bi_ref[...], e_ref[...])
        ar_ref[...], ai_ref[...], bbr_ref[...], bbi_ref[...] = ar, ai, bbr, bbi

    sq, wide = jax.ShapeDtypeStruct((GROUPS, STATE), f32), jax.ShapeDtypeStruct((GROUPS, STATE * GROUP), f32)
    return pl.pallas_call(body, name=name, out_shape=[sq, sq, wide, wide],
                          compiler_params=_cparams())(lr, li, ls, br, bi, _expand_mat())


def _disc_bwd(lr, li, ls, br, bi, dar, dai, dbbr, dbbi, *, name):
    def body(lr_ref, li_ref, ls_ref, br_ref, bi_ref, e_ref, dar_ref, dai_ref, dbbr_ref, dbbi_ref,
             dlr_ref, dli_ref, dls_ref, dbr_ref, dbi_ref):
        ev = e_ref[...]
        _, vjp = jax.vjp(lambda a, b, c, d_, e_: _disc_fn(a, b, c, d_, e_, ev),
                         lr_ref[...], li_ref[...], ls_ref[...], br_ref[...], bi_ref[...])
        dlr, dli, dls, dbr, dbi = vjp((dar_ref[...], dai_ref[...], dbbr_ref[...], dbbi_ref[...]))
        dlr_ref[...], dli_ref[...], dls_ref[...], dbr_ref[...], dbi_ref[...] = dlr, dli, dls, dbr, dbi

    sq, wide = jax.ShapeDtypeStruct((GROUPS, STATE), f32), jax.ShapeDtypeStruct((GROUPS, STATE * GROUP), f32)
    return pl.pallas_call(body, name=name, out_shape=[sq, sq, jax.ShapeDtypeStruct((GROUPS, 1), f32), wide, wide],
                          compiler_params=_cparams())(lr, li, ls, br, bi, _expand_mat(), dar, dai, dbbr, dbbi)


SCAN_TILE = 32
SCAN_PAIRS = 4


def _tile_shift(v, d, reverse):
    if d % 8:
        return _shift_up(v, d) if reverse else _shift_down(v, d)
    z = jnp.zeros((d, v.shape[1]), v.dtype)
    return jnp.concatenate([v[d:], z], axis=0) if reverse else jnp.concatenate([z, v[:v.shape[0] - d]], axis=0)


def _tile_scan(r, i, pows, reverse):
    d = 1
    for br, bi in pows:
        rs, is_ = _tile_shift(r, d, reverse), _tile_shift(i, d, reverse)
        r, i = r + br * rs - bi * is_, i + br * is_ + bi * rs
        d *= 2
    return r, i


def _scan_setup(ar, ai, reverse):
    if reverse:
        ai = -ai
    pows, br, bi, d = [], ar, ai, 1
    while d < SCAN_TILE:
        pows.append((br, bi))
        br, bi, d = br * br - bi * bi, 2.0 * br * bi, 2 * d
    row = lax.broadcasted_iota(jnp.int32, (SCAN_TILE, LANES), 0)
    hit = row == (SCAN_TILE - 1 if reverse else 0)
    pr, pi = _tile_scan(jnp.where(hit, ar, 0.0), jnp.where(hit, ai, 0.0), pows, reverse)
    return pows, pr, pi


def _carry_in(r, i, pr, pi, cr, ci):
    crb, cib = jnp.broadcast_to(cr, r.shape), jnp.broadcast_to(ci, i.shape)
    return r + pr * crb - pi * cib, i + pr * cib + pi * crb


def _pair_cols(q):
    return slice(q * PAIR_LANES, q * PAIR_LANES + LANES), slice(q * PAIR_LANES + LANES, (q + 1) * PAIR_LANES)


_SCAN_W = SCAN_PAIRS * PAIR_LANES


def _scan_specs(s, w):
    per = w.shape[2] // _SCAN_W
    src = pl.BlockSpec((s, LANES), lambda g: (0, g // per))
    mat = pl.BlockSpec((1, LANES, _SCAN_W), lambda g: (g // per, 0, g % per))
    col = pl.BlockSpec((s, _SCAN_W), lambda g: (0, g))
    vec = pl.BlockSpec((SCAN_PAIRS, 1, LANES), lambda g: (g, 0, 0))
    return src, mat, col, vec, (w.shape[0] * per,)


def _scan_fwd(u, wb, ar, ai, *, name):
    s = u.shape[0]
    nt = s // SCAN_TILE

    def body(u_ref, w_ref, ar_ref, ai_ref, x_ref):
        setups = [_scan_setup(ar_ref[q], ai_ref[q], False) for q in range(SCAN_PAIRS)]
        wv = w_ref[0]

        def tile_rows(k):
            return pl.ds(pl.multiple_of(k * SCAN_TILE, SCAN_TILE), SCAN_TILE)

        def tile_in(k):
            return jnp.dot(u_ref[tile_rows(k), :].astype(bf16), wv, preferred_element_type=f32)

        def step(k, carry):
            rows, bu = tile_rows(k), carry[-1]
            ahead = tile_in(jnp.minimum(k + 1, nt - 1))
            out = []
            for q, (pows, pr, pi) in enumerate(setups):
                rc, ic = _pair_cols(q)
                r, i = _tile_scan(bu[:, rc], bu[:, ic], pows, False)
                r, i = _carry_in(r, i, pr, pi, carry[2 * q], carry[2 * q + 1])
                x_ref[rows, rc] = r.astype(bf16)
                x_ref[rows, ic] = i.astype(bf16)
                out += [r[SCAN_TILE - 1:SCAN_TILE, :], i[SCAN_TILE - 1:SCAN_TILE, :]]
            return tuple(out) + (ahead,)

        lax.fori_loop(0, nt, step, tuple(jnp.zeros((1, LANES), f32) for _ in range(2 * SCAN_PAIRS)) + (tile_in(0),))

    src, mat, col, vec, grid = _scan_specs(s, wb)
    return pl.pallas_call(body, name=name, grid=grid, in_specs=[src, mat, vec, vec], out_specs=col,
                          out_shape=jax.ShapeDtypeStruct((s, wb.shape[0] * wb.shape[2]), bf16),
                          compiler_params=_cparams(("parallel",)))(u, wb, ar, ai)


def _scan_bwd(dy, cbt, x, ar, ai, *, name):
    s = dy.shape[0]
    nt = s // SCAN_TILE

    def fold(v):
        out = v[0:8]
        for r in range(8, SCAN_TILE, 8):
            out = out + v[r:r + 8]
        return out

    def body(dy_ref, w_ref, x_ref, ar_ref, ai_ref, g_ref, dar_ref, dai_ref):
        setups = [_scan_setup(ar_ref[q], ai_ref[q], True) for q in range(SCAN_PAIRS)]
        row = lax.broadcasted_iota(jnp.int32, (SCAN_TILE, LANES), 0)
        wv = w_ref[0]

        def tile_in(k):
            return jnp.dot(dy_ref[pl.ds(pl.multiple_of(k * SCAN_TILE, SCAN_TILE), SCAN_TILE), :], wv, preferred_element_type=f32)

        def step(kk, carry):
            k = nt - 1 - kk
            start = pl.multiple_of(k * SCAN_TILE, SCAN_TILE)
            rows = pl.ds(start, SCAN_TILE)
            prev16 = pl.ds(pl.multiple_of(jnp.maximum(start - 16, 0), 16), 16)
            dx = carry[-1]
            ahead = tile_in(jnp.maximum(k - 1, 0))

            def before(cols):
                first = jnp.where(k > 0, x_ref[prev16, cols][15:16, :].astype(f32), 0.0)
                return jnp.where(row == 0, first, pltpu.roll(x_ref[rows, cols].astype(f32), 1, axis=0))

            out = []
            for q, (pows, pr, pi) in enumerate(setups):
                rc, ic = _pair_cols(q)
                cr, ci, acc_r, acc_i = carry[4 * q:4 * q + 4]
                gr, gi = _tile_scan(dx[:, rc], dx[:, ic], pows, True)
                gr, gi = _carry_in(gr, gi, pr, pi, cr, ci)
                g_ref[rows, rc] = gr.astype(bf16)
                g_ref[rows, ic] = gi.astype(bf16)
                xr, xi = before(rc), before(ic)
                out += [gr[0:1, :], gi[0:1, :], acc_r + fold(gr * xr + gi * xi), acc_i + fold(gi * xr - gr * xi)]
            return tuple(out) + (ahead,)

        init = (jnp.zeros((1, LANES), f32), jnp.zeros((1, LANES), f32), jnp.zeros((8, LANES), f32), jnp.zeros((8, LANES), f32))
        res = lax.fori_loop(0, nt, step, init * SCAN_PAIRS + (tile_in(nt - 1),))
        for q in range(SCAN_PAIRS):
            dar_ref[q] = jnp.sum(res[4 * q + 2], axis=0, keepdims=True)
            dai_ref[q] = jnp.sum(res[4 * q + 3], axis=0, keepdims=True)

    src, mat, col, vec, grid = _scan_specs(s, cbt)
    vsh = jax.ShapeDtypeStruct((GROUPS // 2, 1, LANES), f32)
    return pl.pallas_call(body, name=name, grid=grid, in_specs=[src, mat, col, vec, vec],
                          out_specs=[col, vec, vec], out_shape=[jax.ShapeDtypeStruct(x.shape, bf16), vsh, vsh],
                          compiler_params=_cparams(("parallel",)))(dy, cbt, x, ar, ai)


_GELU_C = math.sqrt(2.0 / math.pi)


def _gelu_fwd(y, u, dsk, *, name, ts=512):
    s, d = y.shape
    ts = min(ts, s)

    def body(y_ref, u_ref, d_ref, o_ref):
        o_ref[...] = jax.nn.gelu(y_ref[...] + d_ref[...] * u_ref[...]).astype(bf16)

    row, vec = pl.BlockSpec((ts, d), lambda i: (i, 0)), pl.BlockSpec((1, d), lambda i: (0, 0))
    return pl.pallas_call(body, name=name, grid=(s // ts,), in_specs=[row, row, vec], out_specs=row,
                          out_shape=jax.ShapeDtypeStruct((s, d), bf16), compiler_params=_cparams(("parallel",)))(y, u, dsk)


def _gelu_bwd(dg, y, u, dsk, *, name, ts=512):
    s, d = y.shape
    ts = min(ts, s)

    def body(dg_ref, y_ref, u_ref, d_ref, dy_ref, du_ref, dd_ref):
        @pl.when(pl.program_id(0) == 0)
        def _():
            dd_ref[...] = jnp.zeros_like(dd_ref)

        uv, dv = u_ref[...], d_ref[...]
        z = y_ref[...] + dv * uv
        th = jnp.tanh(_GELU_C * (z + 0.044715 * z * z * z))
        dz = dg_ref[...] * (0.5 * (1.0 + th) + 0.5 * z * (1.0 - th * th) * _GELU_C * (1.0 + 3 * 0.044715 * z * z))
        dy_ref[...] = dz.astype(bf16)
        du_ref[...] = dz * dv
        dd_ref[...] += jnp.sum(dz * uv, axis=0, keepdims=True)

    row, vec = pl.BlockSpec((ts, d), lambda i: (i, 0)), pl.BlockSpec((1, d), lambda i: (0, 0))
    return pl.pallas_call(
        body, name=name, grid=(s // ts,), in_specs=[row, row, row, vec], out_specs=[row, row, vec],
        out_shape=[jax.ShapeDtypeStruct((s, d), bf16), jax.ShapeDtypeStruct((s, d), f32), jax.ShapeDtypeStruct((1, d), f32)],
        compiler_params=_cparams(("arbitrary",)))(dg, y, u, dsk)


def _glu_fwd(x, a, b, *, name, ts=512):
    s, d = x.shape
    ts = min(ts, s)

    def body(x_ref, a_ref, b_ref, o_ref):
        o_ref[...] = x_ref[...] + a_ref[...] * jax.nn.sigmoid(b_ref[...])

    row = pl.BlockSpec((ts, d), lambda i: (i, 0))
    return pl.pallas_call(body, name=name, grid=(s // ts,), in_specs=[row, row, row], out_specs=row,
                          out_shape=jax.ShapeDtypeStruct((s, d), f32), compiler_params=_cparams(("parallel",)))(x, a, b)


def _glu_bwd(dx, a, b, *, name, ts=512):
    s, d = dx.shape
    ts = min(ts, s)

    def body(dx_ref, a_ref, b_ref, da_ref, db_ref):
        sg = jax.nn.sigmoid(b_ref[...])
        dxv = dx_ref[...]
        da_ref[...] = (dxv * sg).astype(bf16)
        db_ref[...] = (dxv * a_ref[...] * sg * (1.0 - sg)).astype(bf16)

    row = pl.BlockSpec((ts, d), lambda i: (i, 0))
    out = jax.ShapeDtypeStruct((s, d), bf16)
    return pl.pallas_call(body, name=name, grid=(s // ts,), in_specs=[row, row, row], out_specs=[row, row],
                          out_shape=[out, out], compiler_params=_cparams(("parallel",)))(dx, a, b)


def _loss_head(y, target, *, name, ts=512):
    s, d = y.shape
    ts = min(ts, s)

    def body(y_ref, t_ref, dy_ref, dyb_ref, l_ref):
        @pl.when(pl.program_id(0) == 0)
        def _():
            l_ref[...] = jnp.zeros_like(l_ref)

        e = y_ref[...] - t_ref[...]
        dy = e * (1.0 / d)
        dy_ref[...] = dy
        dyb_ref[...] = dy.astype(bf16)
        l_ref[...] += 0.5 * jnp.sum(jnp.mean(e * e, axis=-1, keepdims=True))

    row = pl.BlockSpec((ts, d), lambda i: (i, 0))
    return pl.pallas_call(
        body, name=name, grid=(s // ts,), in_specs=[row, row],
        out_specs=[row, row, pl.BlockSpec((8, LANES), lambda i: (0, 0))],
        out_shape=[jax.ShapeDtypeStruct((s, d), f32), jax.ShapeDtypeStruct((s, d), bf16), jax.ShapeDtypeStruct((8, LANES), f32)],
        compiler_params=_cparams(("arbitrary",)))(y, target)


def _adamw(w, g, m, v, *, name, tr=128):
    r, c = w.shape

    def body(w_ref, g_ref, m_ref, v_ref, d_ref, mo_ref, vo_ref):
        gv = g_ref[...]
        mn = ADAM_B1 * m_ref[...] + (1.0 - ADAM_B1) * gv
        vn = ADAM_B2 * v_ref[...] + (1.0 - ADAM_B2) * (gv * gv)
        m_hat = mn / (1.0 - ADAM_B1 ** ADAM_STEP)
        v_hat = vn / (1.0 - ADAM_B2 ** ADAM_STEP)
        d_ref[...] = -ADAM_LR * (m_hat / (jnp.sqrt(v_hat) + ADAM_EPS) + ADAM_WD * w_ref[...])
        mo_ref[...] = mn
        vo_ref[...] = vn

    row = pl.BlockSpec((tr, c), lambda i: (i, 0))
    out = jax.ShapeDtypeStruct((r, c), f32)
    return pl.pallas_call(body, name=name, grid=(r // tr,), in_specs=[row] * 4, out_specs=[row] * 3,
                          out_shape=[out, out, out], compiler_params=_cparams(("parallel",)))(w, g, m, v)


def _sum_slabs(land, *, name, tr=128):
    n, r, c = land.shape

    def body(l_ref, o_ref):
        acc = l_ref[0].astype(f32)
        for i in range(1, n):
            acc = acc + l_ref[i].astype(f32)
        o_ref[...] = acc

    return pl.pallas_call(body, name=name, grid=(r // tr,), in_specs=[pl.BlockSpec((n, tr, c), lambda i: (0, i, 0))],
                          out_specs=pl.BlockSpec((tr, c), lambda i: (i, 0)), out_shape=jax.ShapeDtypeStruct((r, c), f32),
                          compiler_params=_cparams(("parallel",)))(land)


def _pair_sum(g, theirs, *, name, tr=256):
    n, r, c = theirs.shape

    def body(c_ref, g_ref, t_ref, o_ref):
        o_ref[...] = (g_ref[...].astype(f32) + t_ref[...].astype(f32)).astype(bf16)

    blk = pl.BlockSpec((1, tr, c), lambda j, i, c_ref: (j, i, 0))
    mine = pl.BlockSpec((1, tr, c), lambda j, i, c_ref: (2 * j + c_ref[0], i, 0))
    return pl.pallas_call(
        body, name=name,
        grid_spec=pltpu.PrefetchScalarGridSpec(num_scalar_prefetch=1, grid=(n, r // tr), in_specs=[mine, blk], out_specs=blk),
        out_shape=jax.ShapeDtypeStruct(theirs.shape, bf16),
        compiler_params=_cparams(("parallel", "parallel")))(lax.axis_index("c").astype(jnp.int32).reshape(1), g, theirs)


_MESH = pl.DeviceIdType.MESH
_HBM = pl.BlockSpec(memory_space=pltpu.HBM)
N_CHIP = N_DEV // 2


def _position():
    return lax.axis_index("x"), lax.axis_index("y"), lax.axis_index("c")


def _gather8(x, *, name):
    half = x.shape[0] // 2

    def body(x_ref, o_ref, send_sems, recv_sems, local_sem):
        xx, yy, cc = _position()
        me, sibling = (xx, yy, cc), (xx, yy, 1 - cc)
        here, xn, yn, dg = (xx, yy), (1 - xx, yy), (xx, 1 - yy), (1 - xx, 1 - yy)
        first, second = pl.ds(0, half), pl.ds(half, half)

        def slab(chip, pc, rows=None):
            ref = o_ref.at[4 * chip[0] + 2 * chip[1] + pc]
            return ref if rows is None else ref.at[rows]

        def copy(k, ref, to, src=None):
            return pltpu.make_async_remote_copy(src_ref=ref if src is None else src, dst_ref=ref, send_sem=send_sems.at[k],
                                                recv_sem=recv_sems.at[k], device_id=to, device_id_type=_MESH)

        mine = pltpu.make_async_copy(x_ref, slab(here, cc), local_sem)
        mine.start()
        sends = [copy(0, slab(here, cc), sibling, src=x_ref), copy(1, slab(here, cc), (*xn, cc), src=x_ref),
                 copy(2, slab(here, cc), (*yn, cc), src=x_ref)]
        for cp in sends:
            cp.start()
        copy(1, slab(xn, cc), me).wait_recv()
        sends += [copy(3, slab(xn, cc, first), (*yn, cc)), copy(5, slab(xn, cc), sibling)]
        copy(2, slab(yn, cc), me).wait_recv()
        sends += [copy(4, slab(yn, cc, second), (*xn, cc)), copy(6, slab(yn, cc), sibling)]
        for cp in sends[3:]:
            cp.start()
        copy(3, slab(dg, cc, first), me).wait_recv()
        copy(4, slab(dg, cc, second), me).wait_recv()
        sends.append(copy(7, slab(dg, cc), sibling))
        sends[-1].start()
        for k, chip in ((0, here), (5, xn), (6, yn), (7, dg)):
            copy(k, slab(chip, 1 - cc), me).wait_recv()
        for cp in sends:
            cp.wait_send()
        mine.wait()

    return pl.pallas_call(
        body, name=name, in_specs=[_HBM], out_specs=_HBM, out_shape=jax.ShapeDtypeStruct((N_DEV,) + x.shape, x.dtype),
        scratch_shapes=[pltpu.SemaphoreType.DMA((N_DEV,)), pltpu.SemaphoreType.DMA((N_DEV,)), pltpu.SemaphoreType.DMA],
    )(x)


def _pair_exchange(g, *, name):
    def body(g_ref, land_ref, send_sems, recv_sems):
        xx, yy, cc = _position()
        copies = []
        for j in range(N_CHIP):
            cp = pltpu.make_async_remote_copy(src_ref=g_ref.at[2 * j + 1 - cc], dst_ref=land_ref.at[j], send_sem=send_sems.at[j],
                                              recv_sem=recv_sems.at[j], device_id=(xx, yy, 1 - cc), device_id_type=_MESH)
            cp.start()
            copies.append(cp)
        for cp in copies:
            cp.wait_recv()
        for cp in copies:
            cp.wait_send()

    sems = pltpu.SemaphoreType.DMA((N_CHIP,))
    return pl.pallas_call(body, name=name, in_specs=[_HBM], out_specs=_HBM,
                          out_shape=jax.ShapeDtypeStruct((N_CHIP,) + g.shape[1:], g.dtype), scratch_shapes=[sems, sems])(g)


def _cross_exchange(p, *, name):
    half = p.shape[1] // 2

    def body(p_ref, o_ref, relay_ref, send_sems, recv_sems, local_sem):
        xx, yy, cc = _position()
        me = (xx, yy, cc)
        xn, yn, dg = (1 - xx, yy), (xx, 1 - yy), (1 - xx, 1 - yy)
        idx = lambda chip: 2 * chip[0] + chip[1]
        mine = idx((xx, yy))
        first, second = pl.ds(0, half), pl.ds(half, half)

        def copy(k, src, dst, to):
            return pltpu.make_async_remote_copy(src_ref=src, dst_ref=dst, send_sem=send_sems.at[k], recv_sem=recv_sems.at[k],
                                                device_id=to, device_id_type=_MESH)

        local = pltpu.make_async_copy(p_ref.at[mine], o_ref.at[mine], local_sem)
        local.start()
        sends = [copy(0, p_ref.at[idx(xn)], o_ref.at[mine], (*xn, cc)),
                 copy(1, p_ref.at[idx(dg)].at[first], relay_ref.at[0], (*xn, cc)),
                 copy(2, p_ref.at[idx(yn)], o_ref.at[mine], (*yn, cc)),
                 copy(3, p_ref.at[idx(dg)].at[second], relay_ref.at[1], (*yn, cc))]
        for cp in sends:
            cp.start()
        copy(1, relay_ref.at[0], relay_ref.at[0], me).wait_recv()
        sends.append(copy(4, relay_ref.at[0], o_ref.at[idx(xn)].at[first], (*yn, cc)))
        sends[-1].start()
        copy(3, relay_ref.at[1], relay_ref.at[1], me).wait_recv()
        sends.append(copy(5, relay_ref.at[1], o_ref.at[idx(yn)].at[second], (*xn, cc)))
        sends[-1].start()
        for k, dst in ((0, o_ref.at[idx(xn)]), (2, o_ref.at[idx(yn)]), (4, o_ref.at[idx(dg)].at[first]),
                       (5, o_ref.at[idx(dg)].at[second])):
            copy(k, dst, dst, me).wait_recv()
        for cp in sends:
            cp.wait_send()
        local.wait()

    sems = pltpu.SemaphoreType.DMA((6,))
    relay = jax.ShapeDtypeStruct((2, half) + p.shape[2:], p.dtype)
    return pl.pallas_call(body, name=name, in_specs=[_HBM], out_specs=[_HBM, _HBM],
                          out_shape=[jax.ShapeDtypeStruct(p.shape, p.dtype), relay],
                          scratch_shapes=[sems, sems, pltpu.SemaphoreType.DMA])(p)[0]


def _all_sum(x, *, name):
    return _sum_slabs(_gather8(x, name=f"gather_{name}"), name=f"sum_{name}", tr=min(128, x.shape[0]))


def _pack_slabs(parts, rows, axis=0):
    lead = parts[0].shape[:axis]
    slabs = [p.reshape(lead + (-1, D)) for p in parts]
    used = sum(sl.shape[axis] for sl in slabs)
    return jnp.concatenate(slabs + [jnp.zeros(lead + (rows - used, D), slabs[0].dtype)], axis=axis)


def _unpack_slabs(slab, shapes):
    lead, out, off = slab.shape[:-2], [], 0
    for shp in shapes:
        n = int(np.prod(shp)) // D
        out.append(slab[..., off:off + n, :].reshape(lead + tuple(shp)))
        off += n
    return out


def _pack_rows(parts, rows):
    flat = jnp.concatenate([p.reshape(-1) for p in parts])
    return jnp.pad(flat, (0, rows * D - flat.shape[0])).reshape(rows, D)


def _unpack_rows(slab, shapes):
    flat, out, off = slab.reshape(-1), [], 0
    for shp in shapes:
        n = int(np.prod(shp))
        out.append(flat[off:off + n].reshape(shp))
        off += n
    return out


def _full_shape(shard, axis):
    return tuple(d * N_DEV if i == axis else d for i, d in enumerate(shard))


def _row(v):
    return v.reshape(1, -1).astype(f32)


def _pad_gain(g):
    return jnp.pad(g.astype(f32), (0, HEAD_PAD - QK)).reshape(1, HEAD_PAD)


def _ffn_fwd(x, p, tag):
    h = _rms_fwd(x, p["norm"], name=f"ffn_norm_{tag}")
    zg = _mm(h, p["wgT"], tb=True, tn=FFN_H, name=f"ffn_up_g_{tag}")
    zv = _mm(h, p["wvT"], tb=True, tn=FFN_H, name=f"ffn_up_v_{tag}")
    a = _ffn_act_fwd(zg, zv, p["cwg"], p["cwv"], name=f"ffn_act_{tag}")
    y = _mm(a, p["wd"], add=x, name=f"ffn_down_{tag}")
    return y, (x, h, zg, zv, a)


def _ffn_bwd(dy, dyb, p, saved, tag):
    x, h, zg, zv, a = saved
    g = {}
    da = _mm(dyb, p["wd"], tb=True, out_dtype=bf16, name=f"ffn_down_dx_{tag}")
    g["wd"] = _mm(a, dyb, ta=True, out_dtype=bf16, name=f"ffn_down_dw_{tag}")
    dzg, dzv, g["cwg"], g["cwv"] = _ffn_act_bwd(da, zg, zv, p["cwg"], p["cwv"], name=f"ffn_act_bwd_{tag}")
    g["wgT"] = _mm(dzg, h, ta=True, out_dtype=bf16, name=f"ffn_up_g_dw_{tag}")
    g["wvT"] = _mm(dzv, h, ta=True, out_dtype=bf16, name=f"ffn_up_v_dw_{tag}")
    dh = _mm(dzg, p["wgT"], name=f"ffn_up_g_dx_{tag}")
    dh = _mm(dzv, p["wvT"], add=dh, name=f"ffn_up_v_dx_{tag}")
    dx, dxb, g["norm"] = _rms_bwd(dh, x, p["norm"], res=dy, name=f"ffn_norm_bwd_{tag}")
    return dx, dxb, g


def _mla_fwd(x, p, tabs, tag):
    cos_t, sin_t = tabs
    h = _rms_fwd(x, p["norm"], name=f"attn_norm_{tag}")
    proj = _mm(h, p["w_inT"], tb=True, name=f"mix_in_{tag}")
    cqn = _rms_fwd(proj, p["cq_norm"], col=0, name=f"cq_norm_{tag}")
    ckvn = _rms_fwd(proj, p["ckv_norm"], col=1, name=f"ckv_norm_{tag}")
    q_raw = _mm(cqn, p["w_uqT"], tb=True, name=f"uq_{tag}")
    kv_raw = _mm(ckvn, p["w_ukvT"], tb=True, name=f"ukv_{tag}")
    q, k, v = _qk_prep_fwd(q_raw, kv_raw, proj, p["q_gain"], p["k_gain"], cos_t, sin_t, name=f"qk_prep_{tag}")
    o, lse = _flash_fwd(q, k, v, name=f"flash_fwd_{tag}")
    conv = _sconv_fwd(proj, p["sconv_w"], name=f"sconv_{tag}")
    y = _mm(conv, p["w_out"][HEADS * HEAD_PAD:], add=x, name=f"mix_out_conv_{tag}")
    y = _mm(o, p["w_out"][:HEADS * HEAD_PAD], add=y, name=f"mix_out_{tag}")
    return y, (x, h, proj, cqn, ckvn, q_raw, kv_raw, q, k, v, o, lse, conv)


def _mla_bwd(dy, dyb, p, tabs, saved, tag):
    cos_t, sin_t = tabs
    x, h, proj, cqn, ckvn, q_raw, kv_raw, q, k, v, o, lse, conv = saved
    s = x.shape[0]
    g = {}
    dmix = _mm(dyb, p["w_out"], tb=True, name=f"mix_out_dx_{tag}")
    g["w_out"] = jnp.concatenate([_mm(o, dyb, ta=True, out_dtype=bf16, name=f"mix_out_dw_{tag}"),
                                  _mm(conv, dyb, ta=True, out_dtype=bf16, name=f"mix_out_conv_dw_{tag}")], axis=0)
    dgb, dgc, dci, g["sconv_w"] = _sconv_bwd(dmix, proj, p["sconv_w"], name=f"sconv_bwd_{tag}")
    dq, delta = _flash_bwd_dq(q, k, v, o, dmix, lse, name=f"flash_dq_{tag}")
    dk, dv = _flash_bwd_dkv(q, k, v, dmix, lse.reshape(HEADS, 1, s), delta.reshape(HEADS, 1, s), name=f"flash_dkv_{tag}")
    dq_raw, dkv_raw, dkr, g["q_gain"], g["k_gain"] = _qk_prep_bwd(
        dq, dk, dv, q_raw, kv_raw, proj, p["q_gain"], p["k_gain"], cos_t, sin_t, name=f"qk_prep_bwd_{tag}")
    dcqn = _mm(dq_raw, p["w_uqT"], name=f"uq_dx_{tag}")
    g["w_uqT"] = _mm(dq_raw, cqn, ta=True, out_dtype=bf16, name=f"uq_dw_{tag}")
    dckvn = _mm(dkv_raw, p["w_ukvT"], name=f"ukv_dx_{tag}")
    g["w_ukvT"] = _mm(dkv_raw, ckvn, ta=True, out_dtype=bf16, name=f"ukv_dw_{tag}")
    dcq, g["cq_norm"] = _rms_bwd(dcqn, proj, p["cq_norm"], col=0, out_dtype=bf16, name=f"cq_norm_bwd_{tag}")
    dckv, g["ckv_norm"] = _rms_bwd(dckvn, proj, p["ckv_norm"], col=1, out_dtype=bf16, name=f"ckv_norm_bwd_{tag}")
    dproj = jnp.concatenate([dcq, dckv, dgb, dgc, dci, dkr.astype(bf16)], axis=1)
    dh = _mm(dproj, p["w_inT"], name=f"mix_in_dx_{tag}")
    g["w_inT"] = _mm(dproj, h, ta=True, out_dtype=bf16, name=f"mix_in_dw_{tag}")
    dx, dxb, g["norm"] = _rms_bwd(dh, x, p["norm"], res=dy, name=f"attn_norm_bwd_{tag}")
    return dx, dxb, g


def _block_diag(wg):
    nb, ng, r, c = wg.shape
    eye = jnp.eye(ng, dtype=wg.dtype)
    return (wg[:, :, :, None, :] * eye[None, :, None, :, None]).reshape(nb, ng * r, ng * c)


def _s5_mats(bbr, bbi, c_re, c_im):
    nb = GROUPS // 8
    b4 = jnp.stack([bbr.reshape(GROUPS, STATE, GROUP), bbi.reshape(GROUPS, STATE, GROUP)], axis=1)
    wg = jnp.transpose(b4, (0, 3, 1, 2)).reshape(nb, 8, GROUP, 2 * STATE)
    cg = jnp.stack([c_re, -c_im], axis=1)
    cg = jnp.transpose(cg, (0, 1, 3, 2)).reshape(nb, 8, 2 * STATE, GROUP)
    return _state_layout(_block_diag(wg), 2), _state_layout(_block_diag(cg), 1)


def _state_layout(m, axis):
    shp = m.shape
    m = m.reshape(shp[:axis] + (4, 2, 2, STATE) + shp[axis + 1:])
    return jnp.swapaxes(m, axis + 1, axis + 2).reshape(shp)


def _group_blocks(d):
    d = d.reshape(GROUPS // 2, 2, GROUP, 2, 2, STATE)
    return jnp.stack([d[:, 0, :, :, 0, :], d[:, 1, :, :, 1, :]], axis=1).reshape(GROUPS, GROUP, 2, STATE)


def _s5_fwd(x, p, tag):
    h = _rms_fwd(x, p["norm"], name=f"ssm_norm_{tag}")
    u, ub = _mm(h, p["w_in"], twin=True, name=f"ssm_in_{tag}")
    ar, ai, bbr, bbi = _disc_fwd(p["lr"], p["li"], p["ls"], p["br"], p["bi"], name=f"disc_{tag}")
    wb, cb = _s5_mats(bbr, bbi, p["c_re"], p["c_im"])
    a1, a2 = ar.reshape(GROUPS // 2, 1, LANES), ai.reshape(GROUPS // 2, 1, LANES)
    xs = _scan_fwd(ub, wb.astype(bf16), a1, a2, name=f"ssm_scan_{tag}")
    y = _bd_nn(xs, cb.astype(bf16), name=f"ssm_y_{tag}")
    g = _gelu_fwd(y, u, p["d_skip"], name=f"ssm_gelu_{tag}")
    a = _mm(g, p["wgaT"], tb=True, name=f"glu_a_{tag}")
    b = _mm(g, p["wgbT"], tb=True, name=f"glu_b_{tag}")
    out = _glu_fwd(x, a, b, name=f"glu_{tag}")
    return out, (x, h, u, ub, wb, cb, a1, a2, xs, y, g, a, b)


def _s5_bwd(dout, p, saved, tag):
    x, h, u, ub, wb, cb, a1, a2, xs, y, g, a, b = saved
    gr = {}
    da, db = _glu_bwd(dout, a, b, name=f"glu_bwd_{tag}")
    dg = _mm(da, p["wgaT"], name=f"glu_a_dx_{tag}")
    dg = _mm(db, p["wgbT"], add=dg, name=f"glu_b_dx_{tag}")
    gr["wgaT"] = _mm(da, g, ta=True, out_dtype=bf16, name=f"glu_a_dw_{tag}")
    gr["wgbT"] = _mm(db, g, ta=True, out_dtype=bf16, name=f"glu_b_dw_{tag}")
    dy, du1, gr["d_skip"] = _gelu_bwd(dg, y, u, p["d_skip"], name=f"ssm_gelu_bwd_{tag}")
    dct = _group_blocks(_bd_tn_diag(dy, xs, name=f"ssm_y_dw_{tag}"))
    gs, dar, dai = _scan_bwd(dy, jnp.swapaxes(cb, 1, 2).astype(bf16), xs, a1, a2, name=f"ssm_scan_bwd_{tag}")
    du = _bd_nn(gs, jnp.swapaxes(wb, 1, 2).astype(bf16), add=du1, out_dtype=bf16, name=f"ssm_bu_dx_{tag}")
    dwg = _group_blocks(_bd_tn_diag(ub, gs, name=f"ssm_bu_dw_{tag}"))
    dh = _mm(du, p["w_in"], tb=True, name=f"ssm_in_dx_{tag}")
    gr["w_in"] = _mm(h, du, ta=True, out_dtype=bf16, name=f"ssm_in_dw_{tag}")
    dx, dxb, gr["norm"] = _rms_bwd(dh, x, p["norm"], res=dout, name=f"ssm_norm_bwd_{tag}")
    dbb = jnp.transpose(dwg, (2, 0, 3, 1)).reshape(2, GROUPS, STATE * GROUP)
    gr["c_re"] = dct[:, :, 0, :]
    gr["c_im"] = -dct[:, :, 1, :]
    dlr, dli, dls, dbr, dbi = _disc_bwd(p["lr"], p["li"], p["ls"], p["br"], p["bi"], dar.reshape(GROUPS, STATE),
                                        dai.reshape(GROUPS, STATE), dbb[0], dbb[1], name=f"disc_bwd_{tag}")
    gr["lr"], gr["li"], gr["ls"] = dlr, dli, dls.reshape(GROUPS)
    gr["br"], gr["bi"] = dbr.reshape(GROUPS, STATE, GROUP), dbi.reshape(GROUPS, STATE, GROUP)
    return dx, dxb, gr


def _slab_shape(shard, axis):
    return (shard[0], shard[2], shard[1]) if axis == 2 else shard


def _to_slab(w, axis):
    return jnp.swapaxes(w, 1, 2) if axis == 2 else w


def _mix_in_pad(wt):
    z = lambda n: jnp.zeros((n, wt.shape[1]), wt.dtype)
    return jnp.concatenate([wt[:512], wt[544:2080], z(NOPE), wt[512:544], z(HEAD_PAD - QK)], axis=0)


def _mix_in_unpad(g):
    return jnp.concatenate([g[:512], g[2048 + NOPE:2048 + QK], g[512:2048]], axis=0)


def _mix_out_pad(w):
    att = jnp.pad(w[:512].reshape(HEADS, NOPE, D), ((0, 0), (NOPE, 0), (0, 0))).reshape(HEADS * HEAD_PAD, D)
    return jnp.concatenate([att, w[512:]], axis=0)


def _mix_out_unpad(g):
    att = g[:HEADS * HEAD_PAD].reshape(HEADS, HEAD_PAD, D)[:, NOPE:, :].reshape(HEADS * NOPE, D)
    return jnp.concatenate([att, g[HEADS * HEAD_PAD:]], axis=0)


def _layer_params(wl, ws, layer):
    i = layer // 2
    half = N_DEV // 2
    up = wl["ffn_w_up"][layer]
    ffn = dict(norm=_row(ws["ffn_norm"][layer]), wgT=up[:half].reshape(FFN_H, D), wvT=up[half:].reshape(FFN_H, D),
               cwg=ws["ffn_conv_w"][layer][:, :FFN_H], cwv=ws["ffn_conv_w"][layer][:, FFN_H:],
               wd=wl["ffn_w_down"][layer].reshape(FFN_H, D))
    if layer % 2 == 0:
        uq = jnp.pad(wl["w_uq"][i], ((0, 0), (0, HEAD_PAD - QK), (0, 0)))
        mixer = dict(norm=_row(ws["attn_norm"][i]), w_inT=_mix_in_pad(wl["mix_w_in"][i].reshape(-1, D)),
                     cq_norm=_row(ws["cq_norm"][i]), ckv_norm=_row(ws["ckv_norm"][i]),
                     w_uqT=uq.reshape(HEADS * HEAD_PAD, LORA), w_ukvT=wl["w_ukv"][i].reshape(HEADS * HEAD_PAD, LORA),
                     q_gain=_pad_gain(ws["q_gain"][i]), k_gain=_pad_gain(ws["k_gain"][i]), sconv_w=ws["sconv_w"][i],
                     w_out=_mix_out_pad(wl["mix_w_out"][i].reshape(D, D)))
    else:
        glu = wl["w_glu"][i]
        mixer = dict(norm=_row(ws["ssm_norm"][i]), w_in=wl["ssm_w_in"][i].reshape(D, D), lr=ws["lambda_re"][i],
                     li=ws["lambda_im"][i], ls=ws["log_step"][i].reshape(GROUPS, 1),
                     br=ws["b_re"][i].reshape(GROUPS, STATE * GROUP), bi=ws["b_im"][i].reshape(GROUPS, STATE * GROUP),
                     c_re=ws["c_re"][i], c_im=ws["c_im"][i], d_skip=_row(ws["d_skip"][i]),
                     wgaT=glu[:half].reshape(D, D), wgbT=glu[half:].reshape(D, D))
    return mixer, ffn


def _collect_grads(gm, gf):
    ev, od, half = (0, 2), (1, 3), N_DEV // 2
    st = lambda xs: jnp.stack(xs, axis=0)
    per_dev = list
    halves = lambda a, b, rows: jnp.concatenate([a.reshape(half, rows, D), b.reshape(half, rows, D)], axis=0)
    big = {
        "ffn_w_up": per_dev([halves(gf[l]["wgT"], gf[l]["wvT"], FFN_H // half) for l in range(4)]),
        "ffn_w_down": per_dev([gf[l]["wd"].reshape(N_DEV, -1, D) for l in range(4)]),
        "w_glu": per_dev([halves(gm[l]["wgaT"], gm[l]["wgbT"], D // half) for l in od]),
        "mix_w_out": per_dev([_mix_out_unpad(gm[l]["w_out"]).reshape(N_DEV, -1, D) for l in ev]),
        "ssm_w_in": per_dev([gm[l]["w_in"].reshape(N_DEV, -1, D) for l in od]),
        "w_ukv": per_dev([gm[l]["w_ukvT"].reshape(N_DEV, HEAD_PAD, LORA) for l in ev]),
        "w_uq": per_dev([gm[l]["w_uqT"].reshape(N_DEV, HEAD_PAD, LORA)[:, :QK] for l in ev]),
        "mix_w_in": per_dev([_mix_in_unpad(gm[l]["w_inT"]).reshape(N_DEV, -1, D) for l in ev]),
    }
    small = {
        "attn_norm": st([gm[l]["norm"].reshape(D) for l in ev]),
        "cq_norm": st([gm[l]["cq_norm"].reshape(LORA) for l in ev]),
        "ckv_norm": st([gm[l]["ckv_norm"].reshape(LORA) for l in ev]),
        "q_gain": st([gm[l]["q_gain"].reshape(HEAD_PAD)[:QK] for l in ev]),
        "k_gain": st([gm[l]["k_gain"].reshape(HEAD_PAD)[:QK] for l in ev]),
        "sconv_w": st([gm[l]["sconv_w"] for l in ev]),
        "ssm_norm": st([gm[l]["norm"].reshape(D) for l in od]),
        "lambda_re": st([gm[l]["lr"] for l in od]), "lambda_im": st([gm[l]["li"] for l in od]),
        "log_step": st([gm[l]["ls"] for l in od]),
        "b_re": st([gm[l]["br"] for l in od]), "b_im": st([gm[l]["bi"] for l in od]),
        "c_re": st([gm[l]["c_re"] for l in od]), "c_im": st([gm[l]["c_im"] for l in od]),
        "d_skip": st([gm[l]["d_skip"].reshape(D) for l in od]),
        "ffn_norm": st([gf[l]["norm"].reshape(D) for l in range(4)]),
        "ffn_conv_w": st([jnp.concatenate([gf[l]["cwg"], gf[l]["cwv"]], axis=1) for l in range(4)]),
    }
    return big, small


def _local_step(x, target, wl, ws):
    s = x.shape[0]
    tabs = _rope_tables(s)
    saved, params = [], []
    for layer in range(4):
        mixer, ffn = _layer_params(wl, ws, layer)
        params.append((mixer, ffn))
        if layer % 2 == 0:
            x, sm = _mla_fwd(x, mixer, tabs, f"l{layer}")
        else:
            x, sm = _s5_fwd(x, mixer, f"l{layer}")
        x, sf = _ffn_fwd(x, ffn, f"l{layer}")
        saved.append((sm, sf))
    dx, dxb, loss = _loss_head(x, target, name="loss_head")
    gm, gf = [None] * 4, [None] * 4
    for layer in reversed(range(4)):
        mixer, ffn = params[layer]
        sm, sf = saved[layer]
        dx, dxb, gf[layer] = _ffn_bwd(dx, dxb, ffn, sf, f"l{layer}")
        if layer % 2 == 0:
            dx, dxb, gm[layer] = _mla_bwd(dx, dxb, mixer, tabs, sm, f"l{layer}")
        else:
            dx, dxb, gm[layer] = _s5_bwd(dx, mixer, sm, f"l{layer}")
    return loss, dx, _collect_grads(gm, gf)


def kernel(x, attn_norm, mix_w_in, cq_norm, ckv_norm, w_uq, w_ukv, q_gain, k_gain, sconv_w, mix_w_out, ssm_norm, ssm_w_in, lambda_re, lambda_im, log_step, b_re, b_im, c_re, c_im, d_skip, w_glu, ffn_norm, ffn_w_up, ffn_conv_w, ffn_w_down, loss_target, m_attn_norm, m_mix_w_in, m_cq_norm, m_ckv_norm, m_w_uq, m_w_ukv, m_q_gain, m_k_gain, m_sconv_w, m_mix_w_out, m_ssm_norm, m_ssm_w_in, m_lambda_re, m_lambda_im, m_log_step, m_b_re, m_b_im, m_c_re, m_c_im, m_d_skip, m_w_glu, m_ffn_norm, m_ffn_w_up, m_ffn_conv_w, m_ffn_w_down, v_attn_norm, v_mix_w_in, v_cq_norm, v_ckv_norm, v_w_uq, v_w_ukv, v_q_gain, v_k_gain, v_sconv_w, v_mix_w_out, v_ssm_norm, v_ssm_w_in, v_lambda_re, v_lambda_im, v_log_step, v_b_re, v_b_im, v_c_re, v_c_im, v_d_skip, v_w_glu, v_ffn_norm, v_ffn_w_up, v_ffn_conv_w, v_ffn_w_down):
    args = dict(locals())
    wsh = {n: args[n] for n in WEIGHTS}
    msh = {n: args["m_" + n] for n in WEIGHTS}
    vsh = {n: args["v_" + n] for n in WEIGHTS}
    me = 4 * lax.axis_index("x") + 2 * lax.axis_index("y") + lax.axis_index("c")
    big_names = [n for n, _, _ in BIG]
    slab_shapes = [_slab_shape(sh, ax) for _, sh, ax in BIG]
    small_names = [n for n, _ in REPL] + [n for n, _, _ in SMALL]

    mine = _pack_slabs([_to_slab(wsh[n], ax).astype(bf16) for n, _, ax in BIG], BIG_ROWS)
    gathered, wl, off = _gather8(mine, name="gather_weights"), {}, 0
    for n, (layers, rows, inner) in zip(big_names, slab_shapes):
        per = rows * inner // D
        wl[n] = [gathered[:, off + l * per:off + (l + 1) * per, :].reshape(N_DEV, rows, inner) for l in range(layers)]
        off += layers * per
    placed = []
    for n, shard, axis in SMALL:
        start = [0] * len(shard)
        start[axis] = me * shard[axis]
        placed.append(lax.dynamic_update_slice(jnp.zeros(_full_shape(shard, axis), f32), wsh[n], start))
    small_all = _all_sum(_pack_rows(placed, SMALL_FWD_ROWS), name="small_params")
    ws = dict(zip([n for n, _, _ in SMALL], _unpack_rows(small_all, [_full_shape(sh, ax) for _, sh, ax in SMALL])))
    ws.update({n: wsh[n] for n, _ in REPL})

    loss8, grad_x, (big_grads, grads) = _local_step(x[0], loss_target[0], wl, ws)

    pieces = []
    for n in big_names:
        layers = big_grads[n]
        tiled = (layers[0].shape[1] * layers[0].shape[2] // D) % 16 == 0
        pieces += layers if tiled else [jnp.stack(layers, axis=1)]
    contrib = _pack_slabs(pieces, BIG_ROWS, axis=1)
    chip_sum = _pair_sum(contrib, _pair_exchange(contrib, name="grads_pair_exchange"), name="grads_pair_sum")
    g_big = _sum_slabs(_cross_exchange(chip_sum, name="grads_cross_exchange"), name="grads_chip_sum")
    small_vec = _pack_rows([grads[n] for n, _ in REPL] + [grads[n] for n, _, _ in SMALL] + [loss8[0, :1]], SMALL_ROWS)
    small_sum = _all_sum(small_vec, name="small_grads")
    parts = _unpack_rows(small_sum, [sh for _, sh in REPL] + [_full_shape(sh, ax) for _, sh, ax in SMALL] + [(1,)])
    g = {n: val for (n, _), val in zip(REPL, parts)}
    for (n, shard, axis), val in zip(SMALL, parts[len(REPL):]):
        start = [0] * len(shard)
        start[axis] = me * shard[axis]
        g[n] = lax.dynamic_slice(val, start, shard)
    loss = parts[-1].reshape(())
    for (n, _, axis), val in zip(BIG, _unpack_slabs(g_big, slab_shapes)):
        g[n] = _to_slab(val, axis)

    delta, new_m, new_v = {}, {}, {}
    for n, shard, _ in BIG:
        flat = lambda a: a.reshape(-1, shard[-1])
        outs = _adamw(flat(wsh[n]), flat(g[n]), flat(msh[n]), flat(vsh[n]), name=f"adamw_{n}")
        delta[n], new_m[n], new_v[n] = [o.reshape(shard) for o in outs]
    small_state = [_pack_rows([src[n] for n in small_names], SMALL_ROWS) for src in (wsh, g, msh, vsh)]
    for dst, slab in zip((delta, new_m, new_v), _adamw(*small_state, name="adamw_small")):
        dst.update(zip(small_names, _unpack_rows(slab, [wsh[n].shape for n in small_names])))

    return (loss, grad_x[None], *[g[n] for n in WEIGHTS], *[delta[n] for n in WEIGHTS],
            *[new_m[n] for n in WEIGHTS], *[new_v[n] for n in WEIGHTS])
```

```python
import math

import numpy as np
import jax
import jax.numpy as jnp
from jax import lax
from jax.experimental import pallas as pl
from jax.experimental.pallas import tpu as pltpu

f32, bf16 = jnp.float32, jnp.bfloat16

N_DEV = 8
D = 1024
HEADS = 8
NOPE, ROPE, QK = 64, 32, 96
HEAD_PAD = 128
LORA = 256
CONV_CH = 512
MIX_IN_PAD = 2176
FFN_H = 2816
GROUPS, GROUP, STATE = 64, 16, 64
EPS = 1e-6
ROPE_THETA = 10000.0
ADAM_LR, ADAM_B1, ADAM_B2, ADAM_EPS, ADAM_WD, ADAM_STEP = 0.001, 0.9, 0.999, 1e-08, 0.01, 10
LANES = 128
PAIR_LANES = 2 * LANES
VMEM_LIMIT = 56 << 20
MM_VMEM_BUDGET = 40 << 20
NEG = -1e30

BIG = (
    ("ffn_w_up", (4, 1024, 704), 2), ("ffn_w_down", (4, 352, 1024), 1), ("w_glu", (2, 1024, 256), 2),
    ("mix_w_out", (2, 128, 1024), 1), ("ssm_w_in", (2, 128, 1024), 1), ("w_ukv", (2, 256, 128), 2),
    ("w_uq", (2, 256, 96), 2), ("mix_w_in", (2, 1024, 260), 2))
REPL = (("attn_norm", (2, 1024)), ("cq_norm", (2, 256)), ("ckv_norm", (2, 256)), ("q_gain", (2, 96)),
        ("k_gain", (2, 96)), ("lambda_re", (2, 64, 64)), ("lambda_im", (2, 64, 64)), ("log_step", (2, 64)),
        ("b_re", (2, 64, 64, 16)), ("b_im", (2, 64, 64, 16)), ("c_re", (2, 64, 16, 64)), ("c_im", (2, 64, 16, 64)),
        ("ffn_norm", (4, 1024)))
SMALL = (("sconv_w", (2, 3, 64), 2), ("ssm_norm", (2, 128), 1), ("d_skip", (2, 128), 1), ("ffn_conv_w", (4, 3, 704), 2))
WEIGHTS = ['attn_norm', 'mix_w_in', 'cq_norm', 'ckv_norm', 'w_uq', 'w_ukv', 'q_gain', 'k_gain', 'sconv_w', 'mix_w_out',
           'ssm_norm', 'ssm_w_in', 'lambda_re', 'lambda_im', 'log_step', 'b_re', 'b_im', 'c_re', 'c_im', 'd_skip',
           'w_glu', 'ffn_norm', 'ffn_w_up', 'ffn_conv_w', 'ffn_w_down']
BIG_ROWS = 5888
SMALL_FWD_ROWS = 80
SMALL_ROWS = 640


def _cparams(sem=None, **kw):
    return pltpu.CompilerParams(dimension_semantics=sem, vmem_limit_bytes=VMEM_LIMIT, **kw)


def _tile(n, target):
    best = 0
    for t in range(LANES, min(n, target) + 1, LANES):
        if n % t == 0:
            best = t
    return best if best else n


def _mm(a, b, *, ta=False, tb=False, out_dtype=f32, add=None, twin=False, name, tm=1024, tn=1536):
    m, k = (a.shape[1], a.shape[0]) if ta else a.shape
    n = b.shape[0] if tb else b.shape[1]
    assert (b.shape[1] if tb else b.shape[0]) == k
    tm = _tile(m, tm)
    tn_ = _tile(n, tn)
    tn = n if (tn_ < 256 and n <= 2304) else tn_

    def vmem_bytes(t):
        io = 2 * (tm * t * a.dtype.itemsize + t * tn * b.dtype.itemsize + tm * tn * (jnp.dtype(out_dtype).itemsize + 2 * twin))
        return io + (2 * tm * tn * 4 if add is not None else 0) + (tm * tn * 4 if t < k else 0)

    tk = next((t for t in [k] + [t for t in range(k - LANES, 0, -LANES) if k % t == 0] if vmem_bytes(t) <= MM_VMEM_BUDGET), LANES)
    nk = k // tk
    dn = (((0 if ta else 1,), (1 if tb else 0,)), ((), ()))

    def body(*refs):
        a_ref, b_ref = refs[:2]
        add_ref = refs[2] if add is not None else None
        o_ref = refs[3] if add is not None else refs[2]
        twin_ref = refs[4 if add is not None else 3] if twin else None
        part = lax.dot_general(a_ref[...].astype(bf16), b_ref[...].astype(bf16), dn, preferred_element_type=f32)

        def finish(r):
            if add is not None:
                r = r + add_ref[...].astype(f32)
            o_ref[...] = r.astype(out_dtype)
            if twin:
                twin_ref[...] = r.astype(bf16)

        if nk == 1:
            finish(part)
            return
        acc = refs[-1]
        kk = pl.program_id(2)

        @pl.when(kk == 0)
        def _():
            acc[...] = part

        @pl.when(kk > 0)
        def _():
            acc[...] += part

        @pl.when(kk == nk - 1)
        def _():
            finish(acc[...])

    a_spec = pl.BlockSpec((tk, tm), lambda i, j, kk: (kk, i)) if ta else pl.BlockSpec((tm, tk), lambda i, j, kk: (i, kk))
    b_spec = pl.BlockSpec((tn, tk), lambda i, j, kk: (j, kk)) if tb else pl.BlockSpec((tk, tn), lambda i, j, kk: (kk, j))
    in_specs, args = [a_spec, b_spec], [a, b]
    if add is not None:
        in_specs.append(pl.BlockSpec((tm, tn), lambda i, j, kk: (i, j)))
        args.append(add)
    o_spec, o_shape = pl.BlockSpec((tm, tn), lambda i, j, kk: (i, j)), jax.ShapeDtypeStruct((m, n), out_dtype)
    return pl.pallas_call(
        body, name=name, grid=(m // tm, n // tn, nk), in_specs=in_specs,
        out_specs=[o_spec, o_spec] if twin else o_spec,
        out_shape=[o_shape, jax.ShapeDtypeStruct((m, n), bf16)] if twin else o_shape,
        scratch_shapes=[pltpu.VMEM((tm, tn), f32)] if nk > 1 else [],
        compiler_params=_cparams(("parallel", "parallel", "arbitrary")))(*args)


def _bd_nn(a, w, *, out_dtype=f32, add=None, name, ts=2048):
    s = a.shape[0]
    nb, ka, no = w.shape
    ts = min(ts, s)

    def body(a_ref, w_ref, *rest):
        r = jnp.dot(a_ref[...].astype(bf16), w_ref[0].astype(bf16), preferred_element_type=f32)
        if add is not None:
            r = r + rest[0][...].astype(f32)
        rest[-1][...] = r.astype(out_dtype)

    o_spec = pl.BlockSpec((ts, no), lambda b, i: (i, b))
    return pl.pallas_call(
        body, name=name, grid=(nb, s // ts),
        in_specs=[pl.BlockSpec((ts, ka), lambda b, i: (i, b)), pl.BlockSpec((1, ka, no), lambda b, i: (b, 0, 0))]
        + ([o_spec] if add is not None else []),
        out_specs=o_spec, out_shape=jax.ShapeDtypeStruct((s, nb * no), out_dtype),
        compiler_params=_cparams(("parallel", "parallel")))(a, w, *([add] if add is not None else []))


def _bd_tn_diag(a, g, *, name, ts=2048):
    s = a.shape[0]
    nb = a.shape[1] // LANES
    ts = min(ts, s)
    ni = s // ts

    def body(a_ref, g_ref, o_ref, acc):
        i = pl.program_id(1)
        part = lax.dot_general(a_ref[...].astype(bf16), g_ref[...].astype(bf16), (((0,), (0,)), ((), ())),
                               preferred_element_type=f32)

        @pl.when(i == 0)
        def _():
            acc[...] = part

        @pl.when(i > 0)
        def _():
            acc[...] += part

        @pl.when(i == ni - 1)
        def _():
            for j in range(8):
                o_ref[0, j] = acc[j * GROUP:(j + 1) * GROUP, (j // 2) * PAIR_LANES:(j // 2 + 1) * PAIR_LANES]

    return pl.pallas_call(
        body, name=name, grid=(nb, ni),
        in_specs=[pl.BlockSpec((ts, LANES), lambda b, i: (i, b)), pl.BlockSpec((ts, 8 * LANES), lambda b, i: (i, b))],
        out_specs=pl.BlockSpec((1, 8, GROUP, PAIR_LANES), lambda b, i: (b, 0, 0, 0)),
        out_shape=jax.ShapeDtypeStruct((nb, 8, GROUP, PAIR_LANES), f32),
        scratch_shapes=[pltpu.VMEM((LANES, 8 * LANES), f32)],
        compiler_params=_cparams(("parallel", "arbitrary")))(a, g)


def _rms_fwd(x, g, *, col=0, name, ts=512):
    s, d = x.shape[0], g.shape[1]
    ts = min(ts, s)

    def body(x_ref, g_ref, o_ref):
        xv = x_ref[...].astype(f32)
        r = lax.rsqrt(jnp.mean(xv * xv, axis=-1, keepdims=True) + EPS)
        o_ref[...] = (xv * r * g_ref[...]).astype(bf16)

    return pl.pallas_call(
        body, name=name, grid=(s // ts,),
        in_specs=[pl.BlockSpec((ts, d), lambda i: (i, col)), pl.BlockSpec((1, d), lambda i: (0, 0))],
        out_specs=pl.BlockSpec((ts, d), lambda i: (i, 0)),
        out_shape=jax.ShapeDtypeStruct((s, d), bf16),
        compiler_params=_cparams(("parallel",)))(x, g)


def _rms_bwd(dy, x, g, *, col=0, res=None, out_dtype=f32, name, ts=512):
    s, d = dy.shape
    ts = min(ts, s)
    twin = res is not None

    def body(*refs):
        if twin:
            dy_ref, x_ref, g_ref, res_ref, dx_ref, dxb_ref, dg_ref = refs
        else:
            dy_ref, x_ref, g_ref, dx_ref, dg_ref = refs

        @pl.when(pl.program_id(0) == 0)
        def _():
            dg_ref[...] = jnp.zeros_like(dg_ref)

        xv, dyv = x_ref[...].astype(f32), dy_ref[...].astype(f32)
        r = lax.rsqrt(jnp.mean(xv * xv, axis=-1, keepdims=True) + EPS)
        dyg = dyv * g_ref[...]
        dx = r * dyg - xv * (r * r * r) * jnp.mean(xv * dyg, axis=-1, keepdims=True)
        if twin:
            dx = dx + res_ref[...]
            dxb_ref[...] = dx.astype(bf16)
        dx_ref[...] = dx.astype(out_dtype)
        dg_ref[...] += jnp.sum(dyv * xv * r, axis=0, keepdims=True)

    row, vec = pl.BlockSpec((ts, d), lambda i: (i, 0)), pl.BlockSpec((1, d), lambda i: (0, 0))
    in_specs, args = [row, pl.BlockSpec((ts, d), lambda i: (i, col)), vec], [dy, x, g]
    out_specs, out_shape = [row], [jax.ShapeDtypeStruct((s, d), out_dtype)]
    if twin:
        in_specs.append(row)
        args.append(res)
        out_specs.append(row)
        out_shape.append(jax.ShapeDtypeStruct((s, d), bf16))
    return pl.pallas_call(
        body, name=name, grid=(s // ts,), in_specs=in_specs, out_specs=out_specs + [vec],
        out_shape=out_shape + [jax.ShapeDtypeStruct((1, d), f32)],
        compiler_params=_cparams(("arbitrary",)))(*args)


def _swap_halves(z):
    lane = lax.broadcasted_iota(jnp.int32, z.shape, 1)
    return jnp.where(lane < NOPE + ROPE // 2, pltpu.roll(z, LANES - ROPE // 2, axis=1), pltpu.roll(z, ROPE // 2, axis=1))


def _rope_tables(s):
    inv_freq = 1.0 / (ROPE_THETA ** (jnp.arange(0, ROPE, 2, dtype=f32) / ROPE))
    ang = jnp.arange(s, dtype=f32)[:, None] * inv_freq[None, :]
    cos, sin = jnp.cos(ang), jnp.sin(ang)
    one, zero = jnp.ones((s, NOPE), f32), jnp.zeros((s, NOPE), f32)
    pad1, pad0 = jnp.ones((s, HEAD_PAD - QK), f32), jnp.zeros((s, HEAD_PAD - QK), f32)
    return jnp.concatenate([one, cos, cos, pad1], 1), jnp.concatenate([zero, -sin, sin, pad0], 1)


def _qk_prep_fwd(q_raw, kv_raw, proj, qg, kg, cos_t, sin_t, *, name, ts=1024):
    s = q_raw.shape[0]
    ts = min(ts, s)
    rope_blk = (MIX_IN_PAD - HEAD_PAD) // HEAD_PAD

    def body(q_ref, kv_ref, kr_ref, qg_ref, kg_ref, c_ref, s_ref, qo_ref, ko_ref, vo_ref):
        lane = lax.broadcasted_iota(jnp.int32, (ts, HEAD_PAD), 1)
        cosv, sinv = c_ref[...], s_ref[...]

        def norm_rope(z, gain):
            r = lax.rsqrt(jnp.sum(z * z, axis=-1, keepdims=True) * (1.0 / QK) + EPS)
            zn = z * r * gain
            return zn * cosv + _swap_halves(zn) * sinv

        kvv = kv_ref[...]
        qo_ref[...] = (norm_rope(q_ref[...], qg_ref[...]) * _Q_FOLD).astype(bf16)
        ko_ref[...] = norm_rope(jnp.where(lane < NOPE, kvv, kr_ref[...]), kg_ref[...]).astype(bf16)
        vo_ref[...] = jnp.where(lane >= NOPE, kvv, 0.0).astype(bf16)

    head = pl.BlockSpec((ts, HEAD_PAD), lambda i, h: (i, h))
    row = pl.BlockSpec((ts, HEAD_PAD), lambda i, h: (i, 0))
    vec = pl.BlockSpec((1, HEAD_PAD), lambda i, h: (0, 0))
    out = jax.ShapeDtypeStruct((s, HEADS * HEAD_PAD), bf16)
    return pl.pallas_call(
        body, name=name, grid=(s // ts, HEADS),
        in_specs=[head, head, pl.BlockSpec((ts, HEAD_PAD), lambda i, h: (i, rope_blk)), vec, vec, row, row],
        out_specs=[head, head, head], out_shape=[out, out, out],
        compiler_params=_cparams(("parallel", "parallel")))(q_raw, kv_raw, proj, qg, kg, cos_t, sin_t)


def _qk_prep_bwd(dq, dk, dv, q_raw, kv_raw, proj, qg, kg, cos_t, sin_t, *, name, ts=1024):
    s = q_raw.shape[0]
    ts = min(ts, s)
    rope_blk = (MIX_IN_PAD - HEAD_PAD) // HEAD_PAD

    def body(dq_ref, dk_ref, dv_ref, q_ref, kv_ref, kr_ref, qg_ref, kg_ref, c_ref, s_ref,
             dqr_ref, dkvr_ref, dkr_ref, dqg_ref, dkg_ref):
        i, h = pl.program_id(0), pl.program_id(1)
        lane = lax.broadcasted_iota(jnp.int32, (ts, HEAD_PAD), 1)
        is_rope = (lane >= NOPE) & (lane < QK)
        cosv, sinv = c_ref[...], s_ref[...]

        @pl.when((i == 0) & (h == 0))
        def _():
            dqg_ref[...] = jnp.zeros_like(dqg_ref)
            dkg_ref[...] = jnp.zeros_like(dkg_ref)

        @pl.when(h == 0)
        def _():
            dkr_ref[...] = jnp.zeros_like(dkr_ref)

        def back(dout, z, gain):
            dzn = dout * cosv + jnp.where(is_rope, _swap_halves(dout * sinv), 0.0)
            r = lax.rsqrt(jnp.sum(z * z, axis=-1, keepdims=True) * (1.0 / QK) + EPS)
            dzg = dzn * gain
            dz = r * dzg - z * (r * r * r) * (jnp.sum(z * dzg, axis=-1, keepdims=True) * (1.0 / QK))
            return dz, jnp.sum(dzn * z * r, axis=0, keepdims=True)

        dqz, dqg = back(dq_ref[...].astype(f32), q_ref[...], qg_ref[...])
        dqr_ref[...] = dqz.astype(bf16)
        dqg_ref[...] += dqg
        kvv = kv_ref[...]
        dkz, dkg = back(dk_ref[...].astype(f32), jnp.where(lane < NOPE, kvv, kr_ref[...]), kg_ref[...])
        dkg_ref[...] += dkg
        dkvr_ref[...] = jnp.where(lane < NOPE, dkz, dv_ref[...].astype(f32)).astype(bf16)
        dkr_ref[...] += jnp.where(is_rope, dkz, 0.0)

    head = pl.BlockSpec((ts, HEAD_PAD), lambda i, h: (i, h))
    row = pl.BlockSpec((ts, HEAD_PAD), lambda i, h: (i, 0))
    vec = pl.BlockSpec((1, HEAD_PAD), lambda i, h: (0, 0))
    wide = jax.ShapeDtypeStruct((s, HEADS * HEAD_PAD), bf16)
    return pl.pallas_call(
        body, name=name, grid=(s // ts, HEADS),
        in_specs=[head, head, head, head, head, pl.BlockSpec((ts, HEAD_PAD), lambda i, h: (i, rope_blk)), vec, vec, row, row],
        out_specs=[head, head, row, vec, vec],
        out_shape=[wide, wide, jax.ShapeDtypeStruct((s, HEAD_PAD), f32), jax.ShapeDtypeStruct((1, HEAD_PAD), f32),
                   jax.ShapeDtypeStruct((1, HEAD_PAD), f32)],
        compiler_params=_cparams(("arbitrary", "arbitrary")))(dq, dk, dv, q_raw, kv_raw, proj, qg, kg, cos_t, sin_t)


_NT = (((1,), (1,)), ((), ()))
_SCALE = QK ** -0.5
_LOG2E = math.log2(math.e)
_Q_FOLD = _SCALE * _LOG2E
FLASH_TILE = 1024


def _flash_fwd(q, k, v, *, name, tq=FLASH_TILE):
    s = q.shape[0]
    tq = min(tq, s)

    def body(q_ref, k_ref, v_ref, o_ref, lse_ref):
        i = pl.program_id(1)
        qv = q_ref[...]

        def step(j, carry, masked):
            m, l, acc = carry
            st = pl.multiple_of(j * tq, tq)
            kj, vj = k_ref[pl.ds(st, tq), :], v_ref[pl.ds(st, tq), :]
            sc = lax.dot_general(qv, kj, _NT, preferred_element_type=f32)
            if masked:
                rr = lax.broadcasted_iota(jnp.int32, (tq, tq), 0)
                cc = lax.broadcasted_iota(jnp.int32, (tq, tq), 1)
                sc = jnp.where(cc <= rr, sc, NEG)
            m_new = jnp.maximum(m, jnp.max(sc, axis=-1, keepdims=True))
            p = jnp.exp2(sc - m_new)
            alpha = jnp.exp2(m - m_new)
            l = alpha * l + jnp.sum(p, axis=-1, keepdims=True)
            acc = alpha * acc + jnp.dot(p.astype(bf16), vj, preferred_element_type=f32)
            return m_new, l, acc

        init = (jnp.full((tq, 1), NEG, f32), jnp.zeros((tq, 1), f32), jnp.zeros((tq, HEAD_PAD), f32))
        carry = lax.fori_loop(0, i, lambda j, c: step(j, c, False), init)
        m, l, acc = step(i, carry, True)
        o_ref[...] = (acc / l).astype(bf16)
        lse_ref[0] = m + jnp.log2(l)

    blk = pl.BlockSpec((tq, HEAD_PAD), lambda h, i: (i, h))
    full = pl.BlockSpec((s, HEAD_PAD), lambda h, i: (0, h))
    return pl.pallas_call(
        body, name=name, grid=(HEADS, s // tq), in_specs=[blk, full, full],
        out_specs=[blk, pl.BlockSpec((1, tq, 1), lambda h, i: (h, i, 0))],
        out_shape=[jax.ShapeDtypeStruct((s, HEADS * HEAD_PAD), bf16), jax.ShapeDtypeStruct((HEADS, s, 1), f32)],
        compiler_params=_cparams(("parallel", "arbitrary")))(q, k, v)


def _flash_bwd_dq(q, k, v, o, do, lse, *, name, tq=FLASH_TILE):
    s = q.shape[0]
    tq = min(tq, s)

    def body(q_ref, k_ref, v_ref, o_ref, do_ref, lse_ref, dq_ref, dl_ref):
        i = pl.program_id(1)
        qv = q_ref[...]
        dov = do_ref[...].astype(f32)
        delta = jnp.sum(dov * o_ref[...].astype(f32), axis=-1, keepdims=True)
        dob = dov.astype(bf16)
        lsev = lse_ref[0]

        def step(j, acc, masked):
            st = pl.multiple_of(j * tq, tq)
            kj, vj = k_ref[pl.ds(st, tq), :], v_ref[pl.ds(st, tq), :]
            sc = lax.dot_general(qv, kj, _NT, preferred_element_type=f32)
            p = jnp.exp2(sc - lsev)
            if masked:
                rr = lax.broadcasted_iota(jnp.int32, (tq, tq), 0)
                cc = lax.broadcasted_iota(jnp.int32, (tq, tq), 1)
                p = jnp.where(cc <= rr, p, 0.0)
            dp = lax.dot_general(dob, vj, _NT, preferred_element_type=f32)
            ds = p * (dp - delta)
            return acc + jnp.dot(ds.astype(bf16), kj, preferred_element_type=f32)

        acc = lax.fori_loop(0, i, lambda j, c: step(j, c, False), jnp.zeros((tq, HEAD_PAD), f32))
        dq_ref[...] = step(i, acc, True) * _SCALE
        dl_ref[0] = delta

    blk = pl.BlockSpec((tq, HEAD_PAD), lambda h, i: (i, h))
    full = pl.BlockSpec((s, HEAD_PAD), lambda h, i: (0, h))
    col = pl.BlockSpec((1, tq, 1), lambda h, i: (h, i, 0))
    return pl.pallas_call(
        body, name=name, grid=(HEADS, s // tq), in_specs=[blk, full, full, blk, blk, col],
        out_specs=[blk, col],
        out_shape=[jax.ShapeDtypeStruct((s, HEADS * HEAD_PAD), f32), jax.ShapeDtypeStruct((HEADS, s, 1), f32)],
        compiler_params=_cparams(("parallel", "arbitrary")))(q, k, v, o, do, lse)


def _flash_bwd_dkv(q, k, v, do, lse_row, delta_row, *, name, tk=FLASH_TILE):
    s = q.shape[0]
    tk = min(tk, s)
    nblk = s // tk

    def body(q_ref, k_ref, v_ref, do_ref, lse_ref, dl_ref, dk_ref, dv_ref):
        j = pl.program_id(1)
        kv_, vv = k_ref[...], v_ref[...]

        def step(i, carry, masked):
            dk, dv = carry
            st = pl.multiple_of(i * tk, tk)
            qi = q_ref[pl.ds(st, tk), :]
            doi = do_ref[pl.ds(st, tk), :].astype(bf16)
            lse_i = lse_ref[0, :, pl.ds(st, tk)]
            dl_i = dl_ref[0, :, pl.ds(st, tk)]
            st_ = lax.dot_general(kv_, qi, _NT, preferred_element_type=f32)
            pt = jnp.exp2(st_ - lse_i)
            if masked:
                kk = lax.broadcasted_iota(jnp.int32, (tk, tk), 0)
                qq = lax.broadcasted_iota(jnp.int32, (tk, tk), 1)
                pt = jnp.where(kk <= qq, pt, 0.0)
            dv = dv + jnp.dot(pt.astype(bf16), doi, preferred_element_type=f32)
            dpt = lax.dot_general(vv, doi, _NT, preferred_element_type=f32)
            dst = pt * (dpt - dl_i)
            dk = dk + jnp.dot(dst.astype(bf16), qi, preferred_element_type=f32)
            return dk, dv

        zero = jnp.zeros((tk, HEAD_PAD), f32)
        carry = step(j, (zero, zero), True)
        dk, dv = lax.fori_loop(j + 1, nblk, lambda i, c: step(i, c, False), carry)
        dk_ref[...] = dk * (1.0 / _LOG2E)
        dv_ref[...] = dv

    blk = pl.BlockSpec((tk, HEAD_PAD), lambda h, j: (j, h))
    full = pl.BlockSpec((s, HEAD_PAD), lambda h, j: (0, h))
    rowv = pl.BlockSpec((1, 1, s), lambda h, j: (h, 0, 0))
    out = jax.ShapeDtypeStruct((s, HEADS * HEAD_PAD), f32)
    return pl.pallas_call(
        body, name=name, grid=(HEADS, nblk), in_specs=[full, blk, blk, full, rowv, rowv],
        out_specs=[blk, blk], out_shape=[out, out],
        compiler_params=_cparams(("parallel", "arbitrary")))(q, k, v, do, lse_row, delta_row)


SUBLANES = 8


def _shift_down(x, d):
    r = pltpu.roll(x, d, axis=0)
    t = lax.broadcasted_iota(jnp.int32, (SUBLANES, x.shape[1]), 0)
    head = jnp.where(t < d, 0.0, r[:SUBLANES])
    return head if x.shape[0] == SUBLANES else jnp.concatenate([head, r[SUBLANES:]], axis=0)


def _shift_up(x, d):
    s = x.shape[0]
    r = pltpu.roll(x, s - d, axis=0)
    t = lax.broadcasted_iota(jnp.int32, (SUBLANES, x.shape[1]), 0)
    tail = jnp.where(t >= SUBLANES - d, 0.0, r[s - SUBLANES:])
    return tail if s == SUBLANES else jnp.concatenate([r[:s - SUBLANES], tail], axis=0)


def _taps(w_ref):
    return w_ref[0:1, :], w_ref[1:2, :], w_ref[2:3, :]


def _conv3(u, w):
    u1, u2 = _shift_down(u, 1), _shift_down(u, 2)
    return w[0] * u2 + w[1] * u1 + w[2] * u, (u1, u2)


def _ref_shift_down(ref, d):
    s = ref.shape[0]
    return jnp.concatenate([_shift_down(ref[0:SUBLANES, :], d), ref[pl.ds(SUBLANES - d, s - SUBLANES), :]], axis=0)


def _conv3_ref(ref, w):
    return w[0] * _ref_shift_down(ref, 2) + w[1] * _ref_shift_down(ref, 1) + w[2] * ref[...]


def _conv3_t(g, w):
    return w[2] * g + w[1] * _shift_up(g, 1) + w[0] * _shift_up(g, 2)


def _conv3_dw(dw_ref, g, u, shifted):
    dw_ref[0:1, :] = jnp.sum(g * shifted[1], axis=0, keepdims=True)
    dw_ref[1:2, :] = jnp.sum(g * shifted[0], axis=0, keepdims=True)
    dw_ref[2:3, :] = jnp.sum(g * u, axis=0, keepdims=True)


_GB, _GC, _CI = 512 // LANES, 1024 // LANES, 1536 // LANES


def _sconv_fwd(proj, w, *, name):
    s = proj.shape[0]

    def body(gb_ref, gc_ref, ci_ref, w_ref, o_ref):
        o_ref[...] = (gb_ref[...] * _conv3(gc_ref[...] * ci_ref[...], _taps(w_ref))[0]).astype(bf16)

    col = lambda off: pl.BlockSpec((s, LANES), lambda j: (0, off + j))
    return pl.pallas_call(
        body, name=name, grid=(CONV_CH // LANES,),
        in_specs=[col(_GB), col(_GC), col(_CI), pl.BlockSpec((3, LANES), lambda j: (0, j))],
        out_specs=pl.BlockSpec((s, LANES), lambda j: (0, j)),
        out_shape=jax.ShapeDtypeStruct((s, CONV_CH), bf16),
        compiler_params=_cparams(("parallel",)))(proj, proj, proj, w)


def _sconv_bwd(dmix, proj, w, *, name):
    s = proj.shape[0]

    def body(do_ref, gb_ref, gc_ref, ci_ref, w_ref, dgb_ref, dgc_ref, dci_ref, dw_ref):
        wv, gc, ci, do = _taps(w_ref), gc_ref[...], ci_ref[...], do_ref[...].astype(f32)
        u = gc * ci
        conv, shifted = _conv3(u, wv)
        dgb_ref[...] = (do * conv).astype(bf16)
        dc = do * gb_ref[...]
        du = _conv3_t(dc, wv)
        dgc_ref[...] = (du * ci).astype(bf16)
        dci_ref[...] = (du * gc).astype(bf16)
        _conv3_dw(dw_ref, dc, u, shifted)

    col = lambda off: pl.BlockSpec((s, LANES), lambda j: (0, off + j))
    out = jax.ShapeDtypeStruct((s, CONV_CH), bf16)
    return pl.pallas_call(
        body, name=name, grid=(CONV_CH // LANES,),
        in_specs=[col(HEADS), col(_GB), col(_GC), col(_CI), pl.BlockSpec((3, LANES), lambda j: (0, j))],
        out_specs=[col(0), col(0), col(0), pl.BlockSpec((3, LANES), lambda j: (0, j))],
        out_shape=[out, out, out, jax.ShapeDtypeStruct((3, CONV_CH), f32)],
        compiler_params=_cparams(("parallel",)))(dmix, proj, proj, proj, w)


def _ffn_act_fwd(zg, zv, cwg, cwv, *, name):
    s, f = zg.shape

    def body(zg_ref, zv_ref, wg_ref, wv_ref, o_ref):
        o_ref[...] = (jax.nn.silu(_conv3_ref(zg_ref, _taps(wg_ref))) * _conv3_ref(zv_ref, _taps(wv_ref))).astype(bf16)

    col = pl.BlockSpec((s, LANES), lambda j: (0, j))
    wsp = pl.BlockSpec((3, LANES), lambda j: (0, j))
    return pl.pallas_call(
        body, name=name, grid=(f // LANES,), in_specs=[col, col, wsp, wsp], out_specs=col,
        out_shape=jax.ShapeDtypeStruct((s, f), bf16), compiler_params=_cparams(("parallel",)))(zg, zv, cwg, cwv)


def _ffn_act_bwd(da, zg, zv, cwg, cwv, *, name):
    s, f = zg.shape

    def body(da_ref, zg_ref, zv_ref, wg_ref, wv_ref, dzg_ref, dzv_ref, dwg_ref, dwv_ref):
        wg, wv, dav = _taps(wg_ref), _taps(wv_ref), da_ref[...].astype(f32)
        ug, uv = _conv3_ref(zg_ref, wg), _conv3_ref(zv_ref, wv)
        sg = jax.nn.sigmoid(ug)
        dug = dav * uv * (sg * (1.0 + ug * (1.0 - sg)))
        duv = dav * (ug * sg)
        dzg_ref[...] = _conv3_t(dug, wg).astype(bf16)
        dzv_ref[...] = _conv3_t(duv, wv).astype(bf16)
        _conv3_dw(dwg_ref, dug, zg_ref[...], (_ref_shift_down(zg_ref, 1), _ref_shift_down(zg_ref, 2)))
        _conv3_dw(dwv_ref, duv, zv_ref[...], (_ref_shift_down(zv_ref, 1), _ref_shift_down(zv_ref, 2)))

    col = pl.BlockSpec((s, LANES), lambda j: (0, j))
    wsp = pl.BlockSpec((3, LANES), lambda j: (0, j))
    act, wsh = jax.ShapeDtypeStruct((s, f), bf16), jax.ShapeDtypeStruct((3, f), f32)
    return pl.pallas_call(
        body, name=name, grid=(f // LANES,), in_specs=[col, col, col, wsp, wsp], out_specs=[col, col, wsp, wsp],
        out_shape=[act, act, wsh, wsh], compiler_params=_cparams(("parallel",)))(da, zg, zv, cwg, cwv)


def _expand_mat():
    return jnp.asarray(np.kron(np.eye(STATE, dtype=np.float32), np.ones((1, GROUP), np.float32)))


def _disc_fn(lr, li, ls, br, bi, e):
    dt = jnp.exp(ls)
    mag = jnp.exp(lr * dt)
    ar, ai = mag * jnp.cos(li * dt), mag * jnp.sin(li * dt)
    nr, ni = ar - 1.0, ai
    den = lr * lr + li * li
    zr, zi = (nr * lr + ni * li) / den, (ni * lr - nr * li) / den
    zrr = jnp.dot(zr, e, precision=lax.Precision.HIGHEST, preferred_element_type=f32)
    zir = jnp.dot(zi, e, precision=lax.Precision.HIGHEST, preferred_element_type=f32)
    return ar, ai, zrr * br - zir * bi, zrr * bi + zir * br


def _disc_fwd(lr, li, ls, br, bi, *, name):
    def body(lr_ref, li_ref, ls_ref, br_ref, bi_ref, e_ref, ar_ref, ai_ref, bbr_ref, bbi_ref):
        ar, ai, bbr, bbi = _disc_fn(lr_ref[...], li_ref[...], ls_ref[...], br_ref[...], bi_ref[...], e_ref[...])
        ar_ref[...], ai_ref[...], bbr_ref[...], bbi_ref[...] = ar, ai, bbr, bbi

    sq, wide = jax.ShapeDtypeStruct((GROUPS, STATE), f32), jax.ShapeDtypeStruct((GROUPS, STATE * GROUP), f32)
    return pl.pallas_call(body, name=name, out_shape=[sq, sq, wide, wide],
                          compiler_params=_cparams())(lr, li, ls, br, bi, _expand_mat())


def _disc_bwd(lr, li, ls, br, bi, dar, dai, dbbr, dbbi, *, name):
    def body(lr_ref, li_ref, ls_ref, br_ref, bi_ref, e_ref, dar_ref, dai_ref, dbbr_ref, dbbi_ref,
             dlr_ref, dli_ref, dls_ref, dbr_ref, dbi_ref):
        ev = e_ref[...]
        _, vjp = jax.vjp(lambda a, b, c, d_, e_: _disc_fn(a, b, c, d_, e_, ev),
                         lr_ref[...], li_ref[...], ls_ref[...], br_ref[...], bi_ref[...])
        dlr, dli, dls, dbr, dbi = vjp((dar_ref[...], dai_ref[...], dbbr_ref[...], dbbi_ref[...]))
        dlr_ref[...], dli_ref[...], dls_ref[...], dbr_ref[...], dbi_ref[...] = dlr, dli, dls, dbr, dbi

    sq, wide = jax.ShapeDtypeStruct((GROUPS, STATE), f32), jax.ShapeDtypeStruct((GROUPS, STATE * GROUP), f32)
    return pl.pallas_call(body, name=name, out_shape=[sq, sq, jax.ShapeDtypeStruct((GROUPS, 1), f32), wide, wide],
                          compiler_params=_cparams())(lr, li, ls, br, bi, _expand_mat(), dar, dai, dbbr, dbbi)


SCAN_TILE = 32
SCAN_PAIRS = 4


def _tile_shift(v, d, reverse):
    if d % 8:
        return _shift_up(v, d) if reverse else _shift_down(v, d)
    z = jnp.zeros((d, v.shape[1]), v.dtype)
    return jnp.concatenate([v[d:], z], axis=0) if reverse else jnp.concatenate([z, v[:v.shape[0] - d]], axis=0)


def _tile_scan(r, i, pows, reverse):
    d = 1
    for br, bi in pows:
        rs, is_ = _tile_shift(r, d, reverse), _tile_shift(i, d, reverse)
        r, i = r + br * rs - bi * is_, i + br * is_ + bi * rs
        d *= 2
    return r, i


def _scan_setup(ar, ai, reverse):
    if reverse:
        ai = -ai
    pows, br, bi, d = [], ar, ai, 1
    while d < SCAN_TILE:
        pows.append((br, bi))
        br, bi, d = br * br - bi * bi, 2.0 * br * bi, 2 * d
    row = lax.broadcasted_iota(jnp.int32, (SCAN_TILE, LANES), 0)
    hit = row == (SCAN_TILE - 1 if reverse else 0)
    pr, pi = _tile_scan(jnp.where(hit, ar, 0.0), jnp.where(hit, ai, 0.0), pows, reverse)
    return pows, pr, pi


def _carry_in(r, i, pr, pi, cr, ci):
    crb, cib = jnp.broadcast_to(cr, r.shape), jnp.broadcast_to(ci, i.shape)
    return r + pr * crb - pi * cib, i + pr * cib + pi * crb


def _pair_cols(q):
    return slice(q * PAIR_LANES, q * PAIR_LANES + LANES), slice(q * PAIR_LANES + LANES, (q + 1) * PAIR_LANES)


_SCAN_W = SCAN_PAIRS * PAIR_LANES


def _scan_specs(s, w):
    per = w.shape[2] // _SCAN_W
    src = pl.BlockSpec((s, LANES), lambda g: (0, g // per))
    mat = pl.BlockSpec((1, LANES, _SCAN_W), lambda g: (g // per, 0, g % per))
    col = pl.BlockSpec((s, _SCAN_W), lambda g: (0, g))
    vec = pl.BlockSpec((SCAN_PAIRS, 1, LANES), lambda g: (g, 0, 0))
    return src, mat, col, vec, (w.shape[0] * per,)


def _scan_fwd(u, wb, ar, ai, *, name):
    s = u.shape[0]
    nt = s // SCAN_TILE

    def body(u_ref, w_ref, ar_ref, ai_ref, x_ref):
        setups = [_scan_setup(ar_ref[q], ai_ref[q], False) for q in range(SCAN_PAIRS)]
        wv = w_ref[0]

        def tile_rows(k):
            return pl.ds(pl.multiple_of(k * SCAN_TILE, SCAN_TILE), SCAN_TILE)

        def tile_in(k):
            return jnp.dot(u_ref[tile_rows(k), :].astype(bf16), wv, preferred_element_type=f32)

        def step(k, carry):
            rows, bu = tile_rows(k), carry[-1]
            ahead = tile_in(jnp.minimum(k + 1, nt - 1))
            out = []
            for q, (pows, pr, pi) in enumerate(setups):
                rc, ic = _pair_cols(q)
                r, i = _tile_scan(bu[:, rc], bu[:, ic], pows, False)
                r, i = _carry_in(r, i, pr, pi, carry[2 * q], carry[2 * q + 1])
                x_ref[rows, rc] = r.astype(bf16)
                x_ref[rows, ic] = i.astype(bf16)
                out += [r[SCAN_TILE - 1:SCAN_TILE, :], i[SCAN_TILE - 1:SCAN_TILE, :]]
            return tuple(out) + (ahead,)

        lax.fori_loop(0, nt, step, tuple(jnp.zeros((1, LANES), f32) for _ in range(2 * SCAN_PAIRS)) + (tile_in(0),))

    src, mat, col, vec, grid = _scan_specs(s, wb)
    return pl.pallas_call(body, name=name, grid=grid, in_specs=[src, mat, vec, vec], out_specs=col,
                          out_shape=jax.ShapeDtypeStruct((s, wb.shape[0] * wb.shape[2]), bf16),
                          compiler_params=_cparams(("parallel",)))(u, wb, ar, ai)


def _scan_bwd(dy, cbt, x, ar, ai, *, name):
    s = dy.shape[0]
    nt = s // SCAN_TILE

    def fold(v):
        out = v[0:8]
        for r in range(8, SCAN_TILE, 8):
            out = out + v[r:r + 8]
        return out

    def body(dy_ref, w_ref, x_ref, ar_ref, ai_ref, g_ref, dar_ref, dai_ref):
        setups = [_scan_setup(ar_ref[q], ai_ref[q], True) for q in range(SCAN_PAIRS)]
        row = lax.broadcasted_iota(jnp.int32, (SCAN_TILE, LANES), 0)
        wv = w_ref[0]

        def tile_in(k):
            return jnp.dot(dy_ref[pl.ds(pl.multiple_of(k * SCAN_TILE, SCAN_TILE), SCAN_TILE), :], wv, preferred_element_type=f32)

        def step(kk, carry):
            k = nt - 1 - kk
            start = pl.multiple_of(k * SCAN_TILE, SCAN_TILE)
            rows = pl.ds(start, SCAN_TILE)
            prev16 = pl.ds(pl.multiple_of(jnp.maximum(start - 16, 0), 16), 16)
            dx = carry[-1]
            ahead = tile_in(jnp.maximum(k - 1, 0))

            def before(cols):
                first = jnp.where(k > 0, x_ref[prev16, cols][15:16, :].astype(f32), 0.0)
                return jnp.where(row == 0, first, pltpu.roll(x_ref[rows, cols].astype(f32), 1, axis=0))

            out = []
            for q, (pows, pr, pi) in enumerate(setups):
                rc, ic = _pair_cols(q)
                cr, ci, acc_r, acc_i = carry[4 * q:4 * q + 4]
                gr, gi = _tile_scan(dx[:, rc], dx[:, ic], pows, True)
                gr, gi = _carry_in(gr, gi, pr, pi, cr, ci)
                g_ref[rows, rc] = gr.astype(bf16)
                g_ref[rows, ic] = gi.astype(bf16)
                xr, xi = before(rc), before(ic)
                out += [gr[0:1, :], gi[0:1, :], acc_r + fold(gr * xr + gi * xi), acc_i + fold(gi * xr - gr * xi)]
            return tuple(out) + (ahead,)

        init = (jnp.zeros((1, LANES), f32), jnp.zeros((1, LANES), f32), jnp.zeros((8, LANES), f32), jnp.zeros((8, LANES), f32))
        res = lax.fori_loop(0, nt, step, init * SCAN_PAIRS + (tile_in(nt - 1),))
        for q in range(SCAN_PAIRS):
            dar_ref[q] = jnp.sum(res[4 * q + 2], axis=0, keepdims=True)
            dai_ref[q] = jnp.sum(res[4 * q + 3], axis=0, keepdims=True)

    src, mat, col, vec, grid = _scan_specs(s, cbt)
    vsh = jax.ShapeDtypeStruct((GROUPS // 2, 1, LANES), f32)
    return pl.pallas_call(body, name=name, grid=grid, in_specs=[src, mat, col, vec, vec],
                          out_specs=[col, vec, vec], out_shape=[jax.ShapeDtypeStruct(x.shape, bf16), vsh, vsh],
                          compiler_params=_cparams(("parallel",)))(dy, cbt, x, ar, ai)


_GELU_C = math.sqrt(2.0 / math.pi)


def _gelu_fwd(y, u, dsk, *, name, ts=512):
    s, d = y.shape
    ts = min(ts, s)

    def body(y_ref, u_ref, d_ref, o_ref):
        o_ref[...] = jax.nn.gelu(y_ref[...] + d_ref[...] * u_ref[...]).astype(bf16)

    row, vec = pl.BlockSpec((ts, d), lambda i: (i, 0)), pl.BlockSpec((1, d), lambda i: (0, 0))
    return pl.pallas_call(body, name=name, grid=(s // ts,), in_specs=[row, row, vec], out_specs=row,
                          out_shape=jax.ShapeDtypeStruct((s, d), bf16), compiler_params=_cparams(("parallel",)))(y, u, dsk)


def _gelu_bwd(dg, y, u, dsk, *, name, ts=512):
    s, d = y.shape
    ts = min(ts, s)

    def body(dg_ref, y_ref, u_ref, d_ref, dy_ref, du_ref, dd_ref):
        @pl.when(pl.program_id(0) == 0)
        def _():
            dd_ref[...] = jnp.zeros_like(dd_ref)

        uv, dv = u_ref[...], d_ref[...]
        z = y_ref[...] + dv * uv
        th = jnp.tanh(_GELU_C * (z + 0.044715 * z * z * z))
        dz = dg_ref[...] * (0.5 * (1.0 + th) + 0.5 * z * (1.0 - th * th) * _GELU_C * (1.0 + 3 * 0.044715 * z * z))
        dy_ref[...] = dz.astype(bf16)
        du_ref[...] = dz * dv
        dd_ref[...] += jnp.sum(dz * uv, axis=0, keepdims=True)

    row, vec = pl.BlockSpec((ts, d), lambda i: (i, 0)), pl.BlockSpec((1, d), lambda i: (0, 0))
    return pl.pallas_call(
        body, name=name, grid=(s // ts,), in_specs=[row, row, row, vec], out_specs=[row, row, vec],
        out_shape=[jax.ShapeDtypeStruct((s, d), bf16), jax.ShapeDtypeStruct((s, d), f32), jax.ShapeDtypeStruct((1, d), f32)],
        compiler_params=_cparams(("arbitrary",)))(dg, y, u, dsk)


def _glu_fwd(x, a, b, *, name, ts=512):
    s, d = x.shape
    ts = min(ts, s)

    def body(x_ref, a_ref, b_ref, o_ref):
        o_ref[...] = x_ref[...] + a_ref[...] * jax.nn.sigmoid(b_ref[...])

    row = pl.BlockSpec((ts, d), lambda i: (i, 0))
    return pl.pallas_call(body, name=name, grid=(s // ts,), in_specs=[row, row, row], out_specs=row,
                          out_shape=jax.ShapeDtypeStruct((s, d), f32), compiler_params=_cparams(("parallel",)))(x, a, b)


def _glu_bwd(dx, a, b, *, name, ts=512):
    s, d = dx.shape
    ts = min(ts, s)

    def body(dx_ref, a_ref, b_ref, da_ref, db_ref):
        sg = jax.nn.sigmoid(b_ref[...])
        dxv = dx_ref[...]
        da_ref[...] = (dxv * sg).astype(bf16)
        db_ref[...] = (dxv * a_ref[...] * sg * (1.0 - sg)).astype(bf16)

    row = pl.BlockSpec((ts, d), lambda i: (i, 0))
    out = jax.ShapeDtypeStruct((s, d), bf16)
    return pl.pallas_call(body, name=name, grid=(s // ts,), in_specs=[row, row, row], out_specs=[row, row],
                          out_shape=[out, out], compiler_params=_cparams(("parallel",)))(dx, a, b)


def _loss_head(y, target, *, name, ts=512):
    s, d = y.shape
    ts = min(ts, s)

    def body(y_ref, t_ref, dy_ref, dyb_ref, l_ref):
        @pl.when(pl.program_id(0) == 0)
        def _():
            l_ref[...] = jnp.zeros_like(l_ref)

        e = y_ref[...] - t_ref[...]
        dy = e * (1.0 / d)
        dy_ref[...] = dy
        dyb_ref[...] = dy.astype(bf16)
        l_ref[...] += 0.5 * jnp.sum(jnp.mean(e * e, axis=-1, keepdims=True))

    row = pl.BlockSpec((ts, d), lambda i: (i, 0))
    return pl.pallas_call(
        body, name=name, grid=(s // ts,), in_specs=[row, row],
        out_specs=[row, row, pl.BlockSpec((8, LANES), lambda i: (0, 0))],
        out_shape=[jax.ShapeDtypeStruct((s, d), f32), jax.ShapeDtypeStruct((s, d), bf16), jax.ShapeDtypeStruct((8, LANES), f32)],
        compiler_params=_cparams(("arbitrary",)))(y, target)


def _adamw(w, g, m, v, *, name, tr=128):
    r, c = w.shape

    def body(w_ref, g_ref, m_ref, v_ref, d_ref, mo_ref, vo_ref):
        gv = g_ref[...]
        mn = ADAM_B1 * m_ref[...] + (1.0 - ADAM_B1) * gv
        vn = ADAM_B2 * v_ref[...] + (1.0 - ADAM_B2) * (gv * gv)
        m_hat = mn / (1.0 - ADAM_B1 ** ADAM_STEP)
        v_hat = vn / (1.0 - ADAM_B2 ** ADAM_STEP)
        d_ref[...] = -ADAM_LR * (m_hat / (jnp.sqrt(v_hat) + ADAM_EPS) + ADAM_WD * w_ref[...])
        mo_ref[...] = mn
        vo_ref[...] = vn

    row = pl.BlockSpec((tr, c), lambda i: (i, 0))
    out = jax.ShapeDtypeStruct((r, c), f32)
    return pl.pallas_call(body, name=name, grid=(r // tr,), in_specs=[row] * 4, out_specs=[row] * 3,
                          out_shape=[out, out, out], compiler_params=_cparams(("parallel",)))(w, g, m, v)


def _sum_slabs(land, *, name, tr=128):
    n, r, c = land.shape

    def body(l_ref, o_ref):
        acc = l_ref[0].astype(f32)
        for i in range(1, n):
            acc = acc + l_ref[i].astype(f32)
        o_ref[...] = acc

    return pl.pallas_call(body, name=name, grid=(r // tr,), in_specs=[pl.BlockSpec((n, tr, c), lambda i: (0, i, 0))],
                          out_specs=pl.BlockSpec((tr, c), lambda i: (i, 0)), out_shape=jax.ShapeDtypeStruct((r, c), f32),
                          compiler_params=_cparams(("parallel",)))(land)


def _pair_sum(g, theirs, *, name, tr=256):
    n, r, c = theirs.shape

    def body(c_ref, g_ref, t_ref, o_ref):
        o_ref[...] = (g_ref[...].astype(f32) + t_ref[...].astype(f32)).astype(bf16)

    blk = pl.BlockSpec((1, tr, c), lambda j, i, c_ref: (j, i, 0))
    mine = pl.BlockSpec((1, tr, c), lambda j, i, c_ref: (2 * j + c_ref[0], i, 0))
    return pl.pallas_call(
        body, name=name,
        grid_spec=pltpu.PrefetchScalarGridSpec(num_scalar_prefetch=1, grid=(n, r // tr), in_specs=[mine, blk], out_specs=blk),
        out_shape=jax.ShapeDtypeStruct(theirs.shape, bf16),
        compiler_params=_cparams(("parallel", "parallel")))(lax.axis_index("c").astype(jnp.int32).reshape(1), g, theirs)


_MESH = pl.DeviceIdType.MESH
_HBM = pl.BlockSpec(memory_space=pltpu.HBM)
N_CHIP = N_DEV // 2


def _position():
    return lax.axis_index("x"), lax.axis_index("y"), lax.axis_index("c")


def _gather8(x, *, name):
    half = x.shape[0] // 2

    def body(x_ref, o_ref, send_sems, recv_sems, local_sem):
        xx, yy, cc = _position()
        me, sibling = (xx, yy, cc), (xx, yy, 1 - cc)
        here, xn, yn, dg = (xx, yy), (1 - xx, yy), (xx, 1 - yy), (1 - xx, 1 - yy)
        first, second = pl.ds(0, half), pl.ds(half, half)

        def slab(chip, pc, rows=None):
            ref = o_ref.at[4 * chip[0] + 2 * chip[1] + pc]
            return ref if rows is None else ref.at[rows]

        def copy(k, ref, to, src=None):
            return pltpu.make_async_remote_copy(src_ref=ref if src is None else src, dst_ref=ref, send_sem=send_sems.at[k],
                                                recv_sem=recv_sems.at[k], device_id=to, device_id_type=_MESH)

        mine = pltpu.make_async_copy(x_ref, slab(here, cc), local_sem)
        mine.start()
        sends = [copy(0, slab(here, cc), sibling, src=x_ref), copy(1, slab(here, cc), (*xn, cc), src=x_ref),
                 copy(2, slab(here, cc), (*yn, cc), src=x_ref)]
        for cp in sends:
            cp.start()
        copy(1, slab(xn, cc), me).wait_recv()
        sends += [copy(3, slab(xn, cc, first), (*yn, cc)), copy(5, slab(xn, cc), sibling)]
        copy(2, slab(yn, cc), me).wait_recv()
        sends += [copy(4, slab(yn, cc, second), (*xn, cc)), copy(6, slab(yn, cc), sibling)]
        for cp in sends[3:]:
            cp.start()
        copy(3, slab(dg, cc, first), me).wait_recv()
        copy(4, slab(dg, cc, second), me).wait_recv()
        sends.append(copy(7, slab(dg, cc), sibling))
        sends[-1].start()
        for k, chip in ((0, here), (5, xn), (6, yn), (7, dg)):
            copy(k, slab(chip, 1 - cc), me).wait_recv()
        for cp in sends:
            cp.wait_send()
        mine.wait()

    return pl.pallas_call(
        body, name=name, in_specs=[_HBM], out_specs=_HBM, out_shape=jax.ShapeDtypeStruct((N_DEV,) + x.shape, x.dtype),
        scratch_shapes=[pltpu.SemaphoreType.DMA((N_DEV,)), pltpu.SemaphoreType.DMA((N_DEV,)), pltpu.SemaphoreType.DMA],
    )(x)


def _pair_exchange(g, *, name):
    def body(g_ref, land_ref, send_sems, recv_sems):
        xx, yy, cc = _position()
        copies = []
        for j in range(N_CHIP):
            cp = pltpu.make_async_remote_copy(src_ref=g_ref.at[2 * j + 1 - cc], dst_ref=land_ref.at[j], send_sem=send_sems.at[j],
                                              recv_sem=recv_sems.at[j], device_id=(xx, yy, 1 - cc), device_id_type=_MESH)
            cp.start()
            copies.append(cp)
        for cp in copies:
            cp.wait_recv()
        for cp in copies:
            cp.wait_send()

    sems = pltpu.SemaphoreType.DMA((N_CHIP,))
    return pl.pallas_call(body, name=name, in_specs=[_HBM], out_specs=_HBM,
                          out_shape=jax.ShapeDtypeStruct((N_CHIP,) + g.shape[1:], g.dtype), scratch_shapes=[sems, sems])(g)


def _cross_exchange(p, *, name):
    half = p.shape[1] // 2

    def body(p_ref, o_ref, relay_ref, send_sems, recv_sems, local_sem):
        xx, yy, cc = _position()
        me = (xx, yy, cc)
        xn, yn, dg = (1 - xx, yy), (xx, 1 - yy), (1 - xx, 1 - yy)
        idx = lambda chip: 2 * chip[0] + chip[1]
        mine = idx((xx, yy))
        first, second = pl.ds(0, half), pl.ds(half, half)

        def copy(k, src, dst, to):
            return pltpu.make_async_remote_copy(src_ref=src, dst_ref=dst, send_sem=send_sems.at[k], recv_sem=recv_sems.at[k],
                                                device_id=to, device_id_type=_MESH)

        local = pltpu.make_async_copy(p_ref.at[mine], o_ref.at[mine], local_sem)
        local.start()
        sends = [copy(0, p_ref.at[idx(xn)], o_ref.at[mine], (*xn, cc)),
                 copy(1, p_ref.at[idx(dg)].at[first], relay_ref.at[0], (*xn, cc)),
                 copy(2, p_ref.at[idx(yn)], o_ref.at[mine], (*yn, cc)),
                 copy(3, p_ref.at[idx(dg)].at[second], relay_ref.at[1], (*yn, cc))]
        for cp in sends:
            cp.start()
        copy(1, relay_ref.at[0], relay_ref.at[0], me).wait_recv()
        sends.append(copy(4, relay_ref.at[0], o_ref.at[idx(xn)].at[first], (*yn, cc)))
        sends[-1].start()
        copy(3, relay_ref.at[1], relay_ref.at[1], me).wait_recv()
        sends.append(copy(5, relay_ref.at[1], o_ref.at[idx(yn)].at[second], (*xn, cc)))
        sends[-1].start()
        for k, dst in ((0, o_ref.at[idx(xn)]), (2, o_ref.at[idx(yn)]), (4, o_ref.at[idx(dg)].at[first]),
                       (5, o_ref.at[idx(dg)].at[second])):
            copy(k, dst, dst, me).wait_recv()
        for cp in sends:
            cp.wait_send()
        local.wait()

    sems = pltpu.SemaphoreType.DMA((6,))
    relay = jax.ShapeDtypeStruct((2, half) + p.shape[2:], p.dtype)
    return pl.pallas_call(body, name=name, in_specs=[_HBM], out_specs=[_HBM, _HBM],
                          out_shape=[jax.ShapeDtypeStruct(p.shape, p.dtype), relay],
                          scratch_shapes=[sems, sems, pltpu.SemaphoreType.DMA])(p)[0]


def _all_sum(x, *, name):
    return _sum_slabs(_gather8(x, name=f"gather_{name}"), name=f"sum_{name}", tr=min(128, x.shape[0]))


def _pack_slabs(parts, rows, axis=0):
    lead = parts[0].shape[:axis]
    slabs = [p.reshape(lead + (-1, D)) for p in parts]
    used = sum(sl.shape[axis] for sl in slabs)
    return jnp.concatenate(slabs + [jnp.zeros(lead + (rows - used, D), slabs[0].dtype)], axis=axis)


def _unpack_slabs(slab, shapes):
    lead, out, off = slab.shape[:-2], [], 0
    for shp in shapes:
        n = int(np.prod(shp)) // D
        out.append(slab[..., off:off + n, :].reshape(lead + tuple(shp)))
        off += n
    return out


def _pack_rows(parts, rows):
    flat = jnp.concatenate([p.reshape(-1) for p in parts])
    return jnp.pad(flat, (0, rows * D - flat.shape[0])).reshape(rows, D)


def _unpack_rows(slab, shapes):
    flat, out, off = slab.reshape(-1), [], 0
    for shp in shapes:
        n = int(np.prod(shp))
        out.append(flat[off:off + n].reshape(shp))
        off += n
    return out


def _full_shape(shard, axis):
    return tuple(d * N_DEV if i == axis else d for i, d in enumerate(shard))


def _row(v):
    return v.reshape(1, -1).astype(f32)


def _pad_gain(g):
    return jnp.pad(g.astype(f32), (0, HEAD_PAD - QK)).reshape(1, HEAD_PAD)


def _ffn_fwd(x, p, tag):
    h = _rms_fwd(x, p["norm"], name=f"ffn_norm_{tag}")
    zg = _mm(h, p["wgT"], tb=True, tn=FFN_H, name=f"ffn_up_g_{tag}")
    zv = _mm(h, p["wvT"], tb=True, tn=FFN_H, name=f"ffn_up_v_{tag}")
    a = _ffn_act_fwd(zg, zv, p["cwg"], p["cwv"], name=f"ffn_act_{tag}")
    y = _mm(a, p["wd"], add=x, name=f"ffn_down_{tag}")
    return y, (x, h, zg, zv, a)


def _ffn_bwd(dy, dyb, p, saved, tag):
    x, h, zg, zv, a = saved
    g = {}
    da = _mm(dyb, p["wd"], tb=True, out_dtype=bf16, name=f"ffn_down_dx_{tag}")
    g["wd"] = _mm(a, dyb, ta=True, out_dtype=bf16, name=f"ffn_down_dw_{tag}")
    dzg, dzv, g["cwg"], g["cwv"] = _ffn_act_bwd(da, zg, zv, p["cwg"], p["cwv"], name=f"ffn_act_bwd_{tag}")
    g["wgT"] = _mm(dzg, h, ta=True, out_dtype=bf16, name=f"ffn_up_g_dw_{tag}")
    g["wvT"] = _mm(dzv, h, ta=True, out_dtype=bf16, name=f"ffn_up_v_dw_{tag}")
    dh = _mm(dzg, p["wgT"], name=f"ffn_up_g_dx_{tag}")
    dh = _mm(dzv, p["wvT"], add=dh, name=f"ffn_up_v_dx_{tag}")
    dx, dxb, g["norm"] = _rms_bwd(dh, x, p["norm"], res=dy, name=f"ffn_norm_bwd_{tag}")
    return dx, dxb, g


def _mla_fwd(x, p, tabs, tag):
    cos_t, sin_t = tabs
    h = _rms_fwd(x, p["norm"], name=f"attn_norm_{tag}")
    proj = _mm(h, p["w_inT"], tb=True, name=f"mix_in_{tag}")
    cqn = _rms_fwd(proj, p["cq_norm"], col=0, name=f"cq_norm_{tag}")
    ckvn = _rms_fwd(proj, p["ckv_norm"], col=1, name=f"ckv_norm_{tag}")
    q_raw = _mm(cqn, p["w_uqT"], tb=True, name=f"uq_{tag}")
    kv_raw = _mm(ckvn, p["w_ukvT"], tb=True, name=f"ukv_{tag}")
    q, k, v = _qk_prep_fwd(q_raw, kv_raw, proj, p["q_gain"], p["k_gain"], cos_t, sin_t, name=f"qk_prep_{tag}")
    o, lse = _flash_fwd(q, k, v, name=f"flash_fwd_{tag}")
    conv = _sconv_fwd(proj, p["sconv_w"], name=f"sconv_{tag}")
    y = _mm(conv, p["w_out"][HEADS * HEAD_PAD:], add=x, name=f"mix_out_conv_{tag}")
    y = _mm(o, p["w_out"][:HEADS * HEAD_PAD], add=y, name=f"mix_out_{tag}")
    return y, (x, h, proj, cqn, ckvn, q_raw, kv_raw, q, k, v, o, lse, conv)


def _mla_bwd(dy, dyb, p, tabs, saved, tag):
    cos_t, sin_t = tabs
    x, h, proj, cqn, ckvn, q_raw, kv_raw, q, k, v, o, lse, conv = saved
    s = x.shape[0]
    g = {}
    dmix = _mm(dyb, p["w_out"], tb=True, name=f"mix_out_dx_{tag}")
    g["w_out"] = jnp.concatenate([_mm(o, dyb, ta=True, out_dtype=bf16, name=f"mix_out_dw_{tag}"),
                                  _mm(conv, dyb, ta=True, out_dtype=bf16, name=f"mix_out_conv_dw_{tag}")], axis=0)
    dgb, dgc, dci, g["sconv_w"] = _sconv_bwd(dmix, proj, p["sconv_w"], name=f"sconv_bwd_{tag}")
    dq, delta = _flash_bwd_dq(q, k, v, o, dmix, lse, name=f"flash_dq_{tag}")
    dk, dv = _flash_bwd_dkv(q, k, v, dmix, lse.reshape(HEADS, 1, s), delta.reshape(HEADS, 1, s), name=f"flash_dkv_{tag}")
    dq_raw, dkv_raw, dkr, g["q_gain"], g["k_gain"] = _qk_prep_bwd(
        dq, dk, dv, q_raw, kv_raw, proj, p["q_gain"], p["k_gain"], cos_t, sin_t, name=f"qk_prep_bwd_{tag}")
    dcqn = _mm(dq_raw, p["w_uqT"], name=f"uq_dx_{tag}")
    g["w_uqT"] = _mm(dq_raw, cqn, ta=True, out_dtype=bf16, name=f"uq_dw_{tag}")
    dckvn = _mm(dkv_raw, p["w_ukvT"], name=f"ukv_dx_{tag}")
    g["w_ukvT"] = _mm(dkv_raw, ckvn, ta=True, out_dtype=bf16, name=f"ukv_dw_{tag}")
    dcq, g["cq_norm"] = _rms_bwd(dcqn, proj, p["cq_norm"], col=0, out_dtype=bf16, name=f"cq_norm_bwd_{tag}")
    dckv, g["ckv_norm"] = _rms_bwd(dckvn, proj, p["ckv_norm"], col=1, out_dtype=bf16, name=f"ckv_norm_bwd_{tag}")
    dproj = jnp.concatenate([dcq, dckv, dgb, dgc, dci, dkr.astype(bf16)], axis=1)
    dh = _mm(dproj, p["w_inT"], name=f"mix_in_dx_{tag}")
    g["w_inT"] = _mm(dproj, h, ta=True, out_dtype=bf16, name=f"mix_in_dw_{tag}")
    dx, dxb, g["norm"] = _rms_bwd(dh, x, p["norm"], res=dy, name=f"attn_norm_bwd_{tag}")
    return dx, dxb, g


def _block_diag(wg):
    nb, ng, r, c = wg.shape
    eye = jnp.eye(ng, dtype=wg.dtype)
    return (wg[:, :, :, None, :] * eye[None, :, None, :, None]).reshape(nb, ng * r, ng * c)


def _s5_mats(bbr, bbi, c_re, c_im):
    nb = GROUPS // 8
    b4 = jnp.stack([bbr.reshape(GROUPS, STATE, GROUP), bbi.reshape(GROUPS, STATE, GROUP)], axis=1)
    wg = jnp.transpose(b4, (0, 3, 1, 2)).reshape(nb, 8, GROUP, 2 * STATE)
    cg = jnp.stack([c_re, -c_im], axis=1)
    cg = jnp.transpose(cg, (0, 1, 3, 2)).reshape(nb, 8, 2 * STATE, GROUP)
    return _state_layout(_block_diag(wg), 2), _state_layout(_block_diag(cg), 1)


def _state_layout(m, axis):
    shp = m.shape
    m = m.reshape(shp[:axis] + (4, 2, 2, STATE) + shp[axis + 1:])
    return jnp.swapaxes(m, axis + 1, axis + 2).reshape(shp)


def _group_blocks(d):
    d = d.reshape(GROUPS // 2, 2, GROUP, 2, 2, STATE)
    return jnp.stack([d[:, 0, :, :, 0, :], d[:, 1, :, :, 1, :]], axis=1).reshape(GROUPS, GROUP, 2, STATE)


def _s5_fwd(x, p, tag):
    h = _rms_fwd(x, p["norm"], name=f"ssm_norm_{tag}")
    u, ub = _mm(h, p["w_in"], twin=True, name=f"ssm_in_{tag}")
    ar, ai, bbr, bbi = _disc_fwd(p["lr"], p["li"], p["ls"], p["br"], p["bi"], name=f"disc_{tag}")
    wb, cb = _s5_mats(bbr, bbi, p["c_re"], p["c_im"])
    a1, a2 = ar.reshape(GROUPS // 2, 1, LANES), ai.reshape(GROUPS // 2, 1, LANES)
    xs = _scan_fwd(ub, wb.astype(bf16), a1, a2, name=f"ssm_scan_{tag}")
    y = _bd_nn(xs, cb.astype(bf16), name=f"ssm_y_{tag}")
    g = _gelu_fwd(y, u, p["d_skip"], name=f"ssm_gelu_{tag}")
    a = _mm(g, p["wgaT"], tb=True, name=f"glu_a_{tag}")
    b = _mm(g, p["wgbT"], tb=True, name=f"glu_b_{tag}")
    out = _glu_fwd(x, a, b, name=f"glu_{tag}")
    return out, (x, h, u, ub, wb, cb, a1, a2, xs, y, g, a, b)


def _s5_bwd(dout, p, saved, tag):
    x, h, u, ub, wb, cb, a1, a2, xs, y, g, a, b = saved
    gr = {}
    da, db = _glu_bwd(dout, a, b, name=f"glu_bwd_{tag}")
    dg = _mm(da, p["wgaT"], name=f"glu_a_dx_{tag}")
    dg = _mm(db, p["wgbT"], add=dg, name=f"glu_b_dx_{tag}")
    gr["wgaT"] = _mm(da, g, ta=True, out_dtype=bf16, name=f"glu_a_dw_{tag}")
    gr["wgbT"] = _mm(db, g, ta=True, out_dtype=bf16, name=f"glu_b_dw_{tag}")
    dy, du1, gr["d_skip"] = _gelu_bwd(dg, y, u, p["d_skip"], name=f"ssm_gelu_bwd_{tag}")
    dct = _group_blocks(_bd_tn_diag(dy, xs, name=f"ssm_y_dw_{tag}"))
    gs, dar, dai = _scan_bwd(dy, jnp.swapaxes(cb, 1, 2).astype(bf16), xs, a1, a2, name=f"ssm_scan_bwd_{tag}")
    du = _bd_nn(gs, jnp.swapaxes(wb, 1, 2).astype(bf16), add=du1, out_dtype=bf16, name=f"ssm_bu_dx_{tag}")
    dwg = _group_blocks(_bd_tn_diag(ub, gs, name=f"ssm_bu_dw_{tag}"))
    dh = _mm(du, p["w_in"], tb=True, name=f"ssm_in_dx_{tag}")
    gr["w_in"] = _mm(h, du, ta=True, out_dtype=bf16, name=f"ssm_in_dw_{tag}")
    dx, dxb, gr["norm"] = _rms_bwd(dh, x, p["norm"], res=dout, name=f"ssm_norm_bwd_{tag}")
    dbb = jnp.transpose(dwg, (2, 0, 3, 1)).reshape(2, GROUPS, STATE * GROUP)
    gr["c_re"] = dct[:, :, 0, :]
    gr["c_im"] = -dct[:, :, 1, :]
    dlr, dli, dls, dbr, dbi = _disc_bwd(p["lr"], p["li"], p["ls"], p["br"], p["bi"], dar.reshape(GROUPS, STATE),
                                        dai.reshape(GROUPS, STATE), dbb[0], dbb[1], name=f"disc_bwd_{tag}")
    gr["lr"], gr["li"], gr["ls"] = dlr, dli, dls.reshape(GROUPS)
    gr["br"], gr["bi"] = dbr.reshape(GROUPS, STATE, GROUP), dbi.reshape(GROUPS, STATE, GROUP)
    return dx, dxb, gr


def _slab_shape(shard, axis):
    return (shard[0], shard[2], shard[1]) if axis == 2 else shard


def _to_slab(w, axis):
    return jnp.swapaxes(w, 1, 2) if axis == 2 else w


def _mix_in_pad(wt):
    z = lambda n: jnp.zeros((n, wt.shape[1]), wt.dtype)
    return jnp.concatenate([wt[:512], wt[544:2080], z(NOPE), wt[512:544], z(HEAD_PAD - QK)], axis=0)


def _mix_in_unpad(g):
    return jnp.concatenate([g[:512], g[2048 + NOPE:2048 + QK], g[512:2048]], axis=0)


def _mix_out_pad(w):
    att = jnp.pad(w[:512].reshape(HEADS, NOPE, D), ((0, 0), (NOPE, 0), (0, 0))).reshape(HEADS * HEAD_PAD, D)
    return jnp.concatenate([att, w[512:]], axis=0)


def _mix_out_unpad(g):
    att = g[:HEADS * HEAD_PAD].reshape(HEADS, HEAD_PAD, D)[:, NOPE:, :].reshape(HEADS * NOPE, D)
    return jnp.concatenate([att, g[HEADS * HEAD_PAD:]], axis=0)


def _layer_params(wl, ws, layer):
    i = layer // 2
    half = N_DEV // 2
    up = wl["ffn_w_up"][layer]
    ffn = dict(norm=_row(ws["ffn_norm"][layer]), wgT=up[:half].reshape(FFN_H, D), wvT=up[half:].reshape(FFN_H, D),
               cwg=ws["ffn_conv_w"][layer][:, :FFN_H], cwv=ws["ffn_conv_w"][layer][:, FFN_H:],
               wd=wl["ffn_w_down"][layer].reshape(FFN_H, D))
    if layer % 2 == 0:
        uq = jnp.pad(wl["w_uq"][i], ((0, 0), (0, HEAD_PAD - QK), (0, 0)))
        mixer = dict(norm=_row(ws["attn_norm"][i]), w_inT=_mix_in_pad(wl["mix_w_in"][i].reshape(-1, D)),
                     cq_norm=_row(ws["cq_norm"][i]), ckv_norm=_row(ws["ckv_norm"][i]),
                     w_uqT=uq.reshape(HEADS * HEAD_PAD, LORA), w_ukvT=wl["w_ukv"][i].reshape(HEADS * HEAD_PAD, LORA),
                     q_gain=_pad_gain(ws["q_gain"][i]), k_gain=_pad_gain(ws["k_gain"][i]), sconv_w=ws["sconv_w"][i],
                     w_out=_mix_out_pad(wl["mix_w_out"][i].reshape(D, D)))
    else:
        glu = wl["w_glu"][i]
        mixer = dict(norm=_row(ws["ssm_norm"][i]), w_in=wl["ssm_w_in"][i].reshape(D, D), lr=ws["lambda_re"][i],
                     li=ws["lambda_im"][i], ls=ws["log_step"][i].reshape(GROUPS, 1),
                     br=ws["b_re"][i].reshape(GROUPS, STATE * GROUP), bi=ws["b_im"][i].reshape(GROUPS, STATE * GROUP),
                     c_re=ws["c_re"][i], c_im=ws["c_im"][i], d_skip=_row(ws["d_skip"][i]),
                     wgaT=glu[:half].reshape(D, D), wgbT=glu[half:].reshape(D, D))
    return mixer, ffn


def _collect_grads(gm, gf):
    ev, od, half = (0, 2), (1, 3), N_DEV // 2
    st = lambda xs: jnp.stack(xs, axis=0)
    per_dev = list
    halves = lambda a, b, rows: jnp.concatenate([a.reshape(half, rows, D), b.reshape(half, rows, D)], axis=0)
    big = {
        "ffn_w_up": per_dev([halves(gf[l]["wgT"], gf[l]["wvT"], FFN_H // half) for l in range(4)]),
        "ffn_w_down": per_dev([gf[l]["wd"].reshape(N_DEV, -1, D) for l in range(4)]),
        "w_glu": per_dev([halves(gm[l]["wgaT"], gm[l]["wgbT"], D // half) for l in od]),
        "mix_w_out": per_dev([_mix_out_unpad(gm[l]["w_out"]).reshape(N_DEV, -1, D) for l in ev]),
        "ssm_w_in": per_dev([gm[l]["w_in"].reshape(N_DEV, -1, D) for l in od]),
        "w_ukv": per_dev([gm[l]["w_ukvT"].reshape(N_DEV, HEAD_PAD, LORA) for l in ev]),
        "w_uq": per_dev([gm[l]["w_uqT"].reshape(N_DEV, HEAD_PAD, LORA)[:, :QK] for l in ev]),
        "mix_w_in": per_dev([_mix_in_unpad(gm[l]["w_inT"]).reshape(N_DEV, -1, D) for l in ev]),
    }
    small = {
        "attn_norm": st([gm[l]["norm"].reshape(D) for l in ev]),
        "cq_norm": st([gm[l]["cq_norm"].reshape(LORA) for l in ev]),
        "ckv_norm": st([gm[l]["ckv_norm"].reshape(LORA) for l in ev]),
        "q_gain": st([gm[l]["q_gain"].reshape(HEAD_PAD)[:QK] for l in ev]),
        "k_gain": st([gm[l]["k_gain"].reshape(HEAD_PAD)[:QK] for l in ev]),
        "sconv_w": st([gm[l]["sconv_w"] for l in ev]),
        "ssm_norm": st([gm[l]["norm"].reshape(D) for l in od]),
        "lambda_re": st([gm[l]["lr"] for l in od]), "lambda_im": st([gm[l]["li"] for l in od]),
        "log_step": st([gm[l]["ls"] for l in od]),
        "b_re": st([gm[l]["br"] for l in od]), "b_im": st([gm[l]["bi"] for l in od]),
        "c_re": st([gm[l]["c_re"] for l in od]), "c_im": st([gm[l]["c_im"] for l in od]),
        "d_skip": st([gm[l]["d_skip"].reshape(D) for l in od]),
        "ffn_norm": st([gf[l]["norm"].reshape(D) for l in range(4)]),
        "ffn_conv_w": st([jnp.concatenate([gf[l]["cwg"], gf[l]["cwv"]], axis=1) for l in range(4)]),
    }
    return big, small


def _local_step(x, target, wl, ws):
    s = x.shape[0]
    tabs = _rope_tables(s)
    saved, params = [], []
    for layer in range(4):
        mixer, ffn = _layer_params(wl, ws, layer)
        params.append((mixer, ffn))
        if layer % 2 == 0:
            x, sm = _mla_fwd(x, mixer, tabs, f"l{layer}")
        else:
            x, sm = _s5_fwd(x, mixer, f"l{layer}")
        x, sf = _ffn_fwd(x, ffn, f"l{layer}")
        saved.append((sm, sf))
    dx, dxb, loss = _loss_head(x, target, name="loss_head")
    gm, gf = [None] * 4, [None] * 4
    for layer in reversed(range(4)):
        mixer, ffn = params[layer]
        sm, sf = saved[layer]
        dx, dxb, gf[layer] = _ffn_bwd(dx, dxb, ffn, sf, f"l{layer}")
        if layer % 2 == 0:
            dx, dxb, gm[layer] = _mla_bwd(dx, dxb, mixer, tabs, sm, f"l{layer}")
        else:
            dx, dxb, gm[layer] = _s5_bwd(dx, mixer, sm, f"l{layer}")
    return loss, dx, _collect_grads(gm, gf)


def kernel(x, attn_norm, mix_w_in, cq_norm, ckv_norm, w_uq, w_ukv, q_gain, k_gain, sconv_w, mix_w_out, ssm_norm, ssm_w_in, lambda_re, lambda_im, log_step, b_re, b_im, c_re, c_im, d_skip, w_glu, ffn_norm, ffn_w_up, ffn_conv_w, ffn_w_down, loss_target, m_attn_norm, m_mix_w_in, m_cq_norm, m_ckv_norm, m_w_uq, m_w_ukv, m_q_gain, m_k_gain, m_sconv_w, m_mix_w_out, m_ssm_norm, m_ssm_w_in, m_lambda_re, m_lambda_im, m_log_step, m_b_re, m_b_im, m_c_re, m_c_im, m_d_skip, m_w_glu, m_ffn_norm, m_ffn_w_up, m_ffn_conv_w, m_ffn_w_down, v_attn_norm, v_mix_w_in, v_cq_norm, v_ckv_norm, v_w_uq, v_w_ukv, v_q_gain, v_k_gain, v_sconv_w, v_mix_w_out, v_ssm_norm, v_ssm_w_in, v_lambda_re, v_lambda_im, v_log_step, v_b_re, v_b_im, v_c_re, v_c_im, v_d_skip, v_w_glu, v_ffn_norm, v_ffn_w_up, v_ffn_conv_w, v_ffn_w_down):
    args = dict(locals())
    wsh = {n: args[n] for n in WEIGHTS}
    msh = {n: args["m_" + n] for n in WEIGHTS}
    vsh = {n: args["v_" + n] for n in WEIGHTS}
    me = 4 * lax.axis_index("x") + 2 * lax.axis_index("y") + lax.axis_index("c")
    big_names = [n for n, _, _ in BIG]
    slab_shapes = [_slab_shape(sh, ax) for _, sh, ax in BIG]
    small_names = [n for n, _ in REPL] + [n for n, _, _ in SMALL]

    mine = _pack_slabs([_to_slab(wsh[n], ax).astype(bf16) for n, _, ax in BIG], BIG_ROWS)
    gathered, wl, off = _gather8(mine, name="gather_weights"), {}, 0
    for n, (layers, rows, inner) in zip(big_names, slab_shapes):
        per = rows * inner // D
        wl[n] = [gathered[:, off + l * per:off + (l + 1) * per, :].reshape(N_DEV, rows, inner) for l in range(layers)]
        off += layers * per
    placed = []
    for n, shard, axis in SMALL:
        start = [0] * len(shard)
        start[axis] = me * shard[axis]
        placed.append(lax.dynamic_update_slice(jnp.zeros(_full_shape(shard, axis), f32), wsh[n], start))
    small_all = _all_sum(_pack_rows(placed, SMALL_FWD_ROWS), name="small_params")
    ws = dict(zip([n for n, _, _ in SMALL], _unpack_rows(small_all, [_full_shape(sh, ax) for _, sh, ax in SMALL])))
    ws.update({n: wsh[n] for n, _ in REPL})

    loss8, grad_x, (big_grads, grads) = _local_step(x[0], loss_target[0], wl, ws)

    pieces = []
    for n in big_names:
        layers = big_grads[n]
        tiled = (layers[0].shape[1] * layers[0].shape[2] // D) % 16 == 0
        pieces += layers if tiled else [jnp.stack(layers, axis=1)]
    contrib = _pack_slabs(pieces, BIG_ROWS, axis=1)
    chip_sum = _pair_sum(contrib, _pair_exchange(contrib, name="grads_pair_exchange"), name="grads_pair_sum")
    g_big = _sum_slabs(_cross_exchange(chip_sum, name="grads_cross_exchange"), name="grads_chip_sum")
    small_vec = _pack_rows([grads[n] for n, _ in REPL] + [grads[n] for n, _, _ in SMALL] + [loss8[0, :1]], SMALL_ROWS)
    small_sum = _all_sum(small_vec, name="small_grads")
    parts = _unpack_rows(small_sum, [sh for _, sh in REPL] + [_full_shape(sh, ax) for _, sh, ax in SMALL] + [(1,)])
    g = {n: val for (n, _), val in zip(REPL, parts)}
    for (n, shard, axis), val in zip(SMALL, parts[len(REPL):]):
        start = [0] * len(shard)
        start[axis] = me * shard[axis]
        g[n] = lax.dynamic_slice(val, start, shard)
    loss = parts[-1].reshape(())
    for (n, _, axis), val in zip(BIG, _unpack_slabs(g_big, slab_shapes)):
        g[n] = _to_slab(val, axis)

    delta, new_m, new_v = {}, {}, {}
    for n, shard, _ in BIG:
        flat = lambda a: a.reshape(-1, shard[-1])
        outs = _adamw(flat(wsh[n]), flat(g[n]), flat(msh[n]), flat(vsh[n]), name=f"adamw_{n}")
        delta[n], new_m[n], new_v[n] = [o.reshape(shard) for o in outs]
    small_state = [_pack_rows([src[n] for n in small_names], SMALL_ROWS) for src in (wsh, g, msh, vsh)]
    for dst, slab in zip((delta, new_m, new_v), _adamw(*small_state, name="adamw_small")):
        dst.update(zip(small_names, _unpack_rows(slab, [wsh[n].shape for n in small_names])))

    return (loss, grad_x[None], *[g[n] for n in WEIGHTS], *[delta[n] for n in WEIGHTS],
            *[new_m[n] for n in WEIGHTS], *[new_v[n] for n in WEIGHTS])
```

```python
import math

import numpy as np
import jax
import jax.numpy as jnp
from jax import lax
from jax.experimental import pallas as pl
from jax.experimental.pallas import tpu as pltpu

f32, bf16 = jnp.float32, jnp.bfloat16

N_DEV = 8
D = 1024
HEADS = 8
NOPE, ROPE, QK = 64, 32, 96
HEAD_PAD = 128
LORA = 256
CONV_CH = 512
MIX_IN_PAD = 2176
FFN_H = 2816
GROUPS, GROUP, STATE = 64, 16, 64
EPS = 1e-6
ROPE_THETA = 10000.0
ADAM_LR, ADAM_B1, ADAM_B2, ADAM_EPS, ADAM_WD, ADAM_STEP = 0.001, 0.9, 0.999, 1e-08, 0.01, 10
LANES = 128
PAIR_LANES = 2 * LANES
VMEM_LIMIT = 56 << 20
MM_VMEM_BUDGET = 40 << 20
NEG = -1e30

BIG = (
    ("ffn_w_up", (4, 1024, 704), 2), ("ffn_w_down", (4, 352, 1024), 1), ("w_glu", (2, 1024, 256), 2),
    ("mix_w_out", (2, 128, 1024), 1), ("ssm_w_in", (2, 128, 1024), 1), ("w_ukv", (2, 256, 128), 2),
    ("w_uq", (2, 256, 96), 2), ("mix_w_in", (2, 1024, 260), 2))
REPL = (("attn_norm", (2, 1024)), ("cq_norm", (2, 256)), ("ckv_norm", (2, 256)), ("q_gain", (2, 96)),
        ("k_gain", (2, 96)), ("lambda_re", (2, 64, 64)), ("lambda_im", (2, 64, 64)), ("log_step", (2, 64)),
        ("b_re", (2, 64, 64, 16)), ("b_im", (2, 64, 64, 16)), ("c_re", (2, 64, 16, 64)), ("c_im", (2, 64, 16, 64)),
        ("ffn_norm", (4, 1024)))
SMALL = (("sconv_w", (2, 3, 64), 2), ("ssm_norm", (2, 128), 1), ("d_skip", (2, 128), 1), ("ffn_conv_w", (4, 3, 704), 2))
WEIGHTS = ['attn_norm', 'mix_w_in', 'cq_norm', 'ckv_norm', 'w_uq', 'w_ukv', 'q_gain', 'k_gain', 'sconv_w', 'mix_w_out',
           'ssm_norm', 'ssm_w_in', 'lambda_re', 'lambda_im', 'log_step', 'b_re', 'b_im', 'c_re', 'c_im', 'd_skip',
           'w_glu', 'ffn_norm', 'ffn_w_up', 'ffn_conv_w', 'ffn_w_down']
BIG_ROWS = 5888
SMALL_FWD_ROWS = 80
SMALL_ROWS = 640


def _cparams(sem=None, **kw):
    return pltpu.CompilerParams(dimension_semantics=sem, vmem_limit_bytes=VMEM_LIMIT, **kw)


def _row_tile(rows, target):
    fits = [t for t in range(16, min(rows, target) + 1, 16) if rows % t == 0]
    return max(fits) if fits else rows


def _tile(n, target):
    best = 0
    for t in range(LANES, min(n, target) + 1, LANES):
        if n % t == 0:
            best = t
    return best if best else n


def _mm(a, b, *, ta=False, tb=False, out_dtype=f32, add=None, twin=False, name, tm=1024, tn=1536):
    m, k = (a.shape[1], a.shape[0]) if ta else a.shape
    n = b.shape[0] if tb else b.shape[1]
    assert (b.shape[1] if tb else b.shape[0]) == k
    tm = _tile(m, tm)
    tn_ = _tile(n, tn)
    tn = n if (tn_ < 256 and n <= 2304) else tn_

    def vmem_bytes(t):
        io = 2 * (tm * t * a.dtype.itemsize + t * tn * b.dtype.itemsize + tm * tn * (jnp.dtype(out_dtype).itemsize + 2 * twin))
        return io + (2 * tm * tn * 4 if add is not None else 0) + (tm * tn * 4 if t < k else 0)

    tk = next((t for t in [k] + [t for t in range(k - LANES, 0, -LANES) if k % t == 0] if vmem_bytes(t) <= MM_VMEM_BUDGET), LANES)
    nk = k // tk
    dn = (((0 if ta else 1,), (1 if tb else 0,)), ((), ()))

    def body(*refs):
        a_ref, b_ref = refs[:2]
        add_ref = refs[2] if add is not None else None
        o_ref = refs[3] if add is not None else refs[2]
        twin_ref = refs[4 if add is not None else 3] if twin else None
        part = lax.dot_general(a_ref[...].astype(bf16), b_ref[...].astype(bf16), dn, preferred_element_type=f32)

        def finish(r):
            if add is not None:
                r = r + add_ref[...].astype(f32)
            o_ref[...] = r.astype(out_dtype)
            if twin:
                twin_ref[...] = r.astype(bf16)

        if nk == 1:
            finish(part)
            return
        acc = refs[-1]
        kk = pl.program_id(2)

        @pl.when(kk == 0)
        def _():
            acc[...] = part

        @pl.when(kk > 0)
        def _():
            acc[...] += part

        @pl.when(kk == nk - 1)
        def _():
            finish(acc[...])

    a_spec = pl.BlockSpec((tk, tm), lambda i, j, kk: (kk, i)) if ta else pl.BlockSpec((tm, tk), lambda i, j, kk: (i, kk))
    b_spec = pl.BlockSpec((tn, tk), lambda i, j, kk: (j, kk)) if tb else pl.BlockSpec((tk, tn), lambda i, j, kk: (kk, j))
    in_specs, args = [a_spec, b_spec], [a, b]
    if add is not None:
        in_specs.append(pl.BlockSpec((tm, tn), lambda i, j, kk: (i, j)))
        args.append(add)
    o_spec, o_shape = pl.BlockSpec((tm, tn), lambda i, j, kk: (i, j)), jax.ShapeDtypeStruct((m, n), out_dtype)
    return pl.pallas_call(
        body, name=name, grid=(m // tm, n // tn, nk), in_specs=in_specs,
        out_specs=[o_spec, o_spec] if twin else o_spec,
        out_shape=[o_shape, jax.ShapeDtypeStruct((m, n), bf16)] if twin else o_shape,
        scratch_shapes=[pltpu.VMEM((tm, tn), f32)] if nk > 1 else [],
        compiler_params=_cparams(("parallel", "parallel", "arbitrary")))(*args)


def _bd_nn(a, w, *, out_dtype=f32, add=None, name, ts=2048):
    s = a.shape[0]
    nb, ka, no = w.shape
    ts = min(ts, s)

    def body(a_ref, w_ref, *rest):
        r = jnp.dot(a_ref[...].astype(bf16), w_ref[0].astype(bf16), preferred_element_type=f32)
        if add is not None:
            r = r + rest[0][...].astype(f32)
        rest[-1][...] = r.astype(out_dtype)

    o_spec = pl.BlockSpec((ts, no), lambda b, i: (i, b))
    return pl.pallas_call(
        body, name=name, grid=(nb, s // ts),
        in_specs=[pl.BlockSpec((ts, ka), lambda b, i: (i, b)), pl.BlockSpec((1, ka, no), lambda b, i: (b, 0, 0))]
        + ([o_spec] if add is not None else []),
        out_specs=o_spec, out_shape=jax.ShapeDtypeStruct((s, nb * no), out_dtype),
        compiler_params=_cparams(("parallel", "parallel")))(a, w, *([add] if add is not None else []))


def _bd_tn_diag(a, g, *, name, ts=2048):
    s = a.shape[0]
    nb = a.shape[1] // LANES
    ts = min(ts, s)
    ni = s // ts

    def body(a_ref, g_ref, o_ref, acc):
        i = pl.program_id(1)
        part = lax.dot_general(a_ref[...].astype(bf16), g_ref[...].astype(bf16), (((0,), (0,)), ((), ())),
                               preferred_element_type=f32)

        @pl.when(i == 0)
        def _():
            acc[...] = part

        @pl.when(i > 0)
        def _():
            acc[...] += part

        @pl.when(i == ni - 1)
        def _():
            for j in range(8):
                o_ref[0, j] = acc[j * GROUP:(j + 1) * GROUP, (j // 2) * PAIR_LANES:(j // 2 + 1) * PAIR_LANES]

    return pl.pallas_call(
        body, name=name, grid=(nb, ni),
        in_specs=[pl.BlockSpec((ts, LANES), lambda b, i: (i, b)), pl.BlockSpec((ts, 8 * LANES), lambda b, i: (i, b))],
        out_specs=pl.BlockSpec((1, 8, GROUP, PAIR_LANES), lambda b, i: (b, 0, 0, 0)),
        out_shape=jax.ShapeDtypeStruct((nb, 8, GROUP, PAIR_LANES), f32),
        scratch_shapes=[pltpu.VMEM((LANES, 8 * LANES), f32)],
        compiler_params=_cparams(("parallel", "arbitrary")))(a, g)


def _rms_fwd(x, g, *, col=0, name, ts=512):
    s, d = x.shape[0], g.shape[1]
    ts = min(ts, s)

    def body(x_ref, g_ref, o_ref):
        xv = x_ref[...].astype(f32)
        r = lax.rsqrt(jnp.mean(xv * xv, axis=-1, keepdims=True) + EPS)
        o_ref[...] = (xv * r * g_ref[...]).astype(bf16)

    return pl.pallas_call(
        body, name=name, grid=(s // ts,),
        in_specs=[pl.BlockSpec((ts, d), lambda i: (i, col)), pl.BlockSpec((1, d), lambda i: (0, 0))],
        out_specs=pl.BlockSpec((ts, d), lambda i: (i, 0)),
        out_shape=jax.ShapeDtypeStruct((s, d), bf16),
        compiler_params=_cparams(("parallel",)))(x, g)


def _rms_bwd(dy, x, g, *, col=0, res=None, out_dtype=f32, name, ts=512):
    s, d = dy.shape
    ts = min(ts, s)
    twin = res is not None

    def body(*refs):
        if twin:
            dy_ref, x_ref, g_ref, res_ref, dx_ref, dxb_ref, dg_ref = refs
        else:
            dy_ref, x_ref, g_ref, dx_ref, dg_ref = refs

        @pl.when(pl.program_id(0) == 0)
        def _():
            dg_ref[...] = jnp.zeros_like(dg_ref)

        xv, dyv = x_ref[...].astype(f32), dy_ref[...].astype(f32)
        r = lax.rsqrt(jnp.mean(xv * xv, axis=-1, keepdims=True) + EPS)
        dyg = dyv * g_ref[...]
        dx = r * dyg - xv * (r * r * r) * jnp.mean(xv * dyg, axis=-1, keepdims=True)
        if twin:
            dx = dx + res_ref[...]
            dxb_ref[...] = dx.astype(bf16)
        dx_ref[...] = dx.astype(out_dtype)
        dg_ref[...] += jnp.sum(dyv * xv * r, axis=0, keepdims=True)

    row, vec = pl.BlockSpec((ts, d), lambda i: (i, 0)), pl.BlockSpec((1, d), lambda i: (0, 0))
    in_specs, args = [row, pl.BlockSpec((ts, d), lambda i: (i, col)), vec], [dy, x, g]
    out_specs, out_shape = [row], [jax.ShapeDtypeStruct((s, d), out_dtype)]
    if twin:
        in_specs.append(row)
        args.append(res)
        out_specs.append(row)
        out_shape.append(jax.ShapeDtypeStruct((s, d), bf16))
    return pl.pallas_call(
        body, name=name, grid=(s // ts,), in_specs=in_specs, out_specs=out_specs + [vec],
        out_shape=out_shape + [jax.ShapeDtypeStruct((1, d), f32)],
        compiler_params=_cparams(("arbitrary",)))(*args)


def _swap_halves(z):
    lane = lax.broadcasted_iota(jnp.int32, z.shape, 1)
    return jnp.where(lane < NOPE + ROPE // 2, pltpu.roll(z, LANES - ROPE // 2, axis=1), pltpu.roll(z, ROPE // 2, axis=1))


def _rope_tables(s):
    inv_freq = 1.0 / (ROPE_THETA ** (jnp.arange(0, ROPE, 2, dtype=f32) / ROPE))
    ang = jnp.arange(s, dtype=f32)[:, None] * inv_freq[None, :]
    cos, sin = jnp.cos(ang), jnp.sin(ang)
    one, zero = jnp.ones((s, NOPE), f32), jnp.zeros((s, NOPE), f32)
    pad1, pad0 = jnp.ones((s, HEAD_PAD - QK), f32), jnp.zeros((s, HEAD_PAD - QK), f32)
    return jnp.concatenate([one, cos, cos, pad1], 1), jnp.concatenate([zero, -sin, sin, pad0], 1)


def _qk_prep_fwd(q_raw, kv_raw, proj, qg, kg, cos_t, sin_t, *, name, ts=1024):
    s = q_raw.shape[0]
    ts = min(ts, s)
    rope_blk = (MIX_IN_PAD - HEAD_PAD) // HEAD_PAD

    def body(q_ref, kv_ref, kr_ref, qg_ref, kg_ref, c_ref, s_ref, qo_ref, ko_ref, vo_ref):
        lane = lax.broadcasted_iota(jnp.int32, (ts, HEAD_PAD), 1)
        cosv, sinv = c_ref[...], s_ref[...]

        def norm_rope(z, gain):
            r = lax.rsqrt(jnp.sum(z * z, axis=-1, keepdims=True) * (1.0 / QK) + EPS)
            zn = z * r * gain
            return zn * cosv + _swap_halves(zn) * sinv

        kvv = kv_ref[...]
        qo_ref[...] = (norm_rope(q_ref[...], qg_ref[...]) * _Q_FOLD).astype(bf16)
        ko_ref[...] = norm_rope(jnp.where(lane < NOPE, kvv, kr_ref[...]), kg_ref[...]).astype(bf16)
        vo_ref[...] = jnp.where(lane >= NOPE, kvv, 0.0).astype(bf16)

    head = pl.BlockSpec((ts, HEAD_PAD), lambda i, h: (i, h))
    row = pl.BlockSpec((ts, HEAD_PAD), lambda i, h: (i, 0))
    vec = pl.BlockSpec((1, HEAD_PAD), lambda i, h: (0, 0))
    out = jax.ShapeDtypeStruct((s, HEADS * HEAD_PAD), bf16)
    return pl.pallas_call(
        body, name=name, grid=(s // ts, HEADS),
        in_specs=[head, head, pl.BlockSpec((ts, HEAD_PAD), lambda i, h: (i, rope_blk)), vec, vec, row, row],
        out_specs=[head, head, head], out_shape=[out, out, out],
        compiler_params=_cparams(("parallel", "parallel")))(q_raw, kv_raw, proj, qg, kg, cos_t, sin_t)


def _qk_prep_bwd(dq, dk, dv, q_raw, kv_raw, proj, qg, kg, cos_t, sin_t, *, name, ts=1024):
    s = q_raw.shape[0]
    ts = min(ts, s)
    rope_blk = (MIX_IN_PAD - HEAD_PAD) // HEAD_PAD

    def body(dq_ref, dk_ref, dv_ref, q_ref, kv_ref, kr_ref, qg_ref, kg_ref, c_ref, s_ref,
             dqr_ref, dkvr_ref, dkr_ref, dqg_ref, dkg_ref):
        i, h = pl.program_id(0), pl.program_id(1)
        lane = lax.broadcasted_iota(jnp.int32, (ts, HEAD_PAD), 1)
        is_rope = (lane >= NOPE) & (lane < QK)
        cosv, sinv = c_ref[...], s_ref[...]

        @pl.when((i == 0) & (h == 0))
        def _():
            dqg_ref[...] = jnp.zeros_like(dqg_ref)
            dkg_ref[...] = jnp.zeros_like(dkg_ref)

        @pl.when(h == 0)
        def _():
            dkr_ref[...] = jnp.zeros_like(dkr_ref)

        def back(dout, z, gain):
            dzn = dout * cosv + jnp.where(is_rope, _swap_halves(dout * sinv), 0.0)
            r = lax.rsqrt(jnp.sum(z * z, axis=-1, keepdims=True) * (1.0 / QK) + EPS)
            dzg = dzn * gain
            dz = r * dzg - z * (r * r * r) * (jnp.sum(z * dzg, axis=-1, keepdims=True) * (1.0 / QK))
            return dz, jnp.sum(dzn * z * r, axis=0, keepdims=True)

        dqz, dqg = back(dq_ref[...].astype(f32), q_ref[...], qg_ref[...])
        dqr_ref[...] = dqz.astype(bf16)
        dqg_ref[...] += dqg
        kvv = kv_ref[...]
        dkz, dkg = back(dk_ref[...].astype(f32), jnp.where(lane < NOPE, kvv, kr_ref[...]), kg_ref[...])
        dkg_ref[...] += dkg
        dkvr_ref[...] = jnp.where(lane < NOPE, dkz, dv_ref[...].astype(f32)).astype(bf16)
        dkr_ref[...] += jnp.where(is_rope, dkz, 0.0)

    head = pl.BlockSpec((ts, HEAD_PAD), lambda i, h: (i, h))
    row = pl.BlockSpec((ts, HEAD_PAD), lambda i, h: (i, 0))
    vec = pl.BlockSpec((1, HEAD_PAD), lambda i, h: (0, 0))
    wide = jax.ShapeDtypeStruct((s, HEADS * HEAD_PAD), bf16)
    return pl.pallas_call(
        body, name=name, grid=(s // ts, HEADS),
        in_specs=[head, head, head, head, head, pl.BlockSpec((ts, HEAD_PAD), lambda i, h: (i, rope_blk)), vec, vec, row, row],
        out_specs=[head, head, row, vec, vec],
        out_shape=[wide, wide, jax.ShapeDtypeStruct((s, HEAD_PAD), f32), jax.ShapeDtypeStruct((1, HEAD_PAD), f32),
                   jax.ShapeDtypeStruct((1, HEAD_PAD), f32)],
        compiler_params=_cparams(("arbitrary", "arbitrary")))(dq, dk, dv, q_raw, kv_raw, proj, qg, kg, cos_t, sin_t)


_NT = (((1,), (1,)), ((), ()))
_SCALE = QK ** -0.5
_LOG2E = math.log2(math.e)
_Q_FOLD = _SCALE * _LOG2E
FLASH_TILE = 1024


def _flash_fwd(q, k, v, *, name, tq=FLASH_TILE):
    s = q.shape[0]
    tq = min(tq, s)

    def body(q_ref, k_ref, v_ref, o_ref, lse_ref):
        i = pl.program_id(1)
        qv = q_ref[...]

        def step(j, carry, masked):
            m, l, acc = carry
            st = pl.multiple_of(j * tq, tq)
            kj, vj = k_ref[pl.ds(st, tq), :], v_ref[pl.ds(st, tq), :]
            sc = lax.dot_general(qv, kj, _NT, preferred_element_type=f32)
            if masked:
                rr = lax.broadcasted_iota(jnp.int32, (tq, tq), 0)
                cc = lax.broadcasted_iota(jnp.int32, (tq, tq), 1)
                sc = jnp.where(cc <= rr, sc, NEG)
            m_new = jnp.maximum(m, jnp.max(sc, axis=-1, keepdims=True))
            p = jnp.exp2(sc - m_new)
            alpha = jnp.exp2(m - m_new)
            l = alpha * l + jnp.sum(p, axis=-1, keepdims=True)
            acc = alpha * acc + jnp.dot(p.astype(bf16), vj, preferred_element_type=f32)
            return m_new, l, acc

        init = (jnp.full((tq, 1), NEG, f32), jnp.zeros((tq, 1), f32), jnp.zeros((tq, HEAD_PAD), f32))
        carry = lax.fori_loop(0, i, lambda j, c: step(j, c, False), init)
        m, l, acc = step(i, carry, True)
        o_ref[...] = (acc / l).astype(bf16)
        lse_ref[0] = m + jnp.log2(l)

    blk = pl.BlockSpec((tq, HEAD_PAD), lambda h, i: (i, h))
    full = pl.BlockSpec((s, HEAD_PAD), lambda h, i: (0, h))
    return pl.pallas_call(
        body, name=name, grid=(HEADS, s // tq), in_specs=[blk, full, full],
        out_specs=[blk, pl.BlockSpec((1, tq, 1), lambda h, i: (h, i, 0))],
        out_shape=[jax.ShapeDtypeStruct((s, HEADS * HEAD_PAD), bf16), jax.ShapeDtypeStruct((HEADS, s, 1), f32)],
        compiler_params=_cparams(("parallel", "arbitrary")))(q, k, v)


def _flash_bwd_dq(q, k, v, o, do, lse, *, name, tq=FLASH_TILE):
    s = q.shape[0]
    tq = min(tq, s)

    def body(q_ref, k_ref, v_ref, o_ref, do_ref, lse_ref, dq_ref, dl_ref):
        i = pl.program_id(1)
        qv = q_ref[...]
        dov = do_ref[...].astype(f32)
        delta = jnp.sum(dov * o_ref[...].astype(f32), axis=-1, keepdims=True)
        dob = dov.astype(bf16)
        lsev = lse_ref[0]

        def step(j, acc, masked):
            st = pl.multiple_of(j * tq, tq)
            kj, vj = k_ref[pl.ds(st, tq), :], v_ref[pl.ds(st, tq), :]
            sc = lax.dot_general(qv, kj, _NT, preferred_element_type=f32)
            p = jnp.exp2(sc - lsev)
            if masked:
                rr = lax.broadcasted_iota(jnp.int32, (tq, tq), 0)
                cc = lax.broadcasted_iota(jnp.int32, (tq, tq), 1)
                p = jnp.where(cc <= rr, p, 0.0)
            dp = lax.dot_general(dob, vj, _NT, preferred_element_type=f32)
            ds = p * (dp - delta)
            return acc + jnp.dot(ds.astype(bf16), kj, preferred_element_type=f32)

        acc = lax.fori_loop(0, i, lambda j, c: step(j, c, False), jnp.zeros((tq, HEAD_PAD), f32))
        dq_ref[...] = step(i, acc, True) * _SCALE
        dl_ref[0] = delta

    blk = pl.BlockSpec((tq, HEAD_PAD), lambda h, i: (i, h))
    full = pl.BlockSpec((s, HEAD_PAD), lambda h, i: (0, h))
    col = pl.BlockSpec((1, tq, 1), lambda h, i: (h, i, 0))
    return pl.pallas_call(
        body, name=name, grid=(HEADS, s // tq), in_specs=[blk, full, full, blk, blk, col],
        out_specs=[blk, col],
        out_shape=[jax.ShapeDtypeStruct((s, HEADS * HEAD_PAD), f32), jax.ShapeDtypeStruct((HEADS, s, 1), f32)],
        compiler_params=_cparams(("parallel", "arbitrary")))(q, k, v, o, do, lse)


def _flash_bwd_dkv(q, k, v, do, lse_row, delta_row, *, name, tk=FLASH_TILE):
    s = q.shape[0]
    tk = min(tk, s)
    nblk = s // tk

    def body(q_ref, k_ref, v_ref, do_ref, lse_ref, dl_ref, dk_ref, dv_ref):
        j = pl.program_id(1)
        kv_, vv = k_ref[...], v_ref[...]

        def step(i, carry, masked):
            dk, dv = carry
            st = pl.multiple_of(i * tk, tk)
            qi = q_ref[pl.ds(st, tk), :]
            doi = do_ref[pl.ds(st, tk), :].astype(bf16)
            lse_i = lse_ref[0, :, pl.ds(st, tk)]
            dl_i = dl_ref[0, :, pl.ds(st, tk)]
            st_ = lax.dot_general(kv_, qi, _NT, preferred_element_type=f32)
            pt = jnp.exp2(st_ - lse_i)
            if masked:
                kk = lax.broadcasted_iota(jnp.int32, (tk, tk), 0)
                qq = lax.broadcasted_iota(jnp.int32, (tk, tk), 1)
                pt = jnp.where(kk <= qq, pt, 0.0)
            dv = dv + jnp.dot(pt.astype(bf16), doi, preferred_element_type=f32)
            dpt = lax.dot_general(vv, doi, _NT, preferred_element_type=f32)
            dst = pt * (dpt - dl_i)
            dk = dk + jnp.dot(dst.astype(bf16), qi, preferred_element_type=f32)
            return dk, dv

        zero = jnp.zeros((tk, HEAD_PAD), f32)
        carry = step(j, (zero, zero), True)
        dk, dv = lax.fori_loop(j + 1, nblk, lambda i, c: step(i, c, False), carry)
        dk_ref[...] = dk * (1.0 / _LOG2E)
        dv_ref[...] = dv

    blk = pl.BlockSpec((tk, HEAD_PAD), lambda h, j: (j, h))
    full = pl.BlockSpec((s, HEAD_PAD), lambda h, j: (0, h))
    rowv = pl.BlockSpec((1, 1, s), lambda h, j: (h, 0, 0))
    out = jax.ShapeDtypeStruct((s, HEADS * HEAD_PAD), f32)
    return pl.pallas_call(
        body, name=name, grid=(HEADS, nblk), in_specs=[full, blk, blk, full, rowv, rowv],
        out_specs=[blk, blk], out_shape=[out, out],
        compiler_params=_cparams(("parallel", "arbitrary")))(q, k, v, do, lse_row, delta_row)


SUBLANES = 8


def _shift_down(x, d):
    r = pltpu.roll(x, d, axis=0)
    t = lax.broadcasted_iota(jnp.int32, (SUBLANES, x.shape[1]), 0)
    head = jnp.where(t < d, 0.0, r[:SUBLANES])
    return head if x.shape[0] == SUBLANES else jnp.concatenate([head, r[SUBLANES:]], axis=0)


def _shift_up(x, d):
    s = x.shape[0]
    r = pltpu.roll(x, s - d, axis=0)
    t = lax.broadcasted_iota(jnp.int32, (SUBLANES, x.shape[1]), 0)
    tail = jnp.where(t >= SUBLANES - d, 0.0, r[s - SUBLANES:])
    return tail if s == SUBLANES else jnp.concatenate([r[:s - SUBLANES], tail], axis=0)


def _taps(w_ref):
    return w_ref[0:1, :], w_ref[1:2, :], w_ref[2:3, :]


def _conv3(u, w):
    u1, u2 = _shift_down(u, 1), _shift_down(u, 2)
    return w[0] * u2 + w[1] * u1 + w[2] * u, (u1, u2)


def _ref_shift_down(ref, d):
    s = ref.shape[0]
    return jnp.concatenate([_shift_down(ref[0:SUBLANES, :], d), ref[pl.ds(SUBLANES - d, s - SUBLANES), :]], axis=0)


def _conv3_ref(ref, w):
    return w[0] * _ref_shift_down(ref, 2) + w[1] * _ref_shift_down(ref, 1) + w[2] * ref[...]


def _conv3_t(g, w):
    return w[2] * g + w[1] * _shift_up(g, 1) + w[0] * _shift_up(g, 2)


def _conv3_dw(dw_ref, g, u, shifted):
    dw_ref[0:1, :] = jnp.sum(g * shifted[1], axis=0, keepdims=True)
    dw_ref[1:2, :] = jnp.sum(g * shifted[0], axis=0, keepdims=True)
    dw_ref[2:3, :] = jnp.sum(g * u, axis=0, keepdims=True)


_GB, _GC, _CI = 512 // LANES, 1024 // LANES, 1536 // LANES


def _sconv_fwd(proj, w, *, name):
    s = proj.shape[0]

    def body(gb_ref, gc_ref, ci_ref, w_ref, o_ref):
        o_ref[...] = (gb_ref[...] * _conv3(gc_ref[...] * ci_ref[...], _taps(w_ref))[0]).astype(bf16)

    col = lambda off: pl.BlockSpec((s, LANES), lambda j: (0, off + j))
    return pl.pallas_call(
        body, name=name, grid=(CONV_CH // LANES,),
        in_specs=[col(_GB), col(_GC), col(_CI), pl.BlockSpec((3, LANES), lambda j: (0, j))],
        out_specs=pl.BlockSpec((s, LANES), lambda j: (0, j)),
        out_shape=jax.ShapeDtypeStruct((s, CONV_CH), bf16),
        compiler_params=_cparams(("parallel",)))(proj, proj, proj, w)


def _sconv_bwd(dmix, proj, w, *, name):
    s = proj.shape[0]

    def body(do_ref, gb_ref, gc_ref, ci_ref, w_ref, dgb_ref, dgc_ref, dci_ref, dw_ref):
        wv, gc, ci, do = _taps(w_ref), gc_ref[...], ci_ref[...], do_ref[...].astype(f32)
        u = gc * ci
        conv, shifted = _conv3(u, wv)
        dgb_ref[...] = (do * conv).astype(bf16)
        dc = do * gb_ref[...]
        du = _conv3_t(dc, wv)
        dgc_ref[...] = (du * ci).astype(bf16)
        dci_ref[...] = (du * gc).astype(bf16)
        _conv3_dw(dw_ref, dc, u, shifted)

    col = lambda off: pl.BlockSpec((s, LANES), lambda j: (0, off + j))
    out = jax.ShapeDtypeStruct((s, CONV_CH), bf16)
    return pl.pallas_call(
        body, name=name, grid=(CONV_CH // LANES,),
        in_specs=[col(HEADS), col(_GB), col(_GC), col(_CI), pl.BlockSpec((3, LANES), lambda j: (0, j))],
        out_specs=[col(0), col(0), col(0), pl.BlockSpec((3, LANES), lambda j: (0, j))],
        out_shape=[out, out, out, jax.ShapeDtypeStruct((3, CONV_CH), f32)],
        compiler_params=_cparams(("parallel",)))(dmix, proj, proj, proj, w)


def _ffn_act_fwd(zg, zv, cwg, cwv, *, name):
    s, f = zg.shape

    def body(zg_ref, zv_ref, wg_ref, wv_ref, o_ref):
        o_ref[...] = (jax.nn.silu(_conv3_ref(zg_ref, _taps(wg_ref))) * _conv3_ref(zv_ref, _taps(wv_ref))).astype(bf16)

    col = pl.BlockSpec((s, LANES), lambda j: (0, j))
    wsp = pl.BlockSpec((3, LANES), lambda j: (0, j))
    return pl.pallas_call(
        body, name=name, grid=(f // LANES,), in_specs=[col, col, wsp, wsp], out_specs=col,
        out_shape=jax.ShapeDtypeStruct((s, f), bf16), compiler_params=_cparams(("parallel",)))(zg, zv, cwg, cwv)


def _ffn_act_bwd(da, zg, zv, cwg, cwv, *, name):
    s, f = zg.shape

    def body(da_ref, zg_ref, zv_ref, wg_ref, wv_ref, dzg_ref, dzv_ref, dwg_ref, dwv_ref):
        wg, wv, dav = _taps(wg_ref), _taps(wv_ref), da_ref[...].astype(f32)
        ug, uv = _conv3_ref(zg_ref, wg), _conv3_ref(zv_ref, wv)
        sg = jax.nn.sigmoid(ug)
        dug = dav * uv * (sg * (1.0 + ug * (1.0 - sg)))
        duv = dav * (ug * sg)
        dzg_ref[...] = _conv3_t(dug, wg).astype(bf16)
        dzv_ref[...] = _conv3_t(duv, wv).astype(bf16)
        _conv3_dw(dwg_ref, dug, zg_ref[...], (_ref_shift_down(zg_ref, 1), _ref_shift_down(zg_ref, 2)))
        _conv3_dw(dwv_ref, duv, zv_ref[...], (_ref_shift_down(zv_ref, 1), _ref_shift_down(zv_ref, 2)))

    col = pl.BlockSpec((s, LANES), lambda j: (0, j))
    wsp = pl.BlockSpec((3, LANES), lambda j: (0, j))
    act, wsh = jax.ShapeDtypeStruct((s, f), bf16), jax.ShapeDtypeStruct((3, f), f32)
    return pl.pallas_call(
        body, name=name, grid=(f // LANES,), in_specs=[col, col, col, wsp, wsp], out_specs=[col, col, wsp, wsp],
        out_shape=[act, act, wsh, wsh], compiler_params=_cparams(("parallel",)))(da, zg, zv, cwg, cwv)


def _expand_mat():
    return jnp.asarray(np.kron(np.eye(STATE, dtype=np.float32), np.ones((1, GROUP), np.float32)))


def _disc_fn(lr, li, ls, br, bi, e):
    dt = jnp.exp(ls)
    mag = jnp.exp(lr * dt)
    ar, ai = mag * jnp.cos(li * dt), mag * jnp.sin(li * dt)
    nr, ni = ar - 1.0, ai
    den = lr * lr + li * li
    zr, zi = (nr * lr + ni * li) / den, (ni * lr - nr * li) / den
    zrr = jnp.dot(zr, e, precision=lax.Precision.HIGHEST, preferred_element_type=f32)
    zir = jnp.dot(zi, e, precision=lax.Precision.HIGHEST, preferred_element_type=f32)
    return ar, ai, zrr * br - zir * bi, zrr * bi + zir * br


def _disc_fwd(lr, li, ls, br, bi, *, name):
    def body(lr_ref, li_ref, ls_ref, br_ref, bi_ref, e_ref, ar_ref, ai_ref, bbr_ref, bbi_ref):
        ar, ai, bbr, bbi = _disc_fn(lr_ref[...], li_ref[...], ls_ref[...], br_ref[...], bi_ref[...], e_ref[...])
        ar_ref[...], ai_ref[...], bbr_ref[...], bbi_ref[...] = ar, ai, bbr, bbi

    sq, wide = jax.ShapeDtypeStruct((GROUPS, STATE), f32), jax.ShapeDtypeStruct((GROUPS, STATE * GROUP), f32)
    return pl.pallas_call(body, name=name, out_shape=[sq, sq, wide, wide],
                          compiler_params=_cparams())(lr, li, ls, br, bi, _expand_mat())


def _disc_bwd(lr, li, ls, br, bi, dar, dai, dbbr, dbbi, *, name):
    def body(lr_ref, li_ref, ls_ref, br_ref, bi_ref, e_ref, dar_ref, dai_ref, dbbr_ref, dbbi_ref,
             dlr_ref, dli_ref, dls_ref, dbr_ref, dbi_ref):
        ev = e_ref[...]
        _, vjp = jax.vjp(lambda a, b, c, d_, e_: _disc_fn(a, b, c, d_, e_, ev),
                         lr_ref[...], li_ref[...], ls_ref[...], br_ref[...], bi_ref[...])
        dlr, dli, dls, dbr, dbi = vjp((dar_ref[...], dai_ref[...], dbbr_ref[...], dbbi_ref[...]))
        dlr_ref[...], dli_ref[...], dls_ref[...], dbr_ref[...], dbi_ref[...] = dlr, dli, dls, dbr, dbi

    sq, wide = jax.ShapeDtypeStruct((GROUPS, STATE), f32), jax.ShapeDtypeStruct((GROUPS, STATE * GROUP), f32)
    return pl.pallas_call(body, name=name, out_shape=[sq, sq, jax.ShapeDtypeStruct((GROUPS, 1), f32), wide, wide],
                          compiler_params=_cparams())(lr, li, ls, br, bi, _expand_mat(), dar, dai, dbbr, dbbi)


SCAN_TILE = 32
SCAN_PAIRS = 4


def _tile_shift(v, d, reverse):
    if d % 8:
        return _shift_up(v, d) if reverse else _shift_down(v, d)
    z = jnp.zeros((d, v.shape[1]), v.dtype)
    return jnp.concatenate([v[d:], z], axis=0) if reverse else jnp.concatenate([z, v[:v.shape[0] - d]], axis=0)


def _tile_scan(r, i, pows, reverse):
    d = 1
    for br, bi in pows:
        rs, is_ = _tile_shift(r, d, reverse), _tile_shift(i, d, reverse)
        r, i = r + br * rs - bi * is_, i + br * is_ + bi * rs
        d *= 2
    return r, i


def _scan_setup(ar, ai, reverse):
    if reverse:
        ai = -ai
    pows, br, bi, d = [], ar, ai, 1
    while d < SCAN_TILE:
        pows.append((br, bi))
        br, bi, d = br * br - bi * bi, 2.0 * br * bi, 2 * d
    row = lax.broadcasted_iota(jnp.int32, (SCAN_TILE, LANES), 0)
    hit = row == (SCAN_TILE - 1 if reverse else 0)
    pr, pi = _tile_scan(jnp.where(hit, ar, 0.0), jnp.where(hit, ai, 0.0), pows, reverse)
    return pows, pr, pi


def _carry_in(r, i, pr, pi, cr, ci):
    crb, cib = jnp.broadcast_to(cr, r.shape), jnp.broadcast_to(ci, i.shape)
    return r + pr * crb - pi * cib, i + pr * cib + pi * crb


def _pair_cols(q):
    return slice(q * PAIR_LANES, q * PAIR_LANES + LANES), slice(q * PAIR_LANES + LANES, (q + 1) * PAIR_LANES)


_SCAN_W = SCAN_PAIRS * PAIR_LANES


def _scan_specs(s, w):
    per = w.shape[2] // _SCAN_W
    src = pl.BlockSpec((s, LANES), lambda g: (0, g // per))
    mat = pl.BlockSpec((1, LANES, _SCAN_W), lambda g: (g // per, 0, g % per))
    col = pl.BlockSpec((s, _SCAN_W), lambda g: (0, g))
    vec = pl.BlockSpec((SCAN_PAIRS, 1, LANES), lambda g: (g, 0, 0))
    return src, mat, col, vec, (w.shape[0] * per,)


def _scan_fwd(u, wb, ar, ai, *, name):
    s = u.shape[0]
    nt = s // SCAN_TILE

    def body(u_ref, w_ref, ar_ref, ai_ref, x_ref):
        setups = [_scan_setup(ar_ref[q], ai_ref[q], False) for q in range(SCAN_PAIRS)]
        wv = w_ref[0]

        def tile_rows(k):
            return pl.ds(pl.multiple_of(k * SCAN_TILE, SCAN_TILE), SCAN_TILE)

        def tile_in(k):
            return jnp.dot(u_ref[tile_rows(k), :].astype(bf16), wv, preferred_element_type=f32)

        def step(k, carry):
            rows, bu = tile_rows(k), carry[-1]
            ahead = tile_in(jnp.minimum(k + 1, nt - 1))
            out = []
            for q, (pows, pr, pi) in enumerate(setups):
                rc, ic = _pair_cols(q)
                r, i = _tile_scan(bu[:, rc], bu[:, ic], pows, False)
                r, i = _carry_in(r, i, pr, pi, carry[2 * q], carry[2 * q + 1])
                x_ref[rows, rc] = r.astype(bf16)
                x_ref[rows, ic] = i.astype(bf16)
                out += [r[SCAN_TILE - 1:SCAN_TILE, :], i[SCAN_TILE - 1:SCAN_TILE, :]]
            return tuple(out) + (ahead,)

        lax.fori_loop(0, nt, step, tuple(jnp.zeros((1, LANES), f32) for _ in range(2 * SCAN_PAIRS)) + (tile_in(0),))

    src, mat, col, vec, grid = _scan_specs(s, wb)
    return pl.pallas_call(body, name=name, grid=grid, in_specs=[src, mat, vec, vec], out_specs=col,
                          out_shape=jax.ShapeDtypeStruct((s, wb.shape[0] * wb.shape[2]), bf16),
                          compiler_params=_cparams(("parallel",)))(u, wb, ar, ai)


def _scan_bwd(dy, cbt, x, ar, ai, *, name):
    s = dy.shape[0]
    nt = s // SCAN_TILE

    def fold(v):
        out = v[0:8]
        for r in range(8, SCAN_TILE, 8):
            out = out + v[r:r + 8]
        return out

    def body(dy_ref, w_ref, x_ref, ar_ref, ai_ref, g_ref, dar_ref, dai_ref):
        setups = [_scan_setup(ar_ref[q], ai_ref[q], True) for q in range(SCAN_PAIRS)]
        row = lax.broadcasted_iota(jnp.int32, (SCAN_TILE, LANES), 0)
        wv = w_ref[0]

        def tile_in(k):
            return jnp.dot(dy_ref[pl.ds(pl.multiple_of(k * SCAN_TILE, SCAN_TILE), SCAN_TILE), :], wv, preferred_element_type=f32)

        def step(kk, carry):
            k = nt - 1 - kk
            start = pl.multiple_of(k * SCAN_TILE, SCAN_TILE)
            rows = pl.ds(start, SCAN_TILE)
            prev16 = pl.ds(pl.multiple_of(jnp.maximum(start - 16, 0), 16), 16)
            dx = carry[-1]
            ahead = tile_in(jnp.maximum(k - 1, 0))

            def before(cols):
                first = jnp.where(k > 0, x_ref[prev16, cols][15:16, :].astype(f32), 0.0)
                return jnp.where(row == 0, first, pltpu.roll(x_ref[rows, cols].astype(f32), 1, axis=0))

            out = []
            for q, (pows, pr, pi) in enumerate(setups):
                rc, ic = _pair_cols(q)
                cr, ci, acc_r, acc_i = carry[4 * q:4 * q + 4]
                gr, gi = _tile_scan(dx[:, rc], dx[:, ic], pows, True)
                gr, gi = _carry_in(gr, gi, pr, pi, cr, ci)
                g_ref[rows, rc] = gr.astype(bf16)
                g_ref[rows, ic] = gi.astype(bf16)
                xr, xi = before(rc), before(ic)
                out += [gr[0:1, :], gi[0:1, :], acc_r + fold(gr * xr + gi * xi), acc_i + fold(gi * xr - gr * xi)]
            return tuple(out) + (ahead,)

        init = (jnp.zeros((1, LANES), f32), jnp.zeros((1, LANES), f32), jnp.zeros((8, LANES), f32), jnp.zeros((8, LANES), f32))
        res = lax.fori_loop(0, nt, step, init * SCAN_PAIRS + (tile_in(nt - 1),))
        for q in range(SCAN_PAIRS):
            dar_ref[q] = jnp.sum(res[4 * q + 2], axis=0, keepdims=True)
            dai_ref[q] = jnp.sum(res[4 * q + 3], axis=0, keepdims=True)

    src, mat, col, vec, grid = _scan_specs(s, cbt)
    vsh = jax.ShapeDtypeStruct((GROUPS // 2, 1, LANES), f32)
    return pl.pallas_call(body, name=name, grid=grid, in_specs=[src, mat, col, vec, vec],
                          out_specs=[col, vec, vec], out_shape=[jax.ShapeDtypeStruct(x.shape, bf16), vsh, vsh],
                          compiler_params=_cparams(("parallel",)))(dy, cbt, x, ar, ai)


_GELU_C = math.sqrt(2.0 / math.pi)


def _gelu_fwd(y, u, dsk, *, name, ts=512):
    s, d = y.shape
    ts = min(ts, s)

    def body(y_ref, u_ref, d_ref, o_ref):
        o_ref[...] = jax.nn.gelu(y_ref[...] + d_ref[...] * u_ref[...]).astype(bf16)

    row, vec = pl.BlockSpec((ts, d), lambda i: (i, 0)), pl.BlockSpec((1, d), lambda i: (0, 0))
    return pl.pallas_call(body, name=name, grid=(s // ts,), in_specs=[row, row, vec], out_specs=row,
                          out_shape=jax.ShapeDtypeStruct((s, d), bf16), compiler_params=_cparams(("parallel",)))(y, u, dsk)


def _gelu_bwd(dg, y, u, dsk, *, name, ts=512):
    s, d = y.shape
    ts = min(ts, s)

    def body(dg_ref, y_ref, u_ref, d_ref, dy_ref, du_ref, dd_ref):
        @pl.when(pl.program_id(0) == 0)
        def _():
            dd_ref[...] = jnp.zeros_like(dd_ref)

        uv, dv = u_ref[...], d_ref[...]
        z = y_ref[...] + dv * uv
        th = jnp.tanh(_GELU_C * (z + 0.044715 * z * z * z))
        dz = dg_ref[...] * (0.5 * (1.0 + th) + 0.5 * z * (1.0 - th * th) * _GELU_C * (1.0 + 3 * 0.044715 * z * z))
        dy_ref[...] = dz.astype(bf16)
        du_ref[...] = dz * dv
        dd_ref[...] += jnp.sum(dz * uv, axis=0, keepdims=True)

    row, vec = pl.BlockSpec((ts, d), lambda i: (i, 0)), pl.BlockSpec((1, d), lambda i: (0, 0))
    return pl.pallas_call(
        body, name=name, grid=(s // ts,), in_specs=[row, row, row, vec], out_specs=[row, row, vec],
        out_shape=[jax.ShapeDtypeStruct((s, d), bf16), jax.ShapeDtypeStruct((s, d), f32), jax.ShapeDtypeStruct((1, d), f32)],
        compiler_params=_cparams(("arbitrary",)))(dg, y, u, dsk)


def _glu_fwd(x, a, b, *, name, ts=512):
    s, d = x.shape
    ts = min(ts, s)

    def body(x_ref, a_ref, b_ref, o_ref):
        o_ref[...] = x_ref[...] + a_ref[...] * jax.nn.sigmoid(b_ref[...])

    row = pl.BlockSpec((ts, d), lambda i: (i, 0))
    return pl.pallas_call(body, name=name, grid=(s // ts,), in_specs=[row, row, row], out_specs=row,
                          out_shape=jax.ShapeDtypeStruct((s, d), f32), compiler_params=_cparams(("parallel",)))(x, a, b)


def _glu_bwd(dx, a, b, *, name, ts=512):
    s, d = dx.shape
    ts = min(ts, s)

    def body(dx_ref, a_ref, b_ref, da_ref, db_ref):
        sg = jax.nn.sigmoid(b_ref[...])
        dxv = dx_ref[...]
        da_ref[...] = (dxv * sg).astype(bf16)
        db_ref[...] = (dxv * a_ref[...] * sg * (1.0 - sg)).astype(bf16)

    row = pl.BlockSpec((ts, d), lambda i: (i, 0))
    out = jax.ShapeDtypeStruct((s, d), bf16)
    return pl.pallas_call(body, name=name, grid=(s // ts,), in_specs=[row, row, row], out_specs=[row, row],
                          out_shape=[out, out], compiler_params=_cparams(("parallel",)))(dx, a, b)


def _loss_head(y, target, *, name, ts=512):
    s, d = y.shape
    ts = min(ts, s)

    def body(y_ref, t_ref, dy_ref, dyb_ref, l_ref):
        @pl.when(pl.program_id(0) == 0)
        def _():
            l_ref[...] = jnp.zeros_like(l_ref)

        e = y_ref[...] - t_ref[...]
        dy = e * (1.0 / d)
        dy_ref[...] = dy
        dyb_ref[...] = dy.astype(bf16)
        l_ref[...] += 0.5 * jnp.sum(jnp.mean(e * e, axis=-1, keepdims=True))

    row = pl.BlockSpec((ts, d), lambda i: (i, 0))
    return pl.pallas_call(
        body, name=name, grid=(s // ts,), in_specs=[row, row],
        out_specs=[row, row, pl.BlockSpec((8, LANES), lambda i: (0, 0))],
        out_shape=[jax.ShapeDtypeStruct((s, d), f32), jax.ShapeDtypeStruct((s, d), bf16), jax.ShapeDtypeStruct((8, LANES), f32)],
        compiler_params=_cparams(("arbitrary",)))(y, target)


def _adamw(w, g, m, v, *, name, tr=128):
    r, c = w.shape

    def body(w_ref, g_ref, m_ref, v_ref, d_ref, mo_ref, vo_ref):
        gv = g_ref[...]
        mn = ADAM_B1 * m_ref[...] + (1.0 - ADAM_B1) * gv
        vn = ADAM_B2 * v_ref[...] + (1.0 - ADAM_B2) * (gv * gv)
        m_hat = mn / (1.0 - ADAM_B1 ** ADAM_STEP)
        v_hat = vn / (1.0 - ADAM_B2 ** ADAM_STEP)
        d_ref[...] = -ADAM_LR * (m_hat / (jnp.sqrt(v_hat) + ADAM_EPS) + ADAM_WD * w_ref[...])
        mo_ref[...] = mn
        vo_ref[...] = vn

    row = pl.BlockSpec((tr, c), lambda i: (i, 0))
    out = jax.ShapeDtypeStruct((r, c), f32)
    return pl.pallas_call(body, name=name, grid=(r // tr,), in_specs=[row] * 4, out_specs=[row] * 3,
                          out_shape=[out, out, out], compiler_params=_cparams(("parallel",)))(w, g, m, v)


def _sum_slabs(land, *, name, tr=128):
    n, r, c = land.shape

    def body(l_ref, o_ref):
        acc = l_ref[0].astype(f32)
        for i in range(1, n):
            acc = acc + l_ref[i].astype(f32)
        o_ref[...] = acc

    return pl.pallas_call(body, name=name, grid=(r // tr,), in_specs=[pl.BlockSpec((n, tr, c), lambda i: (0, i, 0))],
                          out_specs=pl.BlockSpec((tr, c), lambda i: (i, 0)), out_shape=jax.ShapeDtypeStruct((r, c), f32),
                          compiler_params=_cparams(("parallel",)))(land)


def _pair_sum(g, theirs, *, name):
    n, r, c = theirs.shape
    tr = _row_tile(r, 1024)

    def body(c_ref, g_ref, t_ref, o_ref):
        o_ref[...] = (g_ref[...].astype(f32) + t_ref[...].astype(f32)).astype(bf16)

    blk = pl.BlockSpec((1, tr, c), lambda j, i, c_ref: (j, i, 0))
    mine = pl.BlockSpec((1, tr, c), lambda j, i, c_ref: (2 * j + c_ref[0], i, 0))
    return pl.pallas_call(
        body, name=name,
        grid_spec=pltpu.PrefetchScalarGridSpec(num_scalar_prefetch=1, grid=(n, r // tr), in_specs=[mine, blk], out_specs=blk),
        out_shape=jax.ShapeDtypeStruct(theirs.shape, bf16),
        compiler_params=_cparams(("parallel", "parallel")))(lax.axis_index("c").astype(jnp.int32).reshape(1), g, theirs)


_MESH = pl.DeviceIdType.MESH
_HBM = pl.BlockSpec(memory_space=pltpu.HBM)
N_CHIP = N_DEV // 2


def _position():
    return lax.axis_index("x"), lax.axis_index("y"), lax.axis_index("c")


def _gather8(x, *, name):
    half = x.shape[0] // 2

    def body(x_ref, o_ref, send_sems, recv_sems, local_sem):
        xx, yy, cc = _position()
        me, sibling = (xx, yy, cc), (xx, yy, 1 - cc)
        here, xn, yn, dg = (xx, yy), (1 - xx, yy), (xx, 1 - yy), (1 - xx, 1 - yy)
        first, second = pl.ds(0, half), pl.ds(half, half)

        def slab(chip, pc, rows=None):
            ref = o_ref.at[4 * chip[0] + 2 * chip[1] + pc]
            return ref if rows is None else ref.at[rows]

        def copy(k, ref, to, src=None):
            return pltpu.make_async_remote_copy(src_ref=ref if src is None else src, dst_ref=ref, send_sem=send_sems.at[k],
                                                recv_sem=recv_sems.at[k], device_id=to, device_id_type=_MESH)

        mine = pltpu.make_async_copy(x_ref, slab(here, cc), local_sem)
        mine.start()
        sends = [copy(0, slab(here, cc), sibling, src=x_ref), copy(1, slab(here, cc), (*xn, cc), src=x_ref),
                 copy(2, slab(here, cc), (*yn, cc), src=x_ref)]
        for cp in sends:
            cp.start()
        copy(1, slab(xn, cc), me).wait_recv()
        sends += [copy(3, slab(xn, cc, first), (*yn, cc)), copy(5, slab(xn, cc), sibling)]
        copy(2, slab(yn, cc), me).wait_recv()
        sends += [copy(4, slab(yn, cc, second), (*xn, cc)), copy(6, slab(yn, cc), sibling)]
        for cp in sends[3:]:
            cp.start()
        copy(3, slab(dg, cc, first), me).wait_recv()
        copy(4, slab(dg, cc, second), me).wait_recv()
        sends.append(copy(7, slab(dg, cc), sibling))
        sends[-1].start()
        for k, chip in ((0, here), (5, xn), (6, yn), (7, dg)):
            copy(k, slab(chip, 1 - cc), me).wait_recv()
        for cp in sends:
            cp.wait_send()
        mine.wait()

    return pl.pallas_call(
        body, name=name, in_specs=[_HBM], out_specs=_HBM, out_shape=jax.ShapeDtypeStruct((N_DEV,) + x.shape, x.dtype),
        scratch_shapes=[pltpu.SemaphoreType.DMA((N_DEV,)), pltpu.SemaphoreType.DMA((N_DEV,)), pltpu.SemaphoreType.DMA],
    )(x)


def _pair_exchange(g, *, name):
    def body(g_ref, land_ref, send_sems, recv_sems):
        xx, yy, cc = _position()
        copies = []
        for j in range(N_CHIP):
            cp = pltpu.make_async_remote_copy(src_ref=g_ref.at[2 * j + 1 - cc], dst_ref=land_ref.at[j], send_sem=send_sems.at[j],
                                              recv_sem=recv_sems.at[j], device_id=(xx, yy, 1 - cc), device_id_type=_MESH)
            cp.start()
            copies.append(cp)
        for cp in copies:
            cp.wait_recv()
        for cp in copies:
            cp.wait_send()

    sems = pltpu.SemaphoreType.DMA((N_CHIP,))
    return pl.pallas_call(body, name=name, in_specs=[_HBM], out_specs=_HBM,
                          out_shape=jax.ShapeDtypeStruct((N_CHIP,) + g.shape[1:], g.dtype), scratch_shapes=[sems, sems])(g)


def _cross_exchange(p, *, name):
    half = p.shape[1] // 2

    def body(p_ref, o_ref, relay_ref, send_sems, recv_sems, local_sem):
        xx, yy, cc = _position()
        me = (xx, yy, cc)
        xn, yn, dg = (1 - xx, yy), (xx, 1 - yy), (1 - xx, 1 - yy)
        idx = lambda chip: 2 * chip[0] + chip[1]
        mine = idx((xx, yy))
        first, second = pl.ds(0, half), pl.ds(half, half)

        def copy(k, src, dst, to):
            return pltpu.make_async_remote_copy(src_ref=src, dst_ref=dst, send_sem=send_sems.at[k], recv_sem=recv_sems.at[k],
                                                device_id=to, device_id_type=_MESH)

        local = pltpu.make_async_copy(p_ref.at[mine], o_ref.at[mine], local_sem)
        local.start()
        sends = [copy(0, p_ref.at[idx(xn)], o_ref.at[mine], (*xn, cc)),
                 copy(1, p_ref.at[idx(dg)].at[first], relay_ref.at[0], (*xn, cc)),
                 copy(2, p_ref.at[idx(yn)], o_ref.at[mine], (*yn, cc)),
                 copy(3, p_ref.at[idx(dg)].at[second], relay_ref.at[1], (*yn, cc))]
        for cp in sends:
            cp.start()
        copy(1, relay_ref.at[0], relay_ref.at[0], me).wait_recv()
        sends.append(copy(4, relay_ref.at[0], o_ref.at[idx(xn)].at[first], (*yn, cc)))
        sends[-1].start()
        copy(3, relay_ref.at[1], relay_ref.at[1], me).wait_recv()
        sends.append(copy(5, relay_ref.at[1], o_ref.at[idx(yn)].at[second], (*xn, cc)))
        sends[-1].start()
        for k, dst in ((0, o_ref.at[idx(xn)]), (2, o_ref.at[idx(yn)]), (4, o_ref.at[idx(dg)].at[first]),
                       (5, o_ref.at[idx(dg)].at[second])):
            copy(k, dst, dst, me).wait_recv()
        for cp in sends:
            cp.wait_send()
        local.wait()

    sems = pltpu.SemaphoreType.DMA((6,))
    relay = jax.ShapeDtypeStruct((2, half) + p.shape[2:], p.dtype)
    return pl.pallas_call(body, name=name, in_specs=[_HBM], out_specs=[_HBM, _HBM],
                          out_shape=[jax.ShapeDtypeStruct(p.shape, p.dtype), relay],
                          scratch_shapes=[sems, sems, pltpu.SemaphoreType.DMA])(p)[0]


def _all_sum(x, *, name):
    return _sum_slabs(_gather8(x, name=f"gather_{name}"), name=f"sum_{name}", tr=min(128, x.shape[0]))


def _pack_slabs(parts, rows, axis=0):
    lead = parts[0].shape[:axis]
    slabs = [p.reshape(lead + (-1, D)) for p in parts]
    used = sum(sl.shape[axis] for sl in slabs)
    return jnp.concatenate(slabs + [jnp.zeros(lead + (rows - used, D), slabs[0].dtype)], axis=axis)


def _unpack_slabs(slab, shapes):
    lead, out, off = slab.shape[:-2], [], 0
    for shp in shapes:
        n = int(np.prod(shp)) // D
        out.append(slab[..., off:off + n, :].reshape(lead + tuple(shp)))
        off += n
    return out


def _pack_rows(parts, rows):
    flat = jnp.concatenate([p.reshape(-1) for p in parts])
    return jnp.pad(flat, (0, rows * D - flat.shape[0])).reshape(rows, D)


def _unpack_rows(slab, shapes):
    flat, out, off = slab.reshape(-1), [], 0
    for shp in shapes:
        n = int(np.prod(shp))
        out.append(flat[off:off + n].reshape(shp))
        off += n
    return out


def _full_shape(shard, axis):
    return tuple(d * N_DEV if i == axis else d for i, d in enumerate(shard))


def _row(v):
    return v.reshape(1, -1).astype(f32)


def _pad_gain(g):
    return jnp.pad(g.astype(f32), (0, HEAD_PAD - QK)).reshape(1, HEAD_PAD)


def _ffn_fwd(x, p, tag):
    h = _rms_fwd(x, p["norm"], name=f"ffn_norm_{tag}")
    zg = _mm(h, p["wgT"], tb=True, tn=FFN_H, name=f"ffn_up_g_{tag}")
    zv = _mm(h, p["wvT"], tb=True, tn=FFN_H, name=f"ffn_up_v_{tag}")
    a = _ffn_act_fwd(zg, zv, p["cwg"], p["cwv"], name=f"ffn_act_{tag}")
    y = _mm(a, p["wd"], add=x, name=f"ffn_down_{tag}")
    return y, (x, h, zg, zv, a)


def _ffn_bwd(dy, dyb, p, saved, tag):
    x, h, zg, zv, a = saved
    g = {}
    da = _mm(dyb, p["wd"], tb=True, out_dtype=bf16, name=f"ffn_down_dx_{tag}")
    g["wd"] = _mm(a, dyb, ta=True, out_dtype=bf16, name=f"ffn_down_dw_{tag}")
    dzg, dzv, g["cwg"], g["cwv"] = _ffn_act_bwd(da, zg, zv, p["cwg"], p["cwv"], name=f"ffn_act_bwd_{tag}")
    g["wgT"] = _mm(dzg, h, ta=True, out_dtype=bf16, name=f"ffn_up_g_dw_{tag}")
    g["wvT"] = _mm(dzv, h, ta=True, out_dtype=bf16, name=f"ffn_up_v_dw_{tag}")
    dh = _mm(dzg, p["wgT"], name=f"ffn_up_g_dx_{tag}")
    dh = _mm(dzv, p["wvT"], add=dh, name=f"ffn_up_v_dx_{tag}")
    dx, dxb, g["norm"] = _rms_bwd(dh, x, p["norm"], res=dy, name=f"ffn_norm_bwd_{tag}")
    return dx, dxb, g


def _mla_fwd(x, p, tabs, tag):
    cos_t, sin_t = tabs
    h = _rms_fwd(x, p["norm"], name=f"attn_norm_{tag}")
    proj = _mm(h, p["w_inT"], tb=True, name=f"mix_in_{tag}")
    cqn = _rms_fwd(proj, p["cq_norm"], col=0, name=f"cq_norm_{tag}")
    ckvn = _rms_fwd(proj, p["ckv_norm"], col=1, name=f"ckv_norm_{tag}")
    q_raw = _mm(cqn, p["w_uqT"], tb=True, name=f"uq_{tag}")
    kv_raw = _mm(ckvn, p["w_ukvT"], tb=True, name=f"ukv_{tag}")
    q, k, v = _qk_prep_fwd(q_raw, kv_raw, proj, p["q_gain"], p["k_gain"], cos_t, sin_t, name=f"qk_prep_{tag}")
    o, lse = _flash_fwd(q, k, v, name=f"flash_fwd_{tag}")
    conv = _sconv_fwd(proj, p["sconv_w"], name=f"sconv_{tag}")
    y = _mm(conv, p["w_out"][HEADS * HEAD_PAD:], add=x, name=f"mix_out_conv_{tag}")
    y = _mm(o, p["w_out"][:HEADS * HEAD_PAD], add=y, name=f"mix_out_{tag}")
    return y, (x, h, proj, cqn, ckvn, q_raw, kv_raw, q, k, v, o, lse, conv)


def _mla_bwd(dy, dyb, p, tabs, saved, tag):
    cos_t, sin_t = tabs
    x, h, proj, cqn, ckvn, q_raw, kv_raw, q, k, v, o, lse, conv = saved
    s = x.shape[0]
    g = {}
    dmix = _mm(dyb, p["w_out"], tb=True, name=f"mix_out_dx_{tag}")
    g["w_out"] = jnp.concatenate([_mm(o, dyb, ta=True, out_dtype=bf16, name=f"mix_out_dw_{tag}"),
                                  _mm(conv, dyb, ta=True, out_dtype=bf16, name=f"mix_out_conv_dw_{tag}")], axis=0)
    dgb, dgc, dci, g["sconv_w"] = _sconv_bwd(dmix, proj, p["sconv_w"], name=f"sconv_bwd_{tag}")
    dq, delta = _flash_bwd_dq(q, k, v, o, dmix, lse, name=f"flash_dq_{tag}")
    dk, dv = _flash_bwd_dkv(q, k, v, dmix, lse.reshape(HEADS, 1, s), delta.reshape(HEADS, 1, s), name=f"flash_dkv_{tag}")
    dq_raw, dkv_raw, dkr, g["q_gain"], g["k_gain"] = _qk_prep_bwd(
        dq, dk, dv, q_raw, kv_raw, proj, p["q_gain"], p["k_gain"], cos_t, sin_t, name=f"qk_prep_bwd_{tag}")
    dcqn = _mm(dq_raw, p["w_uqT"], name=f"uq_dx_{tag}")
    g["w_uqT"] = _mm(dq_raw, cqn, ta=True, out_dtype=bf16, name=f"uq_dw_{tag}")
    dckvn = _mm(dkv_raw, p["w_ukvT"], name=f"ukv_dx_{tag}")
    g["w_ukvT"] = _mm(dkv_raw, ckvn, ta=True, out_dtype=bf16, name=f"ukv_dw_{tag}")
    dcq, g["cq_norm"] = _rms_bwd(dcqn, proj, p["cq_norm"], col=0, out_dtype=bf16, name=f"cq_norm_bwd_{tag}")
    dckv, g["ckv_norm"] = _rms_bwd(dckvn, proj, p["ckv_norm"], col=1, out_dtype=bf16, name=f"ckv_norm_bwd_{tag}")
    dproj = jnp.concatenate([dcq, dckv, dgb, dgc, dci, dkr.astype(bf16)], axis=1)
    dh = _mm(dproj, p["w_inT"], name=f"mix_in_dx_{tag}")
    g["w_inT"] = _mm(dproj, h, ta=True, out_dtype=bf16, name=f"mix_in_dw_{tag}")
    dx, dxb, g["norm"] = _rms_bwd(dh, x, p["norm"], res=dy, name=f"attn_norm_bwd_{tag}")
    return dx, dxb, g


def _block_diag(wg):
    nb, ng, r, c = wg.shape
    eye = jnp.eye(ng, dtype=wg.dtype)
    return (wg[:, :, :, None, :] * eye[None, :, None, :, None]).reshape(nb, ng * r, ng * c)


def _s5_mats(bbr, bbi, c_re, c_im):
    nb = GROUPS // 8
    b4 = jnp.stack([bbr.reshape(GROUPS, STATE, GROUP), bbi.reshape(GROUPS, STATE, GROUP)], axis=1)
    wg = jnp.transpose(b4, (0, 3, 1, 2)).reshape(nb, 8, GROUP, 2 * STATE)
    cg = jnp.stack([c_re, -c_im], axis=1)
    cg = jnp.transpose(cg, (0, 1, 3, 2)).reshape(nb, 8, 2 * STATE, GROUP)
    return _state_layout(_block_diag(wg), 2), _state_layout(_block_diag(cg), 1)


def _state_layout(m, axis):
    shp = m.shape
    m = m.reshape(shp[:axis] + (4, 2, 2, STATE) + shp[axis + 1:])
    return jnp.swapaxes(m, axis + 1, axis + 2).reshape(shp)


def _group_blocks(d):
    d = d.reshape(GROUPS // 2, 2, GROUP, 2, 2, STATE)
    return jnp.stack([d[:, 0, :, :, 0, :], d[:, 1, :, :, 1, :]], axis=1).reshape(GROUPS, GROUP, 2, STATE)


def _s5_fwd(x, p, tag):
    h = _rms_fwd(x, p["norm"], name=f"ssm_norm_{tag}")
    u, ub = _mm(h, p["w_in"], twin=True, name=f"ssm_in_{tag}")
    ar, ai, bbr, bbi = _disc_fwd(p["lr"], p["li"], p["ls"], p["br"], p["bi"], name=f"disc_{tag}")
    wb, cb = _s5_mats(bbr, bbi, p["c_re"], p["c_im"])
    a1, a2 = ar.reshape(GROUPS // 2, 1, LANES), ai.reshape(GROUPS // 2, 1, LANES)
    xs = _scan_fwd(ub, wb.astype(bf16), a1, a2, name=f"ssm_scan_{tag}")
    y = _bd_nn(xs, cb.astype(bf16), name=f"ssm_y_{tag}")
    g = _gelu_fwd(y, u, p["d_skip"], name=f"ssm_gelu_{tag}")
    a = _mm(g, p["wgaT"], tb=True, name=f"glu_a_{tag}")
    b = _mm(g, p["wgbT"], tb=True, name=f"glu_b_{tag}")
    out = _glu_fwd(x, a, b, name=f"glu_{tag}")
    return out, (x, h, u, ub, wb, cb, a1, a2, xs, y, g, a, b)


def _s5_bwd(dout, p, saved, tag):
    x, h, u, ub, wb, cb, a1, a2, xs, y, g, a, b = saved
    gr = {}
    da, db = _glu_bwd(dout, a, b, name=f"glu_bwd_{tag}")
    dg = _mm(da, p["wgaT"], name=f"glu_a_dx_{tag}")
    dg = _mm(db, p["wgbT"], add=dg, name=f"glu_b_dx_{tag}")
    gr["wgaT"] = _mm(da, g, ta=True, out_dtype=bf16, name=f"glu_a_dw_{tag}")
    gr["wgbT"] = _mm(db, g, ta=True, out_dtype=bf16, name=f"glu_b_dw_{tag}")
    dy, du1, gr["d_skip"] = _gelu_bwd(dg, y, u, p["d_skip"], name=f"ssm_gelu_bwd_{tag}")
    dct = _group_blocks(_bd_tn_diag(dy, xs, name=f"ssm_y_dw_{tag}"))
    gs, dar, dai = _scan_bwd(dy, jnp.swapaxes(cb, 1, 2).astype(bf16), xs, a1, a2, name=f"ssm_scan_bwd_{tag}")
    du = _bd_nn(gs, jnp.swapaxes(wb, 1, 2).astype(bf16), add=du1, out_dtype=bf16, name=f"ssm_bu_dx_{tag}")
    dwg = _group_blocks(_bd_tn_diag(ub, gs, name=f"ssm_bu_dw_{tag}"))
    dh = _mm(du, p["w_in"], tb=True, name=f"ssm_in_dx_{tag}")
    gr["w_in"] = _mm(h, du, ta=True, out_dtype=bf16, name=f"ssm_in_dw_{tag}")
    dx, dxb, gr["norm"] = _rms_bwd(dh, x, p["norm"], res=dout, name=f"ssm_norm_bwd_{tag}")
    dbb = jnp.transpose(dwg, (2, 0, 3, 1)).reshape(2, GROUPS, STATE * GROUP)
    gr["c_re"] = dct[:, :, 0, :]
    gr["c_im"] = -dct[:, :, 1, :]
    dlr, dli, dls, dbr, dbi = _disc_bwd(p["lr"], p["li"], p["ls"], p["br"], p["bi"], dar.reshape(GROUPS, STATE),
                                        dai.reshape(GROUPS, STATE), dbb[0], dbb[1], name=f"disc_bwd_{tag}")
    gr["lr"], gr["li"], gr["ls"] = dlr, dli, dls.reshape(GROUPS)
    gr["br"], gr["bi"] = dbr.reshape(GROUPS, STATE, GROUP), dbi.reshape(GROUPS, STATE, GROUP)
    return dx, dxb, gr


def _slab_shape(shard, axis):
    return (shard[0], shard[2], shard[1]) if axis == 2 else shard


def _to_slab(w, axis):
    return jnp.swapaxes(w, 1, 2) if axis == 2 else w


def _mix_in_pad(wt):
    z = lambda n: jnp.zeros((n, wt.shape[1]), wt.dtype)
    return jnp.concatenate([wt[:512], wt[544:2080], z(NOPE), wt[512:544], z(HEAD_PAD - QK)], axis=0)


def _mix_in_unpad(g):
    return jnp.concatenate([g[:512], g[2048 + NOPE:2048 + QK], g[512:2048]], axis=0)


def _mix_out_pad(w):
    att = jnp.pad(w[:512].reshape(HEADS, NOPE, D), ((0, 0), (NOPE, 0), (0, 0))).reshape(HEADS * HEAD_PAD, D)
    return jnp.concatenate([att, w[512:]], axis=0)


def _mix_out_unpad(g):
    att = g[:HEADS * HEAD_PAD].reshape(HEADS, HEAD_PAD, D)[:, NOPE:, :].reshape(HEADS * NOPE, D)
    return jnp.concatenate([att, g[HEADS * HEAD_PAD:]], axis=0)


def _layer_params(wl, ws, layer):
    i = layer // 2
    half = N_DEV // 2
    up = wl["ffn_w_up"][layer]
    ffn = dict(norm=_row(ws["ffn_norm"][layer]), wgT=up[:half].reshape(FFN_H, D), wvT=up[half:].reshape(FFN_H, D),
               cwg=ws["ffn_conv_w"][layer][:, :FFN_H], cwv=ws["ffn_conv_w"][layer][:, FFN_H:],
               wd=wl["ffn_w_down"][layer].reshape(FFN_H, D))
    if layer % 2 == 0:
        uq = jnp.pad(wl["w_uq"][i], ((0, 0), (0, HEAD_PAD - QK), (0, 0)))
        mixer = dict(norm=_row(ws["attn_norm"][i]), w_inT=_mix_in_pad(wl["mix_w_in"][i].reshape(-1, D)),
                     cq_norm=_row(ws["cq_norm"][i]), ckv_norm=_row(ws["ckv_norm"][i]),
                     w_uqT=uq.reshape(HEADS * HEAD_PAD, LORA), w_ukvT=wl["w_ukv"][i].reshape(HEADS * HEAD_PAD, LORA),
                     q_gain=_pad_gain(ws["q_gain"][i]), k_gain=_pad_gain(ws["k_gain"][i]), sconv_w=ws["sconv_w"][i],
                     w_out=_mix_out_pad(wl["mix_w_out"][i].reshape(D, D)))
    else:
        glu = wl["w_glu"][i]
        mixer = dict(norm=_row(ws["ssm_norm"][i]), w_in=wl["ssm_w_in"][i].reshape(D, D), lr=ws["lambda_re"][i],
                     li=ws["lambda_im"][i], ls=ws["log_step"][i].reshape(GROUPS, 1),
                     br=ws["b_re"][i].reshape(GROUPS, STATE * GROUP), bi=ws["b_im"][i].reshape(GROUPS, STATE * GROUP),
                     c_re=ws["c_re"][i], c_im=ws["c_im"][i], d_skip=_row(ws["d_skip"][i]),
                     wgaT=glu[:half].reshape(D, D), wgbT=glu[half:].reshape(D, D))
    return mixer, ffn


def _collect_grads(gm, gf):
    ev, od, half = (0, 2), (1, 3), N_DEV // 2
    st = lambda xs: jnp.stack(xs, axis=0)
    per_dev = list
    halves = lambda a, b, rows: jnp.concatenate([a.reshape(half, rows, D), b.reshape(half, rows, D)], axis=0)
    big = {
        "ffn_w_up": per_dev([halves(gf[l]["wgT"], gf[l]["wvT"], FFN_H // half) for l in range(4)]),
        "ffn_w_down": per_dev([gf[l]["wd"].reshape(N_DEV, -1, D) for l in range(4)]),
        "w_glu": per_dev([halves(gm[l]["wgaT"], gm[l]["wgbT"], D // half) for l in od]),
        "mix_w_out": per_dev([_mix_out_unpad(gm[l]["w_out"]).reshape(N_DEV, -1, D) for l in ev]),
        "ssm_w_in": per_dev([gm[l]["w_in"].reshape(N_DEV, -1, D) for l in od]),
        "w_ukv": per_dev([gm[l]["w_ukvT"].reshape(N_DEV, HEAD_PAD, LORA) for l in ev]),
        "w_uq": per_dev([gm[l]["w_uqT"].reshape(N_DEV, HEAD_PAD, LORA)[:, :QK] for l in ev]),
        "mix_w_in": per_dev([_mix_in_unpad(gm[l]["w_inT"]).reshape(N_DEV, -1, D) for l in ev]),
    }
    small = {
        "attn_norm": st([gm[l]["norm"].reshape(D) for l in ev]),
        "cq_norm": st([gm[l]["cq_norm"].reshape(LORA) for l in ev]),
        "ckv_norm": st([gm[l]["ckv_norm"].reshape(LORA) for l in ev]),
        "q_gain": st([gm[l]["q_gain"].reshape(HEAD_PAD)[:QK] for l in ev]),
        "k_gain": st([gm[l]["k_gain"].reshape(HEAD_PAD)[:QK] for l in ev]),
        "sconv_w": st([gm[l]["sconv_w"] for l in ev]),
        "ssm_norm": st([gm[l]["norm"].reshape(D) for l in od]),
        "lambda_re": st([gm[l]["lr"] for l in od]), "lambda_im": st([gm[l]["li"] for l in od]),
        "log_step": st([gm[l]["ls"] for l in od]),
        "b_re": st([gm[l]["br"] for l in od]), "b_im": st([gm[l]["bi"] for l in od]),
        "c_re": st([gm[l]["c_re"] for l in od]), "c_im": st([gm[l]["c_im"] for l in od]),
        "d_skip": st([gm[l]["d_skip"].reshape(D) for l in od]),
        "ffn_norm": st([gf[l]["norm"].reshape(D) for l in range(4)]),
        "ffn_conv_w": st([jnp.concatenate([gf[l]["cwg"], gf[l]["cwv"]], axis=1) for l in range(4)]),
    }
    return big, small


def _local_step(x, target, wl, ws):
    s = x.shape[0]
    tabs = _rope_tables(s)
    saved, params = [], []
    for layer in range(4):
        mixer, ffn = _layer_params(wl, ws, layer)
        params.append((mixer, ffn))
        if layer % 2 == 0:
            x, sm = _mla_fwd(x, mixer, tabs, f"l{layer}")
        else:
            x, sm = _s5_fwd(x, mixer, f"l{layer}")
        x, sf = _ffn_fwd(x, ffn, f"l{layer}")
        saved.append((sm, sf))
    dx, dxb, loss = _loss_head(x, target, name="loss_head")
    gm, gf = [None] * 4, [None] * 4
    for layer in reversed(range(4)):
        mixer, ffn = params[layer]
        sm, sf = saved[layer]
        dx, dxb, gf[layer] = _ffn_bwd(dx, dxb, ffn, sf, f"l{layer}")
        if layer % 2 == 0:
            dx, dxb, gm[layer] = _mla_bwd(dx, dxb, mixer, tabs, sm, f"l{layer}")
        else:
            dx, dxb, gm[layer] = _s5_bwd(dx, mixer, sm, f"l{layer}")
    return loss, dx, _collect_grads(gm, gf)


def kernel(x, attn_norm, mix_w_in, cq_norm, ckv_norm, w_uq, w_ukv, q_gain, k_gain, sconv_w, mix_w_out, ssm_norm, ssm_w_in, lambda_re, lambda_im, log_step, b_re, b_im, c_re, c_im, d_skip, w_glu, ffn_norm, ffn_w_up, ffn_conv_w, ffn_w_down, loss_target, m_attn_norm, m_mix_w_in, m_cq_norm, m_ckv_norm, m_w_uq, m_w_ukv, m_q_gain, m_k_gain, m_sconv_w, m_mix_w_out, m_ssm_norm, m_ssm_w_in, m_lambda_re, m_lambda_im, m_log_step, m_b_re, m_b_im, m_c_re, m_c_im, m_d_skip, m_w_glu, m_ffn_norm, m_ffn_w_up, m_ffn_conv_w, m_ffn_w_down, v_attn_norm, v_mix_w_in, v_cq_norm, v_ckv_norm, v_w_uq, v_w_ukv, v_q_gain, v_k_gain, v_sconv_w, v_mix_w_out, v_ssm_norm, v_ssm_w_in, v_lambda_re, v_lambda_im, v_log_step, v_b_re, v_b_im, v_c_re, v_c_im, v_d_skip, v_w_glu, v_ffn_norm, v_ffn_w_up, v_ffn_conv_w, v_ffn_w_down):
    args = dict(locals())
    wsh = {n: args[n] for n in WEIGHTS}
    msh = {n: args["m_" + n] for n in WEIGHTS}
    vsh = {n: args["v_" + n] for n in WEIGHTS}
    me = 4 * lax.axis_index("x") + 2 * lax.axis_index("y") + lax.axis_index("c")
    big_names = [n for n, _, _ in BIG]
    slab_shapes = [_slab_shape(sh, ax) for _, sh, ax in BIG]
    small_names = [n for n, _ in REPL] + [n for n, _, _ in SMALL]

    mine = _pack_slabs([_to_slab(wsh[n], ax).astype(bf16) for n, _, ax in BIG], BIG_ROWS)
    gathered, wl, off = _gather8(mine, name="gather_weights"), {}, 0
    for n, (layers, rows, inner) in zip(big_names, slab_shapes):
        per = rows * inner // D
        wl[n] = [gathered[:, off + l * per:off + (l + 1) * per, :].reshape(N_DEV, rows, inner) for l in range(layers)]
        off += layers * per
    placed = []
    for n, shard, axis in SMALL:
        start = [0] * len(shard)
        start[axis] = me * shard[axis]
        placed.append(lax.dynamic_update_slice(jnp.zeros(_full_shape(shard, axis), f32), wsh[n], start))
    small_all = _all_sum(_pack_rows(placed, SMALL_FWD_ROWS), name="small_params")
    ws = dict(zip([n for n, _, _ in SMALL], _unpack_rows(small_all, [_full_shape(sh, ax) for _, sh, ax in SMALL])))
    ws.update({n: wsh[n] for n, _ in REPL})

    loss8, grad_x, (big_grads, grads) = _local_step(x[0], loss_target[0], wl, ws)

    pieces = []
    for n in big_names:
        layers = big_grads[n]
        tiled = (layers[0].shape[1] * layers[0].shape[2] // D) % 16 == 0
        pieces += layers if tiled else [jnp.stack(layers, axis=1)]
    contrib = _pack_slabs(pieces, BIG_ROWS, axis=1)
    chip_sum = _pair_sum(contrib, _pair_exchange(contrib, name="grads_pair_exchange"), name="grads_pair_sum")
    g_big = _sum_slabs(_cross_exchange(chip_sum, name="grads_cross_exchange"), name="grads_chip_sum",
                       tr=_row_tile(BIG_ROWS, 1024))
    small_vec = _pack_rows([grads[n] for n, _ in REPL] + [grads[n] for n, _, _ in SMALL] + [loss8[0, :1]], SMALL_ROWS)
    small_sum = _all_sum(small_vec, name="small_grads")
    parts = _unpack_rows(small_sum, [sh for _, sh in REPL] + [_full_shape(sh, ax) for _, sh, ax in SMALL] + [(1,)])
    g = {n: val for (n, _), val in zip(REPL, parts)}
    for (n, shard, axis), val in zip(SMALL, parts[len(REPL):]):
        start = [0] * len(shard)
        start[axis] = me * shard[axis]
        g[n] = lax.dynamic_slice(val, start, shard)
    loss = parts[-1].reshape(())
    for (n, _, axis), val in zip(BIG, _unpack_slabs(g_big, slab_shapes)):
        g[n] = _to_slab(val, axis)

    delta, new_m, new_v = {}, {}, {}
    for n, shard, _ in BIG:
        flat = lambda a: a.reshape(-1, shard[-1])
        outs = _adamw(flat(wsh[n]), flat(g[n]), flat(msh[n]), flat(vsh[n]), name=f"adamw_{n}",
                      tr=_row_tile(shard[0] * shard[1], 512))
        delta[n], new_m[n], new_v[n] = [o.reshape(shard) for o in outs]
    small_state = [_pack_rows([src[n] for n in small_names], SMALL_ROWS) for src in (wsh, g, msh, vsh)]
    for dst, slab in zip((delta, new_m, new_v), _adamw(*small_state, name="adamw_small")):
        dst.update(zip(small_names, _unpack_rows(slab, [wsh[n].shape for n in small_names])))

    return (loss, grad_x[None], *[g[n] for n in WEIGHTS], *[delta[n] for n in WEIGHTS],
            *[new_m[n] for n in WEIGHTS], *[new_v[n] for n in WEIGHTS])
```

```python
import math

import numpy as np
import jax
import jax.numpy as jnp
from jax import lax
from jax.experimental import pallas as pl
from jax.experimental.pallas import tpu as pltpu

f32, bf16 = jnp.float32, jnp.bfloat16

N_DEV = 8
D = 1024
HEADS = 8
NOPE, ROPE, QK = 64, 32, 96
HEAD_PAD = 128
LORA = 256
CONV_CH = 512
MIX_IN_PAD = 2176
FFN_H = 2816
GROUPS, GROUP, STATE = 64, 16, 64
EPS = 1e-6
ROPE_THETA = 10000.0
ADAM_LR, ADAM_B1, ADAM_B2, ADAM_EPS, ADAM_WD, ADAM_STEP = 0.001, 0.9, 0.999, 1e-08, 0.01, 10
LANES = 128
PAIR_LANES = 2 * LANES
VMEM_LIMIT = 56 << 20
MM_VMEM_BUDGET = 40 << 20
ROW_BLOCK = 1024
NEG = -1e30

BIG = (
    ("ffn_w_up", (4, 1024, 704), 2), ("ffn_w_down", (4, 352, 1024), 1), ("w_glu", (2, 1024, 256), 2),
    ("mix_w_out", (2, 128, 1024), 1), ("ssm_w_in", (2, 128, 1024), 1), ("w_ukv", (2, 256, 128), 2),
    ("w_uq", (2, 256, 96), 2), ("mix_w_in", (2, 1024, 260), 2))
REPL = (("attn_norm", (2, 1024)), ("cq_norm", (2, 256)), ("ckv_norm", (2, 256)), ("q_gain", (2, 96)),
        ("k_gain", (2, 96)), ("lambda_re", (2, 64, 64)), ("lambda_im", (2, 64, 64)), ("log_step", (2, 64)),
        ("b_re", (2, 64, 64, 16)), ("b_im", (2, 64, 64, 16)), ("c_re", (2, 64, 16, 64)), ("c_im", (2, 64, 16, 64)),
        ("ffn_norm", (4, 1024)))
SMALL = (("sconv_w", (2, 3, 64), 2), ("ssm_norm", (2, 128), 1), ("d_skip", (2, 128), 1), ("ffn_conv_w", (4, 3, 704), 2))
WEIGHTS = ['attn_norm', 'mix_w_in', 'cq_norm', 'ckv_norm', 'w_uq', 'w_ukv', 'q_gain', 'k_gain', 'sconv_w', 'mix_w_out',
           'ssm_norm', 'ssm_w_in', 'lambda_re', 'lambda_im', 'log_step', 'b_re', 'b_im', 'c_re', 'c_im', 'd_skip',
           'w_glu', 'ffn_norm', 'ffn_w_up', 'ffn_conv_w', 'ffn_w_down']
BIG_ROWS = 5888
SMALL_FWD_ROWS = 80
SMALL_ROWS = 640


def _cparams(sem=None, **kw):
    return pltpu.CompilerParams(dimension_semantics=sem, vmem_limit_bytes=VMEM_LIMIT, **kw)


def _row_tile(rows, target):
    fits = [t for t in range(16, min(rows, target) + 1, 16) if rows % t == 0]
    return max(fits) if fits else rows


def _tile(n, target):
    best = 0
    for t in range(LANES, min(n, target) + 1, LANES):
        if n % t == 0:
            best = t
    return best if best else n


def _mm(a, b, *, ta=False, tb=False, out_dtype=f32, add=None, twin=False, name, tm=1024, tn=1536):
    m, k = (a.shape[1], a.shape[0]) if ta else a.shape
    n = b.shape[0] if tb else b.shape[1]
    assert (b.shape[1] if tb else b.shape[0]) == k
    tm = _tile(m, tm)
    tn_ = _tile(n, tn)
    tn = n if (tn_ < 256 and n <= 2304) else tn_

    def vmem_bytes(t):
        io = 2 * (tm * t * a.dtype.itemsize + t * tn * b.dtype.itemsize + tm * tn * (jnp.dtype(out_dtype).itemsize + 2 * twin))
        return io + (2 * tm * tn * 4 if add is not None else 0) + (tm * tn * 4 if t < k else 0)

    tk = next((t for t in [k] + [t for t in range(k - LANES, 0, -LANES) if k % t == 0] if vmem_bytes(t) <= MM_VMEM_BUDGET), LANES)
    nk = k // tk
    dn = (((0 if ta else 1,), (1 if tb else 0,)), ((), ()))

    def body(*refs):
        a_ref, b_ref = refs[:2]
        add_ref = refs[2] if add is not None else None
        o_ref = refs[3] if add is not None else refs[2]
        twin_ref = refs[4 if add is not None else 3] if twin else None
        part = lax.dot_general(a_ref[...].astype(bf16), b_ref[...].astype(bf16), dn, preferred_element_type=f32)

        def finish(r):
            if add is not None:
                r = r + add_ref[...].astype(f32)
            o_ref[...] = r.astype(out_dtype)
            if twin:
                twin_ref[...] = r.astype(bf16)

        if nk == 1:
            finish(part)
            return
        acc = refs[-1]
        kk = pl.program_id(2)

        @pl.when(kk == 0)
        def _():
            acc[...] = part

        @pl.when(kk > 0)
        def _():
            acc[...] += part

        @pl.when(kk == nk - 1)
        def _():
            finish(acc[...])

    a_spec = pl.BlockSpec((tk, tm), lambda i, j, kk: (kk, i)) if ta else pl.BlockSpec((tm, tk), lambda i, j, kk: (i, kk))
    b_spec = pl.BlockSpec((tn, tk), lambda i, j, kk: (j, kk)) if tb else pl.BlockSpec((tk, tn), lambda i, j, kk: (kk, j))
    in_specs, args = [a_spec, b_spec], [a, b]
    if add is not None:
        in_specs.append(pl.BlockSpec((tm, tn), lambda i, j, kk: (i, j)))
        args.append(add)
    o_spec, o_shape = pl.BlockSpec((tm, tn), lambda i, j, kk: (i, j)), jax.ShapeDtypeStruct((m, n), out_dtype)
    return pl.pallas_call(
        body, name=name, grid=(m // tm, n // tn, nk), in_specs=in_specs,
        out_specs=[o_spec, o_spec] if twin else o_spec,
        out_shape=[o_shape, jax.ShapeDtypeStruct((m, n), bf16)] if twin else o_shape,
        scratch_shapes=[pltpu.VMEM((tm, tn), f32)] if nk > 1 else [],
        compiler_params=_cparams(("parallel", "parallel", "arbitrary")))(*args)


def _bd_nn(a, w, *, out_dtype=f32, add=None, name, ts=2048):
    s = a.shape[0]
    nb, ka, no = w.shape
    ts = min(ts, s)

    def body(a_ref, w_ref, *rest):
        r = jnp.dot(a_ref[...].astype(bf16), w_ref[0].astype(bf16), preferred_element_type=f32)
        if add is not None:
            r = r + rest[0][...].astype(f32)
        rest[-1][...] = r.astype(out_dtype)

    o_spec = pl.BlockSpec((ts, no), lambda b, i: (i, b))
    return pl.pallas_call(
        body, name=name, grid=(nb, s // ts),
        in_specs=[pl.BlockSpec((ts, ka), lambda b, i: (i, b)), pl.BlockSpec((1, ka, no), lambda b, i: (b, 0, 0))]
        + ([o_spec] if add is not None else []),
        out_specs=o_spec, out_shape=jax.ShapeDtypeStruct((s, nb * no), out_dtype),
        compiler_params=_cparams(("parallel", "parallel")))(a, w, *([add] if add is not None else []))


def _bd_tn_diag(a, g, *, name, ts=2048):
    s = a.shape[0]
    nb = a.shape[1] // LANES
    ts = min(ts, s)
    ni = s // ts

    def body(a_ref, g_ref, o_ref, acc):
        i = pl.program_id(1)
        part = lax.dot_general(a_ref[...].astype(bf16), g_ref[...].astype(bf16), (((0,), (0,)), ((), ())),
                               preferred_element_type=f32)

        @pl.when(i == 0)
        def _():
            acc[...] = part

        @pl.when(i > 0)
        def _():
            acc[...] += part

        @pl.when(i == ni - 1)
        def _():
            for j in range(8):
                o_ref[0, j] = acc[j * GROUP:(j + 1) * GROUP, (j // 2) * PAIR_LANES:(j // 2 + 1) * PAIR_LANES]

    return pl.pallas_call(
        body, name=name, grid=(nb, ni),
        in_specs=[pl.BlockSpec((ts, LANES), lambda b, i: (i, b)), pl.BlockSpec((ts, 8 * LANES), lambda b, i: (i, b))],
        out_specs=pl.BlockSpec((1, 8, GROUP, PAIR_LANES), lambda b, i: (b, 0, 0, 0)),
        out_shape=jax.ShapeDtypeStruct((nb, 8, GROUP, PAIR_LANES), f32),
        scratch_shapes=[pltpu.VMEM((LANES, 8 * LANES), f32)],
        compiler_params=_cparams(("parallel", "arbitrary")))(a, g)


def _rms_fwd(x, g, *, col=0, name, ts=ROW_BLOCK):
    s, d = x.shape[0], g.shape[1]
    ts = min(ts, s)

    def body(x_ref, g_ref, o_ref):
        xv = x_ref[...].astype(f32)
        r = lax.rsqrt(jnp.mean(xv * xv, axis=-1, keepdims=True) + EPS)
        o_ref[...] = (xv * r * g_ref[...]).astype(bf16)

    return pl.pallas_call(
        body, name=name, grid=(s // ts,),
        in_specs=[pl.BlockSpec((ts, d), lambda i: (i, col)), pl.BlockSpec((1, d), lambda i: (0, 0))],
        out_specs=pl.BlockSpec((ts, d), lambda i: (i, 0)),
        out_shape=jax.ShapeDtypeStruct((s, d), bf16),
        compiler_params=_cparams(("parallel",)))(x, g)


def _rms_bwd(dy, x, g, *, col=0, res=None, out_dtype=f32, name, ts=ROW_BLOCK):
    s, d = dy.shape
    ts = min(ts, s)
    twin = res is not None

    def body(*refs):
        if twin:
            dy_ref, x_ref, g_ref, res_ref, dx_ref, dxb_ref, dg_ref = refs
        else:
            dy_ref, x_ref, g_ref, dx_ref, dg_ref = refs

        @pl.when(pl.program_id(0) == 0)
        def _():
            dg_ref[...] = jnp.zeros_like(dg_ref)

        xv, dyv = x_ref[...].astype(f32), dy_ref[...].astype(f32)
        r = lax.rsqrt(jnp.mean(xv * xv, axis=-1, keepdims=True) + EPS)
        dyg = dyv * g_ref[...]
        dx = r * dyg - xv * (r * r * r) * jnp.mean(xv * dyg, axis=-1, keepdims=True)
        if twin:
            dx = dx + res_ref[...]
            dxb_ref[...] = dx.astype(bf16)
        dx_ref[...] = dx.astype(out_dtype)
        dg_ref[...] += jnp.sum(dyv * xv * r, axis=0, keepdims=True)

    row, vec = pl.BlockSpec((ts, d), lambda i: (i, 0)), pl.BlockSpec((1, d), lambda i: (0, 0))
    in_specs, args = [row, pl.BlockSpec((ts, d), lambda i: (i, col)), vec], [dy, x, g]
    out_specs, out_shape = [row], [jax.ShapeDtypeStruct((s, d), out_dtype)]
    if twin:
        in_specs.append(row)
        args.append(res)
        out_specs.append(row)
        out_shape.append(jax.ShapeDtypeStruct((s, d), bf16))
    return pl.pallas_call(
        body, name=name, grid=(s // ts,), in_specs=in_specs, out_specs=out_specs + [vec],
        out_shape=out_shape + [jax.ShapeDtypeStruct((1, d), f32)],
        compiler_params=_cparams(("arbitrary",)))(*args)


def _swap_halves(z):
    lane = lax.broadcasted_iota(jnp.int32, z.shape, 1)
    return jnp.where(lane < NOPE + ROPE // 2, pltpu.roll(z, LANES - ROPE // 2, axis=1), pltpu.roll(z, ROPE // 2, axis=1))


def _rope_tables(s):
    inv_freq = 1.0 / (ROPE_THETA ** (jnp.arange(0, ROPE, 2, dtype=f32) / ROPE))
    ang = jnp.arange(s, dtype=f32)[:, None] * inv_freq[None, :]
    cos, sin = jnp.cos(ang), jnp.sin(ang)
    one, zero = jnp.ones((s, NOPE), f32), jnp.zeros((s, NOPE), f32)
    pad1, pad0 = jnp.ones((s, HEAD_PAD - QK), f32), jnp.zeros((s, HEAD_PAD - QK), f32)
    return jnp.concatenate([one, cos, cos, pad1], 1), jnp.concatenate([zero, -sin, sin, pad0], 1)


def _qk_prep_fwd(q_raw, kv_raw, proj, qg, kg, cos_t, sin_t, *, name, ts=2 * ROW_BLOCK):
    s = q_raw.shape[0]
    ts = min(ts, s)
    rope_blk = (MIX_IN_PAD - HEAD_PAD) // HEAD_PAD

    def body(q_ref, kv_ref, kr_ref, qg_ref, kg_ref, c_ref, s_ref, qo_ref, ko_ref, vo_ref):
        lane = lax.broadcasted_iota(jnp.int32, (ts, HEAD_PAD), 1)
        cosv, sinv = c_ref[...], s_ref[...]

        def norm_rope(z, gain):
            r = lax.rsqrt(jnp.sum(z * z, axis=-1, keepdims=True) * (1.0 / QK) + EPS)
            zn = z * r * gain
            return zn * cosv + _swap_halves(zn) * sinv

        kvv = kv_ref[...]
        qo_ref[...] = (norm_rope(q_ref[...], qg_ref[...]) * _Q_FOLD).astype(bf16)
        ko_ref[...] = norm_rope(jnp.where(lane < NOPE, kvv, kr_ref[...]), kg_ref[...]).astype(bf16)
        vo_ref[...] = jnp.where(lane >= NOPE, kvv, 0.0).astype(bf16)

    head = pl.BlockSpec((ts, HEAD_PAD), lambda i, h: (i, h))
    row = pl.BlockSpec((ts, HEAD_PAD), lambda i, h: (i, 0))
    vec = pl.BlockSpec((1, HEAD_PAD), lambda i, h: (0, 0))
    out = jax.ShapeDtypeStruct((s, HEADS * HEAD_PAD), bf16)
    return pl.pallas_call(
        body, name=name, grid=(s // ts, HEADS),
        in_specs=[head, head, pl.BlockSpec((ts, HEAD_PAD), lambda i, h: (i, rope_blk)), vec, vec, row, row],
        out_specs=[head, head, head], out_shape=[out, out, out],
        compiler_params=_cparams(("parallel", "parallel")))(q_raw, kv_raw, proj, qg, kg, cos_t, sin_t)


def _qk_prep_bwd(dq, dk, dv, q_raw, kv_raw, proj, qg, kg, cos_t, sin_t, *, name, ts=2 * ROW_BLOCK):
    s = q_raw.shape[0]
    ts = min(ts, s)
    rope_blk = (MIX_IN_PAD - HEAD_PAD) // HEAD_PAD

    def body(dq_ref, dk_ref, dv_ref, q_ref, kv_ref, kr_ref, qg_ref, kg_ref, c_ref, s_ref,
             dqr_ref, dkvr_ref, dkr_ref, dqg_ref, dkg_ref):
        i, h = pl.program_id(0), pl.program_id(1)
        lane = lax.broadcasted_iota(jnp.int32, (ts, HEAD_PAD), 1)
        is_rope = (lane >= NOPE) & (lane < QK)
        cosv, sinv = c_ref[...], s_ref[...]

        @pl.when((i == 0) & (h == 0))
        def _():
            dqg_ref[...] = jnp.zeros_like(dqg_ref)
            dkg_ref[...] = jnp.zeros_like(dkg_ref)

        @pl.when(h == 0)
        def _():
            dkr_ref[...] = jnp.zeros_like(dkr_ref)

        def back(dout, z, gain):
            dzn = dout * cosv + jnp.where(is_rope, _swap_halves(dout * sinv), 0.0)
            r = lax.rsqrt(jnp.sum(z * z, axis=-1, keepdims=True) * (1.0 / QK) + EPS)
            dzg = dzn * gain
            dz = r * dzg - z * (r * r * r) * (jnp.sum(z * dzg, axis=-1, keepdims=True) * (1.0 / QK))
            return dz, jnp.sum(dzn * z * r, axis=0, keepdims=True)

        dqz, dqg = back(dq_ref[...].astype(f32), q_ref[...], qg_ref[...])
        dqr_ref[...] = dqz.astype(bf16)
        dqg_ref[...] += dqg
        kvv = kv_ref[...]
        dkz, dkg = back(dk_ref[...].astype(f32), jnp.where(lane < NOPE, kvv, kr_ref[...]), kg_ref[...])
        dkg_ref[...] += dkg
        dkvr_ref[...] = jnp.where(lane < NOPE, dkz, dv_ref[...].astype(f32)).astype(bf16)
        dkr_ref[...] += jnp.where(is_rope, dkz, 0.0)

    head = pl.BlockSpec((ts, HEAD_PAD), lambda i, h: (i, h))
    row = pl.BlockSpec((ts, HEAD_PAD), lambda i, h: (i, 0))
    vec = pl.BlockSpec((1, HEAD_PAD), lambda i, h: (0, 0))
    wide = jax.ShapeDtypeStruct((s, HEADS * HEAD_PAD), bf16)
    return pl.pallas_call(
        body, name=name, grid=(s // ts, HEADS),
        in_specs=[head, head, head, head, head, pl.BlockSpec((ts, HEAD_PAD), lambda i, h: (i, rope_blk)), vec, vec, row, row],
        out_specs=[head, head, row, vec, vec],
        out_shape=[wide, wide, jax.ShapeDtypeStruct((s, HEAD_PAD), f32), jax.ShapeDtypeStruct((1, HEAD_PAD), f32),
                   jax.ShapeDtypeStruct((1, HEAD_PAD), f32)],
        compiler_params=_cparams(("arbitrary", "arbitrary")))(dq, dk, dv, q_raw, kv_raw, proj, qg, kg, cos_t, sin_t)


_NT = (((1,), (1,)), ((), ()))
_SCALE = QK ** -0.5
_LOG2E = math.log2(math.e)
_Q_FOLD = _SCALE * _LOG2E
FLASH_TILE = 1024


def _flash_fwd(q, k, v, *, name, tq=FLASH_TILE):
    s = q.shape[0]
    tq = min(tq, s)

    def body(q_ref, k_ref, v_ref, o_ref, lse_ref):
        i = pl.program_id(1)
        qv = q_ref[...]

        def step(j, carry, masked):
            m, l, acc = carry
            st = pl.multiple_of(j * tq, tq)
            kj, vj = k_ref[pl.ds(st, tq), :], v_ref[pl.ds(st, tq), :]
            sc = lax.dot_general(qv, kj, _NT, preferred_element_type=f32)
            if masked:
                rr = lax.broadcasted_iota(jnp.int32, (tq, tq), 0)
                cc = lax.broadcasted_iota(jnp.int32, (tq, tq), 1)
                sc = jnp.where(cc <= rr, sc, NEG)
            m_new = jnp.maximum(m, jnp.max(sc, axis=-1, keepdims=True))
            p = jnp.exp2(sc - m_new)
            alpha = jnp.exp2(m - m_new)
            l = alpha * l + jnp.sum(p, axis=-1, keepdims=True)
            acc = alpha * acc + jnp.dot(p.astype(bf16), vj, preferred_element_type=f32)
            return m_new, l, acc

        init = (jnp.full((tq, 1), NEG, f32), jnp.zeros((tq, 1), f32), jnp.zeros((tq, HEAD_PAD), f32))
        carry = lax.fori_loop(0, i, lambda j, c: step(j, c, False), init)
        m, l, acc = step(i, carry, True)
        o_ref[...] = (acc / l).astype(bf16)
        lse_ref[0] = m + jnp.log2(l)

    blk = pl.BlockSpec((tq, HEAD_PAD), lambda h, i: (i, h))
    full = pl.BlockSpec((s, HEAD_PAD), lambda h, i: (0, h))
    return pl.pallas_call(
        body, name=name, grid=(HEADS, s // tq), in_specs=[blk, full, full],
        out_specs=[blk, pl.BlockSpec((1, tq, 1), lambda h, i: (h, i, 0))],
        out_shape=[jax.ShapeDtypeStruct((s, HEADS * HEAD_PAD), bf16), jax.ShapeDtypeStruct((HEADS, s, 1), f32)],
        compiler_params=_cparams(("parallel", "arbitrary")))(q, k, v)


def _flash_bwd_dq(q, k, v, o, do, lse, *, name, tq=FLASH_TILE):
    s = q.shape[0]
    tq = min(tq, s)

    def body(q_ref, k_ref, v_ref, o_ref, do_ref, lse_ref, dq_ref, dl_ref):
        i = pl.program_id(1)
        qv = q_ref[...]
        dov = do_ref[...].astype(f32)
        delta = jnp.sum(dov * o_ref[...].astype(f32), axis=-1, keepdims=True)
        dob = dov.astype(bf16)
        lsev = lse_ref[0]

        def step(j, acc, masked):
            st = pl.multiple_of(j * tq, tq)
            kj, vj = k_ref[pl.ds(st, tq), :], v_ref[pl.ds(st, tq), :]
            sc = lax.dot_general(qv, kj, _NT, preferred_element_type=f32)
            p = jnp.exp2(sc - lsev)
            if masked:
                rr = lax.broadcasted_iota(jnp.int32, (tq, tq), 0)
                cc = lax.broadcasted_iota(jnp.int32, (tq, tq), 1)
                p = jnp.where(cc <= rr, p, 0.0)
            dp = lax.dot_general(dob, vj, _NT, preferred_element_type=f32)
            ds = p * (dp - delta)
            return acc + jnp.dot(ds.astype(bf16), kj, preferred_element_type=f32)

        acc = lax.fori_loop(0, i, lambda j, c: step(j, c, False), jnp.zeros((tq, HEAD_PAD), f32))
        dq_ref[...] = step(i, acc, True) * _SCALE
        dl_ref[0] = delta

    blk = pl.BlockSpec((tq, HEAD_PAD), lambda h, i: (i, h))
    full = pl.BlockSpec((s, HEAD_PAD), lambda h, i: (0, h))
    col = pl.BlockSpec((1, tq, 1), lambda h, i: (h, i, 0))
    return pl.pallas_call(
        body, name=name, grid=(HEADS, s // tq), in_specs=[blk, full, full, blk, blk, col],
        out_specs=[blk, col],
        out_shape=[jax.ShapeDtypeStruct((s, HEADS * HEAD_PAD), f32), jax.ShapeDtypeStruct((HEADS, s, 1), f32)],
        compiler_params=_cparams(("parallel", "arbitrary")))(q, k, v, o, do, lse)


def _flash_bwd_dkv(q, k, v, do, lse_row, delta_row, *, name, tk=FLASH_TILE):
    s = q.shape[0]
    tk = min(tk, s)
    nblk = s // tk

    def body(q_ref, k_ref, v_ref, do_ref, lse_ref, dl_ref, dk_ref, dv_ref):
        j = pl.program_id(1)
        kv_, vv = k_ref[...], v_ref[...]

        def step(i, carry, masked):
            dk, dv = carry
            st = pl.multiple_of(i * tk, tk)
            qi = q_ref[pl.ds(st, tk), :]
            doi = do_ref[pl.ds(st, tk), :].astype(bf16)
            lse_i = lse_ref[0, :, pl.ds(st, tk)]
            dl_i = dl_ref[0, :, pl.ds(st, tk)]
            st_ = lax.dot_general(kv_, qi, _NT, preferred_element_type=f32)
            pt = jnp.exp2(st_ - lse_i)
            if masked:
                kk = lax.broadcasted_iota(jnp.int32, (tk, tk), 0)
                qq = lax.broadcasted_iota(jnp.int32, (tk, tk), 1)
                pt = jnp.where(kk <= qq, pt, 0.0)
            dv = dv + jnp.dot(pt.astype(bf16), doi, preferred_element_type=f32)
            dpt = lax.dot_general(vv, doi, _NT, preferred_element_type=f32)
            dst = pt * (dpt - dl_i)
            dk = dk + jnp.dot(dst.astype(bf16), qi, preferred_element_type=f32)
            return dk, dv

        zero = jnp.zeros((tk, HEAD_PAD), f32)
        carry = step(j, (zero, zero), True)
        dk, dv = lax.fori_loop(j + 1, nblk, lambda i, c: step(i, c, False), carry)
        dk_ref[...] = dk * (1.0 / _LOG2E)
        dv_ref[...] = dv

    blk = pl.BlockSpec((tk, HEAD_PAD), lambda h, j: (j, h))
    full = pl.BlockSpec((s, HEAD_PAD), lambda h, j: (0, h))
    rowv = pl.BlockSpec((1, 1, s), lambda h, j: (h, 0, 0))
    out = jax.ShapeDtypeStruct((s, HEADS * HEAD_PAD), f32)
    return pl.pallas_call(
        body, name=name, grid=(HEADS, nblk), in_specs=[full, blk, blk, full, rowv, rowv],
        out_specs=[blk, blk], out_shape=[out, out],
        compiler_params=_cparams(("parallel", "arbitrary")))(q, k, v, do, lse_row, delta_row)


SUBLANES = 8


def _shift_down(x, d):
    r = pltpu.roll(x, d, axis=0)
    t = lax.broadcasted_iota(jnp.int32, (SUBLANES, x.shape[1]), 0)
    head = jnp.where(t < d, 0.0, r[:SUBLANES])
    return head if x.shape[0] == SUBLANES else jnp.concatenate([head, r[SUBLANES:]], axis=0)


def _shift_up(x, d):
    s = x.shape[0]
    r = pltpu.roll(x, s - d, axis=0)
    t = lax.broadcasted_iota(jnp.int32, (SUBLANES, x.shape[1]), 0)
    tail = jnp.where(t >= SUBLANES - d, 0.0, r[s - SUBLANES:])
    return tail if s == SUBLANES else jnp.concatenate([r[:s - SUBLANES], tail], axis=0)


def _taps(w_ref):
    return w_ref[0:1, :], w_ref[1:2, :], w_ref[2:3, :]


def _conv3(u, w):
    u1, u2 = _shift_down(u, 1), _shift_down(u, 2)
    return w[0] * u2 + w[1] * u1 + w[2] * u, (u1, u2)


def _ref_shift_down(ref, d):
    s = ref.shape[0]
    return jnp.concatenate([_shift_down(ref[0:SUBLANES, :], d), ref[pl.ds(SUBLANES - d, s - SUBLANES), :]], axis=0)


def _conv3_ref(ref, w):
    return w[0] * _ref_shift_down(ref, 2) + w[1] * _ref_shift_down(ref, 1) + w[2] * ref[...]


def _conv3_t(g, w):
    return w[2] * g + w[1] * _shift_up(g, 1) + w[0] * _shift_up(g, 2)


def _conv3_dw(dw_ref, g, u, shifted):
    dw_ref[0:1, :] = jnp.sum(g * shifted[1], axis=0, keepdims=True)
    dw_ref[1:2, :] = jnp.sum(g * shifted[0], axis=0, keepdims=True)
    dw_ref[2:3, :] = jnp.sum(g * u, axis=0, keepdims=True)


_GB, _GC, _CI = 512 // LANES, 1024 // LANES, 1536 // LANES


def _sconv_fwd(proj, w, *, name):
    s = proj.shape[0]

    def body(gb_ref, gc_ref, ci_ref, w_ref, o_ref):
        o_ref[...] = (gb_ref[...] * _conv3(gc_ref[...] * ci_ref[...], _taps(w_ref))[0]).astype(bf16)

    col = lambda off: pl.BlockSpec((s, LANES), lambda j: (0, off + j))
    return pl.pallas_call(
        body, name=name, grid=(CONV_CH // LANES,),
        in_specs=[col(_GB), col(_GC), col(_CI), pl.BlockSpec((3, LANES), lambda j: (0, j))],
        out_specs=pl.BlockSpec((s, LANES), lambda j: (0, j)),
        out_shape=jax.ShapeDtypeStruct((s, CONV_CH), bf16),
        compiler_params=_cparams(("parallel",)))(proj, proj, proj, w)


def _sconv_bwd(dmix, proj, w, *, name):
    s = proj.shape[0]

    def body(do_ref, gb_ref, gc_ref, ci_ref, w_ref, dgb_ref, dgc_ref, dci_ref, dw_ref):
        wv, gc, ci, do = _taps(w_ref), gc_ref[...], ci_ref[...], do_ref[...].astype(f32)
        u = gc * ci
        conv, shifted = _conv3(u, wv)
        dgb_ref[...] = (do * conv).astype(bf16)
        dc = do * gb_ref[...]
        du = _conv3_t(dc, wv)
        dgc_ref[...] = (du * ci).astype(bf16)
        dci_ref[...] = (du * gc).astype(bf16)
        _conv3_dw(dw_ref, dc, u, shifted)

    col = lambda off: pl.BlockSpec((s, LANES), lambda j: (0, off + j))
    out = jax.ShapeDtypeStruct((s, CONV_CH), bf16)
    return pl.pallas_call(
        body, name=name, grid=(CONV_CH // LANES,),
        in_specs=[col(HEADS), col(_GB), col(_GC), col(_CI), pl.BlockSpec((3, LANES), lambda j: (0, j))],
        out_specs=[col(0), col(0), col(0), pl.BlockSpec((3, LANES), lambda j: (0, j))],
        out_shape=[out, out, out, jax.ShapeDtypeStruct((3, CONV_CH), f32)],
        compiler_params=_cparams(("parallel",)))(dmix, proj, proj, proj, w)


def _ffn_act_fwd(zg, zv, cwg, cwv, *, name):
    s, f = zg.shape

    def body(zg_ref, zv_ref, wg_ref, wv_ref, o_ref):
        o_ref[...] = (jax.nn.silu(_conv3_ref(zg_ref, _taps(wg_ref))) * _conv3_ref(zv_ref, _taps(wv_ref))).astype(bf16)

    col = pl.BlockSpec((s, LANES), lambda j: (0, j))
    wsp = pl.BlockSpec((3, LANES), lambda j: (0, j))
    return pl.pallas_call(
        body, name=name, grid=(f // LANES,), in_specs=[col, col, wsp, wsp], out_specs=col,
        out_shape=jax.ShapeDtypeStruct((s, f), bf16), compiler_params=_cparams(("parallel",)))(zg, zv, cwg, cwv)


def _ffn_act_bwd(da, zg, zv, cwg, cwv, *, name):
    s, f = zg.shape

    def body(da_ref, zg_ref, zv_ref, wg_ref, wv_ref, dzg_ref, dzv_ref, dwg_ref, dwv_ref):
        wg, wv, dav = _taps(wg_ref), _taps(wv_ref), da_ref[...].astype(f32)
        ug, uv = _conv3_ref(zg_ref, wg), _conv3_ref(zv_ref, wv)
        sg = jax.nn.sigmoid(ug)
        dug = dav * uv * (sg * (1.0 + ug * (1.0 - sg)))
        duv = dav * (ug * sg)
        dzg_ref[...] = _conv3_t(dug, wg).astype(bf16)
        dzv_ref[...] = _conv3_t(duv, wv).astype(bf16)
        _conv3_dw(dwg_ref, dug, zg_ref[...], (_ref_shift_down(zg_ref, 1), _ref_shift_down(zg_ref, 2)))
        _conv3_dw(dwv_ref, duv, zv_ref[...], (_ref_shift_down(zv_ref, 1), _ref_shift_down(zv_ref, 2)))

    col = pl.BlockSpec((s, LANES), lambda j: (0, j))
    wsp = pl.BlockSpec((3, LANES), lambda j: (0, j))
    act, wsh = jax.ShapeDtypeStruct((s, f), bf16), jax.ShapeDtypeStruct((3, f), f32)
    return pl.pallas_call(
        body, name=name, grid=(f // LANES,), in_specs=[col, col, col, wsp, wsp], out_specs=[col, col, wsp, wsp],
        out_shape=[act, act, wsh, wsh], compiler_params=_cparams(("parallel",)))(da, zg, zv, cwg, cwv)


def _expand_mat():
    return jnp.asarray(np.kron(np.eye(STATE, dtype=np.float32), np.ones((1, GROUP), np.float32)))


def _disc_fn(lr, li, ls, br, bi, e):
    dt = jnp.exp(ls)
    mag = jnp.exp(lr * dt)
    ar, ai = mag * jnp.cos(li * dt), mag * jnp.sin(li * dt)
    nr, ni = ar - 1.0, ai
    den = lr * lr + li * li
    zr, zi = (nr * lr + ni * li) / den, (ni * lr - nr * li) / den
    zrr = jnp.dot(zr, e, precision=lax.Precision.HIGHEST, preferred_element_type=f32)
    zir = jnp.dot(zi, e, precision=lax.Precision.HIGHEST, preferred_element_type=f32)
    return ar, ai, zrr * br - zir * bi, zrr * bi + zir * br


def _disc_fwd(lr, li, ls, br, bi, *, name):
    def body(lr_ref, li_ref, ls_ref, br_ref, bi_ref, e_ref, ar_ref, ai_ref, bbr_ref, bbi_ref):
        ar, ai, bbr, bbi = _disc_fn(lr_ref[...], li_ref[...], ls_ref[...], br_ref[...], bi_ref[...], e_ref[...])
        ar_ref[...], ai_ref[...], bbr_ref[...], bbi_ref[...] = ar, ai, bbr, bbi

    sq, wide = jax.ShapeDtypeStruct((GROUPS, STATE), f32), jax.ShapeDtypeStruct((GROUPS, STATE * GROUP), f32)
    return pl.pallas_call(body, name=name, out_shape=[sq, sq, wide, wide],
                          compiler_params=_cparams())(lr, li, ls, br, bi, _expand_mat())


def _disc_bwd(lr, li, ls, br, bi, dar, dai, dbbr, dbbi, *, name):
    def body(lr_ref, li_ref, ls_ref, br_ref, bi_ref, e_ref, dar_ref, dai_ref, dbbr_ref, dbbi_ref,
             dlr_ref, dli_ref, dls_ref, dbr_ref, dbi_ref):
        ev = e_ref[...]
        _, vjp = jax.vjp(lambda a, b, c, d_, e_: _disc_fn(a, b, c, d_, e_, ev),
                         lr_ref[...], li_ref[...], ls_ref[...], br_ref[...], bi_ref[...])
        dlr, dli, dls, dbr, dbi = vjp((dar_ref[...], dai_ref[...], dbbr_ref[...], dbbi_ref[...]))
        dlr_ref[...], dli_ref[...], dls_ref[...], dbr_ref[...], dbi_ref[...] = dlr, dli, dls, dbr, dbi

    sq, wide = jax.ShapeDtypeStruct((GROUPS, STATE), f32), jax.ShapeDtypeStruct((GROUPS, STATE * GROUP), f32)
    return pl.pallas_call(body, name=name, out_shape=[sq, sq, jax.ShapeDtypeStruct((GROUPS, 1), f32), wide, wide],
                          compiler_params=_cparams())(lr, li, ls, br, bi, _expand_mat(), dar, dai, dbbr, dbbi)


SCAN_TILE = 32
SCAN_PAIRS = 4


def _tile_shift(v, d, reverse):
    if d % 8:
        return _shift_up(v, d) if reverse else _shift_down(v, d)
    z = jnp.zeros((d, v.shape[1]), v.dtype)
    return jnp.concatenate([v[d:], z], axis=0) if reverse else jnp.concatenate([z, v[:v.shape[0] - d]], axis=0)


def _tile_scan(r, i, pows, reverse):
    d = 1
    for br, bi in pows:
        rs, is_ = _tile_shift(r, d, reverse), _tile_shift(i, d, reverse)
        r, i = r + br * rs - bi * is_, i + br * is_ + bi * rs
        d *= 2
    return r, i


def _scan_setup(ar, ai, reverse):
    if reverse:
        ai = -ai
    pows, br, bi, d = [], ar, ai, 1
    while d < SCAN_TILE:
        pows.append((br, bi))
        br, bi, d = br * br - bi * bi, 2.0 * br * bi, 2 * d
    row = lax.broadcasted_iota(jnp.int32, (SCAN_TILE, LANES), 0)
    hit = row == (SCAN_TILE - 1 if reverse else 0)
    pr, pi = _tile_scan(jnp.where(hit, ar, 0.0), jnp.where(hit, ai, 0.0), pows, reverse)
    return pows, pr, pi


def _carry_in(r, i, pr, pi, cr, ci):
    crb, cib = jnp.broadcast_to(cr, r.shape), jnp.broadcast_to(ci, i.shape)
    return r + pr * crb - pi * cib, i + pr * cib + pi * crb


def _pair_cols(q):
    return slice(q * PAIR_LANES, q * PAIR_LANES + LANES), slice(q * PAIR_LANES + LANES, (q + 1) * PAIR_LANES)


_SCAN_W = SCAN_PAIRS * PAIR_LANES


def _scan_specs(s, w):
    per = w.shape[2] // _SCAN_W
    src = pl.BlockSpec((s, LANES), lambda g: (0, g // per))
    mat = pl.BlockSpec((1, LANES, _SCAN_W), lambda g: (g // per, 0, g % per))
    col = pl.BlockSpec((s, _SCAN_W), lambda g: (0, g))
    vec = pl.BlockSpec((SCAN_PAIRS, 1, LANES), lambda g: (g, 0, 0))
    return src, mat, col, vec, (w.shape[0] * per,)


def _scan_fwd(u, wb, ar, ai, *, name):
    s = u.shape[0]
    nt = s // SCAN_TILE

    def body(u_ref, w_ref, ar_ref, ai_ref, x_ref):
        setups = [_scan_setup(ar_ref[q], ai_ref[q], False) for q in range(SCAN_PAIRS)]
        wv = w_ref[0]

        def tile_rows(k):
            return pl.ds(pl.multiple_of(k * SCAN_TILE, SCAN_TILE), SCAN_TILE)

        def tile_in(k):
            return jnp.dot(u_ref[tile_rows(k), :].astype(bf16), wv, preferred_element_type=f32)

        def step(k, carry):
            rows, bu = tile_rows(k), carry[-1]
            ahead = tile_in(jnp.minimum(k + 1, nt - 1))
            out = []
            for q, (pows, pr, pi) in enumerate(setups):
                rc, ic = _pair_cols(q)
                r, i = _tile_scan(bu[:, rc], bu[:, ic], pows, False)
                r, i = _carry_in(r, i, pr, pi, carry[2 * q], carry[2 * q + 1])
                x_ref[rows, rc] = r.astype(bf16)
                x_ref[rows, ic] = i.astype(bf16)
                out += [r[SCAN_TILE - 1:SCAN_TILE, :], i[SCAN_TILE - 1:SCAN_TILE, :]]
            return tuple(out) + (ahead,)

        lax.fori_loop(0, nt, step, tuple(jnp.zeros((1, LANES), f32) for _ in range(2 * SCAN_PAIRS)) + (tile_in(0),))

    src, mat, col, vec, grid = _scan_specs(s, wb)
    return pl.pallas_call(body, name=name, grid=grid, in_specs=[src, mat, vec, vec], out_specs=col,
                          out_shape=jax.ShapeDtypeStruct((s, wb.shape[0] * wb.shape[2]), bf16),
                          compiler_params=_cparams(("parallel",)))(u, wb, ar, ai)


def _scan_bwd(dy, cbt, x, ar, ai, *, name):
    s = dy.shape[0]
    nt = s // SCAN_TILE

    def fold(v):
        out = v[0:8]
        for r in range(8, SCAN_TILE, 8):
            out = out + v[r:r + 8]
        return out

    def body(dy_ref, w_ref, x_ref, ar_ref, ai_ref, g_ref, dar_ref, dai_ref):
        setups = [_scan_setup(ar_ref[q], ai_ref[q], True) for q in range(SCAN_PAIRS)]
        row = lax.broadcasted_iota(jnp.int32, (SCAN_TILE, LANES), 0)
        wv = w_ref[0]

        def tile_in(k):
            return jnp.dot(dy_ref[pl.ds(pl.multiple_of(k * SCAN_TILE, SCAN_TILE), SCAN_TILE), :], wv, preferred_element_type=f32)

        def step(kk, carry):
            k = nt - 1 - kk
            start = pl.multiple_of(k * SCAN_TILE, SCAN_TILE)
            rows = pl.ds(start, SCAN_TILE)
            prev16 = pl.ds(pl.multiple_of(jnp.maximum(start - 16, 0), 16), 16)
            dx = carry[-1]
            ahead = tile_in(jnp.maximum(k - 1, 0))

            def before(cols):
                first = jnp.where(k > 0, x_ref[prev16, cols][15:16, :].astype(f32), 0.0)
                return jnp.where(row == 0, first, pltpu.roll(x_ref[rows, cols].astype(f32), 1, axis=0))

            out = []
            for q, (pows, pr, pi) in enumerate(setups):
                rc, ic = _pair_cols(q)
                cr, ci, acc_r, acc_i = carry[4 * q:4 * q + 4]
                gr, gi = _tile_scan(dx[:, rc], dx[:, ic], pows, True)
                gr, gi = _carry_in(gr, gi, pr, pi, cr, ci)
                g_ref[rows, rc] = gr.astype(bf16)
                g_ref[rows, ic] = gi.astype(bf16)
                xr, xi = before(rc), before(ic)
                out += [gr[0:1, :], gi[0:1, :], acc_r + fold(gr * xr + gi * xi), acc_i + fold(gi * xr - gr * xi)]
            return tuple(out) + (ahead,)

        init = (jnp.zeros((1, LANES), f32), jnp.zeros((1, LANES), f32), jnp.zeros((8, LANES), f32), jnp.zeros((8, LANES), f32))
        res = lax.fori_loop(0, nt, step, init * SCAN_PAIRS + (tile_in(nt - 1),))
        for q in range(SCAN_PAIRS):
            dar_ref[q] = jnp.sum(res[4 * q + 2], axis=0, keepdims=True)
            dai_ref[q] = jnp.sum(res[4 * q + 3], axis=0, keepdims=True)

    src, mat, col, vec, grid = _scan_specs(s, cbt)
    vsh = jax.ShapeDtypeStruct((GROUPS // 2, 1, LANES), f32)
    return pl.pallas_call(body, name=name, grid=grid, in_specs=[src, mat, col, vec, vec],
                          out_specs=[col, vec, vec], out_shape=[jax.ShapeDtypeStruct(x.shape, bf16), vsh, vsh],
                          compiler_params=_cparams(("parallel",)))(dy, cbt, x, ar, ai)


_GELU_C = math.sqrt(2.0 / math.pi)


def _gelu_fwd(y, u, dsk, *, name, ts=ROW_BLOCK):
    s, d = y.shape
    ts = min(ts, s)

    def body(y_ref, u_ref, d_ref, o_ref):
        o_ref[...] = jax.nn.gelu(y_ref[...] + d_ref[...] * u_ref[...]).astype(bf16)

    row, vec = pl.BlockSpec((ts, d), lambda i: (i, 0)), pl.BlockSpec((1, d), lambda i: (0, 0))
    return pl.pallas_call(body, name=name, grid=(s // ts,), in_specs=[row, row, vec], out_specs=row,
                          out_shape=jax.ShapeDtypeStruct((s, d), bf16), compiler_params=_cparams(("parallel",)))(y, u, dsk)


def _gelu_bwd(dg, y, u, dsk, *, name, ts=ROW_BLOCK):
    s, d = y.shape
    ts = min(ts, s)

    def body(dg_ref, y_ref, u_ref, d_ref, dy_ref, du_ref, dd_ref):
        @pl.when(pl.program_id(0) == 0)
        def _():
            dd_ref[...] = jnp.zeros_like(dd_ref)

        uv, dv = u_ref[...], d_ref[...]
        z = y_ref[...] + dv * uv
        th = jnp.tanh(_GELU_C * (z + 0.044715 * z * z * z))
        dz = dg_ref[...] * (0.5 * (1.0 + th) + 0.5 * z * (1.0 - th * th) * _GELU_C * (1.0 + 3 * 0.044715 * z * z))
        dy_ref[...] = dz.astype(bf16)
        du_ref[...] = dz * dv
        dd_ref[...] += jnp.sum(dz * uv, axis=0, keepdims=True)

    row, vec = pl.BlockSpec((ts, d), lambda i: (i, 0)), pl.BlockSpec((1, d), lambda i: (0, 0))
    return pl.pallas_call(
        body, name=name, grid=(s // ts,), in_specs=[row, row, row, vec], out_specs=[row, row, vec],
        out_shape=[jax.ShapeDtypeStruct((s, d), bf16), jax.ShapeDtypeStruct((s, d), f32), jax.ShapeDtypeStruct((1, d), f32)],
        compiler_params=_cparams(("arbitrary",)))(dg, y, u, dsk)


def _glu_fwd(x, a, b, *, name, ts=ROW_BLOCK):
    s, d = x.shape
    ts = min(ts, s)

    def body(x_ref, a_ref, b_ref, o_ref):
        o_ref[...] = x_ref[...] + a_ref[...] * jax.nn.sigmoid(b_ref[...])

    row = pl.BlockSpec((ts, d), lambda i: (i, 0))
    return pl.pallas_call(body, name=name, grid=(s // ts,), in_specs=[row, row, row], out_specs=row,
                          out_shape=jax.ShapeDtypeStruct((s, d), f32), compiler_params=_cparams(("parallel",)))(x, a, b)


def _glu_bwd(dx, a, b, *, name, ts=ROW_BLOCK):
    s, d = dx.shape
    ts = min(ts, s)

    def body(dx_ref, a_ref, b_ref, da_ref, db_ref):
        sg = jax.nn.sigmoid(b_ref[...])
        dxv = dx_ref[...]
        da_ref[...] = (dxv * sg).astype(bf16)
        db_ref[...] = (dxv * a_ref[...] * sg * (1.0 - sg)).astype(bf16)

    row = pl.BlockSpec((ts, d), lambda i: (i, 0))
    out = jax.ShapeDtypeStruct((s, d), bf16)
    return pl.pallas_call(body, name=name, grid=(s // ts,), in_specs=[row, row, row], out_specs=[row, row],
                          out_shape=[out, out], compiler_params=_cparams(("parallel",)))(dx, a, b)


def _loss_head(y, target, *, name, ts=ROW_BLOCK):
    s, d = y.shape
    ts = min(ts, s)

    def body(y_ref, t_ref, dy_ref, dyb_ref, l_ref):
        @pl.when(pl.program_id(0) == 0)
        def _():
            l_ref[...] = jnp.zeros_like(l_ref)

        e = y_ref[...] - t_ref[...]
        dy = e * (1.0 / d)
        dy_ref[...] = dy
        dyb_ref[...] = dy.astype(bf16)
        l_ref[...] += 0.5 * jnp.sum(jnp.mean(e * e, axis=-1, keepdims=True))

    row = pl.BlockSpec((ts, d), lambda i: (i, 0))
    return pl.pallas_call(
        body, name=name, grid=(s // ts,), in_specs=[row, row],
        out_specs=[row, row, pl.BlockSpec((8, LANES), lambda i: (0, 0))],
        out_shape=[jax.ShapeDtypeStruct((s, d), f32), jax.ShapeDtypeStruct((s, d), bf16), jax.ShapeDtypeStruct((8, LANES), f32)],
        compiler_params=_cparams(("arbitrary",)))(y, target)


def _adamw(w, g, m, v, *, name, tr=128):
    r, c = w.shape

    def body(w_ref, g_ref, m_ref, v_ref, d_ref, mo_ref, vo_ref):
        gv = g_ref[...]
        mn = ADAM_B1 * m_ref[...] + (1.0 - ADAM_B1) * gv
        vn = ADAM_B2 * v_ref[...] + (1.0 - ADAM_B2) * (gv * gv)
        m_hat = mn / (1.0 - ADAM_B1 ** ADAM_STEP)
        v_hat = vn / (1.0 - ADAM_B2 ** ADAM_STEP)
        d_ref[...] = -ADAM_LR * (m_hat / (jnp.sqrt(v_hat) + ADAM_EPS) + ADAM_WD * w_ref[...])
        mo_ref[...] = mn
        vo_ref[...] = vn

    row = pl.BlockSpec((tr, c), lambda i: (i, 0))
    out = jax.ShapeDtypeStruct((r, c), f32)
    return pl.pallas_call(body, name=name, grid=(r // tr,), in_specs=[row] * 4, out_specs=[row] * 3,
                          out_shape=[out, out, out], compiler_params=_cparams(("parallel",)))(w, g, m, v)


def _sum_slabs(land, *, name, tr=128):
    n, r, c = land.shape

    def body(l_ref, o_ref):
        acc = l_ref[0].astype(f32)
        for i in range(1, n):
            acc = acc + l_ref[i].astype(f32)
        o_ref[...] = acc

    return pl.pallas_call(body, name=name, grid=(r // tr,), in_specs=[pl.BlockSpec((n, tr, c), lambda i: (0, i, 0))],
                          out_specs=pl.BlockSpec((tr, c), lambda i: (i, 0)), out_shape=jax.ShapeDtypeStruct((r, c), f32),
                          compiler_params=_cparams(("parallel",)))(land)


def _pair_sum(g, theirs, *, name):
    n, r, c = theirs.shape
    tr = _row_tile(r, 1024)

    def body(c_ref, g_ref, t_ref, o_ref):
        o_ref[...] = (g_ref[...].astype(f32) + t_ref[...].astype(f32)).astype(bf16)

    blk = pl.BlockSpec((1, tr, c), lambda j, i, c_ref: (j, i, 0))
    mine = pl.BlockSpec((1, tr, c), lambda j, i, c_ref: (2 * j + c_ref[0], i, 0))
    return pl.pallas_call(
        body, name=name,
        grid_spec=pltpu.PrefetchScalarGridSpec(num_scalar_prefetch=1, grid=(n, r // tr), in_specs=[mine, blk], out_specs=blk),
        out_shape=jax.ShapeDtypeStruct(theirs.shape, bf16),
        compiler_params=_cparams(("parallel", "parallel")))(lax.axis_index("c").astype(jnp.int32).reshape(1), g, theirs)


_MESH = pl.DeviceIdType.MESH
_HBM = pl.BlockSpec(memory_space=pltpu.HBM)
N_CHIP = N_DEV // 2


def _position():
    return lax.axis_index("x"), lax.axis_index("y"), lax.axis_index("c")


def _gather8(x, *, name):
    half = x.shape[0] // 2

    def body(x_ref, o_ref, send_sems, recv_sems, local_sem):
        xx, yy, cc = _position()
        me, sibling = (xx, yy, cc), (xx, yy, 1 - cc)
        here, xn, yn, dg = (xx, yy), (1 - xx, yy), (xx, 1 - yy), (1 - xx, 1 - yy)
        first, second = pl.ds(0, half), pl.ds(half, half)

        def slab(chip, pc, rows=None):
            ref = o_ref.at[4 * chip[0] + 2 * chip[1] + pc]
            return ref if rows is None else ref.at[rows]

        def copy(k, ref, to, src=None):
            return pltpu.make_async_remote_copy(src_ref=ref if src is None else src, dst_ref=ref, send_sem=send_sems.at[k],
                                                recv_sem=recv_sems.at[k], device_id=to, device_id_type=_MESH)

        mine = pltpu.make_async_copy(x_ref, slab(here, cc), local_sem)
        mine.start()
        sends = [copy(0, slab(here, cc), sibling, src=x_ref), copy(1, slab(here, cc), (*xn, cc), src=x_ref),
                 copy(2, slab(here, cc), (*yn, cc), src=x_ref)]
        for cp in sends:
            cp.start()
        copy(1, slab(xn, cc), me).wait_recv()
        sends += [copy(3, slab(xn, cc, first), (*yn, cc)), copy(5, slab(xn, cc), sibling)]
        copy(2, slab(yn, cc), me).wait_recv()
        sends += [copy(4, slab(yn, cc, second), (*xn, cc)), copy(6, slab(yn, cc), sibling)]
        for cp in sends[3:]:
            cp.start()
        copy(3, slab(dg, cc, first), me).wait_recv()
        copy(4, slab(dg, cc, second), me).wait_recv()
        sends.append(copy(7, slab(dg, cc), sibling))
        sends[-1].start()
        for k, chip in ((0, here), (5, xn), (6, yn), (7, dg)):
            copy(k, slab(chip, 1 - cc), me).wait_recv()
        for cp in sends:
            cp.wait_send()
        mine.wait()

    return pl.pallas_call(
        body, name=name, in_specs=[_HBM], out_specs=_HBM, out_shape=jax.ShapeDtypeStruct((N_DEV,) + x.shape, x.dtype),
        scratch_shapes=[pltpu.SemaphoreType.DMA((N_DEV,)), pltpu.SemaphoreType.DMA((N_DEV,)), pltpu.SemaphoreType.DMA],
    )(x)


def _pair_exchange(g, *, name):
    def body(g_ref, land_ref, send_sems, recv_sems):
        xx, yy, cc = _position()
        copies = []
        for j in range(N_CHIP):
            cp = pltpu.make_async_remote_copy(src_ref=g_ref.at[2 * j + 1 - cc], dst_ref=land_ref.at[j], send_sem=send_sems.at[j],
                                              recv_sem=recv_sems.at[j], device_id=(xx, yy, 1 - cc), device_id_type=_MESH)
            cp.start()
            copies.append(cp)
        for cp in copies:
            cp.wait_recv()
        for cp in copies:
            cp.wait_send()

    sems = pltpu.SemaphoreType.DMA((N_CHIP,))
    return pl.pallas_call(body, name=name, in_specs=[_HBM], out_specs=_HBM,
                          out_shape=jax.ShapeDtypeStruct((N_CHIP,) + g.shape[1:], g.dtype), scratch_shapes=[sems, sems])(g)


def _cross_exchange(p, *, name):
    half = p.shape[1] // 2

    def body(p_ref, o_ref, relay_ref, send_sems, recv_sems, local_sem):
        xx, yy, cc = _position()
        me = (xx, yy, cc)
        xn, yn, dg = (1 - xx, yy), (xx, 1 - yy), (1 - xx, 1 - yy)
        idx = lambda chip: 2 * chip[0] + chip[1]
        mine = idx((xx, yy))
        first, second = pl.ds(0, half), pl.ds(half, half)

        def copy(k, src, dst, to):
            return pltpu.make_async_remote_copy(src_ref=src, dst_ref=dst, send_sem=send_sems.at[k], recv_sem=recv_sems.at[k],
                                                device_id=to, device_id_type=_MESH)

        local = pltpu.make_async_copy(p_ref.at[mine], o_ref.at[mine], local_sem)
        local.start()
        sends = [copy(0, p_ref.at[idx(xn)], o_ref.at[mine], (*xn, cc)),
                 copy(1, p_ref.at[idx(dg)].at[first], relay_ref.at[0], (*xn, cc)),
                 copy(2, p_ref.at[idx(yn)], o_ref.at[mine], (*yn, cc)),
                 copy(3, p_ref.at[idx(dg)].at[second], relay_ref.at[1], (*yn, cc))]
        for cp in sends:
            cp.start()
        copy(1, relay_ref.at[0], relay_ref.at[0], me).wait_recv()
        sends.append(copy(4, relay_ref.at[0], o_ref.at[idx(xn)].at[first], (*yn, cc)))
        sends[-1].start()
        copy(3, relay_ref.at[1], relay_ref.at[1], me).wait_recv()
        sends.append(copy(5, relay_ref.at[1], o_ref.at[idx(yn)].at[second], (*xn, cc)))
        sends[-1].start()
        for k, dst in ((0, o_ref.at[idx(xn)]), (2, o_ref.at[idx(yn)]), (4, o_ref.at[idx(dg)].at[first]),
                       (5, o_ref.at[idx(dg)].at[second])):
            copy(k, dst, dst, me).wait_recv()
        for cp in sends:
            cp.wait_send()
        local.wait()

    sems = pltpu.SemaphoreType.DMA((6,))
    relay = jax.ShapeDtypeStruct((2, half) + p.shape[2:], p.dtype)
    return pl.pallas_call(body, name=name, in_specs=[_HBM], out_specs=[_HBM, _HBM],
                          out_shape=[jax.ShapeDtypeStruct(p.shape, p.dtype), relay],
                          scratch_shapes=[sems, sems, pltpu.SemaphoreType.DMA])(p)[0]


def _all_sum(x, *, name):
    return _sum_slabs(_gather8(x, name=f"gather_{name}"), name=f"sum_{name}", tr=min(128, x.shape[0]))


def _pack_slabs(parts, rows, axis=0):
    lead = parts[0].shape[:axis]
    slabs = [p.reshape(lead + (-1, D)) for p in parts]
    used = sum(sl.shape[axis] for sl in slabs)
    return jnp.concatenate(slabs + [jnp.zeros(lead + (rows - used, D), slabs[0].dtype)], axis=axis)


def _unpack_slabs(slab, shapes):
    lead, out, off = slab.shape[:-2], [], 0
    for shp in shapes:
        n = int(np.prod(shp)) // D
        out.append(slab[..., off:off + n, :].reshape(lead + tuple(shp)))
        off += n
    return out


def _pack_rows(parts, rows):
    flat = jnp.concatenate([p.reshape(-1) for p in parts])
    return jnp.pad(flat, (0, rows * D - flat.shape[0])).reshape(rows, D)


def _unpack_rows(slab, shapes):
    flat, out, off = slab.reshape(-1), [], 0
    for shp in shapes:
        n = int(np.prod(shp))
        out.append(flat[off:off + n].reshape(shp))
        off += n
    return out


def _full_shape(shard, axis):
    return tuple(d * N_DEV if i == axis else d for i, d in enumerate(shard))


def _row(v):
    return v.reshape(1, -1).astype(f32)


def _pad_gain(g):
    return jnp.pad(g.astype(f32), (0, HEAD_PAD - QK)).reshape(1, HEAD_PAD)


def _ffn_fwd(x, p, tag):
    h = _rms_fwd(x, p["norm"], name=f"ffn_norm_{tag}")
    zg = _mm(h, p["wgT"], tb=True, tn=FFN_H, name=f"ffn_up_g_{tag}")
    zv = _mm(h, p["wvT"], tb=True, tn=FFN_H, name=f"ffn_up_v_{tag}")
    a = _ffn_act_fwd(zg, zv, p["cwg"], p["cwv"], name=f"ffn_act_{tag}")
    y = _mm(a, p["wd"], add=x, name=f"ffn_down_{tag}")
    return y, (x, h, zg, zv, a)


def _ffn_bwd(dy, dyb, p, saved, tag):
    x, h, zg, zv, a = saved
    g = {}
    da = _mm(dyb, p["wd"], tb=True, out_dtype=bf16, name=f"ffn_down_dx_{tag}")
    g["wd"] = _mm(a, dyb, ta=True, out_dtype=bf16, name=f"ffn_down_dw_{tag}")
    dzg, dzv, g["cwg"], g["cwv"] = _ffn_act_bwd(da, zg, zv, p["cwg"], p["cwv"], name=f"ffn_act_bwd_{tag}")
    g["wgT"] = _mm(dzg, h, ta=True, out_dtype=bf16, name=f"ffn_up_g_dw_{tag}")
    g["wvT"] = _mm(dzv, h, ta=True, out_dtype=bf16, name=f"ffn_up_v_dw_{tag}")
    dh = _mm(dzg, p["wgT"], name=f"ffn_up_g_dx_{tag}")
    dh = _mm(dzv, p["wvT"], add=dh, name=f"ffn_up_v_dx_{tag}")
    dx, dxb, g["norm"] = _rms_bwd(dh, x, p["norm"], res=dy, name=f"ffn_norm_bwd_{tag}")
    return dx, dxb, g


def _mla_fwd(x, p, tabs, tag):
    cos_t, sin_t = tabs
    h = _rms_fwd(x, p["norm"], name=f"attn_norm_{tag}")
    proj = _mm(h, p["w_inT"], tb=True, name=f"mix_in_{tag}")
    cqn = _rms_fwd(proj, p["cq_norm"], col=0, name=f"cq_norm_{tag}")
    ckvn = _rms_fwd(proj, p["ckv_norm"], col=1, name=f"ckv_norm_{tag}")
    q_raw = _mm(cqn, p["w_uqT"], tb=True, name=f"uq_{tag}")
    kv_raw = _mm(ckvn, p["w_ukvT"], tb=True, name=f"ukv_{tag}")
    q, k, v = _qk_prep_fwd(q_raw, kv_raw, proj, p["q_gain"], p["k_gain"], cos_t, sin_t, name=f"qk_prep_{tag}")
    o, lse = _flash_fwd(q, k, v, name=f"flash_fwd_{tag}")
    conv = _sconv_fwd(proj, p["sconv_w"], name=f"sconv_{tag}")
    y = _mm(conv, p["w_out"][HEADS * HEAD_PAD:], add=x, name=f"mix_out_conv_{tag}")
    y = _mm(o, p["w_out"][:HEADS * HEAD_PAD], add=y, name=f"mix_out_{tag}")
    return y, (x, h, proj, cqn, ckvn, q_raw, kv_raw, q, k, v, o, lse, conv)


def _mla_bwd(dy, dyb, p, tabs, saved, tag):
    cos_t, sin_t = tabs
    x, h, proj, cqn, ckvn, q_raw, kv_raw, q, k, v, o, lse, conv = saved
    s = x.shape[0]
    g = {}
    dmix = _mm(dyb, p["w_out"], tb=True, name=f"mix_out_dx_{tag}")
    g["w_out"] = jnp.concatenate([_mm(o, dyb, ta=True, out_dtype=bf16, name=f"mix_out_dw_{tag}"),
                                  _mm(conv, dyb, ta=True, out_dtype=bf16, name=f"mix_out_conv_dw_{tag}")], axis=0)
    dgb, dgc, dci, g["sconv_w"] = _sconv_bwd(dmix, proj, p["sconv_w"], name=f"sconv_bwd_{tag}")
    dq, delta = _flash_bwd_dq(q, k, v, o, dmix, lse, name=f"flash_dq_{tag}")
    dk, dv = _flash_bwd_dkv(q, k, v, dmix, lse.reshape(HEADS, 1, s), delta.reshape(HEADS, 1, s), name=f"flash_dkv_{tag}")
    dq_raw, dkv_raw, dkr, g["q_gain"], g["k_gain"] = _qk_prep_bwd(
        dq, dk, dv, q_raw, kv_raw, proj, p["q_gain"], p["k_gain"], cos_t, sin_t, name=f"qk_prep_bwd_{tag}")
    dcqn = _mm(dq_raw, p["w_uqT"], name=f"uq_dx_{tag}")
    g["w_uqT"] = _mm(dq_raw, cqn, ta=True, out_dtype=bf16, name=f"uq_dw_{tag}")
    dckvn = _mm(dkv_raw, p["w_ukvT"], name=f"ukv_dx_{tag}")
    g["w_ukvT"] = _mm(dkv_raw, ckvn, ta=True, out_dtype=bf16, name=f"ukv_dw_{tag}")
    dcq, g["cq_norm"] = _rms_bwd(dcqn, proj, p["cq_norm"], col=0, out_dtype=bf16, name=f"cq_norm_bwd_{tag}")
    dckv, g["ckv_norm"] = _rms_bwd(dckvn, proj, p["ckv_norm"], col=1, out_dtype=bf16, name=f"ckv_norm_bwd_{tag}")
    dproj = jnp.concatenate([dcq, dckv, dgb, dgc, dci, dkr.astype(bf16)], axis=1)
    dh = _mm(dproj, p["w_inT"], name=f"mix_in_dx_{tag}")
    g["w_inT"] = _mm(dproj, h, ta=True, out_dtype=bf16, name=f"mix_in_dw_{tag}")
    dx, dxb, g["norm"] = _rms_bwd(dh, x, p["norm"], res=dy, name=f"attn_norm_bwd_{tag}")
    return dx, dxb, g


def _block_diag(wg):
    nb, ng, r, c = wg.shape
    eye = jnp.eye(ng, dtype=wg.dtype)
    return (wg[:, :, :, None, :] * eye[None, :, None, :, None]).reshape(nb, ng * r, ng * c)


def _s5_mats(bbr, bbi, c_re, c_im):
    nb = GROUPS // 8
    b4 = jnp.stack([bbr.reshape(GROUPS, STATE, GROUP), bbi.reshape(GROUPS, STATE, GROUP)], axis=1)
    wg = jnp.transpose(b4, (0, 3, 1, 2)).reshape(nb, 8, GROUP, 2 * STATE)
    cg = jnp.stack([c_re, -c_im], axis=1)
    cg = jnp.transpose(cg, (0, 1, 3, 2)).reshape(nb, 8, 2 * STATE, GROUP)
    return _state_layout(_block_diag(wg), 2), _state_layout(_block_diag(cg), 1)


def _state_layout(m, axis):
    shp = m.shape
    m = m.reshape(shp[:axis] + (4, 2, 2, STATE) + shp[axis + 1:])
    return jnp.swapaxes(m, axis + 1, axis + 2).reshape(shp)


def _group_blocks(d):
    d = d.reshape(GROUPS // 2, 2, GROUP, 2, 2, STATE)
    return jnp.stack([d[:, 0, :, :, 0, :], d[:, 1, :, :, 1, :]], axis=1).reshape(GROUPS, GROUP, 2, STATE)


def _s5_fwd(x, p, tag):
    h = _rms_fwd(x, p["norm"], name=f"ssm_norm_{tag}")
    u, ub = _mm(h, p["w_in"], twin=True, name=f"ssm_in_{tag}")
    ar, ai, bbr, bbi = _disc_fwd(p["lr"], p["li"], p["ls"], p["br"], p["bi"], name=f"disc_{tag}")
    wb, cb = _s5_mats(bbr, bbi, p["c_re"], p["c_im"])
    a1, a2 = ar.reshape(GROUPS // 2, 1, LANES), ai.reshape(GROUPS // 2, 1, LANES)
    xs = _scan_fwd(ub, wb.astype(bf16), a1, a2, name=f"ssm_scan_{tag}")
    y = _bd_nn(xs, cb.astype(bf16), name=f"ssm_y_{tag}")
    g = _gelu_fwd(y, u, p["d_skip"], name=f"ssm_gelu_{tag}")
    a = _mm(g, p["wgaT"], tb=True, name=f"glu_a_{tag}")
    b = _mm(g, p["wgbT"], tb=True, name=f"glu_b_{tag}")
    out = _glu_fwd(x, a, b, name=f"glu_{tag}")
    return out, (x, h, u, ub, wb, cb, a1, a2, xs, y, g, a, b)


def _s5_bwd(dout, p, saved, tag):
    x, h, u, ub, wb, cb, a1, a2, xs, y, g, a, b = saved
    gr = {}
    da, db = _glu_bwd(dout, a, b, name=f"glu_bwd_{tag}")
    dg = _mm(da, p["wgaT"], name=f"glu_a_dx_{tag}")
    dg = _mm(db, p["wgbT"], add=dg, name=f"glu_b_dx_{tag}")
    gr["wgaT"] = _mm(da, g, ta=True, out_dtype=bf16, name=f"glu_a_dw_{tag}")
    gr["wgbT"] = _mm(db, g, ta=True, out_dtype=bf16, name=f"glu_b_dw_{tag}")
    dy, du1, gr["d_skip"] = _gelu_bwd(dg, y, u, p["d_skip"], name=f"ssm_gelu_bwd_{tag}")
    dct = _group_blocks(_bd_tn_diag(dy, xs, name=f"ssm_y_dw_{tag}"))
    gs, dar, dai = _scan_bwd(dy, jnp.swapaxes(cb, 1, 2).astype(bf16), xs, a1, a2, name=f"ssm_scan_bwd_{tag}")
    du = _bd_nn(gs, jnp.swapaxes(wb, 1, 2).astype(bf16), add=du1, out_dtype=bf16, name=f"ssm_bu_dx_{tag}")
    dwg = _group_blocks(_bd_tn_diag(ub, gs, name=f"ssm_bu_dw_{tag}"))
    dh = _mm(du, p["w_in"], tb=True, name=f"ssm_in_dx_{tag}")
    gr["w_in"] = _mm(h, du, ta=True, out_dtype=bf16, name=f"ssm_in_dw_{tag}")
    dx, dxb, gr["norm"] = _rms_bwd(dh, x, p["norm"], res=dout, name=f"ssm_norm_bwd_{tag}")
    dbb = jnp.transpose(dwg, (2, 0, 3, 1)).reshape(2, GROUPS, STATE * GROUP)
    gr["c_re"] = dct[:, :, 0, :]
    gr["c_im"] = -dct[:, :, 1, :]
    dlr, dli, dls, dbr, dbi = _disc_bwd(p["lr"], p["li"], p["ls"], p["br"], p["bi"], dar.reshape(GROUPS, STATE),
                                        dai.reshape(GROUPS, STATE), dbb[0], dbb[1], name=f"disc_bwd_{tag}")
    gr["lr"], gr["li"], gr["ls"] = dlr, dli, dls.reshape(GROUPS)
    gr["br"], gr["bi"] = dbr.reshape(GROUPS, STATE, GROUP), dbi.reshape(GROUPS, STATE, GROUP)
    return dx, dxb, gr


def _slab_shape(shard, axis):
    return (shard[0], shard[2], shard[1]) if axis == 2 else shard


def _to_slab(w, axis):
    return jnp.swapaxes(w, 1, 2) if axis == 2 else w


def _mix_in_pad(wt):
    z = lambda n: jnp.zeros((n, wt.shape[1]), wt.dtype)
    return jnp.concatenate([wt[:512], wt[544:2080], z(NOPE), wt[512:544], z(HEAD_PAD - QK)], axis=0)


def _mix_in_unpad(g):
    return jnp.concatenate([g[:512], g[2048 + NOPE:2048 + QK], g[512:2048]], axis=0)


def _mix_out_pad(w):
    att = jnp.pad(w[:512].reshape(HEADS, NOPE, D), ((0, 0), (NOPE, 0), (0, 0))).reshape(HEADS * HEAD_PAD, D)
    return jnp.concatenate([att, w[512:]], axis=0)


def _mix_out_unpad(g):
    att = g[:HEADS * HEAD_PAD].reshape(HEADS, HEAD_PAD, D)[:, NOPE:, :].reshape(HEADS * NOPE, D)
    return jnp.concatenate([att, g[HEADS * HEAD_PAD:]], axis=0)


def _layer_params(wl, ws, layer):
    i = layer // 2
    half = N_DEV // 2
    up = wl["ffn_w_up"][layer]
    ffn = dict(norm=_row(ws["ffn_norm"][layer]), wgT=up[:half].reshape(FFN_H, D), wvT=up[half:].reshape(FFN_H, D),
               cwg=ws["ffn_conv_w"][layer][:, :FFN_H], cwv=ws["ffn_conv_w"][layer][:, FFN_H:],
               wd=wl["ffn_w_down"][layer].reshape(FFN_H, D))
    if layer % 2 == 0:
        uq = jnp.pad(wl["w_uq"][i], ((0, 0), (0, HEAD_PAD - QK), (0, 0)))
        mixer = dict(norm=_row(ws["attn_norm"][i]), w_inT=_mix_in_pad(wl["mix_w_in"][i].reshape(-1, D)),
                     cq_norm=_row(ws["cq_norm"][i]), ckv_norm=_row(ws["ckv_norm"][i]),
                     w_uqT=uq.reshape(HEADS * HEAD_PAD, LORA), w_ukvT=wl["w_ukv"][i].reshape(HEADS * HEAD_PAD, LORA),
                     q_gain=_pad_gain(ws["q_gain"][i]), k_gain=_pad_gain(ws["k_gain"][i]), sconv_w=ws["sconv_w"][i],
                     w_out=_mix_out_pad(wl["mix_w_out"][i].reshape(D, D)))
    else:
        glu = wl["w_glu"][i]
        mixer = dict(norm=_row(ws["ssm_norm"][i]), w_in=wl["ssm_w_in"][i].reshape(D, D), lr=ws["lambda_re"][i],
                     li=ws["lambda_im"][i], ls=ws["log_step"][i].reshape(GROUPS, 1),
                     br=ws["b_re"][i].reshape(GROUPS, STATE * GROUP), bi=ws["b_im"][i].reshape(GROUPS, STATE * GROUP),
                     c_re=ws["c_re"][i], c_im=ws["c_im"][i], d_skip=_row(ws["d_skip"][i]),
                     wgaT=glu[:half].reshape(D, D), wgbT=glu[half:].reshape(D, D))
    return mixer, ffn


def _collect_grads(gm, gf):
    ev, od, half = (0, 2), (1, 3), N_DEV // 2
    st = lambda xs: jnp.stack(xs, axis=0)
    per_dev = list
    halves = lambda a, b, rows: jnp.concatenate([a.reshape(half, rows, D), b.reshape(half, rows, D)], axis=0)
    big = {
        "ffn_w_up": per_dev([halves(gf[l]["wgT"], gf[l]["wvT"], FFN_H // half) for l in range(4)]),
        "ffn_w_down": per_dev([gf[l]["wd"].reshape(N_DEV, -1, D) for l in range(4)]),
        "w_glu": per_dev([halves(gm[l]["wgaT"], gm[l]["wgbT"], D // half) for l in od]),
        "mix_w_out": per_dev([_mix_out_unpad(gm[l]["w_out"]).reshape(N_DEV, -1, D) for l in ev]),
        "ssm_w_in": per_dev([gm[l]["w_in"].reshape(N_DEV, -1, D) for l in od]),
        "w_ukv": per_dev([gm[l]["w_ukvT"].reshape(N_DEV, HEAD_PAD, LORA) for l in ev]),
        "w_uq": per_dev([gm[l]["w_uqT"].reshape(N_DEV, HEAD_PAD, LORA)[:, :QK] for l in ev]),
        "mix_w_in": per_dev([_mix_in_unpad(gm[l]["w_inT"]).reshape(N_DEV, -1, D) for l in ev]),
    }
    small = {
        "attn_norm": st([gm[l]["norm"].reshape(D) for l in ev]),
        "cq_norm": st([gm[l]["cq_norm"].reshape(LORA) for l in ev]),
        "ckv_norm": st([gm[l]["ckv_norm"].reshape(LORA) for l in ev]),
        "q_gain": st([gm[l]["q_gain"].reshape(HEAD_PAD)[:QK] for l in ev]),
        "k_gain": st([gm[l]["k_gain"].reshape(HEAD_PAD)[:QK] for l in ev]),
        "sconv_w": st([gm[l]["sconv_w"] for l in ev]),
        "ssm_norm": st([gm[l]["norm"].reshape(D) for l in od]),
        "lambda_re": st([gm[l]["lr"] for l in od]), "lambda_im": st([gm[l]["li"] for l in od]),
        "log_step": st([gm[l]["ls"] for l in od]),
        "b_re": st([gm[l]["br"] for l in od]), "b_im": st([gm[l]["bi"] for l in od]),
        "c_re": st([gm[l]["c_re"] for l in od]), "c_im": st([gm[l]["c_im"] for l in od]),
        "d_skip": st([gm[l]["d_skip"].reshape(D) for l in od]),
        "ffn_norm": st([gf[l]["norm"].reshape(D) for l in range(4)]),
        "ffn_conv_w": st([jnp.concatenate([gf[l]["cwg"], gf[l]["cwv"]], axis=1) for l in range(4)]),
    }
    return big, small


def _local_step(x, target, wl, ws):
    s = x.shape[0]
    tabs = _rope_tables(s)
    saved, params = [], []
    for layer in range(4):
        mixer, ffn = _layer_params(wl, ws, layer)
        params.append((mixer, ffn))
        if layer % 2 == 0:
            x, sm = _mla_fwd(x, mixer, tabs, f"l{layer}")
        else:
            x, sm = _s5_fwd(x, mixer, f"l{layer}")
        x, sf = _ffn_fwd(x, ffn, f"l{layer}")
        saved.append((sm, sf))
    dx, dxb, loss = _loss_head(x, target, name="loss_head")
    gm, gf = [None] * 4, [None] * 4
    for layer in reversed(range(4)):
        mixer, ffn = params[layer]
        sm, sf = saved[layer]
        dx, dxb, gf[layer] = _ffn_bwd(dx, dxb, ffn, sf, f"l{layer}")
        if layer % 2 == 0:
            dx, dxb, gm[layer] = _mla_bwd(dx, dxb, mixer, tabs, sm, f"l{layer}")
        else:
            dx, dxb, gm[layer] = _s5_bwd(dx, mixer, sm, f"l{layer}")
    return loss, dx, _collect_grads(gm, gf)


def kernel(x, attn_norm, mix_w_in, cq_norm, ckv_norm, w_uq, w_ukv, q_gain, k_gain, sconv_w, mix_w_out, ssm_norm, ssm_w_in, lambda_re, lambda_im, log_step, b_re, b_im, c_re, c_im, d_skip, w_glu, ffn_norm, ffn_w_up, ffn_conv_w, ffn_w_down, loss_target, m_attn_norm, m_mix_w_in, m_cq_norm, m_ckv_norm, m_w_uq, m_w_ukv, m_q_gain, m_k_gain, m_sconv_w, m_mix_w_out, m_ssm_norm, m_ssm_w_in, m_lambda_re, m_lambda_im, m_log_step, m_b_re, m_b_im, m_c_re, m_c_im, m_d_skip, m_w_glu, m_ffn_norm, m_ffn_w_up, m_ffn_conv_w, m_ffn_w_down, v_attn_norm, v_mix_w_in, v_cq_norm, v_ckv_norm, v_w_uq, v_w_ukv, v_q_gain, v_k_gain, v_sconv_w, v_mix_w_out, v_ssm_norm, v_ssm_w_in, v_lambda_re, v_lambda_im, v_log_step, v_b_re, v_b_im, v_c_re, v_c_im, v_d_skip, v_w_glu, v_ffn_norm, v_ffn_w_up, v_ffn_conv_w, v_ffn_w_down):
    args = dict(locals())
    wsh = {n: args[n] for n in WEIGHTS}
    msh = {n: args["m_" + n] for n in WEIGHTS}
    vsh = {n: args["v_" + n] for n in WEIGHTS}
    me = 4 * lax.axis_index("x") + 2 * lax.axis_index("y") + lax.axis_index("c")
    big_names = [n for n, _, _ in BIG]
    slab_shapes = [_slab_shape(sh, ax) for _, sh, ax in BIG]
    small_names = [n for n, _ in REPL] + [n for n, _, _ in SMALL]

    mine = _pack_slabs([_to_slab(wsh[n], ax).astype(bf16) for n, _, ax in BIG], BIG_ROWS)
    gathered, wl, off = _gather8(mine, name="gather_weights"), {}, 0
    for n, (layers, rows, inner) in zip(big_names, slab_shapes):
        per = rows * inner // D
        wl[n] = [gathered[:, off + l * per:off + (l + 1) * per, :].reshape(N_DEV, rows, inner) for l in range(layers)]
        off += layers * per
    placed = []
    for n, shard, axis in SMALL:
        start = [0] * len(shard)
        start[axis] = me * shard[axis]
        placed.append(lax.dynamic_update_slice(jnp.zeros(_full_shape(shard, axis), f32), wsh[n], start))
    small_all = _all_sum(_pack_rows(placed, SMALL_FWD_ROWS), name="small_params")
    ws = dict(zip([n for n, _, _ in SMALL], _unpack_rows(small_all, [_full_shape(sh, ax) for _, sh, ax in SMALL])))
    ws.update({n: wsh[n] for n, _ in REPL})

    loss8, grad_x, (big_grads, grads) = _local_step(x[0], loss_target[0], wl, ws)

    pieces = []
    for n in big_names:
        layers = big_grads[n]
        tiled = (layers[0].shape[1] * layers[0].shape[2] // D) % 16 == 0
        pieces += layers if tiled else [jnp.stack(layers, axis=1)]
    contrib = _pack_slabs(pieces, BIG_ROWS, axis=1)
    chip_sum = _pair_sum(contrib, _pair_exchange(contrib, name="grads_pair_exchange"), name="grads_pair_sum")
    g_big = _sum_slabs(_cross_exchange(chip_sum, name="grads_cross_exchange"), name="grads_chip_sum",
                       tr=_row_tile(BIG_ROWS, 1024))
    small_vec = _pack_rows([grads[n] for n, _ in REPL] + [grads[n] for n, _, _ in SMALL] + [loss8[0, :1]], SMALL_ROWS)
    small_sum = _all_sum(small_vec, name="small_grads")
    parts = _unpack_rows(small_sum, [sh for _, sh in REPL] + [_full_shape(sh, ax) for _, sh, ax in SMALL] + [(1,)])
    g = {n: val for (n, _), val in zip(REPL, parts)}
    for (n, shard, axis), val in zip(SMALL, parts[len(REPL):]):
        start = [0] * len(shard)
        start[axis] = me * shard[axis]
        g[n] = lax.dynamic_slice(val, start, shard)
    loss = parts[-1].reshape(())
    for (n, _, axis), val in zip(BIG, _unpack_slabs(g_big, slab_shapes)):
        g[n] = _to_slab(val, axis)

    delta, new_m, new_v = {}, {}, {}
    for n, shard, _ in BIG:
        flat = lambda a: a.reshape(-1, shard[-1])
        outs = _adamw(flat(wsh[n]), flat(g[n]), flat(msh[n]), flat(vsh[n]), name=f"adamw_{n}",
                      tr=_row_tile(shard[0] * shard[1], 512))
        delta[n], new_m[n], new_v[n] = [o.reshape(shard) for o in outs]
    small_state = [_pack_rows([src[n] for n in small_names], SMALL_ROWS) for src in (wsh, g, msh, vsh)]
    for dst, slab in zip((delta, new_m, new_v), _adamw(*small_state, name="adamw_small")):
        dst.update(zip(small_names, _unpack_rows(slab, [wsh[n].shape for n in small_names])))

    return (loss, grad_x[None], *[g[n] for n in WEIGHTS], *[delta[n] for n in WEIGHTS],
            *[new_m[n] for n in WEIGHTS], *[new_v[n] for n in WEIGHTS])
```

```python
import math

import numpy as np
import jax
import jax.numpy as jnp
from jax import lax
from jax.experimental import pallas as pl
from jax.experimental.pallas import tpu as pltpu

f32, bf16 = jnp.float32, jnp.bfloat16

N_DEV = 8
D = 1024
HEADS = 8
NOPE, ROPE, QK = 64, 32, 96
HEAD_PAD = 128
LORA = 256
CONV_CH = 512
MIX_IN_PAD = 2176
FFN_H = 2816
GROUPS, GROUP, STATE = 64, 16, 64
EPS = 1e-6
ROPE_THETA = 10000.0
ADAM_LR, ADAM_B1, ADAM_B2, ADAM_EPS, ADAM_WD, ADAM_STEP = 0.001, 0.9, 0.999, 1e-08, 0.01, 10
LANES = 128
PAIR_LANES = 2 * LANES
VMEM_LIMIT = 56 << 20
MM_VMEM_BUDGET = 40 << 20
ROW_BLOCK = 1024
NEG = -1e30

BIG = (
    ("ffn_w_up", (4, 1024, 704), 2), ("ffn_w_down", (4, 352, 1024), 1), ("w_glu", (2, 1024, 256), 2),
    ("mix_w_out", (2, 128, 1024), 1), ("ssm_w_in", (2, 128, 1024), 1), ("w_ukv", (2, 256, 128), 2),
    ("w_uq", (2, 256, 96), 2), ("mix_w_in", (2, 1024, 260), 2))
REPL = (("attn_norm", (2, 1024)), ("cq_norm", (2, 256)), ("ckv_norm", (2, 256)), ("q_gain", (2, 96)),
        ("k_gain", (2, 96)), ("lambda_re", (2, 64, 64)), ("lambda_im", (2, 64, 64)), ("log_step", (2, 64)),
        ("b_re", (2, 64, 64, 16)), ("b_im", (2, 64, 64, 16)), ("c_re", (2, 64, 16, 64)), ("c_im", (2, 64, 16, 64)),
        ("ffn_norm", (4, 1024)))
SMALL = (("sconv_w", (2, 3, 64), 2), ("ssm_norm", (2, 128), 1), ("d_skip", (2, 128), 1), ("ffn_conv_w", (4, 3, 704), 2))
WEIGHTS = ['attn_norm', 'mix_w_in', 'cq_norm', 'ckv_norm', 'w_uq', 'w_ukv', 'q_gain', 'k_gain', 'sconv_w', 'mix_w_out',
           'ssm_norm', 'ssm_w_in', 'lambda_re', 'lambda_im', 'log_step', 'b_re', 'b_im', 'c_re', 'c_im', 'd_skip',
           'w_glu', 'ffn_norm', 'ffn_w_up', 'ffn_conv_w', 'ffn_w_down']
BIG_ROWS = 5888
SMALL_FWD_ROWS = 80
SMALL_ROWS = 640


def _cparams(sem=None, **kw):
    return pltpu.CompilerParams(dimension_semantics=sem, vmem_limit_bytes=VMEM_LIMIT, **kw)


def _row_tile(rows, target):
    fits = [t for t in range(16, min(rows, target) + 1, 16) if rows % t == 0]
    return max(fits) if fits else rows


def _tile(n, target):
    best = 0
    for t in range(LANES, min(n, target) + 1, LANES):
        if n % t == 0:
            best = t
    return best if best else n


def _mm(a, b, *, ta=False, tb=False, out_dtype=f32, add=None, twin=False, name, tm=1024, tn=1536):
    m, k = (a.shape[1], a.shape[0]) if ta else a.shape
    n = b.shape[0] if tb else b.shape[1]
    assert (b.shape[1] if tb else b.shape[0]) == k
    tm = _tile(m, tm)
    tn_ = _tile(n, tn)
    tn = n if (tn_ < 256 and n <= 2304) else tn_

    def vmem_bytes(t):
        io = 2 * (tm * t * a.dtype.itemsize + t * tn * b.dtype.itemsize + tm * tn * (jnp.dtype(out_dtype).itemsize + 2 * twin))
        return io + (2 * tm * tn * 4 if add is not None else 0) + (tm * tn * 4 if t < k else 0)

    tk = next((t for t in [k] + [t for t in range(k - LANES, 0, -LANES) if k % t == 0] if vmem_bytes(t) <= MM_VMEM_BUDGET), LANES)
    nk = k // tk
    dn = (((0 if ta else 1,), (1 if tb else 0,)), ((), ()))

    def body(*refs):
        a_ref, b_ref = refs[:2]
        add_ref = refs[2] if add is not None else None
        o_ref = refs[3] if add is not None else refs[2]
        twin_ref = refs[4 if add is not None else 3] if twin else None
        part = lax.dot_general(a_ref[...].astype(bf16), b_ref[...].astype(bf16), dn, preferred_element_type=f32)

        def finish(r):
            if add is not None:
                r = r + add_ref[...].astype(f32)
            o_ref[...] = r.astype(out_dtype)
            if twin:
                twin_ref[...] = r.astype(bf16)

        if nk == 1:
            finish(part)
            return
        acc = refs[-1]
        kk = pl.program_id(2)

        @pl.when(kk == 0)
        def _():
            acc[...] = part

        @pl.when(kk > 0)
        def _():
            acc[...] += part

        @pl.when(kk == nk - 1)
        def _():
            finish(acc[...])

    a_spec = pl.BlockSpec((tk, tm), lambda i, j, kk: (kk, i)) if ta else pl.BlockSpec((tm, tk), lambda i, j, kk: (i, kk))
    b_spec = pl.BlockSpec((tn, tk), lambda i, j, kk: (j, kk)) if tb else pl.BlockSpec((tk, tn), lambda i, j, kk: (kk, j))
    in_specs, args = [a_spec, b_spec], [a, b]
    if add is not None:
        in_specs.append(pl.BlockSpec((tm, tn), lambda i, j, kk: (i, j)))
        args.append(add)
    o_spec, o_shape = pl.BlockSpec((tm, tn), lambda i, j, kk: (i, j)), jax.ShapeDtypeStruct((m, n), out_dtype)
    return pl.pallas_call(
        body, name=name, grid=(m // tm, n // tn, nk), in_specs=in_specs,
        out_specs=[o_spec, o_spec] if twin else o_spec,
        out_shape=[o_shape, jax.ShapeDtypeStruct((m, n), bf16)] if twin else o_shape,
        scratch_shapes=[pltpu.VMEM((tm, tn), f32)] if nk > 1 else [],
        compiler_params=_cparams(("parallel", "parallel", "arbitrary")))(*args)


def _bd_nn(a, w, *, out_dtype=f32, add=None, name, ts=2048):
    s = a.shape[0]
    nb, ka, no = w.shape
    ts = min(ts, s)

    def body(a_ref, w_ref, *rest):
        r = jnp.dot(a_ref[...].astype(bf16), w_ref[0].astype(bf16), preferred_element_type=f32)
        if add is not None:
            r = r + rest[0][...].astype(f32)
        rest[-1][...] = r.astype(out_dtype)

    o_spec = pl.BlockSpec((ts, no), lambda b, i: (i, b))
    return pl.pallas_call(
        body, name=name, grid=(nb, s // ts),
        in_specs=[pl.BlockSpec((ts, ka), lambda b, i: (i, b)), pl.BlockSpec((1, ka, no), lambda b, i: (b, 0, 0))]
        + ([o_spec] if add is not None else []),
        out_specs=o_spec, out_shape=jax.ShapeDtypeStruct((s, nb * no), out_dtype),
        compiler_params=_cparams(("parallel", "parallel")))(a, w, *([add] if add is not None else []))


def _bd_tn_diag(a, g, *, name, ts=2048):
    s = a.shape[0]
    nb = a.shape[1] // LANES
    ts = min(ts, s)
    ni = s // ts

    def body(a_ref, g_ref, o_ref, acc):
        i = pl.program_id(1)
        part = lax.dot_general(a_ref[...].astype(bf16), g_ref[...].astype(bf16), (((0,), (0,)), ((), ())),
                               preferred_element_type=f32)

        @pl.when(i == 0)
        def _():
            acc[...] = part

        @pl.when(i > 0)
        def _():
            acc[...] += part

        @pl.when(i == ni - 1)
        def _():
            for j in range(8):
                o_ref[0, j] = acc[j * GROUP:(j + 1) * GROUP, (j // 2) * PAIR_LANES:(j // 2 + 1) * PAIR_LANES]

    return pl.pallas_call(
        body, name=name, grid=(nb, ni),
        in_specs=[pl.BlockSpec((ts, LANES), lambda b, i: (i, b)), pl.BlockSpec((ts, 8 * LANES), lambda b, i: (i, b))],
        out_specs=pl.BlockSpec((1, 8, GROUP, PAIR_LANES), lambda b, i: (b, 0, 0, 0)),
        out_shape=jax.ShapeDtypeStruct((nb, 8, GROUP, PAIR_LANES), f32),
        scratch_shapes=[pltpu.VMEM((LANES, 8 * LANES), f32)],
        compiler_params=_cparams(("parallel", "arbitrary")))(a, g)


def _rms_fwd(x, g, *, col=0, name, ts=ROW_BLOCK):
    s, d = x.shape[0], g.shape[1]
    ts = min(ts, s)

    def body(x_ref, g_ref, o_ref):
        xv = x_ref[...].astype(f32)
        r = lax.rsqrt(jnp.mean(xv * xv, axis=-1, keepdims=True) + EPS)
        o_ref[...] = (xv * r * g_ref[...]).astype(bf16)

    return pl.pallas_call(
        body, name=name, grid=(s // ts,),
        in_specs=[pl.BlockSpec((ts, d), lambda i: (i, col)), pl.BlockSpec((1, d), lambda i: (0, 0))],
        out_specs=pl.BlockSpec((ts, d), lambda i: (i, 0)),
        out_shape=jax.ShapeDtypeStruct((s, d), bf16),
        compiler_params=_cparams(("parallel",)))(x, g)


def _rms_bwd(dy, x, g, *, col=0, res=None, out_dtype=f32, name, ts=ROW_BLOCK):
    s, d = dy.shape
    ts = min(ts, s)
    twin = res is not None

    def body(*refs):
        if twin:
            dy_ref, x_ref, g_ref, res_ref, dx_ref, dxb_ref, dg_ref = refs
        else:
            dy_ref, x_ref, g_ref, dx_ref, dg_ref = refs

        @pl.when(pl.program_id(0) == 0)
        def _():
            dg_ref[...] = jnp.zeros_like(dg_ref)

        xv, dyv = x_ref[...].astype(f32), dy_ref[...].astype(f32)
        r = lax.rsqrt(jnp.mean(xv * xv, axis=-1, keepdims=True) + EPS)
        dyg = dyv * g_ref[...]
        dx = r * dyg - xv * (r * r * r) * jnp.mean(xv * dyg, axis=-1, keepdims=True)
        if twin:
            dx = dx + res_ref[...]
            dxb_ref[...] = dx.astype(bf16)
        dx_ref[...] = dx.astype(out_dtype)
        dg_ref[...] += jnp.sum(dyv * xv * r, axis=0, keepdims=True)

    row, vec = pl.BlockSpec((ts, d), lambda i: (i, 0)), pl.BlockSpec((1, d), lambda i: (0, 0))
    in_specs, args = [row, pl.BlockSpec((ts, d), lambda i: (i, col)), vec], [dy, x, g]
    out_specs, out_shape = [row], [jax.ShapeDtypeStruct((s, d), out_dtype)]
    if twin:
        in_specs.append(row)
        args.append(res)
        out_specs.append(row)
        out_shape.append(jax.ShapeDtypeStruct((s, d), bf16))
    return pl.pallas_call(
        body, name=name, grid=(s // ts,), in_specs=in_specs, out_specs=out_specs + [vec],
        out_shape=out_shape + [jax.ShapeDtypeStruct((1, d), f32)],
        compiler_params=_cparams(("arbitrary",)))(*args)


def _swap_halves(z):
    lane = lax.broadcasted_iota(jnp.int32, z.shape, 1)
    return jnp.where(lane < NOPE + ROPE // 2, pltpu.roll(z, LANES - ROPE // 2, axis=1), pltpu.roll(z, ROPE // 2, axis=1))


def _rope_tables(s):
    inv_freq = 1.0 / (ROPE_THETA ** (jnp.arange(0, ROPE, 2, dtype=f32) / ROPE))
    ang = jnp.arange(s, dtype=f32)[:, None] * inv_freq[None, :]
    cos, sin = jnp.cos(ang), jnp.sin(ang)
    one, zero = jnp.ones((s, NOPE), f32), jnp.zeros((s, NOPE), f32)
    pad1, pad0 = jnp.ones((s, HEAD_PAD - QK), f32), jnp.zeros((s, HEAD_PAD - QK), f32)
    return jnp.concatenate([one, cos, cos, pad1], 1), jnp.concatenate([zero, -sin, sin, pad0], 1)


def _qk_prep_fwd(q_raw, kv_raw, proj, qg, kg, cos_t, sin_t, *, name, ts=2 * ROW_BLOCK):
    s = q_raw.shape[0]
    ts = min(ts, s)
    rope_blk = (MIX_IN_PAD - HEAD_PAD) // HEAD_PAD

    def body(q_ref, kv_ref, kr_ref, qg_ref, kg_ref, c_ref, s_ref, qo_ref, ko_ref, vo_ref):
        lane = lax.broadcasted_iota(jnp.int32, (ts, HEAD_PAD), 1)
        cosv, sinv = c_ref[...], s_ref[...]

        def norm_rope(z, gain):
            r = lax.rsqrt(jnp.sum(z * z, axis=-1, keepdims=True) * (1.0 / QK) + EPS)
            zn = z * r * gain
            return zn * cosv + _swap_halves(zn) * sinv

        kvv = kv_ref[...]
        qo_ref[...] = (norm_rope(q_ref[...], qg_ref[...]) * _Q_FOLD).astype(bf16)
        ko_ref[...] = norm_rope(jnp.where(lane < NOPE, kvv, kr_ref[...]), kg_ref[...]).astype(bf16)
        vo_ref[...] = jnp.where(lane >= NOPE, kvv, 0.0).astype(bf16)

    head = pl.BlockSpec((ts, HEAD_PAD), lambda i, h: (i, h))
    row = pl.BlockSpec((ts, HEAD_PAD), lambda i, h: (i, 0))
    vec = pl.BlockSpec((1, HEAD_PAD), lambda i, h: (0, 0))
    out = jax.ShapeDtypeStruct((s, HEADS * HEAD_PAD), bf16)
    return pl.pallas_call(
        body, name=name, grid=(s // ts, HEADS),
        in_specs=[head, head, pl.BlockSpec((ts, HEAD_PAD), lambda i, h: (i, rope_blk)), vec, vec, row, row],
        out_specs=[head, head, head], out_shape=[out, out, out],
        compiler_params=_cparams(("parallel", "parallel")))(q_raw, kv_raw, proj, qg, kg, cos_t, sin_t)


def _qk_prep_bwd(dq, dk, dv, q_raw, kv_raw, proj, qg, kg, cos_t, sin_t, *, name, ts=2 * ROW_BLOCK):
    s = q_raw.shape[0]
    ts = min(ts, s)
    rope_blk = (MIX_IN_PAD - HEAD_PAD) // HEAD_PAD

    def body(dq_ref, dk_ref, dv_ref, q_ref, kv_ref, kr_ref, qg_ref, kg_ref, c_ref, s_ref,
             dqr_ref, dkvr_ref, dkr_ref, dqg_ref, dkg_ref):
        i, h = pl.program_id(0), pl.program_id(1)
        lane = lax.broadcasted_iota(jnp.int32, (ts, HEAD_PAD), 1)
        is_rope = (lane >= NOPE) & (lane < QK)
        cosv, sinv = c_ref[...], s_ref[...]

        @pl.when((i == 0) & (h == 0))
        def _():
            dqg_ref[...] = jnp.zeros_like(dqg_ref)
            dkg_ref[...] = jnp.zeros_like(dkg_ref)

        @pl.when(h == 0)
        def _():
            dkr_ref[...] = jnp.zeros_like(dkr_ref)

        def back(dout, z, gain):
            dzn = dout * cosv + jnp.where(is_rope, _swap_halves(dout * sinv), 0.0)
            r = lax.rsqrt(jnp.sum(z * z, axis=-1, keepdims=True) * (1.0 / QK) + EPS)
            dzg = dzn * gain
            dz = r * dzg - z * (r * r * r) * (jnp.sum(z * dzg, axis=-1, keepdims=True) * (1.0 / QK))
            return dz, jnp.sum(dzn * z * r, axis=0, keepdims=True)

        dqz, dqg = back(dq_ref[...].astype(f32), q_ref[...], qg_ref[...])
        dqr_ref[...] = dqz.astype(bf16)
        dqg_ref[...] += dqg
        kvv = kv_ref[...]
        dkz, dkg = back(dk_ref[...].astype(f32), jnp.where(lane < NOPE, kvv, kr_ref[...]), kg_ref[...])
        dkg_ref[...] += dkg
        dkvr_ref[...] = jnp.where(lane < NOPE, dkz, dv_ref[...].astype(f32)).astype(bf16)
        dkr_ref[...] += jnp.where(is_rope, dkz, 0.0)

    head = pl.BlockSpec((ts, HEAD_PAD), lambda i, h: (i, h))
    row = pl.BlockSpec((ts, HEAD_PAD), lambda i, h: (i, 0))
    vec = pl.BlockSpec((1, HEAD_PAD), lambda i, h: (0, 0))
    wide = jax.ShapeDtypeStruct((s, HEADS * HEAD_PAD), bf16)
    return pl.pallas_call(
        body, name=name, grid=(s // ts, HEADS),
        in_specs=[head, head, head, head, head, pl.BlockSpec((ts, HEAD_PAD), lambda i, h: (i, rope_blk)), vec, vec, row, row],
        out_specs=[head, head, row, vec, vec],
        out_shape=[wide, wide, jax.ShapeDtypeStruct((s, HEAD_PAD), f32), jax.ShapeDtypeStruct((1, HEAD_PAD), f32),
                   jax.ShapeDtypeStruct((1, HEAD_PAD), f32)],
        compiler_params=_cparams(("arbitrary", "arbitrary")))(dq, dk, dv, q_raw, kv_raw, proj, qg, kg, cos_t, sin_t)


_NT = (((1,), (1,)), ((), ()))
_SCALE = QK ** -0.5
_LOG2E = math.log2(math.e)
_Q_FOLD = _SCALE * _LOG2E
FLASH_TILE = 1024


def _flash_fwd(q, k, v, *, name, tq=FLASH_TILE):
    s = q.shape[0]
    tq = min(tq, s)

    def body(q_ref, k_ref, v_ref, o_ref, lse_ref):
        i = pl.program_id(1)
        qv = q_ref[...]

        def step(j, carry, masked):
            m, l, acc = carry
            st = pl.multiple_of(j * tq, tq)
            kj, vj = k_ref[pl.ds(st, tq), :], v_ref[pl.ds(st, tq), :]
            sc = lax.dot_general(qv, kj, _NT, preferred_element_type=f32)
            if masked:
                rr = lax.broadcasted_iota(jnp.int32, (tq, tq), 0)
                cc = lax.broadcasted_iota(jnp.int32, (tq, tq), 1)
                sc = jnp.where(cc <= rr, sc, NEG)
            m_new = jnp.maximum(m, jnp.max(sc, axis=-1, keepdims=True))
            p = jnp.exp2(sc - m_new)
            alpha = jnp.exp2(m - m_new)
            l = alpha * l + jnp.sum(p, axis=-1, keepdims=True)
            acc = alpha * acc + jnp.dot(p.astype(bf16), vj, preferred_element_type=f32)
            return m_new, l, acc

        init = (jnp.full((tq, 1), NEG, f32), jnp.zeros((tq, 1), f32), jnp.zeros((tq, HEAD_PAD), f32))
        carry = lax.fori_loop(0, i, lambda j, c: step(j, c, False), init)
        m, l, acc = step(i, carry, True)
        o_ref[...] = (acc / l).astype(bf16)
        lse_ref[0] = m + jnp.log2(l)

    blk = pl.BlockSpec((tq, HEAD_PAD), lambda h, i: (i, h))
    full = pl.BlockSpec((s, HEAD_PAD), lambda h, i: (0, h))
    return pl.pallas_call(
        body, name=name, grid=(HEADS, s // tq), in_specs=[blk, full, full],
        out_specs=[blk, pl.BlockSpec((1, tq, 1), lambda h, i: (h, i, 0))],
        out_shape=[jax.ShapeDtypeStruct((s, HEADS * HEAD_PAD), bf16), jax.ShapeDtypeStruct((HEADS, s, 1), f32)],
        compiler_params=_cparams(("parallel", "arbitrary")))(q, k, v)


def _flash_bwd_dq(q, k, v, o, do, lse, *, name, tq=FLASH_TILE):
    s = q.shape[0]
    tq = min(tq, s)

    def body(q_ref, k_ref, v_ref, o_ref, do_ref, lse_ref, dq_ref, dl_ref):
        i = pl.program_id(1)
        qv = q_ref[...]
        dov = do_ref[...].astype(f32)
        delta = jnp.sum(dov * o_ref[...].astype(f32), axis=-1, keepdims=True)
        dob = dov.astype(bf16)
        lsev = lse_ref[0]

        def step(j, acc, masked):
            st = pl.multiple_of(j * tq, tq)
            kj, vj = k_ref[pl.ds(st, tq), :], v_ref[pl.ds(st, tq), :]
            sc = lax.dot_general(qv, kj, _NT, preferred_element_type=f32)
            p = jnp.exp2(sc - lsev)
            if masked:
                rr = lax.broadcasted_iota(jnp.int32, (tq, tq), 0)
                cc = lax.broadcasted_iota(jnp.int32, (tq, tq), 1)
                p = jnp.where(cc <= rr, p, 0.0)
            dp = lax.dot_general(dob, vj, _NT, preferred_element_type=f32)
            ds = p * (dp - delta)
            return acc + jnp.dot(ds.astype(bf16), kj, preferred_element_type=f32)

        acc = lax.fori_loop(0, i, lambda j, c: step(j, c, False), jnp.zeros((tq, HEAD_PAD), f32))
        dq_ref[...] = step(i, acc, True) * _SCALE
        dl_ref[0] = delta

    blk = pl.BlockSpec((tq, HEAD_PAD), lambda h, i: (i, h))
    full = pl.BlockSpec((s, HEAD_PAD), lambda h, i: (0, h))
    col = pl.BlockSpec((1, tq, 1), lambda h, i: (h, i, 0))
    return pl.pallas_call(
        body, name=name, grid=(HEADS, s // tq), in_specs=[blk, full, full, blk, blk, col],
        out_specs=[blk, col],
        out_shape=[jax.ShapeDtypeStruct((s, HEADS * HEAD_PAD), f32), jax.ShapeDtypeStruct((HEADS, s, 1), f32)],
        compiler_params=_cparams(("parallel", "arbitrary")))(q, k, v, o, do, lse)


def _flash_bwd_dkv(q, k, v, do, lse_row, delta_row, *, name, tk=FLASH_TILE):
    s = q.shape[0]
    tk = min(tk, s)
    nblk = s // tk

    def body(q_ref, k_ref, v_ref, do_ref, lse_ref, dl_ref, dk_ref, dv_ref):
        j = pl.program_id(1)
        kv_, vv = k_ref[...], v_ref[...]

        def step(i, carry, masked):
            dk, dv = carry
            st = pl.multiple_of(i * tk, tk)
            qi = q_ref[pl.ds(st, tk), :]
            doi = do_ref[pl.ds(st, tk), :].astype(bf16)
            lse_i = lse_ref[0, :, pl.ds(st, tk)]
            dl_i = dl_ref[0, :, pl.ds(st, tk)]
            st_ = lax.dot_general(kv_, qi, _NT, preferred_element_type=f32)
            pt = jnp.exp2(st_ - lse_i)
            if masked:
                kk = lax.broadcasted_iota(jnp.int32, (tk, tk), 0)
                qq = lax.broadcasted_iota(jnp.int32, (tk, tk), 1)
                pt = jnp.where(kk <= qq, pt, 0.0)
            dv = dv + jnp.dot(pt.astype(bf16), doi, preferred_element_type=f32)
            dpt = lax.dot_general(vv, doi, _NT, preferred_element_type=f32)
            dst = pt * (dpt - dl_i)
            dk = dk + jnp.dot(dst.astype(bf16), qi, preferred_element_type=f32)
            return dk, dv

        zero = jnp.zeros((tk, HEAD_PAD), f32)
        carry = step(j, (zero, zero), True)
        dk, dv = lax.fori_loop(j + 1, nblk, lambda i, c: step(i, c, False), carry)
        dk_ref[...] = dk * (1.0 / _LOG2E)
        dv_ref[...] = dv

    blk = pl.BlockSpec((tk, HEAD_PAD), lambda h, j: (j, h))
    full = pl.BlockSpec((s, HEAD_PAD), lambda h, j: (0, h))
    rowv = pl.BlockSpec((1, 1, s), lambda h, j: (h, 0, 0))
    out = jax.ShapeDtypeStruct((s, HEADS * HEAD_PAD), f32)
    return pl.pallas_call(
        body, name=name, grid=(HEADS, nblk), in_specs=[full, blk, blk, full, rowv, rowv],
        out_specs=[blk, blk], out_shape=[out, out],
        compiler_params=_cparams(("parallel", "arbitrary")))(q, k, v, do, lse_row, delta_row)


SUBLANES = 8


def _shift_down(x, d):
    r = pltpu.roll(x, d, axis=0)
    t = lax.broadcasted_iota(jnp.int32, (SUBLANES, x.shape[1]), 0)
    head = jnp.where(t < d, 0.0, r[:SUBLANES])
    return head if x.shape[0] == SUBLANES else jnp.concatenate([head, r[SUBLANES:]], axis=0)


def _shift_up(x, d):
    s = x.shape[0]
    r = pltpu.roll(x, s - d, axis=0)
    t = lax.broadcasted_iota(jnp.int32, (SUBLANES, x.shape[1]), 0)
    tail = jnp.where(t >= SUBLANES - d, 0.0, r[s - SUBLANES:])
    return tail if s == SUBLANES else jnp.concatenate([r[:s - SUBLANES], tail], axis=0)


def _taps(w_ref):
    return w_ref[0:1, :], w_ref[1:2, :], w_ref[2:3, :]


def _conv3(u, w):
    u1, u2 = _shift_down(u, 1), _shift_down(u, 2)
    return w[0] * u2 + w[1] * u1 + w[2] * u, (u1, u2)


def _ref_shift_down(ref, d):
    s = ref.shape[0]
    return jnp.concatenate([_shift_down(ref[0:SUBLANES, :], d), ref[pl.ds(SUBLANES - d, s - SUBLANES), :]], axis=0)


def _conv3_ref(ref, w):
    return w[0] * _ref_shift_down(ref, 2) + w[1] * _ref_shift_down(ref, 1) + w[2] * ref[...]


def _conv3_t(g, w):
    return w[2] * g + w[1] * _shift_up(g, 1) + w[0] * _shift_up(g, 2)


def _conv3_dw(dw_ref, g, u, shifted):
    dw_ref[0:1, :] = jnp.sum(g * shifted[1], axis=0, keepdims=True)
    dw_ref[1:2, :] = jnp.sum(g * shifted[0], axis=0, keepdims=True)
    dw_ref[2:3, :] = jnp.sum(g * u, axis=0, keepdims=True)


_GB, _GC, _CI = 512 // LANES, 1024 // LANES, 1536 // LANES


def _sconv_fwd(proj, w, *, name):
    s = proj.shape[0]

    def body(gb_ref, gc_ref, ci_ref, w_ref, o_ref):
        o_ref[...] = (gb_ref[...] * _conv3(gc_ref[...] * ci_ref[...], _taps(w_ref))[0]).astype(bf16)

    col = lambda off: pl.BlockSpec((s, LANES), lambda j: (0, off + j))
    return pl.pallas_call(
        body, name=name, grid=(CONV_CH // LANES,),
        in_specs=[col(_GB), col(_GC), col(_CI), pl.BlockSpec((3, LANES), lambda j: (0, j))],
        out_specs=pl.BlockSpec((s, LANES), lambda j: (0, j)),
        out_shape=jax.ShapeDtypeStruct((s, CONV_CH), bf16),
        compiler_params=_cparams(("parallel",)))(proj, proj, proj, w)


def _sconv_bwd(dmix, proj, w, *, name):
    s = proj.shape[0]

    def body(do_ref, gb_ref, gc_ref, ci_ref, w_ref, dgb_ref, dgc_ref, dci_ref, dw_ref):
        wv, gc, ci, do = _taps(w_ref), gc_ref[...], ci_ref[...], do_ref[...].astype(f32)
        u = gc * ci
        conv, shifted = _conv3(u, wv)
        dgb_ref[...] = (do * conv).astype(bf16)
        dc = do * gb_ref[...]
        du = _conv3_t(dc, wv)
        dgc_ref[...] = (du * ci).astype(bf16)
        dci_ref[...] = (du * gc).astype(bf16)
        _conv3_dw(dw_ref, dc, u, shifted)

    col = lambda off: pl.BlockSpec((s, LANES), lambda j: (0, off + j))
    out = jax.ShapeDtypeStruct((s, CONV_CH), bf16)
    return pl.pallas_call(
        body, name=name, grid=(CONV_CH // LANES,),
        in_specs=[col(HEADS), col(_GB), col(_GC), col(_CI), pl.BlockSpec((3, LANES), lambda j: (0, j))],
        out_specs=[col(0), col(0), col(0), pl.BlockSpec((3, LANES), lambda j: (0, j))],
        out_shape=[out, out, out, jax.ShapeDtypeStruct((3, CONV_CH), f32)],
        compiler_params=_cparams(("parallel",)))(dmix, proj, proj, proj, w)


def _ffn_act_fwd(zg, zv, cwg, cwv, *, name):
    s, f = zg.shape

    def body(zg_ref, zv_ref, wg_ref, wv_ref, o_ref):
        o_ref[...] = (jax.nn.silu(_conv3_ref(zg_ref, _taps(wg_ref))) * _conv3_ref(zv_ref, _taps(wv_ref))).astype(bf16)

    wide = 2 * LANES
    col = pl.BlockSpec((s, wide), lambda j: (0, j))
    wsp = pl.BlockSpec((3, wide), lambda j: (0, j))
    return pl.pallas_call(
        body, name=name, grid=(f // wide,), in_specs=[col, col, wsp, wsp], out_specs=col,
        out_shape=jax.ShapeDtypeStruct((s, f), bf16), compiler_params=_cparams(("parallel",)))(zg, zv, cwg, cwv)


def _ffn_act_bwd(da, zg, zv, cwg, cwv, *, name):
    s, f = zg.shape

    def body(da_ref, zg_ref, zv_ref, wg_ref, wv_ref, dzg_ref, dzv_ref, dwg_ref, dwv_ref):
        wg, wv, dav = _taps(wg_ref), _taps(wv_ref), da_ref[...].astype(f32)
        ug, uv = _conv3_ref(zg_ref, wg), _conv3_ref(zv_ref, wv)
        sg = jax.nn.sigmoid(ug)
        dug = dav * uv * (sg * (1.0 + ug * (1.0 - sg)))
        duv = dav * (ug * sg)
        dzg_ref[...] = _conv3_t(dug, wg).astype(bf16)
        dzv_ref[...] = _conv3_t(duv, wv).astype(bf16)
        _conv3_dw(dwg_ref, dug, zg_ref[...], (_ref_shift_down(zg_ref, 1), _ref_shift_down(zg_ref, 2)))
        _conv3_dw(dwv_ref, duv, zv_ref[...], (_ref_shift_down(zv_ref, 1), _ref_shift_down(zv_ref, 2)))

    col = pl.BlockSpec((s, LANES), lambda j: (0, j))
    wsp = pl.BlockSpec((3, LANES), lambda j: (0, j))
    act, wsh = jax.ShapeDtypeStruct((s, f), bf16), jax.ShapeDtypeStruct((3, f), f32)
    return pl.pallas_call(
        body, name=name, grid=(f // LANES,), in_specs=[col, col, col, wsp, wsp], out_specs=[col, col, wsp, wsp],
        out_shape=[act, act, wsh, wsh], compiler_params=_cparams(("parallel",)))(da, zg, zv, cwg, cwv)


def _expand_mat():
    return jnp.asarray(np.kron(np.eye(STATE, dtype=np.float32), np.ones((1, GROUP), np.float32)))


def _disc_fn(lr, li, ls, br, bi, e):
    dt = jnp.exp(ls)
    mag = jnp.exp(lr * dt)
    ar, ai = mag * jnp.cos(li * dt), mag * jnp.sin(li * dt)
    nr, ni = ar - 1.0, ai
    den = lr * lr + li * li
    zr, zi = (nr * lr + ni * li) / den, (ni * lr - nr * li) / den
    zrr = jnp.dot(zr, e, precision=lax.Precision.HIGHEST, preferred_element_type=f32)
    zir = jnp.dot(zi, e, precision=lax.Precision.HIGHEST, preferred_element_type=f32)
    return ar, ai, zrr * br - zir * bi, zrr * bi + zir * br


def _disc_fwd(lr, li, ls, br, bi, *, name):
    def body(lr_ref, li_ref, ls_ref, br_ref, bi_ref, e_ref, ar_ref, ai_ref, bbr_ref, bbi_ref):
        ar, ai, bbr, bbi = _disc_fn(lr_ref[...], li_ref[...], ls_ref[...], br_ref[...], bi_ref[...], e_ref[...])
        ar_ref[...], ai_ref[...], bbr_ref[...], bbi_ref[...] = ar, ai, bbr, bbi

    sq, wide = jax.ShapeDtypeStruct((GROUPS, STATE), f32), jax.ShapeDtypeStruct((GROUPS, STATE * GROUP), f32)
    return pl.pallas_call(body, name=name, out_shape=[sq, sq, wide, wide],
                          compiler_params=_cparams())(lr, li, ls, br, bi, _expand_mat())


def _disc_bwd(lr, li, ls, br, bi, dar, dai, dbbr, dbbi, *, name):
    def body(lr_ref, li_ref, ls_ref, br_ref, bi_ref, e_ref, dar_ref, dai_ref, dbbr_ref, dbbi_ref,
             dlr_ref, dli_ref, dls_ref, dbr_ref, dbi_ref):
        ev = e_ref[...]
        _, vjp = jax.vjp(lambda a, b, c, d_, e_: _disc_fn(a, b, c, d_, e_, ev),
                         lr_ref[...], li_ref[...], ls_ref[...], br_ref[...], bi_ref[...])
        dlr, dli, dls, dbr, dbi = vjp((dar_ref[...], dai_ref[...], dbbr_ref[...], dbbi_ref[...]))
        dlr_ref[...], dli_ref[...], dls_ref[...], dbr_ref[...], dbi_ref[...] = dlr, dli, dls, dbr, dbi

    sq, wide = jax.ShapeDtypeStruct((GROUPS, STATE), f32), jax.ShapeDtypeStruct((GROUPS, STATE * GROUP), f32)
    return pl.pallas_call(body, name=name, out_shape=[sq, sq, jax.ShapeDtypeStruct((GROUPS, 1), f32), wide, wide],
                          compiler_params=_cparams())(lr, li, ls, br, bi, _expand_mat(), dar, dai, dbbr, dbbi)


SCAN_TILE = 32
SCAN_PAIRS = 4


def _tile_shift(v, d, reverse):
    if d % 8:
        return _shift_up(v, d) if reverse else _shift_down(v, d)
    z = jnp.zeros((d, v.shape[1]), v.dtype)
    return jnp.concatenate([v[d:], z], axis=0) if reverse else jnp.concatenate([z, v[:v.shape[0] - d]], axis=0)


def _tile_scan(r, i, pows, reverse):
    d = 1
    for br, bi in pows:
        rs, is_ = _tile_shift(r, d, reverse), _tile_shift(i, d, reverse)
        r, i = r + br * rs - bi * is_, i + br * is_ + bi * rs
        d *= 2
    return r, i


def _scan_setup(ar, ai, reverse):
    if reverse:
        ai = -ai
    pows, br, bi, d = [], ar, ai, 1
    while d < SCAN_TILE:
        pows.append((br, bi))
        br, bi, d = br * br - bi * bi, 2.0 * br * bi, 2 * d
    row = lax.broadcasted_iota(jnp.int32, (SCAN_TILE, LANES), 0)
    hit = row == (SCAN_TILE - 1 if reverse else 0)
    pr, pi = _tile_scan(jnp.where(hit, ar, 0.0), jnp.where(hit, ai, 0.0), pows, reverse)
    return pows, pr, pi


def _carry_in(r, i, pr, pi, cr, ci):
    crb, cib = jnp.broadcast_to(cr, r.shape), jnp.broadcast_to(ci, i.shape)
    return r + pr * crb - pi * cib, i + pr * cib + pi * crb


def _pair_cols(q):
    return slice(q * PAIR_LANES, q * PAIR_LANES + LANES), slice(q * PAIR_LANES + LANES, (q + 1) * PAIR_LANES)


_SCAN_W = SCAN_PAIRS * PAIR_LANES


def _scan_specs(s, w):
    per = w.shape[2] // _SCAN_W
    src = pl.BlockSpec((s, LANES), lambda g: (0, g // per))
    mat = pl.BlockSpec((1, LANES, _SCAN_W), lambda g: (g // per, 0, g % per))
    col = pl.BlockSpec((s, _SCAN_W), lambda g: (0, g))
    vec = pl.BlockSpec((SCAN_PAIRS, 1, LANES), lambda g: (g, 0, 0))
    return src, mat, col, vec, (w.shape[0] * per,)


def _scan_fwd(u, wb, ar, ai, *, name):
    s = u.shape[0]
    nt = s // SCAN_TILE

    def body(u_ref, w_ref, ar_ref, ai_ref, x_ref):
        setups = [_scan_setup(ar_ref[q], ai_ref[q], False) for q in range(SCAN_PAIRS)]
        wv = w_ref[0]

        def tile_rows(k):
            return pl.ds(pl.multiple_of(k * SCAN_TILE, SCAN_TILE), SCAN_TILE)

        def tile_in(k):
            return jnp.dot(u_ref[tile_rows(k), :].astype(bf16), wv, preferred_element_type=f32)

        def step(k, carry):
            rows, bu = tile_rows(k), carry[-1]
            ahead = tile_in(jnp.minimum(k + 1, nt - 1))
            out = []
            for q, (pows, pr, pi) in enumerate(setups):
                rc, ic = _pair_cols(q)
                r, i = _tile_scan(bu[:, rc], bu[:, ic], pows, False)
                r, i = _carry_in(r, i, pr, pi, carry[2 * q], carry[2 * q + 1])
                x_ref[rows, rc] = r.astype(bf16)
                x_ref[rows, ic] = i.astype(bf16)
                out += [r[SCAN_TILE - 1:SCAN_TILE, :], i[SCAN_TILE - 1:SCAN_TILE, :]]
            return tuple(out) + (ahead,)

        lax.fori_loop(0, nt, step, tuple(jnp.zeros((1, LANES), f32) for _ in range(2 * SCAN_PAIRS)) + (tile_in(0),))

    src, mat, col, vec, grid = _scan_specs(s, wb)
    return pl.pallas_call(body, name=name, grid=grid, in_specs=[src, mat, vec, vec], out_specs=col,
                          out_shape=jax.ShapeDtypeStruct((s, wb.shape[0] * wb.shape[2]), bf16),
                          compiler_params=_cparams(("parallel",)))(u, wb, ar, ai)


def _scan_bwd(dy, cbt, x, ar, ai, *, name):
    s = dy.shape[0]
    nt = s // SCAN_TILE

    def fold(v):
        out = v[0:8]
        for r in range(8, SCAN_TILE, 8):
            out = out + v[r:r + 8]
        return out

    def body(dy_ref, w_ref, x_ref, ar_ref, ai_ref, g_ref, dar_ref, dai_ref):
        setups = [_scan_setup(ar_ref[q], ai_ref[q], True) for q in range(SCAN_PAIRS)]
        row = lax.broadcasted_iota(jnp.int32, (SCAN_TILE, LANES), 0)
        wv = w_ref[0]

        def tile_in(k):
            return jnp.dot(dy_ref[pl.ds(pl.multiple_of(k * SCAN_TILE, SCAN_TILE), SCAN_TILE), :], wv, preferred_element_type=f32)

        def step(kk, carry):
            k = nt - 1 - kk
            start = pl.multiple_of(k * SCAN_TILE, SCAN_TILE)
            rows = pl.ds(start, SCAN_TILE)
            prev16 = pl.ds(pl.multiple_of(jnp.maximum(start - 16, 0), 16), 16)
            dx = carry[-1]
            ahead = tile_in(jnp.maximum(k - 1, 0))

            def before(cols):
                first = jnp.where(k > 0, x_ref[prev16, cols][15:16, :].astype(f32), 0.0)
                return jnp.where(row == 0, first, pltpu.roll(x_ref[rows, cols].astype(f32), 1, axis=0))

            out = []
            for q, (pows, pr, pi) in enumerate(setups):
                rc, ic = _pair_cols(q)
                cr, ci, acc_r, acc_i = carry[4 * q:4 * q + 4]
                gr, gi = _tile_scan(dx[:, rc], dx[:, ic], pows, True)
                gr, gi = _carry_in(gr, gi, pr, pi, cr, ci)
                g_ref[rows, rc] = gr.astype(bf16)
                g_ref[rows, ic] = gi.astype(bf16)
                xr, xi = before(rc), before(ic)
                out += [gr[0:1, :], gi[0:1, :], acc_r + fold(gr * xr + gi * xi), acc_i + fold(gi * xr - gr * xi)]
            return tuple(out) + (ahead,)

        init = (jnp.zeros((1, LANES), f32), jnp.zeros((1, LANES), f32), jnp.zeros((8, LANES), f32), jnp.zeros((8, LANES), f32))
        res = lax.fori_loop(0, nt, step, init * SCAN_PAIRS + (tile_in(nt - 1),))
        for q in range(SCAN_PAIRS):
            dar_ref[q] = jnp.sum(res[4 * q + 2], axis=0, keepdims=True)
            dai_ref[q] = jnp.sum(res[4 * q + 3], axis=0, keepdims=True)

    src, mat, col, vec, grid = _scan_specs(s, cbt)
    vsh = jax.ShapeDtypeStruct((GROUPS // 2, 1, LANES), f32)
    return pl.pallas_call(body, name=name, grid=grid, in_specs=[src, mat, col, vec, vec],
                          out_specs=[col, vec, vec], out_shape=[jax.ShapeDtypeStruct(x.shape, bf16), vsh, vsh],
                          compiler_params=_cparams(("parallel",)))(dy, cbt, x, ar, ai)


_GELU_C = math.sqrt(2.0 / math.pi)


def _gelu_fwd(y, u, dsk, *, name, ts=ROW_BLOCK):
    s, d = y.shape
    ts = min(ts, s)

    def body(y_ref, u_ref, d_ref, o_ref):
        o_ref[...] = jax.nn.gelu(y_ref[...] + d_ref[...] * u_ref[...]).astype(bf16)

    row, vec = pl.BlockSpec((ts, d), lambda i: (i, 0)), pl.BlockSpec((1, d), lambda i: (0, 0))
    return pl.pallas_call(body, name=name, grid=(s // ts,), in_specs=[row, row, vec], out_specs=row,
                          out_shape=jax.ShapeDtypeStruct((s, d), bf16), compiler_params=_cparams(("parallel",)))(y, u, dsk)


def _gelu_bwd(dg, y, u, dsk, *, name, ts=ROW_BLOCK):
    s, d = y.shape
    ts = min(ts, s)

    def body(dg_ref, y_ref, u_ref, d_ref, dy_ref, du_ref, dd_ref):
        @pl.when(pl.program_id(0) == 0)
        def _():
            dd_ref[...] = jnp.zeros_like(dd_ref)

        uv, dv = u_ref[...], d_ref[...]
        z = y_ref[...] + dv * uv
        th = jnp.tanh(_GELU_C * (z + 0.044715 * z * z * z))
        dz = dg_ref[...] * (0.5 * (1.0 + th) + 0.5 * z * (1.0 - th * th) * _GELU_C * (1.0 + 3 * 0.044715 * z * z))
        dy_ref[...] = dz.astype(bf16)
        du_ref[...] = dz * dv
        dd_ref[...] += jnp.sum(dz * uv, axis=0, keepdims=True)

    row, vec = pl.BlockSpec((ts, d), lambda i: (i, 0)), pl.BlockSpec((1, d), lambda i: (0, 0))
    return pl.pallas_call(
        body, name=name, grid=(s // ts,), in_specs=[row, row, row, vec], out_specs=[row, row, vec],
        out_shape=[jax.ShapeDtypeStruct((s, d), bf16), jax.ShapeDtypeStruct((s, d), f32), jax.ShapeDtypeStruct((1, d), f32)],
        compiler_params=_cparams(("arbitrary",)))(dg, y, u, dsk)


def _glu_fwd(x, a, b, *, name, ts=ROW_BLOCK):
    s, d = x.shape
    ts = min(ts, s)

    def body(x_ref, a_ref, b_ref, o_ref):
        o_ref[...] = x_ref[...] + a_ref[...] * jax.nn.sigmoid(b_ref[...])

    row = pl.BlockSpec((ts, d), lambda i: (i, 0))
    return pl.pallas_call(body, name=name, grid=(s // ts,), in_specs=[row, row, row], out_specs=row,
                          out_shape=jax.ShapeDtypeStruct((s, d), f32), compiler_params=_cparams(("parallel",)))(x, a, b)


def _glu_bwd(dx, a, b, *, name, ts=ROW_BLOCK):
    s, d = dx.shape
    ts = min(ts, s)

    def body(dx_ref, a_ref, b_ref, da_ref, db_ref):
        sg = jax.nn.sigmoid(b_ref[...])
        dxv = dx_ref[...]
        da_ref[...] = (dxv * sg).astype(bf16)
        db_ref[...] = (dxv * a_ref[...] * sg * (1.0 - sg)).astype(bf16)

    row = pl.BlockSpec((ts, d), lambda i: (i, 0))
    out = jax.ShapeDtypeStruct((s, d), bf16)
    return pl.pallas_call(body, name=name, grid=(s // ts,), in_specs=[row, row, row], out_specs=[row, row],
                          out_shape=[out, out], compiler_params=_cparams(("parallel",)))(dx, a, b)


def _loss_head(y, target, *, name, ts=ROW_BLOCK):
    s, d = y.shape
    ts = min(ts, s)

    def body(y_ref, t_ref, dy_ref, dyb_ref, l_ref):
        @pl.when(pl.program_id(0) == 0)
        def _():
            l_ref[...] = jnp.zeros_like(l_ref)

        e = y_ref[...] - t_ref[...]
        dy = e * (1.0 / d)
        dy_ref[...] = dy
        dyb_ref[...] = dy.astype(bf16)
        l_ref[...] += 0.5 * jnp.sum(jnp.mean(e * e, axis=-1, keepdims=True))

    row = pl.BlockSpec((ts, d), lambda i: (i, 0))
    return pl.pallas_call(
        body, name=name, grid=(s // ts,), in_specs=[row, row],
        out_specs=[row, row, pl.BlockSpec((8, LANES), lambda i: (0, 0))],
        out_shape=[jax.ShapeDtypeStruct((s, d), f32), jax.ShapeDtypeStruct((s, d), bf16), jax.ShapeDtypeStruct((8, LANES), f32)],
        compiler_params=_cparams(("arbitrary",)))(y, target)


def _adamw(w, g, m, v, *, name, tr=128):
    r, c = w.shape

    def body(w_ref, g_ref, m_ref, v_ref, d_ref, mo_ref, vo_ref):
        gv = g_ref[...]
        mn = ADAM_B1 * m_ref[...] + (1.0 - ADAM_B1) * gv
        vn = ADAM_B2 * v_ref[...] + (1.0 - ADAM_B2) * (gv * gv)
        m_hat = mn / (1.0 - ADAM_B1 ** ADAM_STEP)
        v_hat = vn / (1.0 - ADAM_B2 ** ADAM_STEP)
        d_ref[...] = -ADAM_LR * (m_hat / (jnp.sqrt(v_hat) + ADAM_EPS) + ADAM_WD * w_ref[...])
        mo_ref[...] = mn
        vo_ref[...] = vn

    row = pl.BlockSpec((tr, c), lambda i: (i, 0))
    out = jax.ShapeDtypeStruct((r, c), f32)
    return pl.pallas_call(body, name=name, grid=(r // tr,), in_specs=[row] * 4, out_specs=[row] * 3,
                          out_shape=[out, out, out], compiler_params=_cparams(("parallel",)))(w, g, m, v)


def _sum_slabs(land, *, name, tr=128):
    n, r, c = land.shape

    def body(l_ref, o_ref):
        acc = l_ref[0].astype(f32)
        for i in range(1, n):
            acc = acc + l_ref[i].astype(f32)
        o_ref[...] = acc

    return pl.pallas_call(body, name=name, grid=(r // tr,), in_specs=[pl.BlockSpec((n, tr, c), lambda i: (0, i, 0))],
                          out_specs=pl.BlockSpec((tr, c), lambda i: (i, 0)), out_shape=jax.ShapeDtypeStruct((r, c), f32),
                          compiler_params=_cparams(("parallel",)))(land)


def _pair_sum(g, theirs, *, name):
    n, r, c = theirs.shape
    tr = _row_tile(r, 1024)

    def body(c_ref, g_ref, t_ref, o_ref):
        o_ref[...] = (g_ref[...].astype(f32) + t_ref[...].astype(f32)).astype(bf16)

    blk = pl.BlockSpec((1, tr, c), lambda j, i, c_ref: (j, i, 0))
    mine = pl.BlockSpec((1, tr, c), lambda j, i, c_ref: (2 * j + c_ref[0], i, 0))
    return pl.pallas_call(
        body, name=name,
        grid_spec=pltpu.PrefetchScalarGridSpec(num_scalar_prefetch=1, grid=(n, r // tr), in_specs=[mine, blk], out_specs=blk),
        out_shape=jax.ShapeDtypeStruct(theirs.shape, bf16),
        compiler_params=_cparams(("parallel", "parallel")))(lax.axis_index("c").astype(jnp.int32).reshape(1), g, theirs)


_MESH = pl.DeviceIdType.MESH
_HBM = pl.BlockSpec(memory_space=pltpu.HBM)
N_CHIP = N_DEV // 2


def _position():
    return lax.axis_index("x"), lax.axis_index("y"), lax.axis_index("c")


def _gather8(x, *, name):
    half = x.shape[0] // 2

    def body(x_ref, o_ref, send_sems, recv_sems, local_sem):
        xx, yy, cc = _position()
        me, sibling = (xx, yy, cc), (xx, yy, 1 - cc)
        here, xn, yn, dg = (xx, yy), (1 - xx, yy), (xx, 1 - yy), (1 - xx, 1 - yy)
        first, second = pl.ds(0, half), pl.ds(half, half)

        def slab(chip, pc, rows=None):
            ref = o_ref.at[4 * chip[0] + 2 * chip[1] + pc]
            return ref if rows is None else ref.at[rows]

        def copy(k, ref, to, src=None):
            return pltpu.make_async_remote_copy(src_ref=ref if src is None else src, dst_ref=ref, send_sem=send_sems.at[k],
                                                recv_sem=recv_sems.at[k], device_id=to, device_id_type=_MESH)

        mine = pltpu.make_async_copy(x_ref, slab(here, cc), local_sem)
        mine.start()
        sends = [copy(0, slab(here, cc), sibling, src=x_ref), copy(1, slab(here, cc), (*xn, cc), src=x_ref),
                 copy(2, slab(here, cc), (*yn, cc), src=x_ref)]
        for cp in sends:
            cp.start()
        copy(1, slab(xn, cc), me).wait_recv()
        sends += [copy(3, slab(xn, cc, first), (*yn, cc)), copy(5, slab(xn, cc), sibling)]
        copy(2, slab(yn, cc), me).wait_recv()
        sends += [copy(4, slab(yn, cc, second), (*xn, cc)), copy(6, slab(yn, cc), sibling)]
        for cp in sends[3:]:
            cp.start()
        copy(3, slab(dg, cc, first), me).wait_recv()
        copy(4, slab(dg, cc, second), me).wait_recv()
        sends.append(copy(7, slab(dg, cc), sibling))
        sends[-1].start()
        for k, chip in ((0, here), (5, xn), (6, yn), (7, dg)):
            copy(k, slab(chip, 1 - cc), me).wait_recv()
        for cp in sends:
            cp.wait_send()
        mine.wait()

    return pl.pallas_call(
        body, name=name, in_specs=[_HBM], out_specs=_HBM, out_shape=jax.ShapeDtypeStruct((N_DEV,) + x.shape, x.dtype),
        scratch_shapes=[pltpu.SemaphoreType.DMA((N_DEV,)), pltpu.SemaphoreType.DMA((N_DEV,)), pltpu.SemaphoreType.DMA],
    )(x)


def _pair_exchange(g, *, name):
    def body(g_ref, land_ref, send_sems, recv_sems):
        xx, yy, cc = _position()
        copies = []
        for j in range(N_CHIP):
            cp = pltpu.make_async_remote_copy(src_ref=g_ref.at[2 * j + 1 - cc], dst_ref=land_ref.at[j], send_sem=send_sems.at[j],
                                              recv_sem=recv_sems.at[j], device_id=(xx, yy, 1 - cc), device_id_type=_MESH)
            cp.start()
            copies.append(cp)
        for cp in copies:
            cp.wait_recv()
        for cp in copies:
            cp.wait_send()

    sems = pltpu.SemaphoreType.DMA((N_CHIP,))
    return pl.pallas_call(body, name=name, in_specs=[_HBM], out_specs=_HBM,
                          out_shape=jax.ShapeDtypeStruct((N_CHIP,) + g.shape[1:], g.dtype), scratch_shapes=[sems, sems])(g)


def _cross_exchange(p, *, name):
    half = p.shape[1] // 2

    def body(p_ref, o_ref, relay_ref, send_sems, recv_sems, local_sem):
        xx, yy, cc = _position()
        me = (xx, yy, cc)
        xn, yn, dg = (1 - xx, yy), (xx, 1 - yy), (1 - xx, 1 - yy)
        idx = lambda chip: 2 * chip[0] + chip[1]
        mine = idx((xx, yy))
        first, second = pl.ds(0, half), pl.ds(half, half)

        def copy(k, src, dst, to):
            return pltpu.make_async_remote_copy(src_ref=src, dst_ref=dst, send_sem=send_sems.at[k], recv_sem=recv_sems.at[k],
                                                device_id=to, device_id_type=_MESH)

        local = pltpu.make_async_copy(p_ref.at[mine], o_ref.at[mine], local_sem)
        local.start()
        sends = [copy(0, p_ref.at[idx(xn)], o_ref.at[mine], (*xn, cc)),
                 copy(1, p_ref.at[idx(dg)].at[first], relay_ref.at[0], (*xn, cc)),
                 copy(2, p_ref.at[idx(yn)], o_ref.at[mine], (*yn, cc)),
                 copy(3, p_ref.at[idx(dg)].at[second], relay_ref.at[1], (*yn, cc))]
        for cp in sends:
            cp.start()
        copy(1, relay_ref.at[0], relay_ref.at[0], me).wait_recv()
        sends.append(copy(4, relay_ref.at[0], o_ref.at[idx(xn)].at[first], (*yn, cc)))
        sends[-1].start()
        copy(3, relay_ref.at[1], relay_ref.at[1], me).wait_recv()
        sends.append(copy(5, relay_ref.at[1], o_ref.at[idx(yn)].at[second], (*xn, cc)))
        sends[-1].start()
        for k, dst in ((0, o_ref.at[idx(xn)]), (2, o_ref.at[idx(yn)]), (4, o_ref.at[idx(dg)].at[first]),
                       (5, o_ref.at[idx(dg)].at[second])):
            copy(k, dst, dst, me).wait_recv()
        for cp in sends:
            cp.wait_send()
        local.wait()

    sems = pltpu.SemaphoreType.DMA((6,))
    relay = jax.ShapeDtypeStruct((2, half) + p.shape[2:], p.dtype)
    return pl.pallas_call(body, name=name, in_specs=[_HBM], out_specs=[_HBM, _HBM],
                          out_shape=[jax.ShapeDtypeStruct(p.shape, p.dtype), relay],
                          scratch_shapes=[sems, sems, pltpu.SemaphoreType.DMA])(p)[0]


def _all_sum(x, *, name):
    return _sum_slabs(_gather8(x, name=f"gather_{name}"), name=f"sum_{name}", tr=min(128, x.shape[0]))


def _pack_slabs(parts, rows, axis=0):
    lead = parts[0].shape[:axis]
    slabs = [p.reshape(lead + (-1, D)) for p in parts]
    used = sum(sl.shape[axis] for sl in slabs)
    return jnp.concatenate(slabs + [jnp.zeros(lead + (rows - used, D), slabs[0].dtype)], axis=axis)


def _unpack_slabs(slab, shapes):
    lead, out, off = slab.shape[:-2], [], 0
    for shp in shapes:
        n = int(np.prod(shp)) // D
        out.append(slab[..., off:off + n, :].reshape(lead + tuple(shp)))
        off += n
    return out


def _pack_rows(parts, rows):
    flat = jnp.concatenate([p.reshape(-1) for p in parts])
    return jnp.pad(flat, (0, rows * D - flat.shape[0])).reshape(rows, D)


def _unpack_rows(slab, shapes):
    flat, out, off = slab.reshape(-1), [], 0
    for shp in shapes:
        n = int(np.prod(shp))
        out.append(flat[off:off + n].reshape(shp))
        off += n
    return out


def _full_shape(shard, axis):
    return tuple(d * N_DEV if i == axis else d for i, d in enumerate(shard))


def _row(v):
    return v.reshape(1, -1).astype(f32)


def _pad_gain(g):
    return jnp.pad(g.astype(f32), (0, HEAD_PAD - QK)).reshape(1, HEAD_PAD)


def _ffn_fwd(x, p, tag):
    h = _rms_fwd(x, p["norm"], name=f"ffn_norm_{tag}")
    zg = _mm(h, p["wgT"], tb=True, tn=FFN_H, name=f"ffn_up_g_{tag}")
    zv = _mm(h, p["wvT"], tb=True, tn=FFN_H, name=f"ffn_up_v_{tag}")
    a = _ffn_act_fwd(zg, zv, p["cwg"], p["cwv"], name=f"ffn_act_{tag}")
    y = _mm(a, p["wd"], add=x, name=f"ffn_down_{tag}")
    return y, (x, h, zg, zv, a)


def _ffn_bwd(dy, dyb, p, saved, tag):
    x, h, zg, zv, a = saved
    g = {}
    da = _mm(dyb, p["wd"], tb=True, out_dtype=bf16, name=f"ffn_down_dx_{tag}")
    g["wd"] = _mm(a, dyb, ta=True, out_dtype=bf16, name=f"ffn_down_dw_{tag}")
    dzg, dzv, g["cwg"], g["cwv"] = _ffn_act_bwd(da, zg, zv, p["cwg"], p["cwv"], name=f"ffn_act_bwd_{tag}")
    g["wgT"] = _mm(dzg, h, ta=True, out_dtype=bf16, name=f"ffn_up_g_dw_{tag}")
    g["wvT"] = _mm(dzv, h, ta=True, out_dtype=bf16, name=f"ffn_up_v_dw_{tag}")
    dh = _mm(dzg, p["wgT"], name=f"ffn_up_g_dx_{tag}")
    dh = _mm(dzv, p["wvT"], add=dh, name=f"ffn_up_v_dx_{tag}")
    dx, dxb, g["norm"] = _rms_bwd(dh, x, p["norm"], res=dy, name=f"ffn_norm_bwd_{tag}")
    return dx, dxb, g


def _mla_fwd(x, p, tabs, tag):
    cos_t, sin_t = tabs
    h = _rms_fwd(x, p["norm"], name=f"attn_norm_{tag}")
    proj = _mm(h, p["w_inT"], tb=True, name=f"mix_in_{tag}")
    cqn = _rms_fwd(proj, p["cq_norm"], col=0, name=f"cq_norm_{tag}")
    ckvn = _rms_fwd(proj, p["ckv_norm"], col=1, name=f"ckv_norm_{tag}")
    q_raw = _mm(cqn, p["w_uqT"], tb=True, name=f"uq_{tag}")
    kv_raw = _mm(ckvn, p["w_ukvT"], tb=True, name=f"ukv_{tag}")
    q, k, v = _qk_prep_fwd(q_raw, kv_raw, proj, p["q_gain"], p["k_gain"], cos_t, sin_t, name=f"qk_prep_{tag}")
    o, lse = _flash_fwd(q, k, v, name=f"flash_fwd_{tag}")
    conv = _sconv_fwd(proj, p["sconv_w"], name=f"sconv_{tag}")
    y = _mm(conv, p["w_out"][HEADS * HEAD_PAD:], add=x, name=f"mix_out_conv_{tag}")
    y = _mm(o, p["w_out"][:HEADS * HEAD_PAD], add=y, name=f"mix_out_{tag}")
    return y, (x, h, proj, cqn, ckvn, q_raw, kv_raw, q, k, v, o, lse, conv)


def _mla_bwd(dy, dyb, p, tabs, saved, tag):
    cos_t, sin_t = tabs
    x, h, proj, cqn, ckvn, q_raw, kv_raw, q, k, v, o, lse, conv = saved
    s = x.shape[0]
    g = {}
    dmix = _mm(dyb, p["w_out"], tb=True, name=f"mix_out_dx_{tag}")
    g["w_out"] = jnp.concatenate([_mm(o, dyb, ta=True, out_dtype=bf16, name=f"mix_out_dw_{tag}"),
                                  _mm(conv, dyb, ta=True, out_dtype=bf16, name=f"mix_out_conv_dw_{tag}")], axis=0)
    dgb, dgc, dci, g["sconv_w"] = _sconv_bwd(dmix, proj, p["sconv_w"], name=f"sconv_bwd_{tag}")
    dq, delta = _flash_bwd_dq(q, k, v, o, dmix, lse, name=f"flash_dq_{tag}")
    dk, dv = _flash_bwd_dkv(q, k, v, dmix, lse.reshape(HEADS, 1, s), delta.reshape(HEADS, 1, s), name=f"flash_dkv_{tag}")
    dq_raw, dkv_raw, dkr, g["q_gain"], g["k_gain"] = _qk_prep_bwd(
        dq, dk, dv, q_raw, kv_raw, proj, p["q_gain"], p["k_gain"], cos_t, sin_t, name=f"qk_prep_bwd_{tag}")
    dcqn = _mm(dq_raw, p["w_uqT"], name=f"uq_dx_{tag}")
    g["w_uqT"] = _mm(dq_raw, cqn, ta=True, out_dtype=bf16, name=f"uq_dw_{tag}")
    dckvn = _mm(dkv_raw, p["w_ukvT"], name=f"ukv_dx_{tag}")
    g["w_ukvT"] = _mm(dkv_raw, ckvn, ta=True, out_dtype=bf16, name=f"ukv_dw_{tag}")
    dcq, g["cq_norm"] = _rms_bwd(dcqn, proj, p["cq_norm"], col=0, out_dtype=bf16, name=f"cq_norm_bwd_{tag}")
    dckv, g["ckv_norm"] = _rms_bwd(dckvn, proj, p["ckv_norm"], col=1, out_dtype=bf16, name=f"ckv_norm_bwd_{tag}")
    dproj = jnp.concatenate([dcq, dckv, dgb, dgc, dci, dkr.astype(bf16)], axis=1)
    dh = _mm(dproj, p["w_inT"], name=f"mix_in_dx_{tag}")
    g["w_inT"] = _mm(dproj, h, ta=True, out_dtype=bf16, name=f"mix_in_dw_{tag}")
    dx, dxb, g["norm"] = _rms_bwd(dh, x, p["norm"], res=dy, name=f"attn_norm_bwd_{tag}")
    return dx, dxb, g


def _block_diag(wg):
    nb, ng, r, c = wg.shape
    eye = jnp.eye(ng, dtype=wg.dtype)
    return (wg[:, :, :, None, :] * eye[None, :, None, :, None]).reshape(nb, ng * r, ng * c)


def _s5_mats(bbr, bbi, c_re, c_im):
    nb = GROUPS // 8
    b4 = jnp.stack([bbr.reshape(GROUPS, STATE, GROUP), bbi.reshape(GROUPS, STATE, GROUP)], axis=1)
    wg = jnp.transpose(b4, (0, 3, 1, 2)).reshape(nb, 8, GROUP, 2 * STATE)
    cg = jnp.stack([c_re, -c_im], axis=1)
    cg = jnp.transpose(cg, (0, 1, 3, 2)).reshape(nb, 8, 2 * STATE, GROUP)
    return _state_layout(_block_diag(wg), 2), _state_layout(_block_diag(cg), 1)


def _state_layout(m, axis):
    shp = m.shape
    m = m.reshape(shp[:axis] + (4, 2, 2, STATE) + shp[axis + 1:])
    return jnp.swapaxes(m, axis + 1, axis + 2).reshape(shp)


def _group_blocks(d):
    d = d.reshape(GROUPS // 2, 2, GROUP, 2, 2, STATE)
    return jnp.stack([d[:, 0, :, :, 0, :], d[:, 1, :, :, 1, :]], axis=1).reshape(GROUPS, GROUP, 2, STATE)


def _s5_fwd(x, p, tag):
    h = _rms_fwd(x, p["norm"], name=f"ssm_norm_{tag}")
    u, ub = _mm(h, p["w_in"], twin=True, name=f"ssm_in_{tag}")
    ar, ai, bbr, bbi = _disc_fwd(p["lr"], p["li"], p["ls"], p["br"], p["bi"], name=f"disc_{tag}")
    wb, cb = _s5_mats(bbr, bbi, p["c_re"], p["c_im"])
    a1, a2 = ar.reshape(GROUPS // 2, 1, LANES), ai.reshape(GROUPS // 2, 1, LANES)
    xs = _scan_fwd(ub, wb.astype(bf16), a1, a2, name=f"ssm_scan_{tag}")
    y = _bd_nn(xs, cb.astype(bf16), name=f"ssm_y_{tag}")
    g = _gelu_fwd(y, u, p["d_skip"], name=f"ssm_gelu_{tag}")
    a = _mm(g, p["wgaT"], tb=True, name=f"glu_a_{tag}")
    b = _mm(g, p["wgbT"], tb=True, name=f"glu_b_{tag}")
    out = _glu_fwd(x, a, b, name=f"glu_{tag}")
    return out, (x, h, u, ub, wb, cb, a1, a2, xs, y, g, a, b)


def _s5_bwd(dout, p, saved, tag):
    x, h, u, ub, wb, cb, a1, a2, xs, y, g, a, b = saved
    gr = {}
    da, db = _glu_bwd(dout, a, b, name=f"glu_bwd_{tag}")
    dg = _mm(da, p["wgaT"], name=f"glu_a_dx_{tag}")
    dg = _mm(db, p["wgbT"], add=dg, name=f"glu_b_dx_{tag}")
    gr["wgaT"] = _mm(da, g, ta=True, out_dtype=bf16, name=f"glu_a_dw_{tag}")
    gr["wgbT"] = _mm(db, g, ta=True, out_dtype=bf16, name=f"glu_b_dw_{tag}")
    dy, du1, gr["d_skip"] = _gelu_bwd(dg, y, u, p["d_skip"], name=f"ssm_gelu_bwd_{tag}")
    dct = _group_blocks(_bd_tn_diag(dy, xs, name=f"ssm_y_dw_{tag}"))
    gs, dar, dai = _scan_bwd(dy, jnp.swapaxes(cb, 1, 2).astype(bf16), xs, a1, a2, name=f"ssm_scan_bwd_{tag}")
    du = _bd_nn(gs, jnp.swapaxes(wb, 1, 2).astype(bf16), add=du1, out_dtype=bf16, name=f"ssm_bu_dx_{tag}")
    dwg = _group_blocks(_bd_tn_diag(ub, gs, name=f"ssm_bu_dw_{tag}"))
    dh = _mm(du, p["w_in"], tb=True, name=f"ssm_in_dx_{tag}")
    gr["w_in"] = _mm(h, du, ta=True, out_dtype=bf16, name=f"ssm_in_dw_{tag}")
    dx, dxb, gr["norm"] = _rms_bwd(dh, x, p["norm"], res=dout, name=f"ssm_norm_bwd_{tag}")
    dbb = jnp.transpose(dwg, (2, 0, 3, 1)).reshape(2, GROUPS, STATE * GROUP)
    gr["c_re"] = dct[:, :, 0, :]
    gr["c_im"] = -dct[:, :, 1, :]
    dlr, dli, dls, dbr, dbi = _disc_bwd(p["lr"], p["li"], p["ls"], p["br"], p["bi"], dar.reshape(GROUPS, STATE),
                                        dai.reshape(GROUPS, STATE), dbb[0], dbb[1], name=f"disc_bwd_{tag}")
    gr["lr"], gr["li"], gr["ls"] = dlr, dli, dls.reshape(GROUPS)
    gr["br"], gr["bi"] = dbr.reshape(GROUPS, STATE, GROUP), dbi.reshape(GROUPS, STATE, GROUP)
    return dx, dxb, gr


def _slab_shape(shard, axis):
    return (shard[0], shard[2], shard[1]) if axis == 2 else shard


def _to_slab(w, axis):
    return jnp.swapaxes(w, 1, 2) if axis == 2 else w


def _mix_in_pad(wt):
    z = lambda n: jnp.zeros((n, wt.shape[1]), wt.dtype)
    return jnp.concatenate([wt[:512], wt[544:2080], z(NOPE), wt[512:544], z(HEAD_PAD - QK)], axis=0)


def _mix_in_unpad(g):
    return jnp.concatenate([g[:512], g[2048 + NOPE:2048 + QK], g[512:2048]], axis=0)


def _mix_out_pad(w):
    att = jnp.pad(w[:512].reshape(HEADS, NOPE, D), ((0, 0), (NOPE, 0), (0, 0))).reshape(HEADS * HEAD_PAD, D)
    return jnp.concatenate([att, w[512:]], axis=0)


def _mix_out_unpad(g):
    att = g[:HEADS * HEAD_PAD].reshape(HEADS, HEAD_PAD, D)[:, NOPE:, :].reshape(HEADS * NOPE, D)
    return jnp.concatenate([att, g[HEADS * HEAD_PAD:]], axis=0)


def _layer_params(wl, ws, layer):
    i = layer // 2
    half = N_DEV // 2
    up = wl["ffn_w_up"][layer]
    ffn = dict(norm=_row(ws["ffn_norm"][layer]), wgT=up[:half].reshape(FFN_H, D), wvT=up[half:].reshape(FFN_H, D),
               cwg=ws["ffn_conv_w"][layer][:, :FFN_H], cwv=ws["ffn_conv_w"][layer][:, FFN_H:],
               wd=wl["ffn_w_down"][layer].reshape(FFN_H, D))
    if layer % 2 == 0:
        uq = jnp.pad(wl["w_uq"][i], ((0, 0), (0, HEAD_PAD - QK), (0, 0)))
        mixer = dict(norm=_row(ws["attn_norm"][i]), w_inT=_mix_in_pad(wl["mix_w_in"][i].reshape(-1, D)),
                     cq_norm=_row(ws["cq_norm"][i]), ckv_norm=_row(ws["ckv_norm"][i]),
                     w_uqT=uq.reshape(HEADS * HEAD_PAD, LORA), w_ukvT=wl["w_ukv"][i].reshape(HEADS * HEAD_PAD, LORA),
                     q_gain=_pad_gain(ws["q_gain"][i]), k_gain=_pad_gain(ws["k_gain"][i]), sconv_w=ws["sconv_w"][i],
                     w_out=_mix_out_pad(wl["mix_w_out"][i].reshape(D, D)))
    else:
        glu = wl["w_glu"][i]
        mixer = dict(norm=_row(ws["ssm_norm"][i]), w_in=wl["ssm_w_in"][i].reshape(D, D), lr=ws["lambda_re"][i],
                     li=ws["lambda_im"][i], ls=ws["log_step"][i].reshape(GROUPS, 1),
                     br=ws["b_re"][i].reshape(GROUPS, STATE * GROUP), bi=ws["b_im"][i].reshape(GROUPS, STATE * GROUP),
                     c_re=ws["c_re"][i], c_im=ws["c_im"][i], d_skip=_row(ws["d_skip"][i]),
                     wgaT=glu[:half].reshape(D, D), wgbT=glu[half:].reshape(D, D))
    return mixer, ffn


def _collect_grads(gm, gf):
    ev, od, half = (0, 2), (1, 3), N_DEV // 2
    st = lambda xs: jnp.stack(xs, axis=0)
    per_dev = list
    halves = lambda a, b, rows: jnp.concatenate([a.reshape(half, rows, D), b.reshape(half, rows, D)], axis=0)
    big = {
        "ffn_w_up": per_dev([halves(gf[l]["wgT"], gf[l]["wvT"], FFN_H // half) for l in range(4)]),
        "ffn_w_down": per_dev([gf[l]["wd"].reshape(N_DEV, -1, D) for l in range(4)]),
        "w_glu": per_dev([halves(gm[l]["wgaT"], gm[l]["wgbT"], D // half) for l in od]),
        "mix_w_out": per_dev([_mix_out_unpad(gm[l]["w_out"]).reshape(N_DEV, -1, D) for l in ev]),
        "ssm_w_in": per_dev([gm[l]["w_in"].reshape(N_DEV, -1, D) for l in od]),
        "w_ukv": per_dev([gm[l]["w_ukvT"].reshape(N_DEV, HEAD_PAD, LORA) for l in ev]),
        "w_uq": per_dev([gm[l]["w_uqT"].reshape(N_DEV, HEAD_PAD, LORA)[:, :QK] for l in ev]),
        "mix_w_in": per_dev([_mix_in_unpad(gm[l]["w_inT"]).reshape(N_DEV, -1, D) for l in ev]),
    }
    small = {
        "attn_norm": st([gm[l]["norm"].reshape(D) for l in ev]),
        "cq_norm": st([gm[l]["cq_norm"].reshape(LORA) for l in ev]),
        "ckv_norm": st([gm[l]["ckv_norm"].reshape(LORA) for l in ev]),
        "q_gain": st([gm[l]["q_gain"].reshape(HEAD_PAD)[:QK] for l in ev]),
        "k_gain": st([gm[l]["k_gain"].reshape(HEAD_PAD)[:QK] for l in ev]),
        "sconv_w": st([gm[l]["sconv_w"] for l in ev]),
        "ssm_norm": st([gm[l]["norm"].reshape(D) for l in od]),
        "lambda_re": st([gm[l]["lr"] for l in od]), "lambda_im": st([gm[l]["li"] for l in od]),
        "log_step": st([gm[l]["ls"] for l in od]),
        "b_re": st([gm[l]["br"] for l in od]), "b_im": st([gm[l]["bi"] for l in od]),
        "c_re": st([gm[l]["c_re"] for l in od]), "c_im": st([gm[l]["c_im"] for l in od]),
        "d_skip": st([gm[l]["d_skip"].reshape(D) for l in od]),
        "ffn_norm": st([gf[l]["norm"].reshape(D) for l in range(4)]),
        "ffn_conv_w": st([jnp.concatenate([gf[l]["cwg"], gf[l]["cwv"]], axis=1) for l in range(4)]),
    }
    return big, small


def _local_step(x, target, wl, ws):
    s = x.shape[0]
    tabs = _rope_tables(s)
    saved, params = [], []
    for layer in range(4):
        mixer, ffn = _layer_params(wl, ws, layer)
        params.append((mixer, ffn))
        if layer % 2 == 0:
            x, sm = _mla_fwd(x, mixer, tabs, f"l{layer}")
        else:
            x, sm = _s5_fwd(x, mixer, f"l{layer}")
        x, sf = _ffn_fwd(x, ffn, f"l{layer}")
        saved.append((sm, sf))
    dx, dxb, loss = _loss_head(x, target, name="loss_head")
    gm, gf = [None] * 4, [None] * 4
    for layer in reversed(range(4)):
        mixer, ffn = params[layer]
        sm, sf = saved[layer]
        dx, dxb, gf[layer] = _ffn_bwd(dx, dxb, ffn, sf, f"l{layer}")
        if layer % 2 == 0:
            dx, dxb, gm[layer] = _mla_bwd(dx, dxb, mixer, tabs, sm, f"l{layer}")
        else:
            dx, dxb, gm[layer] = _s5_bwd(dx, mixer, sm, f"l{layer}")
    return loss, dx, _collect_grads(gm, gf)


def kernel(x, attn_norm, mix_w_in, cq_norm, ckv_norm, w_uq, w_ukv, q_gain, k_gain, sconv_w, mix_w_out, ssm_norm, ssm_w_in, lambda_re, lambda_im, log_step, b_re, b_im, c_re, c_im, d_skip, w_glu, ffn_norm, ffn_w_up, ffn_conv_w, ffn_w_down, loss_target, m_attn_norm, m_mix_w_in, m_cq_norm, m_ckv_norm, m_w_uq, m_w_ukv, m_q_gain, m_k_gain, m_sconv_w, m_mix_w_out, m_ssm_norm, m_ssm_w_in, m_lambda_re, m_lambda_im, m_log_step, m_b_re, m_b_im, m_c_re, m_c_im, m_d_skip, m_w_glu, m_ffn_norm, m_ffn_w_up, m_ffn_conv_w, m_ffn_w_down, v_attn_norm, v_mix_w_in, v_cq_norm, v_ckv_norm, v_w_uq, v_w_ukv, v_q_gain, v_k_gain, v_sconv_w, v_mix_w_out, v_ssm_norm, v_ssm_w_in, v_lambda_re, v_lambda_im, v_log_step, v_b_re, v_b_im, v_c_re, v_c_im, v_d_skip, v_w_glu, v_ffn_norm, v_ffn_w_up, v_ffn_conv_w, v_ffn_w_down):
    args = dict(locals())
    wsh = {n: args[n] for n in WEIGHTS}
    msh = {n: args["m_" + n] for n in WEIGHTS}
    vsh = {n: args["v_" + n] for n in WEIGHTS}
    me = 4 * lax.axis_index("x") + 2 * lax.axis_index("y") + lax.axis_index("c")
    big_names = [n for n, _, _ in BIG]
    slab_shapes = [_slab_shape(sh, ax) for _, sh, ax in BIG]
    small_names = [n for n, _ in REPL] + [n for n, _, _ in SMALL]

    mine = _pack_slabs([_to_slab(wsh[n], ax).astype(bf16) for n, _, ax in BIG], BIG_ROWS)
    gathered, wl, off = _gather8(mine, name="gather_weights"), {}, 0
    for n, (layers, rows, inner) in zip(big_names, slab_shapes):
        per = rows * inner // D
        wl[n] = [gathered[:, off + l * per:off + (l + 1) * per, :].reshape(N_DEV, rows, inner) for l in range(layers)]
        off += layers * per
    placed = []
    for n, shard, axis in SMALL:
        start = [0] * len(shard)
        start[axis] = me * shard[axis]
        placed.append(lax.dynamic_update_slice(jnp.zeros(_full_shape(shard, axis), f32), wsh[n], start))
    small_all = _all_sum(_pack_rows(placed, SMALL_FWD_ROWS), name="small_params")
    ws = dict(zip([n for n, _, _ in SMALL], _unpack_rows(small_all, [_full_shape(sh, ax) for _, sh, ax in SMALL])))
    ws.update({n: wsh[n] for n, _ in REPL})

    loss8, grad_x, (big_grads, grads) = _local_step(x[0], loss_target[0], wl, ws)

    pieces = []
    for n in big_names:
        layers = big_grads[n]
        tiled = (layers[0].shape[1] * layers[0].shape[2] // D) % 16 == 0
        pieces += layers if tiled else [jnp.stack(layers, axis=1)]
    contrib = _pack_slabs(pieces, BIG_ROWS, axis=1)
    chip_sum = _pair_sum(contrib, _pair_exchange(contrib, name="grads_pair_exchange"), name="grads_pair_sum")
    g_big = _sum_slabs(_cross_exchange(chip_sum, name="grads_cross_exchange"), name="grads_chip_sum",
                       tr=_row_tile(BIG_ROWS, 1024))
    small_vec = _pack_rows([grads[n] for n, _ in REPL] + [grads[n] for n, _, _ in SMALL] + [loss8[0, :1]], SMALL_ROWS)
    small_sum = _all_sum(small_vec, name="small_grads")
    parts = _unpack_rows(small_sum, [sh for _, sh in REPL] + [_full_shape(sh, ax) for _, sh, ax in SMALL] + [(1,)])
    g = {n: val for (n, _), val in zip(REPL, parts)}
    for (n, shard, axis), val in zip(SMALL, parts[len(REPL):]):
        start = [0] * len(shard)
        start[axis] = me * shard[axis]
        g[n] = lax.dynamic_slice(val, start, shard)
    loss = parts[-1].reshape(())
    for (n, _, axis), val in zip(BIG, _unpack_slabs(g_big, slab_shapes)):
        g[n] = _to_slab(val, axis)

    delta, new_m, new_v = {}, {}, {}
    for n, shard, _ in BIG:
        flat = lambda a: a.reshape(-1, shard[-1])
        outs = _adamw(flat(wsh[n]), flat(g[n]), flat(msh[n]), flat(vsh[n]), name=f"adamw_{n}",
                      tr=_row_tile(shard[0] * shard[1], 512))
        delta[n], new_m[n], new_v[n] = [o.reshape(shard) for o in outs]
    small_state = [_pack_rows([src[n] for n in small_names], SMALL_ROWS) for src in (wsh, g, msh, vsh)]
    for dst, slab in zip((delta, new_m, new_v), _adamw(*small_state, name="adamw_small")):
        dst.update(zip(small_names, _unpack_rows(slab, [wsh[n].shape for n in small_names])))

    return (loss, grad_x[None], *[g[n] for n in WEIGHTS], *[delta[n] for n in WEIGHTS],
            *[new_m[n] for n in WEIGHTS], *[new_v[n] for n in WEIGHTS])
```
